```python
import jax, jax.numpy as jnp
from jax import lax
import numpy as np

D_MODEL = 1024
BATCH = 8
SEQ = 8192
DEPTH = 1

HEAD_DIM = 64
GRID_W = 64
NA_HEADS = 8
NA_KH = 8
NA_KW = 16
DIL_CONFIGS = ((128, 1), (512, 4), (2048, 16))
DIL_HEADS_PER_GROUP = 4
DIL_HEADS = DIL_HEADS_PER_GROUP * len(DIL_CONFIGS)
DIL_QBLOCK = 64
ROT_DIM = HEAD_DIM // 4
ROPE_THETA = 500000.0
D_FF = -(-8 * D_MODEL // (3 * 256)) * 256
EPS = 1e-6
NEG_INF = -1e30
WA = NA_HEADS * HEAD_DIM
WB = DIL_HEADS * HEAD_DIM
WB_OUT = DIL_HEADS_PER_GROUP * HEAD_DIM
W_IN = 3 * WA + 3 * WB + 2 * D_MODEL

kernel_name = "hybrid_natten_dilated_gated_encoder"

f32 = jnp.float32


def rms_norm(x, g):
    x32 = x.astype(f32)
    y = x32 * lax.rsqrt(jnp.mean(x32 * x32, axis=-1, keepdims=True) + EPS)
    return (y * g.astype(f32)).astype(x.dtype)


def partial_rotary(t, pos):
    half = ROT_DIM // 2
    inv_freq = ROPE_THETA ** (-(jnp.arange(half, dtype=f32) * 2.0) / ROT_DIM)
    ang = pos.astype(f32)[:, None] * inv_freq[None, :]
    cos = jnp.cos(ang)[None, :, None, :]
    sin = jnp.sin(ang)[None, :, None, :]
    x1 = t[..., :half].astype(f32)
    x2 = t[..., half:ROT_DIM].astype(f32)
    rot = jnp.concatenate([x1 * cos - x2 * sin, x2 * cos + x1 * sin], axis=-1).astype(t.dtype)
    return jnp.concatenate([rot, t[..., ROT_DIM:]], axis=-1)


def neighbourhood_attention(q, k, v, rpb):
    b, s, h, dh = q.shape
    rows = s // GRID_W
    kh = min(NA_KH, rows)
    qg = q.reshape(b, rows, GRID_W, h, dh)
    kg = k.reshape(b, rows, GRID_W, h, dh)
    vg = v.reshape(b, rows, GRID_W, h, dh)
    r = jnp.arange(rows)
    row_start = jnp.clip(r - kh // 2, 0, rows - kh)
    row_idx = row_start[:, None] + jnp.arange(kh)[None, :]
    kn = kg[:, row_idx].reshape(b, rows, kh * GRID_W, h, dh)
    vn = vg[:, row_idx].reshape(b, rows, kh * GRID_W, h, dh)
    col = jnp.arange(GRID_W)
    col_start = jnp.clip(col - NA_KW // 2, 0, GRID_W - NA_KW)
    col_mask = (col[None, :] >= col_start[:, None]) & (col[None, :] < col_start[:, None] + NA_KW)
    mask = jnp.broadcast_to(col_mask[:, None, :], (GRID_W, kh, GRID_W)).reshape(GRID_W, kh * GRID_W)
    row_off = row_idx - r[:, None] + (NA_KH - 1)
    col_off = jnp.clip(col[None, :] - col[:, None] + (NA_KW - 1), 0, 2 * NA_KW - 2)
    bias = rpb[:, row_off[:, None, :, None], col_off[None, :, None, :]]
    bias = bias.reshape(h, rows, GRID_W, kh * GRID_W).astype(f32)
    scores = jnp.einsum('brqhd,brkhd->bhrqk', qg, kn).astype(f32) * (dh ** -0.5) + bias[None]
    scores = jnp.where(mask, scores, NEG_INF)
    p = jax.nn.softmax(scores, axis=-1).astype(v.dtype)
    o = jnp.einsum('bhrqk,brkhd->brqhd', p, vn)
    return o.reshape(b, s, h * dh)


def dilated_window_attention(q, k, v, window, dilation):
    b, s, h, dh = q.shape
    half = (window // 2) // dilation
    seg = s // dilation

    def split(t):
        return t.reshape(b, seg, dilation, h, dh).transpose(0, 2, 1, 3, 4).reshape(b * dilation, seg, h, dh)

    qs, ks, vs = split(q), split(k), split(v)
    nb = -(-seg // DIL_QBLOCK)
    lp = nb * DIL_QBLOCK
    qs = jnp.pad(qs, ((0, 0), (0, lp - seg), (0, 0), (0, 0)))
    pad_k = ((0, 0), (half, lp - seg + half), (0, 0), (0, 0))
    kp, vp = jnp.pad(ks, pad_k), jnp.pad(vs, pad_k)
    span = DIL_QBLOCK + 2 * half
    key_idx = jnp.arange(nb)[:, None] * DIL_QBLOCK + jnp.arange(span)[None, :]
    kb = kp[:, key_idx]
    vb = vp[:, key_idx]
    qb = qs.reshape(-1, nb, DIL_QBLOCK, h, dh)
    qpos = jnp.arange(nb)[:, None] * DIL_QBLOCK + jnp.arange(DIL_QBLOCK)[None, :]
    kpos = key_idx - half
    rel = kpos[:, None, :] - qpos[:, :, None]
    mask = (jnp.abs(rel) <= half) & (kpos[:, None, :] >= 0) & (kpos[:, None, :] < seg)
    scores = jnp.einsum('nbqhd,nbkhd->nhbqk', qb, kb).astype(f32) * (dh ** -0.5)
    scores = jnp.where(mask[None, None], scores, NEG_INF)
    lse = jax.nn.logsumexp(scores, axis=-1)
    p = jnp.exp(scores - lse[..., None]).astype(v.dtype)
    o = jnp.einsum('nhbqk,nbkhd->nbqhd', p, vb).reshape(-1, lp, h, dh)[:, :seg]
    lse = lse.transpose(0, 2, 3, 1).reshape(-1, lp, h)[:, :seg]
    o = o.reshape(b, dilation, seg, h, dh).transpose(0, 2, 1, 3, 4).reshape(b, s, h, dh)
    lse = lse.reshape(b, dilation, seg, h).transpose(0, 2, 1, 3).reshape(b, s, h)
    return o, lse


def _fwd_setup_inputs(seed: int = 0) -> dict:
    key = jax.random.key(seed)
    ks = jax.random.split(key, 20)
    nrm = lambda k, shape, scale: jax.random.normal(k, shape, f32) * scale
    d = D_MODEL
    return {
        "x": nrm(ks[0], (BATCH, SEQ, d), 1.0),
        "c": nrm(ks[1], (BATCH, d), 1.0),
        "w_ada": nrm(ks[2], (DEPTH, d, 6 * d), 0.5 * d ** -0.5),
        "b_ada": nrm(ks[3], (DEPTH, 6 * d), 0.02),
        "g_norm1": 1.0 + nrm(ks[4], (DEPTH, d), 0.05),
        "g_norm2": 1.0 + nrm(ks[5], (DEPTH, d), 0.05),
        "w_in": nrm(ks[6], (DEPTH, d, W_IN), d ** -0.5),
        "b_gate": nrm(ks[7], (DEPTH, 2 * d), 0.02),
        "g_qa": 1.0 + nrm(ks[8], (DEPTH, HEAD_DIM), 0.05),
        "g_ka": 1.0 + nrm(ks[9], (DEPTH, HEAD_DIM), 0.05),
        "g_qb": 1.0 + nrm(ks[10], (DEPTH, HEAD_DIM), 0.05),
        "g_kb": 1.0 + nrm(ks[11], (DEPTH, HEAD_DIM), 0.05),
        "rpb": nrm(ks[12], (DEPTH, NA_HEADS, 2 * NA_KH - 1, 2 * NA_KW - 1), 0.1),
        "w_proj_a": nrm(ks[13], (DEPTH, WA, d), WA ** -0.5),
        "w_proj_b": nrm(ks[14], (DEPTH, WB_OUT, d), WB_OUT ** -0.5),
        "w_o": nrm(ks[15], (DEPTH, d, d), d ** -0.5),
        "w_ffn_in": nrm(ks[16], (DEPTH, d, 2 * D_FF), d ** -0.5),
        "w_ffn_out": nrm(ks[17], (DEPTH, D_FF, d), D_FF ** -0.5),
    }


def _fwd_reference(x, c, w_ada, b_ada, g_norm1, g_norm2, w_in, b_gate, g_qa, g_ka, g_qb, g_kb,
              rpb, w_proj_a, w_proj_b, w_o, w_ffn_in, w_ffn_out):
    b, s, _ = x.shape
    pos = jnp.arange(s)
    c_act = jax.nn.silu(c)
    split_at = [WA, 2 * WA, 3 * WA, 3 * WA + WB, 3 * WA + 2 * WB, 3 * WA + 3 * WB]
    for l in range(DEPTH):
        mod = c_act @ w_ada[l] + b_ada[l]
        sh1, sc1, gt1, sh2, sc2, gt2 = [m[:, None, :] for m in jnp.split(mod, 6, axis=-1)]

        h = rms_norm(x, g_norm1[l]) * (1.0 + sc1) + sh1
        proj = h @ w_in[l]
        qa, ka, va, qb, kb, vb, gates = jnp.split(proj, split_at, axis=-1)

        qa = rms_norm(qa.reshape(b, s, NA_HEADS, HEAD_DIM), g_qa[l])
        ka = rms_norm(ka.reshape(b, s, NA_HEADS, HEAD_DIM), g_ka[l])
        va = va.reshape(b, s, NA_HEADS, HEAD_DIM)
        o_a = neighbourhood_attention(qa, ka, va, rpb[l])

        qb = partial_rotary(rms_norm(qb.reshape(b, s, DIL_HEADS, HEAD_DIM), g_qb[l]), pos)
        kb = partial_rotary(rms_norm(kb.reshape(b, s, DIL_HEADS, HEAD_DIM), g_kb[l]), pos)
        vb = vb.reshape(b, s, DIL_HEADS, HEAD_DIM)
        outs, lses = [], []
        for g, (win, dil) in enumerate(DIL_CONFIGS):
            sl = slice(g * DIL_HEADS_PER_GROUP, (g + 1) * DIL_HEADS_PER_GROUP)
            o_g, lse_g = dilated_window_attention(qb[:, :, sl], kb[:, :, sl], vb[:, :, sl], win, dil)
            outs.append(o_g)
            lses.append(lse_g)
        wts = jax.nn.softmax(jnp.stack(lses, axis=0), axis=0)
        o_b = jnp.einsum('gbsh,gbshd->bshd', wts.astype(vb.dtype), jnp.stack(outs, axis=0))
        o_b = o_b.reshape(b, s, WB_OUT)

        gate_a, gate_b = jnp.split(jax.nn.sigmoid(gates + b_gate[l]), 2, axis=-1)
        merged = gate_a * (o_a @ w_proj_a[l]) + gate_b * (o_b @ w_proj_b[l])
        x = x + gt1 * (merged @ w_o[l])

        h2 = rms_norm(x, g_norm2[l]) * (1.0 + sc2) + sh2
        a, up = jnp.split(h2 @ w_ffn_in[l], 2, axis=-1)
        x = x + gt2 * ((jax.nn.silu(a) * up) @ w_ffn_out[l])
    return x


import jax as _jax
import jax.numpy as _jnp

TWIN_FORMAT = 'train_step'
FWD_PARAMS = ['x', 'c', 'w_ada', 'b_ada', 'g_norm1', 'g_norm2', 'w_in', 'b_gate', 'g_qa', 'g_ka', 'g_qb', 'g_kb', 'rpb', 'w_proj_a', 'w_proj_b', 'w_o', 'w_ffn_in', 'w_ffn_out']
TWIN_WEIGHTS = ['w_ada', 'b_ada', 'g_norm1', 'g_norm2', 'w_in', 'b_gate', 'g_qa', 'g_ka', 'g_qb', 'g_kb', 'rpb', 'w_proj_a', 'w_proj_b', 'w_o', 'w_ffn_in', 'w_ffn_out']
TWIN_DIFF_INPUT = 'x'
TWIN_INPUTS = ['x', 'c', 'w_ada', 'b_ada', 'g_norm1', 'g_norm2', 'w_in', 'b_gate', 'g_qa', 'g_ka', 'g_qb', 'g_kb', 'rpb', 'w_proj_a', 'w_proj_b', 'w_o', 'w_ffn_in', 'w_ffn_out', 'loss_target', 'm_w_ada', 'm_b_ada', 'm_g_norm1', 'm_g_norm2', 'm_w_in', 'm_b_gate', 'm_g_qa', 'm_g_ka', 'm_g_qb', 'm_g_kb', 'm_rpb', 'm_w_proj_a', 'm_w_proj_b', 'm_w_o', 'm_w_ffn_in', 'm_w_ffn_out', 'v_w_ada', 'v_b_ada', 'v_g_norm1', 'v_g_norm2', 'v_w_in', 'v_b_gate', 'v_g_qa', 'v_g_ka', 'v_g_qb', 'v_g_kb', 'v_rpb', 'v_w_proj_a', 'v_w_proj_b', 'v_w_o', 'v_w_ffn_in', 'v_w_ffn_out']
TWIN_OUTPUTS = ['loss', 'grad_x', 'grad_w_ada', 'grad_b_ada', 'grad_g_norm1', 'grad_g_norm2', 'grad_w_in', 'grad_b_gate', 'grad_g_qa', 'grad_g_ka', 'grad_g_qb', 'grad_g_kb', 'grad_rpb', 'grad_w_proj_a', 'grad_w_proj_b', 'grad_w_o', 'grad_w_ffn_in', 'grad_w_ffn_out', 'delta_w_ada', 'delta_b_ada', 'delta_g_norm1', 'delta_g_norm2', 'delta_w_in', 'delta_b_gate', 'delta_g_qa', 'delta_g_ka', 'delta_g_qb', 'delta_g_kb', 'delta_rpb', 'delta_w_proj_a', 'delta_w_proj_b', 'delta_w_o', 'delta_w_ffn_in', 'delta_w_ffn_out', 'new_m_w_ada', 'new_m_b_ada', 'new_m_g_norm1', 'new_m_g_norm2', 'new_m_w_in', 'new_m_b_gate', 'new_m_g_qa', 'new_m_g_ka', 'new_m_g_qb', 'new_m_g_kb', 'new_m_rpb', 'new_m_w_proj_a', 'new_m_w_proj_b', 'new_m_w_o', 'new_m_w_ffn_in', 'new_m_w_ffn_out', 'new_v_w_ada', 'new_v_b_ada', 'new_v_g_norm1', 'new_v_g_norm2', 'new_v_w_in', 'new_v_b_gate', 'new_v_g_qa', 'new_v_g_ka', 'new_v_g_qb', 'new_v_g_kb', 'new_v_rpb', 'new_v_w_proj_a', 'new_v_w_proj_b', 'new_v_w_o', 'new_v_w_ffn_in', 'new_v_w_ffn_out']
TWIN_LEAF_KINDS = {'loss': 'loss', 'grad_x': 'grad_x', 'grad_w_ada': 'grad_w', 'grad_b_ada': 'grad_w', 'grad_g_norm1': 'grad_w', 'grad_g_norm2': 'grad_w', 'grad_w_in': 'grad_w', 'grad_b_gate': 'grad_w', 'grad_g_qa': 'grad_w', 'grad_g_ka': 'grad_w', 'grad_g_qb': 'grad_w', 'grad_g_kb': 'grad_w', 'grad_rpb': 'grad_w', 'grad_w_proj_a': 'grad_w', 'grad_w_proj_b': 'grad_w', 'grad_w_o': 'grad_w', 'grad_w_ffn_in': 'grad_w', 'grad_w_ffn_out': 'grad_w', 'delta_w_ada': 'delta_w', 'delta_b_ada': 'delta_w', 'delta_g_norm1': 'delta_w', 'delta_g_norm2': 'delta_w', 'delta_w_in': 'delta_w', 'delta_b_gate': 'delta_w', 'delta_g_qa': 'delta_w', 'delta_g_ka': 'delta_w', 'delta_g_qb': 'delta_w', 'delta_g_kb': 'delta_w', 'delta_rpb': 'delta_w', 'delta_w_proj_a': 'delta_w', 'delta_w_proj_b': 'delta_w', 'delta_w_o': 'delta_w', 'delta_w_ffn_in': 'delta_w', 'delta_w_ffn_out': 'delta_w', 'new_m_w_ada': 'new_m', 'new_m_b_ada': 'new_m', 'new_m_g_norm1': 'new_m', 'new_m_g_norm2': 'new_m', 'new_m_w_in': 'new_m', 'new_m_b_gate': 'new_m', 'new_m_g_qa': 'new_m', 'new_m_g_ka': 'new_m', 'new_m_g_qb': 'new_m', 'new_m_g_kb': 'new_m', 'new_m_rpb': 'new_m', 'new_m_w_proj_a': 'new_m', 'new_m_w_proj_b': 'new_m', 'new_m_w_o': 'new_m', 'new_m_w_ffn_in': 'new_m', 'new_m_w_ffn_out': 'new_m', 'new_v_w_ada': 'new_v', 'new_v_b_ada': 'new_v', 'new_v_g_norm1': 'new_v', 'new_v_g_norm2': 'new_v', 'new_v_w_in': 'new_v', 'new_v_b_gate': 'new_v', 'new_v_g_qa': 'new_v', 'new_v_g_ka': 'new_v', 'new_v_g_qb': 'new_v', 'new_v_g_kb': 'new_v', 'new_v_rpb': 'new_v', 'new_v_w_proj_a': 'new_v', 'new_v_w_proj_b': 'new_v', 'new_v_w_o': 'new_v', 'new_v_w_ffn_in': 'new_v', 'new_v_w_ffn_out': 'new_v'}


def _forward(args):
    return _fwd_reference(*[args[k] for k in FWD_PARAMS])


def _output_shape():
    def fwd():
        inp = _fwd_setup_inputs(0)
        return _fwd_reference(*[inp[k] for k in FWD_PARAMS])
    out = _jax.eval_shape(fwd)
    return out.shape, out.dtype

N_MICROBATCH = 1
ADAM_LR = 0.001
ADAM_B1 = 0.9
ADAM_B2 = 0.999
ADAM_EPS = 1e-08
ADAM_WD = 0.01
ADAM_STEP = 10
PER_EXAMPLE_BATCH_AXIS = {'x': 0, 'c': 0, 'loss_target': 0}
SHARED_INPUTS = []
_WEIGHT_DTYPES = {'w_ada': _jnp.float32, 'b_ada': _jnp.float32, 'g_norm1': _jnp.float32, 'g_norm2': _jnp.float32, 'w_in': _jnp.float32, 'b_gate': _jnp.float32, 'g_qa': _jnp.float32, 'g_ka': _jnp.float32, 'g_qb': _jnp.float32, 'g_kb': _jnp.float32, 'rpb': _jnp.float32, 'w_proj_a': _jnp.float32, 'w_proj_b': _jnp.float32, 'w_o': _jnp.float32, 'w_ffn_in': _jnp.float32, 'w_ffn_out': _jnp.float32}
MOMENT_SCALE = {'w_ada': 1.151020e+00, 'b_ada': 3.158105e+00, 'g_norm1': 4.119962e-02, 'g_norm2': 6.431450e+00, 'w_in': 5.443751e-02, 'b_gate': 5.381151e-02, 'g_qa': 1.944319e-01, 'g_ka': 1.965313e-01, 'g_qb': 1.750317e-01, 'g_kb': 1.748044e-01, 'rpb': 6.203481e-03, 'w_proj_a': 1.024613e-01, 'w_proj_b': 5.192445e-02, 'w_o': 9.415711e-02, 'w_ffn_in': 8.770849e-02, 'w_ffn_out': 1.071574e-01}


def _to_microbatches(a, axis):
    t = _jnp.moveaxis(a, axis, 0)
    t = t.reshape((N_MICROBATCH, t.shape[0] // N_MICROBATCH) + t.shape[1:])
    return _jnp.moveaxis(t, 1, axis + 1)


def setup_inputs(seed: int = 0) -> dict:
    inp = _fwd_setup_inputs(seed)
    key = _jax.random.fold_in(_jax.random.key(seed), 7919)
    shape, _ = _output_shape()
    out = dict(inp)
    out["loss_target"] = _jax.random.normal(_jax.random.fold_in(key, 0), shape, _jnp.float32)
    for i, name in enumerate(TWIN_WEIGHTS):
        w = inp[name].astype(_jnp.float32)
        if MOMENT_SCALE is None:
            s = _jnp.sqrt(_jnp.mean(_jnp.square(w)) + 1e-30)
        else:
            s = MOMENT_SCALE[name]
        km, kv = _jax.random.split(_jax.random.fold_in(key, i + 1))
        out[name] = w
        out["m_" + name] = s * _jax.random.normal(km, w.shape, _jnp.float32)
        out["v_" + name] = (s * s) * _jax.random.uniform(kv, w.shape, _jnp.float32, 0.5, 1.5)
    if N_MICROBATCH > 1:
        for name, axis in PER_EXAMPLE_BATCH_AXIS.items():
            out[name] = _to_microbatches(out[name], axis)
    return {'x': out['x'], 'c': out['c'], 'w_ada': out['w_ada'], 'b_ada': out['b_ada'], 'g_norm1': out['g_norm1'], 'g_norm2': out['g_norm2'], 'w_in': out['w_in'], 'b_gate': out['b_gate'], 'g_qa': out['g_qa'], 'g_ka': out['g_ka'], 'g_qb': out['g_qb'], 'g_kb': out['g_kb'], 'rpb': out['rpb'], 'w_proj_a': out['w_proj_a'], 'w_proj_b': out['w_proj_b'], 'w_o': out['w_o'], 'w_ffn_in': out['w_ffn_in'], 'w_ffn_out': out['w_ffn_out'], 'loss_target': out['loss_target'], 'm_w_ada': out['m_w_ada'], 'm_b_ada': out['m_b_ada'], 'm_g_norm1': out['m_g_norm1'], 'm_g_norm2': out['m_g_norm2'], 'm_w_in': out['m_w_in'], 'm_b_gate': out['m_b_gate'], 'm_g_qa': out['m_g_qa'], 'm_g_ka': out['m_g_ka'], 'm_g_qb': out['m_g_qb'], 'm_g_kb': out['m_g_kb'], 'm_rpb': out['m_rpb'], 'm_w_proj_a': out['m_w_proj_a'], 'm_w_proj_b': out['m_w_proj_b'], 'm_w_o': out['m_w_o'], 'm_w_ffn_in': out['m_w_ffn_in'], 'm_w_ffn_out': out['m_w_ffn_out'], 'v_w_ada': out['v_w_ada'], 'v_b_ada': out['v_b_ada'], 'v_g_norm1': out['v_g_norm1'], 'v_g_norm2': out['v_g_norm2'], 'v_w_in': out['v_w_in'], 'v_b_gate': out['v_b_gate'], 'v_g_qa': out['v_g_qa'], 'v_g_ka': out['v_g_ka'], 'v_g_qb': out['v_g_qb'], 'v_g_kb': out['v_g_kb'], 'v_rpb': out['v_rpb'], 'v_w_proj_a': out['v_w_proj_a'], 'v_w_proj_b': out['v_w_proj_b'], 'v_w_o': out['v_w_o'], 'v_w_ffn_in': out['v_w_ffn_in'], 'v_w_ffn_out': out['v_w_ffn_out']}


def _loss(weights, diff, rest, loss_target):
    with _jax.named_scope("forward"):
        args = {**rest, TWIN_DIFF_INPUT: diff, **{k: w.astype(_WEIGHT_DTYPES[k]) for k, w in weights.items()}}
        y = _forward(args)
    with _jax.named_scope("loss_head"):
        err = _jnp.square(y.astype(_jnp.float32) - loss_target)
        return 0.5 * _jnp.sum(_jnp.mean(err, axis=-1)) if err.ndim else 0.5 * err


def _adamw(w, g, m, v):
    m = ADAM_B1 * m + (1.0 - ADAM_B1) * g
    v = ADAM_B2 * v + (1.0 - ADAM_B2) * _jnp.square(g)
    m_hat = m / (1.0 - ADAM_B1 ** ADAM_STEP)
    v_hat = v / (1.0 - ADAM_B2 ** ADAM_STEP)
    delta = -ADAM_LR * (m_hat / (_jnp.sqrt(v_hat) + ADAM_EPS) + ADAM_WD * w)
    return delta, m, v


def reference(x, c, w_ada, b_ada, g_norm1, g_norm2, w_in, b_gate, g_qa, g_ka, g_qb, g_kb, rpb, w_proj_a, w_proj_b, w_o, w_ffn_in, w_ffn_out, loss_target, m_w_ada, m_b_ada, m_g_norm1, m_g_norm2, m_w_in, m_b_gate, m_g_qa, m_g_ka, m_g_qb, m_g_kb, m_rpb, m_w_proj_a, m_w_proj_b, m_w_o, m_w_ffn_in, m_w_ffn_out, v_w_ada, v_b_ada, v_g_norm1, v_g_norm2, v_w_in, v_b_gate, v_g_qa, v_g_ka, v_g_qb, v_g_kb, v_rpb, v_w_proj_a, v_w_proj_b, v_w_o, v_w_ffn_in, v_w_ffn_out):
    given = dict(x=x, c=c, w_ada=w_ada, b_ada=b_ada, g_norm1=g_norm1, g_norm2=g_norm2, w_in=w_in, b_gate=b_gate, g_qa=g_qa, g_ka=g_ka, g_qb=g_qb, g_kb=g_kb, rpb=rpb, w_proj_a=w_proj_a, w_proj_b=w_proj_b, w_o=w_o, w_ffn_in=w_ffn_in, w_ffn_out=w_ffn_out, loss_target=loss_target, m_w_ada=m_w_ada, m_b_ada=m_b_ada, m_g_norm1=m_g_norm1, m_g_norm2=m_g_norm2, m_w_in=m_w_in, m_b_gate=m_b_gate, m_g_qa=m_g_qa, m_g_ka=m_g_ka, m_g_qb=m_g_qb, m_g_kb=m_g_kb, m_rpb=m_rpb, m_w_proj_a=m_w_proj_a, m_w_proj_b=m_w_proj_b, m_w_o=m_w_o, m_w_ffn_in=m_w_ffn_in, m_w_ffn_out=m_w_ffn_out, v_w_ada=v_w_ada, v_b_ada=v_b_ada, v_g_norm1=v_g_norm1, v_g_norm2=v_g_norm2, v_w_in=v_w_in, v_b_gate=v_b_gate, v_g_qa=v_g_qa, v_g_ka=v_g_ka, v_g_qb=v_g_qb, v_g_kb=v_g_kb, v_rpb=v_rpb, v_w_proj_a=v_w_proj_a, v_w_proj_b=v_w_proj_b, v_w_o=v_w_o, v_w_ffn_in=v_w_ffn_in, v_w_ffn_out=v_w_ffn_out)
    weights = {n: given[n] for n in TWIN_WEIGHTS}
    shared = {n: given[n] for n in SHARED_INPUTS}
    per_example = {n: given[n] for n in ['x', 'c']}
    grad_fn = _jax.value_and_grad(_loss, argnums=(0, 1))

    def one_microbatch(ex, loss_target):
        ex = dict(ex)
        diff = ex.pop(TWIN_DIFF_INPUT)
        return grad_fn(weights, diff, {**shared, **ex}, loss_target)

    if N_MICROBATCH == 1:
        loss, (grad_w, grad_x) = one_microbatch(per_example, given["loss_target"])
    else:
        def body(carry, xs):
            loss_sum, grad_sum = carry
            l_k, (gw_k, gx_k) = one_microbatch(xs[0], xs[1])
            with _jax.named_scope("update"):
                return (loss_sum + l_k, _jax.tree.map(_jnp.add, grad_sum, gw_k)), gx_k

        init = (_jnp.zeros((), _jnp.float32), _jax.tree.map(_jnp.zeros_like, weights))
        (loss, grad_w), grad_x = _jax.lax.scan(body, init, (per_example, given["loss_target"]))
    with _jax.named_scope("update"):
        delta_w, new_m, new_v = {}, {}, {}
        for n in TWIN_WEIGHTS:
            delta_w[n], new_m[n], new_v[n] = _adamw(weights[n], grad_w[n], given["m_" + n], given["v_" + n])
    return (loss, grad_x, *[grad_w[n] for n in TWIN_WEIGHTS], *[delta_w[n] for n in TWIN_WEIGHTS],
            *[new_m[n] for n in TWIN_WEIGHTS], *[new_v[n] for n in TWIN_WEIGHTS])
```

```python
import functools

import numpy as np
import jax
import jax.numpy as jnp
from jax import lax
from jax.experimental import pallas as pl
from jax.experimental.pallas import tpu as pltpu

F32 = jnp.float32
MXU_DTYPE = jnp.bfloat16
WIRE_DTYPE = jnp.bfloat16

HEAD_DIM = 64
GRID_W = 64
NA_HEADS = 8
NA_KH = 8
NA_KW = 16
DIL_CONFIGS = ((128, 1), (512, 4), (2048, 16))
DIL_HEADS_PER_GROUP = 4
DIL_HEADS = DIL_HEADS_PER_GROUP * len(DIL_CONFIGS)
DIL_HALF = 64
ROT_DIM = HEAD_DIM // 4
ROPE_THETA = 500000.0
EPS = 1e-6
NEG_INF = -1e30
WA = NA_HEADS * HEAD_DIM
WB = DIL_HEADS * HEAD_DIM
WB_OUT = DIL_HEADS_PER_GROUP * HEAD_DIM
ADAM_LR = 0.001
ADAM_B1 = 0.9
ADAM_B2 = 0.999
ADAM_EPS = 1e-08
ADAM_WD = 0.01
ADAM_STEP = 10

N_DEV = 8
LANES = 128
VMEM_CAP = 60 * 2**20
MESH = pl.DeviceIdType.MESH
ANY = pl.BlockSpec(memory_space=pl.ANY)


def _vmem(nbytes):
    return int(min(VMEM_CAP, max(16 * 2**20, nbytes * 5 // 4 + 4 * 2**20)))


def _pick(dim, cands):
    for c in cands:
        if c <= dim and dim % c == 0:
            return c
    return dim


def _nbytes(shape, dtype):
    return int(np.prod(shape)) * jnp.dtype(dtype).itemsize


def _dot(a, b, dims=((1,), (0,))):
    return lax.dot_general(a.astype(MXU_DTYPE), b.astype(MXU_DTYPE), (dims, ((), ())), preferred_element_type=F32)


def _dot_nt(a, b):
    return _dot(a, b, ((1,), (1,)))


def _dot_tn(a, b):
    return _dot(a, b, ((0,), (0,)))


def _split3(a):
    hi = a.astype(jnp.bfloat16)
    r1 = a - hi.astype(F32)
    mid = r1.astype(jnp.bfloat16)
    lo = (r1 - mid.astype(F32)).astype(jnp.bfloat16)
    return hi, mid, lo


def _silu(x):
    return x * jax.nn.sigmoid(x)


def _mm(a, b, *, name, ta=False, tb=False, out_dtype=F32, add=None, b_rows=None):
    if ta:
        kdim, m = a.shape
    else:
        m, kdim = a.shape
    off, size = b_rows if b_rows is not None else (0, b.shape[0])
    if tb:
        n = size
        assert b.shape[1] == kdim
    else:
        n = b.shape[1]
        assert size == kdim
    cands = (1024, 768, 512, 384, 256, 128)
    tm = _pick(m, cands)
    if tb:
        tn = _pick(int(np.gcd(off, n)) if off else n, cands)
        tk = _pick(kdim, cands)
    else:
        tn = _pick(n, cands)
        tk = _pick(int(np.gcd(off, kdim)) if off else kdim, cands)
    gm, gn, gk = m // tm, n // tn, kdim // tk

    a_spec = pl.BlockSpec((tk, tm), lambda i, j, k: (k, i)) if ta else pl.BlockSpec((tm, tk), lambda i, j, k: (i, k))
    if tb:
        ob = off // tn
        b_spec = pl.BlockSpec((tn, tk), lambda i, j, k: (j + ob, k))
    else:
        ob = off // tk
        b_spec = pl.BlockSpec((tk, tn), lambda i, j, k: (k + ob, j))
    o_spec = pl.BlockSpec((tm, tn), lambda i, j, k: (i, j))
    in_specs = [a_spec, b_spec]
    operands = [a, b]
    if add is not None:
        in_specs.append(o_spec)
        operands.append(add)
    a_dims = (0,) if ta else (1,)
    b_dims = (1,) if tb else (0,)

    def body(*refs):
        if add is not None:
            a_ref, b_ref, add_ref, o_ref, acc_ref = refs
        else:
            a_ref, b_ref, o_ref, acc_ref = refs
        k = pl.program_id(2)

        @pl.when(k == 0)
        def _():
            acc_ref[...] = jnp.zeros_like(acc_ref)

        acc_ref[...] += _dot(a_ref[...], b_ref[...], (a_dims, b_dims))

        @pl.when(k == gk - 1)
        def _():
            r = acc_ref[...]
            if add is not None:
                r = r + add_ref[...].astype(F32)
            o_ref[...] = r.astype(o_ref.dtype)

    est = 2 * (tm * tk * a.dtype.itemsize + tk * tn * b.dtype.itemsize + tm * tn * jnp.dtype(out_dtype).itemsize)
    est += tm * tn * 4 * (3 if add is not None else 1) + 2 * (tm * tk + tk * tn) * 2
    return pl.pallas_call(
        body,
        name=name,
        grid=(gm, gn, gk),
        in_specs=in_specs,
        out_specs=o_spec,
        out_shape=jax.ShapeDtypeStruct((m, n), out_dtype),
        scratch_shapes=[pltpu.VMEM((tm, tn), F32)],
        compiler_params=pltpu.CompilerParams(
            dimension_semantics=("parallel", "parallel", "arbitrary"), vmem_limit_bytes=_vmem(est)
        ),
    )(*operands)


def _rowmap(fn, rows, vecs, outs, reds, *, name, tm=256):
    rows = [r if isinstance(r, tuple) else (r, r.shape[1], 0) for r in rows]
    t = rows[0][0].shape[0]
    tm = _pick(t, (tm, 128, 64, 32, 16, 8))
    nr, nv, no = len(rows), len(vecs), len(outs)

    def body(*refs):
        row_refs, vec_refs = refs[:nr], refs[nr:nr + nv]
        out_refs, red_refs = refs[nr + nv:nr + nv + no], refs[nr + nv + no:]
        o, rd = fn([r[...] for r in row_refs], [v[...] for v in vec_refs])
        for ref, val in zip(out_refs, o):
            ref[...] = val.astype(ref.dtype)
        if red_refs:
            @pl.when(pl.program_id(0) == 0)
            def _():
                for ref in red_refs:
                    ref[...] = jnp.zeros_like(ref)

            for ref, val in zip(red_refs, rd):
                ref[...] += val

    in_specs = [pl.BlockSpec((tm, w), functools.partial(lambda cb, i: (i, cb), cb)) for (_, w, cb) in rows]
    in_specs += [pl.BlockSpec(v.shape, functools.partial(lambda nd, i: (0,) * nd, v.ndim)) for v in vecs]
    out_specs = [pl.BlockSpec((tm, w), lambda i: (i, 0)) for (w, _) in outs]
    out_specs += [pl.BlockSpec((1, w), lambda i: (0, 0)) for w in reds]
    out_shape = [jax.ShapeDtypeStruct((t, w), d) for (w, d) in outs]
    out_shape += [jax.ShapeDtypeStruct((1, w), F32) for w in reds]
    est = 2 * sum(tm * w * a.dtype.itemsize for (a, w, _) in rows) + 2 * sum(_nbytes(v.shape, v.dtype) for v in vecs)
    est += 2 * sum(tm * w * jnp.dtype(d).itemsize for (w, d) in outs)
    est += 4 * tm * max([w for (_, w, _) in rows] + [w for (w, _) in outs]) * 4
    return pl.pallas_call(
        body,
        name=name,
        grid=(t // tm,),
        in_specs=in_specs,
        out_specs=out_specs,
        out_shape=out_shape,
        compiler_params=pltpu.CompilerParams(dimension_semantics=("arbitrary",), vmem_limit_bytes=_vmem(est)),
    )(*[r[0] for r in rows], *vecs)


def _colsum(v):
    return jnp.sum(v, axis=0, keepdims=True)


def _head_ones():
    i = np.arange(LANES)
    return jnp.asarray((i[:, None] // HEAD_DIM) == (i[None, :] // HEAD_DIM), MXU_DTYPE)


def _headsum(y, ones):
    parts = []
    for j in range(y.shape[1] // LANES):
        c = y[:, j * LANES:(j + 1) * LANES]
        hi = c.astype(MXU_DTYPE)
        lo = c - hi.astype(F32)
        parts.append(_dot(hi, ones) + _dot(lo, ones))
    return parts[0] if len(parts) == 1 else jnp.concatenate(parts, axis=1)


def _rot(y, c, s_lo, s_hi):
    parts = []
    for j in range(y.shape[1] // LANES):
        yc = y[:, j * LANES:(j + 1) * LANES]
        parts.append(yc * c + pltpu.roll(yc, LANES - ROT_DIM // 2, 1) * s_lo + pltpu.roll(yc, ROT_DIM // 2, 1) * s_hi)
    return parts[0] if len(parts) == 1 else jnp.concatenate(parts, axis=1)


def _rot_tables(t):
    half = ROT_DIM // 2
    inv_freq = ROPE_THETA ** (-(jnp.arange(half, dtype=F32) * 2.0) / ROT_DIM)
    ang = jnp.arange(t).astype(F32)[:, None] * inv_freq[None, :]
    cos, sin = jnp.cos(ang), jnp.sin(ang)
    z = lambda w: jnp.zeros((t, w), F32)
    c = jnp.concatenate([cos, cos, jnp.ones((t, HEAD_DIM - ROT_DIM), F32)], axis=1)
    s_lo = jnp.concatenate([-sin, z(HEAD_DIM - half)], axis=1)
    s_hi = jnp.concatenate([z(half), sin, z(HEAD_DIM - ROT_DIM)], axis=1)
    return [jnp.tile(a, (1, LANES // HEAD_DIM)) for a in (c, s_lo, s_hi)]


def _rms(x):
    return lax.rsqrt(jnp.mean(x * x, axis=-1, keepdims=True) + EPS)


def _window(kind, n, bq, t, seg):
    if kind == "na":
        rows = t // GRID_W
        rs = jnp.clip(n - NA_KH // 2, 0, rows - NA_KH)
        return rs
    nk = bq + 2 * DIL_HALF
    return jnp.clip(n * bq - DIL_HALF, 0, t - nk)


def _dil_mask(n, bq, nk, ws, seg):
    qi = n * bq + lax.broadcasted_iota(jnp.int32, (bq, nk), 0)
    ki = ws + lax.broadcasted_iota(jnp.int32, (bq, nk), 1)
    shift = int(np.log2(seg))
    return (jnp.abs(ki - qi) <= DIL_HALF) & ((ki >> shift) == (qi >> shift))


def _lane_pick(blk, e):
    lane = lax.broadcasted_iota(jnp.int32, blk.shape, 1)
    return jnp.max(jnp.where(lane // HEAD_DIM == e, blk, -jnp.inf), axis=1, keepdims=True)


def _attn_fwd(q, k, v, *, kind, name, bias=None, seg=None):
    t, w = q.shape
    pairs = w // LANES
    if kind == "na":
        bq, nk = GRID_W, NA_KH * GRID_W
    else:
        bq = 128
        nk = bq + 2 * DIL_HALF
    nq = t // bq
    scale = HEAD_DIM ** -0.5

    def body(*refs):
        if kind == "na":
            q_ref, k_ref, v_ref, b_ref, o_ref, l_ref = refs
        else:
            q_ref, k_ref, v_ref, o_ref, l_ref = refs
        n = pl.program_id(1)
        if kind == "na":
            rs = _window(kind, n, bq, t, seg)
            ws = pl.multiple_of(rs * GRID_W, GRID_W)
            ro0 = rs - n + (NA_KH - 1)
        else:
            ws = pl.multiple_of(_window(kind, n, bq, t, seg), DIL_HALF)
            mask = _dil_mask(n, bq, nk, ws, seg)
        qv = q_ref[...]
        kv = k_ref[pl.ds(ws, nk), :]
        vv = v_ref[pl.ds(ws, nk), :]
        lane = lax.broadcasted_iota(jnp.int32, (1, LANES), 1)
        o_acc = jnp.zeros((bq, LANES), F32)
        l_acc = jnp.zeros((bq, LANES), F32)
        for e in range(2):
            sel = (lane // HEAD_DIM) == e
            s = _dot_nt(jnp.where(sel, qv, jnp.zeros_like(qv)), kv) * scale
            if kind == "na":
                s = s + jnp.concatenate([b_ref[e, ro0 + 2 * i] for i in range(NA_KH // 2)], axis=1)
            else:
                s = jnp.where(mask, s, NEG_INF)
            m = jnp.max(s, axis=1, keepdims=True)
            p = jnp.exp(s - m)
            l = jnp.sum(p, axis=1, keepdims=True)
            o = _dot(p / l, vv)
            o_acc = jnp.where(sel, o, o_acc)
            l_acc = jnp.where(sel, m + jnp.log(l), l_acc)
        o_ref[...] = o_acc
        l_ref[...] = l_acc

    blk = pl.BlockSpec((bq, LANES), lambda j, n: (n, j))
    res = pl.BlockSpec((t, LANES), lambda j, n: (0, j))
    in_specs = [blk, res, res]
    operands = [q, k, v]
    if kind == "na":
        in_specs.append(pl.BlockSpec((2,) + bias.shape[1:], lambda j, n: (j, 0, 0, 0)))
        operands.append(bias)
    est = 4 * t * LANES * q.dtype.itemsize + 16 * bq * nk * 4 + (2 * _nbytes((2,) + bias.shape[1:], F32) if bias is not None else 0)
    return pl.pallas_call(
        body,
        name=name,
        grid=(pairs, nq),
        in_specs=in_specs,
        out_specs=[blk, blk],
        out_shape=[jax.ShapeDtypeStruct((t, w), F32)] * 2,
        compiler_params=pltpu.CompilerParams(dimension_semantics=("arbitrary", "arbitrary"), vmem_limit_bytes=_vmem(est)),
    )(*operands)


def _attn_bwd(q, k, v, do, dterm, lse, *, kind, name, bias=None, seg=None):
    t, w = q.shape
    pairs = w // LANES
    if kind == "na":
        bq, nk = GRID_W, NA_KH * GRID_W
    else:
        bq = 128
        nk = bq + 2 * DIL_HALF
    nq = t // bq
    scale = HEAD_DIM ** -0.5

    def body(*refs):
        if kind == "na":
            q_ref, k_ref, v_ref, do_ref, dt_ref, l_ref, b_ref, dq_ref, dk_ref, dv_ref, db_ref = refs
        else:
            q_ref, k_ref, v_ref, do_ref, dt_ref, l_ref, dq_ref, dk_ref, dv_ref = refs
        n = pl.program_id(1)

        @pl.when(n == 0)
        def _():
            dk_ref[...] = jnp.zeros_like(dk_ref)
            dv_ref[...] = jnp.zeros_like(dv_ref)
            if kind == "na":
                db_ref[...] = jnp.zeros_like(db_ref)

        if kind == "na":
            rs = _window(kind, n, bq, t, seg)
            ws = pl.multiple_of(rs * GRID_W, GRID_W)
            ro0 = rs - n + (NA_KH - 1)
        else:
            ws = pl.multiple_of(_window(kind, n, bq, t, seg), DIL_HALF)
            mask = _dil_mask(n, bq, nk, ws, seg)
        qv = q_ref[...]
        kv = k_ref[pl.ds(ws, nk), :]
        vv = v_ref[pl.ds(ws, nk), :]
        dov = do_ref[...]
        dtv = dt_ref[...]
        lv = l_ref[...]
        lane = lax.broadcasted_iota(jnp.int32, (1, LANES), 1)
        dq_acc = jnp.zeros((bq, LANES), F32)
        dk_acc = jnp.zeros((nk, LANES), F32)
        dv_acc = jnp.zeros((nk, LANES), F32)
        for e in range(2):
            sel = (lane // HEAD_DIM) == e
            qm = jnp.where(sel, qv, jnp.zeros_like(qv))
            dom = jnp.where(sel, dov, 0.0)
            s = _dot_nt(qm, kv) * scale
            if kind == "na":
                s = s + jnp.concatenate([b_ref[e, ro0 + 2 * i] for i in range(NA_KH // 2)], axis=1)
            else:
                s = jnp.where(mask, s, NEG_INF)
            p = jnp.exp(s - _lane_pick(lv, e))
            dp = _dot_nt(dom, vv)
            ds = p * (dp - _lane_pick(dtv, e))
            if kind == "na":
                for i in range(NA_KH // 2):
                    db_ref[e, ro0 + 2 * i] += ds[:, i * LANES:(i + 1) * LANES]
            dsc = ds * scale
            dq_acc = jnp.where(sel, _dot(dsc, kv), dq_acc)
            dk_acc = dk_acc + _dot_tn(dsc, qm)
            dv_acc = dv_acc + _dot_tn(p, dom)
        dq_ref[...] = dq_acc
        dk_ref[pl.ds(ws, nk), :] += dk_acc
        dv_ref[pl.ds(ws, nk), :] += dv_acc

    blk = pl.BlockSpec((bq, LANES), lambda j, n: (n, j))
    res = pl.BlockSpec((t, LANES), lambda j, n: (0, j))
    in_specs = [blk, res, res, blk, blk, blk]
    operands = [q, k, v, do, dterm, lse]
    out_specs = [blk, res, res]
    out_shape = [jax.ShapeDtypeStruct((t, w), F32)] * 3
    est = 4 * t * LANES * q.dtype.itemsize + 4 * t * LANES * 4 + 24 * bq * nk * 4
    if kind == "na":
        bspec = pl.BlockSpec((2,) + bias.shape[1:], lambda j, n: (j, 0, 0, 0))
        in_specs.append(bspec)
        operands.append(bias)
        out_specs.append(bspec)
        out_shape.append(jax.ShapeDtypeStruct(bias.shape, F32))
        est += 4 * _nbytes((2,) + bias.shape[1:], F32)
    return pl.pallas_call(
        body,
        name=name,
        grid=(pairs, nq),
        in_specs=in_specs,
        out_specs=out_specs,
        out_shape=out_shape,
        compiler_params=pltpu.CompilerParams(dimension_semantics=("arbitrary", "arbitrary"), vmem_limit_bytes=_vmem(est)),
    )(*operands)


def _na_onehot():
    qc = np.arange(GRID_W)[:, None]
    kc = np.arange(GRID_W)[None, :]
    start = np.clip(qc - NA_KW // 2, 0, GRID_W - NA_KW)
    inwin = (kc >= start) & (kc < start + NA_KW)
    off = kc - qc + (NA_KW - 1)
    e_mat = np.zeros((2, 32, GRID_W, 2, GRID_W), np.float32)
    for e in range(2):
        for c in range(2 * NA_KW - 1):
            e_mat[e, c, :, e, :] = (off == c) & inwin
    neg = np.where(inwin, 0.0, NEG_INF).astype(np.float32)
    neg = np.broadcast_to(neg[:, None, :], (GRID_W, 2, GRID_W)).reshape(1, GRID_W * LANES)
    return jnp.asarray(e_mat.reshape(64, GRID_W * LANES), MXU_DTYPE), jnp.asarray(neg)


def _na_rowpairs(rpb):
    p = jnp.pad(rpb, ((0, 0), (0, 0), (0, 1)))
    return jnp.concatenate([p[:, :-1], p[:, 1:]], axis=-1).reshape(NA_HEADS * (2 * NA_KH - 2), 64)


def _na_bias_table(rpb):
    r2 = _na_rowpairs(rpb)
    e_mat, neg = _na_onehot()

    def body(r_ref, e_ref, n_ref, o_ref):
        hi, mid, lo = _split3(r_ref[...])
        e = e_ref[...]
        o_ref[...] = _dot(hi, e) + _dot(mid, e) + _dot(lo, e) + n_ref[...]

    out = pl.pallas_call(
        body,
        name="na_bias_table",
        out_shape=jax.ShapeDtypeStruct((r2.shape[0], GRID_W * LANES), F32),
        compiler_params=pltpu.CompilerParams(vmem_limit_bytes=_vmem(6 * r2.shape[0] * GRID_W * LANES * 4)),
    )(r2, e_mat, neg)
    return out.reshape(NA_HEADS, 2 * NA_KH - 2, GRID_W, LANES)


def _na_bias_grad(dbt):
    e_mat, _ = _na_onehot()
    flat = dbt.reshape(NA_HEADS * (2 * NA_KH - 2), GRID_W * LANES)

    def body(d_ref, e_ref, o_ref):
        hi, mid, lo = _split3(d_ref[...])
        e = e_ref[...]
        o_ref[...] = _dot_nt(hi, e) + _dot_nt(mid, e) + _dot_nt(lo, e)

    g = pl.pallas_call(
        body,
        name="na_bias_grad",
        out_shape=jax.ShapeDtypeStruct((flat.shape[0], 64), F32),
        compiler_params=pltpu.CompilerParams(vmem_limit_bytes=_vmem(6 * flat.shape[0] * GRID_W * LANES * 4)),
    )(flat, e_mat)
    g = g.reshape(NA_HEADS, 2 * NA_KH - 2, 2, 32)[..., :2 * NA_KW - 1]
    first = jnp.pad(g[:, :, 0], ((0, 0), (0, 1), (0, 0)))
    second = jnp.pad(g[:, :, 1], ((0, 0), (1, 0), (0, 0)))
    return first + second


def _all_gather(arrs, *, name):
    na = len(arrs)

    def body(*refs):
        ins, outs = refs[:na], refs[na:2 * na]
        send_sems, recv_sems, local_sems = refs[2 * na:]
        x, y, c = lax.axis_index("x"), lax.axis_index("y"), lax.axis_index("c")
        me, sibling = (x, y, c), (x, y, 1 - c)
        chips = [(1 - x, y), (x, 1 - y), (1 - x, 1 - y)]

        def rows(a, px, py, pc):
            r = ins[a].shape[0]
            return outs[a].at[pl.ds((4 * px + 2 * py + pc) * r, r), :]

        def copy(a, k, block, to, src=None):
            return pltpu.make_async_remote_copy(
                src_ref=rows(a, *block) if src is None else src, dst_ref=rows(a, *block),
                send_sem=send_sems.at[a, k], recv_sem=recv_sems.at[a, k], device_id=to, device_id_type=MESH)

        mine = [pltpu.make_async_copy(ins[a], rows(a, *me), local_sems.at[a]) for a in range(na)]
        for cp in mine:
            cp.start()
        first = []
        for a in range(na):
            first.append(copy(a, 0, me, sibling, src=ins[a]))
            first += [copy(a, 1 + j, me, (*chip, c), src=ins[a]) for j, chip in enumerate(chips)]
        for cp in first:
            cp.start()
        passed = []
        for j, chip in enumerate(chips):
            for a in range(na):
                copy(a, 1 + j, (*chip, c), me).wait_recv()
                cp = copy(a, 4 + j, (*chip, c), sibling)
                cp.start()
                passed.append(cp)
        for a in range(na):
            copy(a, 0, sibling, me).wait_recv()
        for j, chip in enumerate(chips):
            for a in range(na):
                copy(a, 4 + j, (*chip, 1 - c), me).wait_recv()
        for cp in first + passed:
            cp.wait_send()
        for cp in mine:
            cp.wait()

    return pl.pallas_call(
        body,
        name=name,
        in_specs=[ANY] * na,
        out_specs=[ANY] * na,
        out_shape=[jax.ShapeDtypeStruct((N_DEV * a.shape[0], a.shape[1]), a.dtype) for a in arrs],
        scratch_shapes=[pltpu.SemaphoreType.DMA((na, 7)), pltpu.SemaphoreType.DMA((na, 7)), pltpu.SemaphoreType.DMA((na,))],
    )(*arrs)


def _exchange(arrs, *, name):
    na = len(arrs)

    def body(*refs):
        ins, outs = refs[:na], refs[na:2 * na]
        send_sems, recv_sems, local_sems = refs[2 * na:]
        x, y, c = lax.axis_index("x"), lax.axis_index("y"), lax.axis_index("c")
        my = 4 * x + 2 * y + c

        def block(a, idx):
            r = ins[a].shape[0] // N_DEV
            return ins[a].at[pl.ds(idx * r, r), :]

        def copy(a, k):
            px, py, pc = x ^ ((k >> 2) & 1), y ^ ((k >> 1) & 1), c ^ (k & 1)
            peer = 4 * px + 2 * py + pc
            snd = pltpu.make_async_remote_copy(
                src_ref=block(a, peer), dst_ref=outs[a].at[my], send_sem=send_sems.at[a, k - 1],
                recv_sem=recv_sems.at[a, k - 1], device_id=(px, py, pc), device_id_type=MESH)
            rcv = pltpu.make_async_remote_copy(
                src_ref=block(a, peer), dst_ref=outs[a].at[peer], send_sem=send_sems.at[a, k - 1],
                recv_sem=recv_sems.at[a, k - 1], device_id=(px, py, pc), device_id_type=MESH)
            return snd, rcv

        mine = [pltpu.make_async_copy(block(a, my), outs[a].at[my], local_sems.at[a]) for a in range(na)]
        for cp in mine:
            cp.start()
        pairs = [copy(a, k) for k in range(1, N_DEV) for a in range(na)]
        for snd, _ in pairs:
            snd.start()
        for _, rcv in pairs:
            rcv.wait_recv()
        for snd, _ in pairs:
            snd.wait_send()
        for cp in mine:
            cp.wait()

    return pl.pallas_call(
        body,
        name=name,
        in_specs=[ANY] * na,
        out_specs=[ANY] * na,
        out_shape=[jax.ShapeDtypeStruct((N_DEV, a.shape[0] // N_DEV, a.shape[1]), a.dtype) for a in arrs],
        scratch_shapes=[pltpu.SemaphoreType.DMA((na, 7)), pltpu.SemaphoreType.DMA((na, 7)), pltpu.SemaphoreType.DMA((na,))],
    )(*arrs)


def _sum8(a, *, name):
    _, r, w = a.shape
    tr = _pick(r, (256, 128, 64, 32, 16, 8))

    def body(a_ref, o_ref):
        acc = a_ref[0].astype(F32)
        for i in range(1, N_DEV):
            acc = acc + a_ref[i].astype(F32)
        o_ref[...] = acc

    return pl.pallas_call(
        body,
        name=name,
        grid=(r // tr,),
        in_specs=[pl.BlockSpec((N_DEV, tr, w), lambda i: (0, i, 0))],
        out_specs=pl.BlockSpec((tr, w), lambda i: (i, 0)),
        out_shape=jax.ShapeDtypeStruct((r, w), F32),
        compiler_params=pltpu.CompilerParams(dimension_semantics=("parallel",), vmem_limit_bytes=_vmem(4 * N_DEV * tr * w * 4)),
    )(a)


def _adamw(w, g, m, v, *, name):
    def fn(rows, _):
        wv, gv, mv, vv = rows
        m1 = ADAM_B1 * mv + (1.0 - ADAM_B1) * gv
        v1 = ADAM_B2 * vv + (1.0 - ADAM_B2) * jnp.square(gv)
        m_hat = m1 / (1.0 - ADAM_B1 ** ADAM_STEP)
        v_hat = v1 / (1.0 - ADAM_B2 ** ADAM_STEP)
        delta = -ADAM_LR * (m_hat / (jnp.sqrt(v_hat) + ADAM_EPS) + ADAM_WD * wv)
        return [delta, m1, v1], []

    c = w.shape[1]
    return _rowmap(fn, [w, g, m, v], [], [(c, F32)] * 3, [], name=name, tm=128)


_SMALL = ("b_ada", "g_norm1", "g_norm2", "b_gate", "g_qa", "g_ka", "g_qb", "g_kb", "rpb", "loss")


def _pack_small(parts):
    flat = []
    for nme in _SMALL:
        a = parts[nme].reshape(-1).astype(F32)
        flat.append(jnp.pad(a, (0, (-a.shape[0]) % LANES)))
    flat = jnp.concatenate(flat)
    flat = jnp.pad(flat, (0, (-flat.shape[0]) % (8 * LANES)))
    return flat.reshape(-1, LANES)


def _unpack_small(packed, shapes):
    flat = packed.reshape(-1)
    out, pos = {}, 0
    for nme in _SMALL:
        n = int(np.prod(shapes[nme]))
        out[nme] = flat[pos:pos + n].reshape(shapes[nme])
        pos += n + (-n) % LANES
    return out


def _to_class(a, d):
    t, w = a.shape
    return a if d == 1 else a.reshape(t // d, d, w).transpose(1, 0, 2).reshape(t, w)


def _from_class(a, d):
    t, w = a.shape
    return a if d == 1 else a.reshape(d, t // d, w).transpose(1, 0, 2).reshape(t, w)


def kernel(x, c, w_ada, b_ada, g_norm1, g_norm2, w_in, b_gate, g_qa, g_ka, g_qb, g_kb, rpb, w_proj_a, w_proj_b, w_o, w_ffn_in, w_ffn_out, loss_target, m_w_ada, m_b_ada, m_g_norm1, m_g_norm2, m_w_in, m_b_gate, m_g_qa, m_g_ka, m_g_qb, m_g_kb, m_rpb, m_w_proj_a, m_w_proj_b, m_w_o, m_w_ffn_in, m_w_ffn_out, v_w_ada, v_b_ada, v_g_norm1, v_g_norm2, v_w_in, v_b_gate, v_g_qa, v_g_ka, v_g_qb, v_g_kb, v_rpb, v_w_proj_a, v_w_proj_b, v_w_o, v_w_ffn_in, v_w_ffn_out):
    t, d = x.shape[1], x.shape[2]
    d_ff = w_ffn_out.shape[1] * N_DEV
    me = 4 * lax.axis_index("x") + 2 * lax.axis_index("y") + lax.axis_index("c")
    xt, tgt = x[0], loss_target[0]
    ones = _head_ones()

    shards = [w_in[0].T, w_ffn_in[0].T, w_proj_a[0].T, w_proj_b[0].T, w_o[0], w_ffn_out[0]]
    w_in_t, w_ffn_in_t, w_pa_t, w_pb_t, w_o_f, w_ffn_out_f = _all_gather(
        [s.astype(WIRE_DTYPE) for s in shards], name="gather_weights")

    c_all = _all_gather([jnp.pad(c, ((0, 7), (0, 0)))], name="gather_c")[0][::8]
    c_all = jnp.pad(c_all, ((0, LANES - N_DEV), (0, 0)))

    def mod_body(c_ref, w_ref, b_ref, o_ref, act_ref):
        act = _silu(c_ref[...])
        act_ref[...] = act
        hi, mid, lo = _split3(act)
        w = w_ref[...]
        whi, wmid, wlo = _split3(w)
        acc = _dot(hi, whi) + (_dot(hi, wmid) + _dot(mid, whi)) + (_dot(hi, wlo) + _dot(mid, wmid) + _dot(lo, whi))
        o_ref[...] = acc + b_ref[...]

    ncol = w_ada.shape[2]
    b_ada_mine = lax.dynamic_slice(b_ada, (0, me * ncol), (1, ncol))
    mod_part, c_act = pl.pallas_call(
        mod_body,
        name="ada_mod",
        out_shape=[jax.ShapeDtypeStruct((LANES, ncol), F32), jax.ShapeDtypeStruct((LANES, d), F32)],
        compiler_params=pltpu.CompilerParams(vmem_limit_bytes=_vmem(6 * d * ncol * 4)),
    )(c_all, w_ada[0], b_ada_mine)
    mod_all = _all_gather([mod_part[:N_DEV]], name="gather_mod")[0].reshape(N_DEV, N_DEV, ncol)
    mod = lax.dynamic_index_in_dim(mod_all, me, axis=1, keepdims=False).reshape(6, d)
    sh1, sc1, gt1, sh2, sc2, gt2 = [mod[i:i + 1] for i in range(6)]

    def norm_fwd(rows, vecs):
        (xv,), (g, sc, sh) = rows, vecs
        return [xv * _rms(xv) * g * (1.0 + sc) + sh], []

    (h,) = _rowmap(norm_fwd, [xt], [g_norm1, sc1, sh1], [(d, MXU_DTYPE)], [], name="norm1")
    n_a, n_b = 3 * WA, 3 * WB
    qkv_a = _mm(h, w_in_t, tb=True, b_rows=(0, n_a), name="proj_a")
    qkv_b = _mm(h, w_in_t, tb=True, b_rows=(n_a, n_b), name="proj_b")
    gates = _mm(h, w_in_t, tb=True, b_rows=(n_a + n_b, 2 * d), name="proj_gates")

    rot_c, rot_lo, rot_hi = _rot_tables(t)
    tile_g = lambda g, heads: jnp.tile(g, (1, heads))

    def qk_fwd(width, rotate):
        def fn(rows, vecs):
            xv = rows[0]
            gq, gk, on = vecs
            outs = []
            for i, g in enumerate((gq, gk)):
                xi = xv[:, i * width:(i + 1) * width]
                r = lax.rsqrt(_headsum(xi * xi, on) * (1.0 / HEAD_DIM) + EPS)
                yi = xi * r * g
                if rotate:
                    yi = _rot(yi, rows[1], rows[2], rows[3])
                outs.append(yi)
            outs.append(xv[:, 2 * width:])
            return outs, []
        return fn

    qa, ka, va = _rowmap(qk_fwd(WA, False), [qkv_a], [tile_g(g_qa, NA_HEADS), tile_g(g_ka, NA_HEADS), ones],
                         [(WA, MXU_DTYPE)] * 3, [], name="qknorm_a")
    qb, kb, vb = _rowmap(qk_fwd(WB, True), [qkv_b, rot_c, rot_lo, rot_hi],
                         [tile_g(g_qb, DIL_HEADS), tile_g(g_kb, DIL_HEADS), ones], [(WB, MXU_DTYPE)] * 3, [], name="qknorm_b")

    bias_tab = _na_bias_table(rpb[0])
    o_a, lse_a = _attn_fwd(qa, ka, va, kind="na", bias=bias_tab, name="na_fwd")

    grp = []
    for g, (_, dil) in enumerate(DIL_CONFIGS):
        sl = slice(g * WB_OUT, (g + 1) * WB_OUT)
        qg, kg, vg = [_to_class(a[:, sl], dil) for a in (qb, kb, vb)]
        og, lg = _attn_fwd(qg, kg, vg, kind="dil", seg=t // dil, name=f"dil_fwd{g}")
        grp.append(dict(q=qg, k=kg, v=vg, o=_from_class(og, dil), lse=_from_class(lg, dil), lse_c=lg, dil=dil))

    def merge_fwd(rows, _):
        o0, o1, o2, l0, l1, l2 = rows
        mx = jnp.maximum(jnp.maximum(l0, l1), l2)
        e0, e1, e2 = jnp.exp(l0 - mx), jnp.exp(l1 - mx), jnp.exp(l2 - mx)
        s = e0 + e1 + e2
        return [(e0 / s) * o0 + (e1 / s) * o1 + (e2 / s) * o2], []

    (o_b,) = _rowmap(merge_fwd, [gr["o"] for gr in grp] + [gr["lse"] for gr in grp], [], [(WB_OUT, F32)], [], name="dil_merge")

    pa = _mm(o_a, w_pa_t, tb=True, name="proj_out_a")
    pb = _mm(o_b, w_pb_t, tb=True, name="proj_out_b")

    def gate_fwd(rows, vecs):
        gv, pav, pbv = rows
        sg = jax.nn.sigmoid(gv + vecs[0])
        return [sg[:, :d] * pav + sg[:, d:] * pbv], []

    (merged,) = _rowmap(gate_fwd, [gates, pa, pb], [b_gate], [(d, MXU_DTYPE)], [], name="gate_merge")
    att = _mm(merged, w_o_f, name="proj_o")

    def resid_norm(rows, vecs):
        xv, av = rows
        gt, g, sc, sh = vecs
        x1v = xv + gt * av
        return [x1v, x1v * _rms(x1v) * g * (1.0 + sc) + sh], []

    x1, h2 = _rowmap(resid_norm, [xt, att], [gt1, g_norm2, sc2, sh2], [(d, F32), (d, MXU_DTYPE)], [], name="resid_norm2")

    u = _mm(h2, w_ffn_in_t, tb=True, name="ffn_in")

    def swiglu_fwd(rows, _):
        uv = rows[0]
        return [_silu(uv[:, :d_ff]) * uv[:, d_ff:]], []

    (f,) = _rowmap(swiglu_fwd, [u], [], [(d_ff, MXU_DTYPE)], [], name="swiglu")
    y2 = _mm(f, w_ffn_out_f, name="ffn_out")

    def loss_fn(rows, vecs):
        yv, x1v, tv = rows
        gt = vecs[0]
        err = x1v + gt * yv - tv
        dout = err * (1.0 / d)
        return [dout, dout * gt], [_colsum(err * err), _colsum(dout * yv)]

    dout, dy2, err2, dgt2 = _rowmap(loss_fn, [y2, x1, tgt], [gt2], [(d, F32), (d, MXU_DTYPE)], [d, d], name="loss")

    dw_ffn_out = _mm(f, dy2, ta=True, out_dtype=WIRE_DTYPE, name="wgrad_ffn_out")
    df = _mm(dy2, w_ffn_out_f, tb=True, name="dgrad_ffn_out")

    def swiglu_bwd(rows, _):
        dfv, uv = rows
        a, up = uv[:, :d_ff], uv[:, d_ff:]
        sg = jax.nn.sigmoid(a)
        da = dfv * up * (sg * (1.0 + a * (1.0 - sg)))
        return [jnp.concatenate([da, dfv * (a * sg)], axis=1)], []

    (du,) = _rowmap(swiglu_bwd, [df, u], [], [(2 * d_ff, MXU_DTYPE)], [], name="swiglu_bwd", tm=128)
    dw_ffn_in_t = _mm(du, h2, ta=True, out_dtype=WIRE_DTYPE, name="wgrad_ffn_in")
    dh2 = _mm(du, w_ffn_in_t, name="dgrad_ffn_in")

    def norm_bwd(dh, xv, g, sc):
        r = _rms(xv)
        xh = xv * r
        dxh = dh * g * (1.0 + sc)
        dxv = r * (dxh - xh * jnp.mean(dxh * xh, axis=-1, keepdims=True))
        return dxv, [_colsum(dh), _colsum(dh * xh * g), _colsum(dh * xh * (1.0 + sc))]

    def norm2_bwd(rows, vecs):
        dhv, x1v, dov, av = rows
        g, sc, gt = vecs
        dxv, sums = norm_bwd(dhv, x1v, g, sc)
        dx1v = dov + dxv
        return [dx1v, dx1v * gt], sums + [_colsum(dx1v * av)]

    dx1, datt, dsh2, dsc2, dg2, dgt1 = _rowmap(norm2_bwd, [dh2, x1, dout, att], [g_norm2, sc2, gt1],
                                                [(d, F32), (d, MXU_DTYPE)], [d] * 4, name="norm2_bwd")
    dw_o = _mm(merged, datt, ta=True, out_dtype=WIRE_DTYPE, name="wgrad_o")
    dmerged = _mm(datt, w_o_f, tb=True, name="dgrad_o")

    def gate_bwd(rows, vecs):
        dm, gv, pav, pbv = rows
        sg = jax.nn.sigmoid(gv + vecs[0])
        ga, gb = sg[:, :d], sg[:, d:]
        dgp = jnp.concatenate([dm * pav * ga * (1.0 - ga), dm * pbv * gb * (1.0 - gb)], axis=1)
        return [dm * ga, dm * gb, dgp], [_colsum(dgp)]

    dpa, dpb, dgates, db_gate = _rowmap(gate_bwd, [dmerged, gates, pa, pb], [b_gate],
                                        [(d, MXU_DTYPE), (d, MXU_DTYPE), (2 * d, MXU_DTYPE)], [2 * d], name="gate_bwd")
    dw_pa_t = _mm(dpa, o_a, ta=True, out_dtype=WIRE_DTYPE, name="wgrad_proj_a")
    dw_pb_t = _mm(dpb, o_b, ta=True, out_dtype=WIRE_DTYPE, name="wgrad_proj_b")
    do_a = _mm(dpa, w_pa_t, name="dgrad_proj_a")
    do_b = _mm(dpb, w_pb_t, name="dgrad_proj_b")

    def delta_a(rows, vecs):
        return [_headsum(rows[0] * rows[1], vecs[0])], []

    (dterm_a,) = _rowmap(delta_a, [do_a, o_a], [ones], [(WA, F32)], [], name="na_delta")
    dqa, dka, dva, dbias = _attn_bwd(qa, ka, va, do_a, dterm_a, lse_a, kind="na", bias=bias_tab, name="na_bwd")
    g_rpb = _na_bias_grad(dbias)

    def merge_bwd(rows, vecs):
        dob, o0, o1, o2, l0, l1, l2 = rows
        on = vecs[0]
        mx = jnp.maximum(jnp.maximum(l0, l1), l2)
        e0, e1, e2 = jnp.exp(l0 - mx), jnp.exp(l1 - mx), jnp.exp(l2 - mx)
        s = e0 + e1 + e2
        ws = [e0 / s, e1 / s, e2 / s]
        dws = [_headsum(dob * o, on) for o in (o0, o1, o2)]
        mean = ws[0] * dws[0] + ws[1] * dws[1] + ws[2] * dws[2]
        return [w * dob for w in ws] + [w * mean for w in ws], []

    mb = _rowmap(merge_bwd, [do_b] + [gr["o"] for gr in grp] + [gr["lse"] for gr in grp], [ones],
                 [(WB_OUT, F32)] * 6, [], name="dil_merge_bwd")
    dqb, dkb, dvb = [], [], []
    for g, gr in enumerate(grp):
        dil = gr["dil"]
        dq, dk, dv = _attn_bwd(gr["q"], gr["k"], gr["v"], _to_class(mb[g], dil), _to_class(mb[3 + g], dil), gr["lse_c"],
                               kind="dil", seg=t // dil, name=f"dil_bwd{g}")
        dqb.append(_from_class(dq, dil))
        dkb.append(_from_class(dk, dil))
        dvb.append(_from_class(dv, dil))

    def qk_bwd(width, rotate, ngrp):
        def fn(rows, vecs):
            gq, gk, on = vecs
            xv = rows[0]
            pos = 1
            if rotate:
                rc, rlo, rhi = rows[1:4]
                pos = 4
            cat = lambda parts: parts[0] if len(parts) == 1 else jnp.concatenate(parts, axis=1)
            dq, dk, dv = [cat(rows[pos + i * ngrp:pos + (i + 1) * ngrp]) for i in range(3)]
            outs, sums = [], []
            for i, (dy, g) in enumerate(((dq, gq), (dk, gk))):
                if rotate:
                    dy = _rot(dy, rc, -rlo, -rhi)
                xi = xv[:, i * width:(i + 1) * width]
                r = lax.rsqrt(_headsum(xi * xi, on) * (1.0 / HEAD_DIM) + EPS)
                xh = xi * r
                dxh = dy * g
                outs.append(r * (dxh - xh * (_headsum(dxh * xh, on) * (1.0 / HEAD_DIM))))
                sums.append(_colsum(dy * xh))
            return [jnp.concatenate(outs + [dv], axis=1)], sums
        return fn

    dqkv_a, dg_qa, dg_ka = _rowmap(qk_bwd(WA, False, 1), [qkv_a, dqa, dka, dva],
                                   [tile_g(g_qa, NA_HEADS), tile_g(g_ka, NA_HEADS), ones],
                                   [(3 * WA, MXU_DTYPE)], [WA, WA], name="qknorm_a_bwd", tm=128)
    dqkv_b, dg_qb, dg_kb = _rowmap(qk_bwd(WB, True, 3), [qkv_b, rot_c, rot_lo, rot_hi] + dqb + dkb + dvb,
                                   [tile_g(g_qb, DIL_HEADS), tile_g(g_kb, DIL_HEADS), ones],
                                   [(3 * WB, MXU_DTYPE)], [WB, WB], name="qknorm_b_bwd", tm=128)

    dw_in_t = jnp.concatenate([
        _mm(dqkv_a, h, ta=True, out_dtype=WIRE_DTYPE, name="wgrad_in_a"),
        _mm(dqkv_b, h, ta=True, out_dtype=WIRE_DTYPE, name="wgrad_in_b"),
        _mm(dgates, h, ta=True, out_dtype=WIRE_DTYPE, name="wgrad_in_gates")], axis=0)
    dh = _mm(dqkv_a, w_in_t, b_rows=(0, n_a), name="dgrad_in_a")
    dh = _mm(dqkv_b, w_in_t, b_rows=(n_a, n_b), add=dh, name="dgrad_in_b")
    dh = _mm(dgates, w_in_t, b_rows=(n_a + n_b, 2 * d), add=dh, name="dgrad_in_gates")

    def norm1_bwd(rows, vecs):
        dhv, xv, dx1v = rows
        dxv, sums = norm_bwd(dhv, xv, vecs[0], vecs[1])
        return [dx1v + dxv], sums

    grad_x, dsh1, dsc1, dg1 = _rowmap(norm1_bwd, [dh, xt, dx1], [g_norm1, sc1], [(d, F32)], [d] * 3, name="norm1_bwd")

    heads_sum = lambda a, heads: a.reshape(heads, HEAD_DIM).sum(axis=0)
    dmod = jnp.concatenate([dsh1, dsc1, dgt1, dsh2, dsc2, dgt2], axis=1)
    local_small = _pack_small(dict(
        b_ada=dmod, g_norm1=dg1, g_norm2=dg2, b_gate=db_gate, g_qa=heads_sum(dg_qa, NA_HEADS),
        g_ka=heads_sum(dg_ka, NA_HEADS), g_qb=heads_sum(dg_qb, DIL_HEADS), g_kb=heads_sum(dg_kb, DIL_HEADS),
        rpb=g_rpb, loss=(0.5 / d) * jnp.sum(err2)))
    srows = local_small.shape[0]
    small_all = _all_gather([local_small], name="gather_small")[0].reshape(N_DEV, srows, LANES)
    small_sum = _sum8(small_all, name="sum_small")
    small_shapes = dict(b_ada=b_ada.shape, g_norm1=g_norm1.shape, g_norm2=g_norm2.shape, b_gate=b_gate.shape,
                        g_qa=g_qa.shape, g_ka=g_ka.shape, g_qb=g_qb.shape, g_kb=g_kb.shape, rpb=rpb.shape, loss=())
    small_w = dict(b_ada=b_ada, g_norm1=g_norm1, g_norm2=g_norm2, b_gate=b_gate, g_qa=g_qa, g_ka=g_ka, g_qb=g_qb,
                   g_kb=g_kb, rpb=rpb, loss=jnp.zeros((), F32))
    small_m = dict(b_ada=m_b_ada, g_norm1=m_g_norm1, g_norm2=m_g_norm2, b_gate=m_b_gate, g_qa=m_g_qa, g_ka=m_g_ka,
                   g_qb=m_g_qb, g_kb=m_g_kb, rpb=m_rpb, loss=jnp.zeros((), F32))
    small_v = dict(b_ada=v_b_ada, g_norm1=v_g_norm1, g_norm2=v_g_norm2, b_gate=v_b_gate, g_qa=v_g_qa, g_ka=v_g_ka,
                   g_qb=v_g_qb, g_kb=v_g_kb, rpb=v_rpb, loss=jnp.zeros((), F32))
    s_delta, s_m, s_v = _adamw(_pack_small(small_w), small_sum, _pack_small(small_m), _pack_small(small_v), name="adamw_small")
    gs = _unpack_small(small_sum, small_shapes)
    ds_, ms_, vs_ = [_unpack_small(a, small_shapes) for a in (s_delta, s_m, s_v)]

    dmod_all = small_all[:, :6 * d // LANES].reshape(N_DEV, 6 * d)
    dmod_mine = jnp.pad(lax.dynamic_slice(dmod_all, (0, me * ncol), (N_DEV, ncol)), ((0, LANES - N_DEV), (0, 0)))

    def wada_body(c_ref, dm_ref, o_ref):
        chi, cmid, clo = _split3(c_ref[...])
        dhi, dmid, dlo = _split3(dm_ref[...])
        o_ref[...] = (_dot_tn(chi, dhi) + (_dot_tn(chi, dmid) + _dot_tn(cmid, dhi))
                      + (_dot_tn(chi, dlo) + _dot_tn(cmid, dmid) + _dot_tn(clo, dhi)))

    g_w_ada = pl.pallas_call(
        wada_body,
        name="wgrad_ada",
        out_shape=jax.ShapeDtypeStruct((d, ncol), F32),
        compiler_params=pltpu.CompilerParams(vmem_limit_bytes=_vmem(4 * d * ncol * 4)),
    )(c_act, dmod_mine)

    recv = _exchange([dw_in_t, dw_ffn_in_t, dw_pa_t, dw_pb_t, dw_o, dw_ffn_out], name="exchange_grads")
    names = ("w_in", "w_ffn_in", "w_proj_a", "w_proj_b", "w_o", "w_ffn_out")
    transposed = (True, True, True, True, False, False)
    big_g = {}
    for nme, r, tr in zip(names, recv, transposed):
        s = _sum8(r, name=f"sum_{nme}")
        big_g[nme] = s.T if tr else s
    big_g["w_ada"] = g_w_ada
    big_w = dict(w_ada=w_ada, w_in=w_in, w_proj_a=w_proj_a, w_proj_b=w_proj_b, w_o=w_o, w_ffn_in=w_ffn_in, w_ffn_out=w_ffn_out)
    big_m = dict(w_ada=m_w_ada, w_in=m_w_in, w_proj_a=m_w_proj_a, w_proj_b=m_w_proj_b, w_o=m_w_o, w_ffn_in=m_w_ffn_in, w_ffn_out=m_w_ffn_out)
    big_v = dict(w_ada=v_w_ada, w_in=v_w_in, w_proj_a=v_w_proj_a, w_proj_b=v_w_proj_b, w_o=v_w_o, w_ffn_in=v_w_ffn_in, w_ffn_out=v_w_ffn_out)
    grads, deltas, new_m, new_v = {}, {}, {}, {}
    for nme in big_w:
        dl, m1, v1 = _adamw(big_w[nme][0], big_g[nme], big_m[nme][0], big_v[nme][0], name=f"adamw_{nme}")
        grads[nme], deltas[nme], new_m[nme], new_v[nme] = big_g[nme][None], dl[None], m1[None], v1[None]
    for nme in _SMALL[:-1]:
        grads[nme], deltas[nme], new_m[nme], new_v[nme] = gs[nme], ds_[nme], ms_[nme], vs_[nme]

    order = ("w_ada", "b_ada", "g_norm1", "g_norm2", "w_in", "b_gate", "g_qa", "g_ka", "g_qb", "g_kb", "rpb",
             "w_proj_a", "w_proj_b", "w_o", "w_ffn_in", "w_ffn_out")
    return (gs["loss"], grad_x[None], *[grads[n] for n in order], *[deltas[n] for n in order],
            *[new_m[n] for n in order], *[new_v[n] for n in order])
```

```python
import functools

import numpy as np
import jax
import jax.numpy as jnp
from jax import lax
from jax.experimental import pallas as pl
from jax.experimental.pallas import tpu as pltpu

F32 = jnp.float32
MXU_DTYPE = jnp.bfloat16
WIRE_DTYPE = jnp.bfloat16

HEAD_DIM = 64
GRID_W = 64
NA_HEADS = 8
NA_KH = 8
NA_KW = 16
DIL_CONFIGS = ((128, 1), (512, 4), (2048, 16))
DIL_HEADS_PER_GROUP = 4
DIL_HEADS = DIL_HEADS_PER_GROUP * len(DIL_CONFIGS)
DIL_HALF = 64
ROT_DIM = HEAD_DIM // 4
ROPE_THETA = 500000.0
EPS = 1e-6
NEG_INF = -1e30
WA = NA_HEADS * HEAD_DIM
WB = DIL_HEADS * HEAD_DIM
WB_OUT = DIL_HEADS_PER_GROUP * HEAD_DIM
ADAM_LR = 0.001
ADAM_B1 = 0.9
ADAM_B2 = 0.999
ADAM_EPS = 1e-08
ADAM_WD = 0.01
ADAM_STEP = 10

N_DEV = 8
LANES = 128
VMEM_CAP = 60 * 2**20
MESH = pl.DeviceIdType.MESH
ANY = pl.BlockSpec(memory_space=pl.ANY)


def _vmem(nbytes):
    return int(min(VMEM_CAP, max(16 * 2**20, nbytes * 5 // 4 + 4 * 2**20)))


def _pick(dim, cands):
    for c in cands:
        if c <= dim and dim % c == 0:
            return c
    return dim


def _nbytes(shape, dtype):
    return int(np.prod(shape)) * jnp.dtype(dtype).itemsize


def _dot(a, b, dims=((1,), (0,))):
    return lax.dot_general(a.astype(MXU_DTYPE), b.astype(MXU_DTYPE), (dims, ((), ())), preferred_element_type=F32)


def _dot_nt(a, b):
    return _dot(a, b, ((1,), (1,)))


def _dot_tn(a, b):
    return _dot(a, b, ((0,), (0,)))


def _split3(a):
    hi = a.astype(jnp.bfloat16)
    r1 = a - hi.astype(F32)
    mid = r1.astype(jnp.bfloat16)
    lo = (r1 - mid.astype(F32)).astype(jnp.bfloat16)
    return hi, mid, lo


def _silu(x):
    return x * jax.nn.sigmoid(x)


def _mm(a, b, *, name, ta=False, tb=False, out_dtype=F32, add=None, b_rows=None):
    if ta:
        kdim, m = a.shape
    else:
        m, kdim = a.shape
    off, size = b_rows if b_rows is not None else (0, b.shape[0])
    if tb:
        n = size
        assert b.shape[1] == kdim
    else:
        n = b.shape[1]
        assert size == kdim
    cands = (1024, 768, 512, 384, 256, 128)
    tm = _pick(m, cands)
    if tb:
        tn = _pick(int(np.gcd(off, n)) if off else n, cands)
        tk = _pick(kdim, cands)
    else:
        tn = _pick(n, cands)
        tk = _pick(int(np.gcd(off, kdim)) if off else kdim, cands)
    gm, gn, gk = m // tm, n // tn, kdim // tk

    a_spec = pl.BlockSpec((tk, tm), lambda i, j, k: (k, i)) if ta else pl.BlockSpec((tm, tk), lambda i, j, k: (i, k))
    if tb:
        ob = off // tn
        b_spec = pl.BlockSpec((tn, tk), lambda i, j, k: (j + ob, k))
    else:
        ob = off // tk
        b_spec = pl.BlockSpec((tk, tn), lambda i, j, k: (k + ob, j))
    o_spec = pl.BlockSpec((tm, tn), lambda i, j, k: (i, j))
    in_specs = [a_spec, b_spec]
    operands = [a, b]
    if add is not None:
        in_specs.append(o_spec)
        operands.append(add)
    a_dims = (0,) if ta else (1,)
    b_dims = (1,) if tb else (0,)

    def body(*refs):
        if add is not None:
            a_ref, b_ref, add_ref, o_ref, acc_ref = refs
        else:
            a_ref, b_ref, o_ref, acc_ref = refs
        k = pl.program_id(2)

        @pl.when(k == 0)
        def _():
            acc_ref[...] = jnp.zeros_like(acc_ref)

        acc_ref[...] += _dot(a_ref[...], b_ref[...], (a_dims, b_dims))

        @pl.when(k == gk - 1)
        def _():
            r = acc_ref[...]
            if add is not None:
                r = r + add_ref[...].astype(F32)
            o_ref[...] = r.astype(o_ref.dtype)

    est = 2 * (tm * tk * a.dtype.itemsize + tk * tn * b.dtype.itemsize + tm * tn * jnp.dtype(out_dtype).itemsize)
    est += tm * tn * 4 * (3 if add is not None else 1) + 2 * (tm * tk + tk * tn) * 2
    return pl.pallas_call(
        body,
        name=name,
        grid=(gm, gn, gk),
        in_specs=in_specs,
        out_specs=o_spec,
        out_shape=jax.ShapeDtypeStruct((m, n), out_dtype),
        scratch_shapes=[pltpu.VMEM((tm, tn), F32)],
        compiler_params=pltpu.CompilerParams(
            dimension_semantics=("parallel", "parallel", "arbitrary"), vmem_limit_bytes=_vmem(est)
        ),
    )(*operands)


def _rowmap(fn, rows, vecs, outs, reds, *, name, tm=256):
    rows = [r if isinstance(r, tuple) else (r, r.shape[1], 0) for r in rows]
    t = rows[0][0].shape[0]
    tm = _pick(t, (tm, 128, 64, 32, 16, 8))
    nr, nv, no = len(rows), len(vecs), len(outs)

    def body(*refs):
        row_refs, vec_refs = refs[:nr], refs[nr:nr + nv]
        out_refs, red_refs = refs[nr + nv:nr + nv + no], refs[nr + nv + no:]
        o, rd = fn([r[...] for r in row_refs], [v[...] for v in vec_refs])
        for ref, val in zip(out_refs, o):
            ref[...] = val.astype(ref.dtype)
        if red_refs:
            @pl.when(pl.program_id(0) == 0)
            def _():
                for ref in red_refs:
                    ref[...] = jnp.zeros_like(ref)

            for ref, val in zip(red_refs, rd):
                ref[...] += val

    in_specs = [pl.BlockSpec((tm, w), functools.partial(lambda cb, i: (i, cb), cb)) for (_, w, cb) in rows]
    in_specs += [pl.BlockSpec(v.shape, functools.partial(lambda nd, i: (0,) * nd, v.ndim)) for v in vecs]
    out_specs = [pl.BlockSpec((tm, w), lambda i: (i, 0)) for (w, _) in outs]
    out_specs += [pl.BlockSpec((1, w), lambda i: (0, 0)) for w in reds]
    out_shape = [jax.ShapeDtypeStruct((t, w), d) for (w, d) in outs]
    out_shape += [jax.ShapeDtypeStruct((1, w), F32) for w in reds]
    est = 2 * sum(tm * w * a.dtype.itemsize for (a, w, _) in rows) + 2 * sum(_nbytes(v.shape, v.dtype) for v in vecs)
    est += 2 * sum(tm * w * jnp.dtype(d).itemsize for (w, d) in outs)
    est += 4 * tm * max([w for (_, w, _) in rows] + [w for (w, _) in outs]) * 4
    return pl.pallas_call(
        body,
        name=name,
        grid=(t // tm,),
        in_specs=in_specs,
        out_specs=out_specs,
        out_shape=out_shape,
        compiler_params=pltpu.CompilerParams(dimension_semantics=("arbitrary",), vmem_limit_bytes=_vmem(est)),
    )(*[r[0] for r in rows], *vecs)


def _colsum(v):
    return jnp.sum(v, axis=0, keepdims=True)


def _head_ones():
    i = np.arange(LANES)
    return jnp.asarray((i[:, None] // HEAD_DIM) == (i[None, :] // HEAD_DIM), MXU_DTYPE)


def _headsum(y, ones):
    parts = []
    for j in range(y.shape[1] // LANES):
        c = y[:, j * LANES:(j + 1) * LANES]
        hi = c.astype(MXU_DTYPE)
        lo = c - hi.astype(F32)
        parts.append(_dot(hi, ones) + _dot(lo, ones))
    return parts[0] if len(parts) == 1 else jnp.concatenate(parts, axis=1)


def _rot(y, c, s_lo, s_hi):
    parts = []
    for j in range(y.shape[1] // LANES):
        yc = y[:, j * LANES:(j + 1) * LANES]
        parts.append(yc * c + pltpu.roll(yc, LANES - ROT_DIM // 2, 1) * s_lo + pltpu.roll(yc, ROT_DIM // 2, 1) * s_hi)
    return parts[0] if len(parts) == 1 else jnp.concatenate(parts, axis=1)


def _rot_tables(t):
    half = ROT_DIM // 2
    inv_freq = ROPE_THETA ** (-(jnp.arange(half, dtype=F32) * 2.0) / ROT_DIM)
    ang = jnp.arange(t).astype(F32)[:, None] * inv_freq[None, :]
    cos, sin = jnp.cos(ang), jnp.sin(ang)
    z = lambda w: jnp.zeros((t, w), F32)
    c = jnp.concatenate([cos, cos, jnp.ones((t, HEAD_DIM - ROT_DIM), F32)], axis=1)
    s_lo = jnp.concatenate([-sin, z(HEAD_DIM - half)], axis=1)
    s_hi = jnp.concatenate([z(half), sin, z(HEAD_DIM - ROT_DIM)], axis=1)
    return [jnp.tile(a, (1, LANES // HEAD_DIM)) for a in (c, s_lo, s_hi)]


def _rms(x):
    return lax.rsqrt(jnp.mean(x * x, axis=-1, keepdims=True) + EPS)


def _window(kind, n, bq, t, seg):
    if kind == "na":
        rows = t // GRID_W
        rs = jnp.clip(n - NA_KH // 2, 0, rows - NA_KH)
        return rs
    nk = bq + 2 * DIL_HALF
    return jnp.clip(n * bq - DIL_HALF, 0, t - nk)


def _dil_mask(n, bq, nk, ws, seg):
    qi = n * bq + lax.broadcasted_iota(jnp.int32, (bq, nk), 0)
    ki = ws + lax.broadcasted_iota(jnp.int32, (bq, nk), 1)
    shift = int(np.log2(seg))
    return (jnp.abs(ki - qi) <= DIL_HALF) & ((ki >> shift) == (qi >> shift))


HS = 4
QW = HS * HEAD_DIM


def _head_of_lane(width=QW):
    return lax.broadcasted_iota(jnp.int32, (1, width), 1) // HEAD_DIM


def _stack_heads(a):
    head = _head_of_lane()
    return jnp.concatenate([jnp.where(head == e, a, jnp.zeros_like(a)) for e in range(HS)], axis=0)


def _unstack_heads(a, bq):
    head = _head_of_lane()
    out = jnp.zeros((bq, QW), a.dtype)
    for e in range(HS):
        out = jnp.where(head == e, a[e * bq:(e + 1) * bq], out)
    return out


def _stack_cols(blk, bq):
    head = _head_of_lane()
    return jnp.concatenate(
        [jnp.max(jnp.where(head == e, blk, -jnp.inf), axis=1, keepdims=True) for e in range(HS)], axis=0)


def _attn_geometry(kind):
    if kind == "na":
        return GRID_W, NA_KH * GRID_W
    bq = 128
    return bq, bq + 2 * DIL_HALF


def _attn_scores(kind, n, bq, nk, t, seg, qs, k_ref, b_ref):
    scale = HEAD_DIM ** -0.5
    if kind == "na":
        rs = _window(kind, n, bq, t, seg)
        ws = pl.multiple_of(rs * GRID_W, GRID_W)
        ro0 = rs - n + (NA_KH - 1)
        s = _dot_nt(qs, k_ref[pl.ds(ws, nk), :]) * scale
        s = s + jnp.concatenate(
            [jnp.concatenate([b_ref[e, ro0 + 2 * i] for i in range(NA_KH // 2)], axis=1) for e in range(HS)], axis=0)
        return s, ws, ro0
    ws = pl.multiple_of(_window(kind, n, bq, t, seg), DIL_HALF)
    mask = _dil_mask(n, bq, nk, ws, seg)
    s = _dot_nt(qs, k_ref[pl.ds(ws, nk), :]) * scale
    s = jnp.where(jnp.concatenate([mask] * HS, axis=0), s, NEG_INF)
    return s, ws, None


def _attn_fwd(q, k, v, *, kind, name, bias=None, seg=None):
    t, w = q.shape
    quads = w // QW
    bq, nk = _attn_geometry(kind)
    nq = t // bq

    def body(*refs):
        if kind == "na":
            q_ref, k_ref, v_ref, b_ref, o_ref, l_ref = refs
        else:
            (q_ref, k_ref, v_ref, o_ref, l_ref), b_ref = refs, None
        n = pl.program_id(1)
        s, ws, _ = _attn_scores(kind, n, bq, nk, t, seg, _stack_heads(q_ref[...]), k_ref, b_ref)
        m = jnp.max(s, axis=1, keepdims=True)
        p = jnp.exp(s - m)
        l = jnp.sum(p, axis=1, keepdims=True)
        o_ref[...] = _unstack_heads(_dot(p / l, v_ref[pl.ds(ws, nk), :]), bq)
        l_ref[...] = _unstack_heads(jnp.broadcast_to(m + jnp.log(l), (HS * bq, QW)), bq)

    blk = pl.BlockSpec((bq, QW), lambda j, n: (n, j))
    res = pl.BlockSpec((t, QW), lambda j, n: (0, j))
    in_specs = [blk, res, res]
    operands = [q, k, v]
    est = 4 * t * QW * q.dtype.itemsize + 12 * HS * bq * nk * 4
    if kind == "na":
        in_specs.append(pl.BlockSpec((HS,) + bias.shape[1:], lambda j, n: (j, 0, 0, 0)))
        operands.append(bias)
        est += 2 * _nbytes((HS,) + bias.shape[1:], F32)
    return pl.pallas_call(
        body,
        name=name,
        grid=(quads, nq),
        in_specs=in_specs,
        out_specs=[blk, blk],
        out_shape=[jax.ShapeDtypeStruct((t, w), F32)] * 2,
        compiler_params=pltpu.CompilerParams(dimension_semantics=("arbitrary", "arbitrary"), vmem_limit_bytes=_vmem(est)),
    )(*operands)


def _attn_bwd(q, k, v, do, dterm, lse, *, kind, name, bias=None, seg=None):
    t, w = q.shape
    quads = w // QW
    bq, nk = _attn_geometry(kind)
    nq = t // bq
    scale = HEAD_DIM ** -0.5

    def body(*refs):
        if kind == "na":
            q_ref, k_ref, v_ref, do_ref, dt_ref, l_ref, b_ref, dq_ref, dk_hbm, dv_hbm, db_ref, dk_acc, dv_acc, sem = refs
        else:
            q_ref, k_ref, v_ref, do_ref, dt_ref, l_ref, dq_ref, dk_hbm, dv_hbm, dk_acc, dv_acc, sem = refs
            b_ref = None
        j, n = pl.program_id(0), pl.program_id(1)

        @pl.when(n == 0)
        def _():
            dk_acc[...] = jnp.zeros_like(dk_acc)
            dv_acc[...] = jnp.zeros_like(dv_acc)
            if kind == "na":
                db_ref[...] = jnp.zeros_like(db_ref)

        qs = _stack_heads(q_ref[...])
        dos = _stack_heads(do_ref[...])
        s, ws, ro0 = _attn_scores(kind, n, bq, nk, t, seg, qs, k_ref, b_ref)
        p = jnp.exp(s - _stack_cols(l_ref[...], bq))
        dp = _dot_nt(dos, v_ref[pl.ds(ws, nk), :])
        ds = p * (dp - _stack_cols(dt_ref[...], bq))
        if kind == "na":
            for e in range(HS):
                for i in range(NA_KH // 2):
                    db_ref[e, ro0 + 2 * i] += ds[e * bq:(e + 1) * bq, i * LANES:(i + 1) * LANES]
        dsc = ds * scale
        dq_ref[...] = _unstack_heads(_dot(dsc, k_ref[pl.ds(ws, nk), :]), bq)
        dk_acc[pl.ds(ws, nk), :] += _dot_tn(dsc, qs)
        dv_acc[pl.ds(ws, nk), :] += _dot_tn(p, dos)

        @pl.when(n == nq - 1)
        def _():
            ck = pltpu.make_async_copy(dk_acc, dk_hbm.at[j], sem.at[0])
            cv = pltpu.make_async_copy(dv_acc, dv_hbm.at[j], sem.at[1])
            ck.start()
            cv.start()
            ck.wait()
            cv.wait()

    blk = pl.BlockSpec((bq, QW), lambda j, n: (n, j))
    res = pl.BlockSpec((t, QW), lambda j, n: (0, j))
    in_specs = [blk, res, res, blk, blk, blk]
    operands = [q, k, v, do, dterm, lse]
    out_specs = [blk, ANY, ANY]
    out_shape = [jax.ShapeDtypeStruct((t, w), F32)] + [jax.ShapeDtypeStruct((quads, t, QW), F32)] * 2
    est = 4 * t * QW * q.dtype.itemsize + 2 * t * QW * 4 + 16 * HS * bq * nk * 4
    if kind == "na":
        bspec = pl.BlockSpec((HS,) + bias.shape[1:], lambda j, n: (j, 0, 0, 0))
        in_specs.append(bspec)
        operands.append(bias)
        out_specs.append(bspec)
        out_shape.append(jax.ShapeDtypeStruct(bias.shape, F32))
        est += 4 * _nbytes((HS,) + bias.shape[1:], F32)
    res_ = pl.pallas_call(
        body,
        name=name,
        grid=(quads, nq),
        in_specs=in_specs,
        out_specs=out_specs,
        out_shape=out_shape,
        scratch_shapes=[pltpu.VMEM((t, QW), F32), pltpu.VMEM((t, QW), F32), pltpu.SemaphoreType.DMA((2,))],
        compiler_params=pltpu.CompilerParams(dimension_semantics=("arbitrary", "arbitrary"), vmem_limit_bytes=_vmem(est)),
    )(*operands)
    unquad = lambda a: [a[i] for i in range(quads)]
    return (res_[0], unquad(res_[1]), unquad(res_[2])) + tuple(res_[3:])


def _na_onehot():
    qc = np.arange(GRID_W)[:, None]
    kc = np.arange(GRID_W)[None, :]
    start = np.clip(qc - NA_KW // 2, 0, GRID_W - NA_KW)
    inwin = (kc >= start) & (kc < start + NA_KW)
    off = kc - qc + (NA_KW - 1)
    e_mat = np.zeros((2, 32, GRID_W, 2, GRID_W), np.float32)
    for e in range(2):
        for c in range(2 * NA_KW - 1):
            e_mat[e, c, :, e, :] = (off == c) & inwin
    neg = np.where(inwin, 0.0, NEG_INF).astype(np.float32)
    neg = np.broadcast_to(neg[:, None, :], (GRID_W, 2, GRID_W)).reshape(1, GRID_W * LANES)
    return jnp.asarray(e_mat.reshape(64, GRID_W * LANES), MXU_DTYPE), jnp.asarray(neg)


def _na_rowpairs(rpb):
    p = jnp.pad(rpb, ((0, 0), (0, 0), (0, 1)))
    return jnp.concatenate([p[:, :-1], p[:, 1:]], axis=-1).reshape(NA_HEADS * (2 * NA_KH - 2), 64)


def _na_bias_table(rpb):
    r2 = _na_rowpairs(rpb)
    e_mat, neg = _na_onehot()

    def body(r_ref, e_ref, n_ref, o_ref):
        hi, mid, lo = _split3(r_ref[...])
        e = e_ref[...]
        o_ref[...] = _dot(hi, e) + _dot(mid, e) + _dot(lo, e) + n_ref[...]

    out = pl.pallas_call(
        body,
        name="na_bias_table",
        out_shape=jax.ShapeDtypeStruct((r2.shape[0], GRID_W * LANES), F32),
        compiler_params=pltpu.CompilerParams(vmem_limit_bytes=_vmem(6 * r2.shape[0] * GRID_W * LANES * 4)),
    )(r2, e_mat, neg)
    return out.reshape(NA_HEADS, 2 * NA_KH - 2, GRID_W, LANES)


def _na_bias_grad(dbt):
    e_mat, _ = _na_onehot()
    flat = dbt.reshape(NA_HEADS * (2 * NA_KH - 2), GRID_W * LANES)

    def body(d_ref, e_ref, o_ref):
        hi, mid, lo = _split3(d_ref[...])
        e = e_ref[...]
        o_ref[...] = _dot_nt(hi, e) + _dot_nt(mid, e) + _dot_nt(lo, e)

    g = pl.pallas_call(
        body,
        name="na_bias_grad",
        out_shape=jax.ShapeDtypeStruct((flat.shape[0], 64), F32),
        compiler_params=pltpu.CompilerParams(vmem_limit_bytes=_vmem(6 * flat.shape[0] * GRID_W * LANES * 4)),
    )(flat, e_mat)
    g = g.reshape(NA_HEADS, 2 * NA_KH - 2, 2, 32)[..., :2 * NA_KW - 1]
    first = jnp.pad(g[:, :, 0], ((0, 0), (0, 1), (0, 0)))
    second = jnp.pad(g[:, :, 1], ((0, 0), (1, 0), (0, 0)))
    return first + second


def _all_gather(arrs, *, name):
    na = len(arrs)

    def body(*refs):
        ins, outs = refs[:na], refs[na:2 * na]
        send_sems, recv_sems, local_sems = refs[2 * na:]
        x, y, c = lax.axis_index("x"), lax.axis_index("y"), lax.axis_index("c")
        me, sibling = (x, y, c), (x, y, 1 - c)
        chips = [(1 - x, y), (x, 1 - y), (1 - x, 1 - y)]

        def rows(a, px, py, pc):
            r = ins[a].shape[0]
            return outs[a].at[pl.ds((4 * px + 2 * py + pc) * r, r), :]

        def copy(a, k, block, to, src=None):
            return pltpu.make_async_remote_copy(
                src_ref=rows(a, *block) if src is None else src, dst_ref=rows(a, *block),
                send_sem=send_sems.at[a, k], recv_sem=recv_sems.at[a, k], device_id=to, device_id_type=MESH)

        mine = [pltpu.make_async_copy(ins[a], rows(a, *me), local_sems.at[a]) for a in range(na)]
        for cp in mine:
            cp.start()
        first = []
        for a in range(na):
            first.append(copy(a, 0, me, sibling, src=ins[a]))
            first += [copy(a, 1 + j, me, (*chip, c), src=ins[a]) for j, chip in enumerate(chips)]
        for cp in first:
            cp.start()
        passed = []
        for j, chip in enumerate(chips):
            for a in range(na):
                copy(a, 1 + j, (*chip, c), me).wait_recv()
                cp = copy(a, 4 + j, (*chip, c), sibling)
                cp.start()
                passed.append(cp)
        for a in range(na):
            copy(a, 0, sibling, me).wait_recv()
        for j, chip in enumerate(chips):
            for a in range(na):
                copy(a, 4 + j, (*chip, 1 - c), me).wait_recv()
        for cp in first + passed:
            cp.wait_send()
        for cp in mine:
            cp.wait()

    return pl.pallas_call(
        body,
        name=name,
        in_specs=[ANY] * na,
        out_specs=[ANY] * na,
        out_shape=[jax.ShapeDtypeStruct((N_DEV * a.shape[0], a.shape[1]), a.dtype) for a in arrs],
        scratch_shapes=[pltpu.SemaphoreType.DMA((na, 7)), pltpu.SemaphoreType.DMA((na, 7)), pltpu.SemaphoreType.DMA((na,))],
    )(*arrs)


def _exchange(arrs, *, name):
    na = len(arrs)

    def body(*refs):
        ins, outs = refs[:na], refs[na:2 * na]
        send_sems, recv_sems, local_sems = refs[2 * na:]
        x, y, c = lax.axis_index("x"), lax.axis_index("y"), lax.axis_index("c")
        my = 4 * x + 2 * y + c

        def block(a, idx):
            r = ins[a].shape[0] // N_DEV
            return ins[a].at[pl.ds(idx * r, r), :]

        def copy(a, k):
            px, py, pc = x ^ ((k >> 2) & 1), y ^ ((k >> 1) & 1), c ^ (k & 1)
            peer = 4 * px + 2 * py + pc
            snd = pltpu.make_async_remote_copy(
                src_ref=block(a, peer), dst_ref=outs[a].at[my], send_sem=send_sems.at[a, k - 1],
                recv_sem=recv_sems.at[a, k - 1], device_id=(px, py, pc), device_id_type=MESH)
            rcv = pltpu.make_async_remote_copy(
                src_ref=block(a, peer), dst_ref=outs[a].at[peer], send_sem=send_sems.at[a, k - 1],
                recv_sem=recv_sems.at[a, k - 1], device_id=(px, py, pc), device_id_type=MESH)
            return snd, rcv

        mine = [pltpu.make_async_copy(block(a, my), outs[a].at[my], local_sems.at[a]) for a in range(na)]
        for cp in mine:
            cp.start()
        pairs = [copy(a, k) for k in range(1, N_DEV) for a in range(na)]
        for snd, _ in pairs:
            snd.start()
        for _, rcv in pairs:
            rcv.wait_recv()
        for snd, _ in pairs:
            snd.wait_send()
        for cp in mine:
            cp.wait()

    return pl.pallas_call(
        body,
        name=name,
        in_specs=[ANY] * na,
        out_specs=[ANY] * na,
        out_shape=[jax.ShapeDtypeStruct((N_DEV, a.shape[0] // N_DEV, a.shape[1]), a.dtype) for a in arrs],
        scratch_shapes=[pltpu.SemaphoreType.DMA((na, 7)), pltpu.SemaphoreType.DMA((na, 7)), pltpu.SemaphoreType.DMA((na,))],
    )(*arrs)


def _sum8(a, *, name):
    _, r, w = a.shape
    tr = _pick(r, (256, 128, 64, 32, 16, 8))

    def body(a_ref, o_ref):
        acc = a_ref[0].astype(F32)
        for i in range(1, N_DEV):
            acc = acc + a_ref[i].astype(F32)
        o_ref[...] = acc

    return pl.pallas_call(
        body,
        name=name,
        grid=(r // tr,),
        in_specs=[pl.BlockSpec((N_DEV, tr, w), lambda i: (0, i, 0))],
        out_specs=pl.BlockSpec((tr, w), lambda i: (i, 0)),
        out_shape=jax.ShapeDtypeStruct((r, w), F32),
        compiler_params=pltpu.CompilerParams(dimension_semantics=("parallel",), vmem_limit_bytes=_vmem(4 * N_DEV * tr * w * 4)),
    )(a)


def _adamw(w, g, m, v, *, name):
    def fn(rows, _):
        wv, gv, mv, vv = rows
        m1 = ADAM_B1 * mv + (1.0 - ADAM_B1) * gv
        v1 = ADAM_B2 * vv + (1.0 - ADAM_B2) * jnp.square(gv)
        m_hat = m1 / (1.0 - ADAM_B1 ** ADAM_STEP)
        v_hat = v1 / (1.0 - ADAM_B2 ** ADAM_STEP)
        delta = -ADAM_LR * (m_hat / (jnp.sqrt(v_hat) + ADAM_EPS) + ADAM_WD * wv)
        return [delta, m1, v1], []

    c = w.shape[1]
    return _rowmap(fn, [w, g, m, v], [], [(c, F32)] * 3, [], name=name, tm=128)


_SMALL = ("b_ada", "g_norm1", "g_norm2", "b_gate", "g_qa", "g_ka", "g_qb", "g_kb", "rpb", "loss")


def _pack_small(parts):
    flat = []
    for nme in _SMALL:
        a = parts[nme].reshape(-1).astype(F32)
        flat.append(jnp.pad(a, (0, (-a.shape[0]) % LANES)))
    flat = jnp.concatenate(flat)
    flat = jnp.pad(flat, (0, (-flat.shape[0]) % (8 * LANES)))
    return flat.reshape(-1, LANES)


def _unpack_small(packed, shapes):
    flat = packed.reshape(-1)
    out, pos = {}, 0
    for nme in _SMALL:
        n = int(np.prod(shapes[nme]))
        out[nme] = flat[pos:pos + n].reshape(shapes[nme])
        pos += n + (-n) % LANES
    return out


def _to_class(a, d):
    t, w = a.shape
    return a if d == 1 else a.reshape(t // d, d, w).transpose(1, 0, 2).reshape(t, w)


def _from_class(a, d):
    t, w = a.shape
    return a if d == 1 else a.reshape(d, t // d, w).transpose(1, 0, 2).reshape(t, w)


def kernel(x, c, w_ada, b_ada, g_norm1, g_norm2, w_in, b_gate, g_qa, g_ka, g_qb, g_kb, rpb, w_proj_a, w_proj_b, w_o, w_ffn_in, w_ffn_out, loss_target, m_w_ada, m_b_ada, m_g_norm1, m_g_norm2, m_w_in, m_b_gate, m_g_qa, m_g_ka, m_g_qb, m_g_kb, m_rpb, m_w_proj_a, m_w_proj_b, m_w_o, m_w_ffn_in, m_w_ffn_out, v_w_ada, v_b_ada, v_g_norm1, v_g_norm2, v_w_in, v_b_gate, v_g_qa, v_g_ka, v_g_qb, v_g_kb, v_rpb, v_w_proj_a, v_w_proj_b, v_w_o, v_w_ffn_in, v_w_ffn_out):
    t, d = x.shape[1], x.shape[2]
    d_ff = w_ffn_out.shape[1] * N_DEV
    me = 4 * lax.axis_index("x") + 2 * lax.axis_index("y") + lax.axis_index("c")
    xt, tgt = x[0], loss_target[0]
    ones = _head_ones()

    shards = [w_in[0].T, w_ffn_in[0].T, w_proj_a[0].T, w_proj_b[0].T, w_o[0], w_ffn_out[0]]
    w_in_t, w_ffn_in_t, w_pa_t, w_pb_t, w_o_f, w_ffn_out_f = _all_gather(
        [s.astype(WIRE_DTYPE) for s in shards], name="gather_weights")

    c_all = _all_gather([jnp.pad(c, ((0, 7), (0, 0)))], name="gather_c")[0][::8]
    c_all = jnp.pad(c_all, ((0, LANES - N_DEV), (0, 0)))

    def mod_body(c_ref, w_ref, b_ref, o_ref, act_ref):
        act = _silu(c_ref[...])
        act_ref[...] = act
        hi, mid, lo = _split3(act)
        w = w_ref[...]
        whi, wmid, wlo = _split3(w)
        acc = _dot(hi, whi) + (_dot(hi, wmid) + _dot(mid, whi)) + (_dot(hi, wlo) + _dot(mid, wmid) + _dot(lo, whi))
        o_ref[...] = acc + b_ref[...]

    ncol = w_ada.shape[2]
    b_ada_mine = lax.dynamic_slice(b_ada, (0, me * ncol), (1, ncol))
    mod_part, c_act = pl.pallas_call(
        mod_body,
        name="ada_mod",
        out_shape=[jax.ShapeDtypeStruct((LANES, ncol), F32), jax.ShapeDtypeStruct((LANES, d), F32)],
        compiler_params=pltpu.CompilerParams(vmem_limit_bytes=_vmem(6 * d * ncol * 4)),
    )(c_all, w_ada[0], b_ada_mine)
    mod_all = _all_gather([mod_part[:N_DEV]], name="gather_mod")[0].reshape(N_DEV, N_DEV, ncol)
    mod = lax.dynamic_index_in_dim(mod_all, me, axis=1, keepdims=False).reshape(6, d)
    sh1, sc1, gt1, sh2, sc2, gt2 = [mod[i:i + 1] for i in range(6)]

    def norm_fwd(rows, vecs):
        (xv,), (g, sc, sh) = rows, vecs
        return [xv * _rms(xv) * g * (1.0 + sc) + sh], []

    (h,) = _rowmap(norm_fwd, [xt], [g_norm1, sc1, sh1], [(d, MXU_DTYPE)], [], name="norm1")
    n_a, n_b = 3 * WA, 3 * WB
    qkv_a = _mm(h, w_in_t, tb=True, b_rows=(0, n_a), name="proj_a")
    qkv_b = _mm(h, w_in_t, tb=True, b_rows=(n_a, n_b), name="proj_b")
    gates = _mm(h, w_in_t, tb=True, b_rows=(n_a + n_b, 2 * d), name="proj_gates")

    rot_c, rot_lo, rot_hi = _rot_tables(t)
    tile_g = lambda g, heads: jnp.tile(g, (1, heads))

    def qk_fwd(width, rotate):
        def fn(rows, vecs):
            xv = rows[0]
            gq, gk, on = vecs
            outs = []
            for i, g in enumerate((gq, gk)):
                xi = xv[:, i * width:(i + 1) * width]
                r = lax.rsqrt(_headsum(xi * xi, on) * (1.0 / HEAD_DIM) + EPS)
                yi = xi * r * g
                if rotate:
                    yi = _rot(yi, rows[1], rows[2], rows[3])
                outs.append(yi)
            outs.append(xv[:, 2 * width:])
            return outs, []
        return fn

    qa, ka, va = _rowmap(qk_fwd(WA, False), [qkv_a], [tile_g(g_qa, NA_HEADS), tile_g(g_ka, NA_HEADS), ones],
                         [(WA, MXU_DTYPE)] * 3, [], name="qknorm_a")
    qb, kb, vb = _rowmap(qk_fwd(WB, True), [qkv_b, rot_c, rot_lo, rot_hi],
                         [tile_g(g_qb, DIL_HEADS), tile_g(g_kb, DIL_HEADS), ones], [(WB, MXU_DTYPE)] * 3, [], name="qknorm_b")

    bias_tab = _na_bias_table(rpb[0])
    o_a, lse_a = _attn_fwd(qa, ka, va, kind="na", bias=bias_tab, name="na_fwd")

    grp = []
    for g, (_, dil) in enumerate(DIL_CONFIGS):
        sl = slice(g * WB_OUT, (g + 1) * WB_OUT)
        qg, kg, vg = [_to_class(a[:, sl], dil) for a in (qb, kb, vb)]
        og, lg = _attn_fwd(qg, kg, vg, kind="dil", seg=t // dil, name=f"dil_fwd{g}")
        grp.append(dict(q=qg, k=kg, v=vg, o=_from_class(og, dil), lse=_from_class(lg, dil), lse_c=lg, dil=dil))

    def merge_fwd(rows, _):
        o0, o1, o2, l0, l1, l2 = rows
        mx = jnp.maximum(jnp.maximum(l0, l1), l2)
        e0, e1, e2 = jnp.exp(l0 - mx), jnp.exp(l1 - mx), jnp.exp(l2 - mx)
        s = e0 + e1 + e2
        return [(e0 / s) * o0 + (e1 / s) * o1 + (e2 / s) * o2], []

    (o_b,) = _rowmap(merge_fwd, [gr["o"] for gr in grp] + [gr["lse"] for gr in grp], [], [(WB_OUT, F32)], [], name="dil_merge")

    pa = _mm(o_a, w_pa_t, tb=True, name="proj_out_a")
    pb = _mm(o_b, w_pb_t, tb=True, name="proj_out_b")

    def gate_fwd(rows, vecs):
        gv, pav, pbv = rows
        sg = jax.nn.sigmoid(gv + vecs[0])
        return [sg[:, :d] * pav + sg[:, d:] * pbv], []

    (merged,) = _rowmap(gate_fwd, [gates, pa, pb], [b_gate], [(d, MXU_DTYPE)], [], name="gate_merge")
    att = _mm(merged, w_o_f, name="proj_o")

    def resid_norm(rows, vecs):
        xv, av = rows
        gt, g, sc, sh = vecs
        x1v = xv + gt * av
        return [x1v, x1v * _rms(x1v) * g * (1.0 + sc) + sh], []

    x1, h2 = _rowmap(resid_norm, [xt, att], [gt1, g_norm2, sc2, sh2], [(d, F32), (d, MXU_DTYPE)], [], name="resid_norm2")

    u = _mm(h2, w_ffn_in_t, tb=True, name="ffn_in")

    def swiglu_fwd(rows, _):
        uv = rows[0]
        return [_silu(uv[:, :d_ff]) * uv[:, d_ff:]], []

    (f,) = _rowmap(swiglu_fwd, [u], [], [(d_ff, MXU_DTYPE)], [], name="swiglu")
    y2 = _mm(f, w_ffn_out_f, name="ffn_out")

    def loss_fn(rows, vecs):
        yv, x1v, tv = rows
        gt = vecs[0]
        err = x1v + gt * yv - tv
        dout = err * (1.0 / d)
        return [dout, dout * gt], [_colsum(err * err), _colsum(dout * yv)]

    dout, dy2, err2, dgt2 = _rowmap(loss_fn, [y2, x1, tgt], [gt2], [(d, F32), (d, MXU_DTYPE)], [d, d], name="loss")

    dw_ffn_out = _mm(f, dy2, ta=True, out_dtype=WIRE_DTYPE, name="wgrad_ffn_out")
    df = _mm(dy2, w_ffn_out_f, tb=True, name="dgrad_ffn_out")

    def swiglu_bwd(rows, _):
        dfv, uv = rows
        a, up = uv[:, :d_ff], uv[:, d_ff:]
        sg = jax.nn.sigmoid(a)
        da = dfv * up * (sg * (1.0 + a * (1.0 - sg)))
        return [jnp.concatenate([da, dfv * (a * sg)], axis=1)], []

    (du,) = _rowmap(swiglu_bwd, [df, u], [], [(2 * d_ff, MXU_DTYPE)], [], name="swiglu_bwd", tm=128)
    dw_ffn_in_t = _mm(du, h2, ta=True, out_dtype=WIRE_DTYPE, name="wgrad_ffn_in")
    dh2 = _mm(du, w_ffn_in_t, name="dgrad_ffn_in")

    def norm_bwd(dh, xv, g, sc):
        r = _rms(xv)
        xh = xv * r
        dxh = dh * g * (1.0 + sc)
        dxv = r * (dxh - xh * jnp.mean(dxh * xh, axis=-1, keepdims=True))
        return dxv, [_colsum(dh), _colsum(dh * xh * g), _colsum(dh * xh * (1.0 + sc))]

    def norm2_bwd(rows, vecs):
        dhv, x1v, dov, av = rows
        g, sc, gt = vecs
        dxv, sums = norm_bwd(dhv, x1v, g, sc)
        dx1v = dov + dxv
        return [dx1v, dx1v * gt], sums + [_colsum(dx1v * av)]

    dx1, datt, dsh2, dsc2, dg2, dgt1 = _rowmap(norm2_bwd, [dh2, x1, dout, att], [g_norm2, sc2, gt1],
                                                [(d, F32), (d, MXU_DTYPE)], [d] * 4, name="norm2_bwd")
    dw_o = _mm(merged, datt, ta=True, out_dtype=WIRE_DTYPE, name="wgrad_o")
    dmerged = _mm(datt, w_o_f, tb=True, name="dgrad_o")

    def gate_bwd(rows, vecs):
        dm, gv, pav, pbv = rows
        sg = jax.nn.sigmoid(gv + vecs[0])
        ga, gb = sg[:, :d], sg[:, d:]
        dgp = jnp.concatenate([dm * pav * ga * (1.0 - ga), dm * pbv * gb * (1.0 - gb)], axis=1)
        return [dm * ga, dm * gb, dgp], [_colsum(dgp)]

    dpa, dpb, dgates, db_gate = _rowmap(gate_bwd, [dmerged, gates, pa, pb], [b_gate],
                                        [(d, MXU_DTYPE), (d, MXU_DTYPE), (2 * d, MXU_DTYPE)], [2 * d], name="gate_bwd")
    dw_pa_t = _mm(dpa, o_a, ta=True, out_dtype=WIRE_DTYPE, name="wgrad_proj_a")
    dw_pb_t = _mm(dpb, o_b, ta=True, out_dtype=WIRE_DTYPE, name="wgrad_proj_b")
    do_a = _mm(dpa, w_pa_t, name="dgrad_proj_a")
    do_b = _mm(dpb, w_pb_t, name="dgrad_proj_b")

    def delta_a(rows, vecs):
        return [_headsum(rows[0] * rows[1], vecs[0])], []

    (dterm_a,) = _rowmap(delta_a, [do_a, o_a], [ones], [(WA, F32)], [], name="na_delta")
    dqa, dka, dva, dbias = _attn_bwd(qa, ka, va, do_a, dterm_a, lse_a, kind="na", bias=bias_tab, name="na_bwd")
    g_rpb = _na_bias_grad(dbias)

    def merge_bwd(rows, vecs):
        dob, o0, o1, o2, l0, l1, l2 = rows
        on = vecs[0]
        mx = jnp.maximum(jnp.maximum(l0, l1), l2)
        e0, e1, e2 = jnp.exp(l0 - mx), jnp.exp(l1 - mx), jnp.exp(l2 - mx)
        s = e0 + e1 + e2
        ws = [e0 / s, e1 / s, e2 / s]
        dws = [_headsum(dob * o, on) for o in (o0, o1, o2)]
        mean = ws[0] * dws[0] + ws[1] * dws[1] + ws[2] * dws[2]
        return [w * dob for w in ws] + [w * mean for w in ws], []

    mb = _rowmap(merge_bwd, [do_b] + [gr["o"] for gr in grp] + [gr["lse"] for gr in grp], [ones],
                 [(WB_OUT, F32)] * 6, [], name="dil_merge_bwd")
    dqb, dkb, dvb = [], [], []
    for g, gr in enumerate(grp):
        dil = gr["dil"]
        dq, dk, dv = _attn_bwd(gr["q"], gr["k"], gr["v"], _to_class(mb[g], dil), _to_class(mb[3 + g], dil), gr["lse_c"],
                               kind="dil", seg=t // dil, name=f"dil_bwd{g}")
        dqb.append(_from_class(dq, dil))
        dkb.append(_from_class(dk[0], dil))
        dvb.append(_from_class(dv[0], dil))

    def qk_bwd(width, rotate, nparts):
        def fn(rows, vecs):
            gq, gk, on = vecs
            xv = rows[0]
            pos = 1
            if rotate:
                rc, rlo, rhi = rows[1:4]
                pos = 4
            cat = lambda parts: parts[0] if len(parts) == 1 else jnp.concatenate(parts, axis=1)
            ends = np.cumsum((pos,) + nparts)
            dq, dk, dv = [cat(rows[ends[i]:ends[i + 1]]) for i in range(3)]
            outs, sums = [], []
            for i, (dy, g) in enumerate(((dq, gq), (dk, gk))):
                if rotate:
                    dy = _rot(dy, rc, -rlo, -rhi)
                xi = xv[:, i * width:(i + 1) * width]
                r = lax.rsqrt(_headsum(xi * xi, on) * (1.0 / HEAD_DIM) + EPS)
                xh = xi * r
                dxh = dy * g
                outs.append(r * (dxh - xh * (_headsum(dxh * xh, on) * (1.0 / HEAD_DIM))))
                sums.append(_colsum(dy * xh))
            return [jnp.concatenate(outs + [dv], axis=1)], sums
        return fn

    dqkv_a, dg_qa, dg_ka = _rowmap(qk_bwd(WA, False, (1, len(dka), len(dva))), [qkv_a, dqa] + dka + dva,
                                   [tile_g(g_qa, NA_HEADS), tile_g(g_ka, NA_HEADS), ones],
                                   [(3 * WA, MXU_DTYPE)], [WA, WA], name="qknorm_a_bwd", tm=128)
    dqkv_b, dg_qb, dg_kb = _rowmap(qk_bwd(WB, True, (3, 3, 3)), [qkv_b, rot_c, rot_lo, rot_hi] + dqb + dkb + dvb,
                                   [tile_g(g_qb, DIL_HEADS), tile_g(g_kb, DIL_HEADS), ones],
                                   [(3 * WB, MXU_DTYPE)], [WB, WB], name="qknorm_b_bwd", tm=128)

    dw_in_t = jnp.concatenate([
        _mm(dqkv_a, h, ta=True, out_dtype=WIRE_DTYPE, name="wgrad_in_a"),
        _mm(dqkv_b, h, ta=True, out_dtype=WIRE_DTYPE, name="wgrad_in_b"),
        _mm(dgates, h, ta=True, out_dtype=WIRE_DTYPE, name="wgrad_in_gates")], axis=0)
    dh = _mm(dqkv_a, w_in_t, b_rows=(0, n_a), name="dgrad_in_a")
    dh = _mm(dqkv_b, w_in_t, b_rows=(n_a, n_b), add=dh, name="dgrad_in_b")
    dh = _mm(dgates, w_in_t, b_rows=(n_a + n_b, 2 * d), add=dh, name="dgrad_in_gates")

    def norm1_bwd(rows, vecs):
        dhv, xv, dx1v = rows
        dxv, sums = norm_bwd(dhv, xv, vecs[0], vecs[1])
        return [dx1v + dxv], sums

    grad_x, dsh1, dsc1, dg1 = _rowmap(norm1_bwd, [dh, xt, dx1], [g_norm1, sc1], [(d, F32)], [d] * 3, name="norm1_bwd")

    heads_sum = lambda a, heads: a.reshape(heads, HEAD_DIM).sum(axis=0)
    dmod = jnp.concatenate([dsh1, dsc1, dgt1, dsh2, dsc2, dgt2], axis=1)
    local_small = _pack_small(dict(
        b_ada=dmod, g_norm1=dg1, g_norm2=dg2, b_gate=db_gate, g_qa=heads_sum(dg_qa, NA_HEADS),
        g_ka=heads_sum(dg_ka, NA_HEADS), g_qb=heads_sum(dg_qb, DIL_HEADS), g_kb=heads_sum(dg_kb, DIL_HEADS),
        rpb=g_rpb, loss=(0.5 / d) * jnp.sum(err2)))
    srows = local_small.shape[0]
    small_all = _all_gather([local_small], name="gather_small")[0].reshape(N_DEV, srows, LANES)
    small_sum = _sum8(small_all, name="sum_small")
    small_shapes = dict(b_ada=b_ada.shape, g_norm1=g_norm1.shape, g_norm2=g_norm2.shape, b_gate=b_gate.shape,
                        g_qa=g_qa.shape, g_ka=g_ka.shape, g_qb=g_qb.shape, g_kb=g_kb.shape, rpb=rpb.shape, loss=())
    small_w = dict(b_ada=b_ada, g_norm1=g_norm1, g_norm2=g_norm2, b_gate=b_gate, g_qa=g_qa, g_ka=g_ka, g_qb=g_qb,
                   g_kb=g_kb, rpb=rpb, loss=jnp.zeros((), F32))
    small_m = dict(b_ada=m_b_ada, g_norm1=m_g_norm1, g_norm2=m_g_norm2, b_gate=m_b_gate, g_qa=m_g_qa, g_ka=m_g_ka,
                   g_qb=m_g_qb, g_kb=m_g_kb, rpb=m_rpb, loss=jnp.zeros((), F32))
    small_v = dict(b_ada=v_b_ada, g_norm1=v_g_norm1, g_norm2=v_g_norm2, b_gate=v_b_gate, g_qa=v_g_qa, g_ka=v_g_ka,
                   g_qb=v_g_qb, g_kb=v_g_kb, rpb=v_rpb, loss=jnp.zeros((), F32))
    s_delta, s_m, s_v = _adamw(_pack_small(small_w), small_sum, _pack_small(small_m), _pack_small(small_v), name="adamw_small")
    gs = _unpack_small(small_sum, small_shapes)
    ds_, ms_, vs_ = [_unpack_small(a, small_shapes) for a in (s_delta, s_m, s_v)]

    dmod_all = small_all[:, :6 * d // LANES].reshape(N_DEV, 6 * d)
    dmod_mine = jnp.pad(lax.dynamic_slice(dmod_all, (0, me * ncol), (N_DEV, ncol)), ((0, LANES - N_DEV), (0, 0)))

    def wada_body(c_ref, dm_ref, o_ref):
        chi, cmid, clo = _split3(c_ref[...])
        dhi, dmid, dlo = _split3(dm_ref[...])
        o_ref[...] = (_dot_tn(chi, dhi) + (_dot_tn(chi, dmid) + _dot_tn(cmid, dhi))
                      + (_dot_tn(chi, dlo) + _dot_tn(cmid, dmid) + _dot_tn(clo, dhi)))

    g_w_ada = pl.pallas_call(
        wada_body,
        name="wgrad_ada",
        out_shape=jax.ShapeDtypeStruct((d, ncol), F32),
        compiler_params=pltpu.CompilerParams(vmem_limit_bytes=_vmem(4 * d * ncol * 4)),
    )(c_act, dmod_mine)

    recv = _exchange([dw_in_t, dw_ffn_in_t, dw_pa_t, dw_pb_t, dw_o, dw_ffn_out], name="exchange_grads")
    names = ("w_in", "w_ffn_in", "w_proj_a", "w_proj_b", "w_o", "w_ffn_out")
    transposed = (True, True, True, True, False, False)
    big_g = {}
    for nme, r, tr in zip(names, recv, transposed):
        s = _sum8(r, name=f"sum_{nme}")
        big_g[nme] = s.T if tr else s
    big_g["w_ada"] = g_w_ada
    big_w = dict(w_ada=w_ada, w_in=w_in, w_proj_a=w_proj_a, w_proj_b=w_proj_b, w_o=w_o, w_ffn_in=w_ffn_in, w_ffn_out=w_ffn_out)
    big_m = dict(w_ada=m_w_ada, w_in=m_w_in, w_proj_a=m_w_proj_a, w_proj_b=m_w_proj_b, w_o=m_w_o, w_ffn_in=m_w_ffn_in, w_ffn_out=m_w_ffn_out)
    big_v = dict(w_ada=v_w_ada, w_in=v_w_in, w_proj_a=v_w_proj_a, w_proj_b=v_w_proj_b, w_o=v_w_o, w_ffn_in=v_w_ffn_in, w_ffn_out=v_w_ffn_out)
    grads, deltas, new_m, new_v = {}, {}, {}, {}
    for nme in big_w:
        dl, m1, v1 = _adamw(big_w[nme][0], big_g[nme], big_m[nme][0], big_v[nme][0], name=f"adamw_{nme}")
        grads[nme], deltas[nme], new_m[nme], new_v[nme] = big_g[nme][None], dl[None], m1[None], v1[None]
    for nme in _SMALL[:-1]:
        grads[nme], deltas[nme], new_m[nme], new_v[nme] = gs[nme], ds_[nme], ms_[nme], vs_[nme]

    order = ("w_ada", "b_ada", "g_norm1", "g_norm2", "w_in", "b_gate", "g_qa", "g_ka", "g_qb", "g_kb", "rpb",
             "w_proj_a", "w_proj_b", "w_o", "w_ffn_in", "w_ffn_out")
    return (gs["loss"], grad_x[None], *[grads[n] for n in order], *[deltas[n] for n in order],
            *[new_m[n] for n in order], *[new_v[n] for n in order])
```

```python
import functools

import numpy as np
import jax
import jax.numpy as jnp
from jax import lax
from jax.experimental import pallas as pl
from jax.experimental.pallas import tpu as pltpu

F32 = jnp.float32
MXU_DTYPE = jnp.bfloat16
WIRE_DTYPE = jnp.bfloat16

HEAD_DIM = 64
GRID_W = 64
NA_HEADS = 8
NA_KH = 8
NA_KW = 16
DIL_CONFIGS = ((128, 1), (512, 4), (2048, 16))
DIL_HEADS_PER_GROUP = 4
DIL_HEADS = DIL_HEADS_PER_GROUP * len(DIL_CONFIGS)
DIL_HALF = 64
ROT_DIM = HEAD_DIM // 4
ROPE_THETA = 500000.0
EPS = 1e-6
NEG_INF = -1e30
WA = NA_HEADS * HEAD_DIM
WB = DIL_HEADS * HEAD_DIM
WB_OUT = DIL_HEADS_PER_GROUP * HEAD_DIM
ADAM_LR = 0.001
ADAM_B1 = 0.9
ADAM_B2 = 0.999
ADAM_EPS = 1e-08
ADAM_WD = 0.01
ADAM_STEP = 10

N_DEV = 8
LANES = 128
VMEM_CAP = 60 * 2**20
MESH = pl.DeviceIdType.MESH
ANY = pl.BlockSpec(memory_space=pl.ANY)


def _vmem(nbytes):
    return int(min(VMEM_CAP, max(16 * 2**20, nbytes * 5 // 4 + 4 * 2**20)))


def _pick(dim, cands):
    for c in cands:
        if c <= dim and dim % c == 0:
            return c
    return dim


def _nbytes(shape, dtype):
    return int(np.prod(shape)) * jnp.dtype(dtype).itemsize


def _dot(a, b, dims=((1,), (0,))):
    return lax.dot_general(a.astype(MXU_DTYPE), b.astype(MXU_DTYPE), (dims, ((), ())), preferred_element_type=F32)


def _dot_nt(a, b):
    return _dot(a, b, ((1,), (1,)))


def _dot_tn(a, b):
    return _dot(a, b, ((0,), (0,)))


def _split3(a):
    hi = a.astype(jnp.bfloat16)
    r1 = a - hi.astype(F32)
    mid = r1.astype(jnp.bfloat16)
    lo = (r1 - mid.astype(F32)).astype(jnp.bfloat16)
    return hi, mid, lo


def _silu(x):
    return x * jax.nn.sigmoid(x)


def _mm(a, b, *, name, ta=False, tb=False, out_dtype=F32, add=None, b_rows=None):
    if ta:
        kdim, m = a.shape
    else:
        m, kdim = a.shape
    off, size = b_rows if b_rows is not None else (0, b.shape[0])
    if tb:
        n = size
        assert b.shape[1] == kdim
    else:
        n = b.shape[1]
        assert size == kdim
    cands = (1024, 768, 512, 384, 256, 128)
    tm = _pick(m, cands)
    if tb:
        tn = _pick(int(np.gcd(off, n)) if off else n, cands)
        tk = _pick(kdim, cands)
    else:
        tn = _pick(n, cands)
        tk = _pick(int(np.gcd(off, kdim)) if off else kdim, cands)
    gm, gn, gk = m // tm, n // tn, kdim // tk

    a_spec = pl.BlockSpec((tk, tm), lambda i, j, k: (k, i)) if ta else pl.BlockSpec((tm, tk), lambda i, j, k: (i, k))
    if tb:
        ob = off // tn
        b_spec = pl.BlockSpec((tn, tk), lambda i, j, k: (j + ob, k))
    else:
        ob = off // tk
        b_spec = pl.BlockSpec((tk, tn), lambda i, j, k: (k + ob, j))
    o_spec = pl.BlockSpec((tm, tn), lambda i, j, k: (i, j))
    in_specs = [a_spec, b_spec]
    operands = [a, b]
    if add is not None:
        in_specs.append(o_spec)
        operands.append(add)
    a_dims = (0,) if ta else (1,)
    b_dims = (1,) if tb else (0,)

    def body(*refs):
        if add is not None:
            a_ref, b_ref, add_ref, o_ref, acc_ref = refs
        else:
            a_ref, b_ref, o_ref, acc_ref = refs
        k = pl.program_id(2)

        @pl.when(k == 0)
        def _():
            acc_ref[...] = jnp.zeros_like(acc_ref)

        acc_ref[...] += _dot(a_ref[...], b_ref[...], (a_dims, b_dims))

        @pl.when(k == gk - 1)
        def _():
            r = acc_ref[...]
            if add is not None:
                r = r + add_ref[...].astype(F32)
            o_ref[...] = r.astype(o_ref.dtype)

    est = 2 * (tm * tk * a.dtype.itemsize + tk * tn * b.dtype.itemsize + tm * tn * jnp.dtype(out_dtype).itemsize)
    est += tm * tn * 4 * (3 if add is not None else 1) + 2 * (tm * tk + tk * tn) * 2
    return pl.pallas_call(
        body,
        name=name,
        grid=(gm, gn, gk),
        in_specs=in_specs,
        out_specs=o_spec,
        out_shape=jax.ShapeDtypeStruct((m, n), out_dtype),
        scratch_shapes=[pltpu.VMEM((tm, tn), F32)],
        compiler_params=pltpu.CompilerParams(
            dimension_semantics=("parallel", "parallel", "arbitrary"), vmem_limit_bytes=_vmem(est)
        ),
    )(*operands)


def _rowmap(fn, rows, vecs, outs, reds, *, name, tm=256):
    rows = [r if isinstance(r, tuple) else (r, r.shape[1], 0) for r in rows]
    t = rows[0][0].shape[0]
    tm = _pick(t, (tm, 128, 64, 32, 16, 8))
    nr, nv, no = len(rows), len(vecs), len(outs)

    def body(*refs):
        row_refs, vec_refs = refs[:nr], refs[nr:nr + nv]
        out_refs, red_refs = refs[nr + nv:nr + nv + no], refs[nr + nv + no:]
        o, rd = fn([r[...] for r in row_refs], [v[...] for v in vec_refs])
        for ref, val in zip(out_refs, o):
            ref[...] = val.astype(ref.dtype)
        if red_refs:
            @pl.when(pl.program_id(0) == 0)
            def _():
                for ref in red_refs:
                    ref[...] = jnp.zeros_like(ref)

            for ref, val in zip(red_refs, rd):
                ref[...] += val

    in_specs = [pl.BlockSpec((tm, w), functools.partial(lambda cb, i: (i, cb), cb)) for (_, w, cb) in rows]
    in_specs += [pl.BlockSpec(v.shape, functools.partial(lambda nd, i: (0,) * nd, v.ndim)) for v in vecs]
    out_specs = [pl.BlockSpec((tm, w), lambda i: (i, 0)) for (w, _) in outs]
    out_specs += [pl.BlockSpec((1, w), lambda i: (0, 0)) for w in reds]
    out_shape = [jax.ShapeDtypeStruct((t, w), d) for (w, d) in outs]
    out_shape += [jax.ShapeDtypeStruct((1, w), F32) for w in reds]
    est = 2 * sum(tm * w * a.dtype.itemsize for (a, w, _) in rows) + 2 * sum(_nbytes(v.shape, v.dtype) for v in vecs)
    est += 2 * sum(tm * w * jnp.dtype(d).itemsize for (w, d) in outs)
    est += 4 * tm * max([w for (_, w, _) in rows] + [w for (w, _) in outs]) * 4
    return pl.pallas_call(
        body,
        name=name,
        grid=(t // tm,),
        in_specs=in_specs,
        out_specs=out_specs,
        out_shape=out_shape,
        compiler_params=pltpu.CompilerParams(dimension_semantics=("arbitrary",), vmem_limit_bytes=_vmem(est)),
    )(*[r[0] for r in rows], *vecs)


def _colsum(v):
    return jnp.sum(v, axis=0, keepdims=True)


def _head_ones():
    i = np.arange(LANES)
    return jnp.asarray((i[:, None] // HEAD_DIM) == (i[None, :] // HEAD_DIM), MXU_DTYPE)


def _headsum(y, ones):
    parts = []
    for j in range(y.shape[1] // LANES):
        c = y[:, j * LANES:(j + 1) * LANES]
        hi = c.astype(MXU_DTYPE)
        lo = c - hi.astype(F32)
        parts.append(_dot(hi, ones) + _dot(lo, ones))
    return parts[0] if len(parts) == 1 else jnp.concatenate(parts, axis=1)


def _rot(y, c, s_lo, s_hi):
    parts = []
    for j in range(y.shape[1] // LANES):
        yc = y[:, j * LANES:(j + 1) * LANES]
        parts.append(yc * c + pltpu.roll(yc, LANES - ROT_DIM // 2, 1) * s_lo + pltpu.roll(yc, ROT_DIM // 2, 1) * s_hi)
    return parts[0] if len(parts) == 1 else jnp.concatenate(parts, axis=1)


def _rot_tables(t):
    half = ROT_DIM // 2
    inv_freq = ROPE_THETA ** (-(jnp.arange(half, dtype=F32) * 2.0) / ROT_DIM)
    ang = jnp.arange(t).astype(F32)[:, None] * inv_freq[None, :]
    cos, sin = jnp.cos(ang), jnp.sin(ang)
    z = lambda w: jnp.zeros((t, w), F32)
    c = jnp.concatenate([cos, cos, jnp.ones((t, HEAD_DIM - ROT_DIM), F32)], axis=1)
    s_lo = jnp.concatenate([-sin, z(HEAD_DIM - half)], axis=1)
    s_hi = jnp.concatenate([z(half), sin, z(HEAD_DIM - ROT_DIM)], axis=1)
    return [jnp.tile(a, (1, LANES // HEAD_DIM)) for a in (c, s_lo, s_hi)]


def _rms(x):
    return lax.rsqrt(jnp.mean(x * x, axis=-1, keepdims=True) + EPS)


def _window(kind, n, bq, t, seg):
    if kind == "na":
        rows = t // GRID_W
        rs = jnp.clip(n - NA_KH // 2, 0, rows - NA_KH)
        return rs
    nk = bq + 2 * DIL_HALF
    return jnp.clip(n * bq - DIL_HALF, 0, t - nk)


def _dil_mask(n, bq, nk, ws, seg):
    qi = n * bq + lax.broadcasted_iota(jnp.int32, (bq, nk), 0)
    ki = ws + lax.broadcasted_iota(jnp.int32, (bq, nk), 1)
    shift = int(np.log2(seg))
    return (jnp.abs(ki - qi) <= DIL_HALF) & ((ki >> shift) == (qi >> shift))


HS = 4
QW = HS * HEAD_DIM


def _head_of_lane(width=QW):
    return lax.broadcasted_iota(jnp.int32, (1, width), 1) // HEAD_DIM


def _stack_heads(a):
    head = _head_of_lane()
    return jnp.concatenate([jnp.where(head == e, a, jnp.zeros_like(a)) for e in range(HS)], axis=0)


def _unstack_heads(a, bq):
    head = _head_of_lane()
    out = jnp.zeros((bq, QW), a.dtype)
    for e in range(HS):
        out = jnp.where(head == e, a[e * bq:(e + 1) * bq], out)
    return out


def _stack_cols(blk, bq):
    head = _head_of_lane()
    return jnp.concatenate(
        [jnp.max(jnp.where(head == e, blk, -jnp.inf), axis=1, keepdims=True) for e in range(HS)], axis=0)


def _attn_geometry(kind):
    if kind == "na":
        return GRID_W, NA_KH * GRID_W
    bq = 128
    return bq, bq + 2 * DIL_HALF


def _attn_scores(kind, n, bq, nk, t, seg, qs, k_ref, b_ref):
    scale = HEAD_DIM ** -0.5
    if kind == "na":
        rs = _window(kind, n, bq, t, seg)
        ws = pl.multiple_of(rs * GRID_W, GRID_W)
        ro0 = rs - n + (NA_KH - 1)
        s = _dot_nt(qs, k_ref[pl.ds(ws, nk), :]) * scale
        s = s + jnp.concatenate(
            [jnp.concatenate([b_ref[e, ro0 + 2 * i] for i in range(NA_KH // 2)], axis=1) for e in range(HS)], axis=0)
        return s, ws, ro0
    ws = pl.multiple_of(_window(kind, n, bq, t, seg), DIL_HALF)
    mask = _dil_mask(n, bq, nk, ws, seg)
    s = _dot_nt(qs, k_ref[pl.ds(ws, nk), :]) * scale
    s = jnp.where(jnp.concatenate([mask] * HS, axis=0), s, NEG_INF)
    return s, ws, None


def _attn_fwd(q, k, v, *, kind, name, bias=None, seg=None):
    t, w = q.shape
    quads = w // QW
    bq, nk = _attn_geometry(kind)
    nq = t // bq

    def body(*refs):
        if kind == "na":
            q_ref, k_ref, v_ref, b_ref, o_ref, l_ref = refs
        else:
            (q_ref, k_ref, v_ref, o_ref, l_ref), b_ref = refs, None
        n = pl.program_id(1)
        s, ws, _ = _attn_scores(kind, n, bq, nk, t, seg, _stack_heads(q_ref[...]), k_ref, b_ref)
        m = jnp.max(s, axis=1, keepdims=True)
        p = jnp.exp(s - m)
        l = jnp.sum(p, axis=1, keepdims=True)
        o_ref[...] = _unstack_heads(_dot(p / l, v_ref[pl.ds(ws, nk), :]), bq)
        l_ref[...] = _unstack_heads(jnp.broadcast_to(m + jnp.log(l), (HS * bq, QW)), bq)

    blk = pl.BlockSpec((bq, QW), lambda j, n: (n, j))
    res = pl.BlockSpec((t, QW), lambda j, n: (0, j))
    in_specs = [blk, res, res]
    operands = [q, k, v]
    est = 4 * t * QW * q.dtype.itemsize + 12 * HS * bq * nk * 4
    if kind == "na":
        in_specs.append(pl.BlockSpec((HS,) + bias.shape[1:], lambda j, n: (j, 0, 0, 0)))
        operands.append(bias)
        est += 2 * _nbytes((HS,) + bias.shape[1:], F32)
    return pl.pallas_call(
        body,
        name=name,
        grid=(quads, nq),
        in_specs=in_specs,
        out_specs=[blk, blk],
        out_shape=[jax.ShapeDtypeStruct((t, w), F32)] * 2,
        compiler_params=pltpu.CompilerParams(dimension_semantics=("arbitrary", "arbitrary"), vmem_limit_bytes=_vmem(est)),
    )(*operands)


def _attn_bwd(q, k, v, do, dterm, lse, *, kind, name, bias=None, seg=None):
    t, w = q.shape
    quads = w // QW
    bq, nk = _attn_geometry(kind)
    nq = t // bq
    scale = HEAD_DIM ** -0.5

    def body(*refs):
        if kind == "na":
            q_ref, k_ref, v_ref, do_ref, dt_ref, l_ref, b_ref, dq_ref, dk_hbm, dv_hbm, db_ref, dk_acc, dv_acc, sem = refs
        else:
            q_ref, k_ref, v_ref, do_ref, dt_ref, l_ref, dq_ref, dk_hbm, dv_hbm, dk_acc, dv_acc, sem = refs
            b_ref = None
        j, n = pl.program_id(0), pl.program_id(1)

        @pl.when(n == 0)
        def _():
            dk_acc[...] = jnp.zeros_like(dk_acc)
            dv_acc[...] = jnp.zeros_like(dv_acc)
            if kind == "na":
                db_ref[...] = jnp.zeros_like(db_ref)

        qs = _stack_heads(q_ref[...])
        dos = _stack_heads(do_ref[...])
        s, ws, ro0 = _attn_scores(kind, n, bq, nk, t, seg, qs, k_ref, b_ref)
        p = jnp.exp(s - _stack_cols(l_ref[...], bq))
        dp = _dot_nt(dos, v_ref[pl.ds(ws, nk), :])
        ds = p * (dp - _stack_cols(dt_ref[...], bq))
        if kind == "na":
            for e in range(HS):
                for i in range(NA_KH // 2):
                    db_ref[e, ro0 + 2 * i] += ds[e * bq:(e + 1) * bq, i * LANES:(i + 1) * LANES]
        dsc = ds * scale
        dq_ref[...] = _unstack_heads(_dot(dsc, k_ref[pl.ds(ws, nk), :]), bq)
        dk_acc[pl.ds(ws, nk), :] += _dot_tn(dsc, qs)
        dv_acc[pl.ds(ws, nk), :] += _dot_tn(p, dos)

        @pl.when(n == nq - 1)
        def _():
            ck = pltpu.make_async_copy(dk_acc, dk_hbm.at[j], sem.at[0])
            cv = pltpu.make_async_copy(dv_acc, dv_hbm.at[j], sem.at[1])
            ck.start()
            cv.start()
            ck.wait()
            cv.wait()

    blk = pl.BlockSpec((bq, QW), lambda j, n: (n, j))
    res = pl.BlockSpec((t, QW), lambda j, n: (0, j))
    in_specs = [blk, res, res, blk, blk, blk]
    operands = [q, k, v, do, dterm, lse]
    out_specs = [blk, ANY, ANY]
    out_shape = [jax.ShapeDtypeStruct((t, w), F32)] + [jax.ShapeDtypeStruct((quads, t, QW), F32)] * 2
    est = 4 * t * QW * q.dtype.itemsize + 2 * t * QW * 4 + 16 * HS * bq * nk * 4
    if kind == "na":
        bspec = pl.BlockSpec((HS,) + bias.shape[1:], lambda j, n: (j, 0, 0, 0))
        in_specs.append(bspec)
        operands.append(bias)
        out_specs.append(bspec)
        out_shape.append(jax.ShapeDtypeStruct(bias.shape, F32))
        est += 4 * _nbytes((HS,) + bias.shape[1:], F32)
    res_ = pl.pallas_call(
        body,
        name=name,
        grid=(quads, nq),
        in_specs=in_specs,
        out_specs=out_specs,
        out_shape=out_shape,
        scratch_shapes=[pltpu.VMEM((t, QW), F32), pltpu.VMEM((t, QW), F32), pltpu.SemaphoreType.DMA((2,))],
        compiler_params=pltpu.CompilerParams(dimension_semantics=("arbitrary", "arbitrary"), vmem_limit_bytes=_vmem(est)),
    )(*operands)
    unquad = lambda a: [a[i] for i in range(quads)]
    return (res_[0], unquad(res_[1]), unquad(res_[2])) + tuple(res_[3:])


def _na_onehot():
    qc = np.arange(GRID_W)[:, None]
    kc = np.arange(GRID_W)[None, :]
    start = np.clip(qc - NA_KW // 2, 0, GRID_W - NA_KW)
    inwin = (kc >= start) & (kc < start + NA_KW)
    off = kc - qc + (NA_KW - 1)
    e_mat = np.zeros((2, 32, GRID_W, 2, GRID_W), np.float32)
    for e in range(2):
        for c in range(2 * NA_KW - 1):
            e_mat[e, c, :, e, :] = (off == c) & inwin
    neg = np.where(inwin, 0.0, NEG_INF).astype(np.float32)
    neg = np.broadcast_to(neg[:, None, :], (GRID_W, 2, GRID_W)).reshape(1, GRID_W * LANES)
    return jnp.asarray(e_mat.reshape(64, GRID_W * LANES), MXU_DTYPE), jnp.asarray(neg)


def _na_rowpairs(rpb):
    p = jnp.pad(rpb, ((0, 0), (0, 0), (0, 1)))
    return jnp.concatenate([p[:, :-1], p[:, 1:]], axis=-1).reshape(NA_HEADS * (2 * NA_KH - 2), 64)


def _na_bias_table(rpb):
    r2 = _na_rowpairs(rpb)
    e_mat, neg = _na_onehot()

    def body(r_ref, e_ref, n_ref, o_ref):
        hi, mid, lo = _split3(r_ref[...])
        e = e_ref[...]
        o_ref[...] = _dot(hi, e) + _dot(mid, e) + _dot(lo, e) + n_ref[...]

    out = pl.pallas_call(
        body,
        name="na_bias_table",
        out_shape=jax.ShapeDtypeStruct((r2.shape[0], GRID_W * LANES), F32),
        compiler_params=pltpu.CompilerParams(vmem_limit_bytes=_vmem(6 * r2.shape[0] * GRID_W * LANES * 4)),
    )(r2, e_mat, neg)
    return out.reshape(NA_HEADS, 2 * NA_KH - 2, GRID_W, LANES)


def _na_bias_grad(dbt):
    e_mat, _ = _na_onehot()
    flat = dbt.reshape(NA_HEADS * (2 * NA_KH - 2), GRID_W * LANES)

    def body(d_ref, e_ref, o_ref):
        hi, mid, lo = _split3(d_ref[...])
        e = e_ref[...]
        o_ref[...] = _dot_nt(hi, e) + _dot_nt(mid, e) + _dot_nt(lo, e)

    g = pl.pallas_call(
        body,
        name="na_bias_grad",
        out_shape=jax.ShapeDtypeStruct((flat.shape[0], 64), F32),
        compiler_params=pltpu.CompilerParams(vmem_limit_bytes=_vmem(6 * flat.shape[0] * GRID_W * LANES * 4)),
    )(flat, e_mat)
    g = g.reshape(NA_HEADS, 2 * NA_KH - 2, 2, 32)[..., :2 * NA_KW - 1]
    first = jnp.pad(g[:, :, 0], ((0, 0), (0, 1), (0, 0)))
    second = jnp.pad(g[:, :, 1], ((0, 0), (1, 0), (0, 0)))
    return first + second


def _all_gather(arrs, *, name):
    na = len(arrs)

    def body(*refs):
        ins, outs = refs[:na], refs[na:2 * na]
        send_sems, recv_sems, local_sems = refs[2 * na:]
        x, y, c = lax.axis_index("x"), lax.axis_index("y"), lax.axis_index("c")
        me, sibling = (x, y, c), (x, y, 1 - c)
        chips = [(1 - x, y), (x, 1 - y), (1 - x, 1 - y)]

        def rows(a, px, py, pc):
            r = ins[a].shape[0]
            return outs[a].at[pl.ds((4 * px + 2 * py + pc) * r, r), :]

        def copy(a, k, block, to, src=None):
            return pltpu.make_async_remote_copy(
                src_ref=rows(a, *block) if src is None else src, dst_ref=rows(a, *block),
                send_sem=send_sems.at[a, k], recv_sem=recv_sems.at[a, k], device_id=to, device_id_type=MESH)

        mine = [pltpu.make_async_copy(ins[a], rows(a, *me), local_sems.at[a]) for a in range(na)]
        for cp in mine:
            cp.start()
        first = []
        for a in range(na):
            first.append(copy(a, 0, me, sibling, src=ins[a]))
            first += [copy(a, 1 + j, me, (*chip, c), src=ins[a]) for j, chip in enumerate(chips)]
        for cp in first:
            cp.start()
        passed = []
        for j, chip in enumerate(chips):
            for a in range(na):
                copy(a, 1 + j, (*chip, c), me).wait_recv()
                cp = copy(a, 4 + j, (*chip, c), sibling)
                cp.start()
                passed.append(cp)
        for a in range(na):
            copy(a, 0, sibling, me).wait_recv()
        for j, chip in enumerate(chips):
            for a in range(na):
                copy(a, 4 + j, (*chip, 1 - c), me).wait_recv()
        for cp in first + passed:
            cp.wait_send()
        for cp in mine:
            cp.wait()

    return pl.pallas_call(
        body,
        name=name,
        in_specs=[ANY] * na,
        out_specs=[ANY] * na,
        out_shape=[jax.ShapeDtypeStruct((N_DEV * a.shape[0], a.shape[1]), a.dtype) for a in arrs],
        scratch_shapes=[pltpu.SemaphoreType.DMA((na, 7)), pltpu.SemaphoreType.DMA((na, 7)), pltpu.SemaphoreType.DMA((na,))],
    )(*arrs)


HBM = pl.BlockSpec(memory_space=pltpu.HBM)
SEM = pl.BlockSpec(memory_space=pltpu.SEMAPHORE)
EFFECT = pltpu.SideEffectType.DATAFLOW_SIDE_EFFECTING


def _peer_of(k):
    x, y, c = lax.axis_index("x"), lax.axis_index("y"), lax.axis_index("c")
    return x ^ ((k >> 2) & 1), y ^ ((k >> 1) & 1), c ^ (k & 1)


def _split_copies(gather, src_ref, land_ref, send_sems, recv_sems):
    x, y, c = lax.axis_index("x"), lax.axis_index("y"), lax.axis_index("c")
    my = 4 * x + 2 * y + c
    r = src_ref.shape[0] if gather else src_ref.shape[0] // N_DEV
    copies = []
    for k in range(1, N_DEV):
        px, py, pc = _peer_of(k)
        if gather:
            src, dst = src_ref, land_ref.at[pl.ds(my * r, r), :]
        else:
            src, dst = src_ref.at[pl.ds((4 * px + 2 * py + pc) * r, r), :], land_ref.at[k - 1]
        copies.append(pltpu.make_async_remote_copy(
            src_ref=src, dst_ref=dst, send_sem=send_sems.at[k - 1], recv_sem=recv_sems.at[k - 1],
            device_id=(px, py, pc), device_id_type=MESH))
    return copies


def _split_start(srcs, lands, *, gather, name):
    na = len(srcs)

    def body(*refs):
        src_refs, land_refs = refs[:na], refs[na:2 * na]
        outs = refs[2 * na:]
        for a in range(na):
            for cp in _split_copies(gather, src_refs[a], land_refs[a], outs[4 * a], outs[4 * a + 1]):
                cp.start()
        outs[4 * na][...] = jnp.zeros_like(outs[4 * na])

    out_shape, out_specs, aliases = [], [], {}
    for a in range(na):
        out_shape += [pltpu.SemaphoreType.DMA((N_DEV - 1,)), pltpu.SemaphoreType.DMA((N_DEV - 1,)),
                      pltpu.HBM(srcs[a].shape, srcs[a].dtype), pltpu.HBM(lands[a].shape, lands[a].dtype)]
        out_specs += [SEM, SEM, HBM, HBM]
        aliases[a] = 4 * a + 2
        aliases[na + a] = 4 * a + 3
    out_shape.append(jax.ShapeDtypeStruct((8, LANES), F32))
    out_specs.append(pl.BlockSpec(memory_space=pltpu.VMEM))
    res = pl.pallas_call(
        body,
        name=name,
        out_shape=tuple(out_shape),
        in_specs=[HBM] * (2 * na),
        out_specs=tuple(out_specs),
        input_output_aliases=aliases,
        compiler_params=pltpu.CompilerParams(has_side_effects=EFFECT),
    )(*[pltpu.with_memory_space_constraint(a, pltpu.HBM) for a in list(srcs) + list(lands)])
    return [tuple(res[4 * a:4 * a + 4]) for a in range(na)], res[4 * na][0, 0]


def _split_wait(handles, after, *, gather, name):
    na = len(handles)

    def body(*refs):
        src_refs, land_refs = refs[:na], refs[na:2 * na]
        sems = refs[2 * na:4 * na]
        for a in range(na):
            for cp in _split_copies(gather, src_refs[a], land_refs[a], sems[2 * a], sems[2 * a + 1]):
                cp.wait_send()
                cp.wait_recv()

    srcs = [h[2] for h in handles]
    lands = [h[3] for h in handles]
    sems = [s for h in handles for s in h[:2]]
    res = pl.pallas_call(
        body,
        name=name,
        out_shape=tuple(pltpu.HBM(a.shape, a.dtype) for a in srcs + lands),
        in_specs=[HBM] * (2 * na) + [SEM] * (2 * na) + [ANY],
        out_specs=tuple([HBM] * (2 * na)),
        input_output_aliases={i: i for i in range(2 * na)},
        compiler_params=pltpu.CompilerParams(has_side_effects=EFFECT),
    )(*srcs, *lands, *sems, after)
    return list(res[:na]), list(res[na:])


def _sum8(own, recv, *, name):
    _, r, w = recv.shape
    tr = _pick(r, (256, 128, 64, 32, 16, 8))

    def body(own_ref, a_ref, o_ref):
        acc = own_ref[...].astype(F32)
        for i in range(N_DEV - 1):
            acc = acc + a_ref[i].astype(F32)
        o_ref[...] = acc

    return pl.pallas_call(
        body,
        name=name,
        grid=(r // tr,),
        in_specs=[pl.BlockSpec((tr, w), lambda i: (i, 0)), pl.BlockSpec((N_DEV - 1, tr, w), lambda i: (0, i, 0))],
        out_specs=pl.BlockSpec((tr, w), lambda i: (i, 0)),
        out_shape=jax.ShapeDtypeStruct((r, w), F32),
        compiler_params=pltpu.CompilerParams(dimension_semantics=("parallel",), vmem_limit_bytes=_vmem(4 * N_DEV * tr * w * 4)),
    )(own, recv)


def _adamw(w, g, m, v, *, name):
    def fn(rows, _):
        wv, gv, mv, vv = rows
        m1 = ADAM_B1 * mv + (1.0 - ADAM_B1) * gv
        v1 = ADAM_B2 * vv + (1.0 - ADAM_B2) * jnp.square(gv)
        m_hat = m1 / (1.0 - ADAM_B1 ** ADAM_STEP)
        v_hat = v1 / (1.0 - ADAM_B2 ** ADAM_STEP)
        delta = -ADAM_LR * (m_hat / (jnp.sqrt(v_hat) + ADAM_EPS) + ADAM_WD * wv)
        return [delta, m1, v1], []

    c = w.shape[1]
    return _rowmap(fn, [w, g, m, v], [], [(c, F32)] * 3, [], name=name, tm=128)


_SMALL = ("b_ada", "g_norm1", "g_norm2", "b_gate", "g_qa", "g_ka", "g_qb", "g_kb", "rpb", "loss")


def _pack_small(parts):
    flat = []
    for nme in _SMALL:
        a = parts[nme].reshape(-1).astype(F32)
        flat.append(jnp.pad(a, (0, (-a.shape[0]) % LANES)))
    flat = jnp.concatenate(flat)
    flat = jnp.pad(flat, (0, (-flat.shape[0]) % (8 * LANES)))
    return flat.reshape(-1, LANES)


def _unpack_small(packed, shapes):
    flat = packed.reshape(-1)
    out, pos = {}, 0
    for nme in _SMALL:
        n = int(np.prod(shapes[nme]))
        out[nme] = flat[pos:pos + n].reshape(shapes[nme])
        pos += n + (-n) % LANES
    return out


def _to_class(a, d):
    t, w = a.shape
    return a if d == 1 else a.reshape(t // d, d, w).transpose(1, 0, 2).reshape(t, w)


def _from_class(a, d):
    t, w = a.shape
    return a if d == 1 else a.reshape(d, t // d, w).transpose(1, 0, 2).reshape(t, w)


def kernel(x, c, w_ada, b_ada, g_norm1, g_norm2, w_in, b_gate, g_qa, g_ka, g_qb, g_kb, rpb, w_proj_a, w_proj_b, w_o, w_ffn_in, w_ffn_out, loss_target, m_w_ada, m_b_ada, m_g_norm1, m_g_norm2, m_w_in, m_b_gate, m_g_qa, m_g_ka, m_g_qb, m_g_kb, m_rpb, m_w_proj_a, m_w_proj_b, m_w_o, m_w_ffn_in, m_w_ffn_out, v_w_ada, v_b_ada, v_g_norm1, v_g_norm2, v_w_in, v_b_gate, v_g_qa, v_g_ka, v_g_qb, v_g_kb, v_rpb, v_w_proj_a, v_w_proj_b, v_w_o, v_w_ffn_in, v_w_ffn_out):
    t, d = x.shape[1], x.shape[2]
    d_ff = w_ffn_out.shape[1] * N_DEV
    me = 4 * lax.axis_index("x") + 2 * lax.axis_index("y") + lax.axis_index("c")
    xt, tgt = x[0], loss_target[0]
    ones = _head_ones()

    shards = [s.astype(WIRE_DTYPE) for s in (w_in[0].T, w_ffn_in[0].T, w_proj_a[0].T, w_proj_b[0].T, w_o[0], w_ffn_out[0])]
    lands = [lax.dynamic_update_slice(jnp.zeros((N_DEV * s.shape[0], s.shape[1]), s.dtype), s, (me * s.shape[0], 0))
             for s in shards]
    w_handles, w_token = _split_start(shards, lands, gather=True, name="gather_weights_start")

    c_all = _all_gather([jnp.pad(c + w_token, ((0, 7), (0, 0)))], name="gather_c")[0][::8]
    c_all = jnp.pad(c_all, ((0, LANES - N_DEV), (0, 0)))

    def mod_body(c_ref, w_ref, b_ref, o_ref, act_ref):
        act = _silu(c_ref[...])
        act_ref[...] = act
        hi, mid, lo = _split3(act)
        w = w_ref[...]
        whi, wmid, wlo = _split3(w)
        acc = _dot(hi, whi) + (_dot(hi, wmid) + _dot(mid, whi)) + (_dot(hi, wlo) + _dot(mid, wmid) + _dot(lo, whi))
        o_ref[...] = acc + b_ref[...]

    ncol = w_ada.shape[2]
    b_ada_mine = lax.dynamic_slice(b_ada, (0, me * ncol), (1, ncol))
    mod_part, c_act = pl.pallas_call(
        mod_body,
        name="ada_mod",
        out_shape=[jax.ShapeDtypeStruct((LANES, ncol), F32), jax.ShapeDtypeStruct((LANES, d), F32)],
        compiler_params=pltpu.CompilerParams(vmem_limit_bytes=_vmem(6 * d * ncol * 4)),
    )(c_all, w_ada[0], b_ada_mine)
    mod_all = _all_gather([mod_part[:N_DEV]], name="gather_mod")[0].reshape(N_DEV, N_DEV, ncol)
    mod = lax.dynamic_index_in_dim(mod_all, me, axis=1, keepdims=False).reshape(6, d)
    sh1, sc1, gt1, sh2, sc2, gt2 = [mod[i:i + 1] for i in range(6)]

    def norm_fwd(rows, vecs):
        (xv,), (g, sc, sh) = rows, vecs
        return [xv * _rms(xv) * g * (1.0 + sc) + sh], []

    (h,) = _rowmap(norm_fwd, [xt], [g_norm1, sc1, sh1], [(d, MXU_DTYPE)], [], name="norm1")
    n_a, n_b = 3 * WA, 3 * WB
    (w_in_t,) = _split_wait(w_handles[:1], h, gather=True, name="gather_w_in_wait")[1]
    qkv_a = _mm(h, w_in_t, tb=True, b_rows=(0, n_a), name="proj_a")
    qkv_b = _mm(h, w_in_t, tb=True, b_rows=(n_a, n_b), name="proj_b")
    gates = _mm(h, w_in_t, tb=True, b_rows=(n_a + n_b, 2 * d), name="proj_gates")

    rot_c, rot_lo, rot_hi = _rot_tables(t)
    tile_g = lambda g, heads: jnp.tile(g, (1, heads))

    def qk_fwd(width, rotate):
        def fn(rows, vecs):
            xv = rows[0]
            gq, gk, on = vecs
            outs = []
            for i, g in enumerate((gq, gk)):
                xi = xv[:, i * width:(i + 1) * width]
                r = lax.rsqrt(_headsum(xi * xi, on) * (1.0 / HEAD_DIM) + EPS)
                yi = xi * r * g
                if rotate:
                    yi = _rot(yi, rows[1], rows[2], rows[3])
                outs.append(yi)
            outs.append(xv[:, 2 * width:])
            return outs, []
        return fn

    qa, ka, va = _rowmap(qk_fwd(WA, False), [qkv_a], [tile_g(g_qa, NA_HEADS), tile_g(g_ka, NA_HEADS), ones],
                         [(WA, MXU_DTYPE)] * 3, [], name="qknorm_a")
    qb, kb, vb = _rowmap(qk_fwd(WB, True), [qkv_b, rot_c, rot_lo, rot_hi],
                         [tile_g(g_qb, DIL_HEADS), tile_g(g_kb, DIL_HEADS), ones], [(WB, MXU_DTYPE)] * 3, [], name="qknorm_b")

    bias_tab = _na_bias_table(rpb[0])
    o_a, lse_a = _attn_fwd(qa, ka, va, kind="na", bias=bias_tab, name="na_fwd")

    grp = []
    for g, (_, dil) in enumerate(DIL_CONFIGS):
        sl = slice(g * WB_OUT, (g + 1) * WB_OUT)
        qg, kg, vg = [_to_class(a[:, sl], dil) for a in (qb, kb, vb)]
        og, lg = _attn_fwd(qg, kg, vg, kind="dil", seg=t // dil, name=f"dil_fwd{g}")
        grp.append(dict(q=qg, k=kg, v=vg, o=_from_class(og, dil), lse=_from_class(lg, dil), lse_c=lg, dil=dil))

    def merge_fwd(rows, _):
        o0, o1, o2, l0, l1, l2 = rows
        mx = jnp.maximum(jnp.maximum(l0, l1), l2)
        e0, e1, e2 = jnp.exp(l0 - mx), jnp.exp(l1 - mx), jnp.exp(l2 - mx)
        s = e0 + e1 + e2
        return [(e0 / s) * o0 + (e1 / s) * o1 + (e2 / s) * o2], []

    (o_b,) = _rowmap(merge_fwd, [gr["o"] for gr in grp] + [gr["lse"] for gr in grp], [], [(WB_OUT, F32)], [], name="dil_merge")

    w_pa_t, w_pb_t, w_o_f = _split_wait(w_handles[2:5], o_b, gather=True, name="gather_w_out_wait")[1]
    pa = _mm(o_a, w_pa_t, tb=True, name="proj_out_a")
    pb = _mm(o_b, w_pb_t, tb=True, name="proj_out_b")

    def gate_fwd(rows, vecs):
        gv, pav, pbv = rows
        sg = jax.nn.sigmoid(gv + vecs[0])
        return [sg[:, :d] * pav + sg[:, d:] * pbv], []

    (merged,) = _rowmap(gate_fwd, [gates, pa, pb], [b_gate], [(d, MXU_DTYPE)], [], name="gate_merge")
    att = _mm(merged, w_o_f, name="proj_o")

    def resid_norm(rows, vecs):
        xv, av = rows
        gt, g, sc, sh = vecs
        x1v = xv + gt * av
        return [x1v, x1v * _rms(x1v) * g * (1.0 + sc) + sh], []

    x1, h2 = _rowmap(resid_norm, [xt, att], [gt1, g_norm2, sc2, sh2], [(d, F32), (d, MXU_DTYPE)], [], name="resid_norm2")

    w_ffn_in_t, w_ffn_out_f = _split_wait([w_handles[1], w_handles[5]], h2, gather=True, name="gather_w_ffn_wait")[1]
    u = _mm(h2, w_ffn_in_t, tb=True, name="ffn_in")

    def swiglu_fwd(rows, _):
        uv = rows[0]
        return [_silu(uv[:, :d_ff]) * uv[:, d_ff:]], []

    (f,) = _rowmap(swiglu_fwd, [u], [], [(d_ff, MXU_DTYPE)], [], name="swiglu")
    y2 = _mm(f, w_ffn_out_f, name="ffn_out")

    def loss_fn(rows, vecs):
        yv, x1v, tv = rows
        gt = vecs[0]
        err = x1v + gt * yv - tv
        dout = err * (1.0 / d)
        return [dout, dout * gt], [_colsum(err * err), _colsum(dout * yv)]

    dout, dy2, err2, dgt2 = _rowmap(loss_fn, [y2, x1, tgt], [gt2], [(d, F32), (d, MXU_DTYPE)], [d, d], name="loss")

    dw_ffn_out = _mm(f, dy2, ta=True, out_dtype=WIRE_DTYPE, name="wgrad_ffn_out")
    df = _mm(dy2, w_ffn_out_f, tb=True, name="dgrad_ffn_out")

    def swiglu_bwd(rows, _):
        dfv, uv = rows
        a, up = uv[:, :d_ff], uv[:, d_ff:]
        sg = jax.nn.sigmoid(a)
        da = dfv * up * (sg * (1.0 + a * (1.0 - sg)))
        return [jnp.concatenate([da, dfv * (a * sg)], axis=1)], []

    (du,) = _rowmap(swiglu_bwd, [df, u], [], [(2 * d_ff, MXU_DTYPE)], [], name="swiglu_bwd", tm=128)
    dw_ffn_in_t = _mm(du, h2, ta=True, out_dtype=WIRE_DTYPE, name="wgrad_ffn_in")
    land7 = lambda a: jnp.zeros((N_DEV - 1, a.shape[0] // N_DEV, a.shape[1]), a.dtype)
    own_block = lambda a: lax.dynamic_slice(a, (me * (a.shape[0] // N_DEV), 0), (a.shape[0] // N_DEV, a.shape[1]))
    g_ffn = [dw_ffn_in_t, dw_ffn_out]
    h_ffn, tok_ffn = _split_start(g_ffn, [land7(a) for a in g_ffn], gather=False, name="exchange_ffn_start")
    dh2 = _mm(du, w_ffn_in_t, name="dgrad_ffn_in")

    def norm_bwd(dh, xv, g, sc):
        r = _rms(xv)
        xh = xv * r
        dxh = dh * g * (1.0 + sc)
        dxv = r * (dxh - xh * jnp.mean(dxh * xh, axis=-1, keepdims=True))
        return dxv, [_colsum(dh), _colsum(dh * xh * g), _colsum(dh * xh * (1.0 + sc))]

    def norm2_bwd(rows, vecs):
        dhv, x1v, dov, av = rows
        g, sc, gt = vecs
        dxv, sums = norm_bwd(dhv, x1v, g, sc)
        dx1v = dov + dxv
        return [dx1v, dx1v * gt], sums + [_colsum(dx1v * av)]

    dx1, datt, dsh2, dsc2, dg2, dgt1 = _rowmap(norm2_bwd, [dh2, x1, dout, att], [g_norm2 + tok_ffn, sc2, gt1],
                                                [(d, F32), (d, MXU_DTYPE)], [d] * 4, name="norm2_bwd")
    dw_o = _mm(merged, datt, ta=True, out_dtype=WIRE_DTYPE, name="wgrad_o")
    dmerged = _mm(datt, w_o_f, tb=True, name="dgrad_o")

    def gate_bwd(rows, vecs):
        dm, gv, pav, pbv = rows
        sg = jax.nn.sigmoid(gv + vecs[0])
        ga, gb = sg[:, :d], sg[:, d:]
        dgp = jnp.concatenate([dm * pav * ga * (1.0 - ga), dm * pbv * gb * (1.0 - gb)], axis=1)
        return [dm * ga, dm * gb, dgp], [_colsum(dgp)]

    dpa, dpb, dgates, db_gate = _rowmap(gate_bwd, [dmerged, gates, pa, pb], [b_gate],
                                        [(d, MXU_DTYPE), (d, MXU_DTYPE), (2 * d, MXU_DTYPE)], [2 * d], name="gate_bwd")
    dw_pa_t = _mm(dpa, o_a, ta=True, out_dtype=WIRE_DTYPE, name="wgrad_proj_a")
    dw_pb_t = _mm(dpb, o_b, ta=True, out_dtype=WIRE_DTYPE, name="wgrad_proj_b")
    g_out = [dw_pa_t, dw_pb_t, dw_o]
    h_out, tok_out = _split_start(g_out, [land7(a) for a in g_out], gather=False, name="exchange_out_start")
    do_a = _mm(dpa, w_pa_t, name="dgrad_proj_a")
    do_b = _mm(dpb, w_pb_t, name="dgrad_proj_b")

    def delta_a(rows, vecs):
        return [_headsum(rows[0] * rows[1], vecs[0])], []

    (dterm_a,) = _rowmap(delta_a, [do_a, o_a], [ones + tok_out.astype(ones.dtype)], [(WA, F32)], [], name="na_delta")
    dqa, dka, dva, dbias = _attn_bwd(qa, ka, va, do_a, dterm_a, lse_a, kind="na", bias=bias_tab, name="na_bwd")
    g_rpb = _na_bias_grad(dbias)

    def merge_bwd(rows, vecs):
        dob, o0, o1, o2, l0, l1, l2 = rows
        on = vecs[0]
        mx = jnp.maximum(jnp.maximum(l0, l1), l2)
        e0, e1, e2 = jnp.exp(l0 - mx), jnp.exp(l1 - mx), jnp.exp(l2 - mx)
        s = e0 + e1 + e2
        ws = [e0 / s, e1 / s, e2 / s]
        dws = [_headsum(dob * o, on) for o in (o0, o1, o2)]
        mean = ws[0] * dws[0] + ws[1] * dws[1] + ws[2] * dws[2]
        return [w * dob for w in ws] + [w * mean for w in ws], []

    mb = _rowmap(merge_bwd, [do_b] + [gr["o"] for gr in grp] + [gr["lse"] for gr in grp], [ones],
                 [(WB_OUT, F32)] * 6, [], name="dil_merge_bwd")
    dqb, dkb, dvb = [], [], []
    for g, gr in enumerate(grp):
        dil = gr["dil"]
        dq, dk, dv = _attn_bwd(gr["q"], gr["k"], gr["v"], _to_class(mb[g], dil), _to_class(mb[3 + g], dil), gr["lse_c"],
                               kind="dil", seg=t // dil, name=f"dil_bwd{g}")
        dqb.append(_from_class(dq, dil))
        dkb.append(_from_class(dk[0], dil))
        dvb.append(_from_class(dv[0], dil))

    def qk_bwd(width, rotate, nparts):
        def fn(rows, vecs):
            gq, gk, on = vecs
            xv = rows[0]
            pos = 1
            if rotate:
                rc, rlo, rhi = rows[1:4]
                pos = 4
            cat = lambda parts: parts[0] if len(parts) == 1 else jnp.concatenate(parts, axis=1)
            ends = np.cumsum((pos,) + nparts)
            dq, dk, dv = [cat(rows[ends[i]:ends[i + 1]]) for i in range(3)]
            outs, sums = [], []
            for i, (dy, g) in enumerate(((dq, gq), (dk, gk))):
                if rotate:
                    dy = _rot(dy, rc, -rlo, -rhi)
                xi = xv[:, i * width:(i + 1) * width]
                r = lax.rsqrt(_headsum(xi * xi, on) * (1.0 / HEAD_DIM) + EPS)
                xh = xi * r
                dxh = dy * g
                outs.append(r * (dxh - xh * (_headsum(dxh * xh, on) * (1.0 / HEAD_DIM))))
                sums.append(_colsum(dy * xh))
            return [jnp.concatenate(outs + [dv], axis=1)], sums
        return fn

    dqkv_a, dg_qa, dg_ka = _rowmap(qk_bwd(WA, False, (1, len(dka), len(dva))), [qkv_a, dqa] + dka + dva,
                                   [tile_g(g_qa, NA_HEADS), tile_g(g_ka, NA_HEADS), ones],
                                   [(3 * WA, MXU_DTYPE)], [WA, WA], name="qknorm_a_bwd", tm=128)
    dqkv_b, dg_qb, dg_kb = _rowmap(qk_bwd(WB, True, (3, 3, 3)), [qkv_b, rot_c, rot_lo, rot_hi] + dqb + dkb + dvb,
                                   [tile_g(g_qb, DIL_HEADS), tile_g(g_kb, DIL_HEADS), ones],
                                   [(3 * WB, MXU_DTYPE)], [WB, WB], name="qknorm_b_bwd", tm=128)

    dw_in_t = jnp.concatenate([
        _mm(dqkv_a, h, ta=True, out_dtype=WIRE_DTYPE, name="wgrad_in_a"),
        _mm(dqkv_b, h, ta=True, out_dtype=WIRE_DTYPE, name="wgrad_in_b"),
        _mm(dgates, h, ta=True, out_dtype=WIRE_DTYPE, name="wgrad_in_gates")], axis=0)
    h_in, tok_in = _split_start([dw_in_t], [land7(dw_in_t)], gather=False, name="exchange_in_start")
    dh = _mm(dqkv_a, w_in_t, b_rows=(0, n_a), name="dgrad_in_a")
    dh = _mm(dqkv_b, w_in_t, b_rows=(n_a, n_b), add=dh, name="dgrad_in_b")
    dh = _mm(dgates, w_in_t, b_rows=(n_a + n_b, 2 * d), add=dh, name="dgrad_in_gates")

    def norm1_bwd(rows, vecs):
        dhv, xv, dx1v = rows
        dxv, sums = norm_bwd(dhv, xv, vecs[0], vecs[1])
        return [dx1v + dxv], sums

    grad_x, dsh1, dsc1, dg1 = _rowmap(norm1_bwd, [dh, xt, dx1], [g_norm1 + tok_in, sc1], [(d, F32)], [d] * 3, name="norm1_bwd")

    heads_sum = lambda a, heads: a.reshape(heads, HEAD_DIM).sum(axis=0)
    dmod = jnp.concatenate([dsh1, dsc1, dgt1, dsh2, dsc2, dgt2], axis=1)
    local_small = _pack_small(dict(
        b_ada=dmod, g_norm1=dg1, g_norm2=dg2, b_gate=db_gate, g_qa=heads_sum(dg_qa, NA_HEADS),
        g_ka=heads_sum(dg_ka, NA_HEADS), g_qb=heads_sum(dg_qb, DIL_HEADS), g_kb=heads_sum(dg_kb, DIL_HEADS),
        rpb=g_rpb, loss=(0.5 / d) * jnp.sum(err2)))
    srows = local_small.shape[0]
    small_all = _all_gather([local_small], name="gather_small")[0].reshape(N_DEV, srows, LANES)
    small_sum = _sum8(small_all[0], small_all[1:], name="sum_small")
    small_shapes = dict(b_ada=b_ada.shape, g_norm1=g_norm1.shape, g_norm2=g_norm2.shape, b_gate=b_gate.shape,
                        g_qa=g_qa.shape, g_ka=g_ka.shape, g_qb=g_qb.shape, g_kb=g_kb.shape, rpb=rpb.shape, loss=())
    small_w = dict(b_ada=b_ada, g_norm1=g_norm1, g_norm2=g_norm2, b_gate=b_gate, g_qa=g_qa, g_ka=g_ka, g_qb=g_qb,
                   g_kb=g_kb, rpb=rpb, loss=jnp.zeros((), F32))
    small_m = dict(b_ada=m_b_ada, g_norm1=m_g_norm1, g_norm2=m_g_norm2, b_gate=m_b_gate, g_qa=m_g_qa, g_ka=m_g_ka,
                   g_qb=m_g_qb, g_kb=m_g_kb, rpb=m_rpb, loss=jnp.zeros((), F32))
    small_v = dict(b_ada=v_b_ada, g_norm1=v_g_norm1, g_norm2=v_g_norm2, b_gate=v_b_gate, g_qa=v_g_qa, g_ka=v_g_ka,
                   g_qb=v_g_qb, g_kb=v_g_kb, rpb=v_rpb, loss=jnp.zeros((), F32))
    s_delta, s_m, s_v = _adamw(_pack_small(small_w), small_sum, _pack_small(small_m), _pack_small(small_v), name="adamw_small")
    gs = _unpack_small(small_sum, small_shapes)
    ds_, ms_, vs_ = [_unpack_small(a, small_shapes) for a in (s_delta, s_m, s_v)]

    dmod_all = small_all[:, :6 * d // LANES].reshape(N_DEV, 6 * d)
    dmod_mine = jnp.pad(lax.dynamic_slice(dmod_all, (0, me * ncol), (N_DEV, ncol)), ((0, LANES - N_DEV), (0, 0)))

    def wada_body(c_ref, dm_ref, o_ref):
        chi, cmid, clo = _split3(c_ref[...])
        dhi, dmid, dlo = _split3(dm_ref[...])
        o_ref[...] = (_dot_tn(chi, dhi) + (_dot_tn(chi, dmid) + _dot_tn(cmid, dhi))
                      + (_dot_tn(chi, dlo) + _dot_tn(cmid, dmid) + _dot_tn(clo, dhi)))

    g_w_ada = pl.pallas_call(
        wada_body,
        name="wgrad_ada",
        out_shape=jax.ShapeDtypeStruct((d, ncol), F32),
        compiler_params=pltpu.CompilerParams(vmem_limit_bytes=_vmem(4 * d * ncol * 4)),
    )(c_act, dmod_mine)

    sent, recv = _split_wait(h_in + h_ffn + h_out, small_sum, gather=False, name="exchange_wait")
    names = ("w_in", "w_ffn_in", "w_ffn_out", "w_proj_a", "w_proj_b", "w_o")
    transposed = (True, True, False, True, True, False)
    big_g = {}
    for nme, own, r, tr in zip(names, sent, recv, transposed):
        s = _sum8(own_block(own), r, name=f"sum_{nme}")
        big_g[nme] = s.T if tr else s
    big_g["w_ada"] = g_w_ada
    big_w = dict(w_ada=w_ada, w_in=w_in, w_proj_a=w_proj_a, w_proj_b=w_proj_b, w_o=w_o, w_ffn_in=w_ffn_in, w_ffn_out=w_ffn_out)
    big_m = dict(w_ada=m_w_ada, w_in=m_w_in, w_proj_a=m_w_proj_a, w_proj_b=m_w_proj_b, w_o=m_w_o, w_ffn_in=m_w_ffn_in, w_ffn_out=m_w_ffn_out)
    big_v = dict(w_ada=v_w_ada, w_in=v_w_in, w_proj_a=v_w_proj_a, w_proj_b=v_w_proj_b, w_o=v_w_o, w_ffn_in=v_w_ffn_in, w_ffn_out=v_w_ffn_out)
    grads, deltas, new_m, new_v = {}, {}, {}, {}
    for nme in big_w:
        dl, m1, v1 = _adamw(big_w[nme][0], big_g[nme], big_m[nme][0], big_v[nme][0], name=f"adamw_{nme}")
        grads[nme], deltas[nme], new_m[nme], new_v[nme] = big_g[nme][None], dl[None], m1[None], v1[None]
    for nme in _SMALL[:-1]:
        grads[nme], deltas[nme], new_m[nme], new_v[nme] = gs[nme], ds_[nme], ms_[nme], vs_[nme]

    order = ("w_ada", "b_ada", "g_norm1", "g_norm2", "w_in", "b_gate", "g_qa", "g_ka", "g_qb", "g_kb", "rpb",
             "w_proj_a", "w_proj_b", "w_o", "w_ffn_in", "w_ffn_out")
    return (gs["loss"], grad_x[None], *[grads[n] for n in order], *[deltas[n] for n in order],
            *[new_m[n] for n in order], *[new_v[n] for n in order])
```

```python
import functools

import numpy as np
import jax
import jax.numpy as jnp
from jax import lax
from jax.experimental import pallas as pl
from jax.experimental.pallas import tpu as pltpu

F32 = jnp.float32
MXU_DTYPE = jnp.bfloat16
WIRE_DTYPE = jnp.bfloat16

HEAD_DIM = 64
GRID_W = 64
NA_HEADS = 8
NA_KH = 8
NA_KW = 16
DIL_CONFIGS = ((128, 1), (512, 4), (2048, 16))
DIL_HEADS_PER_GROUP = 4
DIL_HEADS = DIL_HEADS_PER_GROUP * len(DIL_CONFIGS)
DIL_HALF = 64
ROT_DIM = HEAD_DIM // 4
ROPE_THETA = 500000.0
EPS = 1e-6
NEG_INF = -1e30
WA = NA_HEADS * HEAD_DIM
WB = DIL_HEADS * HEAD_DIM
WB_OUT = DIL_HEADS_PER_GROUP * HEAD_DIM
ADAM_LR = 0.001
ADAM_B1 = 0.9
ADAM_B2 = 0.999
ADAM_EPS = 1e-08
ADAM_WD = 0.01
ADAM_STEP = 10

N_DEV = 8
LANES = 128
VMEM_CAP = 60 * 2**20
VMEM_FLOOR = 56 * 2**20
MESH = pl.DeviceIdType.MESH
ANY = pl.BlockSpec(memory_space=pl.ANY)


def _vmem(nbytes):
    return int(min(VMEM_CAP, max(VMEM_FLOOR, nbytes * 5 // 4 + 4 * 2**20)))


def _pick(dim, cands):
    for c in cands:
        if c <= dim and dim % c == 0:
            return c
    return dim


def _nbytes(shape, dtype):
    return int(np.prod(shape)) * jnp.dtype(dtype).itemsize


def _dot(a, b, dims=((1,), (0,))):
    return lax.dot_general(a.astype(MXU_DTYPE), b.astype(MXU_DTYPE), (dims, ((), ())), preferred_element_type=F32)


def _dot_nt(a, b):
    return _dot(a, b, ((1,), (1,)))


def _dot_tn(a, b):
    return _dot(a, b, ((0,), (0,)))


def _split3(a):
    hi = a.astype(jnp.bfloat16)
    r1 = a - hi.astype(F32)
    mid = r1.astype(jnp.bfloat16)
    lo = (r1 - mid.astype(F32)).astype(jnp.bfloat16)
    return hi, mid, lo


def _silu(x):
    return x * jax.nn.sigmoid(x)


def _mm(a, b, *, name, ta=False, tb=False, out_dtype=F32, add=None, b_rows=None):
    if ta:
        kdim, m = a.shape
    else:
        m, kdim = a.shape
    off, size = b_rows if b_rows is not None else (0, b.shape[0])
    if tb:
        n = size
        assert b.shape[1] == kdim
    else:
        n = b.shape[1]
        assert size == kdim
    cands = (1024, 768, 512, 384, 256, 128)
    tm = _pick(m, cands)
    if tb:
        tn = _pick(int(np.gcd(off, n)) if off else n, cands)
        tk = _pick(kdim, cands)
    else:
        tn = _pick(n, cands)
        tk = _pick(int(np.gcd(off, kdim)) if off else kdim, cands)
    gm, gn, gk = m // tm, n // tn, kdim // tk

    a_spec = pl.BlockSpec((tk, tm), lambda i, j, k: (k, i)) if ta else pl.BlockSpec((tm, tk), lambda i, j, k: (i, k))
    if tb:
        ob = off // tn
        b_spec = pl.BlockSpec((tn, tk), lambda i, j, k: (j + ob, k))
    else:
        ob = off // tk
        b_spec = pl.BlockSpec((tk, tn), lambda i, j, k: (k + ob, j))
    o_spec = pl.BlockSpec((tm, tn), lambda i, j, k: (i, j))
    in_specs = [a_spec, b_spec]
    operands = [a, b]
    if add is not None:
        in_specs.append(o_spec)
        operands.append(add)
    a_dims = (0,) if ta else (1,)
    b_dims = (1,) if tb else (0,)

    def body(*refs):
        if add is not None:
            a_ref, b_ref, add_ref, o_ref, acc_ref = refs
        else:
            a_ref, b_ref, o_ref, acc_ref = refs
        k = pl.program_id(2)

        @pl.when(k == 0)
        def _():
            acc_ref[...] = jnp.zeros_like(acc_ref)

        acc_ref[...] += _dot(a_ref[...], b_ref[...], (a_dims, b_dims))

        @pl.when(k == gk - 1)
        def _():
            r = acc_ref[...]
            if add is not None:
                r = r + add_ref[...].astype(F32)
            o_ref[...] = r.astype(o_ref.dtype)

    est = 2 * (tm * tk * a.dtype.itemsize + tk * tn * b.dtype.itemsize + tm * tn * jnp.dtype(out_dtype).itemsize)
    est += tm * tn * 4 * (3 if add is not None else 1) + 2 * (tm * tk + tk * tn) * 2
    return pl.pallas_call(
        body,
        name=name,
        grid=(gm, gn, gk),
        in_specs=in_specs,
        out_specs=o_spec,
        out_shape=jax.ShapeDtypeStruct((m, n), out_dtype),
        scratch_shapes=[pltpu.VMEM((tm, tn), F32)],
        compiler_params=pltpu.CompilerParams(
            dimension_semantics=("parallel", "parallel", "arbitrary"), vmem_limit_bytes=_vmem(est)
        ),
    )(*operands)


def _rowmap(fn, rows, vecs, outs, reds, *, name, tm=256):
    rows = [r if isinstance(r, tuple) else (r, r.shape[1], 0) for r in rows]
    t = rows[0][0].shape[0]
    tm = _pick(t, (tm, 128, 64, 32, 16, 8))
    nr, nv, no = len(rows), len(vecs), len(outs)

    def body(*refs):
        row_refs, vec_refs = refs[:nr], refs[nr:nr + nv]
        out_refs, red_refs = refs[nr + nv:nr + nv + no], refs[nr + nv + no:]
        o, rd = fn([r[...] for r in row_refs], [v[...] for v in vec_refs])
        for ref, val in zip(out_refs, o):
            ref[...] = val.astype(ref.dtype)
        if red_refs:
            @pl.when(pl.program_id(0) == 0)
            def _():
                for ref in red_refs:
                    ref[...] = jnp.zeros_like(ref)

            for ref, val in zip(red_refs, rd):
                ref[...] += val

    in_specs = [pl.BlockSpec((tm, w), functools.partial(lambda cb, i: (i, cb), cb)) for (_, w, cb) in rows]
    in_specs += [pl.BlockSpec(v.shape, functools.partial(lambda nd, i: (0,) * nd, v.ndim)) for v in vecs]
    out_specs = [pl.BlockSpec((tm, w), lambda i: (i, 0)) for (w, _) in outs]
    out_specs += [pl.BlockSpec((1, w), lambda i: (0, 0)) for w in reds]
    out_shape = [jax.ShapeDtypeStruct((t, w), d) for (w, d) in outs]
    out_shape += [jax.ShapeDtypeStruct((1, w), F32) for w in reds]
    est = 2 * sum(tm * w * a.dtype.itemsize for (a, w, _) in rows) + 2 * sum(_nbytes(v.shape, v.dtype) for v in vecs)
    est += 2 * sum(tm * w * jnp.dtype(d).itemsize for (w, d) in outs)
    est += 4 * tm * max([w for (_, w, _) in rows] + [w for (w, _) in outs]) * 4
    return pl.pallas_call(
        body,
        name=name,
        grid=(t // tm,),
        in_specs=in_specs,
        out_specs=out_specs,
        out_shape=out_shape,
        compiler_params=pltpu.CompilerParams(dimension_semantics=("arbitrary",), vmem_limit_bytes=_vmem(est)),
    )(*[r[0] for r in rows], *vecs)


def _colsum(v):
    return jnp.sum(v, axis=0, keepdims=True)


def _head_ones():
    i = np.arange(LANES)
    return jnp.asarray((i[:, None] // HEAD_DIM) == (i[None, :] // HEAD_DIM), MXU_DTYPE)


def _headsum(y, ones):
    parts = []
    for j in range(y.shape[1] // LANES):
        c = y[:, j * LANES:(j + 1) * LANES]
        hi = c.astype(MXU_DTYPE)
        lo = c - hi.astype(F32)
        parts.append(_dot(hi, ones) + _dot(lo, ones))
    return parts[0] if len(parts) == 1 else jnp.concatenate(parts, axis=1)


def _rot(y, c, s_lo, s_hi):
    parts = []
    for j in range(y.shape[1] // LANES):
        yc = y[:, j * LANES:(j + 1) * LANES]
        parts.append(yc * c + pltpu.roll(yc, LANES - ROT_DIM // 2, 1) * s_lo + pltpu.roll(yc, ROT_DIM // 2, 1) * s_hi)
    return parts[0] if len(parts) == 1 else jnp.concatenate(parts, axis=1)


def _rot_tables(t):
    half = ROT_DIM // 2
    inv_freq = ROPE_THETA ** (-(jnp.arange(half, dtype=F32) * 2.0) / ROT_DIM)
    ang = jnp.arange(t).astype(F32)[:, None] * inv_freq[None, :]
    cos, sin = jnp.cos(ang), jnp.sin(ang)
    z = lambda w: jnp.zeros((t, w), F32)
    c = jnp.concatenate([cos, cos, jnp.ones((t, HEAD_DIM - ROT_DIM), F32)], axis=1)
    s_lo = jnp.concatenate([-sin, z(HEAD_DIM - half)], axis=1)
    s_hi = jnp.concatenate([z(half), sin, z(HEAD_DIM - ROT_DIM)], axis=1)
    return [jnp.tile(a, (1, LANES // HEAD_DIM)) for a in (c, s_lo, s_hi)]


def _rms(x):
    return lax.rsqrt(jnp.mean(x * x, axis=-1, keepdims=True) + EPS)


def _window(kind, n, bq, t, seg):
    if kind == "na":
        rows = t // GRID_W
        rs = jnp.clip(n - NA_KH // 2, 0, rows - NA_KH)
        return rs
    nk = bq + 2 * DIL_HALF
    return jnp.clip(n * bq - DIL_HALF, 0, t - nk)


def _dil_mask(n, bq, nk, ws, seg):
    qi = n * bq + lax.broadcasted_iota(jnp.int32, (bq, nk), 0)
    ki = ws + lax.broadcasted_iota(jnp.int32, (bq, nk), 1)
    shift = int(np.log2(seg))
    return (jnp.abs(ki - qi) <= DIL_HALF) & ((ki >> shift) == (qi >> shift))


HS = 4
QW = HS * HEAD_DIM


def _head_of_lane(width=QW):
    return lax.broadcasted_iota(jnp.int32, (1, width), 1) // HEAD_DIM


def _stack_heads(a):
    head = _head_of_lane()
    return jnp.concatenate([jnp.where(head == e, a, jnp.zeros_like(a)) for e in range(HS)], axis=0)


def _unstack_heads(a, bq):
    head = _head_of_lane()
    out = jnp.zeros((bq, QW), a.dtype)
    for e in range(HS):
        out = jnp.where(head == e, a[e * bq:(e + 1) * bq], out)
    return out


def _stack_cols(blk, bq):
    head = _head_of_lane()
    return jnp.concatenate(
        [jnp.max(jnp.where(head == e, blk, -jnp.inf), axis=1, keepdims=True) for e in range(HS)], axis=0)


def _attn_geometry(kind):
    if kind == "na":
        return GRID_W, NA_KH * GRID_W
    bq = 128
    return bq, bq + 2 * DIL_HALF


def _attn_scores(kind, n, bq, nk, t, seg, qs, k_ref, b_ref):
    scale = HEAD_DIM ** -0.5
    if kind == "na":
        rs = _window(kind, n, bq, t, seg)
        ws = pl.multiple_of(rs * GRID_W, GRID_W)
        ro0 = rs - n + (NA_KH - 1)
        s = _dot_nt(qs, k_ref[pl.ds(ws, nk), :]) * scale
        s = s + jnp.concatenate(
            [jnp.concatenate([b_ref[e, ro0 + 2 * i] for i in range(NA_KH // 2)], axis=1) for e in range(HS)], axis=0)
        return s, ws, ro0
    ws = pl.multiple_of(_window(kind, n, bq, t, seg), DIL_HALF)
    mask = _dil_mask(n, bq, nk, ws, seg)
    s = _dot_nt(qs, k_ref[pl.ds(ws, nk), :]) * scale
    s = jnp.where(jnp.concatenate([mask] * HS, axis=0), s, NEG_INF)
    return s, ws, None


def _attn_fwd(q, k, v, *, kind, name, bias=None, seg=None):
    t, w = q.shape
    quads = w // QW
    bq, nk = _attn_geometry(kind)
    nq = t // bq

    def body(*refs):
        if kind == "na":
            q_ref, k_ref, v_ref, b_ref, o_ref, l_ref = refs
        else:
            (q_ref, k_ref, v_ref, o_ref, l_ref), b_ref = refs, None
        n = pl.program_id(1)
        s, ws, _ = _attn_scores(kind, n, bq, nk, t, seg, _stack_heads(q_ref[...]), k_ref, b_ref)
        m = jnp.max(s, axis=1, keepdims=True)
        p = jnp.exp(s - m)
        l = jnp.sum(p, axis=1, keepdims=True)
        o_ref[...] = _unstack_heads(_dot(p / l, v_ref[pl.ds(ws, nk), :]), bq)
        l_ref[...] = _unstack_heads(jnp.broadcast_to(m + jnp.log(l), (HS * bq, QW)), bq)

    blk = pl.BlockSpec((bq, QW), lambda j, n: (n, j))
    res = pl.BlockSpec((t, QW), lambda j, n: (0, j))
    in_specs = [blk, res, res]
    operands = [q, k, v]
    est = 4 * t * QW * q.dtype.itemsize + 12 * HS * bq * nk * 4
    if kind == "na":
        in_specs.append(pl.BlockSpec((HS,) + bias.shape[1:], lambda j, n: (j, 0, 0, 0)))
        operands.append(bias)
        est += 2 * _nbytes((HS,) + bias.shape[1:], F32)
    return pl.pallas_call(
        body,
        name=name,
        grid=(quads, nq),
        in_specs=in_specs,
        out_specs=[blk, blk],
        out_shape=[jax.ShapeDtypeStruct((t, w), F32)] * 2,
        compiler_params=pltpu.CompilerParams(dimension_semantics=("arbitrary", "arbitrary"), vmem_limit_bytes=_vmem(est)),
    )(*operands)


def _attn_bwd(q, k, v, do, dterm, lse, *, kind, name, bias=None, seg=None):
    t, w = q.shape
    quads = w // QW
    bq, nk = _attn_geometry(kind)
    nq = t // bq
    scale = HEAD_DIM ** -0.5

    def body(*refs):
        if kind == "na":
            q_ref, k_ref, v_ref, do_ref, dt_ref, l_ref, b_ref, dq_ref, dk_hbm, dv_hbm, db_ref, dk_acc, dv_acc, sem = refs
        else:
            q_ref, k_ref, v_ref, do_ref, dt_ref, l_ref, dq_ref, dk_hbm, dv_hbm, dk_acc, dv_acc, sem = refs
            b_ref = None
        j, n = pl.program_id(0), pl.program_id(1)

        @pl.when(n == 0)
        def _():
            dk_acc[...] = jnp.zeros_like(dk_acc)
            dv_acc[...] = jnp.zeros_like(dv_acc)
            if kind == "na":
                db_ref[...] = jnp.zeros_like(db_ref)

        qs = _stack_heads(q_ref[...])
        dos = _stack_heads(do_ref[...])
        s, ws, ro0 = _attn_scores(kind, n, bq, nk, t, seg, qs, k_ref, b_ref)
        p = jnp.exp(s - _stack_cols(l_ref[...], bq))
        dp = _dot_nt(dos, v_ref[pl.ds(ws, nk), :])
        ds = p * (dp - _stack_cols(dt_ref[...], bq))
        if kind == "na":
            for e in range(HS):
                for i in range(NA_KH // 2):
                    db_ref[e, ro0 + 2 * i] += ds[e * bq:(e + 1) * bq, i * LANES:(i + 1) * LANES]
        dsc = ds * scale
        dq_ref[...] = _unstack_heads(_dot(dsc, k_ref[pl.ds(ws, nk), :]), bq)
        dk_acc[pl.ds(ws, nk), :] += _dot_tn(dsc, qs)
        dv_acc[pl.ds(ws, nk), :] += _dot_tn(p, dos)

        @pl.when(n == nq - 1)
        def _():
            ck = pltpu.make_async_copy(dk_acc, dk_hbm.at[j], sem.at[0])
            cv = pltpu.make_async_copy(dv_acc, dv_hbm.at[j], sem.at[1])
            ck.start()
            cv.start()
            ck.wait()
            cv.wait()

    blk = pl.BlockSpec((bq, QW), lambda j, n: (n, j))
    res = pl.BlockSpec((t, QW), lambda j, n: (0, j))
    in_specs = [blk, res, res, blk, blk, blk]
    operands = [q, k, v, do, dterm, lse]
    out_specs = [blk, ANY, ANY]
    out_shape = [jax.ShapeDtypeStruct((t, w), F32)] + [jax.ShapeDtypeStruct((quads, t, QW), F32)] * 2
    est = 4 * t * QW * q.dtype.itemsize + 2 * t * QW * 4 + 16 * HS * bq * nk * 4
    if kind == "na":
        bspec = pl.BlockSpec((HS,) + bias.shape[1:], lambda j, n: (j, 0, 0, 0))
        in_specs.append(bspec)
        operands.append(bias)
        out_specs.append(bspec)
        out_shape.append(jax.ShapeDtypeStruct(bias.shape, F32))
        est += 4 * _nbytes((HS,) + bias.shape[1:], F32)
    res_ = pl.pallas_call(
        body,
        name=name,
        grid=(quads, nq),
        in_specs=in_specs,
        out_specs=out_specs,
        out_shape=out_shape,
        scratch_shapes=[pltpu.VMEM((t, QW), F32), pltpu.VMEM((t, QW), F32), pltpu.SemaphoreType.DMA((2,))],
        compiler_params=pltpu.CompilerParams(dimension_semantics=("arbitrary", "arbitrary"), vmem_limit_bytes=_vmem(est)),
    )(*operands)
    unquad = lambda a: [a[i] for i in range(quads)]
    return (res_[0], unquad(res_[1]), unquad(res_[2])) + tuple(res_[3:])


def _na_onehot():
    qc = np.arange(GRID_W)[:, None]
    kc = np.arange(GRID_W)[None, :]
    start = np.clip(qc - NA_KW // 2, 0, GRID_W - NA_KW)
    inwin = (kc >= start) & (kc < start + NA_KW)
    off = kc - qc + (NA_KW - 1)
    e_mat = np.zeros((2, 32, GRID_W, 2, GRID_W), np.float32)
    for e in range(2):
        for c in range(2 * NA_KW - 1):
            e_mat[e, c, :, e, :] = (off == c) & inwin
    neg = np.where(inwin, 0.0, NEG_INF).astype(np.float32)
    neg = np.broadcast_to(neg[:, None, :], (GRID_W, 2, GRID_W)).reshape(1, GRID_W * LANES)
    return jnp.asarray(e_mat.reshape(64, GRID_W * LANES), MXU_DTYPE), jnp.asarray(neg)


def _na_rowpairs(rpb):
    p = jnp.pad(rpb, ((0, 0), (0, 0), (0, 1)))
    return jnp.concatenate([p[:, :-1], p[:, 1:]], axis=-1).reshape(NA_HEADS * (2 * NA_KH - 2), 64)


def _na_bias_table(rpb):
    r2 = _na_rowpairs(rpb)
    e_mat, neg = _na_onehot()

    def body(r_ref, e_ref, n_ref, o_ref):
        hi, mid, lo = _split3(r_ref[...])
        e = e_ref[...]
        o_ref[...] = _dot(hi, e) + _dot(mid, e) + _dot(lo, e) + n_ref[...]

    out = pl.pallas_call(
        body,
        name="na_bias_table",
        out_shape=jax.ShapeDtypeStruct((r2.shape[0], GRID_W * LANES), F32),
        compiler_params=pltpu.CompilerParams(vmem_limit_bytes=_vmem(6 * r2.shape[0] * GRID_W * LANES * 4)),
    )(r2, e_mat, neg)
    return out.reshape(NA_HEADS, 2 * NA_KH - 2, GRID_W, LANES)


def _na_bias_grad(dbt):
    e_mat, _ = _na_onehot()
    flat = dbt.reshape(NA_HEADS * (2 * NA_KH - 2), GRID_W * LANES)

    def body(d_ref, e_ref, o_ref):
        hi, mid, lo = _split3(d_ref[...])
        e = e_ref[...]
        o_ref[...] = _dot_nt(hi, e) + _dot_nt(mid, e) + _dot_nt(lo, e)

    g = pl.pallas_call(
        body,
        name="na_bias_grad",
        out_shape=jax.ShapeDtypeStruct((flat.shape[0], 64), F32),
        compiler_params=pltpu.CompilerParams(vmem_limit_bytes=_vmem(6 * flat.shape[0] * GRID_W * LANES * 4)),
    )(flat, e_mat)
    g = g.reshape(NA_HEADS, 2 * NA_KH - 2, 2, 32)[..., :2 * NA_KW - 1]
    first = jnp.pad(g[:, :, 0], ((0, 0), (0, 1), (0, 0)))
    second = jnp.pad(g[:, :, 1], ((0, 0), (1, 0), (0, 0)))
    return first + second


def _all_gather(arrs, *, name):
    na = len(arrs)

    def body(*refs):
        ins, outs = refs[:na], refs[na:2 * na]
        send_sems, recv_sems, local_sems = refs[2 * na:]
        x, y, c = lax.axis_index("x"), lax.axis_index("y"), lax.axis_index("c")
        me, sibling = (x, y, c), (x, y, 1 - c)
        chips = [(1 - x, y), (x, 1 - y), (1 - x, 1 - y)]

        def rows(a, px, py, pc):
            r = ins[a].shape[0]
            return outs[a].at[pl.ds((4 * px + 2 * py + pc) * r, r), :]

        def copy(a, k, block, to, src=None):
            return pltpu.make_async_remote_copy(
                src_ref=rows(a, *block) if src is None else src, dst_ref=rows(a, *block),
                send_sem=send_sems.at[a, k], recv_sem=recv_sems.at[a, k], device_id=to, device_id_type=MESH)

        mine = [pltpu.make_async_copy(ins[a], rows(a, *me), local_sems.at[a]) for a in range(na)]
        for cp in mine:
            cp.start()
        first = []
        for a in range(na):
            first.append(copy(a, 0, me, sibling, src=ins[a]))
            first += [copy(a, 1 + j, me, (*chip, c), src=ins[a]) for j, chip in enumerate(chips)]
        for cp in first:
            cp.start()
        passed = []
        for j, chip in enumerate(chips):
            for a in range(na):
                copy(a, 1 + j, (*chip, c), me).wait_recv()
                cp = copy(a, 4 + j, (*chip, c), sibling)
                cp.start()
                passed.append(cp)
        for a in range(na):
            copy(a, 0, sibling, me).wait_recv()
        for j, chip in enumerate(chips):
            for a in range(na):
                copy(a, 4 + j, (*chip, 1 - c), me).wait_recv()
        for cp in first + passed:
            cp.wait_send()
        for cp in mine:
            cp.wait()

    return pl.pallas_call(
        body,
        name=name,
        in_specs=[ANY] * na,
        out_specs=[ANY] * na,
        out_shape=[jax.ShapeDtypeStruct((N_DEV * a.shape[0], a.shape[1]), a.dtype) for a in arrs],
        scratch_shapes=[pltpu.SemaphoreType.DMA((na, 7)), pltpu.SemaphoreType.DMA((na, 7)), pltpu.SemaphoreType.DMA((na,))],
    )(*arrs)


HBM = pl.BlockSpec(memory_space=pltpu.HBM)
SEM = pl.BlockSpec(memory_space=pltpu.SEMAPHORE)
EFFECT = pltpu.SideEffectType.DATAFLOW_SIDE_EFFECTING


def _peer_of(k):
    x, y, c = lax.axis_index("x"), lax.axis_index("y"), lax.axis_index("c")
    return x ^ ((k >> 2) & 1), y ^ ((k >> 1) & 1), c ^ (k & 1)


def _split_copies(gather, src_ref, land_ref, send_sems, recv_sems):
    x, y, c = lax.axis_index("x"), lax.axis_index("y"), lax.axis_index("c")
    my = 4 * x + 2 * y + c
    r = src_ref.shape[0] if gather else src_ref.shape[0] // N_DEV
    copies = []
    for k in range(1, N_DEV):
        px, py, pc = _peer_of(k)
        if gather:
            src, dst = src_ref, land_ref.at[pl.ds(my * r, r), :]
        else:
            src, dst = src_ref.at[pl.ds((4 * px + 2 * py + pc) * r, r), :], land_ref.at[k - 1]
        copies.append(pltpu.make_async_remote_copy(
            src_ref=src, dst_ref=dst, send_sem=send_sems.at[k - 1], recv_sem=recv_sems.at[k - 1],
            device_id=(px, py, pc), device_id_type=MESH))
    return copies


def _split_start(srcs, lands, *, gather, name, after=None):
    na = len(srcs)
    extra = [] if after is None else [after]

    def body(*refs):
        src_refs, land_refs = refs[:na], refs[na:2 * na]
        outs = refs[2 * na + len(extra):]
        for a in range(na):
            for cp in _split_copies(gather, src_refs[a], land_refs[a], outs[4 * a], outs[4 * a + 1]):
                cp.start()
        outs[4 * na][...] = jnp.zeros_like(outs[4 * na])

    out_shape, out_specs, aliases = [], [], {}
    for a in range(na):
        out_shape += [pltpu.SemaphoreType.DMA((N_DEV - 1,)), pltpu.SemaphoreType.DMA((N_DEV - 1,)),
                      pltpu.HBM(srcs[a].shape, srcs[a].dtype), pltpu.HBM(lands[a].shape, lands[a].dtype)]
        out_specs += [SEM, SEM, HBM, HBM]
        aliases[a] = 4 * a + 2
        aliases[na + a] = 4 * a + 3
    out_shape.append(jax.ShapeDtypeStruct((8, LANES), F32))
    out_specs.append(pl.BlockSpec(memory_space=pltpu.VMEM))
    res = pl.pallas_call(
        body,
        name=name,
        out_shape=tuple(out_shape),
        in_specs=[HBM] * (2 * na) + [ANY] * len(extra),
        out_specs=tuple(out_specs),
        input_output_aliases=aliases,
        compiler_params=pltpu.CompilerParams(has_side_effects=EFFECT),
    )(*[pltpu.with_memory_space_constraint(a, pltpu.HBM) for a in list(srcs) + list(lands)], *extra)
    return [tuple(res[4 * a:4 * a + 4]) for a in range(na)], res[4 * na][0, 0]


def _split_wait(handles, after, *, gather, name):
    na = len(handles)

    def body(*refs):
        src_refs, land_refs = refs[:na], refs[na:2 * na]
        sems = refs[2 * na:4 * na]
        for a in range(na):
            for cp in _split_copies(gather, src_refs[a], land_refs[a], sems[2 * a], sems[2 * a + 1]):
                cp.wait_send()
                cp.wait_recv()

    srcs = [h[2] for h in handles]
    lands = [h[3] for h in handles]
    sems = [s for h in handles for s in h[:2]]
    res = pl.pallas_call(
        body,
        name=name,
        out_shape=tuple(pltpu.HBM(a.shape, a.dtype) for a in srcs + lands),
        in_specs=[HBM] * (2 * na) + [SEM] * (2 * na) + [ANY],
        out_specs=tuple([HBM] * (2 * na)),
        input_output_aliases={i: i for i in range(2 * na)},
        compiler_params=pltpu.CompilerParams(has_side_effects=EFFECT),
    )(*srcs, *lands, *sems, after)
    return list(res[:na]), list(res[na:])


def _sum8(own, recv, *, name):
    _, r, w = recv.shape
    tr = _pick(r, (256, 128, 64, 32, 16, 8))

    def body(own_ref, a_ref, o_ref):
        acc = own_ref[...].astype(F32)
        for i in range(N_DEV - 1):
            acc = acc + a_ref[i].astype(F32)
        o_ref[...] = acc

    return pl.pallas_call(
        body,
        name=name,
        grid=(r // tr,),
        in_specs=[pl.BlockSpec((tr, w), lambda i: (i, 0)), pl.BlockSpec((N_DEV - 1, tr, w), lambda i: (0, i, 0))],
        out_specs=pl.BlockSpec((tr, w), lambda i: (i, 0)),
        out_shape=jax.ShapeDtypeStruct((r, w), F32),
        compiler_params=pltpu.CompilerParams(dimension_semantics=("parallel",), vmem_limit_bytes=_vmem(4 * N_DEV * tr * w * 4)),
    )(own, recv)


def _adamw(w, g, m, v, *, name):
    def fn(rows, _):
        wv, gv, mv, vv = rows
        m1 = ADAM_B1 * mv + (1.0 - ADAM_B1) * gv
        v1 = ADAM_B2 * vv + (1.0 - ADAM_B2) * jnp.square(gv)
        m_hat = m1 / (1.0 - ADAM_B1 ** ADAM_STEP)
        v_hat = v1 / (1.0 - ADAM_B2 ** ADAM_STEP)
        delta = -ADAM_LR * (m_hat / (jnp.sqrt(v_hat) + ADAM_EPS) + ADAM_WD * wv)
        return [delta, m1, v1], []

    c = w.shape[1]
    return _rowmap(fn, [w, g, m, v], [], [(c, F32)] * 3, [], name=name, tm=128)


_SMALL = ("b_ada", "g_norm1", "g_norm2", "b_gate", "g_qa", "g_ka", "g_qb", "g_kb", "rpb", "loss")


def _pack_small(parts):
    flat = []
    for nme in _SMALL:
        a = parts[nme].reshape(-1).astype(F32)
        flat.append(jnp.pad(a, (0, (-a.shape[0]) % LANES)))
    flat = jnp.concatenate(flat)
    flat = jnp.pad(flat, (0, (-flat.shape[0]) % (8 * LANES)))
    return flat.reshape(-1, LANES)


def _unpack_small(packed, shapes):
    flat = packed.reshape(-1)
    out, pos = {}, 0
    for nme in _SMALL:
        n = int(np.prod(shapes[nme]))
        out[nme] = flat[pos:pos + n].reshape(shapes[nme])
        pos += n + (-n) % LANES
    return out


def _to_class(a, d):
    t, w = a.shape
    return a if d == 1 else a.reshape(t // d, d, w).transpose(1, 0, 2).reshape(t, w)


def _from_class(a, d):
    t, w = a.shape
    return a if d == 1 else a.reshape(d, t // d, w).transpose(1, 0, 2).reshape(t, w)


def kernel(x, c, w_ada, b_ada, g_norm1, g_norm2, w_in, b_gate, g_qa, g_ka, g_qb, g_kb, rpb, w_proj_a, w_proj_b, w_o, w_ffn_in, w_ffn_out, loss_target, m_w_ada, m_b_ada, m_g_norm1, m_g_norm2, m_w_in, m_b_gate, m_g_qa, m_g_ka, m_g_qb, m_g_kb, m_rpb, m_w_proj_a, m_w_proj_b, m_w_o, m_w_ffn_in, m_w_ffn_out, v_w_ada, v_b_ada, v_g_norm1, v_g_norm2, v_w_in, v_b_gate, v_g_qa, v_g_ka, v_g_qb, v_g_kb, v_rpb, v_w_proj_a, v_w_proj_b, v_w_o, v_w_ffn_in, v_w_ffn_out):
    t, d = x.shape[1], x.shape[2]
    d_ff = w_ffn_out.shape[1] * N_DEV
    me = 4 * lax.axis_index("x") + 2 * lax.axis_index("y") + lax.axis_index("c")
    xt, tgt = x[0], loss_target[0]
    ones = _head_ones()

    shards = [s.astype(WIRE_DTYPE) for s in (w_in[0].T, w_ffn_in[0].T, w_proj_a[0].T, w_proj_b[0].T, w_o[0], w_ffn_out[0])]
    lands = [lax.dynamic_update_slice(lax.empty((N_DEV * s.shape[0], s.shape[1]), s.dtype), s, (me * s.shape[0], 0))
             for s in shards]

    c_all = _all_gather([jnp.pad(c, ((0, 7), (0, 0)))], name="gather_c")[0][::8]
    c_all = jnp.pad(c_all, ((0, LANES - N_DEV), (0, 0)))

    def mod_body(c_ref, w_ref, b_ref, o_ref, act_ref):
        act = _silu(c_ref[...])
        act_ref[...] = act
        hi, mid, lo = _split3(act)
        w = w_ref[...]
        whi, wmid, wlo = _split3(w)
        acc = _dot(hi, whi) + (_dot(hi, wmid) + _dot(mid, whi)) + (_dot(hi, wlo) + _dot(mid, wmid) + _dot(lo, whi))
        o_ref[...] = acc + b_ref[...]

    ncol = w_ada.shape[2]
    b_ada_mine = lax.dynamic_slice(b_ada, (0, me * ncol), (1, ncol))
    mod_part, c_act = pl.pallas_call(
        mod_body,
        name="ada_mod",
        out_shape=[jax.ShapeDtypeStruct((LANES, ncol), F32), jax.ShapeDtypeStruct((LANES, d), F32)],
        compiler_params=pltpu.CompilerParams(vmem_limit_bytes=_vmem(6 * d * ncol * 4)),
    )(c_all, w_ada[0], b_ada_mine)
    mod_all = _all_gather([mod_part[:N_DEV]], name="gather_mod")[0].reshape(N_DEV, N_DEV, ncol)
    mod = lax.dynamic_index_in_dim(mod_all, me, axis=1, keepdims=False).reshape(6, d)
    sh1, sc1, gt1, sh2, sc2, gt2 = [mod[i:i + 1] for i in range(6)]

    def norm_fwd(rows, vecs):
        (xv,), (g, sc, sh) = rows, vecs
        return [xv * _rms(xv) * g * (1.0 + sc) + sh], []

    w_handles, w_token = _split_start(shards, lands, gather=True, after=mod, name="gather_weights_start")
    (h,) = _rowmap(norm_fwd, [xt], [g_norm1 + w_token, sc1, sh1], [(d, MXU_DTYPE)], [], name="norm1")
    n_a, n_b = 3 * WA, 3 * WB
    (w_in_t,) = _split_wait(w_handles[:1], h, gather=True, name="gather_w_in_wait")[1]
    qkv_a = _mm(h, w_in_t, tb=True, b_rows=(0, n_a), name="proj_a")
    qkv_b = _mm(h, w_in_t, tb=True, b_rows=(n_a, n_b), name="proj_b")
    gates = _mm(h, w_in_t, tb=True, b_rows=(n_a + n_b, 2 * d), name="proj_gates")

    rot_c, rot_lo, rot_hi = _rot_tables(t)
    tile_g = lambda g, heads: jnp.tile(g, (1, heads))

    def qk_fwd(width, rotate):
        def fn(rows, vecs):
            xv = rows[0]
            gq, gk, on = vecs
            outs = []
            for i, g in enumerate((gq, gk)):
                xi = xv[:, i * width:(i + 1) * width]
                r = lax.rsqrt(_headsum(xi * xi, on) * (1.0 / HEAD_DIM) + EPS)
                yi = xi * r * g
                if rotate:
                    yi = _rot(yi, rows[1], rows[2], rows[3])
                outs.append(yi)
            outs.append(xv[:, 2 * width:])
            return outs, []
        return fn

    qa, ka, va = _rowmap(qk_fwd(WA, False), [qkv_a], [tile_g(g_qa, NA_HEADS), tile_g(g_ka, NA_HEADS), ones],
                         [(WA, MXU_DTYPE)] * 3, [], name="qknorm_a")
    qb, kb, vb = _rowmap(qk_fwd(WB, True), [qkv_b, rot_c, rot_lo, rot_hi],
                         [tile_g(g_qb, DIL_HEADS), tile_g(g_kb, DIL_HEADS), ones], [(WB, MXU_DTYPE)] * 3, [], name="qknorm_b")

    bias_tab = _na_bias_table(rpb[0])
    o_a, lse_a = _attn_fwd(qa, ka, va, kind="na", bias=bias_tab, name="na_fwd")

    grp = []
    for g, (_, dil) in enumerate(DIL_CONFIGS):
        sl = slice(g * WB_OUT, (g + 1) * WB_OUT)
        qg, kg, vg = [_to_class(a[:, sl], dil) for a in (qb, kb, vb)]
        og, lg = _attn_fwd(qg, kg, vg, kind="dil", seg=t // dil, name=f"dil_fwd{g}")
        grp.append(dict(q=qg, k=kg, v=vg, o=_from_class(og, dil), lse=_from_class(lg, dil), lse_c=lg, dil=dil))

    def merge_fwd(rows, _):
        o0, o1, o2, l0, l1, l2 = rows
        mx = jnp.maximum(jnp.maximum(l0, l1), l2)
        e0, e1, e2 = jnp.exp(l0 - mx), jnp.exp(l1 - mx), jnp.exp(l2 - mx)
        s = e0 + e1 + e2
        return [(e0 / s) * o0 + (e1 / s) * o1 + (e2 / s) * o2], []

    (o_b,) = _rowmap(merge_fwd, [gr["o"] for gr in grp] + [gr["lse"] for gr in grp], [], [(WB_OUT, F32)], [], name="dil_merge")

    w_pa_t, w_pb_t, w_o_f = _split_wait(w_handles[2:5], o_b, gather=True, name="gather_w_out_wait")[1]
    pa = _mm(o_a, w_pa_t, tb=True, name="proj_out_a")
    pb = _mm(o_b, w_pb_t, tb=True, name="proj_out_b")

    def gate_fwd(rows, vecs):
        gv, pav, pbv = rows
        sg = jax.nn.sigmoid(gv + vecs[0])
        return [sg[:, :d] * pav + sg[:, d:] * pbv], []

    (merged,) = _rowmap(gate_fwd, [gates, pa, pb], [b_gate], [(d, MXU_DTYPE)], [], name="gate_merge")
    att = _mm(merged, w_o_f, name="proj_o")

    def resid_norm(rows, vecs):
        xv, av = rows
        gt, g, sc, sh = vecs
        x1v = xv + gt * av
        return [x1v, x1v * _rms(x1v) * g * (1.0 + sc) + sh], []

    x1, h2 = _rowmap(resid_norm, [xt, att], [gt1, g_norm2, sc2, sh2], [(d, F32), (d, MXU_DTYPE)], [], name="resid_norm2")

    w_ffn_in_t, w_ffn_out_f = _split_wait([w_handles[1], w_handles[5]], h2, gather=True, name="gather_w_ffn_wait")[1]
    u = _mm(h2, w_ffn_in_t, tb=True, name="ffn_in")

    def swiglu_fwd(rows, _):
        uv = rows[0]
        return [_silu(uv[:, :d_ff]) * uv[:, d_ff:]], []

    (f,) = _rowmap(swiglu_fwd, [u], [], [(d_ff, MXU_DTYPE)], [], name="swiglu")
    y2 = _mm(f, w_ffn_out_f, name="ffn_out")

    def loss_fn(rows, vecs):
        yv, x1v, tv = rows
        gt = vecs[0]
        err = x1v + gt * yv - tv
        dout = err * (1.0 / d)
        return [dout, dout * gt], [_colsum(err * err), _colsum(dout * yv)]

    dout, dy2, err2, dgt2 = _rowmap(loss_fn, [y2, x1, tgt], [gt2], [(d, F32), (d, MXU_DTYPE)], [d, d], name="loss")

    dw_ffn_out = _mm(f, dy2, ta=True, out_dtype=WIRE_DTYPE, name="wgrad_ffn_out")
    df = _mm(dy2, w_ffn_out_f, tb=True, name="dgrad_ffn_out")

    def swiglu_bwd(rows, _):
        dfv, uv = rows
        a, up = uv[:, :d_ff], uv[:, d_ff:]
        sg = jax.nn.sigmoid(a)
        da = dfv * up * (sg * (1.0 + a * (1.0 - sg)))
        return [jnp.concatenate([da, dfv * (a * sg)], axis=1)], []

    (du,) = _rowmap(swiglu_bwd, [df, u], [], [(2 * d_ff, MXU_DTYPE)], [], name="swiglu_bwd", tm=128)
    dw_ffn_in_t = _mm(du, h2, ta=True, out_dtype=WIRE_DTYPE, name="wgrad_ffn_in")
    land7 = lambda a: lax.empty((N_DEV - 1, a.shape[0] // N_DEV, a.shape[1]), a.dtype)
    own_block = lambda a: lax.dynamic_slice(a, (me * (a.shape[0] // N_DEV), 0), (a.shape[0] // N_DEV, a.shape[1]))
    g_ffn = [dw_ffn_in_t, dw_ffn_out]
    h_ffn, tok_ffn = _split_start(g_ffn, [land7(a) for a in g_ffn], gather=False, name="exchange_ffn_start")
    dh2 = _mm(du, w_ffn_in_t, name="dgrad_ffn_in")

    def norm_bwd(dh, xv, g, sc):
        r = _rms(xv)
        xh = xv * r
        dxh = dh * g * (1.0 + sc)
        dxv = r * (dxh - xh * jnp.mean(dxh * xh, axis=-1, keepdims=True))
        return dxv, [_colsum(dh), _colsum(dh * xh * g), _colsum(dh * xh * (1.0 + sc))]

    def norm2_bwd(rows, vecs):
        dhv, x1v, dov, av = rows
        g, sc, gt = vecs
        dxv, sums = norm_bwd(dhv, x1v, g, sc)
        dx1v = dov + dxv
        return [dx1v, dx1v * gt], sums + [_colsum(dx1v * av)]

    dx1, datt, dsh2, dsc2, dg2, dgt1 = _rowmap(norm2_bwd, [dh2, x1, dout, att], [g_norm2 + tok_ffn, sc2, gt1],
                                                [(d, F32), (d, MXU_DTYPE)], [d] * 4, name="norm2_bwd")
    dw_o = _mm(merged, datt, ta=True, out_dtype=WIRE_DTYPE, name="wgrad_o")
    dmerged = _mm(datt, w_o_f, tb=True, name="dgrad_o")

    def gate_bwd(rows, vecs):
        dm, gv, pav, pbv = rows
        sg = jax.nn.sigmoid(gv + vecs[0])
        ga, gb = sg[:, :d], sg[:, d:]
        dgp = jnp.concatenate([dm * pav * ga * (1.0 - ga), dm * pbv * gb * (1.0 - gb)], axis=1)
        return [dm * ga, dm * gb, dgp], [_colsum(dgp)]

    dpa, dpb, dgates, db_gate = _rowmap(gate_bwd, [dmerged, gates, pa, pb], [b_gate],
                                        [(d, MXU_DTYPE), (d, MXU_DTYPE), (2 * d, MXU_DTYPE)], [2 * d], name="gate_bwd")
    dw_pa_t = _mm(dpa, o_a, ta=True, out_dtype=WIRE_DTYPE, name="wgrad_proj_a")
    dw_pb_t = _mm(dpb, o_b, ta=True, out_dtype=WIRE_DTYPE, name="wgrad_proj_b")
    g_out = [dw_pa_t, dw_pb_t, dw_o]
    h_out, tok_out = _split_start(g_out, [land7(a) for a in g_out], gather=False, name="exchange_out_start")
    do_a = _mm(dpa, w_pa_t, name="dgrad_proj_a")
    do_b = _mm(dpb, w_pb_t, name="dgrad_proj_b")

    def delta_a(rows, vecs):
        return [_headsum(rows[0] * rows[1], vecs[0])], []

    (dterm_a,) = _rowmap(delta_a, [do_a, o_a], [ones + tok_out.astype(ones.dtype)], [(WA, F32)], [], name="na_delta")
    dqa, dka, dva, dbias = _attn_bwd(qa, ka, va, do_a, dterm_a, lse_a, kind="na", bias=bias_tab, name="na_bwd")
    g_rpb = _na_bias_grad(dbias)

    def merge_bwd(rows, vecs):
        dob, o0, o1, o2, l0, l1, l2 = rows
        on = vecs[0]
        mx = jnp.maximum(jnp.maximum(l0, l1), l2)
        e0, e1, e2 = jnp.exp(l0 - mx), jnp.exp(l1 - mx), jnp.exp(l2 - mx)
        s = e0 + e1 + e2
        ws = [e0 / s, e1 / s, e2 / s]
        dws = [_headsum(dob * o, on) for o in (o0, o1, o2)]
        mean = ws[0] * dws[0] + ws[1] * dws[1] + ws[2] * dws[2]
        return [w * dob for w in ws] + [w * mean for w in ws], []

    mb = _rowmap(merge_bwd, [do_b] + [gr["o"] for gr in grp] + [gr["lse"] for gr in grp], [ones],
                 [(WB_OUT, F32)] * 6, [], name="dil_merge_bwd")
    dqb, dkb, dvb = [], [], []
    for g, gr in enumerate(grp):
        dil = gr["dil"]
        dq, dk, dv = _attn_bwd(gr["q"], gr["k"], gr["v"], _to_class(mb[g], dil), _to_class(mb[3 + g], dil), gr["lse_c"],
                               kind="dil", seg=t // dil, name=f"dil_bwd{g}")
        dqb.append(_from_class(dq, dil))
        dkb.append(_from_class(dk[0], dil))
        dvb.append(_from_class(dv[0], dil))

    def qk_bwd(width, rotate, nparts):
        def fn(rows, vecs):
            gq, gk, on = vecs
            xv = rows[0]
            pos = 1
            if rotate:
                rc, rlo, rhi = rows[1:4]
                pos = 4
            cat = lambda parts: parts[0] if len(parts) == 1 else jnp.concatenate(parts, axis=1)
            ends = np.cumsum((pos,) + nparts)
            dq, dk, dv = [cat(rows[ends[i]:ends[i + 1]]) for i in range(3)]
            outs, sums = [], []
            for i, (dy, g) in enumerate(((dq, gq), (dk, gk))):
                if rotate:
                    dy = _rot(dy, rc, -rlo, -rhi)
                xi = xv[:, i * width:(i + 1) * width]
                r = lax.rsqrt(_headsum(xi * xi, on) * (1.0 / HEAD_DIM) + EPS)
                xh = xi * r
                dxh = dy * g
                outs.append(r * (dxh - xh * (_headsum(dxh * xh, on) * (1.0 / HEAD_DIM))))
                sums.append(_colsum(dy * xh))
            return [jnp.concatenate(outs + [dv], axis=1)], sums
        return fn

    dqkv_a, dg_qa, dg_ka = _rowmap(qk_bwd(WA, False, (1, len(dka), len(dva))), [qkv_a, dqa] + dka + dva,
                                   [tile_g(g_qa, NA_HEADS), tile_g(g_ka, NA_HEADS), ones],
                                   [(3 * WA, MXU_DTYPE)], [WA, WA], name="qknorm_a_bwd", tm=128)
    dqkv_b, dg_qb, dg_kb = _rowmap(qk_bwd(WB, True, (3, 3, 3)), [qkv_b, rot_c, rot_lo, rot_hi] + dqb + dkb + dvb,
                                   [tile_g(g_qb, DIL_HEADS), tile_g(g_kb, DIL_HEADS), ones],
                                   [(3 * WB, MXU_DTYPE)], [WB, WB], name="qknorm_b_bwd", tm=128)

    dw_in_t = jnp.concatenate([
        _mm(dqkv_a, h, ta=True, out_dtype=WIRE_DTYPE, name="wgrad_in_a"),
        _mm(dqkv_b, h, ta=True, out_dtype=WIRE_DTYPE, name="wgrad_in_b"),
        _mm(dgates, h, ta=True, out_dtype=WIRE_DTYPE, name="wgrad_in_gates")], axis=0)
    h_in, tok_in = _split_start([dw_in_t], [land7(dw_in_t)], gather=False, name="exchange_in_start")
    dh = _mm(dqkv_a, w_in_t, b_rows=(0, n_a), name="dgrad_in_a")
    dh = _mm(dqkv_b, w_in_t, b_rows=(n_a, n_b), add=dh, name="dgrad_in_b")
    dh = _mm(dgates, w_in_t, b_rows=(n_a + n_b, 2 * d), add=dh, name="dgrad_in_gates")

    def norm1_bwd(rows, vecs):
        dhv, xv, dx1v = rows
        dxv, sums = norm_bwd(dhv, xv, vecs[0], vecs[1])
        return [dx1v + dxv], sums

    grad_x, dsh1, dsc1, dg1 = _rowmap(norm1_bwd, [dh, xt, dx1], [g_norm1 + tok_in, sc1], [(d, F32)], [d] * 3, name="norm1_bwd")

    heads_sum = lambda a, heads: a.reshape(heads, HEAD_DIM).sum(axis=0)
    dmod = jnp.concatenate([dsh1, dsc1, dgt1, dsh2, dsc2, dgt2], axis=1)
    local_small = _pack_small(dict(
        b_ada=dmod, g_norm1=dg1, g_norm2=dg2, b_gate=db_gate, g_qa=heads_sum(dg_qa, NA_HEADS),
        g_ka=heads_sum(dg_ka, NA_HEADS), g_qb=heads_sum(dg_qb, DIL_HEADS), g_kb=heads_sum(dg_kb, DIL_HEADS),
        rpb=g_rpb, loss=(0.5 / d) * jnp.sum(err2)))
    srows = local_small.shape[0]
    small_all = _all_gather([local_small], name="gather_small")[0].reshape(N_DEV, srows, LANES)
    small_sum = _sum8(small_all[0], small_all[1:], name="sum_small")
    small_shapes = dict(b_ada=b_ada.shape, g_norm1=g_norm1.shape, g_norm2=g_norm2.shape, b_gate=b_gate.shape,
                        g_qa=g_qa.shape, g_ka=g_ka.shape, g_qb=g_qb.shape, g_kb=g_kb.shape, rpb=rpb.shape, loss=())
    small_w = dict(b_ada=b_ada, g_norm1=g_norm1, g_norm2=g_norm2, b_gate=b_gate, g_qa=g_qa, g_ka=g_ka, g_qb=g_qb,
                   g_kb=g_kb, rpb=rpb, loss=jnp.zeros((), F32))
    small_m = dict(b_ada=m_b_ada, g_norm1=m_g_norm1, g_norm2=m_g_norm2, b_gate=m_b_gate, g_qa=m_g_qa, g_ka=m_g_ka,
                   g_qb=m_g_qb, g_kb=m_g_kb, rpb=m_rpb, loss=jnp.zeros((), F32))
    small_v = dict(b_ada=v_b_ada, g_norm1=v_g_norm1, g_norm2=v_g_norm2, b_gate=v_b_gate, g_qa=v_g_qa, g_ka=v_g_ka,
                   g_qb=v_g_qb, g_kb=v_g_kb, rpb=v_rpb, loss=jnp.zeros((), F32))
    s_delta, s_m, s_v = _adamw(_pack_small(small_w), small_sum, _pack_small(small_m), _pack_small(small_v), name="adamw_small")
    gs = _unpack_small(small_sum, small_shapes)
    ds_, ms_, vs_ = [_unpack_small(a, small_shapes) for a in (s_delta, s_m, s_v)]

    dmod_all = small_all[:, :6 * d // LANES].reshape(N_DEV, 6 * d)
    dmod_mine = jnp.pad(lax.dynamic_slice(dmod_all, (0, me * ncol), (N_DEV, ncol)), ((0, LANES - N_DEV), (0, 0)))

    def wada_body(c_ref, dm_ref, o_ref):
        chi, cmid, clo = _split3(c_ref[...])
        dhi, dmid, dlo = _split3(dm_ref[...])
        o_ref[...] = (_dot_tn(chi, dhi) + (_dot_tn(chi, dmid) + _dot_tn(cmid, dhi))
                      + (_dot_tn(chi, dlo) + _dot_tn(cmid, dmid) + _dot_tn(clo, dhi)))

    g_w_ada = pl.pallas_call(
        wada_body,
        name="wgrad_ada",
        out_shape=jax.ShapeDtypeStruct((d, ncol), F32),
        compiler_params=pltpu.CompilerParams(vmem_limit_bytes=_vmem(4 * d * ncol * 4)),
    )(c_act, dmod_mine)

    sent, recv = _split_wait(h_in + h_ffn + h_out, small_sum, gather=False, name="exchange_wait")
    names = ("w_in", "w_ffn_in", "w_ffn_out", "w_proj_a", "w_proj_b", "w_o")
    transposed = (True, True, False, True, True, False)
    big_g = {}
    for nme, own, r, tr in zip(names, sent, recv, transposed):
        s = _sum8(own_block(own), r, name=f"sum_{nme}")
        big_g[nme] = s.T if tr else s
    big_g["w_ada"] = g_w_ada
    big_w = dict(w_ada=w_ada, w_in=w_in, w_proj_a=w_proj_a, w_proj_b=w_proj_b, w_o=w_o, w_ffn_in=w_ffn_in, w_ffn_out=w_ffn_out)
    big_m = dict(w_ada=m_w_ada, w_in=m_w_in, w_proj_a=m_w_proj_a, w_proj_b=m_w_proj_b, w_o=m_w_o, w_ffn_in=m_w_ffn_in, w_ffn_out=m_w_ffn_out)
    big_v = dict(w_ada=v_w_ada, w_in=v_w_in, w_proj_a=v_w_proj_a, w_proj_b=v_w_proj_b, w_o=v_w_o, w_ffn_in=v_w_ffn_in, w_ffn_out=v_w_ffn_out)
    grads, deltas, new_m, new_v = {}, {}, {}, {}
    for nme in big_w:
        dl, m1, v1 = _adamw(big_w[nme][0], big_g[nme], big_m[nme][0], big_v[nme][0], name=f"adamw_{nme}")
        grads[nme], deltas[nme], new_m[nme], new_v[nme] = big_g[nme][None], dl[None], m1[None], v1[None]
    for nme in _SMALL[:-1]:
        grads[nme], deltas[nme], new_m[nme], new_v[nme] = gs[nme], ds_[nme], ms_[nme], vs_[nme]

    order = ("w_ada", "b_ada", "g_norm1", "g_norm2", "w_in", "b_gate", "g_qa", "g_ka", "g_qb", "g_kb", "rpb",
             "w_proj_a", "w_proj_b", "w_o", "w_ffn_in", "w_ffn_out")
    return (gs["loss"], grad_x[None], *[grads[n] for n in order], *[deltas[n] for n in order],
            *[new_m[n] for n in order], *[new_v[n] for n in order])
```

```python
import functools

import numpy as np
import jax
import jax.numpy as jnp
from jax import lax
from jax.experimental import pallas as pl
from jax.experimental.pallas import tpu as pltpu

F32 = jnp.float32
MXU_DTYPE = jnp.bfloat16
WIRE_DTYPE = jnp.bfloat16
ACT_DTYPE = jnp.bfloat16

HEAD_DIM = 64
GRID_W = 64
NA_HEADS = 8
NA_KH = 8
NA_KW = 16
DIL_CONFIGS = ((128, 1), (512, 4), (2048, 16))
DIL_HEADS_PER_GROUP = 4
DIL_HEADS = DIL_HEADS_PER_GROUP * len(DIL_CONFIGS)
DIL_HALF = 64
ROT_DIM = HEAD_DIM // 4
ROPE_THETA = 500000.0
EPS = 1e-6
NEG_INF = -1e30
WA = NA_HEADS * HEAD_DIM
WB = DIL_HEADS * HEAD_DIM
WB_OUT = DIL_HEADS_PER_GROUP * HEAD_DIM
ADAM_LR = 0.001
ADAM_B1 = 0.9
ADAM_B2 = 0.999
ADAM_EPS = 1e-08
ADAM_WD = 0.01
ADAM_STEP = 10

N_DEV = 8
LANES = 128
VMEM_CAP = 60 * 2**20
VMEM_FLOOR = 56 * 2**20
MESH = pl.DeviceIdType.MESH
ANY = pl.BlockSpec(memory_space=pl.ANY)


def _vmem(nbytes):
    return int(min(VMEM_CAP, max(VMEM_FLOOR, nbytes * 5 // 4 + 4 * 2**20)))


def _pick(dim, cands):
    for c in cands:
        if c <= dim and dim % c == 0:
            return c
    return dim


def _nbytes(shape, dtype):
    return int(np.prod(shape)) * jnp.dtype(dtype).itemsize


def _dot(a, b, dims=((1,), (0,))):
    return lax.dot_general(a.astype(MXU_DTYPE), b.astype(MXU_DTYPE), (dims, ((), ())), preferred_element_type=F32)


def _dot_nt(a, b):
    return _dot(a, b, ((1,), (1,)))


def _dot_tn(a, b):
    return _dot(a, b, ((0,), (0,)))


def _split3(a):
    hi = a.astype(jnp.bfloat16)
    r1 = a - hi.astype(F32)
    mid = r1.astype(jnp.bfloat16)
    lo = (r1 - mid.astype(F32)).astype(jnp.bfloat16)
    return hi, mid, lo


def _silu(x):
    return x * jax.nn.sigmoid(x)


def _divisors(dim, unit):
    return [c for c in range(unit, dim + 1, unit) if dim % c == 0] or [dim]


def _mm_tiles(m, n, kdim, off_n, off_k, a_item, b_item, o_item, has_add):
    step_us, hbm_bytes_per_us, flops_per_us, budget = 0.35, 3.0e6, 8.0e8, 40 * 2**20
    best = None
    for tm in _divisors(m, LANES):
        for tn in _divisors(n, LANES):
            for tk in _divisors(kdim, LANES):
                if off_n % tn or off_k % tk:
                    continue
                gm, gn, gk = m // tm, n // tn, kdim // tk
                vmem = 2 * (tm * tk * a_item + tk * tn * b_item + tm * tn * o_item) + 2 * (tm * tk + tk * tn)
                vmem += tm * tn * 4 * ((1 if gk > 1 else 0) + 1 + (2 if has_add else 0))
                if vmem > budget:
                    continue
                a_reads = m * kdim * a_item * (gn if gk > 1 else 1)
                traffic = a_reads + kdim * n * b_item * gm + m * n * (o_item + (4 if has_add else 0))
                cost = gm * gn * gk * step_us + max(traffic / hbm_bytes_per_us, 2.0 * m * n * kdim / flops_per_us)
                if best is None or cost < best[0]:
                    best = (cost, tm, tn, tk)
    return best[1:]


def _mm(a, b, *, name, ta=False, tb=False, out_dtype=F32, add=None, b_rows=None):
    if ta:
        kdim, m = a.shape
    else:
        m, kdim = a.shape
    off, size = b_rows if b_rows is not None else (0, b.shape[0])
    if tb:
        n = size
        assert b.shape[1] == kdim
    else:
        n = b.shape[1]
        assert size == kdim
    tm, tn, tk = _mm_tiles(m, n, kdim, off if tb else 0, 0 if tb else off, a.dtype.itemsize, b.dtype.itemsize,
                           jnp.dtype(out_dtype).itemsize, add is not None)
    gm, gn, gk = m // tm, n // tn, kdim // tk

    a_spec = pl.BlockSpec((tk, tm), lambda i, j, k: (k, i)) if ta else pl.BlockSpec((tm, tk), lambda i, j, k: (i, k))
    if tb:
        ob = off // tn
        b_spec = pl.BlockSpec((tn, tk), lambda i, j, k: (j + ob, k))
    else:
        ob = off // tk
        b_spec = pl.BlockSpec((tk, tn), lambda i, j, k: (k + ob, j))
    o_spec = pl.BlockSpec((tm, tn), lambda i, j, k: (i, j))
    in_specs = [a_spec, b_spec]
    operands = [a, b]
    if add is not None:
        in_specs.append(o_spec)
        operands.append(add)
    a_dims = (0,) if ta else (1,)
    b_dims = (1,) if tb else (0,)

    def body(*refs):
        a_ref, b_ref = refs[:2]
        add_ref = refs[2] if add is not None else None
        o_ref = refs[3] if add is not None else refs[2]

        def finish(r):
            if add is not None:
                r = r + add_ref[...].astype(F32)
            o_ref[...] = r.astype(o_ref.dtype)

        if gk == 1:
            finish(_dot(a_ref[...], b_ref[...], (a_dims, b_dims)))
            return
        acc_ref = refs[-1]
        k = pl.program_id(2)

        @pl.when(k == 0)
        def _():
            acc_ref[...] = jnp.zeros_like(acc_ref)

        acc_ref[...] += _dot(a_ref[...], b_ref[...], (a_dims, b_dims))

        @pl.when(k == gk - 1)
        def _():
            finish(acc_ref[...])

    est = 2 * (tm * tk * a.dtype.itemsize + tk * tn * b.dtype.itemsize + tm * tn * jnp.dtype(out_dtype).itemsize)
    est += tm * tn * 4 * (3 if add is not None else 1) + 2 * (tm * tk + tk * tn) * 2
    return pl.pallas_call(
        body,
        name=name,
        grid=(gm, gn, gk),
        in_specs=in_specs,
        out_specs=o_spec,
        out_shape=jax.ShapeDtypeStruct((m, n), out_dtype),
        scratch_shapes=[pltpu.VMEM((tm, tn), F32)] if gk > 1 else [],
        compiler_params=pltpu.CompilerParams(
            dimension_semantics=("parallel", "parallel", "arbitrary"), vmem_limit_bytes=_vmem(est)
        ),
    )(*operands)


def _rowmap(fn, rows, vecs, outs, reds, *, name, tm=256):
    rows = [r if isinstance(r, tuple) else (r, r.shape[1], 0) for r in rows]
    t = rows[0][0].shape[0]
    tm = _pick(t, (tm, 128, 64, 32, 16, 8))
    nr, nv, no = len(rows), len(vecs), len(outs)

    def body(*refs):
        row_refs, vec_refs = refs[:nr], refs[nr:nr + nv]
        out_refs, red_refs = refs[nr + nv:nr + nv + no], refs[nr + nv + no:]
        o, rd = fn([r[...].astype(F32) for r in row_refs], [v[...] for v in vec_refs])
        for ref, val in zip(out_refs, o):
            ref[...] = val.astype(ref.dtype)
        if red_refs:
            @pl.when(pl.program_id(0) == 0)
            def _():
                for ref in red_refs:
                    ref[...] = jnp.zeros_like(ref)

            for ref, val in zip(red_refs, rd):
                ref[...] += val

    in_specs = [pl.BlockSpec((tm, w), functools.partial(lambda cb, i: (i, cb), cb)) for (_, w, cb) in rows]
    in_specs += [pl.BlockSpec(v.shape, functools.partial(lambda nd, i: (0,) * nd, v.ndim)) for v in vecs]
    out_specs = [pl.BlockSpec((tm, w), lambda i: (i, 0)) for (w, _) in outs]
    out_specs += [pl.BlockSpec((1, w), lambda i: (0, 0)) for w in reds]
    out_shape = [jax.ShapeDtypeStruct((t, w), d) for (w, d) in outs]
    out_shape += [jax.ShapeDtypeStruct((1, w), F32) for w in reds]
    est = 2 * sum(tm * w * a.dtype.itemsize for (a, w, _) in rows) + 2 * sum(_nbytes(v.shape, v.dtype) for v in vecs)
    est += 2 * sum(tm * w * jnp.dtype(d).itemsize for (w, d) in outs)
    est += 4 * tm * max([w for (_, w, _) in rows] + [w for (w, _) in outs]) * 4
    return pl.pallas_call(
        body,
        name=name,
        grid=(t // tm,),
        in_specs=in_specs,
        out_specs=out_specs,
        out_shape=out_shape,
        compiler_params=pltpu.CompilerParams(dimension_semantics=("arbitrary",), vmem_limit_bytes=_vmem(est)),
    )(*[r[0] for r in rows], *vecs)


def _colsum(v):
    return jnp.sum(v, axis=0, keepdims=True)


def _head_ones():
    i = np.arange(LANES)
    return jnp.asarray((i[:, None] // HEAD_DIM) == (i[None, :] // HEAD_DIM), MXU_DTYPE)


def _headsum(y, ones):
    parts = []
    for j in range(y.shape[1] // LANES):
        c = y[:, j * LANES:(j + 1) * LANES]
        hi = c.astype(MXU_DTYPE)
        lo = c - hi.astype(F32)
        parts.append(_dot(hi, ones) + _dot(lo, ones))
    return parts[0] if len(parts) == 1 else jnp.concatenate(parts, axis=1)


def _rot(y, c, s_lo, s_hi):
    parts = []
    for j in range(y.shape[1] // LANES):
        yc = y[:, j * LANES:(j + 1) * LANES]
        parts.append(yc * c + pltpu.roll(yc, LANES - ROT_DIM // 2, 1) * s_lo + pltpu.roll(yc, ROT_DIM // 2, 1) * s_hi)
    return parts[0] if len(parts) == 1 else jnp.concatenate(parts, axis=1)


def _rot_tables(t):
    half = ROT_DIM // 2
    inv_freq = ROPE_THETA ** (-(jnp.arange(half, dtype=F32) * 2.0) / ROT_DIM)
    ang = jnp.arange(t).astype(F32)[:, None] * inv_freq[None, :]
    cos, sin = jnp.cos(ang), jnp.sin(ang)
    z = lambda w: jnp.zeros((t, w), F32)
    c = jnp.concatenate([cos, cos, jnp.ones((t, HEAD_DIM - ROT_DIM), F32)], axis=1)
    s_lo = jnp.concatenate([-sin, z(HEAD_DIM - half)], axis=1)
    s_hi = jnp.concatenate([z(half), sin, z(HEAD_DIM - ROT_DIM)], axis=1)
    return [jnp.tile(a, (1, LANES // HEAD_DIM)) for a in (c, s_lo, s_hi)]


def _rms(x):
    return lax.rsqrt(jnp.mean(x * x, axis=-1, keepdims=True) + EPS)


def _window(kind, n, bq, t, seg):
    if kind == "na":
        rows = t // GRID_W
        rs = jnp.clip(n - NA_KH // 2, 0, rows - NA_KH)
        return rs
    nk = bq + 2 * DIL_HALF
    return jnp.clip(n * bq - DIL_HALF, 0, t - nk)


def _dil_mask(n, bq, nk, ws, seg):
    qi = n * bq + lax.broadcasted_iota(jnp.int32, (bq, nk), 0)
    ki = ws + lax.broadcasted_iota(jnp.int32, (bq, nk), 1)
    shift = int(np.log2(seg))
    return (jnp.abs(ki - qi) <= DIL_HALF) & ((ki >> shift) == (qi >> shift))


HS = 4
QW = HS * HEAD_DIM


def _head_of_lane(width=QW):
    return lax.broadcasted_iota(jnp.int32, (1, width), 1) // HEAD_DIM


def _stack_heads(a):
    head = _head_of_lane()
    return jnp.concatenate([jnp.where(head == e, a, jnp.zeros_like(a)) for e in range(HS)], axis=0)


def _unstack_heads(a, bq):
    head = _head_of_lane()
    out = jnp.zeros((bq, QW), a.dtype)
    for e in range(HS):
        out = jnp.where(head == e, a[e * bq:(e + 1) * bq], out)
    return out


def _stack_cols(blk, bq):
    head = _head_of_lane()
    return jnp.concatenate(
        [jnp.max(jnp.where(head == e, blk, -jnp.inf), axis=1, keepdims=True) for e in range(HS)], axis=0)


def _attn_geometry(kind):
    if kind == "na":
        return GRID_W, NA_KH * GRID_W, 4
    bq = 128
    return bq, bq + 2 * DIL_HALF, 2


def _attn_scores(kind, n, bq, nk, t, seg, qs, k_ref, b_ref):
    scale = HEAD_DIM ** -0.5
    if kind == "na":
        rs = _window(kind, n, bq, t, seg)
        ws = pl.multiple_of(rs * GRID_W, GRID_W)
        ro0 = rs - n + (NA_KH - 1)
        s = _dot_nt(qs, k_ref[pl.ds(ws, nk), :]) * scale
        s = s + jnp.concatenate(
            [jnp.concatenate([b_ref[e, ro0 + 2 * i] for i in range(NA_KH // 2)], axis=1) for e in range(HS)], axis=0)
        return s, ws, ro0
    ws = pl.multiple_of(_window(kind, n, bq, t, seg), DIL_HALF)
    mask = _dil_mask(n, bq, nk, ws, seg)
    s = _dot_nt(qs, k_ref[pl.ds(ws, nk), :]) * scale
    s = jnp.where(jnp.concatenate([mask] * HS, axis=0), s, NEG_INF)
    return s, ws, None


def _attn_fwd(q, k, v, *, kind, name, bias=None, seg=None):
    t, w = q.shape
    quads = w // QW
    bq, nk, sub = _attn_geometry(kind)
    nq = t // (bq * sub)

    def body(*refs):
        if kind == "na":
            q_ref, k_ref, v_ref, b_ref, o_ref, l_ref = refs
        else:
            (q_ref, k_ref, v_ref, o_ref, l_ref), b_ref = refs, None
        for i in range(sub):
            n = pl.program_id(1) * sub + i
            rows = slice(i * bq, (i + 1) * bq)
            s, ws, _ = _attn_scores(kind, n, bq, nk, t, seg, _stack_heads(q_ref[rows, :]), k_ref, b_ref)
            m = jnp.max(s, axis=1, keepdims=True)
            p = jnp.exp(s - m)
            l = jnp.sum(p, axis=1, keepdims=True)
            o_ref[rows, :] = _unstack_heads(_dot(p / l, v_ref[pl.ds(ws, nk), :]), bq)
            l_ref[rows, :] = _unstack_heads(jnp.broadcast_to(m + jnp.log(l), (HS * bq, QW)), bq)

    blk = pl.BlockSpec((bq * sub, QW), lambda j, n: (n, j))
    res = pl.BlockSpec((t, QW), lambda j, n: (0, j))
    in_specs = [blk, res, res]
    operands = [q, k, v]
    est = 4 * t * QW * q.dtype.itemsize + 12 * sub * HS * bq * nk * 4
    if kind == "na":
        in_specs.append(pl.BlockSpec((HS,) + bias.shape[1:], lambda j, n: (j, 0, 0, 0)))
        operands.append(bias)
        est += 2 * _nbytes((HS,) + bias.shape[1:], F32)
    return pl.pallas_call(
        body,
        name=name,
        grid=(quads, nq),
        in_specs=in_specs,
        out_specs=[blk, blk],
        out_shape=[jax.ShapeDtypeStruct((t, w), F32)] * 2,
        compiler_params=pltpu.CompilerParams(dimension_semantics=("arbitrary", "arbitrary"), vmem_limit_bytes=_vmem(est)),
    )(*operands)


def _attn_bwd(q, k, v, do, dterm, lse, *, kind, name, bias=None, seg=None):
    t, w = q.shape
    quads = w // QW
    bq, nk, sub = _attn_geometry(kind)
    nq = t // (bq * sub)
    scale = HEAD_DIM ** -0.5

    def body(*refs):
        if kind == "na":
            q_ref, k_ref, v_ref, do_ref, dt_ref, l_ref, b_ref, dq_ref, dk_hbm, dv_hbm, db_ref, dk_acc, dv_acc, sem = refs
        else:
            q_ref, k_ref, v_ref, do_ref, dt_ref, l_ref, dq_ref, dk_hbm, dv_hbm, dk_acc, dv_acc, sem = refs
            b_ref = None
        j, step = pl.program_id(0), pl.program_id(1)

        @pl.when(step == 0)
        def _():
            dk_acc[...] = jnp.zeros_like(dk_acc)
            dv_acc[...] = jnp.zeros_like(dv_acc)
            if kind == "na":
                db_ref[...] = jnp.zeros_like(db_ref)

        for b in range(sub):
            n = step * sub + b
            rows = slice(b * bq, (b + 1) * bq)
            qs = _stack_heads(q_ref[rows, :])
            dos = _stack_heads(do_ref[rows, :])
            s, ws, ro0 = _attn_scores(kind, n, bq, nk, t, seg, qs, k_ref, b_ref)
            p = jnp.exp(s - _stack_cols(l_ref[rows, :], bq))
            dp = _dot_nt(dos, v_ref[pl.ds(ws, nk), :])
            ds = p * (dp - _stack_cols(dt_ref[rows, :], bq))
            if kind == "na":
                for e in range(HS):
                    for i in range(NA_KH // 2):
                        db_ref[e, ro0 + 2 * i] += ds[e * bq:(e + 1) * bq, i * LANES:(i + 1) * LANES]
            dsc = ds * scale
            dq_ref[rows, :] = _unstack_heads(_dot(dsc, k_ref[pl.ds(ws, nk), :]), bq)
            dk_acc[pl.ds(ws, nk), :] += _dot_tn(dsc, qs)
            dv_acc[pl.ds(ws, nk), :] += _dot_tn(p, dos)

        @pl.when(step == nq - 1)
        def _():
            ck = pltpu.make_async_copy(dk_acc, dk_hbm.at[j], sem.at[0])
            cv = pltpu.make_async_copy(dv_acc, dv_hbm.at[j], sem.at[1])
            ck.start()
            cv.start()
            ck.wait()
            cv.wait()

    blk = pl.BlockSpec((bq * sub, QW), lambda j, n: (n, j))
    res = pl.BlockSpec((t, QW), lambda j, n: (0, j))
    in_specs = [blk, res, res, blk, blk, blk]
    operands = [q, k, v, do, dterm, lse]
    out_specs = [blk, ANY, ANY]
    out_shape = [jax.ShapeDtypeStruct((t, w), F32)] + [jax.ShapeDtypeStruct((quads, t, QW), F32)] * 2
    est = 4 * t * QW * q.dtype.itemsize + 2 * t * QW * 4 + 16 * sub * HS * bq * nk * 4
    if kind == "na":
        bspec = pl.BlockSpec((HS,) + bias.shape[1:], lambda j, n: (j, 0, 0, 0))
        in_specs.append(bspec)
        operands.append(bias)
        out_specs.append(bspec)
        out_shape.append(jax.ShapeDtypeStruct(bias.shape, F32))
        est += 4 * _nbytes((HS,) + bias.shape[1:], F32)
    res_ = pl.pallas_call(
        body,
        name=name,
        grid=(quads, nq),
        in_specs=in_specs,
        out_specs=out_specs,
        out_shape=out_shape,
        scratch_shapes=[pltpu.VMEM((t, QW), F32), pltpu.VMEM((t, QW), F32), pltpu.SemaphoreType.DMA((2,))],
        compiler_params=pltpu.CompilerParams(dimension_semantics=("arbitrary", "arbitrary"), vmem_limit_bytes=_vmem(est)),
    )(*operands)
    unquad = lambda a: [a[i] for i in range(quads)]
    return (res_[0], unquad(res_[1]), unquad(res_[2])) + tuple(res_[3:])


def _na_onehot():
    qc = np.arange(GRID_W)[:, None]
    kc = np.arange(GRID_W)[None, :]
    start = np.clip(qc - NA_KW // 2, 0, GRID_W - NA_KW)
    inwin = (kc >= start) & (kc < start + NA_KW)
    off = kc - qc + (NA_KW - 1)
    e_mat = np.zeros((2, 32, GRID_W, 2, GRID_W), np.float32)
    for e in range(2):
        for c in range(2 * NA_KW - 1):
            e_mat[e, c, :, e, :] = (off == c) & inwin
    neg = np.where(inwin, 0.0, NEG_INF).astype(np.float32)
    neg = np.broadcast_to(neg[:, None, :], (GRID_W, 2, GRID_W)).reshape(1, GRID_W * LANES)
    return jnp.asarray(e_mat.reshape(64, GRID_W * LANES), MXU_DTYPE), jnp.asarray(neg)


def _na_rowpairs(rpb):
    p = jnp.pad(rpb, ((0, 0), (0, 0), (0, 1)))
    return jnp.concatenate([p[:, :-1], p[:, 1:]], axis=-1).reshape(NA_HEADS * (2 * NA_KH - 2), 64)


def _na_bias_table(rpb):
    r2 = _na_rowpairs(rpb)
    e_mat, neg = _na_onehot()

    def body(r_ref, e_ref, n_ref, o_ref):
        hi, mid, lo = _split3(r_ref[...])
        e = e_ref[...]
        o_ref[...] = _dot(hi, e) + _dot(mid, e) + _dot(lo, e) + n_ref[...]

    out = pl.pallas_call(
        body,
        name="na_bias_table",
        out_shape=jax.ShapeDtypeStruct((r2.shape[0], GRID_W * LANES), F32),
        compiler_params=pltpu.CompilerParams(vmem_limit_bytes=_vmem(6 * r2.shape[0] * GRID_W * LANES * 4)),
    )(r2, e_mat, neg)
    return out.reshape(NA_HEADS, 2 * NA_KH - 2, GRID_W, LANES)


def _na_bias_grad(dbt):
    e_mat, _ = _na_onehot()
    flat = dbt.reshape(NA_HEADS * (2 * NA_KH - 2), GRID_W * LANES)

    def body(d_ref, e_ref, o_ref):
        hi, mid, lo = _split3(d_ref[...])
        e = e_ref[...]
        o_ref[...] = _dot_nt(hi, e) + _dot_nt(mid, e) + _dot_nt(lo, e)

    g = pl.pallas_call(
        body,
        name="na_bias_grad",
        out_shape=jax.ShapeDtypeStruct((flat.shape[0], 64), F32),
        compiler_params=pltpu.CompilerParams(vmem_limit_bytes=_vmem(6 * flat.shape[0] * GRID_W * LANES * 4)),
    )(flat, e_mat)
    g = g.reshape(NA_HEADS, 2 * NA_KH - 2, 2, 32)[..., :2 * NA_KW - 1]
    first = jnp.pad(g[:, :, 0], ((0, 0), (0, 1), (0, 0)))
    second = jnp.pad(g[:, :, 1], ((0, 0), (1, 0), (0, 0)))
    return first + second


def _all_gather(arrs, *, name):
    na = len(arrs)

    def body(*refs):
        ins, outs = refs[:na], refs[na:2 * na]
        send_sems, recv_sems, local_sems = refs[2 * na:]
        x, y, c = lax.axis_index("x"), lax.axis_index("y"), lax.axis_index("c")
        me, sibling = (x, y, c), (x, y, 1 - c)
        chips = [(1 - x, y), (x, 1 - y), (1 - x, 1 - y)]

        def rows(a, px, py, pc):
            r = ins[a].shape[0]
            return outs[a].at[pl.ds((4 * px + 2 * py + pc) * r, r), :]

        def copy(a, k, block, to, src=None):
            return pltpu.make_async_remote_copy(
                src_ref=rows(a, *block) if src is None else src, dst_ref=rows(a, *block),
                send_sem=send_sems.at[a, k], recv_sem=recv_sems.at[a, k], device_id=to, device_id_type=MESH)

        mine = [pltpu.make_async_copy(ins[a], rows(a, *me), local_sems.at[a]) for a in range(na)]
        for cp in mine:
            cp.start()
        first = []
        for a in range(na):
            first.append(copy(a, 0, me, sibling, src=ins[a]))
            first += [copy(a, 1 + j, me, (*chip, c), src=ins[a]) for j, chip in enumerate(chips)]
        for cp in first:
            cp.start()
        passed = []
        for j, chip in enumerate(chips):
            for a in range(na):
                copy(a, 1 + j, (*chip, c), me).wait_recv()
                cp = copy(a, 4 + j, (*chip, c), sibling)
                cp.start()
                passed.append(cp)
        for a in range(na):
            copy(a, 0, sibling, me).wait_recv()
        for j, chip in enumerate(chips):
            for a in range(na):
                copy(a, 4 + j, (*chip, 1 - c), me).wait_recv()
        for cp in first + passed:
            cp.wait_send()
        for cp in mine:
            cp.wait()

    return pl.pallas_call(
        body,
        name=name,
        in_specs=[ANY] * na,
        out_specs=[ANY] * na,
        out_shape=[jax.ShapeDtypeStruct((N_DEV * a.shape[0], a.shape[1]), a.dtype) for a in arrs],
        scratch_shapes=[pltpu.SemaphoreType.DMA((na, 7)), pltpu.SemaphoreType.DMA((na, 7)), pltpu.SemaphoreType.DMA((na,))],
    )(*arrs)


HBM = pl.BlockSpec(memory_space=pltpu.HBM)
SEM = pl.BlockSpec(memory_space=pltpu.SEMAPHORE)
EFFECT = pltpu.SideEffectType.DATAFLOW_SIDE_EFFECTING


def _peer_of(k):
    x, y, c = lax.axis_index("x"), lax.axis_index("y"), lax.axis_index("c")
    return x ^ ((k >> 2) & 1), y ^ ((k >> 1) & 1), c ^ (k & 1)


def _split_copies(gather, src_ref, land_ref, send_sems, recv_sems):
    x, y, c = lax.axis_index("x"), lax.axis_index("y"), lax.axis_index("c")
    my = 4 * x + 2 * y + c
    r = src_ref.shape[0] if gather else src_ref.shape[0] // N_DEV
    copies = []
    for k in range(1, N_DEV):
        px, py, pc = _peer_of(k)
        if gather:
            src, dst = src_ref, land_ref.at[pl.ds(my * r, r), :]
        else:
            src, dst = src_ref.at[pl.ds((4 * px + 2 * py + pc) * r, r), :], land_ref.at[k - 1]
        copies.append(pltpu.make_async_remote_copy(
            src_ref=src, dst_ref=dst, send_sem=send_sems.at[k - 1], recv_sem=recv_sems.at[k - 1],
            device_id=(px, py, pc), device_id_type=MESH))
    return copies


def _split_start(srcs, lands, *, gather, name, after=None):
    na = len(srcs)
    extra = [] if after is None else [after]

    def body(*refs):
        src_refs, land_refs = refs[:na], refs[na:2 * na]
        outs = refs[2 * na + len(extra):]
        for a in range(na):
            for cp in _split_copies(gather, src_refs[a], land_refs[a], outs[4 * a], outs[4 * a + 1]):
                cp.start()
        outs[4 * na][...] = jnp.zeros_like(outs[4 * na])

    out_shape, out_specs, aliases = [], [], {}
    for a in range(na):
        out_shape += [pltpu.SemaphoreType.DMA((N_DEV - 1,)), pltpu.SemaphoreType.DMA((N_DEV - 1,)),
                      pltpu.HBM(srcs[a].shape, srcs[a].dtype), pltpu.HBM(lands[a].shape, lands[a].dtype)]
        out_specs += [SEM, SEM, HBM, HBM]
        aliases[a] = 4 * a + 2
        aliases[na + a] = 4 * a + 3
    out_shape.append(jax.ShapeDtypeStruct((8, LANES), F32))
    out_specs.append(pl.BlockSpec(memory_space=pltpu.VMEM))
    res = pl.pallas_call(
        body,
        name=name,
        out_shape=tuple(out_shape),
        in_specs=[HBM] * (2 * na) + [ANY] * len(extra),
        out_specs=tuple(out_specs),
        input_output_aliases=aliases,
        compiler_params=pltpu.CompilerParams(has_side_effects=EFFECT),
    )(*[pltpu.with_memory_space_constraint(a, pltpu.HBM) for a in list(srcs) + list(lands)], *extra)
    return [tuple(res[4 * a:4 * a + 4]) for a in range(na)], res[4 * na][0, 0]


def _split_wait(handles, after, *, gather, name):
    na = len(handles)

    def body(*refs):
        src_refs, land_refs = refs[:na], refs[na:2 * na]
        sems = refs[2 * na:4 * na]
        for a in range(na):
            for cp in _split_copies(gather, src_refs[a], land_refs[a], sems[2 * a], sems[2 * a + 1]):
                cp.wait_send()
                cp.wait_recv()

    srcs = [h[2] for h in handles]
    lands = [h[3] for h in handles]
    sems = [s for h in handles for s in h[:2]]
    res = pl.pallas_call(
        body,
        name=name,
        out_shape=tuple(pltpu.HBM(a.shape, a.dtype) for a in srcs + lands),
        in_specs=[HBM] * (2 * na) + [SEM] * (2 * na) + [ANY],
        out_specs=tuple([HBM] * (2 * na)),
        input_output_aliases={i: i for i in range(2 * na)},
        compiler_params=pltpu.CompilerParams(has_side_effects=EFFECT),
    )(*srcs, *lands, *sems, after)
    return list(res[:na]), list(res[na:])


def _sum8(own, recv, *, name):
    _, r, w = recv.shape
    tr = _pick(r, (256, 128, 64, 32, 16, 8))

    def body(own_ref, a_ref, o_ref):
        acc = own_ref[...].astype(F32)
        for i in range(N_DEV - 1):
            acc = acc + a_ref[i].astype(F32)
        o_ref[...] = acc

    return pl.pallas_call(
        body,
        name=name,
        grid=(r // tr,),
        in_specs=[pl.BlockSpec((tr, w), lambda i: (i, 0)), pl.BlockSpec((N_DEV - 1, tr, w), lambda i: (0, i, 0))],
        out_specs=pl.BlockSpec((tr, w), lambda i: (i, 0)),
        out_shape=jax.ShapeDtypeStruct((r, w), F32),
        compiler_params=pltpu.CompilerParams(dimension_semantics=("parallel",), vmem_limit_bytes=_vmem(4 * N_DEV * tr * w * 4)),
    )(own, recv)


def _adamw(w, g, m, v, *, name):
    def fn(rows, _):
        wv, gv, mv, vv = rows
        m1 = ADAM_B1 * mv + (1.0 - ADAM_B1) * gv
        v1 = ADAM_B2 * vv + (1.0 - ADAM_B2) * jnp.square(gv)
        m_hat = m1 / (1.0 - ADAM_B1 ** ADAM_STEP)
        v_hat = v1 / (1.0 - ADAM_B2 ** ADAM_STEP)
        delta = -ADAM_LR * (m_hat / (jnp.sqrt(v_hat) + ADAM_EPS) + ADAM_WD * wv)
        return [delta, m1, v1], []

    c = w.shape[1]
    return _rowmap(fn, [w, g, m, v], [], [(c, F32)] * 3, [], name=name, tm=128)


_SMALL = ("b_ada", "g_norm1", "g_norm2", "b_gate", "g_qa", "g_ka", "g_qb", "g_kb", "rpb", "loss")


def _pack_small(parts):
    flat = []
    for nme in _SMALL:
        a = parts[nme].reshape(-1).astype(F32)
        flat.append(jnp.pad(a, (0, (-a.shape[0]) % LANES)))
    flat = jnp.concatenate(flat)
    flat = jnp.pad(flat, (0, (-flat.shape[0]) % (8 * LANES)))
    return flat.reshape(-1, LANES)


def _unpack_small(packed, shapes):
    flat = packed.reshape(-1)
    out, pos = {}, 0
    for nme in _SMALL:
        n = int(np.prod(shapes[nme]))
        out[nme] = flat[pos:pos + n].reshape(shapes[nme])
        pos += n + (-n) % LANES
    return out


def _to_class(a, d):
    t, w = a.shape
    return a if d == 1 else a.reshape(t // d, d, w).transpose(1, 0, 2).reshape(t, w)


def _from_class(a, d):
    t, w = a.shape
    return a if d == 1 else a.reshape(d, t // d, w).transpose(1, 0, 2).reshape(t, w)


def kernel(x, c, w_ada, b_ada, g_norm1, g_norm2, w_in, b_gate, g_qa, g_ka, g_qb, g_kb, rpb, w_proj_a, w_proj_b, w_o, w_ffn_in, w_ffn_out, loss_target, m_w_ada, m_b_ada, m_g_norm1, m_g_norm2, m_w_in, m_b_gate, m_g_qa, m_g_ka, m_g_qb, m_g_kb, m_rpb, m_w_proj_a, m_w_proj_b, m_w_o, m_w_ffn_in, m_w_ffn_out, v_w_ada, v_b_ada, v_g_norm1, v_g_norm2, v_w_in, v_b_gate, v_g_qa, v_g_ka, v_g_qb, v_g_kb, v_rpb, v_w_proj_a, v_w_proj_b, v_w_o, v_w_ffn_in, v_w_ffn_out):
    t, d = x.shape[1], x.shape[2]
    d_ff = w_ffn_out.shape[1] * N_DEV
    me = 4 * lax.axis_index("x") + 2 * lax.axis_index("y") + lax.axis_index("c")
    xt, tgt = x[0], loss_target[0]
    ones = _head_ones()

    shards = [s.astype(WIRE_DTYPE) for s in (w_in[0].T, w_ffn_in[0].T, w_proj_a[0].T, w_proj_b[0].T, w_o[0], w_ffn_out[0])]
    lands = [lax.dynamic_update_slice(lax.empty((N_DEV * s.shape[0], s.shape[1]), s.dtype), s, (me * s.shape[0], 0))
             for s in shards]

    c_all = _all_gather([jnp.pad(c, ((0, 7), (0, 0)))], name="gather_c")[0][::8]
    c_all = jnp.pad(c_all, ((0, LANES - N_DEV), (0, 0)))

    def mod_body(c_ref, w_ref, b_ref, o_ref, act_ref):
        act = _silu(c_ref[...])
        act_ref[...] = act
        hi, mid, lo = _split3(act)
        w = w_ref[...]
        whi, wmid, wlo = _split3(w)
        acc = _dot(hi, whi) + (_dot(hi, wmid) + _dot(mid, whi)) + (_dot(hi, wlo) + _dot(mid, wmid) + _dot(lo, whi))
        o_ref[...] = acc + b_ref[...]

    ncol = w_ada.shape[2]
    b_ada_mine = lax.dynamic_slice(b_ada, (0, me * ncol), (1, ncol))
    mod_part, c_act = pl.pallas_call(
        mod_body,
        name="ada_mod",
        out_shape=[jax.ShapeDtypeStruct((LANES, ncol), F32), jax.ShapeDtypeStruct((LANES, d), F32)],
        compiler_params=pltpu.CompilerParams(vmem_limit_bytes=_vmem(6 * d * ncol * 4)),
    )(c_all, w_ada[0], b_ada_mine)
    mod_all = _all_gather([mod_part[:N_DEV]], name="gather_mod")[0].reshape(N_DEV, N_DEV, ncol)
    mod = lax.dynamic_index_in_dim(mod_all, me, axis=1, keepdims=False).reshape(6, d)
    sh1, sc1, gt1, sh2, sc2, gt2 = [mod[i:i + 1] for i in range(6)]

    def norm_fwd(rows, vecs):
        (xv,), (g, sc, sh) = rows, vecs
        return [xv * _rms(xv) * g * (1.0 + sc) + sh], []

    w_handles, w_token = _split_start(shards, lands, gather=True, after=mod, name="gather_weights_start")
    (h,) = _rowmap(norm_fwd, [xt], [g_norm1 + w_token, sc1, sh1], [(d, MXU_DTYPE)], [], name="norm1")
    n_a, n_b = 3 * WA, 3 * WB
    (w_in_t,) = _split_wait(w_handles[:1], h, gather=True, name="gather_w_in_wait")[1]
    qkv_a = _mm(h, w_in_t, tb=True, b_rows=(0, n_a), out_dtype=ACT_DTYPE, name="proj_a")
    qkv_b = _mm(h, w_in_t, tb=True, b_rows=(n_a, n_b), out_dtype=ACT_DTYPE, name="proj_b")
    gates = _mm(h, w_in_t, tb=True, b_rows=(n_a + n_b, 2 * d), out_dtype=ACT_DTYPE, name="proj_gates")

    rot_c, rot_lo, rot_hi = _rot_tables(t)
    tile_g = lambda g, heads: jnp.tile(g, (1, heads))

    def qk_fwd(width, rotate):
        def fn(rows, vecs):
            xv = rows[0]
            gq, gk, on = vecs
            outs = []
            for i, g in enumerate((gq, gk)):
                xi = xv[:, i * width:(i + 1) * width]
                r = lax.rsqrt(_headsum(xi * xi, on) * (1.0 / HEAD_DIM) + EPS)
                yi = xi * r * g
                if rotate:
                    yi = _rot(yi, rows[1], rows[2], rows[3])
                outs.append(yi)
            outs.append(xv[:, 2 * width:])
            return outs, []
        return fn

    qa, ka, va = _rowmap(qk_fwd(WA, False), [qkv_a], [tile_g(g_qa, NA_HEADS), tile_g(g_ka, NA_HEADS), ones],
                         [(WA, MXU_DTYPE)] * 3, [], name="qknorm_a")
    qb, kb, vb = _rowmap(qk_fwd(WB, True), [qkv_b, rot_c, rot_lo, rot_hi],
                         [tile_g(g_qb, DIL_HEADS), tile_g(g_kb, DIL_HEADS), ones], [(WB, MXU_DTYPE)] * 3, [], name="qknorm_b")

    bias_tab = _na_bias_table(rpb[0])
    o_a, lse_a = _attn_fwd(qa, ka, va, kind="na", bias=bias_tab, name="na_fwd")

    grp = []
    for g, (_, dil) in enumerate(DIL_CONFIGS):
        sl = slice(g * WB_OUT, (g + 1) * WB_OUT)
        qg, kg, vg = [_to_class(a[:, sl], dil) for a in (qb, kb, vb)]
        og, lg = _attn_fwd(qg, kg, vg, kind="dil", seg=t // dil, name=f"dil_fwd{g}")
        grp.append(dict(q=qg, k=kg, v=vg, o=_from_class(og, dil), lse=_from_class(lg, dil), lse_c=lg, dil=dil))

    def merge_fwd(rows, _):
        o0, o1, o2, l0, l1, l2 = rows
        mx = jnp.maximum(jnp.maximum(l0, l1), l2)
        e0, e1, e2 = jnp.exp(l0 - mx), jnp.exp(l1 - mx), jnp.exp(l2 - mx)
        s = e0 + e1 + e2
        return [(e0 / s) * o0 + (e1 / s) * o1 + (e2 / s) * o2], []

    (o_b,) = _rowmap(merge_fwd, [gr["o"] for gr in grp] + [gr["lse"] for gr in grp], [], [(WB_OUT, F32)], [], name="dil_merge")

    w_pa_t, w_pb_t, w_o_f = _split_wait(w_handles[2:5], o_b, gather=True, name="gather_w_out_wait")[1]
    pa = _mm(o_a, w_pa_t, tb=True, out_dtype=ACT_DTYPE, name="proj_out_a")
    pb = _mm(o_b, w_pb_t, tb=True, out_dtype=ACT_DTYPE, name="proj_out_b")

    def gate_fwd(rows, vecs):
        gv, pav, pbv = rows
        sg = jax.nn.sigmoid(gv + vecs[0])
        return [sg[:, :d] * pav + sg[:, d:] * pbv], []

    (merged,) = _rowmap(gate_fwd, [gates, pa, pb], [b_gate], [(d, MXU_DTYPE)], [], name="gate_merge")
    att = _mm(merged, w_o_f, name="proj_o")

    def resid_norm(rows, vecs):
        xv, av = rows
        gt, g, sc, sh = vecs
        x1v = xv + gt * av
        return [x1v, x1v * _rms(x1v) * g * (1.0 + sc) + sh], []

    x1, h2 = _rowmap(resid_norm, [xt, att], [gt1, g_norm2, sc2, sh2], [(d, F32), (d, MXU_DTYPE)], [], name="resid_norm2")

    w_ffn_in_t, w_ffn_out_f = _split_wait([w_handles[1], w_handles[5]], h2, gather=True, name="gather_w_ffn_wait")[1]
    u = _mm(h2, w_ffn_in_t, tb=True, out_dtype=ACT_DTYPE, name="ffn_in")

    def swiglu_fwd(rows, _):
        uv = rows[0]
        return [_silu(uv[:, :d_ff]) * uv[:, d_ff:]], []

    (f,) = _rowmap(swiglu_fwd, [u], [], [(d_ff, MXU_DTYPE)], [], name="swiglu")
    y2 = _mm(f, w_ffn_out_f, name="ffn_out")

    def loss_fn(rows, vecs):
        yv, x1v, tv = rows
        gt = vecs[0]
        err = x1v + gt * yv - tv
        dout = err * (1.0 / d)
        return [dout, dout * gt], [_colsum(err * err), _colsum(dout * yv)]

    dout, dy2, err2, dgt2 = _rowmap(loss_fn, [y2, x1, tgt], [gt2], [(d, F32), (d, MXU_DTYPE)], [d, d], name="loss")

    dw_ffn_out = _mm(f, dy2, ta=True, out_dtype=WIRE_DTYPE, name="wgrad_ffn_out")
    df = _mm(dy2, w_ffn_out_f, tb=True, out_dtype=ACT_DTYPE, name="dgrad_ffn_out")

    def swiglu_bwd(rows, _):
        dfv, uv = rows
        a, up = uv[:, :d_ff], uv[:, d_ff:]
        sg = jax.nn.sigmoid(a)
        da = dfv * up * (sg * (1.0 + a * (1.0 - sg)))
        return [jnp.concatenate([da, dfv * (a * sg)], axis=1)], []

    (du,) = _rowmap(swiglu_bwd, [df, u], [], [(2 * d_ff, MXU_DTYPE)], [], name="swiglu_bwd", tm=128)
    dw_ffn_in_t = _mm(du, h2, ta=True, out_dtype=WIRE_DTYPE, name="wgrad_ffn_in")
    land7 = lambda a: lax.empty((N_DEV - 1, a.shape[0] // N_DEV, a.shape[1]), a.dtype)
    own_block = lambda a: lax.dynamic_slice(a, (me * (a.shape[0] // N_DEV), 0), (a.shape[0] // N_DEV, a.shape[1]))
    g_ffn = [dw_ffn_in_t, dw_ffn_out]
    h_ffn, tok_ffn = _split_start(g_ffn, [land7(a) for a in g_ffn], gather=False, name="exchange_ffn_start")
    dh2 = _mm(du, w_ffn_in_t, name="dgrad_ffn_in")

    def norm_bwd(dh, xv, g, sc):
        r = _rms(xv)
        xh = xv * r
        dxh = dh * g * (1.0 + sc)
        dxv = r * (dxh - xh * jnp.mean(dxh * xh, axis=-1, keepdims=True))
        return dxv, [_colsum(dh), _colsum(dh * xh * g), _colsum(dh * xh * (1.0 + sc))]

    def norm2_bwd(rows, vecs):
        dhv, x1v, dov, av = rows
        g, sc, gt = vecs
        dxv, sums = norm_bwd(dhv, x1v, g, sc)
        dx1v = dov + dxv
        return [dx1v, dx1v * gt], sums + [_colsum(dx1v * av)]

    dx1, datt, dsh2, dsc2, dg2, dgt1 = _rowmap(norm2_bwd, [dh2, x1, dout, att], [g_norm2 + tok_ffn, sc2, gt1],
                                                [(d, F32), (d, MXU_DTYPE)], [d] * 4, name="norm2_bwd")
    dw_o = _mm(merged, datt, ta=True, out_dtype=WIRE_DTYPE, name="wgrad_o")
    dmerged = _mm(datt, w_o_f, tb=True, out_dtype=ACT_DTYPE, name="dgrad_o")

    def gate_bwd(rows, vecs):
        dm, gv, pav, pbv = rows
        sg = jax.nn.sigmoid(gv + vecs[0])
        ga, gb = sg[:, :d], sg[:, d:]
        dgp = jnp.concatenate([dm * pav * ga * (1.0 - ga), dm * pbv * gb * (1.0 - gb)], axis=1)
        return [dm * ga, dm * gb, dgp], [_colsum(dgp)]

    dpa, dpb, dgates, db_gate = _rowmap(gate_bwd, [dmerged, gates, pa, pb], [b_gate],
                                        [(d, MXU_DTYPE), (d, MXU_DTYPE), (2 * d, MXU_DTYPE)], [2 * d], name="gate_bwd")
    dw_pa_t = _mm(dpa, o_a, ta=True, out_dtype=WIRE_DTYPE, name="wgrad_proj_a")
    dw_pb_t = _mm(dpb, o_b, ta=True, out_dtype=WIRE_DTYPE, name="wgrad_proj_b")
    g_out = [dw_pa_t, dw_pb_t, dw_o]
    h_out, tok_out = _split_start(g_out, [land7(a) for a in g_out], gather=False, name="exchange_out_start")
    do_a = _mm(dpa, w_pa_t, name="dgrad_proj_a")
    do_b = _mm(dpb, w_pb_t, name="dgrad_proj_b")

    def delta_a(rows, vecs):
        return [_headsum(rows[0] * rows[1], vecs[0])], []

    (dterm_a,) = _rowmap(delta_a, [do_a, o_a], [ones + tok_out.astype(ones.dtype)], [(WA, F32)], [], name="na_delta")
    dqa, dka, dva, dbias = _attn_bwd(qa, ka, va, do_a, dterm_a, lse_a, kind="na", bias=bias_tab, name="na_bwd")
    g_rpb = _na_bias_grad(dbias)

    def merge_bwd(rows, vecs):
        dob, o0, o1, o2, l0, l1, l2 = rows
        on = vecs[0]
        mx = jnp.maximum(jnp.maximum(l0, l1), l2)
        e0, e1, e2 = jnp.exp(l0 - mx), jnp.exp(l1 - mx), jnp.exp(l2 - mx)
        s = e0 + e1 + e2
        ws = [e0 / s, e1 / s, e2 / s]
        dws = [_headsum(dob * o, on) for o in (o0, o1, o2)]
        mean = ws[0] * dws[0] + ws[1] * dws[1] + ws[2] * dws[2]
        return [w * dob for w in ws] + [w * mean for w in ws], []

    mb = _rowmap(merge_bwd, [do_b] + [gr["o"] for gr in grp] + [gr["lse"] for gr in grp], [ones],
                 [(WB_OUT, F32)] * 6, [], name="dil_merge_bwd")
    dqb, dkb, dvb = [], [], []
    for g, gr in enumerate(grp):
        dil = gr["dil"]
        dq, dk, dv = _attn_bwd(gr["q"], gr["k"], gr["v"], _to_class(mb[g], dil), _to_class(mb[3 + g], dil), gr["lse_c"],
                               kind="dil", seg=t // dil, name=f"dil_bwd{g}")
        dqb.append(_from_class(dq, dil))
        dkb.append(_from_class(dk[0], dil))
        dvb.append(_from_class(dv[0], dil))

    def qk_bwd(width, rotate, nparts):
        def fn(rows, vecs):
            gq, gk, on = vecs
            xv = rows[0]
            pos = 1
            if rotate:
                rc, rlo, rhi = rows[1:4]
                pos = 4
            cat = lambda parts: parts[0] if len(parts) == 1 else jnp.concatenate(parts, axis=1)
            ends = np.cumsum((pos,) + nparts)
            dq, dk, dv = [cat(rows[ends[i]:ends[i + 1]]) for i in range(3)]
            outs, sums = [], []
            for i, (dy, g) in enumerate(((dq, gq), (dk, gk))):
                if rotate:
                    dy = _rot(dy, rc, -rlo, -rhi)
                xi = xv[:, i * width:(i + 1) * width]
                r = lax.rsqrt(_headsum(xi * xi, on) * (1.0 / HEAD_DIM) + EPS)
                xh = xi * r
                dxh = dy * g
                outs.append(r * (dxh - xh * (_headsum(dxh * xh, on) * (1.0 / HEAD_DIM))))
                sums.append(_colsum(dy * xh))
            return [jnp.concatenate(outs + [dv], axis=1)], sums
        return fn

    dqkv_a, dg_qa, dg_ka = _rowmap(qk_bwd(WA, False, (1, len(dka), len(dva))), [qkv_a, dqa] + dka + dva,
                                   [tile_g(g_qa, NA_HEADS), tile_g(g_ka, NA_HEADS), ones],
                                   [(3 * WA, MXU_DTYPE)], [WA, WA], name="qknorm_a_bwd", tm=128)
    dqkv_b, dg_qb, dg_kb = _rowmap(qk_bwd(WB, True, (3, 3, 3)), [qkv_b, rot_c, rot_lo, rot_hi] + dqb + dkb + dvb,
                                   [tile_g(g_qb, DIL_HEADS), tile_g(g_kb, DIL_HEADS), ones],
                                   [(3 * WB, MXU_DTYPE)], [WB, WB], name="qknorm_b_bwd", tm=128)

    dw_in_t = jnp.concatenate([
        _mm(dqkv_a, h, ta=True, out_dtype=WIRE_DTYPE, name="wgrad_in_a"),
        _mm(dqkv_b, h, ta=True, out_dtype=WIRE_DTYPE, name="wgrad_in_b"),
        _mm(dgates, h, ta=True, out_dtype=WIRE_DTYPE, name="wgrad_in_gates")], axis=0)
    h_in, tok_in = _split_start([dw_in_t], [land7(dw_in_t)], gather=False, name="exchange_in_start")
    dh = _mm(dqkv_a, w_in_t, b_rows=(0, n_a), name="dgrad_in_a")
    dh = _mm(dqkv_b, w_in_t, b_rows=(n_a, n_b), add=dh, name="dgrad_in_b")
    dh = _mm(dgates, w_in_t, b_rows=(n_a + n_b, 2 * d), add=dh, name="dgrad_in_gates")

    def norm1_bwd(rows, vecs):
        dhv, xv, dx1v = rows
        dxv, sums = norm_bwd(dhv, xv, vecs[0], vecs[1])
        return [dx1v + dxv], sums

    grad_x, dsh1, dsc1, dg1 = _rowmap(norm1_bwd, [dh, xt, dx1], [g_norm1 + tok_in, sc1], [(d, F32)], [d] * 3, name="norm1_bwd")

    heads_sum = lambda a, heads: a.reshape(heads, HEAD_DIM).sum(axis=0)
    dmod = jnp.concatenate([dsh1, dsc1, dgt1, dsh2, dsc2, dgt2], axis=1)
    local_small = _pack_small(dict(
        b_ada=dmod, g_norm1=dg1, g_norm2=dg2, b_gate=db_gate, g_qa=heads_sum(dg_qa, NA_HEADS),
        g_ka=heads_sum(dg_ka, NA_HEADS), g_qb=heads_sum(dg_qb, DIL_HEADS), g_kb=heads_sum(dg_kb, DIL_HEADS),
        rpb=g_rpb, loss=(0.5 / d) * jnp.sum(err2)))
    srows = local_small.shape[0]
    small_all = _all_gather([local_small], name="gather_small")[0].reshape(N_DEV, srows, LANES)
    small_sum = _sum8(small_all[0], small_all[1:], name="sum_small")
    small_shapes = dict(b_ada=b_ada.shape, g_norm1=g_norm1.shape, g_norm2=g_norm2.shape, b_gate=b_gate.shape,
                        g_qa=g_qa.shape, g_ka=g_ka.shape, g_qb=g_qb.shape, g_kb=g_kb.shape, rpb=rpb.shape, loss=())
    small_w = dict(b_ada=b_ada, g_norm1=g_norm1, g_norm2=g_norm2, b_gate=b_gate, g_qa=g_qa, g_ka=g_ka, g_qb=g_qb,
                   g_kb=g_kb, rpb=rpb, loss=jnp.zeros((), F32))
    small_m = dict(b_ada=m_b_ada, g_norm1=m_g_norm1, g_norm2=m_g_norm2, b_gate=m_b_gate, g_qa=m_g_qa, g_ka=m_g_ka,
                   g_qb=m_g_qb, g_kb=m_g_kb, rpb=m_rpb, loss=jnp.zeros((), F32))
    small_v = dict(b_ada=v_b_ada, g_norm1=v_g_norm1, g_norm2=v_g_norm2, b_gate=v_b_gate, g_qa=v_g_qa, g_ka=v_g_ka,
                   g_qb=v_g_qb, g_kb=v_g_kb, rpb=v_rpb, loss=jnp.zeros((), F32))
    s_delta, s_m, s_v = _adamw(_pack_small(small_w), small_sum, _pack_small(small_m), _pack_small(small_v), name="adamw_small")
    gs = _unpack_small(small_sum, small_shapes)
    ds_, ms_, vs_ = [_unpack_small(a, small_shapes) for a in (s_delta, s_m, s_v)]

    dmod_all = small_all[:, :6 * d // LANES].reshape(N_DEV, 6 * d)
    dmod_mine = jnp.pad(lax.dynamic_slice(dmod_all, (0, me * ncol), (N_DEV, ncol)), ((0, LANES - N_DEV), (0, 0)))

    def wada_body(c_ref, dm_ref, o_ref):
        chi, cmid, clo = _split3(c_ref[...])
        dhi, dmid, dlo = _split3(dm_ref[...])
        o_ref[...] = (_dot_tn(chi, dhi) + (_dot_tn(chi, dmid) + _dot_tn(cmid, dhi))
                      + (_dot_tn(chi, dlo) + _dot_tn(cmid, dmid) + _dot_tn(clo, dhi)))

    g_w_ada = pl.pallas_call(
        wada_body,
        name="wgrad_ada",
        out_shape=jax.ShapeDtypeStruct((d, ncol), F32),
        compiler_params=pltpu.CompilerParams(vmem_limit_bytes=_vmem(4 * d * ncol * 4)),
    )(c_act, dmod_mine)

    sent, recv = _split_wait(h_in + h_ffn + h_out, small_sum, gather=False, name="exchange_wait")
    names = ("w_in", "w_ffn_in", "w_ffn_out", "w_proj_a", "w_proj_b", "w_o")
    transposed = (True, True, False, True, True, False)
    big_g = {}
    for nme, own, r, tr in zip(names, sent, recv, transposed):
        s = _sum8(own_block(own), r, name=f"sum_{nme}")
        big_g[nme] = s.T if tr else s
    big_g["w_ada"] = g_w_ada
    big_w = dict(w_ada=w_ada, w_in=w_in, w_proj_a=w_proj_a, w_proj_b=w_proj_b, w_o=w_o, w_ffn_in=w_ffn_in, w_ffn_out=w_ffn_out)
    big_m = dict(w_ada=m_w_ada, w_in=m_w_in, w_proj_a=m_w_proj_a, w_proj_b=m_w_proj_b, w_o=m_w_o, w_ffn_in=m_w_ffn_in, w_ffn_out=m_w_ffn_out)
    big_v = dict(w_ada=v_w_ada, w_in=v_w_in, w_proj_a=v_w_proj_a, w_proj_b=v_w_proj_b, w_o=v_w_o, w_ffn_in=v_w_ffn_in, w_ffn_out=v_w_ffn_out)
    grads, deltas, new_m, new_v = {}, {}, {}, {}
    for nme in big_w:
        dl, m1, v1 = _adamw(big_w[nme][0], big_g[nme], big_m[nme][0], big_v[nme][0], name=f"adamw_{nme}")
        grads[nme], deltas[nme], new_m[nme], new_v[nme] = big_g[nme][None], dl[None], m1[None], v1[None]
    for nme in _SMALL[:-1]:
        grads[nme], deltas[nme], new_m[nme], new_v[nme] = gs[nme], ds_[nme], ms_[nme], vs_[nme]

    order = ("w_ada", "b_ada", "g_norm1", "g_norm2", "w_in", "b_gate", "g_qa", "g_ka", "g_qb", "g_kb", "rpb",
             "w_proj_a", "w_proj_b", "w_o", "w_ffn_in", "w_ffn_out")
    return (gs["loss"], grad_x[None], *[grads[n] for n in order], *[deltas[n] for n in order],
            *[new_m[n] for n in order], *[new_v[n] for n in order])
```

```python
import functools

import numpy as np
import jax
import jax.numpy as jnp
from jax import lax
from jax.experimental import pallas as pl
from jax.experimental.pallas import tpu as pltpu

F32 = jnp.float32
MXU_DTYPE = jnp.bfloat16
WIRE_DTYPE = jnp.bfloat16
ACT_DTYPE = jnp.bfloat16

HEAD_DIM = 64
GRID_W = 64
NA_HEADS = 8
NA_KH = 8
NA_KW = 16
DIL_CONFIGS = ((128, 1), (512, 4), (2048, 16))
DIL_HEADS_PER_GROUP = 4
DIL_HEADS = DIL_HEADS_PER_GROUP * len(DIL_CONFIGS)
DIL_HALF = 64
ROT_DIM = HEAD_DIM // 4
ROPE_THETA = 500000.0
EPS = 1e-6
NEG_INF = -1e30
WA = NA_HEADS * HEAD_DIM
WB = DIL_HEADS * HEAD_DIM
WB_OUT = DIL_HEADS_PER_GROUP * HEAD_DIM
ADAM_LR = 0.001
ADAM_B1 = 0.9
ADAM_B2 = 0.999
ADAM_EPS = 1e-08
ADAM_WD = 0.01
ADAM_STEP = 10

N_DEV = 8
LANES = 128
VMEM_CAP = 60 * 2**20
VMEM_FLOOR = 56 * 2**20
MESH = pl.DeviceIdType.MESH
ANY = pl.BlockSpec(memory_space=pl.ANY)


def _vmem(nbytes):
    return int(min(VMEM_CAP, max(VMEM_FLOOR, nbytes * 5 // 4 + 4 * 2**20)))


def _pick(dim, cands):
    for c in cands:
        if c <= dim and dim % c == 0:
            return c
    return dim


def _nbytes(shape, dtype):
    return int(np.prod(shape)) * jnp.dtype(dtype).itemsize


def _dot(a, b, dims=((1,), (0,))):
    return lax.dot_general(a.astype(MXU_DTYPE), b.astype(MXU_DTYPE), (dims, ((), ())), preferred_element_type=F32)


def _dot_nt(a, b):
    return _dot(a, b, ((1,), (1,)))


def _dot_tn(a, b):
    return _dot(a, b, ((0,), (0,)))


def _split3(a):
    hi = a.astype(jnp.bfloat16)
    r1 = a - hi.astype(F32)
    mid = r1.astype(jnp.bfloat16)
    lo = (r1 - mid.astype(F32)).astype(jnp.bfloat16)
    return hi, mid, lo


def _silu(x):
    return x * jax.nn.sigmoid(x)


def _divisors(dim, unit):
    return [c for c in range(unit, dim + 1, unit) if dim % c == 0] or [dim]


def _mm_tiles(m, n, kdim, a_item, b_item, o_item, has_add):
    step_us, hbm_bytes_per_us, flops_per_us, budget = 0.35, 3.0e6, 8.0e8, 40 * 2**20
    best = None
    for tm in _divisors(m, LANES):
        for tn in _divisors(n, LANES):
            for tk in _divisors(kdim, LANES):
                gm, gn, gk = m // tm, n // tn, kdim // tk
                vmem = 2 * (tm * tk * a_item + tk * tn * b_item + tm * tn * o_item) + 2 * (tm * tk + tk * tn)
                vmem += tm * tn * 4 * ((1 if gk > 1 else 0) + 1 + (2 if has_add else 0))
                if vmem > budget:
                    continue
                a_reads = m * kdim * a_item * (gn if gk > 1 else 1)
                traffic = a_reads + kdim * n * b_item * gm + m * n * (o_item + (4 if has_add else 0))
                cost = gm * gn * gk * step_us + max(traffic / hbm_bytes_per_us, 2.0 * m * n * kdim / flops_per_us)
                if best is None or cost < best[0]:
                    best = (cost, tm, tn, tk)
    return best[1:]


def _mm(a, b, *, name, ta=False, tb=False, out_dtype=F32, add=None):
    if ta:
        kdim, m = a.shape
    else:
        m, kdim = a.shape
    n = b.shape[0] if tb else b.shape[1]
    assert b.shape[1 if tb else 0] == kdim
    tm, tn, tk = _mm_tiles(m, n, kdim, a.dtype.itemsize, b.dtype.itemsize, jnp.dtype(out_dtype).itemsize, add is not None)
    gm, gn, gk = m // tm, n // tn, kdim // tk

    a_spec = pl.BlockSpec((tk, tm), lambda i, j, k: (k, i)) if ta else pl.BlockSpec((tm, tk), lambda i, j, k: (i, k))
    b_spec = pl.BlockSpec((tn, tk), lambda i, j, k: (j, k)) if tb else pl.BlockSpec((tk, tn), lambda i, j, k: (k, j))
    o_spec = pl.BlockSpec((tm, tn), lambda i, j, k: (i, j))
    in_specs = [a_spec, b_spec]
    operands = [a, b]
    if add is not None:
        in_specs.append(o_spec)
        operands.append(add)
    a_dims = (0,) if ta else (1,)
    b_dims = (1,) if tb else (0,)

    def body(*refs):
        a_ref, b_ref = refs[:2]
        add_ref = refs[2] if add is not None else None
        o_ref = refs[3] if add is not None else refs[2]

        def finish(r):
            if add is not None:
                r = r + add_ref[...].astype(F32)
            o_ref[...] = r.astype(o_ref.dtype)

        if gk == 1:
            finish(_dot(a_ref[...], b_ref[...], (a_dims, b_dims)))
            return
        acc_ref = refs[-1]
        k = pl.program_id(2)

        @pl.when(k == 0)
        def _():
            acc_ref[...] = jnp.zeros_like(acc_ref)

        acc_ref[...] += _dot(a_ref[...], b_ref[...], (a_dims, b_dims))

        @pl.when(k == gk - 1)
        def _():
            finish(acc_ref[...])

    est = 2 * (tm * tk * a.dtype.itemsize + tk * tn * b.dtype.itemsize + tm * tn * jnp.dtype(out_dtype).itemsize)
    est += tm * tn * 4 * (3 if add is not None else 1) + 2 * (tm * tk + tk * tn) * 2
    return pl.pallas_call(
        body,
        name=name,
        grid=(gm, gn, gk),
        in_specs=in_specs,
        out_specs=o_spec,
        out_shape=jax.ShapeDtypeStruct((m, n), out_dtype),
        scratch_shapes=[pltpu.VMEM((tm, tn), F32)] if gk > 1 else [],
        compiler_params=pltpu.CompilerParams(
            dimension_semantics=("parallel", "parallel", "arbitrary"), vmem_limit_bytes=_vmem(est)
        ),
    )(*operands)


def _mm_rows(a, b, fn, rows, vecs, outs, reds, *, name, tb=False):
    m, kdim = a.shape
    n = b.shape[0] if tb else b.shape[1]
    nr, nv, no = len(rows), len(vecs), len(outs)
    row_bytes = sum(r.shape[1] * r.dtype.itemsize for r in rows) + sum(w * jnp.dtype(dt).itemsize for (w, dt) in outs)
    best = None
    for tm in _divisors(m, LANES):
        for tk in _divisors(kdim, LANES):
            gm, gk = m // tm, kdim // tk
            vmem = 2 * (tm * tk * a.dtype.itemsize + tk * n * b.dtype.itemsize + tm * row_bytes) + tm * n * 4 * 5
            if vmem > 44 * 2**20:
                continue
            traffic = m * kdim * a.dtype.itemsize + kdim * n * b.dtype.itemsize * (gm if gk > 1 else 1) + m * row_bytes
            cost = gm * gk * 0.35 + max(traffic / 3.0e6, 2.0 * m * n * kdim / 8.0e8)
            if best is None or cost < best[0]:
                best = (cost, tm, tk, vmem)
    _, tm, tk, est = best
    gm, gk = m // tm, kdim // tk
    b_dims = (1,) if tb else (0,)

    def body(*refs):
        a_ref, b_ref = refs[:2]
        row_refs, vec_refs = refs[2:2 + nr], refs[2 + nr:2 + nr + nv]
        out_refs = refs[2 + nr + nv:2 + nr + nv + no]
        red_refs = refs[2 + nr + nv + no:2 + nr + nv + no + len(reds)]

        def finish(r):
            o, rd = fn(r, [x[...].astype(F32) for x in row_refs], [v[...] for v in vec_refs])
            for ref, val in zip(out_refs, o):
                ref[...] = val.astype(ref.dtype)
            if red_refs:
                @pl.when(pl.program_id(0) == 0)
                def _():
                    for ref in red_refs:
                        ref[...] = jnp.zeros_like(ref)

                for ref, val in zip(red_refs, rd):
                    ref[...] += val

        if gk == 1:
            finish(_dot(a_ref[...], b_ref[...], ((1,), b_dims)))
            return
        acc_ref = refs[-1]
        k = pl.program_id(1)

        @pl.when(k == 0)
        def _():
            acc_ref[...] = jnp.zeros_like(acc_ref)

        acc_ref[...] += _dot(a_ref[...], b_ref[...], ((1,), b_dims))

        @pl.when(k == gk - 1)
        def _():
            finish(acc_ref[...])

    in_specs = [pl.BlockSpec((tm, tk), lambda i, k: (i, k)),
                pl.BlockSpec((n, tk), lambda i, k: (0, k)) if tb else pl.BlockSpec((tk, n), lambda i, k: (k, 0))]
    in_specs += [pl.BlockSpec((tm, r.shape[1]), lambda i, k: (i, 0)) for r in rows]
    in_specs += [pl.BlockSpec(v.shape, functools.partial(lambda nd, i, k: (0,) * nd, v.ndim)) for v in vecs]
    out_specs = [pl.BlockSpec((tm, w), lambda i, k: (i, 0)) for (w, _) in outs]
    out_specs += [pl.BlockSpec((1, w), lambda i, k: (0, 0)) for w in reds]
    out_shape = [jax.ShapeDtypeStruct((m, w), dt) for (w, dt) in outs] + [jax.ShapeDtypeStruct((1, w), F32) for w in reds]
    return pl.pallas_call(
        body,
        name=name,
        grid=(gm, gk),
        in_specs=in_specs,
        out_specs=out_specs,
        out_shape=out_shape,
        scratch_shapes=[pltpu.VMEM((tm, n), F32)] if gk > 1 else [],
        compiler_params=pltpu.CompilerParams(dimension_semantics=("arbitrary", "arbitrary"), vmem_limit_bytes=_vmem(est)),
    )(a, b, *rows, *vecs)


def _rowmap(fn, rows, vecs, outs, reds, *, name, tm=None):
    rows = [r if isinstance(r, tuple) else (r, r.shape[1], 0) for r in rows]
    t = rows[0][0].shape[0]
    if tm is None:
        per_row = 2 * sum(w * a.dtype.itemsize for (a, w, _) in rows) + 2 * sum(w * jnp.dtype(d).itemsize for (w, d) in outs)
        per_row += 6 * 4 * max([w for (_, w, _) in rows] + [w for (w, _) in outs])
        tm = max(8, min(1024, (36 * 2**20) // per_row))
    tm = _pick(t, tuple(c for c in (1024, 512, 256, 128, 64, 32, 16, 8) if c <= tm))
    nr, nv, no = len(rows), len(vecs), len(outs)

    def body(*refs):
        row_refs, vec_refs = refs[:nr], refs[nr:nr + nv]
        out_refs, red_refs = refs[nr + nv:nr + nv + no], refs[nr + nv + no:]
        o, rd = fn([r[...].astype(F32) for r in row_refs], [v[...] for v in vec_refs])
        for ref, val in zip(out_refs, o):
            ref[...] = val.astype(ref.dtype)
        if red_refs:
            @pl.when(pl.program_id(0) == 0)
            def _():
                for ref in red_refs:
                    ref[...] = jnp.zeros_like(ref)

            for ref, val in zip(red_refs, rd):
                ref[...] += val

    in_specs = [pl.BlockSpec((tm, w), functools.partial(lambda cb, i: (i, cb), cb)) for (_, w, cb) in rows]
    in_specs += [pl.BlockSpec(v.shape, functools.partial(lambda nd, i: (0,) * nd, v.ndim)) for v in vecs]
    out_specs = [pl.BlockSpec((tm, w), lambda i: (i, 0)) for (w, _) in outs]
    out_specs += [pl.BlockSpec((1, w), lambda i: (0, 0)) for w in reds]
    out_shape = [jax.ShapeDtypeStruct((t, w), d) for (w, d) in outs]
    out_shape += [jax.ShapeDtypeStruct((1, w), F32) for w in reds]
    est = 2 * sum(tm * w * a.dtype.itemsize for (a, w, _) in rows) + 2 * sum(_nbytes(v.shape, v.dtype) for v in vecs)
    est += 2 * sum(tm * w * jnp.dtype(d).itemsize for (w, d) in outs)
    est += 4 * tm * max([w for (_, w, _) in rows] + [w for (w, _) in outs]) * 4
    return pl.pallas_call(
        body,
        name=name,
        grid=(t // tm,),
        in_specs=in_specs,
        out_specs=out_specs,
        out_shape=out_shape,
        compiler_params=pltpu.CompilerParams(dimension_semantics=("arbitrary",), vmem_limit_bytes=_vmem(est)),
    )(*[r[0] for r in rows], *vecs)


def _colsum(v):
    return jnp.sum(v, axis=0, keepdims=True)


def _head_ones():
    i = np.arange(LANES)
    return jnp.asarray((i[:, None] // HEAD_DIM) == (i[None, :] // HEAD_DIM), MXU_DTYPE)


def _headsum(y, ones):
    parts = []
    for j in range(y.shape[1] // LANES):
        c = y[:, j * LANES:(j + 1) * LANES]
        hi = c.astype(MXU_DTYPE)
        lo = c - hi.astype(F32)
        parts.append(_dot(hi, ones) + _dot(lo, ones))
    return parts[0] if len(parts) == 1 else jnp.concatenate(parts, axis=1)


def _rot(y, c, s_lo, s_hi):
    parts = []
    for j in range(y.shape[1] // LANES):
        yc = y[:, j * LANES:(j + 1) * LANES]
        parts.append(yc * c + pltpu.roll(yc, LANES - ROT_DIM // 2, 1) * s_lo + pltpu.roll(yc, ROT_DIM // 2, 1) * s_hi)
    return parts[0] if len(parts) == 1 else jnp.concatenate(parts, axis=1)


def _rot_tables(t):
    half = ROT_DIM // 2
    inv_freq = ROPE_THETA ** (-(jnp.arange(half, dtype=F32) * 2.0) / ROT_DIM)
    ang = jnp.arange(t).astype(F32)[:, None] * inv_freq[None, :]
    cos, sin = jnp.cos(ang), jnp.sin(ang)
    z = lambda w: jnp.zeros((t, w), F32)
    c = jnp.concatenate([cos, cos, jnp.ones((t, HEAD_DIM - ROT_DIM), F32)], axis=1)
    s_lo = jnp.concatenate([-sin, z(HEAD_DIM - half)], axis=1)
    s_hi = jnp.concatenate([z(half), sin, z(HEAD_DIM - ROT_DIM)], axis=1)
    return [jnp.tile(a, (1, LANES // HEAD_DIM)) for a in (c, s_lo, s_hi)]


def _rms(x):
    return lax.rsqrt(jnp.mean(x * x, axis=-1, keepdims=True) + EPS)


def _window(kind, n, bq, t, seg):
    if kind == "na":
        rows = t // GRID_W
        rs = jnp.clip(n - NA_KH // 2, 0, rows - NA_KH)
        return rs
    nk = bq + 2 * DIL_HALF
    return jnp.clip(n * bq - DIL_HALF, 0, t - nk)


def _dil_mask(n, bq, nk, ws, seg):
    qi = n * bq + lax.broadcasted_iota(jnp.int32, (bq, nk), 0)
    ki = ws + lax.broadcasted_iota(jnp.int32, (bq, nk), 1)
    shift = int(np.log2(seg))
    return (jnp.abs(ki - qi) <= DIL_HALF) & ((ki >> shift) == (qi >> shift))


HS = 4
QW = HS * HEAD_DIM


def _head_of_lane(width=QW):
    return lax.broadcasted_iota(jnp.int32, (1, width), 1) // HEAD_DIM


def _stack_heads(a):
    head = _head_of_lane()
    return jnp.concatenate([jnp.where(head == e, a, jnp.zeros_like(a)) for e in range(HS)], axis=0)


def _unstack_heads(a, bq):
    head = _head_of_lane()
    out = jnp.zeros((bq, QW), a.dtype)
    for e in range(HS):
        out = jnp.where(head == e, a[e * bq:(e + 1) * bq], out)
    return out


def _stack_cols(blk, bq):
    head = _head_of_lane()
    return jnp.concatenate(
        [jnp.max(jnp.where(head == e, blk, -jnp.inf), axis=1, keepdims=True) for e in range(HS)], axis=0)


def _attn_geometry(kind):
    if kind == "na":
        return GRID_W, NA_KH * GRID_W, 4
    bq = 128
    return bq, bq + 2 * DIL_HALF, 2


def _attn_scores(kind, n, bq, nk, t, seg, qs, k_ref, b_ref):
    scale = HEAD_DIM ** -0.5
    if kind == "na":
        rs = _window(kind, n, bq, t, seg)
        ws = pl.multiple_of(rs * GRID_W, GRID_W)
        ro0 = rs - n + (NA_KH - 1)
        s = _dot_nt(qs, k_ref[pl.ds(ws, nk), :]) * scale
        s = s + jnp.concatenate(
            [jnp.concatenate([b_ref[e, ro0 + 2 * i] for i in range(NA_KH // 2)], axis=1) for e in range(HS)], axis=0)
        return s, ws, ro0
    ws = pl.multiple_of(_window(kind, n, bq, t, seg), DIL_HALF)
    mask = _dil_mask(n, bq, nk, ws, seg)
    s = _dot_nt(qs, k_ref[pl.ds(ws, nk), :]) * scale
    s = jnp.where(jnp.concatenate([mask] * HS, axis=0), s, NEG_INF)
    return s, ws, None


def _attn_fwd(q, k, v, *, kind, name, bias=None, seg=None):
    t, w = q.shape
    quads = w // QW
    bq, nk, sub = _attn_geometry(kind)
    nq = t // (bq * sub)

    def body(*refs):
        if kind == "na":
            q_ref, k_ref, v_ref, b_ref, o_ref, l_ref = refs
        else:
            (q_ref, k_ref, v_ref, o_ref, l_ref), b_ref = refs, None
        for i in range(sub):
            n = pl.program_id(1) * sub + i
            rows = slice(i * bq, (i + 1) * bq)
            s, ws, _ = _attn_scores(kind, n, bq, nk, t, seg, _stack_heads(q_ref[rows, :]), k_ref, b_ref)
            m = jnp.max(s, axis=1, keepdims=True)
            p = jnp.exp(s - m)
            l = jnp.sum(p, axis=1, keepdims=True)
            o_ref[rows, :] = _unstack_heads(_dot(p / l, v_ref[pl.ds(ws, nk), :]), bq)
            l_ref[rows, :] = _unstack_heads(jnp.broadcast_to(m + jnp.log(l), (HS * bq, QW)), bq)

    blk = pl.BlockSpec((bq * sub, QW), lambda j, n: (n, j))
    res = pl.BlockSpec((t, QW), lambda j, n: (0, j))
    in_specs = [blk, res, res]
    operands = [q, k, v]
    est = 4 * t * QW * q.dtype.itemsize + 12 * sub * HS * bq * nk * 4
    if kind == "na":
        in_specs.append(pl.BlockSpec((HS,) + bias.shape[1:], lambda j, n: (j, 0, 0, 0)))
        operands.append(bias)
        est += 2 * _nbytes((HS,) + bias.shape[1:], F32)
    return pl.pallas_call(
        body,
        name=name,
        grid=(quads, nq),
        in_specs=in_specs,
        out_specs=[blk, blk],
        out_shape=[jax.ShapeDtypeStruct((t, w), F32)] * 2,
        compiler_params=pltpu.CompilerParams(dimension_semantics=("arbitrary", "arbitrary"), vmem_limit_bytes=_vmem(est)),
    )(*operands)


def _attn_bwd(q, k, v, do, dterm, lse, *, kind, name, bias=None, seg=None):
    t, w = q.shape
    quads = w // QW
    bq, nk, sub = _attn_geometry(kind)
    nq = t // (bq * sub)
    scale = HEAD_DIM ** -0.5

    def body(*refs):
        if kind == "na":
            q_ref, k_ref, v_ref, do_ref, dt_ref, l_ref, b_ref, dq_ref, dk_hbm, dv_hbm, db_ref, dk_acc, dv_acc, sem = refs
        else:
            q_ref, k_ref, v_ref, do_ref, dt_ref, l_ref, dq_ref, dk_hbm, dv_hbm, dk_acc, dv_acc, sem = refs
            b_ref = None
        j, step = pl.program_id(0), pl.program_id(1)

        @pl.when(step == 0)
        def _():
            dk_acc[...] = jnp.zeros_like(dk_acc)
            dv_acc[...] = jnp.zeros_like(dv_acc)
            if kind == "na":
                db_ref[...] = jnp.zeros_like(db_ref)

        for b in range(sub):
            n = step * sub + b
            rows = slice(b * bq, (b + 1) * bq)
            qs = _stack_heads(q_ref[rows, :])
            dos = _stack_heads(do_ref[rows, :])
            s, ws, ro0 = _attn_scores(kind, n, bq, nk, t, seg, qs, k_ref, b_ref)
            p = jnp.exp(s - _stack_cols(l_ref[rows, :], bq))
            dp = _dot_nt(dos, v_ref[pl.ds(ws, nk), :])
            ds = p * (dp - _stack_cols(dt_ref[rows, :], bq))
            if kind == "na":
                for e in range(HS):
                    for i in range(NA_KH // 2):
                        db_ref[e, ro0 + 2 * i] += ds[e * bq:(e + 1) * bq, i * LANES:(i + 1) * LANES]
            dsc = ds * scale
            dq_ref[rows, :] = _unstack_heads(_dot(dsc, k_ref[pl.ds(ws, nk), :]), bq)
            dk_acc[pl.ds(ws, nk), :] += _dot_tn(dsc, qs)
            dv_acc[pl.ds(ws, nk), :] += _dot_tn(p, dos)

        @pl.when(step == nq - 1)
        def _():
            ck = pltpu.make_async_copy(dk_acc, dk_hbm.at[j], sem.at[0])
            cv = pltpu.make_async_copy(dv_acc, dv_hbm.at[j], sem.at[1])
            ck.start()
            cv.start()
            ck.wait()
            cv.wait()

    blk = pl.BlockSpec((bq * sub, QW), lambda j, n: (n, j))
    res = pl.BlockSpec((t, QW), lambda j, n: (0, j))
    in_specs = [blk, res, res, blk, blk, blk]
    operands = [q, k, v, do, dterm, lse]
    out_specs = [blk, ANY, ANY]
    out_shape = [jax.ShapeDtypeStruct((t, w), F32)] + [jax.ShapeDtypeStruct((quads, t, QW), F32)] * 2
    est = 4 * t * QW * q.dtype.itemsize + 2 * t * QW * 4 + 16 * sub * HS * bq * nk * 4
    if kind == "na":
        bspec = pl.BlockSpec((HS,) + bias.shape[1:], lambda j, n: (j, 0, 0, 0))
        in_specs.append(bspec)
        operands.append(bias)
        out_specs.append(bspec)
        out_shape.append(jax.ShapeDtypeStruct(bias.shape, F32))
        est += 4 * _nbytes((HS,) + bias.shape[1:], F32)
    res_ = pl.pallas_call(
        body,
        name=name,
        grid=(quads, nq),
        in_specs=in_specs,
        out_specs=out_specs,
        out_shape=out_shape,
        scratch_shapes=[pltpu.VMEM((t, QW), F32), pltpu.VMEM((t, QW), F32), pltpu.SemaphoreType.DMA((2,))],
        compiler_params=pltpu.CompilerParams(dimension_semantics=("arbitrary", "arbitrary"), vmem_limit_bytes=_vmem(est)),
    )(*operands)
    unquad = lambda a: [a[i] for i in range(quads)]
    return (res_[0], unquad(res_[1]), unquad(res_[2])) + tuple(res_[3:])


def _na_onehot():
    qc = np.arange(GRID_W)[:, None]
    kc = np.arange(GRID_W)[None, :]
    start = np.clip(qc - NA_KW // 2, 0, GRID_W - NA_KW)
    inwin = (kc >= start) & (kc < start + NA_KW)
    off = kc - qc + (NA_KW - 1)
    e_mat = np.zeros((2, 32, GRID_W, 2, GRID_W), np.float32)
    for e in range(2):
        for c in range(2 * NA_KW - 1):
            e_mat[e, c, :, e, :] = (off == c) & inwin
    neg = np.where(inwin, 0.0, NEG_INF).astype(np.float32)
    neg = np.broadcast_to(neg[:, None, :], (GRID_W, 2, GRID_W)).reshape(1, GRID_W * LANES)
    return jnp.asarray(e_mat.reshape(64, GRID_W * LANES), MXU_DTYPE), jnp.asarray(neg)


def _na_rowpairs(rpb):
    p = jnp.pad(rpb, ((0, 0), (0, 0), (0, 1)))
    return jnp.concatenate([p[:, :-1], p[:, 1:]], axis=-1).reshape(NA_HEADS * (2 * NA_KH - 2), 64)


def _na_bias_table(rpb):
    r2 = _na_rowpairs(rpb)
    e_mat, neg = _na_onehot()

    def body(r_ref, e_ref, n_ref, o_ref):
        hi, mid, lo = _split3(r_ref[...])
        e = e_ref[...]
        o_ref[...] = _dot(hi, e) + _dot(mid, e) + _dot(lo, e) + n_ref[...]

    out = pl.pallas_call(
        body,
        name="na_bias_table",
        out_shape=jax.ShapeDtypeStruct((r2.shape[0], GRID_W * LANES), F32),
        compiler_params=pltpu.CompilerParams(vmem_limit_bytes=_vmem(6 * r2.shape[0] * GRID_W * LANES * 4)),
    )(r2, e_mat, neg)
    return out.reshape(NA_HEADS, 2 * NA_KH - 2, GRID_W, LANES)


def _na_bias_grad(dbt):
    e_mat, _ = _na_onehot()
    flat = dbt.reshape(NA_HEADS * (2 * NA_KH - 2), GRID_W * LANES)

    def body(d_ref, e_ref, o_ref):
        hi, mid, lo = _split3(d_ref[...])
        e = e_ref[...]
        o_ref[...] = _dot_nt(hi, e) + _dot_nt(mid, e) + _dot_nt(lo, e)

    g = pl.pallas_call(
        body,
        name="na_bias_grad",
        out_shape=jax.ShapeDtypeStruct((flat.shape[0], 64), F32),
        compiler_params=pltpu.CompilerParams(vmem_limit_bytes=_vmem(6 * flat.shape[0] * GRID_W * LANES * 4)),
    )(flat, e_mat)
    g = g.reshape(NA_HEADS, 2 * NA_KH - 2, 2, 32)[..., :2 * NA_KW - 1]
    first = jnp.pad(g[:, :, 0], ((0, 0), (0, 1), (0, 0)))
    second = jnp.pad(g[:, :, 1], ((0, 0), (1, 0), (0, 0)))
    return first + second


def _all_gather(arrs, *, name):
    na = len(arrs)

    def body(*refs):
        ins, outs = refs[:na], refs[na:2 * na]
        send_sems, recv_sems, local_sems = refs[2 * na:]
        x, y, c = lax.axis_index("x"), lax.axis_index("y"), lax.axis_index("c")
        me, sibling = (x, y, c), (x, y, 1 - c)
        chips = [(1 - x, y), (x, 1 - y), (1 - x, 1 - y)]

        def rows(a, px, py, pc):
            r = ins[a].shape[0]
            return outs[a].at[pl.ds((4 * px + 2 * py + pc) * r, r), :]

        def copy(a, k, block, to, src=None):
            return pltpu.make_async_remote_copy(
                src_ref=rows(a, *block) if src is None else src, dst_ref=rows(a, *block),
                send_sem=send_sems.at[a, k], recv_sem=recv_sems.at[a, k], device_id=to, device_id_type=MESH)

        mine = [pltpu.make_async_copy(ins[a], rows(a, *me), local_sems.at[a]) for a in range(na)]
        for cp in mine:
            cp.start()
        first = []
        for a in range(na):
            first.append(copy(a, 0, me, sibling, src=ins[a]))
            first += [copy(a, 1 + j, me, (*chip, c), src=ins[a]) for j, chip in enumerate(chips)]
        for cp in first:
            cp.start()
        passed = []
        for j, chip in enumerate(chips):
            for a in range(na):
                copy(a, 1 + j, (*chip, c), me).wait_recv()
                cp = copy(a, 4 + j, (*chip, c), sibling)
                cp.start()
                passed.append(cp)
        for a in range(na):
            copy(a, 0, sibling, me).wait_recv()
        for j, chip in enumerate(chips):
            for a in range(na):
                copy(a, 4 + j, (*chip, 1 - c), me).wait_recv()
        for cp in first + passed:
            cp.wait_send()
        for cp in mine:
            cp.wait()

    return pl.pallas_call(
        body,
        name=name,
        in_specs=[ANY] * na,
        out_specs=[ANY] * na,
        out_shape=[jax.ShapeDtypeStruct((N_DEV * a.shape[0], a.shape[1]), a.dtype) for a in arrs],
        scratch_shapes=[pltpu.SemaphoreType.DMA((na, 7)), pltpu.SemaphoreType.DMA((na, 7)), pltpu.SemaphoreType.DMA((na,))],
    )(*arrs)


HBM = pl.BlockSpec(memory_space=pltpu.HBM)
SEM = pl.BlockSpec(memory_space=pltpu.SEMAPHORE)
EFFECT = pltpu.SideEffectType.DATAFLOW_SIDE_EFFECTING


def _peer_of(k):
    x, y, c = lax.axis_index("x"), lax.axis_index("y"), lax.axis_index("c")
    return x ^ ((k >> 2) & 1), y ^ ((k >> 1) & 1), c ^ (k & 1)


def _split_copies(gather, src_ref, land_ref, send_sems, recv_sems):
    x, y, c = lax.axis_index("x"), lax.axis_index("y"), lax.axis_index("c")
    my = 4 * x + 2 * y + c
    r = src_ref.shape[0] if gather else src_ref.shape[0] // N_DEV
    copies = []
    for k in range(1, N_DEV):
        px, py, pc = _peer_of(k)
        if gather:
            src, dst = src_ref, land_ref.at[pl.ds(my * r, r), :]
        else:
            src, dst = src_ref.at[pl.ds((4 * px + 2 * py + pc) * r, r), :], land_ref.at[k - 1]
        copies.append(pltpu.make_async_remote_copy(
            src_ref=src, dst_ref=dst, send_sem=send_sems.at[k - 1], recv_sem=recv_sems.at[k - 1],
            device_id=(px, py, pc), device_id_type=MESH))
    return copies


def _split_start(srcs, lands, *, gather, name, after=None):
    na = len(srcs)
    extra = [] if after is None else [after]

    def body(*refs):
        src_refs, land_refs = refs[:na], refs[na:2 * na]
        outs = refs[2 * na + len(extra):]
        for a in range(na):
            for cp in _split_copies(gather, src_refs[a], land_refs[a], outs[4 * a], outs[4 * a + 1]):
                cp.start()
        outs[4 * na][...] = jnp.zeros_like(outs[4 * na])

    out_shape, out_specs, aliases = [], [], {}
    for a in range(na):
        out_shape += [pltpu.SemaphoreType.DMA((N_DEV - 1,)), pltpu.SemaphoreType.DMA((N_DEV - 1,)),
                      pltpu.HBM(srcs[a].shape, srcs[a].dtype), pltpu.HBM(lands[a].shape, lands[a].dtype)]
        out_specs += [SEM, SEM, HBM, HBM]
        aliases[a] = 4 * a + 2
        aliases[na + a] = 4 * a + 3
    out_shape.append(jax.ShapeDtypeStruct((8, LANES), F32))
    out_specs.append(pl.BlockSpec(memory_space=pltpu.VMEM))
    res = pl.pallas_call(
        body,
        name=name,
        out_shape=tuple(out_shape),
        in_specs=[HBM] * (2 * na) + [ANY] * len(extra),
        out_specs=tuple(out_specs),
        input_output_aliases=aliases,
        compiler_params=pltpu.CompilerParams(has_side_effects=EFFECT),
    )(*[pltpu.with_memory_space_constraint(a, pltpu.HBM) for a in list(srcs) + list(lands)], *extra)
    return [tuple(res[4 * a:4 * a + 4]) for a in range(na)], res[4 * na][0, 0]


def _split_wait(handles, after, *, gather, name):
    na = len(handles)

    def body(*refs):
        src_refs, land_refs = refs[:na], refs[na:2 * na]
        sems = refs[2 * na:4 * na]
        for a in range(na):
            for cp in _split_copies(gather, src_refs[a], land_refs[a], sems[2 * a], sems[2 * a + 1]):
                cp.wait_send()
                cp.wait_recv()

    srcs = [h[2] for h in handles]
    lands = [h[3] for h in handles]
    sems = [s for h in handles for s in h[:2]]
    res = pl.pallas_call(
        body,
        name=name,
        out_shape=tuple(pltpu.HBM(a.shape, a.dtype) for a in srcs + lands),
        in_specs=[HBM] * (2 * na) + [SEM] * (2 * na) + [ANY],
        out_specs=tuple([HBM] * (2 * na)),
        input_output_aliases={i: i for i in range(2 * na)},
        compiler_params=pltpu.CompilerParams(has_side_effects=EFFECT),
    )(*srcs, *lands, *sems, after)
    return list(res[:na]), list(res[na:])


def _sum8(own, recv, *, name):
    _, r, w = recv.shape
    tr = _pick(r, (256, 128, 64, 32, 16, 8))

    def body(own_ref, a_ref, o_ref):
        acc = own_ref[...].astype(F32)
        for i in range(N_DEV - 1):
            acc = acc + a_ref[i].astype(F32)
        o_ref[...] = acc

    return pl.pallas_call(
        body,
        name=name,
        grid=(r // tr,),
        in_specs=[pl.BlockSpec((tr, w), lambda i: (i, 0)), pl.BlockSpec((N_DEV - 1, tr, w), lambda i: (0, i, 0))],
        out_specs=pl.BlockSpec((tr, w), lambda i: (i, 0)),
        out_shape=jax.ShapeDtypeStruct((r, w), F32),
        compiler_params=pltpu.CompilerParams(dimension_semantics=("parallel",), vmem_limit_bytes=_vmem(4 * N_DEV * tr * w * 4)),
    )(own, recv)


def _adamw(w, g, m, v, *, name):
    def fn(rows, _):
        wv, gv, mv, vv = rows
        m1 = ADAM_B1 * mv + (1.0 - ADAM_B1) * gv
        v1 = ADAM_B2 * vv + (1.0 - ADAM_B2) * jnp.square(gv)
        m_hat = m1 / (1.0 - ADAM_B1 ** ADAM_STEP)
        v_hat = v1 / (1.0 - ADAM_B2 ** ADAM_STEP)
        delta = -ADAM_LR * (m_hat / (jnp.sqrt(v_hat) + ADAM_EPS) + ADAM_WD * wv)
        return [delta, m1, v1], []

    c = w.shape[1]
    return _rowmap(fn, [w, g, m, v], [], [(c, F32)] * 3, [], name=name)


_SMALL = ("b_ada", "g_norm1", "g_norm2", "b_gate", "g_qa", "g_ka", "g_qb", "g_kb", "rpb", "loss")


def _pack_small(parts):
    flat = []
    for nme in _SMALL:
        a = parts[nme].reshape(-1).astype(F32)
        flat.append(jnp.pad(a, (0, (-a.shape[0]) % LANES)))
    flat = jnp.concatenate(flat)
    flat = jnp.pad(flat, (0, (-flat.shape[0]) % (8 * LANES)))
    return flat.reshape(-1, LANES)


def _unpack_small(packed, shapes):
    flat = packed.reshape(-1)
    out, pos = {}, 0
    for nme in _SMALL:
        n = int(np.prod(shapes[nme]))
        out[nme] = flat[pos:pos + n].reshape(shapes[nme])
        pos += n + (-n) % LANES
    return out


def _to_class(a, d):
    t, w = a.shape
    return a if d == 1 else a.reshape(t // d, d, w).transpose(1, 0, 2).reshape(t, w)


def _from_class(a, d):
    t, w = a.shape
    return a if d == 1 else a.reshape(d, t // d, w).transpose(1, 0, 2).reshape(t, w)


def kernel(x, c, w_ada, b_ada, g_norm1, g_norm2, w_in, b_gate, g_qa, g_ka, g_qb, g_kb, rpb, w_proj_a, w_proj_b, w_o, w_ffn_in, w_ffn_out, loss_target, m_w_ada, m_b_ada, m_g_norm1, m_g_norm2, m_w_in, m_b_gate, m_g_qa, m_g_ka, m_g_qb, m_g_kb, m_rpb, m_w_proj_a, m_w_proj_b, m_w_o, m_w_ffn_in, m_w_ffn_out, v_w_ada, v_b_ada, v_g_norm1, v_g_norm2, v_w_in, v_b_gate, v_g_qa, v_g_ka, v_g_qb, v_g_kb, v_rpb, v_w_proj_a, v_w_proj_b, v_w_o, v_w_ffn_in, v_w_ffn_out):
    t, d = x.shape[1], x.shape[2]
    d_ff = w_ffn_out.shape[1] * N_DEV
    me = 4 * lax.axis_index("x") + 2 * lax.axis_index("y") + lax.axis_index("c")
    xt, tgt = x.reshape(t, d), loss_target.reshape(t, d)
    ones = _head_ones()

    shards = [s.astype(WIRE_DTYPE) for s in (w_in[0].T, w_ffn_in[0].T, w_proj_a[0].T, w_proj_b[0].T, w_o[0], w_ffn_out[0])]
    lands = [lax.dynamic_update_slice(lax.empty((N_DEV * s.shape[0], s.shape[1]), s.dtype), s, (me * s.shape[0], 0))
             for s in shards]

    c_all = _all_gather([jnp.pad(c, ((0, 7), (0, 0)))], name="gather_c")[0][::8]
    c_all = jnp.pad(c_all, ((0, LANES - N_DEV), (0, 0)))

    def mod_body(c_ref, w_ref, b_ref, o_ref, act_ref):
        act = _silu(c_ref[...])
        act_ref[...] = act
        hi, mid, lo = _split3(act)
        w = w_ref[...]
        whi, wmid, wlo = _split3(w)
        acc = _dot(hi, whi) + (_dot(hi, wmid) + _dot(mid, whi)) + (_dot(hi, wlo) + _dot(mid, wmid) + _dot(lo, whi))
        o_ref[...] = acc + b_ref[...]

    ncol = w_ada.shape[2]
    b_ada_mine = lax.dynamic_slice(b_ada, (0, me * ncol), (1, ncol))
    mod_part, c_act = pl.pallas_call(
        mod_body,
        name="ada_mod",
        out_shape=[jax.ShapeDtypeStruct((LANES, ncol), F32), jax.ShapeDtypeStruct((LANES, d), F32)],
        compiler_params=pltpu.CompilerParams(vmem_limit_bytes=_vmem(6 * d * ncol * 4)),
    )(c_all, w_ada[0], b_ada_mine)
    mod_all = _all_gather([mod_part[:N_DEV]], name="gather_mod")[0].reshape(N_DEV, N_DEV, ncol)
    mod = lax.dynamic_index_in_dim(mod_all, me, axis=1, keepdims=False).reshape(6, d)
    sh1, sc1, gt1, sh2, sc2, gt2 = [mod[i:i + 1] for i in range(6)]

    def norm_fwd(rows, vecs):
        (xv,), (g, sc, sh) = rows, vecs
        return [xv * _rms(xv) * g * (1.0 + sc) + sh], []

    w_handles, w_token = _split_start(shards, lands, gather=True, after=mod, name="gather_weights_start")
    (h,) = _rowmap(norm_fwd, [xt], [g_norm1 + w_token, sc1, sh1], [(d, MXU_DTYPE)], [], name="norm1")
    n_a, n_b = 3 * WA, 3 * WB
    (w_in_t,) = _split_wait(w_handles[:1], h, gather=True, name="gather_w_in_wait")[1]
    w_in_a, w_in_b, w_in_g = w_in_t[:n_a], w_in_t[n_a:n_a + n_b], w_in_t[n_a + n_b:]
    qkv_a = _mm(h, w_in_a, tb=True, out_dtype=ACT_DTYPE, name="proj_a")
    qkv_b = _mm(h, w_in_b, tb=True, out_dtype=ACT_DTYPE, name="proj_b")
    gates = _mm(h, w_in_g, tb=True, out_dtype=ACT_DTYPE, name="proj_gates")

    rot_c, rot_lo, rot_hi = _rot_tables(t)
    tile_g = lambda g, heads: jnp.tile(g, (1, heads))

    def qk_fwd(width, rotate):
        def fn(rows, vecs):
            xv = rows[0]
            gq, gk, on = vecs
            outs = []
            for i, g in enumerate((gq, gk)):
                xi = xv[:, i * width:(i + 1) * width]
                r = lax.rsqrt(_headsum(xi * xi, on) * (1.0 / HEAD_DIM) + EPS)
                yi = xi * r * g
                if rotate:
                    yi = _rot(yi, rows[1], rows[2], rows[3])
                outs.append(yi)
            outs.append(xv[:, 2 * width:])
            return outs, []
        return fn

    qa, ka, va = _rowmap(qk_fwd(WA, False), [qkv_a], [tile_g(g_qa, NA_HEADS), tile_g(g_ka, NA_HEADS), ones],
                         [(WA, MXU_DTYPE)] * 3, [], name="qknorm_a")
    qb, kb, vb = _rowmap(qk_fwd(WB, True), [qkv_b, rot_c, rot_lo, rot_hi],
                         [tile_g(g_qb, DIL_HEADS), tile_g(g_kb, DIL_HEADS), ones], [(WB, MXU_DTYPE)] * 3, [], name="qknorm_b")

    bias_tab = _na_bias_table(rpb[0])
    o_a, lse_a = _attn_fwd(qa, ka, va, kind="na", bias=bias_tab, name="na_fwd")

    grp = []
    for g, (_, dil) in enumerate(DIL_CONFIGS):
        sl = slice(g * WB_OUT, (g + 1) * WB_OUT)
        qg, kg, vg = [_to_class(a[:, sl], dil) for a in (qb, kb, vb)]
        og, lg = _attn_fwd(qg, kg, vg, kind="dil", seg=t // dil, name=f"dil_fwd{g}")
        grp.append(dict(q=qg, k=kg, v=vg, o=_from_class(og, dil), lse=_from_class(lg, dil), lse_c=lg, dil=dil))

    def merge_fwd(rows, _):
        o0, o1, o2, l0, l1, l2 = rows
        mx = jnp.maximum(jnp.maximum(l0, l1), l2)
        e0, e1, e2 = jnp.exp(l0 - mx), jnp.exp(l1 - mx), jnp.exp(l2 - mx)
        s = e0 + e1 + e2
        return [(e0 / s) * o0 + (e1 / s) * o1 + (e2 / s) * o2], []

    (o_b,) = _rowmap(merge_fwd, [gr["o"] for gr in grp] + [gr["lse"] for gr in grp], [], [(WB_OUT, F32)], [], name="dil_merge")

    w_pa_t, w_pb_t, w_o_f = _split_wait(w_handles[2:5], o_b, gather=True, name="gather_w_out_wait")[1]
    pa = _mm(o_a, w_pa_t, tb=True, out_dtype=ACT_DTYPE, name="proj_out_a")
    pb = _mm(o_b, w_pb_t, tb=True, out_dtype=ACT_DTYPE, name="proj_out_b")

    def gate_fwd(rows, vecs):
        gv, pav, pbv = rows
        sg = jax.nn.sigmoid(gv + vecs[0])
        return [sg[:, :d] * pav + sg[:, d:] * pbv], []

    (merged,) = _rowmap(gate_fwd, [gates, pa, pb], [b_gate], [(d, MXU_DTYPE)], [], name="gate_merge")
    def resid_norm(av, rows, vecs):
        (xv,), (gt, g, sc, sh) = rows, vecs
        x1v = xv + gt * av
        return [av, x1v, x1v * _rms(x1v) * g * (1.0 + sc) + sh], []

    att, x1, h2 = _mm_rows(merged, w_o_f, resid_norm, [xt], [gt1, g_norm2, sc2, sh2],
                           [(d, F32), (d, F32), (d, MXU_DTYPE)], [], name="proj_o_resid_norm2")

    w_ffn_in_t, w_ffn_out_f = _split_wait([w_handles[1], w_handles[5]], h2, gather=True, name="gather_w_ffn_wait")[1]
    u = _mm(h2, w_ffn_in_t, tb=True, out_dtype=ACT_DTYPE, name="ffn_in")

    def swiglu_fwd(rows, _):
        uv = rows[0]
        return [_silu(uv[:, :d_ff]) * uv[:, d_ff:]], []

    (f,) = _rowmap(swiglu_fwd, [u], [], [(d_ff, MXU_DTYPE)], [], name="swiglu")
    def loss_fn(yv, rows, vecs):
        (x1v, tv), gt = rows, vecs[0]
        err = x1v + gt * yv - tv
        dout = err * (1.0 / d)
        return [dout, dout * gt], [_colsum(err * err), _colsum(dout * yv)]

    dout, dy2, err2, dgt2 = _mm_rows(f, w_ffn_out_f, loss_fn, [x1, tgt], [gt2], [(d, F32), (d, MXU_DTYPE)], [d, d],
                                     name="ffn_out_loss")

    dw_ffn_out = _mm(f, dy2, ta=True, out_dtype=WIRE_DTYPE, name="wgrad_ffn_out")
    df = _mm(dy2, w_ffn_out_f, tb=True, out_dtype=ACT_DTYPE, name="dgrad_ffn_out")

    def swiglu_bwd(rows, _):
        dfv, uv = rows
        a, up = uv[:, :d_ff], uv[:, d_ff:]
        sg = jax.nn.sigmoid(a)
        da = dfv * up * (sg * (1.0 + a * (1.0 - sg)))
        return [jnp.concatenate([da, dfv * (a * sg)], axis=1)], []

    (du,) = _rowmap(swiglu_bwd, [df, u], [], [(2 * d_ff, MXU_DTYPE)], [], name="swiglu_bwd")
    dw_ffn_in_t = _mm(du, h2, ta=True, out_dtype=WIRE_DTYPE, name="wgrad_ffn_in")
    land7 = lambda a: lax.empty((N_DEV - 1, a.shape[0] // N_DEV, a.shape[1]), a.dtype)
    own_block = lambda a: lax.dynamic_slice(a, (me * (a.shape[0] // N_DEV), 0), (a.shape[0] // N_DEV, a.shape[1]))
    g_ffn = [dw_ffn_in_t, dw_ffn_out]
    h_ffn, tok_ffn = _split_start(g_ffn, [land7(a) for a in g_ffn], gather=False, name="exchange_ffn_start")
    def norm_bwd(dh, xv, g, sc):
        r = _rms(xv)
        xh = xv * r
        dxh = dh * g * (1.0 + sc)
        dxv = r * (dxh - xh * jnp.mean(dxh * xh, axis=-1, keepdims=True))
        return dxv, [_colsum(dh), _colsum(dh * xh * g), _colsum(dh * xh * (1.0 + sc))]

    def norm2_bwd(dhv, rows, vecs):
        (x1v, dov, av), (g, sc, gt) = rows, vecs
        dxv, sums = norm_bwd(dhv, x1v, g, sc)
        dx1v = dov + dxv
        return [dx1v, dx1v * gt], sums + [_colsum(dx1v * av)]

    dx1, datt, dsh2, dsc2, dg2, dgt1 = _mm_rows(du, w_ffn_in_t, norm2_bwd, [x1, dout, att], [g_norm2 + tok_ffn, sc2, gt1],
                                                 [(d, F32), (d, MXU_DTYPE)], [d] * 4, name="dgrad_ffn_in_norm2_bwd")
    dw_o = _mm(merged, datt, ta=True, out_dtype=WIRE_DTYPE, name="wgrad_o")
    dmerged = _mm(datt, w_o_f, tb=True, out_dtype=ACT_DTYPE, name="dgrad_o")

    def gate_bwd(rows, vecs):
        dm, gv, pav, pbv = rows
        sg = jax.nn.sigmoid(gv + vecs[0])
        ga, gb = sg[:, :d], sg[:, d:]
        dgp = jnp.concatenate([dm * pav * ga * (1.0 - ga), dm * pbv * gb * (1.0 - gb)], axis=1)
        return [dm * ga, dm * gb, dgp], [_colsum(dgp)]

    dpa, dpb, dgates, db_gate = _rowmap(gate_bwd, [dmerged, gates, pa, pb], [b_gate],
                                        [(d, MXU_DTYPE), (d, MXU_DTYPE), (2 * d, MXU_DTYPE)], [2 * d], name="gate_bwd")
    dw_pa_t = _mm(dpa, o_a, ta=True, out_dtype=WIRE_DTYPE, name="wgrad_proj_a")
    dw_pb_t = _mm(dpb, o_b, ta=True, out_dtype=WIRE_DTYPE, name="wgrad_proj_b")
    g_out = [dw_pa_t, dw_pb_t, dw_o]
    h_out, tok_out = _split_start(g_out, [land7(a) for a in g_out], gather=False, name="exchange_out_start")
    do_a = _mm(dpa, w_pa_t, name="dgrad_proj_a")
    do_b = _mm(dpb, w_pb_t, name="dgrad_proj_b")

    def delta_a(rows, vecs):
        return [_headsum(rows[0] * rows[1], vecs[0])], []

    (dterm_a,) = _rowmap(delta_a, [do_a, o_a], [ones + tok_out.astype(ones.dtype)], [(WA, F32)], [], name="na_delta")
    dqa, dka, dva, dbias = _attn_bwd(qa, ka, va, do_a, dterm_a, lse_a, kind="na", bias=bias_tab, name="na_bwd")
    g_rpb = _na_bias_grad(dbias)

    def merge_bwd(rows, vecs):
        dob, o0, o1, o2, l0, l1, l2 = rows
        on = vecs[0]
        mx = jnp.maximum(jnp.maximum(l0, l1), l2)
        e0, e1, e2 = jnp.exp(l0 - mx), jnp.exp(l1 - mx), jnp.exp(l2 - mx)
        s = e0 + e1 + e2
        ws = [e0 / s, e1 / s, e2 / s]
        dws = [_headsum(dob * o, on) for o in (o0, o1, o2)]
        mean = ws[0] * dws[0] + ws[1] * dws[1] + ws[2] * dws[2]
        return [w * dob for w in ws] + [w * mean for w in ws], []

    mb = _rowmap(merge_bwd, [do_b] + [gr["o"] for gr in grp] + [gr["lse"] for gr in grp], [ones],
                 [(WB_OUT, F32)] * 6, [], name="dil_merge_bwd")
    dqb, dkb, dvb = [], [], []
    for g, gr in enumerate(grp):
        dil = gr["dil"]
        dq, dk, dv = _attn_bwd(gr["q"], gr["k"], gr["v"], _to_class(mb[g], dil), _to_class(mb[3 + g], dil), gr["lse_c"],
                               kind="dil", seg=t // dil, name=f"dil_bwd{g}")
        dqb.append(_from_class(dq, dil))
        dkb.append(_from_class(dk[0], dil))
        dvb.append(_from_class(dv[0], dil))

    def qk_bwd(width, rotate, nparts):
        def fn(rows, vecs):
            gq, gk, on = vecs
            xv = rows[0]
            pos = 1
            if rotate:
                rc, rlo, rhi = rows[1:4]
                pos = 4
            cat = lambda parts: parts[0] if len(parts) == 1 else jnp.concatenate(parts, axis=1)
            ends = np.cumsum((pos,) + nparts)
            dq, dk, dv = [cat(rows[ends[i]:ends[i + 1]]) for i in range(3)]
            outs, sums = [], []
            for i, (dy, g) in enumerate(((dq, gq), (dk, gk))):
                if rotate:
                    dy = _rot(dy, rc, -rlo, -rhi)
                xi = xv[:, i * width:(i + 1) * width]
                r = lax.rsqrt(_headsum(xi * xi, on) * (1.0 / HEAD_DIM) + EPS)
                xh = xi * r
                dxh = dy * g
                outs.append(r * (dxh - xh * (_headsum(dxh * xh, on) * (1.0 / HEAD_DIM))))
                sums.append(_colsum(dy * xh))
            return [jnp.concatenate(outs + [dv], axis=1)], sums
        return fn

    dqkv_a, dg_qa, dg_ka = _rowmap(qk_bwd(WA, False, (1, len(dka), len(dva))), [qkv_a, dqa] + dka + dva,
                                   [tile_g(g_qa, NA_HEADS), tile_g(g_ka, NA_HEADS), ones],
                                   [(3 * WA, MXU_DTYPE)], [WA, WA], name="qknorm_a_bwd")
    dqkv_b, dg_qb, dg_kb = _rowmap(qk_bwd(WB, True, (3, 3, 3)), [qkv_b, rot_c, rot_lo, rot_hi] + dqb + dkb + dvb,
                                   [tile_g(g_qb, DIL_HEADS), tile_g(g_kb, DIL_HEADS), ones],
                                   [(3 * WB, MXU_DTYPE)], [WB, WB], name="qknorm_b_bwd")

    dw_in_t = jnp.concatenate([
        _mm(dqkv_a, h, ta=True, out_dtype=WIRE_DTYPE, name="wgrad_in_a"),
        _mm(dqkv_b, h, ta=True, out_dtype=WIRE_DTYPE, name="wgrad_in_b"),
        _mm(dgates, h, ta=True, out_dtype=WIRE_DTYPE, name="wgrad_in_gates")], axis=0)
    h_in, tok_in = _split_start([dw_in_t], [land7(dw_in_t)], gather=False, name="exchange_in_start")
    dh = _mm(dqkv_a, w_in_a, name="dgrad_in_a")
    dh = _mm(dqkv_b, w_in_b, add=dh, name="dgrad_in_b")

    def norm1_bwd(dhg, rows, vecs):
        dhv, xv, dx1v = rows
        dxv, sums = norm_bwd(dhv + dhg, xv, vecs[0], vecs[1])
        return [dx1v + dxv], sums

    grad_x, dsh1, dsc1, dg1 = _mm_rows(dgates, w_in_g, norm1_bwd, [dh, xt, dx1], [g_norm1 + tok_in, sc1], [(d, F32)],
                                       [d] * 3, name="dgrad_in_gates_norm1_bwd")

    heads_sum = lambda a, heads: a.reshape(heads, HEAD_DIM).sum(axis=0)
    dmod = jnp.concatenate([dsh1, dsc1, dgt1, dsh2, dsc2, dgt2], axis=1)
    local_small = _pack_small(dict(
        b_ada=dmod, g_norm1=dg1, g_norm2=dg2, b_gate=db_gate, g_qa=heads_sum(dg_qa, NA_HEADS),
        g_ka=heads_sum(dg_ka, NA_HEADS), g_qb=heads_sum(dg_qb, DIL_HEADS), g_kb=heads_sum(dg_kb, DIL_HEADS),
        rpb=g_rpb, loss=(0.5 / d) * jnp.sum(err2)))
    srows = local_small.shape[0]
    small_all = _all_gather([local_small], name="gather_small")[0].reshape(N_DEV, srows, LANES)
    small_sum = _sum8(small_all[0], small_all[1:], name="sum_small")
    small_shapes = dict(b_ada=b_ada.shape, g_norm1=g_norm1.shape, g_norm2=g_norm2.shape, b_gate=b_gate.shape,
                        g_qa=g_qa.shape, g_ka=g_ka.shape, g_qb=g_qb.shape, g_kb=g_kb.shape, rpb=rpb.shape, loss=())
    small_w = dict(b_ada=b_ada, g_norm1=g_norm1, g_norm2=g_norm2, b_gate=b_gate, g_qa=g_qa, g_ka=g_ka, g_qb=g_qb,
                   g_kb=g_kb, rpb=rpb, loss=jnp.zeros((), F32))
    small_m = dict(b_ada=m_b_ada, g_norm1=m_g_norm1, g_norm2=m_g_norm2, b_gate=m_b_gate, g_qa=m_g_qa, g_ka=m_g_ka,
                   g_qb=m_g_qb, g_kb=m_g_kb, rpb=m_rpb, loss=jnp.zeros((), F32))
    small_v = dict(b_ada=v_b_ada, g_norm1=v_g_norm1, g_norm2=v_g_norm2, b_gate=v_b_gate, g_qa=v_g_qa, g_ka=v_g_ka,
                   g_qb=v_g_qb, g_kb=v_g_kb, rpb=v_rpb, loss=jnp.zeros((), F32))
    s_delta, s_m, s_v = _adamw(_pack_small(small_w), small_sum, _pack_small(small_m), _pack_small(small_v), name="adamw_small")
    gs = _unpack_small(small_sum, small_shapes)
    ds_, ms_, vs_ = [_unpack_small(a, small_shapes) for a in (s_delta, s_m, s_v)]

    dmod_all = small_all[:, :6 * d // LANES].reshape(N_DEV, 6 * d)
    dmod_mine = jnp.pad(lax.dynamic_slice(dmod_all, (0, me * ncol), (N_DEV, ncol)), ((0, LANES - N_DEV), (0, 0)))

    def wada_body(c_ref, dm_ref, o_ref):
        chi, cmid, clo = _split3(c_ref[...])
        dhi, dmid, dlo = _split3(dm_ref[...])
        o_ref[...] = (_dot_tn(chi, dhi) + (_dot_tn(chi, dmid) + _dot_tn(cmid, dhi))
                      + (_dot_tn(chi, dlo) + _dot_tn(cmid, dmid) + _dot_tn(clo, dhi)))

    g_w_ada = pl.pallas_call(
        wada_body,
        name="wgrad_ada",
        out_shape=jax.ShapeDtypeStruct((d, ncol), F32),
        compiler_params=pltpu.CompilerParams(vmem_limit_bytes=_vmem(4 * d * ncol * 4)),
    )(c_act, dmod_mine)

    sent, recv = _split_wait(h_in + h_ffn + h_out, small_sum, gather=False, name="exchange_wait")
    names = ("w_in", "w_ffn_in", "w_ffn_out", "w_proj_a", "w_proj_b", "w_o")
    transposed = (True, True, False, True, True, False)
    big_g = {}
    for nme, own, r, tr in zip(names, sent, recv, transposed):
        s = _sum8(own_block(own), r, name=f"sum_{nme}")
        big_g[nme] = s.T if tr else s
    big_g["w_ada"] = g_w_ada
    big_w = dict(w_ada=w_ada, w_in=w_in, w_proj_a=w_proj_a, w_proj_b=w_proj_b, w_o=w_o, w_ffn_in=w_ffn_in, w_ffn_out=w_ffn_out)
    big_m = dict(w_ada=m_w_ada, w_in=m_w_in, w_proj_a=m_w_proj_a, w_proj_b=m_w_proj_b, w_o=m_w_o, w_ffn_in=m_w_ffn_in, w_ffn_out=m_w_ffn_out)
    big_v = dict(w_ada=v_w_ada, w_in=v_w_in, w_proj_a=v_w_proj_a, w_proj_b=v_w_proj_b, w_o=v_w_o, w_ffn_in=v_w_ffn_in, w_ffn_out=v_w_ffn_out)
    grads, deltas, new_m, new_v = {}, {}, {}, {}
    for nme in big_w:
        dl, m1, v1 = _adamw(big_w[nme][0], big_g[nme], big_m[nme][0], big_v[nme][0], name=f"adamw_{nme}")
        grads[nme], deltas[nme], new_m[nme], new_v[nme] = big_g[nme][None], dl[None], m1[None], v1[None]
    for nme in _SMALL[:-1]:
        grads[nme], deltas[nme], new_m[nme], new_v[nme] = gs[nme], ds_[nme], ms_[nme], vs_[nme]

    order = ("w_ada", "b_ada", "g_norm1", "g_norm2", "w_in", "b_gate", "g_qa", "g_ka", "g_qb", "g_kb", "rpb",
             "w_proj_a", "w_proj_b", "w_o", "w_ffn_in", "w_ffn_out")
    return (gs["loss"], grad_x[None], *[grads[n] for n in order], *[deltas[n] for n in order],
            *[new_m[n] for n in order], *[new_v[n] for n in order])
```

```python
import functools

import numpy as np
import jax
import jax.numpy as jnp
from jax import lax
from jax.experimental import pallas as pl
from jax.experimental.pallas import tpu as pltpu

F32 = jnp.float32
MXU_DTYPE = jnp.bfloat16
WIRE_DTYPE = jnp.bfloat16
ACT_DTYPE = jnp.bfloat16

HEAD_DIM = 64
GRID_W = 64
NA_HEADS = 8
NA_KH = 8
NA_KW = 16
DIL_CONFIGS = ((128, 1), (512, 4), (2048, 16))
DIL_HEADS_PER_GROUP = 4
DIL_HEADS = DIL_HEADS_PER_GROUP * len(DIL_CONFIGS)
DIL_HALF = 64
ROT_DIM = HEAD_DIM // 4
ROPE_THETA = 500000.0
EPS = 1e-6
NEG_INF = -1e30
WA = NA_HEADS * HEAD_DIM
WB = DIL_HEADS * HEAD_DIM
WB_OUT = DIL_HEADS_PER_GROUP * HEAD_DIM
ADAM_LR = 0.001
ADAM_B1 = 0.9
ADAM_B2 = 0.999
ADAM_EPS = 1e-08
ADAM_WD = 0.01
ADAM_STEP = 10

N_DEV = 8
LANES = 128
VMEM_CAP = 60 * 2**20
VMEM_FLOOR = 56 * 2**20
MESH = pl.DeviceIdType.MESH
ANY = pl.BlockSpec(memory_space=pl.ANY)


def _vmem(nbytes):
    return int(min(VMEM_CAP, max(VMEM_FLOOR, nbytes * 5 // 4 + 4 * 2**20)))


def _pick(dim, cands):
    for c in cands:
        if c <= dim and dim % c == 0:
            return c
    return dim


def _nbytes(shape, dtype):
    return int(np.prod(shape)) * jnp.dtype(dtype).itemsize


def _dot(a, b, dims=((1,), (0,))):
    return lax.dot_general(a.astype(MXU_DTYPE), b.astype(MXU_DTYPE), (dims, ((), ())), preferred_element_type=F32)


def _dot_nt(a, b):
    return _dot(a, b, ((1,), (1,)))


def _dot_tn(a, b):
    return _dot(a, b, ((0,), (0,)))


def _split3(a):
    hi = a.astype(jnp.bfloat16)
    r1 = a - hi.astype(F32)
    mid = r1.astype(jnp.bfloat16)
    lo = (r1 - mid.astype(F32)).astype(jnp.bfloat16)
    return hi, mid, lo


def _silu(x):
    return x * jax.nn.sigmoid(x)


def _divisors(dim, unit):
    return [c for c in range(unit, dim + 1, unit) if dim % c == 0] or [dim]


def _mm_tiles(m, n, kdim, a_item, b_item, o_item):
    step_us, hbm_bytes_per_us, flops_per_us, budget = 0.35, 3.0e6, 8.0e8, 40 * 2**20
    best = None
    for tm in _divisors(m, LANES):
        for tn in _divisors(n, LANES):
            for tk in _divisors(kdim, LANES):
                gm, gn, gk = m // tm, n // tn, kdim // tk
                vmem = 2 * (tm * tk * a_item + tk * tn * b_item + tm * tn * o_item) + 2 * (tm * tk + tk * tn)
                vmem += tm * tn * 4 * ((1 if gk > 1 else 0) + 1)
                if vmem > budget:
                    continue
                a_reads = m * kdim * a_item * (gn if gk > 1 else 1)
                traffic = a_reads + kdim * n * b_item * gm + m * n * o_item
                cost = gm * gn * gk * step_us + max(traffic / hbm_bytes_per_us, 2.0 * m * n * kdim / flops_per_us)
                if best is None or cost < best[0]:
                    best = (cost, tm, tn, tk)
    return best[1:]


def _mm(a, b, *, name, ta=False, tb=False, out_dtype=F32):
    if ta:
        kdim, m = a.shape
    else:
        m, kdim = a.shape
    n = b.shape[0] if tb else b.shape[1]
    assert b.shape[1 if tb else 0] == kdim
    tm, tn, tk = _mm_tiles(m, n, kdim, a.dtype.itemsize, b.dtype.itemsize, jnp.dtype(out_dtype).itemsize)
    gm, gn, gk = m // tm, n // tn, kdim // tk

    a_spec = pl.BlockSpec((tk, tm), lambda i, j, k: (k, i)) if ta else pl.BlockSpec((tm, tk), lambda i, j, k: (i, k))
    b_spec = pl.BlockSpec((tn, tk), lambda i, j, k: (j, k)) if tb else pl.BlockSpec((tk, tn), lambda i, j, k: (k, j))
    o_spec = pl.BlockSpec((tm, tn), lambda i, j, k: (i, j))
    a_dims = (0,) if ta else (1,)
    b_dims = (1,) if tb else (0,)

    def body(a_ref, b_ref, o_ref, *scratch):
        if gk == 1:
            o_ref[...] = _dot(a_ref[...], b_ref[...], (a_dims, b_dims)).astype(o_ref.dtype)
            return
        (acc_ref,) = scratch
        k = pl.program_id(2)

        @pl.when(k == 0)
        def _():
            acc_ref[...] = jnp.zeros_like(acc_ref)

        acc_ref[...] += _dot(a_ref[...], b_ref[...], (a_dims, b_dims))

        @pl.when(k == gk - 1)
        def _():
            o_ref[...] = acc_ref[...].astype(o_ref.dtype)

    est = 2 * (tm * tk * a.dtype.itemsize + tk * tn * b.dtype.itemsize + tm * tn * jnp.dtype(out_dtype).itemsize)
    est += tm * tn * 4 + 2 * (tm * tk + tk * tn) * 2
    return pl.pallas_call(
        body,
        name=name,
        grid=(gm, gn, gk),
        in_specs=[a_spec, b_spec],
        out_specs=o_spec,
        out_shape=jax.ShapeDtypeStruct((m, n), out_dtype),
        scratch_shapes=[pltpu.VMEM((tm, tn), F32)] if gk > 1 else [],
        compiler_params=pltpu.CompilerParams(
            dimension_semantics=("parallel", "parallel", "arbitrary"), vmem_limit_bytes=_vmem(est)
        ),
    )(a, b)


def _mm_rows(a, b, fn, rows, vecs, outs, reds, *, name, tb=False):
    m, kdim = a.shape
    n = b.shape[0] if tb else b.shape[1]
    nr, nv, no = len(rows), len(vecs), len(outs)
    row_bytes = sum(r.shape[1] * r.dtype.itemsize for r in rows) + sum(w * jnp.dtype(dt).itemsize for (w, dt) in outs)
    best = None
    for tm in _divisors(m, LANES):
        for tk in _divisors(kdim, LANES):
            gm, gk = m // tm, kdim // tk
            vmem = 2 * (tm * tk * a.dtype.itemsize + tk * n * b.dtype.itemsize + tm * row_bytes) + tm * n * 4 * 5
            if vmem > 44 * 2**20:
                continue
            traffic = m * kdim * a.dtype.itemsize + kdim * n * b.dtype.itemsize * (gm if gk > 1 else 1) + m * row_bytes
            cost = gm * gk * 0.35 + max(traffic / 3.0e6, 2.0 * m * n * kdim / 8.0e8)
            if best is None or cost < best[0]:
                best = (cost, tm, tk, vmem)
    _, tm, tk, est = best
    gm, gk = m // tm, kdim // tk
    b_dims = (1,) if tb else (0,)

    def body(*refs):
        a_ref, b_ref = refs[:2]
        row_refs, vec_refs = refs[2:2 + nr], refs[2 + nr:2 + nr + nv]
        out_refs = refs[2 + nr + nv:2 + nr + nv + no]
        red_refs = refs[2 + nr + nv + no:2 + nr + nv + no + len(reds)]

        def finish(r):
            o, rd = fn(r, [x[...].astype(F32) for x in row_refs], [v[...] for v in vec_refs])
            for ref, val in zip(out_refs, o):
                ref[...] = val.astype(ref.dtype)
            if red_refs:
                @pl.when(pl.program_id(0) == 0)
                def _():
                    for ref in red_refs:
                        ref[...] = jnp.zeros_like(ref)

                for ref, val in zip(red_refs, rd):
                    ref[...] += val

        if gk == 1:
            finish(_dot(a_ref[...], b_ref[...], ((1,), b_dims)))
            return
        acc_ref = refs[-1]
        k = pl.program_id(1)

        @pl.when(k == 0)
        def _():
            acc_ref[...] = jnp.zeros_like(acc_ref)

        acc_ref[...] += _dot(a_ref[...], b_ref[...], ((1,), b_dims))

        @pl.when(k == gk - 1)
        def _():
            finish(acc_ref[...])

    in_specs = [pl.BlockSpec((tm, tk), lambda i, k: (i, k)),
                pl.BlockSpec((n, tk), lambda i, k: (0, k)) if tb else pl.BlockSpec((tk, n), lambda i, k: (k, 0))]
    in_specs += [pl.BlockSpec((tm, r.shape[1]), lambda i, k: (i, 0)) for r in rows]
    in_specs += [pl.BlockSpec(v.shape, functools.partial(lambda nd, i, k: (0,) * nd, v.ndim)) for v in vecs]
    out_specs = [pl.BlockSpec((tm, w), lambda i, k: (i, 0)) for (w, _) in outs]
    out_specs += [pl.BlockSpec((1, w), lambda i, k: (0, 0)) for w in reds]
    out_shape = [jax.ShapeDtypeStruct((m, w), dt) for (w, dt) in outs] + [jax.ShapeDtypeStruct((1, w), F32) for w in reds]
    return pl.pallas_call(
        body,
        name=name,
        grid=(gm, gk),
        in_specs=in_specs,
        out_specs=out_specs,
        out_shape=out_shape,
        scratch_shapes=[pltpu.VMEM((tm, n), F32)] if gk > 1 else [],
        compiler_params=pltpu.CompilerParams(dimension_semantics=("arbitrary", "arbitrary"), vmem_limit_bytes=_vmem(est)),
    )(a, b, *rows, *vecs)


def _row_tile(m, fixed_bytes, bytes_per_row, budget=46 * 2**20):
    fits = [tm for tm in _divisors(m, LANES) if fixed_bytes + tm * bytes_per_row <= budget]
    return max(fits) if fits else _divisors(m, LANES)[0]


def _mm_parts_rows(parts, fn, rows, vecs, outs, reds, *, name):
    m, n = parts[0][0].shape[0], parts[0][1].shape[1]
    npart, nr, nv, no = len(parts), len(rows), len(vecs), len(outs)
    row_bytes = sum(r.shape[1] * r.dtype.itemsize for r in rows) + sum(w * jnp.dtype(dt).itemsize for (w, dt) in outs)
    a_row_bytes = sum(a.shape[1] * a.dtype.itemsize for a, _ in parts)
    fixed = 2 * sum(_nbytes(b.shape, b.dtype) for _, b in parts)
    per_row = 2 * (a_row_bytes + row_bytes) + n * 4 * 5
    tm = _row_tile(m, fixed, per_row)

    def body(*refs):
        ab = refs[:2 * npart]
        row_refs, vec_refs = refs[2 * npart:2 * npart + nr], refs[2 * npart + nr:2 * npart + nr + nv]
        out_refs = refs[2 * npart + nr + nv:2 * npart + nr + nv + no]
        red_refs = refs[2 * npart + nr + nv + no:]
        r = _dot(ab[0][...], ab[1][...])
        for p in range(1, npart):
            r = r + _dot(ab[2 * p][...], ab[2 * p + 1][...])
        o, rd = fn(r, [x[...].astype(F32) for x in row_refs], [v[...] for v in vec_refs])
        for ref, val in zip(out_refs, o):
            ref[...] = val.astype(ref.dtype)
        if red_refs:
            @pl.when(pl.program_id(0) == 0)
            def _():
                for ref in red_refs:
                    ref[...] = jnp.zeros_like(ref)

            for ref, val in zip(red_refs, rd):
                ref[...] += val

    in_specs, operands = [], []
    for a, b in parts:
        in_specs += [pl.BlockSpec((tm, a.shape[1]), lambda i: (i, 0)), pl.BlockSpec(b.shape, lambda i: (0, 0))]
        operands += [a, b]
    in_specs += [pl.BlockSpec((tm, r.shape[1]), lambda i: (i, 0)) for r in rows]
    in_specs += [pl.BlockSpec(v.shape, functools.partial(lambda nd, i: (0,) * nd, v.ndim)) for v in vecs]
    out_specs = [pl.BlockSpec((tm, w), lambda i: (i, 0)) for (w, _) in outs]
    out_specs += [pl.BlockSpec((1, w), lambda i: (0, 0)) for w in reds]
    out_shape = [jax.ShapeDtypeStruct((m, w), dt) for (w, dt) in outs] + [jax.ShapeDtypeStruct((1, w), F32) for w in reds]
    return pl.pallas_call(
        body,
        name=name,
        grid=(m // tm,),
        in_specs=in_specs,
        out_specs=out_specs,
        out_shape=out_shape,
        compiler_params=pltpu.CompilerParams(dimension_semantics=("arbitrary",), vmem_limit_bytes=_vmem(fixed + tm * per_row)),
    )(*operands, *rows, *vecs)


def _mm_ew(a, bs, fn, rows, outs, *, name):
    m, kdim = a.shape
    n = bs[0].shape[0]
    nb, nr, no = len(bs), len(rows), len(outs)
    best = None
    for tm in _divisors(m, LANES):
        for tn in _divisors(n, LANES):
            tile_bytes = sum(r.dtype.itemsize for r in rows) + sum(jnp.dtype(dt).itemsize for dt in outs)
            vmem = 2 * (tm * kdim * a.dtype.itemsize + nb * tn * kdim * bs[0].dtype.itemsize + tm * tn * tile_bytes)
            vmem += tm * tn * 4 * (nb + 4)
            if vmem > 44 * 2**20:
                continue
            gm, gn = m // tm, n // tn
            traffic = m * kdim * a.dtype.itemsize + gm * nb * n * kdim * bs[0].dtype.itemsize + m * n * tile_bytes
            cost = gm * gn * 0.35 + max(traffic / 3.0e6, 2.0 * nb * m * n * kdim / 8.0e8)
            if best is None or cost < best[0]:
                best = (cost, tm, tn, vmem)
    _, tm, tn, est = best

    def body(*refs):
        a_ref, b_refs = refs[0], refs[1:1 + nb]
        row_refs, out_refs = refs[1 + nb:1 + nb + nr], refs[1 + nb + nr:]
        av = a_ref[...]
        o = fn([_dot_nt(av, b[...]) for b in b_refs], [x[...].astype(F32) for x in row_refs])
        for ref, val in zip(out_refs, o):
            ref[...] = val.astype(ref.dtype)

    tile = pl.BlockSpec((tm, tn), lambda i, j: (i, j))
    return pl.pallas_call(
        body,
        name=name,
        grid=(m // tm, n // tn),
        in_specs=[pl.BlockSpec((tm, kdim), lambda i, j: (i, 0))] + [pl.BlockSpec((tn, kdim), lambda i, j: (j, 0))] * nb
        + [tile] * nr,
        out_specs=[tile] * no,
        out_shape=[jax.ShapeDtypeStruct((m, n), dt) for dt in outs],
        compiler_params=pltpu.CompilerParams(dimension_semantics=("parallel", "parallel"), vmem_limit_bytes=_vmem(est)),
    )(a, *bs, *rows)


def _rowmap(fn, rows, vecs, outs, reds, *, name, tm=None):
    norm = []
    for r in rows:
        if not isinstance(r, tuple):
            norm.append((r, r.shape[1], 0, None))
        elif len(r) == 2:
            norm.append((r[0], r[0].shape[2], 0, r[1]))
        else:
            norm.append((r[0], r[1], r[2], None))
    rows = norm
    t = rows[0][0].shape[-2]
    if tm is None:
        per_row = 2 * sum(w * a.dtype.itemsize for (a, w, _, _) in rows) + 2 * sum(w * jnp.dtype(d).itemsize for (w, d) in outs)
        per_row += 3 * 4 * max([w for (_, w, _, _) in rows] + [w for (w, _) in outs])
        tm = max(8, min(1024, (40 * 2**20) // per_row))
    tm = _pick(t, tuple(c for c in (1024, 512, 256, 128, 64, 32, 16, 8) if c <= tm))
    nr, nv, no = len(rows), len(vecs), len(outs)

    def body(*refs):
        row_refs, vec_refs = refs[:nr], refs[nr:nr + nv]
        out_refs, red_refs = refs[nr + nv:nr + nv + no], refs[nr + nv + no:]
        o, rd = fn([r[...].astype(F32) for r in row_refs], [v[...] for v in vec_refs])
        for ref, val in zip(out_refs, o):
            ref[...] = val.astype(ref.dtype)
        if red_refs:
            @pl.when(pl.program_id(0) == 0)
            def _():
                for ref in red_refs:
                    ref[...] = jnp.zeros_like(ref)

            for ref, val in zip(red_refs, rd):
                ref[...] += val

    in_specs = [pl.BlockSpec((tm, w), functools.partial(lambda cb, i: (i, cb), cb)) if lead is None
                else pl.BlockSpec((None, tm, w), functools.partial(lambda ld, i: (ld, i, 0), lead)) for (_, w, cb, lead) in rows]
    in_specs += [pl.BlockSpec(v.shape, functools.partial(lambda nd, i: (0,) * nd, v.ndim)) for v in vecs]
    out_specs = [pl.BlockSpec((tm, w), lambda i: (i, 0)) for (w, _) in outs]
    out_specs += [pl.BlockSpec((1, w), lambda i: (0, 0)) for w in reds]
    out_shape = [jax.ShapeDtypeStruct((t, w), d) for (w, d) in outs]
    out_shape += [jax.ShapeDtypeStruct((1, w), F32) for w in reds]
    est = 2 * sum(tm * w * a.dtype.itemsize for (a, w, _, _) in rows) + 2 * sum(_nbytes(v.shape, v.dtype) for v in vecs)
    est += 2 * sum(tm * w * jnp.dtype(d).itemsize for (w, d) in outs)
    est += 6 * tm * max([w for (_, w, _, _) in rows] + [w for (w, _) in outs]) * 4
    return pl.pallas_call(
        body,
        name=name,
        grid=(t // tm,),
        in_specs=in_specs,
        out_specs=out_specs,
        out_shape=out_shape,
        compiler_params=pltpu.CompilerParams(dimension_semantics=("arbitrary",), vmem_limit_bytes=_vmem(est)),
    )(*[r[0] for r in rows], *vecs)


def _colsum(v):
    return jnp.sum(v, axis=0, keepdims=True)


def _head_ones():
    i = np.arange(LANES)
    return jnp.asarray((i[:, None] // HEAD_DIM) == (i[None, :] // HEAD_DIM), MXU_DTYPE)


def _headsum(y, ones):
    parts = []
    for j in range(y.shape[1] // LANES):
        c = y[:, j * LANES:(j + 1) * LANES]
        hi = c.astype(MXU_DTYPE)
        lo = c - hi.astype(F32)
        parts.append(_dot(hi, ones) + _dot(lo, ones))
    return parts[0] if len(parts) == 1 else jnp.concatenate(parts, axis=1)


def _rot(y, c, s_lo, s_hi):
    parts = []
    for j in range(y.shape[1] // LANES):
        yc = y[:, j * LANES:(j + 1) * LANES]
        parts.append(yc * c + pltpu.roll(yc, LANES - ROT_DIM // 2, 1) * s_lo + pltpu.roll(yc, ROT_DIM // 2, 1) * s_hi)
    return parts[0] if len(parts) == 1 else jnp.concatenate(parts, axis=1)


def _rot_tables(t):
    half = ROT_DIM // 2
    inv_freq = ROPE_THETA ** (-(jnp.arange(half, dtype=F32) * 2.0) / ROT_DIM)
    ang = jnp.arange(t).astype(F32)[:, None] * inv_freq[None, :]
    cos, sin = jnp.cos(ang), jnp.sin(ang)
    z = lambda w: jnp.zeros((t, w), F32)
    c = jnp.concatenate([cos, cos, jnp.ones((t, HEAD_DIM - ROT_DIM), F32)], axis=1)
    s_lo = jnp.concatenate([-sin, z(HEAD_DIM - half)], axis=1)
    s_hi = jnp.concatenate([z(half), sin, z(HEAD_DIM - ROT_DIM)], axis=1)
    return [jnp.tile(a, (1, LANES // HEAD_DIM)) for a in (c, s_lo, s_hi)]


def _rms(x):
    return lax.rsqrt(jnp.mean(x * x, axis=-1, keepdims=True) + EPS)


def _window(kind, n, bq, t, seg):
    if kind == "na":
        rows = t // GRID_W
        rs = jnp.clip(n - NA_KH // 2, 0, rows - NA_KH)
        return rs
    nk = bq + 2 * DIL_HALF
    return jnp.clip(n * bq - DIL_HALF, 0, t - nk)


def _dil_mask(n, bq, nk, ws, seg):
    qi = n * bq + lax.broadcasted_iota(jnp.int32, (bq, nk), 0)
    ki = ws + lax.broadcasted_iota(jnp.int32, (bq, nk), 1)
    shift = int(np.log2(seg))
    return (jnp.abs(ki - qi) <= DIL_HALF) & ((ki >> shift) == (qi >> shift))


HS = 4
QW = HS * HEAD_DIM


def _head_of_lane(width=QW):
    return lax.broadcasted_iota(jnp.int32, (1, width), 1) // HEAD_DIM


def _stack_heads(a):
    head = _head_of_lane()
    return jnp.concatenate([jnp.where(head == e, a, jnp.zeros_like(a)) for e in range(HS)], axis=0)


def _unstack_heads(a, bq):
    head = _head_of_lane()
    out = jnp.zeros((bq, QW), a.dtype)
    for e in range(HS):
        out = jnp.where(head == e, a[e * bq:(e + 1) * bq], out)
    return out


def _stack_cols(blk, bq):
    head = _head_of_lane()
    return jnp.concatenate(
        [jnp.max(jnp.where(head == e, blk, -jnp.inf), axis=1, keepdims=True) for e in range(HS)], axis=0)


def _attn_geometry(kind):
    if kind == "na":
        return GRID_W, NA_KH * GRID_W, 4
    bq = 128
    return bq, bq + 2 * DIL_HALF, 2


def _attn_scores(kind, n, bq, nk, t, seg, qs, k_ref, b_ref):
    scale = HEAD_DIM ** -0.5
    if kind == "na":
        rs = _window(kind, n, bq, t, seg)
        ws = pl.multiple_of(rs * GRID_W, GRID_W)
        ro0 = rs - n + (NA_KH - 1)
        s = _dot_nt(qs, k_ref[pl.ds(ws, nk), :]) * scale
        s = s + jnp.concatenate(
            [jnp.concatenate([b_ref[e, ro0 + 2 * i] for i in range(NA_KH // 2)], axis=1) for e in range(HS)], axis=0)
        return s, ws, ro0
    ws = pl.multiple_of(_window(kind, n, bq, t, seg), DIL_HALF)
    mask = _dil_mask(n, bq, nk, ws, seg)
    s = _dot_nt(qs, k_ref[pl.ds(ws, nk), :]) * scale
    s = jnp.where(jnp.concatenate([mask] * HS, axis=0), s, NEG_INF)
    return s, ws, None


def _attn_fwd(q, k, v, *, kind, name, bias=None, seg=None):
    t, w = q.shape
    quads = w // QW
    bq, nk, sub = _attn_geometry(kind)
    nq = t // (bq * sub)

    def body(*refs):
        if kind == "na":
            q_ref, k_ref, v_ref, b_ref, o_ref, l_ref = refs
        else:
            (q_ref, k_ref, v_ref, o_ref, l_ref), b_ref = refs, None
        for i in range(sub):
            n = pl.program_id(1) * sub + i
            rows = slice(i * bq, (i + 1) * bq)
            s, ws, _ = _attn_scores(kind, n, bq, nk, t, seg, _stack_heads(q_ref[rows, :]), k_ref, b_ref)
            m = jnp.max(s, axis=1, keepdims=True)
            p = jnp.exp(s - m)
            l = jnp.sum(p, axis=1, keepdims=True)
            o_ref[rows, :] = _unstack_heads(_dot(p / l, v_ref[pl.ds(ws, nk), :]), bq)
            l_ref[rows, :] = _unstack_heads(jnp.broadcast_to(m + jnp.log(l), (HS * bq, QW)), bq)

    blk = pl.BlockSpec((bq * sub, QW), lambda j, n: (n, j))
    res = pl.BlockSpec((t, QW), lambda j, n: (0, j))
    in_specs = [blk, res, res]
    operands = [q, k, v]
    est = 4 * t * QW * q.dtype.itemsize + 12 * sub * HS * bq * nk * 4
    if kind == "na":
        in_specs.append(pl.BlockSpec((HS,) + bias.shape[1:], lambda j, n: (j, 0, 0, 0)))
        operands.append(bias)
        est += 2 * _nbytes((HS,) + bias.shape[1:], F32)
    return pl.pallas_call(
        body,
        name=name,
        grid=(quads, nq),
        in_specs=in_specs,
        out_specs=[blk, blk],
        out_shape=[jax.ShapeDtypeStruct((t, w), F32)] * 2,
        compiler_params=pltpu.CompilerParams(dimension_semantics=("arbitrary", "arbitrary"), vmem_limit_bytes=_vmem(est)),
    )(*operands)


def _attn_bwd(q, k, v, do, dterm, lse, *, kind, name, bias=None, seg=None):
    t, w = q.shape
    quads = w // QW
    bq, nk, sub = _attn_geometry(kind)
    nq = t // (bq * sub)
    scale = HEAD_DIM ** -0.5

    def body(*refs):
        if kind == "na":
            q_ref, k_ref, v_ref, do_ref, dt_ref, l_ref, b_ref, dq_ref, dk_hbm, dv_hbm, db_ref, dk_acc, dv_acc, sem = refs
        else:
            q_ref, k_ref, v_ref, do_ref, dt_ref, l_ref, dq_ref, dk_hbm, dv_hbm, dk_acc, dv_acc, sem = refs
            b_ref = None
        j, step = pl.program_id(0), pl.program_id(1)

        @pl.when(step == 0)
        def _():
            dk_acc[...] = jnp.zeros_like(dk_acc)
            dv_acc[...] = jnp.zeros_like(dv_acc)
            if kind == "na":
                db_ref[...] = jnp.zeros_like(db_ref)

        for b in range(sub):
            n = step * sub + b
            rows = slice(b * bq, (b + 1) * bq)
            qs = _stack_heads(q_ref[rows, :])
            dos = _stack_heads(do_ref[rows, :])
            s, ws, ro0 = _attn_scores(kind, n, bq, nk, t, seg, qs, k_ref, b_ref)
            p = jnp.exp(s - _stack_cols(l_ref[rows, :], bq))
            dp = _dot_nt(dos, v_ref[pl.ds(ws, nk), :])
            ds = p * (dp - _stack_cols(dt_ref[rows, :], bq))
            if kind == "na":
                for e in range(HS):
                    for i in range(NA_KH // 2):
                        db_ref[e, ro0 + 2 * i] += ds[e * bq:(e + 1) * bq, i * LANES:(i + 1) * LANES]
            dsc = ds * scale
            dq_ref[rows, :] = _unstack_heads(_dot(dsc, k_ref[pl.ds(ws, nk), :]), bq)
            dk_acc[pl.ds(ws, nk), :] += _dot_tn(dsc, qs)
            dv_acc[pl.ds(ws, nk), :] += _dot_tn(p, dos)

        @pl.when(step == nq - 1)
        def _():
            ck = pltpu.make_async_copy(dk_acc, dk_hbm.at[j], sem.at[0])
            cv = pltpu.make_async_copy(dv_acc, dv_hbm.at[j], sem.at[1])
            ck.start()
            cv.start()
            ck.wait()
            cv.wait()

    blk = pl.BlockSpec((bq * sub, QW), lambda j, n: (n, j))
    res = pl.BlockSpec((t, QW), lambda j, n: (0, j))
    in_specs = [blk, res, res, blk, blk, blk]
    operands = [q, k, v, do, dterm, lse]
    out_specs = [blk, ANY, ANY]
    out_shape = [jax.ShapeDtypeStruct((t, w), F32)] + [jax.ShapeDtypeStruct((quads, t, QW), F32)] * 2
    est = 4 * t * QW * q.dtype.itemsize + 2 * t * QW * 4 + 16 * sub * HS * bq * nk * 4
    if kind == "na":
        bspec = pl.BlockSpec((HS,) + bias.shape[1:], lambda j, n: (j, 0, 0, 0))
        in_specs.append(bspec)
        operands.append(bias)
        out_specs.append(bspec)
        out_shape.append(jax.ShapeDtypeStruct(bias.shape, F32))
        est += 4 * _nbytes((HS,) + bias.shape[1:], F32)
    res_ = pl.pallas_call(
        body,
        name=name,
        grid=(quads, nq),
        in_specs=in_specs,
        out_specs=out_specs,
        out_shape=out_shape,
        scratch_shapes=[pltpu.VMEM((t, QW), F32), pltpu.VMEM((t, QW), F32), pltpu.SemaphoreType.DMA((2,))],
        compiler_params=pltpu.CompilerParams(dimension_semantics=("arbitrary", "arbitrary"), vmem_limit_bytes=_vmem(est)),
    )(*operands)
    unquad = lambda a: [(a, i) for i in range(quads)]
    return (res_[0], unquad(res_[1]), unquad(res_[2])) + tuple(res_[3:])


def _na_onehot():
    qc = np.arange(GRID_W)[:, None]
    kc = np.arange(GRID_W)[None, :]
    start = np.clip(qc - NA_KW // 2, 0, GRID_W - NA_KW)
    inwin = (kc >= start) & (kc < start + NA_KW)
    off = kc - qc + (NA_KW - 1)
    e_mat = np.zeros((2, 32, GRID_W, 2, GRID_W), np.float32)
    for e in range(2):
        for c in range(2 * NA_KW - 1):
            e_mat[e, c, :, e, :] = (off == c) & inwin
    neg = np.where(inwin, 0.0, NEG_INF).astype(np.float32)
    neg = np.broadcast_to(neg[:, None, :], (GRID_W, 2, GRID_W)).reshape(1, GRID_W * LANES)
    return jnp.asarray(e_mat.reshape(64, GRID_W * LANES), MXU_DTYPE), jnp.asarray(neg)


def _na_rowpairs(rpb):
    p = jnp.pad(rpb, ((0, 0), (0, 0), (0, 1)))
    return jnp.concatenate([p[:, :-1], p[:, 1:]], axis=-1).reshape(NA_HEADS * (2 * NA_KH - 2), 64)


def _na_bias_table(rpb):
    r2 = _na_rowpairs(rpb)
    e_mat, neg = _na_onehot()

    def body(r_ref, e_ref, n_ref, o_ref):
        hi, mid, lo = _split3(r_ref[...])
        e = e_ref[...]
        o_ref[...] = _dot(hi, e) + _dot(mid, e) + _dot(lo, e) + n_ref[...]

    out = pl.pallas_call(
        body,
        name="na_bias_table",
        out_shape=jax.ShapeDtypeStruct((r2.shape[0], GRID_W * LANES), F32),
        compiler_params=pltpu.CompilerParams(vmem_limit_bytes=_vmem(6 * r2.shape[0] * GRID_W * LANES * 4)),
    )(r2, e_mat, neg)
    return out.reshape(NA_HEADS, 2 * NA_KH - 2, GRID_W, LANES)


def _na_bias_grad(dbt):
    e_mat, _ = _na_onehot()
    flat = dbt.reshape(NA_HEADS * (2 * NA_KH - 2), GRID_W * LANES)

    def body(d_ref, e_ref, o_ref):
        hi, mid, lo = _split3(d_ref[...])
        e = e_ref[...]
        o_ref[...] = _dot_nt(hi, e) + _dot_nt(mid, e) + _dot_nt(lo, e)

    g = pl.pallas_call(
        body,
        name="na_bias_grad",
        out_shape=jax.ShapeDtypeStruct((flat.shape[0], 64), F32),
        compiler_params=pltpu.CompilerParams(vmem_limit_bytes=_vmem(6 * flat.shape[0] * GRID_W * LANES * 4)),
    )(flat, e_mat)
    g = g.reshape(NA_HEADS, 2 * NA_KH - 2, 2, 32)[..., :2 * NA_KW - 1]
    first = jnp.pad(g[:, :, 0], ((0, 0), (0, 1), (0, 0)))
    second = jnp.pad(g[:, :, 1], ((0, 0), (1, 0), (0, 0)))
    return first + second


def _all_gather(arrs, *, name):
    na = len(arrs)

    def body(*refs):
        ins, outs = refs[:na], refs[na:2 * na]
        send_sems, recv_sems, local_sems = refs[2 * na:]
        x, y, c = lax.axis_index("x"), lax.axis_index("y"), lax.axis_index("c")
        me, sibling = (x, y, c), (x, y, 1 - c)
        chips = [(1 - x, y), (x, 1 - y), (1 - x, 1 - y)]

        def rows(a, px, py, pc):
            r = ins[a].shape[0]
            return outs[a].at[pl.ds((4 * px + 2 * py + pc) * r, r), :]

        def copy(a, k, block, to, src=None):
            return pltpu.make_async_remote_copy(
                src_ref=rows(a, *block) if src is None else src, dst_ref=rows(a, *block),
                send_sem=send_sems.at[a, k], recv_sem=recv_sems.at[a, k], device_id=to, device_id_type=MESH)

        mine = [pltpu.make_async_copy(ins[a], rows(a, *me), local_sems.at[a]) for a in range(na)]
        for cp in mine:
            cp.start()
        first = []
        for a in range(na):
            first.append(copy(a, 0, me, sibling, src=ins[a]))
            first += [copy(a, 1 + j, me, (*chip, c), src=ins[a]) for j, chip in enumerate(chips)]
        for cp in first:
            cp.start()
        passed = []
        for j, chip in enumerate(chips):
            for a in range(na):
                copy(a, 1 + j, (*chip, c), me).wait_recv()
                cp = copy(a, 4 + j, (*chip, c), sibling)
                cp.start()
                passed.append(cp)
        for a in range(na):
            copy(a, 0, sibling, me).wait_recv()
        for j, chip in enumerate(chips):
            for a in range(na):
                copy(a, 4 + j, (*chip, 1 - c), me).wait_recv()
        for cp in first + passed:
            cp.wait_send()
        for cp in mine:
            cp.wait()

    return pl.pallas_call(
        body,
        name=name,
        in_specs=[ANY] * na,
        out_specs=[ANY] * na,
        out_shape=[jax.ShapeDtypeStruct((N_DEV * a.shape[0], a.shape[1]), a.dtype) for a in arrs],
        scratch_shapes=[pltpu.SemaphoreType.DMA((na, 7)), pltpu.SemaphoreType.DMA((na, 7)), pltpu.SemaphoreType.DMA((na,))],
    )(*arrs)


HBM = pl.BlockSpec(memory_space=pltpu.HBM)
SEM = pl.BlockSpec(memory_space=pltpu.SEMAPHORE)
EFFECT = pltpu.SideEffectType.DATAFLOW_SIDE_EFFECTING


def _peer_of(k):
    x, y, c = lax.axis_index("x"), lax.axis_index("y"), lax.axis_index("c")
    return x ^ ((k >> 2) & 1), y ^ ((k >> 1) & 1), c ^ (k & 1)


def _split_copies(gather, src_ref, land_ref, send_sems, recv_sems):
    x, y, c = lax.axis_index("x"), lax.axis_index("y"), lax.axis_index("c")
    my = 4 * x + 2 * y + c
    r = src_ref.shape[0] if gather else src_ref.shape[0] // N_DEV
    copies = []
    for k in range(1, N_DEV):
        px, py, pc = _peer_of(k)
        if gather:
            src, dst = src_ref, land_ref.at[pl.ds(my * r, r), :]
        else:
            src, dst = src_ref.at[pl.ds((4 * px + 2 * py + pc) * r, r), :], land_ref.at[k - 1]
        copies.append(pltpu.make_async_remote_copy(
            src_ref=src, dst_ref=dst, send_sem=send_sems.at[k - 1], recv_sem=recv_sems.at[k - 1],
            device_id=(px, py, pc), device_id_type=MESH))
    return copies


def _split_start(srcs, lands, *, gather, name, after=None):
    na = len(srcs)
    extra = [] if after is None else [after]

    def body(*refs):
        src_refs, land_refs = refs[:na], refs[na:2 * na]
        outs = refs[2 * na + len(extra):]
        for a in range(na):
            for cp in _split_copies(gather, src_refs[a], land_refs[a], outs[4 * a], outs[4 * a + 1]):
                cp.start()
        outs[4 * na][...] = jnp.zeros_like(outs[4 * na])

    out_shape, out_specs, aliases = [], [], {}
    for a in range(na):
        out_shape += [pltpu.SemaphoreType.DMA((N_DEV - 1,)), pltpu.SemaphoreType.DMA((N_DEV - 1,)),
                      pltpu.HBM(srcs[a].shape, srcs[a].dtype), pltpu.HBM(lands[a].shape, lands[a].dtype)]
        out_specs += [SEM, SEM, HBM, HBM]
        aliases[a] = 4 * a + 2
        aliases[na + a] = 4 * a + 3
    out_shape.append(jax.ShapeDtypeStruct((8, LANES), F32))
    out_specs.append(pl.BlockSpec(memory_space=pltpu.VMEM))
    res = pl.pallas_call(
        body,
        name=name,
        out_shape=tuple(out_shape),
        in_specs=[HBM] * (2 * na) + [ANY] * len(extra),
        out_specs=tuple(out_specs),
        input_output_aliases=aliases,
        compiler_params=pltpu.CompilerParams(has_side_effects=EFFECT),
    )(*[pltpu.with_memory_space_constraint(a, pltpu.HBM) for a in list(srcs) + list(lands)], *extra)
    return [tuple(res[4 * a:4 * a + 4]) for a in range(na)], res[4 * na][0, 0]


def _split_wait(handles, after, *, gather, name):
    na = len(handles)

    def body(*refs):
        src_refs, land_refs = refs[:na], refs[na:2 * na]
        sems = refs[2 * na:4 * na]
        for a in range(na):
            for cp in _split_copies(gather, src_refs[a], land_refs[a], sems[2 * a], sems[2 * a + 1]):
                cp.wait_send()
                cp.wait_recv()

    srcs = [h[2] for h in handles]
    lands = [h[3] for h in handles]
    sems = [s for h in handles for s in h[:2]]
    res = pl.pallas_call(
        body,
        name=name,
        out_shape=tuple(pltpu.HBM(a.shape, a.dtype) for a in srcs + lands),
        in_specs=[HBM] * (2 * na) + [SEM] * (2 * na) + [ANY],
        out_specs=tuple([HBM] * (2 * na)),
        input_output_aliases={i: i for i in range(2 * na)},
        compiler_params=pltpu.CompilerParams(has_side_effects=EFFECT),
    )(*srcs, *lands, *sems, after)
    return list(res[:na]), list(res[na:])


def _sum8(own, recv, *, name):
    _, r, w = recv.shape
    tr = _pick(r, (256, 128, 64, 32, 16, 8))

    def body(own_ref, a_ref, o_ref):
        acc = own_ref[...].astype(F32)
        for i in range(N_DEV - 1):
            acc = acc + a_ref[i].astype(F32)
        o_ref[...] = acc

    return pl.pallas_call(
        body,
        name=name,
        grid=(r // tr,),
        in_specs=[pl.BlockSpec((tr, w), lambda i: (i, 0)), pl.BlockSpec((N_DEV - 1, tr, w), lambda i: (0, i, 0))],
        out_specs=pl.BlockSpec((tr, w), lambda i: (i, 0)),
        out_shape=jax.ShapeDtypeStruct((r, w), F32),
        compiler_params=pltpu.CompilerParams(dimension_semantics=("parallel",), vmem_limit_bytes=_vmem(4 * N_DEV * tr * w * 4)),
    )(own, recv)


def _adamw(w, g, m, v, *, name):
    def fn(rows, _):
        wv, gv, mv, vv = rows
        m1 = ADAM_B1 * mv + (1.0 - ADAM_B1) * gv
        v1 = ADAM_B2 * vv + (1.0 - ADAM_B2) * jnp.square(gv)
        m_hat = m1 / (1.0 - ADAM_B1 ** ADAM_STEP)
        v_hat = v1 / (1.0 - ADAM_B2 ** ADAM_STEP)
        delta = -ADAM_LR * (m_hat / (jnp.sqrt(v_hat) + ADAM_EPS) + ADAM_WD * wv)
        return [delta, m1, v1], []

    c = w.shape[1]
    return _rowmap(fn, [w, g, m, v], [], [(c, F32)] * 3, [], name=name)


_SMALL = ("b_ada", "g_norm1", "g_norm2", "b_gate", "g_qa", "g_ka", "g_qb", "g_kb", "rpb", "loss")


def _pack_small(parts):
    flat = []
    for nme in _SMALL:
        a = parts[nme].reshape(-1).astype(F32)
        flat.append(jnp.pad(a, (0, (-a.shape[0]) % LANES)))
    flat = jnp.concatenate(flat)
    flat = jnp.pad(flat, (0, (-flat.shape[0]) % (8 * LANES)))
    return flat.reshape(-1, LANES)


def _unpack_small(packed, shapes):
    flat = packed.reshape(-1)
    out, pos = {}, 0
    for nme in _SMALL:
        n = int(np.prod(shapes[nme]))
        out[nme] = flat[pos:pos + n].reshape(shapes[nme])
        pos += n + (-n) % LANES
    return out


def _to_class(a, d):
    t, w = a.shape
    return a if d == 1 else a.reshape(t // d, d, w).transpose(1, 0, 2).reshape(t, w)


def _from_class(a, d):
    t, w = a.shape
    return a if d == 1 else a.reshape(d, t // d, w).transpose(1, 0, 2).reshape(t, w)


def kernel(x, c, w_ada, b_ada, g_norm1, g_norm2, w_in, b_gate, g_qa, g_ka, g_qb, g_kb, rpb, w_proj_a, w_proj_b, w_o, w_ffn_in, w_ffn_out, loss_target, m_w_ada, m_b_ada, m_g_norm1, m_g_norm2, m_w_in, m_b_gate, m_g_qa, m_g_ka, m_g_qb, m_g_kb, m_rpb, m_w_proj_a, m_w_proj_b, m_w_o, m_w_ffn_in, m_w_ffn_out, v_w_ada, v_b_ada, v_g_norm1, v_g_norm2, v_w_in, v_b_gate, v_g_qa, v_g_ka, v_g_qb, v_g_kb, v_rpb, v_w_proj_a, v_w_proj_b, v_w_o, v_w_ffn_in, v_w_ffn_out):
    t, d = x.shape[1], x.shape[2]
    d_ff = w_ffn_out.shape[1] * N_DEV
    me = 4 * lax.axis_index("x") + 2 * lax.axis_index("y") + lax.axis_index("c")
    xt, tgt = x.reshape(t, d), loss_target.reshape(t, d)
    ones = _head_ones()

    shards = [s.astype(WIRE_DTYPE) for s in (w_in[0].T, w_ffn_in[0].T, w_proj_a[0].T, w_proj_b[0].T, w_o[0], w_ffn_out[0])]
    lands = [lax.dynamic_update_slice(lax.empty((N_DEV * s.shape[0], s.shape[1]), s.dtype), s, (me * s.shape[0], 0))
             for s in shards]

    c_all = _all_gather([jnp.pad(c, ((0, 7), (0, 0)))], name="gather_c")[0][::8]
    c_all = jnp.pad(c_all, ((0, LANES - N_DEV), (0, 0)))

    def mod_body(c_ref, w_ref, b_ref, o_ref, act_ref):
        act = _silu(c_ref[...])
        act_ref[...] = act
        hi, mid, lo = _split3(act)
        w = w_ref[...]
        whi, wmid, wlo = _split3(w)
        acc = _dot(hi, whi) + (_dot(hi, wmid) + _dot(mid, whi)) + (_dot(hi, wlo) + _dot(mid, wmid) + _dot(lo, whi))
        o_ref[...] = acc + b_ref[...]

    ncol = w_ada.shape[2]
    b_ada_mine = lax.dynamic_slice(b_ada, (0, me * ncol), (1, ncol))
    mod_part, c_act = pl.pallas_call(
        mod_body,
        name="ada_mod",
        out_shape=[jax.ShapeDtypeStruct((LANES, ncol), F32), jax.ShapeDtypeStruct((LANES, d), F32)],
        compiler_params=pltpu.CompilerParams(vmem_limit_bytes=_vmem(6 * d * ncol * 4)),
    )(c_all, w_ada[0], b_ada_mine)
    mod_all = _all_gather([mod_part[:N_DEV]], name="gather_mod")[0].reshape(N_DEV, N_DEV, ncol)
    mod = lax.dynamic_index_in_dim(mod_all, me, axis=1, keepdims=False).reshape(6, d)
    sh1, sc1, gt1, sh2, sc2, gt2 = [mod[i:i + 1] for i in range(6)]

    def norm_fwd(rows, vecs):
        (xv,), (g, sc, sh) = rows, vecs
        return [xv * _rms(xv) * g * (1.0 + sc) + sh], []

    w_handles, w_token = _split_start(shards, lands, gather=True, after=mod, name="gather_weights_start")
    (h,) = _rowmap(norm_fwd, [xt], [g_norm1 + w_token, sc1, sh1], [(d, MXU_DTYPE)], [], name="norm1")
    n_a, n_b = 3 * WA, 3 * WB
    (w_in_t,) = _split_wait(w_handles[:1], h, gather=True, name="gather_w_in_wait")[1]
    w_in_a, w_in_b, w_in_g = w_in_t[:n_a], w_in_t[n_a:n_a + n_b], w_in_t[n_a + n_b:]
    qkv_a = _mm(h, w_in_a, tb=True, out_dtype=ACT_DTYPE, name="proj_a")
    qkv_b = _mm(h, w_in_b, tb=True, out_dtype=ACT_DTYPE, name="proj_b")
    gates = _mm(h, w_in_g, tb=True, out_dtype=ACT_DTYPE, name="proj_gates")

    rot_c, rot_lo, rot_hi = _rot_tables(t)
    tile_g = lambda g, heads: jnp.tile(g, (1, heads))

    def qk_fwd(width, rotate):
        def fn(rows, vecs):
            xv = rows[0]
            gq, gk, on = vecs
            outs = []
            for i, g in enumerate((gq, gk)):
                xi = xv[:, i * width:(i + 1) * width]
                r = lax.rsqrt(_headsum(xi * xi, on) * (1.0 / HEAD_DIM) + EPS)
                yi = xi * r * g
                if rotate:
                    yi = _rot(yi, rows[1], rows[2], rows[3])
                outs.append(yi)
            outs.append(xv[:, 2 * width:])
            return outs, []
        return fn

    qa, ka, va = _rowmap(qk_fwd(WA, False), [qkv_a], [tile_g(g_qa, NA_HEADS), tile_g(g_ka, NA_HEADS), ones],
                         [(WA, MXU_DTYPE)] * 3, [], name="qknorm_a")
    qb, kb, vb = _rowmap(qk_fwd(WB, True), [qkv_b, rot_c, rot_lo, rot_hi],
                         [tile_g(g_qb, DIL_HEADS), tile_g(g_kb, DIL_HEADS), ones], [(WB, MXU_DTYPE)] * 3, [], name="qknorm_b")

    bias_tab = _na_bias_table(rpb[0])
    o_a, lse_a = _attn_fwd(qa, ka, va, kind="na", bias=bias_tab, name="na_fwd")

    grp = []
    for g, (_, dil) in enumerate(DIL_CONFIGS):
        sl = slice(g * WB_OUT, (g + 1) * WB_OUT)
        qg, kg, vg = [_to_class(a[:, sl], dil) for a in (qb, kb, vb)]
        og, lg = _attn_fwd(qg, kg, vg, kind="dil", seg=t // dil, name=f"dil_fwd{g}")
        grp.append(dict(q=qg, k=kg, v=vg, o=_from_class(og, dil), lse=_from_class(lg, dil), lse_c=lg, dil=dil))

    def merge_fwd(rows, _):
        o0, o1, o2, l0, l1, l2 = rows
        mx = jnp.maximum(jnp.maximum(l0, l1), l2)
        e0, e1, e2 = jnp.exp(l0 - mx), jnp.exp(l1 - mx), jnp.exp(l2 - mx)
        s = e0 + e1 + e2
        return [(e0 / s) * o0 + (e1 / s) * o1 + (e2 / s) * o2], []

    (o_b,) = _rowmap(merge_fwd, [gr["o"] for gr in grp] + [gr["lse"] for gr in grp], [], [(WB_OUT, F32)], [], name="dil_merge")

    w_pa_t, w_pb_t, w_o_f = _split_wait(w_handles[2:5], o_b, gather=True, name="gather_w_out_wait")[1]
    pa = _mm(o_a, w_pa_t, tb=True, out_dtype=ACT_DTYPE, name="proj_out_a")
    pb = _mm(o_b, w_pb_t, tb=True, out_dtype=ACT_DTYPE, name="proj_out_b")

    def gate_fwd(rows, vecs):
        gv, pav, pbv = rows
        sg = jax.nn.sigmoid(gv + vecs[0])
        return [sg[:, :d] * pav + sg[:, d:] * pbv], []

    (merged,) = _rowmap(gate_fwd, [gates, pa, pb], [b_gate], [(d, MXU_DTYPE)], [], name="gate_merge")
    def resid_norm(av, rows, vecs):
        (xv,), (gt, g, sc, sh) = rows, vecs
        x1v = xv + gt * av
        return [av, x1v, x1v * _rms(x1v) * g * (1.0 + sc) + sh], []

    att, x1, h2 = _mm_rows(merged, w_o_f, resid_norm, [xt], [gt1, g_norm2, sc2, sh2],
                           [(d, F32), (d, F32), (d, MXU_DTYPE)], [], name="proj_o_resid_norm2")

    w_ffn_in_t, w_ffn_out_f = _split_wait([w_handles[1], w_handles[5]], h2, gather=True, name="gather_w_ffn_wait")[1]
    w_ffn_a, w_ffn_up = w_ffn_in_t[:d_ff], w_ffn_in_t[d_ff:]

    def swiglu_fwd(prods, _):
        a, up = prods
        return [a, up, _silu(a) * up]

    ua, uu, f = _mm_ew(h2, [w_ffn_a, w_ffn_up], swiglu_fwd, [], [ACT_DTYPE, ACT_DTYPE, MXU_DTYPE], name="ffn_in_swiglu")

    def loss_fn(yv, rows, vecs):
        (x1v, tv), gt = rows, vecs[0]
        err = x1v + gt * yv - tv
        dout = err * (1.0 / d)
        return [dout, dout * gt], [_colsum(err * err), _colsum(dout * yv)]

    dout, dy2, err2, dgt2 = _mm_rows(f, w_ffn_out_f, loss_fn, [x1, tgt], [gt2], [(d, F32), (d, MXU_DTYPE)], [d, d],
                                     name="ffn_out_loss")

    dw_ffn_out = _mm(f, dy2, ta=True, out_dtype=WIRE_DTYPE, name="wgrad_ffn_out")
    def swiglu_bwd(prods, rows):
        (dfv,), (a, up) = prods, rows
        sg = jax.nn.sigmoid(a)
        return [dfv * up * (sg * (1.0 + a * (1.0 - sg))), dfv * (a * sg)]

    da, dup = _mm_ew(dy2, [w_ffn_out_f], swiglu_bwd, [ua, uu], [MXU_DTYPE, MXU_DTYPE], name="dgrad_ffn_out_swiglu_bwd")
    dw_ffn_in_t = jnp.concatenate([_mm(da, h2, ta=True, out_dtype=WIRE_DTYPE, name="wgrad_ffn_in_a"),
                                   _mm(dup, h2, ta=True, out_dtype=WIRE_DTYPE, name="wgrad_ffn_in_up")], axis=0)
    land7 = lambda a: lax.empty((N_DEV - 1, a.shape[0] // N_DEV, a.shape[1]), a.dtype)
    own_block = lambda a: lax.dynamic_slice(a, (me * (a.shape[0] // N_DEV), 0), (a.shape[0] // N_DEV, a.shape[1]))
    g_ffn = [dw_ffn_in_t, dw_ffn_out]
    h_ffn, tok_ffn = _split_start(g_ffn, [land7(a) for a in g_ffn], gather=False, name="exchange_ffn_start")
    def norm_bwd(dh, xv, g, sc):
        r = _rms(xv)
        xh = xv * r
        dxh = dh * g * (1.0 + sc)
        dxv = r * (dxh - xh * jnp.mean(dxh * xh, axis=-1, keepdims=True))
        return dxv, [_colsum(dh), _colsum(dh * xh * g), _colsum(dh * xh * (1.0 + sc))]

    def norm2_bwd(dhv, rows, vecs):
        (x1v, dov, av), (g, sc, gt) = rows, vecs
        dxv, sums = norm_bwd(dhv, x1v, g, sc)
        dx1v = dov + dxv
        return [dx1v, dx1v * gt], sums + [_colsum(dx1v * av)]

    dx1, datt, dsh2, dsc2, dg2, dgt1 = _mm_parts_rows(
        [(da, w_ffn_a), (dup, w_ffn_up)], norm2_bwd, [x1, dout, att], [g_norm2 + tok_ffn, sc2, gt1],
        [(d, F32), (d, MXU_DTYPE)], [d] * 4, name="dgrad_ffn_in_norm2_bwd")
    dw_o = _mm(merged, datt, ta=True, out_dtype=WIRE_DTYPE, name="wgrad_o")
    dmerged = _mm(datt, w_o_f, tb=True, out_dtype=ACT_DTYPE, name="dgrad_o")

    def gate_bwd(rows, vecs):
        dm, gv, pav, pbv = rows
        sg = jax.nn.sigmoid(gv + vecs[0])
        ga, gb = sg[:, :d], sg[:, d:]
        dgp = jnp.concatenate([dm * pav * ga * (1.0 - ga), dm * pbv * gb * (1.0 - gb)], axis=1)
        return [dm * ga, dm * gb, dgp], [_colsum(dgp)]

    dpa, dpb, dgates, db_gate = _rowmap(gate_bwd, [dmerged, gates, pa, pb], [b_gate],
                                        [(d, MXU_DTYPE), (d, MXU_DTYPE), (2 * d, MXU_DTYPE)], [2 * d], name="gate_bwd")
    dw_pa_t = _mm(dpa, o_a, ta=True, out_dtype=WIRE_DTYPE, name="wgrad_proj_a")
    dw_pb_t = _mm(dpb, o_b, ta=True, out_dtype=WIRE_DTYPE, name="wgrad_proj_b")
    g_out = [dw_pa_t, dw_pb_t, dw_o]
    h_out, tok_out = _split_start(g_out, [land7(a) for a in g_out], gather=False, name="exchange_out_start")
    do_a = _mm(dpa, w_pa_t, name="dgrad_proj_a")
    do_b = _mm(dpb, w_pb_t, name="dgrad_proj_b")

    def delta_a(rows, vecs):
        return [_headsum(rows[0] * rows[1], vecs[0])], []

    (dterm_a,) = _rowmap(delta_a, [do_a, o_a], [ones + tok_out.astype(ones.dtype)], [(WA, F32)], [], name="na_delta")
    dqa, dka, dva, dbias = _attn_bwd(qa, ka, va, do_a, dterm_a, lse_a, kind="na", bias=bias_tab, name="na_bwd")
    g_rpb = _na_bias_grad(dbias)

    def merge_bwd(rows, vecs):
        dob, o0, o1, o2, l0, l1, l2 = rows
        on = vecs[0]
        mx = jnp.maximum(jnp.maximum(l0, l1), l2)
        e0, e1, e2 = jnp.exp(l0 - mx), jnp.exp(l1 - mx), jnp.exp(l2 - mx)
        s = e0 + e1 + e2
        ws = [e0 / s, e1 / s, e2 / s]
        dws = [_headsum(dob * o, on) for o in (o0, o1, o2)]
        mean = ws[0] * dws[0] + ws[1] * dws[1] + ws[2] * dws[2]
        return [w * dob for w in ws] + [w * mean for w in ws], []

    mb = _rowmap(merge_bwd, [do_b] + [gr["o"] for gr in grp] + [gr["lse"] for gr in grp], [ones],
                 [(WB_OUT, F32)] * 6, [], name="dil_merge_bwd")
    dqb, dkb, dvb = [], [], []
    for g, gr in enumerate(grp):
        dil = gr["dil"]
        dq, dk, dv = _attn_bwd(gr["q"], gr["k"], gr["v"], _to_class(mb[g], dil), _to_class(mb[3 + g], dil), gr["lse_c"],
                               kind="dil", seg=t // dil, name=f"dil_bwd{g}")
        dqb.append(_from_class(dq, dil))
        dkb.append(_from_class(dk[0][0][0], dil))
        dvb.append(_from_class(dv[0][0][0], dil))

    def qk_bwd(width, rotate, nparts):
        def fn(rows, vecs):
            gq, gk, on = vecs
            xv = rows[0]
            pos = 1
            if rotate:
                rc, rlo, rhi = rows[1:4]
                pos = 4
            cat = lambda parts: parts[0] if len(parts) == 1 else jnp.concatenate(parts, axis=1)
            ends = np.cumsum((pos,) + nparts)
            dq, dk, dv = [cat(rows[ends[i]:ends[i + 1]]) for i in range(3)]
            outs, sums = [], []
            for i, (dy, g) in enumerate(((dq, gq), (dk, gk))):
                if rotate:
                    dy = _rot(dy, rc, -rlo, -rhi)
                xi = xv[:, i * width:(i + 1) * width]
                r = lax.rsqrt(_headsum(xi * xi, on) * (1.0 / HEAD_DIM) + EPS)
                xh = xi * r
                dxh = dy * g
                outs.append(r * (dxh - xh * (_headsum(dxh * xh, on) * (1.0 / HEAD_DIM))))
                sums.append(_colsum(dy * xh))
            return [jnp.concatenate(outs + [dv], axis=1)], sums
        return fn

    dqkv_a, dg_qa, dg_ka = _rowmap(qk_bwd(WA, False, (1, len(dka), len(dva))), [qkv_a, dqa] + dka + dva,
                                   [tile_g(g_qa, NA_HEADS), tile_g(g_ka, NA_HEADS), ones],
                                   [(3 * WA, MXU_DTYPE)], [WA, WA], name="qknorm_a_bwd")
    dqkv_b, dg_qb, dg_kb = _rowmap(qk_bwd(WB, True, (3, 3, 3)), [qkv_b, rot_c, rot_lo, rot_hi] + dqb + dkb + dvb,
                                   [tile_g(g_qb, DIL_HEADS), tile_g(g_kb, DIL_HEADS), ones],
                                   [(3 * WB, MXU_DTYPE)], [WB, WB], name="qknorm_b_bwd")

    dw_in_t = jnp.concatenate([
        _mm(dqkv_a, h, ta=True, out_dtype=WIRE_DTYPE, name="wgrad_in_a"),
        _mm(dqkv_b, h, ta=True, out_dtype=WIRE_DTYPE, name="wgrad_in_b"),
        _mm(dgates, h, ta=True, out_dtype=WIRE_DTYPE, name="wgrad_in_gates")], axis=0)
    h_in, tok_in = _split_start([dw_in_t], [land7(dw_in_t)], gather=False, name="exchange_in_start")
    def norm1_bwd(dhv, rows, vecs):
        xv, dx1v = rows
        dxv, sums = norm_bwd(dhv, xv, vecs[0], vecs[1])
        return [dx1v + dxv], sums

    grad_x, dsh1, dsc1, dg1 = _mm_parts_rows(
        [(dqkv_a, w_in_a), (dqkv_b, w_in_b), (dgates, w_in_g)], norm1_bwd, [xt, dx1], [g_norm1 + tok_in, sc1],
        [(d, F32)], [d] * 3, name="dgrad_in_norm1_bwd")

    heads_sum = lambda a, heads: a.reshape(heads, HEAD_DIM).sum(axis=0)
    dmod = jnp.concatenate([dsh1, dsc1, dgt1, dsh2, dsc2, dgt2], axis=1)
    local_small = _pack_small(dict(
        b_ada=dmod, g_norm1=dg1, g_norm2=dg2, b_gate=db_gate, g_qa=heads_sum(dg_qa, NA_HEADS),
        g_ka=heads_sum(dg_ka, NA_HEADS), g_qb=heads_sum(dg_qb, DIL_HEADS), g_kb=heads_sum(dg_kb, DIL_HEADS),
        rpb=g_rpb, loss=(0.5 / d) * jnp.sum(err2)))
    srows = local_small.shape[0]
    small_all = _all_gather([local_small], name="gather_small")[0].reshape(N_DEV, srows, LANES)
    small_sum = _sum8(small_all[0], small_all[1:], name="sum_small")
    small_shapes = dict(b_ada=b_ada.shape, g_norm1=g_norm1.shape, g_norm2=g_norm2.shape, b_gate=b_gate.shape,
                        g_qa=g_qa.shape, g_ka=g_ka.shape, g_qb=g_qb.shape, g_kb=g_kb.shape, rpb=rpb.shape, loss=())
    small_w = dict(b_ada=b_ada, g_norm1=g_norm1, g_norm2=g_norm2, b_gate=b_gate, g_qa=g_qa, g_ka=g_ka, g_qb=g_qb,
                   g_kb=g_kb, rpb=rpb, loss=jnp.zeros((), F32))
    small_m = dict(b_ada=m_b_ada, g_norm1=m_g_norm1, g_norm2=m_g_norm2, b_gate=m_b_gate, g_qa=m_g_qa, g_ka=m_g_ka,
                   g_qb=m_g_qb, g_kb=m_g_kb, rpb=m_rpb, loss=jnp.zeros((), F32))
    small_v = dict(b_ada=v_b_ada, g_norm1=v_g_norm1, g_norm2=v_g_norm2, b_gate=v_b_gate, g_qa=v_g_qa, g_ka=v_g_ka,
                   g_qb=v_g_qb, g_kb=v_g_kb, rpb=v_rpb, loss=jnp.zeros((), F32))
    s_delta, s_m, s_v = _adamw(_pack_small(small_w), small_sum, _pack_small(small_m), _pack_small(small_v), name="adamw_small")
    gs = _unpack_small(small_sum, small_shapes)
    ds_, ms_, vs_ = [_unpack_small(a, small_shapes) for a in (s_delta, s_m, s_v)]

    dmod_all = small_all[:, :6 * d // LANES].reshape(N_DEV, 6 * d)
    dmod_mine = jnp.pad(lax.dynamic_slice(dmod_all, (0, me * ncol), (N_DEV, ncol)), ((0, LANES - N_DEV), (0, 0)))

    def wada_body(c_ref, dm_ref, o_ref):
        chi, cmid, clo = _split3(c_ref[...])
        dhi, dmid, dlo = _split3(dm_ref[...])
        o_ref[...] = (_dot_tn(chi, dhi) + (_dot_tn(chi, dmid) + _dot_tn(cmid, dhi))
                      + (_dot_tn(chi, dlo) + _dot_tn(cmid, dmid) + _dot_tn(clo, dhi)))

    g_w_ada = pl.pallas_call(
        wada_body,
        name="wgrad_ada",
        out_shape=jax.ShapeDtypeStruct((d, ncol), F32),
        compiler_params=pltpu.CompilerParams(vmem_limit_bytes=_vmem(4 * d * ncol * 4)),
    )(c_act, dmod_mine)

    sent, recv = _split_wait(h_in + h_ffn + h_out, small_sum, gather=False, name="exchange_wait")
    names = ("w_in", "w_ffn_in", "w_ffn_out", "w_proj_a", "w_proj_b", "w_o")
    transposed = (True, True, False, True, True, False)
    big_g = {}
    for nme, own, r, tr in zip(names, sent, recv, transposed):
        s = _sum8(own_block(own), r, name=f"sum_{nme}")
        big_g[nme] = s.T if tr else s
    big_g["w_ada"] = g_w_ada
    big_w = dict(w_ada=w_ada, w_in=w_in, w_proj_a=w_proj_a, w_proj_b=w_proj_b, w_o=w_o, w_ffn_in=w_ffn_in, w_ffn_out=w_ffn_out)
    big_m = dict(w_ada=m_w_ada, w_in=m_w_in, w_proj_a=m_w_proj_a, w_proj_b=m_w_proj_b, w_o=m_w_o, w_ffn_in=m_w_ffn_in, w_ffn_out=m_w_ffn_out)
    big_v = dict(w_ada=v_w_ada, w_in=v_w_in, w_proj_a=v_w_proj_a, w_proj_b=v_w_proj_b, w_o=v_w_o, w_ffn_in=v_w_ffn_in, w_ffn_out=v_w_ffn_out)
    grads, deltas, new_m, new_v = {}, {}, {}, {}
    for nme in big_w:
        dl, m1, v1 = _adamw(big_w[nme][0], big_g[nme], big_m[nme][0], big_v[nme][0], name=f"adamw_{nme}")
        grads[nme], deltas[nme], new_m[nme], new_v[nme] = big_g[nme][None], dl[None], m1[None], v1[None]
    for nme in _SMALL[:-1]:
        grads[nme], deltas[nme], new_m[nme], new_v[nme] = gs[nme], ds_[nme], ms_[nme], vs_[nme]

    order = ("w_ada", "b_ada", "g_norm1", "g_norm2", "w_in", "b_gate", "g_qa", "g_ka", "g_qb", "g_kb", "rpb",
             "w_proj_a", "w_proj_b", "w_o", "w_ffn_in", "w_ffn_out")
    return (gs["loss"], grad_x[None], *[grads[n] for n in order], *[deltas[n] for n in order],
            *[new_m[n] for n in order], *[new_v[n] for n in order])
```

```python
import functools

import numpy as np
import jax
import jax.numpy as jnp
from jax import lax
from jax.experimental import pallas as pl
from jax.experimental.pallas import tpu as pltpu

F32 = jnp.float32
MXU_DTYPE = jnp.bfloat16
WIRE_DTYPE = jnp.bfloat16
ACT_DTYPE = jnp.bfloat16

HEAD_DIM = 64
GRID_W = 64
NA_HEADS = 8
NA_KH = 8
NA_KW = 16
DIL_CONFIGS = ((128, 1), (512, 4), (2048, 16))
DIL_HEADS_PER_GROUP = 4
DIL_HEADS = DIL_HEADS_PER_GROUP * len(DIL_CONFIGS)
DIL_HALF = 64
ROT_DIM = HEAD_DIM // 4
ROPE_THETA = 500000.0
EPS = 1e-6
NEG_INF = -1e30
WA = NA_HEADS * HEAD_DIM
WB = DIL_HEADS * HEAD_DIM
WB_OUT = DIL_HEADS_PER_GROUP * HEAD_DIM
ADAM_LR = 0.001
ADAM_B1 = 0.9
ADAM_B2 = 0.999
ADAM_EPS = 1e-08
ADAM_WD = 0.01
ADAM_STEP = 10

N_DEV = 8
LANES = 128
VMEM_CAP = 60 * 2**20
VMEM_FLOOR = 56 * 2**20
MESH = pl.DeviceIdType.MESH
ANY = pl.BlockSpec(memory_space=pl.ANY)


def _vmem(nbytes):
    return int(min(VMEM_CAP, max(VMEM_FLOOR, nbytes * 5 // 4 + 4 * 2**20)))


def _pick(dim, cands):
    for c in cands:
        if c <= dim and dim % c == 0:
            return c
    return dim


def _nbytes(shape, dtype):
    return int(np.prod(shape)) * jnp.dtype(dtype).itemsize


def _dot(a, b, dims=((1,), (0,))):
    return lax.dot_general(a.astype(MXU_DTYPE), b.astype(MXU_DTYPE), (dims, ((), ())), preferred_element_type=F32)


def _dot_nt(a, b):
    return _dot(a, b, ((1,), (1,)))


def _dot_tn(a, b):
    return _dot(a, b, ((0,), (0,)))


def _split3(a):
    hi = a.astype(jnp.bfloat16)
    r1 = a - hi.astype(F32)
    mid = r1.astype(jnp.bfloat16)
    lo = (r1 - mid.astype(F32)).astype(jnp.bfloat16)
    return hi, mid, lo


def _silu(x):
    return x * jax.nn.sigmoid(x)


def _divisors(dim, unit):
    return [c for c in range(unit, dim + 1, unit) if dim % c == 0] or [dim]


def _mm_tiles(m, n, kdim, a_item, b_item, o_item):
    step_us, hbm_bytes_per_us, flops_per_us, budget = 0.35, 3.0e6, 8.0e8, 40 * 2**20
    best = None
    for tm in _divisors(m, LANES):
        for tn in _divisors(n, LANES):
            for tk in _divisors(kdim, LANES):
                gm, gn, gk = m // tm, n // tn, kdim // tk
                vmem = 2 * (tm * tk * a_item + tk * tn * b_item + tm * tn * o_item) + 2 * (tm * tk + tk * tn)
                vmem += tm * tn * 4 * ((1 if gk > 1 else 0) + 1)
                if vmem > budget:
                    continue
                a_reads = m * kdim * a_item * (gn if gk > 1 else 1)
                traffic = a_reads + kdim * n * b_item * gm + m * n * o_item
                cost = gm * gn * gk * step_us + max(traffic / hbm_bytes_per_us, 2.0 * m * n * kdim / flops_per_us)
                if best is None or cost < best[0]:
                    best = (cost, tm, tn, tk)
    return best[1:]


def _mm(a, b, *, name, ta=False, tb=False, out_dtype=F32):
    if ta:
        kdim, m = a.shape
    else:
        m, kdim = a.shape
    n = b.shape[0] if tb else b.shape[1]
    assert b.shape[1 if tb else 0] == kdim
    tm, tn, tk = _mm_tiles(m, n, kdim, a.dtype.itemsize, b.dtype.itemsize, jnp.dtype(out_dtype).itemsize)
    gm, gn, gk = m // tm, n // tn, kdim // tk

    a_spec = pl.BlockSpec((tk, tm), lambda i, j, k: (k, i)) if ta else pl.BlockSpec((tm, tk), lambda i, j, k: (i, k))
    b_spec = pl.BlockSpec((tn, tk), lambda i, j, k: (j, k)) if tb else pl.BlockSpec((tk, tn), lambda i, j, k: (k, j))
    o_spec = pl.BlockSpec((tm, tn), lambda i, j, k: (i, j))
    a_dims = (0,) if ta else (1,)
    b_dims = (1,) if tb else (0,)

    def body(a_ref, b_ref, o_ref, *scratch):
        if gk == 1:
            o_ref[...] = _dot(a_ref[...], b_ref[...], (a_dims, b_dims)).astype(o_ref.dtype)
            return
        (acc_ref,) = scratch
        k = pl.program_id(2)

        @pl.when(k == 0)
        def _():
            acc_ref[...] = jnp.zeros_like(acc_ref)

        acc_ref[...] += _dot(a_ref[...], b_ref[...], (a_dims, b_dims))

        @pl.when(k == gk - 1)
        def _():
            o_ref[...] = acc_ref[...].astype(o_ref.dtype)

    est = 2 * (tm * tk * a.dtype.itemsize + tk * tn * b.dtype.itemsize + tm * tn * jnp.dtype(out_dtype).itemsize)
    est += tm * tn * 4 + 2 * (tm * tk + tk * tn) * 2
    return pl.pallas_call(
        body,
        name=name,
        grid=(gm, gn, gk),
        in_specs=[a_spec, b_spec],
        out_specs=o_spec,
        out_shape=jax.ShapeDtypeStruct((m, n), out_dtype),
        scratch_shapes=[pltpu.VMEM((tm, tn), F32)] if gk > 1 else [],
        compiler_params=pltpu.CompilerParams(
            dimension_semantics=("parallel", "parallel", "arbitrary"), vmem_limit_bytes=_vmem(est)
        ),
    )(a, b)


def _row_tile(m, fixed_bytes, bytes_per_row, budget=46 * 2**20):
    fits = [tm for tm in _divisors(m, LANES) if fixed_bytes + tm * bytes_per_row <= budget]
    return max(fits) if fits else _divisors(m, LANES)[0]


def _mm_parts_rows(parts, fn, rows, vecs, outs, reds, *, name):
    m, n = parts[0][0].shape[0], parts[0][1].shape[1]
    npart, nr, nv, no = len(parts), len(rows), len(vecs), len(outs)
    row_bytes = sum(r.shape[1] * r.dtype.itemsize for r in rows) + sum(w * jnp.dtype(dt).itemsize for (w, dt) in outs)
    a_row_bytes = sum(a.shape[1] * a.dtype.itemsize for a, _ in parts)
    fixed = 2 * sum(_nbytes(b.shape, b.dtype) for _, b in parts)
    per_row = 2 * (a_row_bytes + row_bytes) + n * 4 * 5
    tm = _row_tile(m, fixed, per_row)
    sub = min(tm, 2 * LANES)

    def body(*refs):
        ab = refs[:2 * npart]
        row_refs, vec_refs = refs[2 * npart:2 * npart + nr], refs[2 * npart + nr:2 * npart + nr + nv]
        out_refs = refs[2 * npart + nr + nv:2 * npart + nr + nv + no]
        red_refs = refs[2 * npart + nr + nv + no:]
        if red_refs:
            @pl.when(pl.program_id(0) == 0)
            def _():
                for ref in red_refs:
                    ref[...] = jnp.zeros_like(ref)

        vecs_v = [v[...] for v in vec_refs]
        for s0 in range(0, tm, sub):
            sl = slice(s0, s0 + sub)
            r = _dot(ab[0][sl, :], ab[1][...])
            for p in range(1, npart):
                r = r + _dot(ab[2 * p][sl, :], ab[2 * p + 1][...])
            o, rd = fn(r, [x[sl, :].astype(F32) for x in row_refs], vecs_v)
            for ref, val in zip(out_refs, o):
                ref[sl, :] = val.astype(ref.dtype)
            for ref, val in zip(red_refs, rd):
                ref[...] += val

    in_specs, operands = [], []
    for a, b in parts:
        in_specs += [pl.BlockSpec((tm, a.shape[1]), lambda i: (i, 0)), pl.BlockSpec(b.shape, lambda i: (0, 0))]
        operands += [a, b]
    in_specs += [pl.BlockSpec((tm, r.shape[1]), lambda i: (i, 0)) for r in rows]
    in_specs += [pl.BlockSpec(v.shape, functools.partial(lambda nd, i: (0,) * nd, v.ndim)) for v in vecs]
    out_specs = [pl.BlockSpec((tm, w), lambda i: (i, 0)) for (w, _) in outs]
    out_specs += [pl.BlockSpec((1, w), lambda i: (0, 0)) for w in reds]
    out_shape = [jax.ShapeDtypeStruct((m, w), dt) for (w, dt) in outs] + [jax.ShapeDtypeStruct((1, w), F32) for w in reds]
    return pl.pallas_call(
        body,
        name=name,
        grid=(m // tm,),
        in_specs=in_specs,
        out_specs=out_specs,
        out_shape=out_shape,
        compiler_params=pltpu.CompilerParams(dimension_semantics=("arbitrary",), vmem_limit_bytes=_vmem(fixed + tm * per_row)),
    )(*operands, *rows, *vecs)


def _mm_ew(a, bs, fn, rows, outs, *, name):
    m, kdim = a.shape
    n = bs[0].shape[0]
    nb, nr, no = len(bs), len(rows), len(outs)
    cw = _pick(n, (2 * LANES, LANES))
    fixed = 2 * nb * n * kdim * bs[0].dtype.itemsize
    per_row = 2 * (kdim * a.dtype.itemsize + n * (sum(r.dtype.itemsize for r in rows) + sum(jnp.dtype(dt).itemsize for dt in outs)))
    per_row += cw * 4 * 4 * (nb + 4)
    tm = _row_tile(m, fixed, per_row)

    def body(*refs):
        a_ref, b_refs = refs[0], refs[1:1 + nb]
        row_refs, out_refs = refs[1 + nb:1 + nb + nr], refs[1 + nb + nr:]
        av = a_ref[...]
        for c0 in range(0, n, cw):
            cols = slice(c0, c0 + cw)
            o = fn([_dot_nt(av, b[cols, :]) for b in b_refs], [x[:, cols].astype(F32) for x in row_refs])
            for ref, val in zip(out_refs, o):
                ref[:, cols] = val.astype(ref.dtype)

    tile = pl.BlockSpec((tm, n), lambda i: (i, 0))
    return pl.pallas_call(
        body,
        name=name,
        grid=(m // tm,),
        in_specs=[pl.BlockSpec((tm, kdim), lambda i: (i, 0))] + [pl.BlockSpec((n, kdim), lambda i: (0, 0))] * nb + [tile] * nr,
        out_specs=[tile] * no,
        out_shape=[jax.ShapeDtypeStruct((m, n), dt) for dt in outs],
        compiler_params=pltpu.CompilerParams(dimension_semantics=("parallel",), vmem_limit_bytes=_vmem(fixed + tm * per_row)),
    )(a, *bs, *rows)


def _rowmap(fn, rows, vecs, outs, reds, *, name, tm=None):
    norm = []
    for r in rows:
        if not isinstance(r, tuple):
            norm.append((r, r.shape[1], 0, None))
        elif len(r) == 2:
            norm.append((r[0], r[0].shape[2], 0, r[1]))
        else:
            norm.append((r[0], r[1], r[2], None))
    rows = norm
    t = rows[0][0].shape[-2]
    if tm is None:
        per_row = 2 * sum(w * a.dtype.itemsize for (a, w, _, _) in rows) + 2 * sum(w * jnp.dtype(d).itemsize for (w, d) in outs)
        per_row += 3 * 4 * max([w for (_, w, _, _) in rows] + [w for (w, _) in outs])
        tm = max(8, min(1024, (40 * 2**20) // per_row))
    tm = _pick(t, tuple(c for c in (1024, 512, 256, 128, 64, 32, 16, 8) if c <= tm))
    nr, nv, no = len(rows), len(vecs), len(outs)

    def body(*refs):
        row_refs, vec_refs = refs[:nr], refs[nr:nr + nv]
        out_refs, red_refs = refs[nr + nv:nr + nv + no], refs[nr + nv + no:]
        o, rd = fn([r[...].astype(F32) for r in row_refs], [v[...] for v in vec_refs])
        for ref, val in zip(out_refs, o):
            ref[...] = val.astype(ref.dtype)
        if red_refs:
            @pl.when(pl.program_id(0) == 0)
            def _():
                for ref in red_refs:
                    ref[...] = jnp.zeros_like(ref)

            for ref, val in zip(red_refs, rd):
                ref[...] += val

    in_specs = [pl.BlockSpec((tm, w), functools.partial(lambda cb, i: (i, cb), cb)) if lead is None
                else pl.BlockSpec((None, tm, w), functools.partial(lambda ld, i: (ld, i, 0), lead)) for (_, w, cb, lead) in rows]
    in_specs += [pl.BlockSpec(v.shape, functools.partial(lambda nd, i: (0,) * nd, v.ndim)) for v in vecs]
    out_specs = [pl.BlockSpec((tm, w), lambda i: (i, 0)) for (w, _) in outs]
    out_specs += [pl.BlockSpec((1, w), lambda i: (0, 0)) for w in reds]
    out_shape = [jax.ShapeDtypeStruct((t, w), d) for (w, d) in outs]
    out_shape += [jax.ShapeDtypeStruct((1, w), F32) for w in reds]
    est = 2 * sum(tm * w * a.dtype.itemsize for (a, w, _, _) in rows) + 2 * sum(_nbytes(v.shape, v.dtype) for v in vecs)
    est += 2 * sum(tm * w * jnp.dtype(d).itemsize for (w, d) in outs)
    est += 6 * tm * max([w for (_, w, _, _) in rows] + [w for (w, _) in outs]) * 4
    return pl.pallas_call(
        body,
        name=name,
        grid=(t // tm,),
        in_specs=in_specs,
        out_specs=out_specs,
        out_shape=out_shape,
        compiler_params=pltpu.CompilerParams(dimension_semantics=("arbitrary",), vmem_limit_bytes=_vmem(est)),
    )(*[r[0] for r in rows], *vecs)


def _colsum(v):
    return jnp.sum(v, axis=0, keepdims=True)


def _head_ones():
    i = np.arange(LANES)
    return jnp.asarray((i[:, None] // HEAD_DIM) == (i[None, :] // HEAD_DIM), MXU_DTYPE)


def _headsum(y, ones):
    parts = []
    for j in range(y.shape[1] // LANES):
        c = y[:, j * LANES:(j + 1) * LANES]
        hi = c.astype(MXU_DTYPE)
        lo = c - hi.astype(F32)
        parts.append(_dot(hi, ones) + _dot(lo, ones))
    return parts[0] if len(parts) == 1 else jnp.concatenate(parts, axis=1)


def _rot(y, c, s_lo, s_hi):
    parts = []
    for j in range(y.shape[1] // LANES):
        yc = y[:, j * LANES:(j + 1) * LANES]
        parts.append(yc * c + pltpu.roll(yc, LANES - ROT_DIM // 2, 1) * s_lo + pltpu.roll(yc, ROT_DIM // 2, 1) * s_hi)
    return parts[0] if len(parts) == 1 else jnp.concatenate(parts, axis=1)


def _rot_tables(t):
    half = ROT_DIM // 2
    inv_freq = ROPE_THETA ** (-(jnp.arange(half, dtype=F32) * 2.0) / ROT_DIM)
    ang = jnp.arange(t).astype(F32)[:, None] * inv_freq[None, :]
    cos, sin = jnp.cos(ang), jnp.sin(ang)
    z = lambda w: jnp.zeros((t, w), F32)
    c = jnp.concatenate([cos, cos, jnp.ones((t, HEAD_DIM - ROT_DIM), F32)], axis=1)
    s_lo = jnp.concatenate([-sin, z(HEAD_DIM - half)], axis=1)
    s_hi = jnp.concatenate([z(half), sin, z(HEAD_DIM - ROT_DIM)], axis=1)
    return [jnp.tile(a, (1, LANES // HEAD_DIM)) for a in (c, s_lo, s_hi)]


def _rms(x):
    return lax.rsqrt(jnp.mean(x * x, axis=-1, keepdims=True) + EPS)


def _window(kind, n, bq, t, seg):
    if kind == "na":
        rows = t // GRID_W
        rs = jnp.clip(n - NA_KH // 2, 0, rows - NA_KH)
        return rs
    nk = bq + 2 * DIL_HALF
    return jnp.clip(n * bq - DIL_HALF, 0, t - nk)


def _dil_mask(n, bq, nk, ws, seg):
    qi = n * bq + lax.broadcasted_iota(jnp.int32, (bq, nk), 0)
    ki = ws + lax.broadcasted_iota(jnp.int32, (bq, nk), 1)
    shift = int(np.log2(seg))
    return (jnp.abs(ki - qi) <= DIL_HALF) & ((ki >> shift) == (qi >> shift))


HS = 4
QW = HS * HEAD_DIM


def _head_of_lane(width=QW):
    return lax.broadcasted_iota(jnp.int32, (1, width), 1) // HEAD_DIM


def _stack_heads(a):
    head = _head_of_lane()
    return jnp.concatenate([jnp.where(head == e, a, jnp.zeros_like(a)) for e in range(HS)], axis=0)


def _unstack_heads(a, bq):
    head = _head_of_lane()
    out = jnp.zeros((bq, QW), a.dtype)
    for e in range(HS):
        out = jnp.where(head == e, a[e * bq:(e + 1) * bq], out)
    return out


def _stack_cols(blk, bq):
    head = _head_of_lane()
    return jnp.concatenate(
        [jnp.max(jnp.where(head == e, blk, -jnp.inf), axis=1, keepdims=True) for e in range(HS)], axis=0)


def _attn_geometry(kind):
    if kind == "na":
        return GRID_W, NA_KH * GRID_W, 4
    bq = 128
    return bq, bq + 2 * DIL_HALF, 2


def _attn_scores(kind, n, bq, nk, t, seg, qs, k_ref, b_ref):
    scale = HEAD_DIM ** -0.5
    if kind == "na":
        rs = _window(kind, n, bq, t, seg)
        ws = pl.multiple_of(rs * GRID_W, GRID_W)
        ro0 = rs - n + (NA_KH - 1)
        s = _dot_nt(qs, k_ref[pl.ds(ws, nk), :]) * scale
        s = s + jnp.concatenate(
            [jnp.concatenate([b_ref[e, ro0 + 2 * i] for i in range(NA_KH // 2)], axis=1) for e in range(HS)], axis=0)
        return s, ws, ro0
    ws = pl.multiple_of(_window(kind, n, bq, t, seg), DIL_HALF)
    mask = _dil_mask(n, bq, nk, ws, seg)
    s = _dot_nt(qs, k_ref[pl.ds(ws, nk), :]) * scale
    s = jnp.where(jnp.concatenate([mask] * HS, axis=0), s, NEG_INF)
    return s, ws, None


def _attn_fwd(q, k, v, *, kind, name, bias=None, seg=None):
    t, w = q.shape
    quads = w // QW
    bq, nk, sub = _attn_geometry(kind)
    nq = t // (bq * sub)

    def body(*refs):
        if kind == "na":
            q_ref, k_ref, v_ref, b_ref, o_ref, l_ref = refs
        else:
            (q_ref, k_ref, v_ref, o_ref, l_ref), b_ref = refs, None
        for i in range(sub):
            n = pl.program_id(1) * sub + i
            rows = slice(i * bq, (i + 1) * bq)
            s, ws, _ = _attn_scores(kind, n, bq, nk, t, seg, _stack_heads(q_ref[rows, :]), k_ref, b_ref)
            m = jnp.max(s, axis=1, keepdims=True)
            p = jnp.exp(s - m)
            l = jnp.sum(p, axis=1, keepdims=True)
            o_ref[rows, :] = _unstack_heads(_dot(p / l, v_ref[pl.ds(ws, nk), :]), bq)
            l_ref[rows, :] = _unstack_heads(jnp.broadcast_to(m + jnp.log(l), (HS * bq, QW)), bq)

    blk = pl.BlockSpec((bq * sub, QW), lambda j, n: (n, j))
    res = pl.BlockSpec((t, QW), lambda j, n: (0, j))
    in_specs = [blk, res, res]
    operands = [q, k, v]
    est = 4 * t * QW * q.dtype.itemsize + 12 * sub * HS * bq * nk * 4
    if kind == "na":
        in_specs.append(pl.BlockSpec((HS,) + bias.shape[1:], lambda j, n: (j, 0, 0, 0)))
        operands.append(bias)
        est += 2 * _nbytes((HS,) + bias.shape[1:], F32)
    return pl.pallas_call(
        body,
        name=name,
        grid=(quads, nq),
        in_specs=in_specs,
        out_specs=[blk, blk],
        out_shape=[jax.ShapeDtypeStruct((t, w), F32)] * 2,
        compiler_params=pltpu.CompilerParams(dimension_semantics=("arbitrary", "arbitrary"), vmem_limit_bytes=_vmem(est)),
    )(*operands)


def _attn_bwd(q, k, v, do, dterm, lse, *, kind, name, bias=None, seg=None):
    t, w = q.shape
    quads = w // QW
    bq, nk, sub = _attn_geometry(kind)
    nq = t // (bq * sub)
    scale = HEAD_DIM ** -0.5

    def body(*refs):
        if kind == "na":
            q_ref, k_ref, v_ref, do_ref, dt_ref, l_ref, b_ref, dq_ref, dk_hbm, dv_hbm, db_ref, dk_acc, dv_acc, sem = refs
        else:
            q_ref, k_ref, v_ref, do_ref, dt_ref, l_ref, dq_ref, dk_hbm, dv_hbm, dk_acc, dv_acc, sem = refs
            b_ref = None
        j, step = pl.program_id(0), pl.program_id(1)

        @pl.when(step == 0)
        def _():
            dk_acc[...] = jnp.zeros_like(dk_acc)
            dv_acc[...] = jnp.zeros_like(dv_acc)
            if kind == "na":
                db_ref[...] = jnp.zeros_like(db_ref)

        for b in range(sub):
            n = step * sub + b
            rows = slice(b * bq, (b + 1) * bq)
            qs = _stack_heads(q_ref[rows, :])
            dos = _stack_heads(do_ref[rows, :])
            s, ws, ro0 = _attn_scores(kind, n, bq, nk, t, seg, qs, k_ref, b_ref)
            p = jnp.exp(s - _stack_cols(l_ref[rows, :], bq))
            dp = _dot_nt(dos, v_ref[pl.ds(ws, nk), :])
            ds = p * (dp - _stack_cols(dt_ref[rows, :], bq))
            if kind == "na":
                for e in range(HS):
                    for i in range(NA_KH // 2):
                        db_ref[e, ro0 + 2 * i] += ds[e * bq:(e + 1) * bq, i * LANES:(i + 1) * LANES]
            dsc = ds * scale
            dq_ref[rows, :] = _unstack_heads(_dot(dsc, k_ref[pl.ds(ws, nk), :]), bq)
            dk_acc[pl.ds(ws, nk), :] += _dot_tn(dsc, qs)
            dv_acc[pl.ds(ws, nk), :] += _dot_tn(p, dos)

        @pl.when(step == nq - 1)
        def _():
            ck = pltpu.make_async_copy(dk_acc, dk_hbm.at[j], sem.at[0])
            cv = pltpu.make_async_copy(dv_acc, dv_hbm.at[j], sem.at[1])
            ck.start()
            cv.start()
            ck.wait()
            cv.wait()

    blk = pl.BlockSpec((bq * sub, QW), lambda j, n: (n, j))
    res = pl.BlockSpec((t, QW), lambda j, n: (0, j))
    in_specs = [blk, res, res, blk, blk, blk]
    operands = [q, k, v, do, dterm, lse]
    out_specs = [blk, ANY, ANY]
    out_shape = [jax.ShapeDtypeStruct((t, w), F32)] + [jax.ShapeDtypeStruct((quads, t, QW), F32)] * 2
    est = 4 * t * QW * q.dtype.itemsize + 2 * t * QW * 4 + 16 * sub * HS * bq * nk * 4
    if kind == "na":
        bspec = pl.BlockSpec((HS,) + bias.shape[1:], lambda j, n: (j, 0, 0, 0))
        in_specs.append(bspec)
        operands.append(bias)
        out_specs.append(bspec)
        out_shape.append(jax.ShapeDtypeStruct(bias.shape, F32))
        est += 4 * _nbytes((HS,) + bias.shape[1:], F32)
    res_ = pl.pallas_call(
        body,
        name=name,
        grid=(quads, nq),
        in_specs=in_specs,
        out_specs=out_specs,
        out_shape=out_shape,
        scratch_shapes=[pltpu.VMEM((t, QW), F32), pltpu.VMEM((t, QW), F32), pltpu.SemaphoreType.DMA((2,))],
        compiler_params=pltpu.CompilerParams(dimension_semantics=("arbitrary", "arbitrary"), vmem_limit_bytes=_vmem(est)),
    )(*operands)
    unquad = lambda a: [(a, i) for i in range(quads)]
    return (res_[0], unquad(res_[1]), unquad(res_[2])) + tuple(res_[3:])


def _na_onehot():
    qc = np.arange(GRID_W)[:, None]
    kc = np.arange(GRID_W)[None, :]
    start = np.clip(qc - NA_KW // 2, 0, GRID_W - NA_KW)
    inwin = (kc >= start) & (kc < start + NA_KW)
    off = kc - qc + (NA_KW - 1)
    e_mat = np.zeros((2, 32, GRID_W, 2, GRID_W), np.float32)
    for e in range(2):
        for c in range(2 * NA_KW - 1):
            e_mat[e, c, :, e, :] = (off == c) & inwin
    neg = np.where(inwin, 0.0, NEG_INF).astype(np.float32)
    neg = np.broadcast_to(neg[:, None, :], (GRID_W, 2, GRID_W)).reshape(1, GRID_W * LANES)
    return jnp.asarray(e_mat.reshape(64, GRID_W * LANES), MXU_DTYPE), jnp.asarray(neg)


def _na_rowpairs(rpb):
    p = jnp.pad(rpb, ((0, 0), (0, 0), (0, 1)))
    return jnp.concatenate([p[:, :-1], p[:, 1:]], axis=-1).reshape(NA_HEADS * (2 * NA_KH - 2), 64)


def _na_bias_table(rpb):
    r2 = _na_rowpairs(rpb)
    e_mat, neg = _na_onehot()

    def body(r_ref, e_ref, n_ref, o_ref):
        hi, mid, lo = _split3(r_ref[...])
        e = e_ref[...]
        o_ref[...] = _dot(hi, e) + _dot(mid, e) + _dot(lo, e) + n_ref[...]

    out = pl.pallas_call(
        body,
        name="na_bias_table",
        out_shape=jax.ShapeDtypeStruct((r2.shape[0], GRID_W * LANES), F32),
        compiler_params=pltpu.CompilerParams(vmem_limit_bytes=_vmem(6 * r2.shape[0] * GRID_W * LANES * 4)),
    )(r2, e_mat, neg)
    return out.reshape(NA_HEADS, 2 * NA_KH - 2, GRID_W, LANES)


def _na_bias_grad(dbt):
    e_mat, _ = _na_onehot()
    flat = dbt.reshape(NA_HEADS * (2 * NA_KH - 2), GRID_W * LANES)

    def body(d_ref, e_ref, o_ref):
        hi, mid, lo = _split3(d_ref[...])
        e = e_ref[...]
        o_ref[...] = _dot_nt(hi, e) + _dot_nt(mid, e) + _dot_nt(lo, e)

    g = pl.pallas_call(
        body,
        name="na_bias_grad",
        out_shape=jax.ShapeDtypeStruct((flat.shape[0], 64), F32),
        compiler_params=pltpu.CompilerParams(vmem_limit_bytes=_vmem(6 * flat.shape[0] * GRID_W * LANES * 4)),
    )(flat, e_mat)
    g = g.reshape(NA_HEADS, 2 * NA_KH - 2, 2, 32)[..., :2 * NA_KW - 1]
    first = jnp.pad(g[:, :, 0], ((0, 0), (0, 1), (0, 0)))
    second = jnp.pad(g[:, :, 1], ((0, 0), (1, 0), (0, 0)))
    return first + second


def _all_gather(arrs, *, name):
    na = len(arrs)

    def body(*refs):
        ins, outs = refs[:na], refs[na:2 * na]
        send_sems, recv_sems, local_sems = refs[2 * na:]
        x, y, c = lax.axis_index("x"), lax.axis_index("y"), lax.axis_index("c")
        me, sibling = (x, y, c), (x, y, 1 - c)
        chips = [(1 - x, y), (x, 1 - y), (1 - x, 1 - y)]

        def rows(a, px, py, pc):
            r = ins[a].shape[0]
            return outs[a].at[pl.ds((4 * px + 2 * py + pc) * r, r), :]

        def copy(a, k, block, to, src=None):
            return pltpu.make_async_remote_copy(
                src_ref=rows(a, *block) if src is None else src, dst_ref=rows(a, *block),
                send_sem=send_sems.at[a, k], recv_sem=recv_sems.at[a, k], device_id=to, device_id_type=MESH)

        mine = [pltpu.make_async_copy(ins[a], rows(a, *me), local_sems.at[a]) for a in range(na)]
        for cp in mine:
            cp.start()
        first = []
        for a in range(na):
            first.append(copy(a, 0, me, sibling, src=ins[a]))
            first += [copy(a, 1 + j, me, (*chip, c), src=ins[a]) for j, chip in enumerate(chips)]
        for cp in first:
            cp.start()
        passed = []
        for j, chip in enumerate(chips):
            for a in range(na):
                copy(a, 1 + j, (*chip, c), me).wait_recv()
                cp = copy(a, 4 + j, (*chip, c), sibling)
                cp.start()
                passed.append(cp)
        for a in range(na):
            copy(a, 0, sibling, me).wait_recv()
        for j, chip in enumerate(chips):
            for a in range(na):
                copy(a, 4 + j, (*chip, 1 - c), me).wait_recv()
        for cp in first + passed:
            cp.wait_send()
        for cp in mine:
            cp.wait()

    return pl.pallas_call(
        body,
        name=name,
        in_specs=[ANY] * na,
        out_specs=[ANY] * na,
        out_shape=[jax.ShapeDtypeStruct((N_DEV * a.shape[0], a.shape[1]), a.dtype) for a in arrs],
        scratch_shapes=[pltpu.SemaphoreType.DMA((na, 7)), pltpu.SemaphoreType.DMA((na, 7)), pltpu.SemaphoreType.DMA((na,))],
    )(*arrs)


HBM = pl.BlockSpec(memory_space=pltpu.HBM)
SEM = pl.BlockSpec(memory_space=pltpu.SEMAPHORE)
EFFECT = pltpu.SideEffectType.DATAFLOW_SIDE_EFFECTING


def _peer_of(k):
    x, y, c = lax.axis_index("x"), lax.axis_index("y"), lax.axis_index("c")
    return x ^ ((k >> 2) & 1), y ^ ((k >> 1) & 1), c ^ (k & 1)


def _split_copies(gather, src_ref, land_ref, send_sems, recv_sems):
    x, y, c = lax.axis_index("x"), lax.axis_index("y"), lax.axis_index("c")
    my = 4 * x + 2 * y + c
    r = src_ref.shape[0] if gather else src_ref.shape[0] // N_DEV
    copies = []
    for k in range(1, N_DEV):
        px, py, pc = _peer_of(k)
        if gather:
            src, dst = src_ref, land_ref.at[pl.ds(my * r, r), :]
        else:
            src, dst = src_ref.at[pl.ds((4 * px + 2 * py + pc) * r, r), :], land_ref.at[k - 1]
        copies.append(pltpu.make_async_remote_copy(
            src_ref=src, dst_ref=dst, send_sem=send_sems.at[k - 1], recv_sem=recv_sems.at[k - 1],
            device_id=(px, py, pc), device_id_type=MESH))
    return copies


def _split_start(srcs, lands, *, gather, name, after=None):
    na = len(srcs)
    extra = [] if after is None else [after]

    def body(*refs):
        src_refs, land_refs = refs[:na], refs[na:2 * na]
        outs = refs[2 * na + len(extra):]
        for a in range(na):
            for cp in _split_copies(gather, src_refs[a], land_refs[a], outs[4 * a], outs[4 * a + 1]):
                cp.start()
        outs[4 * na][...] = jnp.zeros_like(outs[4 * na])

    out_shape, out_specs, aliases = [], [], {}
    for a in range(na):
        out_shape += [pltpu.SemaphoreType.DMA((N_DEV - 1,)), pltpu.SemaphoreType.DMA((N_DEV - 1,)),
                      pltpu.HBM(srcs[a].shape, srcs[a].dtype), pltpu.HBM(lands[a].shape, lands[a].dtype)]
        out_specs += [SEM, SEM, HBM, HBM]
        aliases[a] = 4 * a + 2
        aliases[na + a] = 4 * a + 3
    out_shape.append(jax.ShapeDtypeStruct((8, LANES), F32))
    out_specs.append(pl.BlockSpec(memory_space=pltpu.VMEM))
    res = pl.pallas_call(
        body,
        name=name,
        out_shape=tuple(out_shape),
        in_specs=[HBM] * (2 * na) + [ANY] * len(extra),
        out_specs=tuple(out_specs),
        input_output_aliases=aliases,
        compiler_params=pltpu.CompilerParams(has_side_effects=EFFECT),
    )(*[pltpu.with_memory_space_constraint(a, pltpu.HBM) for a in list(srcs) + list(lands)], *extra)
    return [tuple(res[4 * a:4 * a + 4]) for a in range(na)], res[4 * na][0, 0]


def _split_wait(handles, after, *, gather, name):
    na = len(handles)

    def body(*refs):
        src_refs, land_refs = refs[:na], refs[na:2 * na]
        sems = refs[2 * na:4 * na]
        for a in range(na):
            for cp in _split_copies(gather, src_refs[a], land_refs[a], sems[2 * a], sems[2 * a + 1]):
                cp.wait_send()
                cp.wait_recv()

    srcs = [h[2] for h in handles]
    lands = [h[3] for h in handles]
    sems = [s for h in handles for s in h[:2]]
    res = pl.pallas_call(
        body,
        name=name,
        out_shape=tuple(pltpu.HBM(a.shape, a.dtype) for a in srcs + lands),
        in_specs=[HBM] * (2 * na) + [SEM] * (2 * na) + [ANY],
        out_specs=tuple([HBM] * (2 * na)),
        input_output_aliases={i: i for i in range(2 * na)},
        compiler_params=pltpu.CompilerParams(has_side_effects=EFFECT),
    )(*srcs, *lands, *sems, after)
    return list(res[:na]), list(res[na:])


def _sum8(own, recv, *, name):
    _, r, w = recv.shape
    tr = _pick(r, (256, 128, 64, 32, 16, 8))

    def body(own_ref, a_ref, o_ref):
        acc = own_ref[...].astype(F32)
        for i in range(N_DEV - 1):
            acc = acc + a_ref[i].astype(F32)
        o_ref[...] = acc

    return pl.pallas_call(
        body,
        name=name,
        grid=(r // tr,),
        in_specs=[pl.BlockSpec((tr, w), lambda i: (i, 0)), pl.BlockSpec((N_DEV - 1, tr, w), lambda i: (0, i, 0))],
        out_specs=pl.BlockSpec((tr, w), lambda i: (i, 0)),
        out_shape=jax.ShapeDtypeStruct((r, w), F32),
        compiler_params=pltpu.CompilerParams(dimension_semantics=("parallel",), vmem_limit_bytes=_vmem(4 * N_DEV * tr * w * 4)),
    )(own, recv)


def _adamw(w, g, m, v, *, name):
    def fn(rows, _):
        wv, gv, mv, vv = rows
        m1 = ADAM_B1 * mv + (1.0 - ADAM_B1) * gv
        v1 = ADAM_B2 * vv + (1.0 - ADAM_B2) * jnp.square(gv)
        m_hat = m1 / (1.0 - ADAM_B1 ** ADAM_STEP)
        v_hat = v1 / (1.0 - ADAM_B2 ** ADAM_STEP)
        delta = -ADAM_LR * (m_hat / (jnp.sqrt(v_hat) + ADAM_EPS) + ADAM_WD * wv)
        return [delta, m1, v1], []

    c = w.shape[1]
    return _rowmap(fn, [w, g, m, v], [], [(c, F32)] * 3, [], name=name)


_SMALL = ("b_ada", "g_norm1", "g_norm2", "b_gate", "g_qa", "g_ka", "g_qb", "g_kb", "rpb", "loss")


def _pack_small(parts):
    flat = []
    for nme in _SMALL:
        a = parts[nme].reshape(-1).astype(F32)
        flat.append(jnp.pad(a, (0, (-a.shape[0]) % LANES)))
    flat = jnp.concatenate(flat)
    flat = jnp.pad(flat, (0, (-flat.shape[0]) % (8 * LANES)))
    return flat.reshape(-1, LANES)


def _unpack_small(packed, shapes):
    flat = packed.reshape(-1)
    out, pos = {}, 0
    for nme in _SMALL:
        n = int(np.prod(shapes[nme]))
        out[nme] = flat[pos:pos + n].reshape(shapes[nme])
        pos += n + (-n) % LANES
    return out


def _to_class(a, d):
    t, w = a.shape
    return a if d == 1 else a.reshape(t // d, d, w).transpose(1, 0, 2).reshape(t, w)


def _from_class(a, d):
    t, w = a.shape
    return a if d == 1 else a.reshape(d, t // d, w).transpose(1, 0, 2).reshape(t, w)


def kernel(x, c, w_ada, b_ada, g_norm1, g_norm2, w_in, b_gate, g_qa, g_ka, g_qb, g_kb, rpb, w_proj_a, w_proj_b, w_o, w_ffn_in, w_ffn_out, loss_target, m_w_ada, m_b_ada, m_g_norm1, m_g_norm2, m_w_in, m_b_gate, m_g_qa, m_g_ka, m_g_qb, m_g_kb, m_rpb, m_w_proj_a, m_w_proj_b, m_w_o, m_w_ffn_in, m_w_ffn_out, v_w_ada, v_b_ada, v_g_norm1, v_g_norm2, v_w_in, v_b_gate, v_g_qa, v_g_ka, v_g_qb, v_g_kb, v_rpb, v_w_proj_a, v_w_proj_b, v_w_o, v_w_ffn_in, v_w_ffn_out):
    t, d = x.shape[1], x.shape[2]
    d_ff = w_ffn_out.shape[1] * N_DEV
    me = 4 * lax.axis_index("x") + 2 * lax.axis_index("y") + lax.axis_index("c")
    xt, tgt = x.reshape(t, d), loss_target.reshape(t, d)
    ones = _head_ones()

    shards = [s.astype(WIRE_DTYPE) for s in (w_in[0].T, w_ffn_in[0].T, w_proj_a[0].T, w_proj_b[0].T, w_o[0], w_ffn_out[0])]
    lands = [lax.dynamic_update_slice(lax.empty((N_DEV * s.shape[0], s.shape[1]), s.dtype), s, (me * s.shape[0], 0))
             for s in shards]

    c_all = _all_gather([jnp.pad(c, ((0, 7), (0, 0)))], name="gather_c")[0][::8]
    c_all = jnp.pad(c_all, ((0, LANES - N_DEV), (0, 0)))

    def mod_body(c_ref, w_ref, b_ref, o_ref, act_ref):
        act = _silu(c_ref[...])
        act_ref[...] = act
        hi, mid, lo = _split3(act)
        w = w_ref[...]
        whi, wmid, wlo = _split3(w)
        acc = _dot(hi, whi) + (_dot(hi, wmid) + _dot(mid, whi)) + (_dot(hi, wlo) + _dot(mid, wmid) + _dot(lo, whi))
        o_ref[...] = acc + b_ref[...]

    ncol = w_ada.shape[2]
    b_ada_mine = lax.dynamic_slice(b_ada, (0, me * ncol), (1, ncol))
    mod_part, c_act = pl.pallas_call(
        mod_body,
        name="ada_mod",
        out_shape=[jax.ShapeDtypeStruct((LANES, ncol), F32), jax.ShapeDtypeStruct((LANES, d), F32)],
        compiler_params=pltpu.CompilerParams(vmem_limit_bytes=_vmem(6 * d * ncol * 4)),
    )(c_all, w_ada[0], b_ada_mine)
    mod_all = _all_gather([mod_part[:N_DEV]], name="gather_mod")[0].reshape(N_DEV, N_DEV, ncol)
    mod = lax.dynamic_index_in_dim(mod_all, me, axis=1, keepdims=False).reshape(6, d)
    sh1, sc1, gt1, sh2, sc2, gt2 = [mod[i:i + 1] for i in range(6)]

    def norm_fwd(rows, vecs):
        (xv,), (g, sc, sh) = rows, vecs
        return [xv * _rms(xv) * g * (1.0 + sc) + sh], []

    w_handles, w_token = _split_start(shards, lands, gather=True, after=mod, name="gather_weights_start")
    (h,) = _rowmap(norm_fwd, [xt], [g_norm1 + w_token, sc1, sh1], [(d, MXU_DTYPE)], [], name="norm1")
    n_a, n_b = 3 * WA, 3 * WB
    (w_in_t,) = _split_wait(w_handles[:1], h, gather=True, name="gather_w_in_wait")[1]
    w_in_a, w_in_b, w_in_g = w_in_t[:n_a], w_in_t[n_a:n_a + n_b], w_in_t[n_a + n_b:]
    qkv_a = _mm(h, w_in_a, tb=True, out_dtype=ACT_DTYPE, name="proj_a")
    qkv_b = _mm(h, w_in_b, tb=True, out_dtype=ACT_DTYPE, name="proj_b")
    gates = _mm(h, w_in_g, tb=True, out_dtype=ACT_DTYPE, name="proj_gates")

    rot_c, rot_lo, rot_hi = _rot_tables(t)
    tile_g = lambda g, heads: jnp.tile(g, (1, heads))

    def qk_fwd(width, rotate):
        def fn(rows, vecs):
            xv = rows[0]
            gq, gk, on = vecs
            outs = []
            for i, g in enumerate((gq, gk)):
                xi = xv[:, i * width:(i + 1) * width]
                r = lax.rsqrt(_headsum(xi * xi, on) * (1.0 / HEAD_DIM) + EPS)
                yi = xi * r * g
                if rotate:
                    yi = _rot(yi, rows[1], rows[2], rows[3])
                outs.append(yi)
            outs.append(xv[:, 2 * width:])
            return outs, []
        return fn

    qa, ka, va = _rowmap(qk_fwd(WA, False), [qkv_a], [tile_g(g_qa, NA_HEADS), tile_g(g_ka, NA_HEADS), ones],
                         [(WA, MXU_DTYPE)] * 3, [], name="qknorm_a")
    qb, kb, vb = _rowmap(qk_fwd(WB, True), [qkv_b, rot_c, rot_lo, rot_hi],
                         [tile_g(g_qb, DIL_HEADS), tile_g(g_kb, DIL_HEADS), ones], [(WB, MXU_DTYPE)] * 3, [], name="qknorm_b")

    bias_tab = _na_bias_table(rpb[0])
    o_a, lse_a = _attn_fwd(qa, ka, va, kind="na", bias=bias_tab, name="na_fwd")

    grp = []
    for g, (_, dil) in enumerate(DIL_CONFIGS):
        sl = slice(g * WB_OUT, (g + 1) * WB_OUT)
        qg, kg, vg = [_to_class(a[:, sl], dil) for a in (qb, kb, vb)]
        og, lg = _attn_fwd(qg, kg, vg, kind="dil", seg=t // dil, name=f"dil_fwd{g}")
        grp.append(dict(q=qg, k=kg, v=vg, o=_from_class(og, dil), lse=_from_class(lg, dil), lse_c=lg, dil=dil))

    def merge_fwd(rows, _):
        o0, o1, o2, l0, l1, l2 = rows
        mx = jnp.maximum(jnp.maximum(l0, l1), l2)
        e0, e1, e2 = jnp.exp(l0 - mx), jnp.exp(l1 - mx), jnp.exp(l2 - mx)
        s = e0 + e1 + e2
        return [(e0 / s) * o0 + (e1 / s) * o1 + (e2 / s) * o2], []

    (o_b,) = _rowmap(merge_fwd, [gr["o"] for gr in grp] + [gr["lse"] for gr in grp], [], [(WB_OUT, F32)], [], name="dil_merge")

    w_pa_t, w_pb_t, w_o_f = _split_wait(w_handles[2:5], o_b, gather=True, name="gather_w_out_wait")[1]
    pa = _mm(o_a, w_pa_t, tb=True, out_dtype=ACT_DTYPE, name="proj_out_a")
    pb = _mm(o_b, w_pb_t, tb=True, out_dtype=ACT_DTYPE, name="proj_out_b")

    def gate_fwd(rows, vecs):
        gv, pav, pbv = rows
        sg = jax.nn.sigmoid(gv + vecs[0])
        return [sg[:, :d] * pav + sg[:, d:] * pbv], []

    (merged,) = _rowmap(gate_fwd, [gates, pa, pb], [b_gate], [(d, MXU_DTYPE)], [], name="gate_merge")
    def resid_norm(av, rows, vecs):
        (xv,), (gt, g, sc, sh) = rows, vecs
        x1v = xv + gt * av
        return [av, x1v, x1v * _rms(x1v) * g * (1.0 + sc) + sh], []

    att, x1, h2 = _mm_parts_rows([(merged, w_o_f)], resid_norm, [xt], [gt1, g_norm2, sc2, sh2],
                           [(d, F32), (d, F32), (d, MXU_DTYPE)], [], name="proj_o_resid_norm2")

    w_ffn_in_t, w_ffn_out_f = _split_wait([w_handles[1], w_handles[5]], h2, gather=True, name="gather_w_ffn_wait")[1]
    w_ffn_a, w_ffn_up = w_ffn_in_t[:d_ff], w_ffn_in_t[d_ff:]

    def swiglu_fwd(prods, _):
        a, up = prods
        return [a, up, _silu(a) * up]

    ua, uu, f = _mm_ew(h2, [w_ffn_a, w_ffn_up], swiglu_fwd, [], [ACT_DTYPE, ACT_DTYPE, MXU_DTYPE], name="ffn_in_swiglu")

    def loss_fn(yv, rows, vecs):
        (x1v, tv), gt = rows, vecs[0]
        err = x1v + gt * yv - tv
        dout = err * (1.0 / d)
        return [dout, dout * gt], [_colsum(err * err), _colsum(dout * yv)]

    dout, dy2, err2, dgt2 = _mm_parts_rows([(f, w_ffn_out_f)], loss_fn, [x1, tgt], [gt2], [(d, F32), (d, MXU_DTYPE)],
                                           [d, d], name="ffn_out_loss")

    dw_ffn_out = _mm(f, dy2, ta=True, out_dtype=WIRE_DTYPE, name="wgrad_ffn_out")
    def swiglu_bwd(prods, rows):
        (dfv,), (a, up) = prods, rows
        sg = jax.nn.sigmoid(a)
        return [dfv * up * (sg * (1.0 + a * (1.0 - sg))), dfv * (a * sg)]

    da, dup = _mm_ew(dy2, [w_ffn_out_f], swiglu_bwd, [ua, uu], [MXU_DTYPE, MXU_DTYPE], name="dgrad_ffn_out_swiglu_bwd")
    dw_ffn_in_t = jnp.concatenate([_mm(da, h2, ta=True, out_dtype=WIRE_DTYPE, name="wgrad_ffn_in_a"),
                                   _mm(dup, h2, ta=True, out_dtype=WIRE_DTYPE, name="wgrad_ffn_in_up")], axis=0)
    land7 = lambda a: lax.empty((N_DEV - 1, a.shape[0] // N_DEV, a.shape[1]), a.dtype)
    own_block = lambda a: lax.dynamic_slice(a, (me * (a.shape[0] // N_DEV), 0), (a.shape[0] // N_DEV, a.shape[1]))
    g_ffn = [dw_ffn_in_t, dw_ffn_out]
    h_ffn, tok_ffn = _split_start(g_ffn, [land7(a) for a in g_ffn], gather=False, name="exchange_ffn_start")
    def norm_bwd(dh, xv, g, sc):
        r = _rms(xv)
        xh = xv * r
        dxh = dh * g * (1.0 + sc)
        dxv = r * (dxh - xh * jnp.mean(dxh * xh, axis=-1, keepdims=True))
        return dxv, [_colsum(dh), _colsum(dh * xh * g), _colsum(dh * xh * (1.0 + sc))]

    def norm2_bwd(dhv, rows, vecs):
        (x1v, dov, av), (g, sc, gt) = rows, vecs
        dxv, sums = norm_bwd(dhv, x1v, g, sc)
        dx1v = dov + dxv
        return [dx1v, dx1v * gt], sums + [_colsum(dx1v * av)]

    dx1, datt, dsh2, dsc2, dg2, dgt1 = _mm_parts_rows(
        [(da, w_ffn_a), (dup, w_ffn_up)], norm2_bwd, [x1, dout, att], [g_norm2 + tok_ffn, sc2, gt1],
        [(d, F32), (d, MXU_DTYPE)], [d] * 4, name="dgrad_ffn_in_norm2_bwd")
    dw_o = _mm(merged, datt, ta=True, out_dtype=WIRE_DTYPE, name="wgrad_o")
    dmerged = _mm(datt, w_o_f, tb=True, out_dtype=ACT_DTYPE, name="dgrad_o")

    def gate_bwd(rows, vecs):
        dm, gv, pav, pbv = rows
        sg = jax.nn.sigmoid(gv + vecs[0])
        ga, gb = sg[:, :d], sg[:, d:]
        dgp = jnp.concatenate([dm * pav * ga * (1.0 - ga), dm * pbv * gb * (1.0 - gb)], axis=1)
        return [dm * ga, dm * gb, dgp], [_colsum(dgp)]

    dpa, dpb, dgates, db_gate = _rowmap(gate_bwd, [dmerged, gates, pa, pb], [b_gate],
                                        [(d, MXU_DTYPE), (d, MXU_DTYPE), (2 * d, MXU_DTYPE)], [2 * d], name="gate_bwd")
    dw_pa_t = _mm(dpa, o_a, ta=True, out_dtype=WIRE_DTYPE, name="wgrad_proj_a")
    dw_pb_t = _mm(dpb, o_b, ta=True, out_dtype=WIRE_DTYPE, name="wgrad_proj_b")
    g_out = [dw_pa_t, dw_pb_t, dw_o]
    h_out, tok_out = _split_start(g_out, [land7(a) for a in g_out], gather=False, name="exchange_out_start")
    do_a = _mm(dpa, w_pa_t, name="dgrad_proj_a")
    do_b = _mm(dpb, w_pb_t, name="dgrad_proj_b")

    def delta_a(rows, vecs):
        return [_headsum(rows[0] * rows[1], vecs[0])], []

    (dterm_a,) = _rowmap(delta_a, [do_a, o_a], [ones + tok_out.astype(ones.dtype)], [(WA, F32)], [], name="na_delta")
    dqa, dka, dva, dbias = _attn_bwd(qa, ka, va, do_a, dterm_a, lse_a, kind="na", bias=bias_tab, name="na_bwd")
    g_rpb = _na_bias_grad(dbias)

    def merge_bwd(rows, vecs):
        dob, o0, o1, o2, l0, l1, l2 = rows
        on = vecs[0]
        mx = jnp.maximum(jnp.maximum(l0, l1), l2)
        e0, e1, e2 = jnp.exp(l0 - mx), jnp.exp(l1 - mx), jnp.exp(l2 - mx)
        s = e0 + e1 + e2
        ws = [e0 / s, e1 / s, e2 / s]
        dws = [_headsum(dob * o, on) for o in (o0, o1, o2)]
        mean = ws[0] * dws[0] + ws[1] * dws[1] + ws[2] * dws[2]
        return [w * dob for w in ws] + [w * mean for w in ws], []

    mb = _rowmap(merge_bwd, [do_b] + [gr["o"] for gr in grp] + [gr["lse"] for gr in grp], [ones],
                 [(WB_OUT, F32)] * 6, [], name="dil_merge_bwd")
    dqb, dkb, dvb = [], [], []
    for g, gr in enumerate(grp):
        dil = gr["dil"]
        dq, dk, dv = _attn_bwd(gr["q"], gr["k"], gr["v"], _to_class(mb[g], dil), _to_class(mb[3 + g], dil), gr["lse_c"],
                               kind="dil", seg=t // dil, name=f"dil_bwd{g}")
        dqb.append(_from_class(dq, dil))
        dkb.append(_from_class(dk[0][0][0], dil))
        dvb.append(_from_class(dv[0][0][0], dil))

    def qk_bwd(width, rotate, nparts):
        def fn(rows, vecs):
            gq, gk, on = vecs
            xv = rows[0]
            pos = 1
            if rotate:
                rc, rlo, rhi = rows[1:4]
                pos = 4
            cat = lambda parts: parts[0] if len(parts) == 1 else jnp.concatenate(parts, axis=1)
            ends = np.cumsum((pos,) + nparts)
            dq, dk, dv = [cat(rows[ends[i]:ends[i + 1]]) for i in range(3)]
            outs, sums = [], []
            for i, (dy, g) in enumerate(((dq, gq), (dk, gk))):
                if rotate:
                    dy = _rot(dy, rc, -rlo, -rhi)
                xi = xv[:, i * width:(i + 1) * width]
                r = lax.rsqrt(_headsum(xi * xi, on) * (1.0 / HEAD_DIM) + EPS)
                xh = xi * r
                dxh = dy * g
                outs.append(r * (dxh - xh * (_headsum(dxh * xh, on) * (1.0 / HEAD_DIM))))
                sums.append(_colsum(dy * xh))
            return [jnp.concatenate(outs + [dv], axis=1)], sums
        return fn

    dqkv_a, dg_qa, dg_ka = _rowmap(qk_bwd(WA, False, (1, len(dka), len(dva))), [qkv_a, dqa] + dka + dva,
                                   [tile_g(g_qa, NA_HEADS), tile_g(g_ka, NA_HEADS), ones],
                                   [(3 * WA, MXU_DTYPE)], [WA, WA], name="qknorm_a_bwd")
    dqkv_b, dg_qb, dg_kb = _rowmap(qk_bwd(WB, True, (3, 3, 3)), [qkv_b, rot_c, rot_lo, rot_hi] + dqb + dkb + dvb,
                                   [tile_g(g_qb, DIL_HEADS), tile_g(g_kb, DIL_HEADS), ones],
                                   [(3 * WB, MXU_DTYPE)], [WB, WB], name="qknorm_b_bwd")

    dw_in_t = jnp.concatenate([
        _mm(dqkv_a, h, ta=True, out_dtype=WIRE_DTYPE, name="wgrad_in_a"),
        _mm(dqkv_b, h, ta=True, out_dtype=WIRE_DTYPE, name="wgrad_in_b"),
        _mm(dgates, h, ta=True, out_dtype=WIRE_DTYPE, name="wgrad_in_gates")], axis=0)
    h_in, tok_in = _split_start([dw_in_t], [land7(dw_in_t)], gather=False, name="exchange_in_start")
    def norm1_bwd(dhv, rows, vecs):
        xv, dx1v = rows
        dxv, sums = norm_bwd(dhv, xv, vecs[0], vecs[1])
        return [dx1v + dxv], sums

    grad_x, dsh1, dsc1, dg1 = _mm_parts_rows(
        [(dqkv_a, w_in_a), (dqkv_b, w_in_b), (dgates, w_in_g)], norm1_bwd, [xt, dx1], [g_norm1 + tok_in, sc1],
        [(d, F32)], [d] * 3, name="dgrad_in_norm1_bwd")

    heads_sum = lambda a, heads: a.reshape(heads, HEAD_DIM).sum(axis=0)
    dmod = jnp.concatenate([dsh1, dsc1, dgt1, dsh2, dsc2, dgt2], axis=1)
    local_small = _pack_small(dict(
        b_ada=dmod, g_norm1=dg1, g_norm2=dg2, b_gate=db_gate, g_qa=heads_sum(dg_qa, NA_HEADS),
        g_ka=heads_sum(dg_ka, NA_HEADS), g_qb=heads_sum(dg_qb, DIL_HEADS), g_kb=heads_sum(dg_kb, DIL_HEADS),
        rpb=g_rpb, loss=(0.5 / d) * jnp.sum(err2)))
    srows = local_small.shape[0]
    small_all = _all_gather([local_small], name="gather_small")[0].reshape(N_DEV, srows, LANES)
    small_sum = _sum8(small_all[0], small_all[1:], name="sum_small")
    small_shapes = dict(b_ada=b_ada.shape, g_norm1=g_norm1.shape, g_norm2=g_norm2.shape, b_gate=b_gate.shape,
                        g_qa=g_qa.shape, g_ka=g_ka.shape, g_qb=g_qb.shape, g_kb=g_kb.shape, rpb=rpb.shape, loss=())
    small_w = dict(b_ada=b_ada, g_norm1=g_norm1, g_norm2=g_norm2, b_gate=b_gate, g_qa=g_qa, g_ka=g_ka, g_qb=g_qb,
                   g_kb=g_kb, rpb=rpb, loss=jnp.zeros((), F32))
    small_m = dict(b_ada=m_b_ada, g_norm1=m_g_norm1, g_norm2=m_g_norm2, b_gate=m_b_gate, g_qa=m_g_qa, g_ka=m_g_ka,
                   g_qb=m_g_qb, g_kb=m_g_kb, rpb=m_rpb, loss=jnp.zeros((), F32))
    small_v = dict(b_ada=v_b_ada, g_norm1=v_g_norm1, g_norm2=v_g_norm2, b_gate=v_b_gate, g_qa=v_g_qa, g_ka=v_g_ka,
                   g_qb=v_g_qb, g_kb=v_g_kb, rpb=v_rpb, loss=jnp.zeros((), F32))
    s_delta, s_m, s_v = _adamw(_pack_small(small_w), small_sum, _pack_small(small_m), _pack_small(small_v), name="adamw_small")
    gs = _unpack_small(small_sum, small_shapes)
    ds_, ms_, vs_ = [_unpack_small(a, small_shapes) for a in (s_delta, s_m, s_v)]

    dmod_all = small_all[:, :6 * d // LANES].reshape(N_DEV, 6 * d)
    dmod_mine = jnp.pad(lax.dynamic_slice(dmod_all, (0, me * ncol), (N_DEV, ncol)), ((0, LANES - N_DEV), (0, 0)))

    def wada_body(c_ref, dm_ref, o_ref):
        chi, cmid, clo = _split3(c_ref[...])
        dhi, dmid, dlo = _split3(dm_ref[...])
        o_ref[...] = (_dot_tn(chi, dhi) + (_dot_tn(chi, dmid) + _dot_tn(cmid, dhi))
                      + (_dot_tn(chi, dlo) + _dot_tn(cmid, dmid) + _dot_tn(clo, dhi)))

    g_w_ada = pl.pallas_call(
        wada_body,
        name="wgrad_ada",
        out_shape=jax.ShapeDtypeStruct((d, ncol), F32),
        compiler_params=pltpu.CompilerParams(vmem_limit_bytes=_vmem(4 * d * ncol * 4)),
    )(c_act, dmod_mine)

    sent, recv = _split_wait(h_in + h_ffn + h_out, small_sum, gather=False, name="exchange_wait")
    names = ("w_in", "w_ffn_in", "w_ffn_out", "w_proj_a", "w_proj_b", "w_o")
    transposed = (True, True, False, True, True, False)
    big_g = {}
    for nme, own, r, tr in zip(names, sent, recv, transposed):
        s = _sum8(own_block(own), r, name=f"sum_{nme}")
        big_g[nme] = s.T if tr else s
    big_g["w_ada"] = g_w_ada
    big_w = dict(w_ada=w_ada, w_in=w_in, w_proj_a=w_proj_a, w_proj_b=w_proj_b, w_o=w_o, w_ffn_in=w_ffn_in, w_ffn_out=w_ffn_out)
    big_m = dict(w_ada=m_w_ada, w_in=m_w_in, w_proj_a=m_w_proj_a, w_proj_b=m_w_proj_b, w_o=m_w_o, w_ffn_in=m_w_ffn_in, w_ffn_out=m_w_ffn_out)
    big_v = dict(w_ada=v_w_ada, w_in=v_w_in, w_proj_a=v_w_proj_a, w_proj_b=v_w_proj_b, w_o=v_w_o, w_ffn_in=v_w_ffn_in, w_ffn_out=v_w_ffn_out)
    grads, deltas, new_m, new_v = {}, {}, {}, {}
    for nme in big_w:
        dl, m1, v1 = _adamw(big_w[nme][0], big_g[nme], big_m[nme][0], big_v[nme][0], name=f"adamw_{nme}")
        grads[nme], deltas[nme], new_m[nme], new_v[nme] = big_g[nme][None], dl[None], m1[None], v1[None]
    for nme in _SMALL[:-1]:
        grads[nme], deltas[nme], new_m[nme], new_v[nme] = gs[nme], ds_[nme], ms_[nme], vs_[nme]

    order = ("w_ada", "b_ada", "g_norm1", "g_norm2", "w_in", "b_gate", "g_qa", "g_ka", "g_qb", "g_kb", "rpb",
             "w_proj_a", "w_proj_b", "w_o", "w_ffn_in", "w_ffn_out")
    return (gs["loss"], grad_x[None], *[grads[n] for n in order], *[deltas[n] for n in order],
            *[new_m[n] for n in order], *[new_v[n] for n in order])
```

```python
import functools

import numpy as np
import jax
import jax.numpy as jnp
from jax import lax
from jax.experimental import pallas as pl
from jax.experimental.pallas import tpu as pltpu

F32 = jnp.float32
MXU_DTYPE = jnp.bfloat16
WIRE_DTYPE = jnp.bfloat16
ACT_DTYPE = jnp.bfloat16

HEAD_DIM = 64
GRID_W = 64
NA_HEADS = 8
NA_KH = 8
NA_KW = 16
DIL_CONFIGS = ((128, 1), (512, 4), (2048, 16))
DIL_HEADS_PER_GROUP = 4
DIL_HEADS = DIL_HEADS_PER_GROUP * len(DIL_CONFIGS)
DIL_HALF = 64
ROT_DIM = HEAD_DIM // 4
ROPE_THETA = 500000.0
EPS = 1e-6
NEG_INF = -1e30
WA = NA_HEADS * HEAD_DIM
WB = DIL_HEADS * HEAD_DIM
WB_OUT = DIL_HEADS_PER_GROUP * HEAD_DIM
ADAM_LR = 0.001
ADAM_B1 = 0.9
ADAM_B2 = 0.999
ADAM_EPS = 1e-08
ADAM_WD = 0.01
ADAM_STEP = 10

N_DEV = 8
LANES = 128
VMEM_CAP = 60 * 2**20
VMEM_FLOOR = 56 * 2**20
MESH = pl.DeviceIdType.MESH
ANY = pl.BlockSpec(memory_space=pl.ANY)


def _vmem(nbytes):
    return int(min(VMEM_CAP, max(VMEM_FLOOR, nbytes * 5 // 4 + 4 * 2**20)))


def _pick(dim, cands):
    for c in cands:
        if c <= dim and dim % c == 0:
            return c
    return dim


def _nbytes(shape, dtype):
    return int(np.prod(shape)) * jnp.dtype(dtype).itemsize


def _dot(a, b, dims=((1,), (0,))):
    return lax.dot_general(a.astype(MXU_DTYPE), b.astype(MXU_DTYPE), (dims, ((), ())), preferred_element_type=F32)


def _dot_nt(a, b):
    return _dot(a, b, ((1,), (1,)))


def _dot_tn(a, b):
    return _dot(a, b, ((0,), (0,)))


def _split3(a):
    hi = a.astype(jnp.bfloat16)
    r1 = a - hi.astype(F32)
    mid = r1.astype(jnp.bfloat16)
    lo = (r1 - mid.astype(F32)).astype(jnp.bfloat16)
    return hi, mid, lo


def _silu(x):
    return x * jax.nn.sigmoid(x)


def _divisors(dim, unit):
    return [c for c in range(unit, dim + 1, unit) if dim % c == 0] or [dim]


def _mm_tiles(m, n, kdim, a_item, b_item, o_item):
    step_us, hbm_bytes_per_us, flops_per_us, budget = 0.35, 3.0e6, 8.0e8, 40 * 2**20
    best = None
    for tm in _divisors(m, LANES):
        for tn in _divisors(n, LANES):
            for tk in _divisors(kdim, LANES):
                gm, gn, gk = m // tm, n // tn, kdim // tk
                vmem = 2 * (tm * tk * a_item + tk * tn * b_item + tm * tn * o_item) + 2 * (tm * tk + tk * tn)
                vmem += tm * tn * 4 * ((1 if gk > 1 else 0) + 1)
                if vmem > budget:
                    continue
                a_reads = m * kdim * a_item * (gn if gk > 1 else 1)
                traffic = a_reads + kdim * n * b_item * gm + m * n * o_item
                cost = gm * gn * gk * step_us + max(traffic / hbm_bytes_per_us, 2.0 * m * n * kdim / flops_per_us)
                if best is None or cost < best[0]:
                    best = (cost, tm, tn, tk)
    return best[1:]


def _mm(a, b, *, name, ta=False, tb=False, out_dtype=F32):
    if ta:
        kdim, m = a.shape
    else:
        m, kdim = a.shape
    n = b.shape[0] if tb else b.shape[1]
    assert b.shape[1 if tb else 0] == kdim
    tm, tn, tk = _mm_tiles(m, n, kdim, a.dtype.itemsize, b.dtype.itemsize, jnp.dtype(out_dtype).itemsize)
    gm, gn, gk = m // tm, n // tn, kdim // tk

    a_spec = pl.BlockSpec((tk, tm), lambda i, j, k: (k, i)) if ta else pl.BlockSpec((tm, tk), lambda i, j, k: (i, k))
    b_spec = pl.BlockSpec((tn, tk), lambda i, j, k: (j, k)) if tb else pl.BlockSpec((tk, tn), lambda i, j, k: (k, j))
    o_spec = pl.BlockSpec((tm, tn), lambda i, j, k: (i, j))
    a_dims = (0,) if ta else (1,)
    b_dims = (1,) if tb else (0,)

    def body(a_ref, b_ref, o_ref, *scratch):
        if gk == 1:
            o_ref[...] = _dot(a_ref[...], b_ref[...], (a_dims, b_dims)).astype(o_ref.dtype)
            return
        (acc_ref,) = scratch
        k = pl.program_id(2)

        @pl.when(k == 0)
        def _():
            acc_ref[...] = jnp.zeros_like(acc_ref)

        acc_ref[...] += _dot(a_ref[...], b_ref[...], (a_dims, b_dims))

        @pl.when(k == gk - 1)
        def _():
            o_ref[...] = acc_ref[...].astype(o_ref.dtype)

    est = 2 * (tm * tk * a.dtype.itemsize + tk * tn * b.dtype.itemsize + tm * tn * jnp.dtype(out_dtype).itemsize)
    est += tm * tn * 4 + 2 * (tm * tk + tk * tn) * 2
    return pl.pallas_call(
        body,
        name=name,
        grid=(gm, gn, gk),
        in_specs=[a_spec, b_spec],
        out_specs=o_spec,
        out_shape=jax.ShapeDtypeStruct((m, n), out_dtype),
        scratch_shapes=[pltpu.VMEM((tm, tn), F32)] if gk > 1 else [],
        compiler_params=pltpu.CompilerParams(
            dimension_semantics=("parallel", "parallel", "arbitrary"), vmem_limit_bytes=_vmem(est)
        ),
    )(a, b)


def _resident(shape):
    return pl.BlockSpec(shape, lambda i: (0,) * len(shape), pipeline_mode=pl.Buffered(1))


def _row_tile(m, fixed_bytes, bytes_per_row, budget=50 * 2**20):
    fits = [tm for tm in _divisors(m, LANES) if fixed_bytes + tm * bytes_per_row <= budget]
    return max(fits) if fits else _divisors(m, LANES)[0]


def _mm_parts_rows(parts, fn, rows, vecs, outs, reds, *, name):
    m, n = parts[0][0].shape[0], parts[0][1].shape[1]
    npart, nr, nv, no = len(parts), len(rows), len(vecs), len(outs)
    row_bytes = sum(r.shape[1] * r.dtype.itemsize for r in rows) + sum(w * jnp.dtype(dt).itemsize for (w, dt) in outs)
    a_row_bytes = sum(a.shape[1] * a.dtype.itemsize for a, _ in parts)
    fixed = sum(_nbytes(b.shape, b.dtype) for _, b in parts)
    per_row = 2 * (a_row_bytes + row_bytes) + n * 4 * 5
    tm = _row_tile(m, fixed, per_row)
    sub = min(tm, 2 * LANES)

    def body(*refs):
        ab = refs[:2 * npart]
        row_refs, vec_refs = refs[2 * npart:2 * npart + nr], refs[2 * npart + nr:2 * npart + nr + nv]
        out_refs = refs[2 * npart + nr + nv:2 * npart + nr + nv + no]
        red_refs = refs[2 * npart + nr + nv + no:]
        if red_refs:
            @pl.when(pl.program_id(0) == 0)
            def _():
                for ref in red_refs:
                    ref[...] = jnp.zeros_like(ref)

        vecs_v = [v[...] for v in vec_refs]
        for s0 in range(0, tm, sub):
            sl = slice(s0, s0 + sub)
            r = _dot(ab[0][sl, :], ab[1][...])
            for p in range(1, npart):
                r = r + _dot(ab[2 * p][sl, :], ab[2 * p + 1][...])
            o, rd = fn(r, [x[sl, :].astype(F32) for x in row_refs], vecs_v)
            for ref, val in zip(out_refs, o):
                ref[sl, :] = val.astype(ref.dtype)
            for ref, val in zip(red_refs, rd):
                ref[...] += val

    in_specs, operands = [], []
    for a, b in parts:
        in_specs += [pl.BlockSpec((tm, a.shape[1]), lambda i: (i, 0)), _resident(b.shape)]
        operands += [a, b]
    in_specs += [pl.BlockSpec((tm, r.shape[1]), lambda i: (i, 0)) for r in rows]
    in_specs += [pl.BlockSpec(v.shape, functools.partial(lambda nd, i: (0,) * nd, v.ndim)) for v in vecs]
    out_specs = [pl.BlockSpec((tm, w), lambda i: (i, 0)) for (w, _) in outs]
    out_specs += [pl.BlockSpec((1, w), lambda i: (0, 0)) for w in reds]
    out_shape = [jax.ShapeDtypeStruct((m, w), dt) for (w, dt) in outs] + [jax.ShapeDtypeStruct((1, w), F32) for w in reds]
    return pl.pallas_call(
        body,
        name=name,
        grid=(m // tm,),
        in_specs=in_specs,
        out_specs=out_specs,
        out_shape=out_shape,
        compiler_params=pltpu.CompilerParams(dimension_semantics=("arbitrary",), vmem_limit_bytes=_vmem(fixed + tm * per_row)),
    )(*operands, *rows, *vecs)


def _mm_ew(a, bs, fn, rows, outs, *, name):
    m, kdim = a.shape
    n = bs[0].shape[0]
    nb, nr, no = len(bs), len(rows), len(outs)
    cw = _pick(n, (2 * LANES, LANES))
    fixed = nb * n * kdim * bs[0].dtype.itemsize
    per_row = 2 * (kdim * a.dtype.itemsize + n * (sum(r.dtype.itemsize for r in rows) + sum(jnp.dtype(dt).itemsize for dt in outs)))
    per_row += cw * 4 * 4 * (nb + 4)
    tm = _row_tile(m, fixed, per_row)

    def body(*refs):
        a_ref, b_refs = refs[0], refs[1:1 + nb]
        row_refs, out_refs = refs[1 + nb:1 + nb + nr], refs[1 + nb + nr:]
        av = a_ref[...]
        for c0 in range(0, n, cw):
            cols = slice(c0, c0 + cw)
            o = fn([_dot_nt(av, b[cols, :]) for b in b_refs], [x[:, cols].astype(F32) for x in row_refs])
            for ref, val in zip(out_refs, o):
                ref[:, cols] = val.astype(ref.dtype)

    tile = pl.BlockSpec((tm, n), lambda i: (i, 0))
    return pl.pallas_call(
        body,
        name=name,
        grid=(m // tm,),
        in_specs=[pl.BlockSpec((tm, kdim), lambda i: (i, 0))] + [_resident((n, kdim))] * nb + [tile] * nr,
        out_specs=[tile] * no,
        out_shape=[jax.ShapeDtypeStruct((m, n), dt) for dt in outs],
        compiler_params=pltpu.CompilerParams(dimension_semantics=("parallel",), vmem_limit_bytes=_vmem(fixed + tm * per_row)),
    )(a, *bs, *rows)


def _rowmap(fn, rows, vecs, outs, reds, *, name, tm=None):
    norm = []
    for r in rows:
        if not isinstance(r, tuple):
            norm.append((r, r.shape[1], 0, None))
        elif len(r) == 2:
            norm.append((r[0], r[0].shape[2], 0, r[1]))
        else:
            norm.append((r[0], r[1], r[2], None))
    rows = norm
    t = rows[0][0].shape[-2]
    if tm is None:
        per_row = 2 * sum(w * a.dtype.itemsize for (a, w, _, _) in rows) + 2 * sum(w * jnp.dtype(d).itemsize for (w, d) in outs)
        per_row += 3 * 4 * max([w for (_, w, _, _) in rows] + [w for (w, _) in outs])
        tm = max(8, min(1024, (40 * 2**20) // per_row))
    tm = _pick(t, tuple(c for c in (1024, 512, 256, 128, 64, 32, 16, 8) if c <= tm))
    nr, nv, no = len(rows), len(vecs), len(outs)

    def body(*refs):
        row_refs, vec_refs = refs[:nr], refs[nr:nr + nv]
        out_refs, red_refs = refs[nr + nv:nr + nv + no], refs[nr + nv + no:]
        o, rd = fn([r[...].astype(F32) for r in row_refs], [v[...] for v in vec_refs])
        for ref, val in zip(out_refs, o):
            ref[...] = val.astype(ref.dtype)
        if red_refs:
            @pl.when(pl.program_id(0) == 0)
            def _():
                for ref in red_refs:
                    ref[...] = jnp.zeros_like(ref)

            for ref, val in zip(red_refs, rd):
                ref[...] += val

    in_specs = [pl.BlockSpec((tm, w), functools.partial(lambda cb, i: (i, cb), cb)) if lead is None
                else pl.BlockSpec((None, tm, w), functools.partial(lambda ld, i: (ld, i, 0), lead)) for (_, w, cb, lead) in rows]
    in_specs += [pl.BlockSpec(v.shape, functools.partial(lambda nd, i: (0,) * nd, v.ndim)) for v in vecs]
    out_specs = [pl.BlockSpec((tm, w), lambda i: (i, 0)) for (w, _) in outs]
    out_specs += [pl.BlockSpec((1, w), lambda i: (0, 0)) for w in reds]
    out_shape = [jax.ShapeDtypeStruct((t, w), d) for (w, d) in outs]
    out_shape += [jax.ShapeDtypeStruct((1, w), F32) for w in reds]
    est = 2 * sum(tm * w * a.dtype.itemsize for (a, w, _, _) in rows) + 2 * sum(_nbytes(v.shape, v.dtype) for v in vecs)
    est += 2 * sum(tm * w * jnp.dtype(d).itemsize for (w, d) in outs)
    est += 6 * tm * max([w for (_, w, _, _) in rows] + [w for (w, _) in outs]) * 4
    return pl.pallas_call(
        body,
        name=name,
        grid=(t // tm,),
        in_specs=in_specs,
        out_specs=out_specs,
        out_shape=out_shape,
        compiler_params=pltpu.CompilerParams(dimension_semantics=("arbitrary",), vmem_limit_bytes=_vmem(est)),
    )(*[r[0] for r in rows], *vecs)


def _colsum(v):
    return jnp.sum(v, axis=0, keepdims=True)


def _head_ones():
    i = np.arange(LANES)
    return jnp.asarray((i[:, None] // HEAD_DIM) == (i[None, :] // HEAD_DIM), MXU_DTYPE)


def _headsum(y, ones):
    parts = []
    for j in range(y.shape[1] // LANES):
        c = y[:, j * LANES:(j + 1) * LANES]
        hi = c.astype(MXU_DTYPE)
        lo = c - hi.astype(F32)
        parts.append(_dot(hi, ones) + _dot(lo, ones))
    return parts[0] if len(parts) == 1 else jnp.concatenate(parts, axis=1)


def _rot(y, c, s_lo, s_hi):
    parts = []
    for j in range(y.shape[1] // LANES):
        yc = y[:, j * LANES:(j + 1) * LANES]
        parts.append(yc * c + pltpu.roll(yc, LANES - ROT_DIM // 2, 1) * s_lo + pltpu.roll(yc, ROT_DIM // 2, 1) * s_hi)
    return parts[0] if len(parts) == 1 else jnp.concatenate(parts, axis=1)


def _rot_tables(t):
    half = ROT_DIM // 2
    inv_freq = ROPE_THETA ** (-(jnp.arange(half, dtype=F32) * 2.0) / ROT_DIM)
    ang = jnp.arange(t).astype(F32)[:, None] * inv_freq[None, :]
    cos, sin = jnp.cos(ang), jnp.sin(ang)
    z = lambda w: jnp.zeros((t, w), F32)
    c = jnp.concatenate([cos, cos, jnp.ones((t, HEAD_DIM - ROT_DIM), F32)], axis=1)
    s_lo = jnp.concatenate([-sin, z(HEAD_DIM - half)], axis=1)
    s_hi = jnp.concatenate([z(half), sin, z(HEAD_DIM - ROT_DIM)], axis=1)
    return [jnp.tile(a, (1, LANES // HEAD_DIM)) for a in (c, s_lo, s_hi)]


def _rms(x):
    return lax.rsqrt(jnp.mean(x * x, axis=-1, keepdims=True) + EPS)


def _window(kind, n, bq, t, seg):
    if kind == "na":
        rows = t // GRID_W
        rs = jnp.clip(n - NA_KH // 2, 0, rows - NA_KH)
        return rs
    nk = bq + 2 * DIL_HALF
    return jnp.clip(n * bq - DIL_HALF, 0, t - nk)


def _dil_mask(n, bq, nk, ws, seg):
    qi = n * bq + lax.broadcasted_iota(jnp.int32, (bq, nk), 0)
    ki = ws + lax.broadcasted_iota(jnp.int32, (bq, nk), 1)
    shift = int(np.log2(seg))
    return (jnp.abs(ki - qi) <= DIL_HALF) & ((ki >> shift) == (qi >> shift))


HS = 4
QW = HS * HEAD_DIM


def _head_of_lane(width=QW):
    return lax.broadcasted_iota(jnp.int32, (1, width), 1) // HEAD_DIM


def _stack_heads(a):
    head = _head_of_lane()
    return jnp.concatenate([jnp.where(head == e, a, jnp.zeros_like(a)) for e in range(HS)], axis=0)


def _unstack_heads(a, bq):
    head = _head_of_lane()
    out = jnp.zeros((bq, QW), a.dtype)
    for e in range(HS):
        out = jnp.where(head == e, a[e * bq:(e + 1) * bq], out)
    return out


def _stack_cols(blk, bq):
    head = _head_of_lane()
    return jnp.concatenate(
        [jnp.max(jnp.where(head == e, blk, -jnp.inf), axis=1, keepdims=True) for e in range(HS)], axis=0)


def _attn_geometry(kind):
    if kind == "na":
        return GRID_W, NA_KH * GRID_W, 4
    bq = 128
    return bq, bq + 2 * DIL_HALF, 2


def _attn_scores(kind, n, bq, nk, t, seg, qs, k_ref, b_ref):
    scale = HEAD_DIM ** -0.5
    if kind == "na":
        rs = _window(kind, n, bq, t, seg)
        ws = pl.multiple_of(rs * GRID_W, GRID_W)
        ro0 = rs - n + (NA_KH - 1)
        s = _dot_nt(qs, k_ref[pl.ds(ws, nk), :]) * scale
        s = s + jnp.concatenate(
            [jnp.concatenate([b_ref[e, ro0 + 2 * i] for i in range(NA_KH // 2)], axis=1) for e in range(HS)], axis=0)
        return s, ws, ro0
    ws = pl.multiple_of(_window(kind, n, bq, t, seg), DIL_HALF)
    mask = _dil_mask(n, bq, nk, ws, seg)
    s = _dot_nt(qs, k_ref[pl.ds(ws, nk), :]) * scale
    s = jnp.where(jnp.concatenate([mask] * HS, axis=0), s, NEG_INF)
    return s, ws, None


def _attn_fwd(q, k, v, *, kind, name, bias=None, seg=None):
    t, w = q.shape
    quads = w // QW
    bq, nk, sub = _attn_geometry(kind)
    nq = t // (bq * sub)

    def body(*refs):
        if kind == "na":
            q_ref, k_ref, v_ref, b_ref, o_ref, l_ref = refs
        else:
            (q_ref, k_ref, v_ref, o_ref, l_ref), b_ref = refs, None
        for i in range(sub):
            n = pl.program_id(1) * sub + i
            rows = slice(i * bq, (i + 1) * bq)
            s, ws, _ = _attn_scores(kind, n, bq, nk, t, seg, _stack_heads(q_ref[rows, :]), k_ref, b_ref)
            m = jnp.max(s, axis=1, keepdims=True)
            p = jnp.exp(s - m)
            l = jnp.sum(p, axis=1, keepdims=True)
            o_ref[rows, :] = _unstack_heads(_dot(p / l, v_ref[pl.ds(ws, nk), :]), bq)
            l_ref[rows, :] = _unstack_heads(jnp.broadcast_to(m + jnp.log(l), (HS * bq, QW)), bq)

    blk = pl.BlockSpec((bq * sub, QW), lambda j, n: (n, j))
    res = pl.BlockSpec((t, QW), lambda j, n: (0, j))
    in_specs = [blk, res, res]
    operands = [q, k, v]
    est = 4 * t * QW * q.dtype.itemsize + 12 * sub * HS * bq * nk * 4
    if kind == "na":
        in_specs.append(pl.BlockSpec((HS,) + bias.shape[1:], lambda j, n: (j, 0, 0, 0)))
        operands.append(bias)
        est += 2 * _nbytes((HS,) + bias.shape[1:], F32)
    return pl.pallas_call(
        body,
        name=name,
        grid=(quads, nq),
        in_specs=in_specs,
        out_specs=[blk, blk],
        out_shape=[jax.ShapeDtypeStruct((t, w), F32)] * 2,
        compiler_params=pltpu.CompilerParams(dimension_semantics=("arbitrary", "arbitrary"), vmem_limit_bytes=_vmem(est)),
    )(*operands)


def _attn_bwd(q, k, v, do, dterm, lse, *, kind, name, bias=None, seg=None):
    t, w = q.shape
    quads = w // QW
    bq, nk, sub = _attn_geometry(kind)
    nq = t // (bq * sub)
    scale = HEAD_DIM ** -0.5

    def body(*refs):
        if kind == "na":
            q_ref, k_ref, v_ref, do_ref, dt_ref, l_ref, b_ref, dq_ref, dk_hbm, dv_hbm, db_ref, dk_acc, dv_acc, sem = refs
        else:
            q_ref, k_ref, v_ref, do_ref, dt_ref, l_ref, dq_ref, dk_hbm, dv_hbm, dk_acc, dv_acc, sem = refs
            b_ref = None
        j, step = pl.program_id(0), pl.program_id(1)

        @pl.when(step == 0)
        def _():
            dk_acc[...] = jnp.zeros_like(dk_acc)
            dv_acc[...] = jnp.zeros_like(dv_acc)
            if kind == "na":
                db_ref[...] = jnp.zeros_like(db_ref)

        for b in range(sub):
            n = step * sub + b
            rows = slice(b * bq, (b + 1) * bq)
            qs = _stack_heads(q_ref[rows, :])
            dos = _stack_heads(do_ref[rows, :])
            s, ws, ro0 = _attn_scores(kind, n, bq, nk, t, seg, qs, k_ref, b_ref)
            p = jnp.exp(s - _stack_cols(l_ref[rows, :], bq))
            dp = _dot_nt(dos, v_ref[pl.ds(ws, nk), :])
            ds = p * (dp - _stack_cols(dt_ref[rows, :], bq))
            if kind == "na":
                for e in range(HS):
                    for i in range(NA_KH // 2):
                        db_ref[e, ro0 + 2 * i] += ds[e * bq:(e + 1) * bq, i * LANES:(i + 1) * LANES]
            dsc = ds * scale
            dq_ref[rows, :] = _unstack_heads(_dot(dsc, k_ref[pl.ds(ws, nk), :]), bq)
            dk_acc[pl.ds(ws, nk), :] += _dot_tn(dsc, qs)
            dv_acc[pl.ds(ws, nk), :] += _dot_tn(p, dos)

        @pl.when(step == nq - 1)
        def _():
            ck = pltpu.make_async_copy(dk_acc, dk_hbm.at[j], sem.at[0])
            cv = pltpu.make_async_copy(dv_acc, dv_hbm.at[j], sem.at[1])
            ck.start()
            cv.start()
            ck.wait()
            cv.wait()

    blk = pl.BlockSpec((bq * sub, QW), lambda j, n: (n, j))
    res = pl.BlockSpec((t, QW), lambda j, n: (0, j))
    in_specs = [blk, res, res, blk, blk, blk]
    operands = [q, k, v, do, dterm, lse]
    out_specs = [blk, ANY, ANY]
    out_shape = [jax.ShapeDtypeStruct((t, w), F32)] + [jax.ShapeDtypeStruct((quads, t, QW), F32)] * 2
    est = 4 * t * QW * q.dtype.itemsize + 2 * t * QW * 4 + 16 * sub * HS * bq * nk * 4
    if kind == "na":
        bspec = pl.BlockSpec((HS,) + bias.shape[1:], lambda j, n: (j, 0, 0, 0))
        in_specs.append(bspec)
        operands.append(bias)
        out_specs.append(bspec)
        out_shape.append(jax.ShapeDtypeStruct(bias.shape, F32))
        est += 4 * _nbytes((HS,) + bias.shape[1:], F32)
    res_ = pl.pallas_call(
        body,
        name=name,
        grid=(quads, nq),
        in_specs=in_specs,
        out_specs=out_specs,
        out_shape=out_shape,
        scratch_shapes=[pltpu.VMEM((t, QW), F32), pltpu.VMEM((t, QW), F32), pltpu.SemaphoreType.DMA((2,))],
        compiler_params=pltpu.CompilerParams(dimension_semantics=("arbitrary", "arbitrary"), vmem_limit_bytes=_vmem(est)),
    )(*operands)
    unquad = lambda a: [(a, i) for i in range(quads)]
    return (res_[0], unquad(res_[1]), unquad(res_[2])) + tuple(res_[3:])


def _na_onehot():
    qc = np.arange(GRID_W)[:, None]
    kc = np.arange(GRID_W)[None, :]
    start = np.clip(qc - NA_KW // 2, 0, GRID_W - NA_KW)
    inwin = (kc >= start) & (kc < start + NA_KW)
    off = kc - qc + (NA_KW - 1)
    e_mat = np.zeros((2, 32, GRID_W, 2, GRID_W), np.float32)
    for e in range(2):
        for c in range(2 * NA_KW - 1):
            e_mat[e, c, :, e, :] = (off == c) & inwin
    neg = np.where(inwin, 0.0, NEG_INF).astype(np.float32)
    neg = np.broadcast_to(neg[:, None, :], (GRID_W, 2, GRID_W)).reshape(1, GRID_W * LANES)
    return jnp.asarray(e_mat.reshape(64, GRID_W * LANES), MXU_DTYPE), jnp.asarray(neg)


def _na_rowpairs(rpb):
    p = jnp.pad(rpb, ((0, 0), (0, 0), (0, 1)))
    return jnp.concatenate([p[:, :-1], p[:, 1:]], axis=-1).reshape(NA_HEADS * (2 * NA_KH - 2), 64)


def _na_bias_table(rpb):
    r2 = _na_rowpairs(rpb)
    e_mat, neg = _na_onehot()

    def body(r_ref, e_ref, n_ref, o_ref):
        hi, mid, lo = _split3(r_ref[...])
        e = e_ref[...]
        o_ref[...] = _dot(hi, e) + _dot(mid, e) + _dot(lo, e) + n_ref[...]

    out = pl.pallas_call(
        body,
        name="na_bias_table",
        out_shape=jax.ShapeDtypeStruct((r2.shape[0], GRID_W * LANES), F32),
        compiler_params=pltpu.CompilerParams(vmem_limit_bytes=_vmem(6 * r2.shape[0] * GRID_W * LANES * 4)),
    )(r2, e_mat, neg)
    return out.reshape(NA_HEADS, 2 * NA_KH - 2, GRID_W, LANES)


def _na_bias_grad(dbt):
    e_mat, _ = _na_onehot()
    flat = dbt.reshape(NA_HEADS * (2 * NA_KH - 2), GRID_W * LANES)

    def body(d_ref, e_ref, o_ref):
        hi, mid, lo = _split3(d_ref[...])
        e = e_ref[...]
        o_ref[...] = _dot_nt(hi, e) + _dot_nt(mid, e) + _dot_nt(lo, e)

    g = pl.pallas_call(
        body,
        name="na_bias_grad",
        out_shape=jax.ShapeDtypeStruct((flat.shape[0], 64), F32),
        compiler_params=pltpu.CompilerParams(vmem_limit_bytes=_vmem(6 * flat.shape[0] * GRID_W * LANES * 4)),
    )(flat, e_mat)
    g = g.reshape(NA_HEADS, 2 * NA_KH - 2, 2, 32)[..., :2 * NA_KW - 1]
    first = jnp.pad(g[:, :, 0], ((0, 0), (0, 1), (0, 0)))
    second = jnp.pad(g[:, :, 1], ((0, 0), (1, 0), (0, 0)))
    return first + second


def _all_gather(arrs, *, name):
    na = len(arrs)

    def body(*refs):
        ins, outs = refs[:na], refs[na:2 * na]
        send_sems, recv_sems, local_sems = refs[2 * na:]
        x, y, c = lax.axis_index("x"), lax.axis_index("y"), lax.axis_index("c")
        me, sibling = (x, y, c), (x, y, 1 - c)
        chips = [(1 - x, y), (x, 1 - y), (1 - x, 1 - y)]

        def rows(a, px, py, pc):
            r = ins[a].shape[0]
            return outs[a].at[pl.ds((4 * px + 2 * py + pc) * r, r), :]

        def copy(a, k, block, to, src=None):
            return pltpu.make_async_remote_copy(
                src_ref=rows(a, *block) if src is None else src, dst_ref=rows(a, *block),
                send_sem=send_sems.at[a, k], recv_sem=recv_sems.at[a, k], device_id=to, device_id_type=MESH)

        mine = [pltpu.make_async_copy(ins[a], rows(a, *me), local_sems.at[a]) for a in range(na)]
        for cp in mine:
            cp.start()
        first = []
        for a in range(na):
            first.append(copy(a, 0, me, sibling, src=ins[a]))
            first += [copy(a, 1 + j, me, (*chip, c), src=ins[a]) for j, chip in enumerate(chips)]
        for cp in first:
            cp.start()
        passed = []
        for j, chip in enumerate(chips):
            for a in range(na):
                copy(a, 1 + j, (*chip, c), me).wait_recv()
                cp = copy(a, 4 + j, (*chip, c), sibling)
                cp.start()
                passed.append(cp)
        for a in range(na):
            copy(a, 0, sibling, me).wait_recv()
        for j, chip in enumerate(chips):
            for a in range(na):
                copy(a, 4 + j, (*chip, 1 - c), me).wait_recv()
        for cp in first + passed:
            cp.wait_send()
        for cp in mine:
            cp.wait()

    return pl.pallas_call(
        body,
        name=name,
        in_specs=[ANY] * na,
        out_specs=[ANY] * na,
        out_shape=[jax.ShapeDtypeStruct((N_DEV * a.shape[0], a.shape[1]), a.dtype) for a in arrs],
        scratch_shapes=[pltpu.SemaphoreType.DMA((na, 7)), pltpu.SemaphoreType.DMA((na, 7)), pltpu.SemaphoreType.DMA((na,))],
    )(*arrs)


HBM = pl.BlockSpec(memory_space=pltpu.HBM)
SEM = pl.BlockSpec(memory_space=pltpu.SEMAPHORE)
EFFECT = pltpu.SideEffectType.DATAFLOW_SIDE_EFFECTING


def _peer_of(k):
    x, y, c = lax.axis_index("x"), lax.axis_index("y"), lax.axis_index("c")
    return x ^ ((k >> 2) & 1), y ^ ((k >> 1) & 1), c ^ (k & 1)


def _split_copies(gather, src_ref, land_ref, send_sems, recv_sems):
    x, y, c = lax.axis_index("x"), lax.axis_index("y"), lax.axis_index("c")
    my = 4 * x + 2 * y + c
    r = src_ref.shape[0] if gather else src_ref.shape[0] // N_DEV
    copies = []
    for k in range(1, N_DEV):
        px, py, pc = _peer_of(k)
        if gather:
            src, dst = src_ref, land_ref.at[pl.ds(my * r, r), :]
        else:
            src, dst = src_ref.at[pl.ds((4 * px + 2 * py + pc) * r, r), :], land_ref.at[k - 1]
        copies.append(pltpu.make_async_remote_copy(
            src_ref=src, dst_ref=dst, send_sem=send_sems.at[k - 1], recv_sem=recv_sems.at[k - 1],
            device_id=(px, py, pc), device_id_type=MESH))
    return copies


def _split_start(srcs, lands, *, gather, name, after=None):
    na = len(srcs)
    extra = [] if after is None else [after]

    def body(*refs):
        src_refs, land_refs = refs[:na], refs[na:2 * na]
        outs = refs[2 * na + len(extra):]
        for a in range(na):
            for cp in _split_copies(gather, src_refs[a], land_refs[a], outs[4 * a], outs[4 * a + 1]):
                cp.start()
        outs[4 * na][...] = jnp.zeros_like(outs[4 * na])

    out_shape, out_specs, aliases = [], [], {}
    for a in range(na):
        out_shape += [pltpu.SemaphoreType.DMA((N_DEV - 1,)), pltpu.SemaphoreType.DMA((N_DEV - 1,)),
                      pltpu.HBM(srcs[a].shape, srcs[a].dtype), pltpu.HBM(lands[a].shape, lands[a].dtype)]
        out_specs += [SEM, SEM, HBM, HBM]
        aliases[a] = 4 * a + 2
        aliases[na + a] = 4 * a + 3
    out_shape.append(jax.ShapeDtypeStruct((8, LANES), F32))
    out_specs.append(pl.BlockSpec(memory_space=pltpu.VMEM))
    res = pl.pallas_call(
        body,
        name=name,
        out_shape=tuple(out_shape),
        in_specs=[HBM] * (2 * na) + [ANY] * len(extra),
        out_specs=tuple(out_specs),
        input_output_aliases=aliases,
        compiler_params=pltpu.CompilerParams(has_side_effects=EFFECT),
    )(*[pltpu.with_memory_space_constraint(a, pltpu.HBM) for a in list(srcs) + list(lands)], *extra)
    return [tuple(res[4 * a:4 * a + 4]) for a in range(na)], res[4 * na][0, 0]


def _split_wait(handles, after, *, gather, name):
    na = len(handles)

    def body(*refs):
        src_refs, land_refs = refs[:na], refs[na:2 * na]
        sems = refs[2 * na:4 * na]
        for a in range(na):
            for cp in _split_copies(gather, src_refs[a], land_refs[a], sems[2 * a], sems[2 * a + 1]):
                cp.wait_send()
                cp.wait_recv()

    srcs = [h[2] for h in handles]
    lands = [h[3] for h in handles]
    sems = [s for h in handles for s in h[:2]]
    res = pl.pallas_call(
        body,
        name=name,
        out_shape=tuple(pltpu.HBM(a.shape, a.dtype) for a in srcs + lands),
        in_specs=[HBM] * (2 * na) + [SEM] * (2 * na) + [ANY],
        out_specs=tuple([HBM] * (2 * na)),
        input_output_aliases={i: i for i in range(2 * na)},
        compiler_params=pltpu.CompilerParams(has_side_effects=EFFECT),
    )(*srcs, *lands, *sems, after)
    return list(res[:na]), list(res[na:])


def _sum8(own, recv, *, name):
    _, r, w = recv.shape
    tr = _pick(r, (256, 128, 64, 32, 16, 8))

    def body(own_ref, a_ref, o_ref):
        acc = own_ref[...].astype(F32)
        for i in range(N_DEV - 1):
            acc = acc + a_ref[i].astype(F32)
        o_ref[...] = acc

    return pl.pallas_call(
        body,
        name=name,
        grid=(r // tr,),
        in_specs=[pl.BlockSpec((tr, w), lambda i: (i, 0)), pl.BlockSpec((N_DEV - 1, tr, w), lambda i: (0, i, 0))],
        out_specs=pl.BlockSpec((tr, w), lambda i: (i, 0)),
        out_shape=jax.ShapeDtypeStruct((r, w), F32),
        compiler_params=pltpu.CompilerParams(dimension_semantics=("parallel",), vmem_limit_bytes=_vmem(4 * N_DEV * tr * w * 4)),
    )(own, recv)


def _adamw(w, g, m, v, *, name):
    def fn(rows, _):
        wv, gv, mv, vv = rows
        m1 = ADAM_B1 * mv + (1.0 - ADAM_B1) * gv
        v1 = ADAM_B2 * vv + (1.0 - ADAM_B2) * jnp.square(gv)
        m_hat = m1 / (1.0 - ADAM_B1 ** ADAM_STEP)
        v_hat = v1 / (1.0 - ADAM_B2 ** ADAM_STEP)
        delta = -ADAM_LR * (m_hat / (jnp.sqrt(v_hat) + ADAM_EPS) + ADAM_WD * wv)
        return [delta, m1, v1], []

    c = w.shape[1]
    return _rowmap(fn, [w, g, m, v], [], [(c, F32)] * 3, [], name=name)


_SMALL = ("b_ada", "g_norm1", "g_norm2", "b_gate", "g_qa", "g_ka", "g_qb", "g_kb", "rpb", "loss")


def _pack_small(parts):
    flat = []
    for nme in _SMALL:
        a = parts[nme].reshape(-1).astype(F32)
        flat.append(jnp.pad(a, (0, (-a.shape[0]) % LANES)))
    flat = jnp.concatenate(flat)
    flat = jnp.pad(flat, (0, (-flat.shape[0]) % (8 * LANES)))
    return flat.reshape(-1, LANES)


def _unpack_small(packed, shapes):
    flat = packed.reshape(-1)
    out, pos = {}, 0
    for nme in _SMALL:
        n = int(np.prod(shapes[nme]))
        out[nme] = flat[pos:pos + n].reshape(shapes[nme])
        pos += n + (-n) % LANES
    return out


def _to_class(a, d):
    t, w = a.shape
    return a if d == 1 else a.reshape(t // d, d, w).transpose(1, 0, 2).reshape(t, w)


def _from_class(a, d):
    t, w = a.shape
    return a if d == 1 else a.reshape(d, t // d, w).transpose(1, 0, 2).reshape(t, w)


def kernel(x, c, w_ada, b_ada, g_norm1, g_norm2, w_in, b_gate, g_qa, g_ka, g_qb, g_kb, rpb, w_proj_a, w_proj_b, w_o, w_ffn_in, w_ffn_out, loss_target, m_w_ada, m_b_ada, m_g_norm1, m_g_norm2, m_w_in, m_b_gate, m_g_qa, m_g_ka, m_g_qb, m_g_kb, m_rpb, m_w_proj_a, m_w_proj_b, m_w_o, m_w_ffn_in, m_w_ffn_out, v_w_ada, v_b_ada, v_g_norm1, v_g_norm2, v_w_in, v_b_gate, v_g_qa, v_g_ka, v_g_qb, v_g_kb, v_rpb, v_w_proj_a, v_w_proj_b, v_w_o, v_w_ffn_in, v_w_ffn_out):
    t, d = x.shape[1], x.shape[2]
    d_ff = w_ffn_out.shape[1] * N_DEV
    me = 4 * lax.axis_index("x") + 2 * lax.axis_index("y") + lax.axis_index("c")
    xt, tgt = x.reshape(t, d), loss_target.reshape(t, d)
    ones = _head_ones()

    shards = [s.astype(WIRE_DTYPE) for s in (w_in[0].T, w_ffn_in[0].T, w_proj_a[0].T, w_proj_b[0].T, w_o[0], w_ffn_out[0])]
    lands = [lax.dynamic_update_slice(lax.empty((N_DEV * s.shape[0], s.shape[1]), s.dtype), s, (me * s.shape[0], 0))
             for s in shards]

    c_all = _all_gather([jnp.pad(c, ((0, 7), (0, 0)))], name="gather_c")[0][::8]
    c_all = jnp.pad(c_all, ((0, LANES - N_DEV), (0, 0)))

    def mod_body(c_ref, w_ref, b_ref, o_ref, act_ref):
        act = _silu(c_ref[...])
        act_ref[...] = act
        hi, mid, lo = _split3(act)
        w = w_ref[...]
        whi, wmid, wlo = _split3(w)
        acc = _dot(hi, whi) + (_dot(hi, wmid) + _dot(mid, whi)) + (_dot(hi, wlo) + _dot(mid, wmid) + _dot(lo, whi))
        o_ref[...] = acc + b_ref[...]

    ncol = w_ada.shape[2]
    b_ada_mine = lax.dynamic_slice(b_ada, (0, me * ncol), (1, ncol))
    mod_part, c_act = pl.pallas_call(
        mod_body,
        name="ada_mod",
        out_shape=[jax.ShapeDtypeStruct((LANES, ncol), F32), jax.ShapeDtypeStruct((LANES, d), F32)],
        compiler_params=pltpu.CompilerParams(vmem_limit_bytes=_vmem(6 * d * ncol * 4)),
    )(c_all, w_ada[0], b_ada_mine)
    mod_all = _all_gather([mod_part[:N_DEV]], name="gather_mod")[0].reshape(N_DEV, N_DEV, ncol)
    mod = lax.dynamic_index_in_dim(mod_all, me, axis=1, keepdims=False).reshape(6, d)
    sh1, sc1, gt1, sh2, sc2, gt2 = [mod[i:i + 1] for i in range(6)]

    def norm_fwd(rows, vecs):
        (xv,), (g, sc, sh) = rows, vecs
        return [xv * _rms(xv) * g * (1.0 + sc) + sh], []

    w_handles, w_token = _split_start(shards, lands, gather=True, after=mod, name="gather_weights_start")
    (h,) = _rowmap(norm_fwd, [xt], [g_norm1 + w_token, sc1, sh1], [(d, MXU_DTYPE)], [], name="norm1")
    n_a, n_b = 3 * WA, 3 * WB
    (w_in_t,) = _split_wait(w_handles[:1], h, gather=True, name="gather_w_in_wait")[1]
    w_in_a, w_in_b, w_in_g = w_in_t[:n_a], w_in_t[n_a:n_a + n_b], w_in_t[n_a + n_b:]
    qkv_a = _mm(h, w_in_a, tb=True, out_dtype=ACT_DTYPE, name="proj_a")
    qkv_b = _mm(h, w_in_b, tb=True, out_dtype=ACT_DTYPE, name="proj_b")
    gates = _mm(h, w_in_g, tb=True, out_dtype=ACT_DTYPE, name="proj_gates")

    rot_c, rot_lo, rot_hi = _rot_tables(t)
    tile_g = lambda g, heads: jnp.tile(g, (1, heads))

    def qk_fwd(width, rotate):
        def fn(rows, vecs):
            xv = rows[0]
            gq, gk, on = vecs
            outs = []
            for i, g in enumerate((gq, gk)):
                xi = xv[:, i * width:(i + 1) * width]
                r = lax.rsqrt(_headsum(xi * xi, on) * (1.0 / HEAD_DIM) + EPS)
                yi = xi * r * g
                if rotate:
                    yi = _rot(yi, rows[1], rows[2], rows[3])
                outs.append(yi)
            outs.append(xv[:, 2 * width:])
            return outs, []
        return fn

    qa, ka, va = _rowmap(qk_fwd(WA, False), [qkv_a], [tile_g(g_qa, NA_HEADS), tile_g(g_ka, NA_HEADS), ones],
                         [(WA, MXU_DTYPE)] * 3, [], name="qknorm_a")
    qb, kb, vb = _rowmap(qk_fwd(WB, True), [qkv_b, rot_c, rot_lo, rot_hi],
                         [tile_g(g_qb, DIL_HEADS), tile_g(g_kb, DIL_HEADS), ones], [(WB, MXU_DTYPE)] * 3, [], name="qknorm_b")

    bias_tab = _na_bias_table(rpb[0])
    o_a, lse_a = _attn_fwd(qa, ka, va, kind="na", bias=bias_tab, name="na_fwd")

    grp = []
    for g, (_, dil) in enumerate(DIL_CONFIGS):
        sl = slice(g * WB_OUT, (g + 1) * WB_OUT)
        qg, kg, vg = [_to_class(a[:, sl], dil) for a in (qb, kb, vb)]
        og, lg = _attn_fwd(qg, kg, vg, kind="dil", seg=t // dil, name=f"dil_fwd{g}")
        grp.append(dict(q=qg, k=kg, v=vg, o=_from_class(og, dil), lse=_from_class(lg, dil), lse_c=lg, dil=dil))

    def merge_fwd(rows, _):
        o0, o1, o2, l0, l1, l2 = rows
        mx = jnp.maximum(jnp.maximum(l0, l1), l2)
        e0, e1, e2 = jnp.exp(l0 - mx), jnp.exp(l1 - mx), jnp.exp(l2 - mx)
        s = e0 + e1 + e2
        return [(e0 / s) * o0 + (e1 / s) * o1 + (e2 / s) * o2], []

    (o_b,) = _rowmap(merge_fwd, [gr["o"] for gr in grp] + [gr["lse"] for gr in grp], [], [(WB_OUT, F32)], [], name="dil_merge")

    w_pa_t, w_pb_t, w_o_f = _split_wait(w_handles[2:5], o_b, gather=True, name="gather_w_out_wait")[1]
    pa = _mm(o_a, w_pa_t, tb=True, out_dtype=ACT_DTYPE, name="proj_out_a")
    pb = _mm(o_b, w_pb_t, tb=True, out_dtype=ACT_DTYPE, name="proj_out_b")

    def gate_fwd(rows, vecs):
        gv, pav, pbv = rows
        sg = jax.nn.sigmoid(gv + vecs[0])
        return [sg[:, :d] * pav + sg[:, d:] * pbv], []

    (merged,) = _rowmap(gate_fwd, [gates, pa, pb], [b_gate], [(d, MXU_DTYPE)], [], name="gate_merge")
    def resid_norm(av, rows, vecs):
        (xv,), (gt, g, sc, sh) = rows, vecs
        x1v = xv + gt * av
        return [av, x1v, x1v * _rms(x1v) * g * (1.0 + sc) + sh], []

    att, x1, h2 = _mm_parts_rows([(merged, w_o_f)], resid_norm, [xt], [gt1, g_norm2, sc2, sh2],
                           [(d, F32), (d, F32), (d, MXU_DTYPE)], [], name="proj_o_resid_norm2")

    w_ffn_in_t, w_ffn_out_f = _split_wait([w_handles[1], w_handles[5]], h2, gather=True, name="gather_w_ffn_wait")[1]
    w_ffn_a, w_ffn_up = w_ffn_in_t[:d_ff], w_ffn_in_t[d_ff:]

    def swiglu_fwd(prods, _):
        a, up = prods
        return [a, up, _silu(a) * up]

    ua, uu, f = _mm_ew(h2, [w_ffn_a, w_ffn_up], swiglu_fwd, [], [ACT_DTYPE, ACT_DTYPE, MXU_DTYPE], name="ffn_in_swiglu")

    def loss_fn(yv, rows, vecs):
        (x1v, tv), gt = rows, vecs[0]
        err = x1v + gt * yv - tv
        dout = err * (1.0 / d)
        return [dout, dout * gt], [_colsum(err * err), _colsum(dout * yv)]

    dout, dy2, err2, dgt2 = _mm_parts_rows([(f, w_ffn_out_f)], loss_fn, [x1, tgt], [gt2], [(d, F32), (d, MXU_DTYPE)],
                                           [d, d], name="ffn_out_loss")

    dw_ffn_out = _mm(f, dy2, ta=True, out_dtype=WIRE_DTYPE, name="wgrad_ffn_out")
    def swiglu_bwd(prods, rows):
        (dfv,), (a, up) = prods, rows
        sg = jax.nn.sigmoid(a)
        return [dfv * up * (sg * (1.0 + a * (1.0 - sg))), dfv * (a * sg)]

    da, dup = _mm_ew(dy2, [w_ffn_out_f], swiglu_bwd, [ua, uu], [MXU_DTYPE, MXU_DTYPE], name="dgrad_ffn_out_swiglu_bwd")
    dw_ffn_in_t = jnp.concatenate([_mm(da, h2, ta=True, out_dtype=WIRE_DTYPE, name="wgrad_ffn_in_a"),
                                   _mm(dup, h2, ta=True, out_dtype=WIRE_DTYPE, name="wgrad_ffn_in_up")], axis=0)
    land7 = lambda a: lax.empty((N_DEV - 1, a.shape[0] // N_DEV, a.shape[1]), a.dtype)
    own_block = lambda a: lax.dynamic_slice(a, (me * (a.shape[0] // N_DEV), 0), (a.shape[0] // N_DEV, a.shape[1]))
    g_ffn = [dw_ffn_in_t, dw_ffn_out]
    h_ffn, tok_ffn = _split_start(g_ffn, [land7(a) for a in g_ffn], gather=False, name="exchange_ffn_start")
    def norm_bwd(dh, xv, g, sc):
        r = _rms(xv)
        xh = xv * r
        dxh = dh * g * (1.0 + sc)
        dxv = r * (dxh - xh * jnp.mean(dxh * xh, axis=-1, keepdims=True))
        return dxv, [_colsum(dh), _colsum(dh * xh * g), _colsum(dh * xh * (1.0 + sc))]

    def norm2_bwd(dhv, rows, vecs):
        (x1v, dov, av), (g, sc, gt) = rows, vecs
        dxv, sums = norm_bwd(dhv, x1v, g, sc)
        dx1v = dov + dxv
        return [dx1v, dx1v * gt], sums + [_colsum(dx1v * av)]

    dx1, datt, dsh2, dsc2, dg2, dgt1 = _mm_parts_rows(
        [(da, w_ffn_a), (dup, w_ffn_up)], norm2_bwd, [x1, dout, att], [g_norm2 + tok_ffn, sc2, gt1],
        [(d, F32), (d, MXU_DTYPE)], [d] * 4, name="dgrad_ffn_in_norm2_bwd")
    dw_o = _mm(merged, datt, ta=True, out_dtype=WIRE_DTYPE, name="wgrad_o")
    def gate_bwd(dm, rows, vecs):
        gv, pav, pbv = rows
        sg = jax.nn.sigmoid(gv + vecs[0])
        ga, gb = sg[:, :d], sg[:, d:]
        dgp = jnp.concatenate([dm * pav * ga * (1.0 - ga), dm * pbv * gb * (1.0 - gb)], axis=1)
        return [dm * ga, dm * gb, dgp], [_colsum(dgp)]

    dpa, dpb, dgates, db_gate = _mm_parts_rows(
        [(datt, w_o_f.T)], gate_bwd, [gates, pa, pb], [b_gate],
        [(d, MXU_DTYPE), (d, MXU_DTYPE), (2 * d, MXU_DTYPE)], [2 * d], name="dgrad_o_gate_bwd")
    dw_pa_t = _mm(dpa, o_a, ta=True, out_dtype=WIRE_DTYPE, name="wgrad_proj_a")
    dw_pb_t = _mm(dpb, o_b, ta=True, out_dtype=WIRE_DTYPE, name="wgrad_proj_b")
    g_out = [dw_pa_t, dw_pb_t, dw_o]
    h_out, tok_out = _split_start(g_out, [land7(a) for a in g_out], gather=False, name="exchange_out_start")
    def delta_a(doa, rows, vecs):
        return [doa, _headsum(doa * rows[0], vecs[0])], []

    do_a, dterm_a = _mm_parts_rows([(dpa, w_pa_t)], delta_a, [o_a], [ones + tok_out.astype(ones.dtype)],
                                   [(WA, F32), (WA, F32)], [], name="dgrad_proj_a_delta")
    dqa, dka, dva, dbias = _attn_bwd(qa, ka, va, do_a, dterm_a, lse_a, kind="na", bias=bias_tab, name="na_bwd")
    g_rpb = _na_bias_grad(dbias)

    def merge_bwd(dob, rows, vecs):
        o0, o1, o2, l0, l1, l2 = rows
        on = vecs[0]
        mx = jnp.maximum(jnp.maximum(l0, l1), l2)
        e0, e1, e2 = jnp.exp(l0 - mx), jnp.exp(l1 - mx), jnp.exp(l2 - mx)
        s = e0 + e1 + e2
        ws = [e0 / s, e1 / s, e2 / s]
        dws = [_headsum(dob * o, on) for o in (o0, o1, o2)]
        mean = ws[0] * dws[0] + ws[1] * dws[1] + ws[2] * dws[2]
        return [w * dob for w in ws] + [w * mean for w in ws], []

    mb = _mm_parts_rows([(dpb, w_pb_t)], merge_bwd, [gr["o"] for gr in grp] + [gr["lse"] for gr in grp], [ones],
                        [(WB_OUT, F32)] * 6, [], name="dgrad_proj_b_merge_bwd")
    dqb, dkb, dvb = [], [], []
    for g, gr in enumerate(grp):
        dil = gr["dil"]
        dq, dk, dv = _attn_bwd(gr["q"], gr["k"], gr["v"], _to_class(mb[g], dil), _to_class(mb[3 + g], dil), gr["lse_c"],
                               kind="dil", seg=t // dil, name=f"dil_bwd{g}")
        dqb.append(_from_class(dq, dil))
        dkb.append(_from_class(dk[0][0][0], dil))
        dvb.append(_from_class(dv[0][0][0], dil))

    def qk_bwd(width, rotate, nparts):
        def fn(rows, vecs):
            gq, gk, on = vecs
            xv = rows[0]
            pos = 1
            if rotate:
                rc, rlo, rhi = rows[1:4]
                pos = 4
            cat = lambda parts: parts[0] if len(parts) == 1 else jnp.concatenate(parts, axis=1)
            ends = np.cumsum((pos,) + nparts)
            dq, dk, dv = [cat(rows[ends[i]:ends[i + 1]]) for i in range(3)]
            outs, sums = [], []
            for i, (dy, g) in enumerate(((dq, gq), (dk, gk))):
                if rotate:
                    dy = _rot(dy, rc, -rlo, -rhi)
                xi = xv[:, i * width:(i + 1) * width]
                r = lax.rsqrt(_headsum(xi * xi, on) * (1.0 / HEAD_DIM) + EPS)
                xh = xi * r
                dxh = dy * g
                outs.append(r * (dxh - xh * (_headsum(dxh * xh, on) * (1.0 / HEAD_DIM))))
                sums.append(_colsum(dy * xh))
            return [jnp.concatenate(outs + [dv], axis=1)], sums
        return fn

    dqkv_a, dg_qa, dg_ka = _rowmap(qk_bwd(WA, False, (1, len(dka), len(dva))), [qkv_a, dqa] + dka + dva,
                                   [tile_g(g_qa, NA_HEADS), tile_g(g_ka, NA_HEADS), ones],
                                   [(3 * WA, MXU_DTYPE)], [WA, WA], name="qknorm_a_bwd")
    dqkv_b, dg_qb, dg_kb = _rowmap(qk_bwd(WB, True, (3, 3, 3)), [qkv_b, rot_c, rot_lo, rot_hi] + dqb + dkb + dvb,
                                   [tile_g(g_qb, DIL_HEADS), tile_g(g_kb, DIL_HEADS), ones],
                                   [(3 * WB, MXU_DTYPE)], [WB, WB], name="qknorm_b_bwd")

    dw_in_t = jnp.concatenate([
        _mm(dqkv_a, h, ta=True, out_dtype=WIRE_DTYPE, name="wgrad_in_a"),
        _mm(dqkv_b, h, ta=True, out_dtype=WIRE_DTYPE, name="wgrad_in_b"),
        _mm(dgates, h, ta=True, out_dtype=WIRE_DTYPE, name="wgrad_in_gates")], axis=0)
    h_in, tok_in = _split_start([dw_in_t], [land7(dw_in_t)], gather=False, name="exchange_in_start")
    def norm1_bwd(dhv, rows, vecs):
        xv, dx1v = rows
        dxv, sums = norm_bwd(dhv, xv, vecs[0], vecs[1])
        return [dx1v + dxv], sums

    grad_x, dsh1, dsc1, dg1 = _mm_parts_rows(
        [(dqkv_a, w_in_a), (dqkv_b, w_in_b), (dgates, w_in_g)], norm1_bwd, [xt, dx1], [g_norm1 + tok_in, sc1],
        [(d, F32)], [d] * 3, name="dgrad_in_norm1_bwd")

    heads_sum = lambda a, heads: a.reshape(heads, HEAD_DIM).sum(axis=0)
    dmod = jnp.concatenate([dsh1, dsc1, dgt1, dsh2, dsc2, dgt2], axis=1)
    local_small = _pack_small(dict(
        b_ada=dmod, g_norm1=dg1, g_norm2=dg2, b_gate=db_gate, g_qa=heads_sum(dg_qa, NA_HEADS),
        g_ka=heads_sum(dg_ka, NA_HEADS), g_qb=heads_sum(dg_qb, DIL_HEADS), g_kb=heads_sum(dg_kb, DIL_HEADS),
        rpb=g_rpb, loss=(0.5 / d) * jnp.sum(err2)))
    srows = local_small.shape[0]
    small_all = _all_gather([local_small], name="gather_small")[0].reshape(N_DEV, srows, LANES)
    small_sum = _sum8(small_all[0], small_all[1:], name="sum_small")
    small_shapes = dict(b_ada=b_ada.shape, g_norm1=g_norm1.shape, g_norm2=g_norm2.shape, b_gate=b_gate.shape,
                        g_qa=g_qa.shape, g_ka=g_ka.shape, g_qb=g_qb.shape, g_kb=g_kb.shape, rpb=rpb.shape, loss=())
    small_w = dict(b_ada=b_ada, g_norm1=g_norm1, g_norm2=g_norm2, b_gate=b_gate, g_qa=g_qa, g_ka=g_ka, g_qb=g_qb,
                   g_kb=g_kb, rpb=rpb, loss=jnp.zeros((), F32))
    small_m = dict(b_ada=m_b_ada, g_norm1=m_g_norm1, g_norm2=m_g_norm2, b_gate=m_b_gate, g_qa=m_g_qa, g_ka=m_g_ka,
                   g_qb=m_g_qb, g_kb=m_g_kb, rpb=m_rpb, loss=jnp.zeros((), F32))
    small_v = dict(b_ada=v_b_ada, g_norm1=v_g_norm1, g_norm2=v_g_norm2, b_gate=v_b_gate, g_qa=v_g_qa, g_ka=v_g_ka,
                   g_qb=v_g_qb, g_kb=v_g_kb, rpb=v_rpb, loss=jnp.zeros((), F32))
    s_delta, s_m, s_v = _adamw(_pack_small(small_w), small_sum, _pack_small(small_m), _pack_small(small_v), name="adamw_small")
    gs = _unpack_small(small_sum, small_shapes)
    ds_, ms_, vs_ = [_unpack_small(a, small_shapes) for a in (s_delta, s_m, s_v)]

    dmod_all = small_all[:, :6 * d // LANES].reshape(N_DEV, 6 * d)
    dmod_mine = jnp.pad(lax.dynamic_slice(dmod_all, (0, me * ncol), (N_DEV, ncol)), ((0, LANES - N_DEV), (0, 0)))

    def wada_body(c_ref, dm_ref, o_ref):
        chi, cmid, clo = _split3(c_ref[...])
        dhi, dmid, dlo = _split3(dm_ref[...])
        o_ref[...] = (_dot_tn(chi, dhi) + (_dot_tn(chi, dmid) + _dot_tn(cmid, dhi))
                      + (_dot_tn(chi, dlo) + _dot_tn(cmid, dmid) + _dot_tn(clo, dhi)))

    g_w_ada = pl.pallas_call(
        wada_body,
        name="wgrad_ada",
        out_shape=jax.ShapeDtypeStruct((d, ncol), F32),
        compiler_params=pltpu.CompilerParams(vmem_limit_bytes=_vmem(4 * d * ncol * 4)),
    )(c_act, dmod_mine)

    sent, recv = _split_wait(h_in + h_ffn + h_out, small_sum, gather=False, name="exchange_wait")
    names = ("w_in", "w_ffn_in", "w_ffn_out", "w_proj_a", "w_proj_b", "w_o")
    transposed = (True, True, False, True, True, False)
    big_g = {}
    for nme, own, r, tr in zip(names, sent, recv, transposed):
        s = _sum8(own_block(own), r, name=f"sum_{nme}")
        big_g[nme] = s.T if tr else s
    big_g["w_ada"] = g_w_ada
    big_w = dict(w_ada=w_ada, w_in=w_in, w_proj_a=w_proj_a, w_proj_b=w_proj_b, w_o=w_o, w_ffn_in=w_ffn_in, w_ffn_out=w_ffn_out)
    big_m = dict(w_ada=m_w_ada, w_in=m_w_in, w_proj_a=m_w_proj_a, w_proj_b=m_w_proj_b, w_o=m_w_o, w_ffn_in=m_w_ffn_in, w_ffn_out=m_w_ffn_out)
    big_v = dict(w_ada=v_w_ada, w_in=v_w_in, w_proj_a=v_w_proj_a, w_proj_b=v_w_proj_b, w_o=v_w_o, w_ffn_in=v_w_ffn_in, w_ffn_out=v_w_ffn_out)
    grads, deltas, new_m, new_v = {}, {}, {}, {}
    for nme in big_w:
        dl, m1, v1 = _adamw(big_w[nme][0], big_g[nme], big_m[nme][0], big_v[nme][0], name=f"adamw_{nme}")
        grads[nme], deltas[nme], new_m[nme], new_v[nme] = big_g[nme][None], dl[None], m1[None], v1[None]
    for nme in _SMALL[:-1]:
        grads[nme], deltas[nme], new_m[nme], new_v[nme] = gs[nme], ds_[nme], ms_[nme], vs_[nme]

    order = ("w_ada", "b_ada", "g_norm1", "g_norm2", "w_in", "b_gate", "g_qa", "g_ka", "g_qb", "g_kb", "rpb",
             "w_proj_a", "w_proj_b", "w_o", "w_ffn_in", "w_ffn_out")
    return (gs["loss"], grad_x[None], *[grads[n] for n in order], *[deltas[n] for n in order],
            *[new_m[n] for n in order], *[new_v[n] for n in order])
```

```python
import functools

import numpy as np
import jax
import jax.numpy as jnp
from jax import lax
from jax.experimental import pallas as pl
from jax.experimental.pallas import tpu as pltpu

F32 = jnp.float32
MXU_DTYPE = jnp.bfloat16
WIRE_DTYPE = jnp.bfloat16
ACT_DTYPE = jnp.bfloat16

HEAD_DIM = 64
GRID_W = 64
NA_HEADS = 8
NA_KH = 8
NA_KW = 16
DIL_CONFIGS = ((128, 1), (512, 4), (2048, 16))
DIL_HEADS_PER_GROUP = 4
DIL_HEADS = DIL_HEADS_PER_GROUP * len(DIL_CONFIGS)
DIL_HALF = 64
ROT_DIM = HEAD_DIM // 4
ROPE_THETA = 500000.0
EPS = 1e-6
NEG_INF = -1e30
WA = NA_HEADS * HEAD_DIM
WB = DIL_HEADS * HEAD_DIM
WB_OUT = DIL_HEADS_PER_GROUP * HEAD_DIM
ADAM_LR = 0.001
ADAM_B1 = 0.9
ADAM_B2 = 0.999
ADAM_EPS = 1e-08
ADAM_WD = 0.01
ADAM_STEP = 10

N_DEV = 8
LANES = 128
VMEM_CAP = 60 * 2**20
VMEM_FLOOR = 56 * 2**20
MESH = pl.DeviceIdType.MESH
ANY = pl.BlockSpec(memory_space=pl.ANY)


def _vmem(nbytes):
    return int(min(VMEM_CAP, max(VMEM_FLOOR, nbytes * 5 // 4 + 4 * 2**20)))


def _pick(dim, cands):
    for c in cands:
        if c <= dim and dim % c == 0:
            return c
    return dim


def _nbytes(shape, dtype):
    return int(np.prod(shape)) * jnp.dtype(dtype).itemsize


def _dot(a, b, dims=((1,), (0,))):
    return lax.dot_general(a.astype(MXU_DTYPE), b.astype(MXU_DTYPE), (dims, ((), ())), preferred_element_type=F32)


def _dot_nt(a, b):
    return _dot(a, b, ((1,), (1,)))


def _dot_tn(a, b):
    return _dot(a, b, ((0,), (0,)))


def _split3(a):
    hi = a.astype(jnp.bfloat16)
    r1 = a - hi.astype(F32)
    mid = r1.astype(jnp.bfloat16)
    lo = (r1 - mid.astype(F32)).astype(jnp.bfloat16)
    return hi, mid, lo


def _silu(x):
    return x * jax.nn.sigmoid(x)


def _divisors(dim, unit):
    return [c for c in range(unit, dim + 1, unit) if dim % c == 0] or [dim]


def _mm_tiles(m, n, kdim, a_item, b_item, o_item):
    step_us, hbm_bytes_per_us, flops_per_us, budget = 0.35, 3.0e6, 8.0e8, 40 * 2**20
    best = None
    for tm in _divisors(m, LANES):
        for tn in _divisors(n, LANES):
            for tk in _divisors(kdim, LANES):
                gm, gn, gk = m // tm, n // tn, kdim // tk
                vmem = 2 * (tm * tk * a_item + tk * tn * b_item + tm * tn * o_item) + 2 * (tm * tk + tk * tn)
                vmem += tm * tn * 4 * ((1 if gk > 1 else 0) + 1)
                if vmem > budget:
                    continue
                a_reads = m * kdim * a_item * (gn if gk > 1 else 1)
                traffic = a_reads + kdim * n * b_item * gm + m * n * o_item
                cost = gm * gn * gk * step_us + max(traffic / hbm_bytes_per_us, 2.0 * m * n * kdim / flops_per_us)
                if best is None or cost < best[0]:
                    best = (cost, tm, tn, tk)
    return best[1:]


def _mm(a, b, *, name, ta=False, tb=False, out_dtype=F32):
    if ta:
        kdim, m = a.shape
    else:
        m, kdim = a.shape
    n = b.shape[0] if tb else b.shape[1]
    assert b.shape[1 if tb else 0] == kdim
    tm, tn, tk = _mm_tiles(m, n, kdim, a.dtype.itemsize, b.dtype.itemsize, jnp.dtype(out_dtype).itemsize)
    gm, gn, gk = m // tm, n // tn, kdim // tk

    a_spec = pl.BlockSpec((tk, tm), lambda i, j, k: (k, i)) if ta else pl.BlockSpec((tm, tk), lambda i, j, k: (i, k))
    b_spec = pl.BlockSpec((tn, tk), lambda i, j, k: (j, k)) if tb else pl.BlockSpec((tk, tn), lambda i, j, k: (k, j))
    o_spec = pl.BlockSpec((tm, tn), lambda i, j, k: (i, j))
    a_dims = (0,) if ta else (1,)
    b_dims = (1,) if tb else (0,)

    def body(a_ref, b_ref, o_ref, *scratch):
        if gk == 1:
            o_ref[...] = _dot(a_ref[...], b_ref[...], (a_dims, b_dims)).astype(o_ref.dtype)
            return
        (acc_ref,) = scratch
        k = pl.program_id(2)

        @pl.when(k == 0)
        def _():
            acc_ref[...] = jnp.zeros_like(acc_ref)

        acc_ref[...] += _dot(a_ref[...], b_ref[...], (a_dims, b_dims))

        @pl.when(k == gk - 1)
        def _():
            o_ref[...] = acc_ref[...].astype(o_ref.dtype)

    est = 2 * (tm * tk * a.dtype.itemsize + tk * tn * b.dtype.itemsize + tm * tn * jnp.dtype(out_dtype).itemsize)
    est += tm * tn * 4 + 2 * (tm * tk + tk * tn) * 2
    return pl.pallas_call(
        body,
        name=name,
        grid=(gm, gn, gk),
        in_specs=[a_spec, b_spec],
        out_specs=o_spec,
        out_shape=jax.ShapeDtypeStruct((m, n), out_dtype),
        scratch_shapes=[pltpu.VMEM((tm, tn), F32)] if gk > 1 else [],
        compiler_params=pltpu.CompilerParams(
            dimension_semantics=("parallel", "parallel", "arbitrary"), vmem_limit_bytes=_vmem(est)
        ),
    )(a, b)


def _resident(shape):
    return pl.BlockSpec(shape, lambda i: (0,) * len(shape), pipeline_mode=pl.Buffered(1))


def _row_tile(m, fixed_bytes, bytes_per_row, budget=50 * 2**20):
    fits = [tm for tm in _divisors(m, LANES) if fixed_bytes + tm * bytes_per_row <= budget]
    return max(fits) if fits else _divisors(m, LANES)[0]


def _mm_parts_rows(parts, fn, rows, vecs, outs, reds, *, name):
    parts = [(p[0], p[1], len(p) > 2) for p in parts]
    m, n = parts[0][0].shape[0], parts[0][1].shape[0 if parts[0][2] else 1]
    npart, nr, nv, no = len(parts), len(rows), len(vecs), len(outs)
    row_bytes = sum(r.shape[1] * r.dtype.itemsize for r in rows) + sum(w * jnp.dtype(dt).itemsize for (w, dt) in outs)
    a_row_bytes = sum(a.shape[1] * a.dtype.itemsize for a, _, _ in parts)
    fixed = sum(_nbytes(b.shape, b.dtype) for _, b, _ in parts)
    per_row = 2 * (a_row_bytes + row_bytes) + n * 4 * 5
    tm = _row_tile(m, fixed, per_row)
    sub = min(tm, 2 * LANES)

    def body(*refs):
        ab = refs[:2 * npart]
        row_refs, vec_refs = refs[2 * npart:2 * npart + nr], refs[2 * npart + nr:2 * npart + nr + nv]
        out_refs = refs[2 * npart + nr + nv:2 * npart + nr + nv + no]
        red_refs = refs[2 * npart + nr + nv + no:]
        if red_refs:
            @pl.when(pl.program_id(0) == 0)
            def _():
                for ref in red_refs:
                    ref[...] = jnp.zeros_like(ref)

        vecs_v = [v[...] for v in vec_refs]
        for s0 in range(0, tm, sub):
            sl = slice(s0, s0 + sub)
            r = None
            for p, (_, _, nt) in enumerate(parts):
                term = (_dot_nt if nt else _dot)(ab[2 * p][sl, :], ab[2 * p + 1][...])
                r = term if r is None else r + term
            o, rd = fn(r, [x[sl, :].astype(F32) for x in row_refs], vecs_v)
            for ref, val in zip(out_refs, o):
                ref[sl, :] = val.astype(ref.dtype)
            for ref, val in zip(red_refs, rd):
                ref[...] += val

    in_specs, operands = [], []
    for a, b, _ in parts:
        in_specs += [pl.BlockSpec((tm, a.shape[1]), lambda i: (i, 0)), _resident(b.shape)]
        operands += [a, b]
    in_specs += [pl.BlockSpec((tm, r.shape[1]), lambda i: (i, 0)) for r in rows]
    in_specs += [pl.BlockSpec(v.shape, functools.partial(lambda nd, i: (0,) * nd, v.ndim)) for v in vecs]
    out_specs = [pl.BlockSpec((tm, w), lambda i: (i, 0)) for (w, _) in outs]
    out_specs += [pl.BlockSpec((1, w), lambda i: (0, 0)) for w in reds]
    out_shape = [jax.ShapeDtypeStruct((m, w), dt) for (w, dt) in outs] + [jax.ShapeDtypeStruct((1, w), F32) for w in reds]
    return pl.pallas_call(
        body,
        name=name,
        grid=(m // tm,),
        in_specs=in_specs,
        out_specs=out_specs,
        out_shape=out_shape,
        compiler_params=pltpu.CompilerParams(dimension_semantics=("arbitrary",), vmem_limit_bytes=_vmem(fixed + tm * per_row)),
    )(*operands, *rows, *vecs)


def _mm_ew(a, bs, fn, rows, outs, *, name):
    m, kdim = a.shape
    n = bs[0].shape[0]
    nb, nr, no = len(bs), len(rows), len(outs)
    cw = _pick(n, (2 * LANES, LANES))
    fixed = nb * n * kdim * bs[0].dtype.itemsize
    per_row = 2 * (kdim * a.dtype.itemsize + n * (sum(r.dtype.itemsize for r in rows) + sum(jnp.dtype(dt).itemsize for dt in outs)))
    per_row += cw * 4 * 4 * (nb + 4)
    tm = _row_tile(m, fixed, per_row)

    def body(*refs):
        a_ref, b_refs = refs[0], refs[1:1 + nb]
        row_refs, out_refs = refs[1 + nb:1 + nb + nr], refs[1 + nb + nr:]
        av = a_ref[...]
        for c0 in range(0, n, cw):
            cols = slice(c0, c0 + cw)
            o = fn([_dot_nt(av, b[cols, :]) for b in b_refs], [x[:, cols].astype(F32) for x in row_refs])
            for ref, val in zip(out_refs, o):
                ref[:, cols] = val.astype(ref.dtype)

    tile = pl.BlockSpec((tm, n), lambda i: (i, 0))
    return pl.pallas_call(
        body,
        name=name,
        grid=(m // tm,),
        in_specs=[pl.BlockSpec((tm, kdim), lambda i: (i, 0))] + [_resident((n, kdim))] * nb + [tile] * nr,
        out_specs=[tile] * no,
        out_shape=[jax.ShapeDtypeStruct((m, n), dt) for dt in outs],
        compiler_params=pltpu.CompilerParams(dimension_semantics=("parallel",), vmem_limit_bytes=_vmem(fixed + tm * per_row)),
    )(a, *bs, *rows)


def _rowmap(fn, rows, vecs, outs, reds, *, name, tm=None):
    norm = []
    for r in rows:
        if not isinstance(r, tuple):
            norm.append((r, r.shape[1], 0, None))
        elif len(r) == 2:
            norm.append((r[0], r[0].shape[2], 0, r[1]))
        else:
            norm.append((r[0], r[1], r[2], None))
    rows = norm
    t = rows[0][0].shape[-2]
    if tm is None:
        per_row = 2 * sum(w * a.dtype.itemsize for (a, w, _, _) in rows) + 2 * sum(w * jnp.dtype(d).itemsize for (w, d) in outs)
        per_row += 3 * 4 * max([w for (_, w, _, _) in rows] + [w for (w, _) in outs])
        tm = max(8, min(1024, (40 * 2**20) // per_row))
    tm = _pick(t, tuple(c for c in (1024, 512, 256, 128, 64, 32, 16, 8) if c <= tm))
    nr, nv, no = len(rows), len(vecs), len(outs)

    def body(*refs):
        row_refs, vec_refs = refs[:nr], refs[nr:nr + nv]
        out_refs, red_refs = refs[nr + nv:nr + nv + no], refs[nr + nv + no:]
        o, rd = fn([r[...].astype(F32) for r in row_refs], [v[...] for v in vec_refs])
        for ref, val in zip(out_refs, o):
            ref[...] = val.astype(ref.dtype)
        if red_refs:
            @pl.when(pl.program_id(0) == 0)
            def _():
                for ref in red_refs:
                    ref[...] = jnp.zeros_like(ref)

            for ref, val in zip(red_refs, rd):
                ref[...] += val

    in_specs = [pl.BlockSpec((tm, w), functools.partial(lambda cb, i: (i, cb), cb)) if lead is None
                else pl.BlockSpec((None, tm, w), functools.partial(lambda ld, i: (ld, i, 0), lead)) for (_, w, cb, lead) in rows]
    in_specs += [pl.BlockSpec(v.shape, functools.partial(lambda nd, i: (0,) * nd, v.ndim)) for v in vecs]
    out_specs = [pl.BlockSpec((tm, w), lambda i: (i, 0)) for (w, _) in outs]
    out_specs += [pl.BlockSpec((1, w), lambda i: (0, 0)) for w in reds]
    out_shape = [jax.ShapeDtypeStruct((t, w), d) for (w, d) in outs]
    out_shape += [jax.ShapeDtypeStruct((1, w), F32) for w in reds]
    est = 2 * sum(tm * w * a.dtype.itemsize for (a, w, _, _) in rows) + 2 * sum(_nbytes(v.shape, v.dtype) for v in vecs)
    est += 2 * sum(tm * w * jnp.dtype(d).itemsize for (w, d) in outs)
    est += 6 * tm * max([w for (_, w, _, _) in rows] + [w for (w, _) in outs]) * 4
    return pl.pallas_call(
        body,
        name=name,
        grid=(t // tm,),
        in_specs=in_specs,
        out_specs=out_specs,
        out_shape=out_shape,
        compiler_params=pltpu.CompilerParams(dimension_semantics=("arbitrary",), vmem_limit_bytes=_vmem(est)),
    )(*[r[0] for r in rows], *vecs)


def _colsum(v):
    return jnp.sum(v, axis=0, keepdims=True)


def _head_ones():
    i = np.arange(LANES)
    return jnp.asarray((i[:, None] // HEAD_DIM) == (i[None, :] // HEAD_DIM), MXU_DTYPE)


def _headsum(y, ones):
    parts = []
    for j in range(y.shape[1] // LANES):
        c = y[:, j * LANES:(j + 1) * LANES]
        hi = c.astype(MXU_DTYPE)
        lo = c - hi.astype(F32)
        parts.append(_dot(hi, ones) + _dot(lo, ones))
    return parts[0] if len(parts) == 1 else jnp.concatenate(parts, axis=1)


def _rot(y, c, s_lo, s_hi):
    parts = []
    for j in range(y.shape[1] // LANES):
        yc = y[:, j * LANES:(j + 1) * LANES]
        parts.append(yc * c + pltpu.roll(yc, LANES - ROT_DIM // 2, 1) * s_lo + pltpu.roll(yc, ROT_DIM // 2, 1) * s_hi)
    return parts[0] if len(parts) == 1 else jnp.concatenate(parts, axis=1)


def _rot_tables(t):
    half = ROT_DIM // 2
    inv_freq = ROPE_THETA ** (-(jnp.arange(half, dtype=F32) * 2.0) / ROT_DIM)
    ang = jnp.arange(t).astype(F32)[:, None] * inv_freq[None, :]
    cos, sin = jnp.cos(ang), jnp.sin(ang)
    z = lambda w: jnp.zeros((t, w), F32)
    c = jnp.concatenate([cos, cos, jnp.ones((t, HEAD_DIM - ROT_DIM), F32)], axis=1)
    s_lo = jnp.concatenate([-sin, z(HEAD_DIM - half)], axis=1)
    s_hi = jnp.concatenate([z(half), sin, z(HEAD_DIM - ROT_DIM)], axis=1)
    return [jnp.tile(a, (1, LANES // HEAD_DIM)) for a in (c, s_lo, s_hi)]


def _rms(x):
    return lax.rsqrt(jnp.mean(x * x, axis=-1, keepdims=True) + EPS)


def _window(kind, n, bq, t, seg):
    if kind == "na":
        rows = t // GRID_W
        rs = jnp.clip(n - NA_KH // 2, 0, rows - NA_KH)
        return rs
    nk = bq + 2 * DIL_HALF
    return jnp.clip(n * bq - DIL_HALF, 0, t - nk)


def _dil_mask(n, bq, nk, ws, seg):
    qi = n * bq + lax.broadcasted_iota(jnp.int32, (bq, nk), 0)
    ki = ws + lax.broadcasted_iota(jnp.int32, (bq, nk), 1)
    shift = int(np.log2(seg))
    return (jnp.abs(ki - qi) <= DIL_HALF) & ((ki >> shift) == (qi >> shift))


HS = 4
QW = HS * HEAD_DIM


def _head_of_lane(width=QW):
    return lax.broadcasted_iota(jnp.int32, (1, width), 1) // HEAD_DIM


def _stack_heads(a):
    head = _head_of_lane()
    return jnp.concatenate([jnp.where(head == e, a, jnp.zeros_like(a)) for e in range(HS)], axis=0)


def _unstack_heads(a, bq):
    head = _head_of_lane()
    out = jnp.zeros((bq, QW), a.dtype)
    for e in range(HS):
        out = jnp.where(head == e, a[e * bq:(e + 1) * bq], out)
    return out


def _stack_cols(blk, bq):
    head = _head_of_lane()
    return jnp.concatenate(
        [jnp.max(jnp.where(head == e, blk, -jnp.inf), axis=1, keepdims=True) for e in range(HS)], axis=0)


def _attn_geometry(kind):
    if kind == "na":
        return GRID_W, NA_KH * GRID_W, 8
    bq = 128
    return bq, bq + 2 * DIL_HALF, 4


def _attn_scores(kind, n, bq, nk, t, seg, qs, k_ref, b_ref):
    scale = HEAD_DIM ** -0.5
    if kind == "na":
        rs = _window(kind, n, bq, t, seg)
        ws = pl.multiple_of(rs * GRID_W, GRID_W)
        ro0 = rs - n + (NA_KH - 1)
        s = _dot_nt(qs, k_ref[pl.ds(ws, nk), :]) * scale
        s = s + jnp.concatenate(
            [jnp.concatenate([b_ref[e, ro0 + 2 * i] for i in range(NA_KH // 2)], axis=1) for e in range(HS)], axis=0)
        return s, ws, ro0
    ws = pl.multiple_of(_window(kind, n, bq, t, seg), DIL_HALF)
    mask = _dil_mask(n, bq, nk, ws, seg)
    s = _dot_nt(qs, k_ref[pl.ds(ws, nk), :]) * scale
    s = jnp.where(jnp.concatenate([mask] * HS, axis=0), s, NEG_INF)
    return s, ws, None


def _attn_fwd(q, k, v, *, kind, name, bias=None, seg=None):
    t, w = q.shape
    quads = w // QW
    bq, nk, sub = _attn_geometry(kind)
    nq = t // (bq * sub)

    def body(*refs):
        if kind == "na":
            q_ref, k_ref, v_ref, b_ref, o_ref, l_ref = refs
        else:
            (q_ref, k_ref, v_ref, o_ref, l_ref), b_ref = refs, None
        for i in range(sub):
            n = pl.program_id(1) * sub + i
            rows = slice(i * bq, (i + 1) * bq)
            s, ws, _ = _attn_scores(kind, n, bq, nk, t, seg, _stack_heads(q_ref[rows, :]), k_ref, b_ref)
            m = jnp.max(s, axis=1, keepdims=True)
            p = jnp.exp(s - m)
            l = jnp.sum(p, axis=1, keepdims=True)
            o_ref[rows, :] = _unstack_heads(_dot(p / l, v_ref[pl.ds(ws, nk), :]), bq)
            l_ref[rows, :] = _unstack_heads(jnp.broadcast_to(m + jnp.log(l), (HS * bq, QW)), bq)

    blk = pl.BlockSpec((bq * sub, QW), lambda j, n: (n, j))
    res = pl.BlockSpec((t, QW), lambda j, n: (0, j))
    in_specs = [blk, res, res]
    operands = [q, k, v]
    est = 4 * t * QW * q.dtype.itemsize + 12 * sub * HS * bq * nk * 4
    if kind == "na":
        in_specs.append(pl.BlockSpec((HS,) + bias.shape[1:], lambda j, n: (j, 0, 0, 0)))
        operands.append(bias)
        est += 2 * _nbytes((HS,) + bias.shape[1:], F32)
    return pl.pallas_call(
        body,
        name=name,
        grid=(quads, nq),
        in_specs=in_specs,
        out_specs=[blk, blk],
        out_shape=[jax.ShapeDtypeStruct((t, w), F32)] * 2,
        compiler_params=pltpu.CompilerParams(dimension_semantics=("arbitrary", "arbitrary"), vmem_limit_bytes=_vmem(est)),
    )(*operands)


def _attn_bwd(q, k, v, do, dterm, lse, *, kind, name, bias=None, seg=None):
    t, w = q.shape
    quads = w // QW
    bq, nk, sub = _attn_geometry(kind)
    nq = t // (bq * sub)
    scale = HEAD_DIM ** -0.5

    def body(*refs):
        if kind == "na":
            q_ref, k_ref, v_ref, do_ref, dt_ref, l_ref, b_ref, dq_ref, dk_hbm, dv_hbm, db_ref, dk_acc, dv_acc, sem = refs
        else:
            q_ref, k_ref, v_ref, do_ref, dt_ref, l_ref, dq_ref, dk_hbm, dv_hbm, dk_acc, dv_acc, sem = refs
            b_ref = None
        j, step = pl.program_id(0), pl.program_id(1)

        @pl.when(step == 0)
        def _():
            dk_acc[...] = jnp.zeros_like(dk_acc)
            dv_acc[...] = jnp.zeros_like(dv_acc)
            if kind == "na":
                db_ref[...] = jnp.zeros_like(db_ref)

        for b in range(sub):
            n = step * sub + b
            rows = slice(b * bq, (b + 1) * bq)
            qs = _stack_heads(q_ref[rows, :])
            dos = _stack_heads(do_ref[rows, :])
            s, ws, ro0 = _attn_scores(kind, n, bq, nk, t, seg, qs, k_ref, b_ref)
            p = jnp.exp(s - _stack_cols(l_ref[rows, :], bq))
            dp = _dot_nt(dos, v_ref[pl.ds(ws, nk), :])
            ds = p * (dp - _stack_cols(dt_ref[rows, :], bq))
            if kind == "na":
                for e in range(HS):
                    for i in range(NA_KH // 2):
                        db_ref[e, ro0 + 2 * i] += ds[e * bq:(e + 1) * bq, i * LANES:(i + 1) * LANES]
            dsc = ds * scale
            dq_ref[rows, :] = _unstack_heads(_dot(dsc, k_ref[pl.ds(ws, nk), :]), bq)
            dk_acc[pl.ds(ws, nk), :] += _dot_tn(dsc, qs)
            dv_acc[pl.ds(ws, nk), :] += _dot_tn(p, dos)

        @pl.when(step == nq - 1)
        def _():
            ck = pltpu.make_async_copy(dk_acc, dk_hbm.at[j], sem.at[0])
            cv = pltpu.make_async_copy(dv_acc, dv_hbm.at[j], sem.at[1])
            ck.start()
            cv.start()
            ck.wait()
            cv.wait()

    blk = pl.BlockSpec((bq * sub, QW), lambda j, n: (n, j))
    res = pl.BlockSpec((t, QW), lambda j, n: (0, j))
    in_specs = [blk, res, res, blk, blk, blk]
    operands = [q, k, v, do, dterm, lse]
    out_specs = [blk, ANY, ANY]
    out_shape = [jax.ShapeDtypeStruct((t, w), F32)] + [jax.ShapeDtypeStruct((quads, t, QW), F32)] * 2
    est = 4 * t * QW * q.dtype.itemsize + 2 * t * QW * 4 + 16 * sub * HS * bq * nk * 4
    if kind == "na":
        bspec = pl.BlockSpec((HS,) + bias.shape[1:], lambda j, n: (j, 0, 0, 0))
        in_specs.append(bspec)
        operands.append(bias)
        out_specs.append(bspec)
        out_shape.append(jax.ShapeDtypeStruct(bias.shape, F32))
        est += 4 * _nbytes((HS,) + bias.shape[1:], F32)
    res_ = pl.pallas_call(
        body,
        name=name,
        grid=(quads, nq),
        in_specs=in_specs,
        out_specs=out_specs,
        out_shape=out_shape,
        scratch_shapes=[pltpu.VMEM((t, QW), F32), pltpu.VMEM((t, QW), F32), pltpu.SemaphoreType.DMA((2,))],
        compiler_params=pltpu.CompilerParams(dimension_semantics=("arbitrary", "arbitrary"), vmem_limit_bytes=_vmem(est)),
    )(*operands)
    unquad = lambda a: [(a, i) for i in range(quads)]
    return (res_[0], unquad(res_[1]), unquad(res_[2])) + tuple(res_[3:])


def _na_onehot():
    qc = np.arange(GRID_W)[:, None]
    kc = np.arange(GRID_W)[None, :]
    start = np.clip(qc - NA_KW // 2, 0, GRID_W - NA_KW)
    inwin = (kc >= start) & (kc < start + NA_KW)
    off = kc - qc + (NA_KW - 1)
    e_mat = np.zeros((2, 32, GRID_W, 2, GRID_W), np.float32)
    for e in range(2):
        for c in range(2 * NA_KW - 1):
            e_mat[e, c, :, e, :] = (off == c) & inwin
    neg = np.where(inwin, 0.0, NEG_INF).astype(np.float32)
    neg = np.broadcast_to(neg[:, None, :], (GRID_W, 2, GRID_W)).reshape(1, GRID_W * LANES)
    return jnp.asarray(e_mat.reshape(64, GRID_W * LANES), MXU_DTYPE), jnp.asarray(neg)


def _na_rowpairs(rpb):
    p = jnp.pad(rpb, ((0, 0), (0, 0), (0, 1)))
    return jnp.concatenate([p[:, :-1], p[:, 1:]], axis=-1).reshape(NA_HEADS * (2 * NA_KH - 2), 64)


def _na_bias_table(rpb):
    r2 = _na_rowpairs(rpb)
    e_mat, neg = _na_onehot()

    def body(r_ref, e_ref, n_ref, o_ref):
        hi, mid, lo = _split3(r_ref[...])
        e = e_ref[...]
        o_ref[...] = _dot(hi, e) + _dot(mid, e) + _dot(lo, e) + n_ref[...]

    out = pl.pallas_call(
        body,
        name="na_bias_table",
        out_shape=jax.ShapeDtypeStruct((r2.shape[0], GRID_W * LANES), F32),
        compiler_params=pltpu.CompilerParams(vmem_limit_bytes=_vmem(6 * r2.shape[0] * GRID_W * LANES * 4)),
    )(r2, e_mat, neg)
    return out.reshape(NA_HEADS, 2 * NA_KH - 2, GRID_W, LANES)


def _na_bias_grad(dbt):
    e_mat, _ = _na_onehot()
    flat = dbt.reshape(NA_HEADS * (2 * NA_KH - 2), GRID_W * LANES)

    def body(d_ref, e_ref, o_ref):
        hi, mid, lo = _split3(d_ref[...])
        e = e_ref[...]
        o_ref[...] = _dot_nt(hi, e) + _dot_nt(mid, e) + _dot_nt(lo, e)

    g = pl.pallas_call(
        body,
        name="na_bias_grad",
        out_shape=jax.ShapeDtypeStruct((flat.shape[0], 64), F32),
        compiler_params=pltpu.CompilerParams(vmem_limit_bytes=_vmem(6 * flat.shape[0] * GRID_W * LANES * 4)),
    )(flat, e_mat)
    g = g.reshape(NA_HEADS, 2 * NA_KH - 2, 2, 32)[..., :2 * NA_KW - 1]
    first = jnp.pad(g[:, :, 0], ((0, 0), (0, 1), (0, 0)))
    second = jnp.pad(g[:, :, 1], ((0, 0), (1, 0), (0, 0)))
    return first + second


def _all_gather(arrs, *, name):
    na = len(arrs)

    def body(*refs):
        ins, outs = refs[:na], refs[na:2 * na]
        send_sems, recv_sems, local_sems = refs[2 * na:]
        x, y, c = lax.axis_index("x"), lax.axis_index("y"), lax.axis_index("c")
        me, sibling = (x, y, c), (x, y, 1 - c)
        chips = [(1 - x, y), (x, 1 - y), (1 - x, 1 - y)]

        def rows(a, px, py, pc):
            r = ins[a].shape[0]
            return outs[a].at[pl.ds((4 * px + 2 * py + pc) * r, r), :]

        def copy(a, k, block, to, src=None):
            return pltpu.make_async_remote_copy(
                src_ref=rows(a, *block) if src is None else src, dst_ref=rows(a, *block),
                send_sem=send_sems.at[a, k], recv_sem=recv_sems.at[a, k], device_id=to, device_id_type=MESH)

        mine = [pltpu.make_async_copy(ins[a], rows(a, *me), local_sems.at[a]) for a in range(na)]
        for cp in mine:
            cp.start()
        first = []
        for a in range(na):
            first.append(copy(a, 0, me, sibling, src=ins[a]))
            first += [copy(a, 1 + j, me, (*chip, c), src=ins[a]) for j, chip in enumerate(chips)]
        for cp in first:
            cp.start()
        passed = []
        for j, chip in enumerate(chips):
            for a in range(na):
                copy(a, 1 + j, (*chip, c), me).wait_recv()
                cp = copy(a, 4 + j, (*chip, c), sibling)
                cp.start()
                passed.append(cp)
        for a in range(na):
            copy(a, 0, sibling, me).wait_recv()
        for j, chip in enumerate(chips):
            for a in range(na):
                copy(a, 4 + j, (*chip, 1 - c), me).wait_recv()
        for cp in first + passed:
            cp.wait_send()
        for cp in mine:
            cp.wait()

    return pl.pallas_call(
        body,
        name=name,
        in_specs=[ANY] * na,
        out_specs=[ANY] * na,
        out_shape=[jax.ShapeDtypeStruct((N_DEV * a.shape[0], a.shape[1]), a.dtype) for a in arrs],
        scratch_shapes=[pltpu.SemaphoreType.DMA((na, 7)), pltpu.SemaphoreType.DMA((na, 7)), pltpu.SemaphoreType.DMA((na,))],
    )(*arrs)


HBM = pl.BlockSpec(memory_space=pltpu.HBM)
SEM = pl.BlockSpec(memory_space=pltpu.SEMAPHORE)
EFFECT = pltpu.SideEffectType.DATAFLOW_SIDE_EFFECTING


def _peer_of(k):
    x, y, c = lax.axis_index("x"), lax.axis_index("y"), lax.axis_index("c")
    return x ^ ((k >> 2) & 1), y ^ ((k >> 1) & 1), c ^ (k & 1)


def _split_copies(gather, src_ref, land_ref, send_sems, recv_sems):
    x, y, c = lax.axis_index("x"), lax.axis_index("y"), lax.axis_index("c")
    my = 4 * x + 2 * y + c
    r = src_ref.shape[0] if gather else src_ref.shape[0] // N_DEV
    copies = []
    for k in range(1, N_DEV):
        px, py, pc = _peer_of(k)
        if gather:
            src, dst = src_ref, land_ref.at[pl.ds(my * r, r), :]
        else:
            src, dst = src_ref.at[pl.ds((4 * px + 2 * py + pc) * r, r), :], land_ref.at[k - 1]
        copies.append(pltpu.make_async_remote_copy(
            src_ref=src, dst_ref=dst, send_sem=send_sems.at[k - 1], recv_sem=recv_sems.at[k - 1],
            device_id=(px, py, pc), device_id_type=MESH))
    return copies


def _split_start(srcs, lands, *, gather, name, after=None):
    na = len(srcs)
    extra = [] if after is None else [after]

    def body(*refs):
        src_refs, land_refs = refs[:na], refs[na:2 * na]
        outs = refs[2 * na + len(extra):]
        for a in range(na):
            for cp in _split_copies(gather, src_refs[a], land_refs[a], outs[4 * a], outs[4 * a + 1]):
                cp.start()
        outs[4 * na][...] = jnp.zeros_like(outs[4 * na])

    out_shape, out_specs, aliases = [], [], {}
    for a in range(na):
        out_shape += [pltpu.SemaphoreType.DMA((N_DEV - 1,)), pltpu.SemaphoreType.DMA((N_DEV - 1,)),
                      pltpu.HBM(srcs[a].shape, srcs[a].dtype), pltpu.HBM(lands[a].shape, lands[a].dtype)]
        out_specs += [SEM, SEM, HBM, HBM]
        aliases[a] = 4 * a + 2
        aliases[na + a] = 4 * a + 3
    out_shape.append(jax.ShapeDtypeStruct((8, LANES), F32))
    out_specs.append(pl.BlockSpec(memory_space=pltpu.VMEM))
    res = pl.pallas_call(
        body,
        name=name,
        out_shape=tuple(out_shape),
        in_specs=[HBM] * (2 * na) + [ANY] * len(extra),
        out_specs=tuple(out_specs),
        input_output_aliases=aliases,
        compiler_params=pltpu.CompilerParams(has_side_effects=EFFECT),
    )(*[pltpu.with_memory_space_constraint(a, pltpu.HBM) for a in list(srcs) + list(lands)], *extra)
    return [tuple(res[4 * a:4 * a + 4]) for a in range(na)], res[4 * na][0, 0]


def _split_wait(handles, after, *, gather, name):
    na = len(handles)

    def body(*refs):
        src_refs, land_refs = refs[:na], refs[na:2 * na]
        sems = refs[2 * na:4 * na]
        for a in range(na):
            for cp in _split_copies(gather, src_refs[a], land_refs[a], sems[2 * a], sems[2 * a + 1]):
                cp.wait_send()
                cp.wait_recv()

    srcs = [h[2] for h in handles]
    lands = [h[3] for h in handles]
    sems = [s for h in handles for s in h[:2]]
    res = pl.pallas_call(
        body,
        name=name,
        out_shape=tuple(pltpu.HBM(a.shape, a.dtype) for a in srcs + lands),
        in_specs=[HBM] * (2 * na) + [SEM] * (2 * na) + [ANY],
        out_specs=tuple([HBM] * (2 * na)),
        input_output_aliases={i: i for i in range(2 * na)},
        compiler_params=pltpu.CompilerParams(has_side_effects=EFFECT),
    )(*srcs, *lands, *sems, after)
    return list(res[:na]), list(res[na:])


def _sum8(own, recv, *, name):
    _, r, w = recv.shape
    tr = _pick(r, (256, 128, 64, 32, 16, 8))

    def body(own_ref, a_ref, o_ref):
        acc = own_ref[...].astype(F32)
        for i in range(N_DEV - 1):
            acc = acc + a_ref[i].astype(F32)
        o_ref[...] = acc

    return pl.pallas_call(
        body,
        name=name,
        grid=(r // tr,),
        in_specs=[pl.BlockSpec((tr, w), lambda i: (i, 0)), pl.BlockSpec((N_DEV - 1, tr, w), lambda i: (0, i, 0))],
        out_specs=pl.BlockSpec((tr, w), lambda i: (i, 0)),
        out_shape=jax.ShapeDtypeStruct((r, w), F32),
        compiler_params=pltpu.CompilerParams(dimension_semantics=("parallel",), vmem_limit_bytes=_vmem(4 * N_DEV * tr * w * 4)),
    )(own, recv)


def _adamw(w, g, m, v, *, name):
    def fn(rows, _):
        wv, gv, mv, vv = rows
        m1 = ADAM_B1 * mv + (1.0 - ADAM_B1) * gv
        v1 = ADAM_B2 * vv + (1.0 - ADAM_B2) * jnp.square(gv)
        m_hat = m1 / (1.0 - ADAM_B1 ** ADAM_STEP)
        v_hat = v1 / (1.0 - ADAM_B2 ** ADAM_STEP)
        delta = -ADAM_LR * (m_hat / (jnp.sqrt(v_hat) + ADAM_EPS) + ADAM_WD * wv)
        return [delta, m1, v1], []

    c = w.shape[1]
    return _rowmap(fn, [w, g, m, v], [], [(c, F32)] * 3, [], name=name)


_SMALL = ("b_ada", "g_norm1", "g_norm2", "b_gate", "g_qa", "g_ka", "g_qb", "g_kb", "rpb", "loss")


def _pack_small(parts):
    flat = []
    for nme in _SMALL:
        a = parts[nme].reshape(-1).astype(F32)
        flat.append(jnp.pad(a, (0, (-a.shape[0]) % LANES)))
    flat = jnp.concatenate(flat)
    flat = jnp.pad(flat, (0, (-flat.shape[0]) % (LANES * LANES)))
    return flat.reshape(-1, LANES)


def _unpack_small(packed, shapes):
    flat = packed.reshape(-1)
    out, pos = {}, 0
    for nme in _SMALL:
        n = int(np.prod(shapes[nme]))
        out[nme] = flat[pos:pos + n].reshape(shapes[nme])
        pos += n + (-n) % LANES
    return out


def _to_class(a, d):
    t, w = a.shape
    return a if d == 1 else a.reshape(t // d, d, w).transpose(1, 0, 2).reshape(t, w)


def _from_class(a, d):
    t, w = a.shape
    return a if d == 1 else a.reshape(d, t // d, w).transpose(1, 0, 2).reshape(t, w)


def kernel(x, c, w_ada, b_ada, g_norm1, g_norm2, w_in, b_gate, g_qa, g_ka, g_qb, g_kb, rpb, w_proj_a, w_proj_b, w_o, w_ffn_in, w_ffn_out, loss_target, m_w_ada, m_b_ada, m_g_norm1, m_g_norm2, m_w_in, m_b_gate, m_g_qa, m_g_ka, m_g_qb, m_g_kb, m_rpb, m_w_proj_a, m_w_proj_b, m_w_o, m_w_ffn_in, m_w_ffn_out, v_w_ada, v_b_ada, v_g_norm1, v_g_norm2, v_w_in, v_b_gate, v_g_qa, v_g_ka, v_g_qb, v_g_kb, v_rpb, v_w_proj_a, v_w_proj_b, v_w_o, v_w_ffn_in, v_w_ffn_out):
    t, d = x.shape[1], x.shape[2]
    d_ff = w_ffn_out.shape[1] * N_DEV
    me = 4 * lax.axis_index("x") + 2 * lax.axis_index("y") + lax.axis_index("c")
    xt, tgt = x.reshape(t, d), loss_target.reshape(t, d)
    ones = _head_ones()

    shards = [s.astype(WIRE_DTYPE) for s in (w_in[0].T, w_ffn_in[0].T, w_proj_a[0].T, w_proj_b[0].T, w_o[0], w_ffn_out[0])]
    lands = [lax.dynamic_update_slice(lax.empty((N_DEV * s.shape[0], s.shape[1]), s.dtype), s, (me * s.shape[0], 0))
             for s in shards]

    c_all = _all_gather([jnp.pad(c, ((0, 7), (0, 0)))], name="gather_c")[0][::8]
    c_all = jnp.pad(c_all, ((0, LANES - N_DEV), (0, 0)))

    def mod_body(c_ref, w_ref, b_ref, o_ref, act_ref):
        act = _silu(c_ref[...])
        act_ref[...] = act
        hi, mid, lo = _split3(act)
        w = w_ref[...]
        whi, wmid, wlo = _split3(w)
        acc = _dot(hi, whi) + (_dot(hi, wmid) + _dot(mid, whi)) + (_dot(hi, wlo) + _dot(mid, wmid) + _dot(lo, whi))
        o_ref[...] = acc + b_ref[...]

    ncol = w_ada.shape[2]
    b_ada_mine = lax.dynamic_slice(b_ada, (0, me * ncol), (1, ncol))
    mod_part, c_act = pl.pallas_call(
        mod_body,
        name="ada_mod",
        out_shape=[jax.ShapeDtypeStruct((LANES, ncol), F32), jax.ShapeDtypeStruct((LANES, d), F32)],
        compiler_params=pltpu.CompilerParams(vmem_limit_bytes=_vmem(6 * d * ncol * 4)),
    )(c_all, w_ada[0], b_ada_mine)
    mod_all = _all_gather([mod_part[:N_DEV]], name="gather_mod")[0].reshape(N_DEV, N_DEV, ncol)
    mod = lax.dynamic_index_in_dim(mod_all, me, axis=1, keepdims=False).reshape(6, d)
    sh1, sc1, gt1, sh2, sc2, gt2 = [mod[i:i + 1] for i in range(6)]

    def norm_fwd(rows, vecs):
        (xv,), (g, sc, sh) = rows, vecs
        return [xv * _rms(xv) * g * (1.0 + sc) + sh], []

    w_handles, w_token = _split_start(shards, lands, gather=True, after=mod, name="gather_weights_start")
    (h,) = _rowmap(norm_fwd, [xt], [g_norm1 + w_token, sc1, sh1], [(d, MXU_DTYPE)], [], name="norm1")
    n_a, n_b = 3 * WA, 3 * WB
    (w_in_t,) = _split_wait(w_handles[:1], h, gather=True, name="gather_w_in_wait")[1]
    w_in_a, w_in_b, w_in_g = w_in_t[:n_a], w_in_t[n_a:n_a + n_b], w_in_t[n_a + n_b:]
    gates = _mm(h, w_in_g, tb=True, out_dtype=ACT_DTYPE, name="proj_gates")

    rot_c, rot_lo, rot_hi = _rot_tables(t)
    tile_g = lambda g, heads: jnp.tile(g, (1, heads))

    def qk_fwd(width, rotate):
        def fn(xv, rows, vecs):
            gq, gk, on = vecs
            outs = [xv]
            for i, g in enumerate((gq, gk)):
                xi = xv[:, i * width:(i + 1) * width]
                r = lax.rsqrt(_headsum(xi * xi, on) * (1.0 / HEAD_DIM) + EPS)
                yi = xi * r * g
                if rotate:
                    yi = _rot(yi, rows[0], rows[1], rows[2])
                outs.append(yi)
            outs.append(xv[:, 2 * width:])
            return outs, []
        return fn

    qkv_a, qa, ka, va = _mm_parts_rows(
        [(h, w_in_a, "nt")], qk_fwd(WA, False), [], [tile_g(g_qa, NA_HEADS), tile_g(g_ka, NA_HEADS), ones],
        [(3 * WA, ACT_DTYPE)] + [(WA, MXU_DTYPE)] * 3, [], name="proj_a_qknorm")
    qkv_b, qb, kb, vb = _mm_parts_rows(
        [(h, w_in_b, "nt")], qk_fwd(WB, True), [rot_c, rot_lo, rot_hi],
        [tile_g(g_qb, DIL_HEADS), tile_g(g_kb, DIL_HEADS), ones],
        [(3 * WB, ACT_DTYPE)] + [(WB, MXU_DTYPE)] * 3, [], name="proj_b_qknorm")

    bias_tab = _na_bias_table(rpb[0])
    o_a, lse_a = _attn_fwd(qa, ka, va, kind="na", bias=bias_tab, name="na_fwd")

    grp = []
    for g, (_, dil) in enumerate(DIL_CONFIGS):
        sl = slice(g * WB_OUT, (g + 1) * WB_OUT)
        qg, kg, vg = [_to_class(a[:, sl], dil) for a in (qb, kb, vb)]
        og, lg = _attn_fwd(qg, kg, vg, kind="dil", seg=t // dil, name=f"dil_fwd{g}")
        grp.append(dict(q=qg, k=kg, v=vg, o=_from_class(og, dil), lse=_from_class(lg, dil), lse_c=lg, dil=dil))

    def merge_fwd(rows, _):
        o0, o1, o2, l0, l1, l2 = rows
        mx = jnp.maximum(jnp.maximum(l0, l1), l2)
        e0, e1, e2 = jnp.exp(l0 - mx), jnp.exp(l1 - mx), jnp.exp(l2 - mx)
        s = e0 + e1 + e2
        return [(e0 / s) * o0 + (e1 / s) * o1 + (e2 / s) * o2], []

    (o_b,) = _rowmap(merge_fwd, [gr["o"] for gr in grp] + [gr["lse"] for gr in grp], [], [(WB_OUT, F32)], [], name="dil_merge")

    w_pa_t, w_pb_t, w_o_f = _split_wait(w_handles[2:5], o_b, gather=True, name="gather_w_out_wait")[1]
    pa = _mm(o_a, w_pa_t, tb=True, out_dtype=ACT_DTYPE, name="proj_out_a")
    pb = _mm(o_b, w_pb_t, tb=True, out_dtype=ACT_DTYPE, name="proj_out_b")

    def gate_fwd(rows, vecs):
        gv, pav, pbv = rows
        sg = jax.nn.sigmoid(gv + vecs[0])
        return [sg[:, :d] * pav + sg[:, d:] * pbv], []

    (merged,) = _rowmap(gate_fwd, [gates, pa, pb], [b_gate], [(d, MXU_DTYPE)], [], name="gate_merge")
    def resid_norm(av, rows, vecs):
        (xv,), (gt, g, sc, sh) = rows, vecs
        x1v = xv + gt * av
        return [av, x1v, x1v * _rms(x1v) * g * (1.0 + sc) + sh], []

    att, x1, h2 = _mm_parts_rows([(merged, w_o_f)], resid_norm, [xt], [gt1, g_norm2, sc2, sh2],
                           [(d, F32), (d, F32), (d, MXU_DTYPE)], [], name="proj_o_resid_norm2")

    w_ffn_in_t, w_ffn_out_f = _split_wait([w_handles[1], w_handles[5]], h2, gather=True, name="gather_w_ffn_wait")[1]
    w_ffn_a, w_ffn_up = w_ffn_in_t[:d_ff], w_ffn_in_t[d_ff:]

    def swiglu_fwd(prods, _):
        a, up = prods
        return [a, up, _silu(a) * up]

    ua, uu, f = _mm_ew(h2, [w_ffn_a, w_ffn_up], swiglu_fwd, [], [ACT_DTYPE, ACT_DTYPE, MXU_DTYPE], name="ffn_in_swiglu")

    def loss_fn(yv, rows, vecs):
        (x1v, tv), gt = rows, vecs[0]
        err = x1v + gt * yv - tv
        dout = err * (1.0 / d)
        return [dout, dout * gt], [_colsum(err * err), _colsum(dout * yv)]

    dout, dy2, err2, dgt2 = _mm_parts_rows([(f, w_ffn_out_f)], loss_fn, [x1, tgt], [gt2], [(d, F32), (d, MXU_DTYPE)],
                                           [d, d], name="ffn_out_loss")

    dw_ffn_out = _mm(f, dy2, ta=True, out_dtype=WIRE_DTYPE, name="wgrad_ffn_out")
    def swiglu_bwd(prods, rows):
        (dfv,), (a, up) = prods, rows
        sg = jax.nn.sigmoid(a)
        return [dfv * up * (sg * (1.0 + a * (1.0 - sg))), dfv * (a * sg)]

    da, dup = _mm_ew(dy2, [w_ffn_out_f], swiglu_bwd, [ua, uu], [MXU_DTYPE, MXU_DTYPE], name="dgrad_ffn_out_swiglu_bwd")
    dw_ffn_in_t = jnp.concatenate([_mm(da, h2, ta=True, out_dtype=WIRE_DTYPE, name="wgrad_ffn_in_a"),
                                   _mm(dup, h2, ta=True, out_dtype=WIRE_DTYPE, name="wgrad_ffn_in_up")], axis=0)
    land7 = lambda a: lax.empty((N_DEV - 1, a.shape[0] // N_DEV, a.shape[1]), a.dtype)
    own_block = lambda a: lax.dynamic_slice(a, (me * (a.shape[0] // N_DEV), 0), (a.shape[0] // N_DEV, a.shape[1]))
    g_ffn = [dw_ffn_in_t, dw_ffn_out]
    h_ffn, tok_ffn = _split_start(g_ffn, [land7(a) for a in g_ffn], gather=False, name="exchange_ffn_start")
    def norm_bwd(dh, xv, g, sc):
        r = _rms(xv)
        xh = xv * r
        dxh = dh * g * (1.0 + sc)
        dxv = r * (dxh - xh * jnp.mean(dxh * xh, axis=-1, keepdims=True))
        return dxv, [_colsum(dh), _colsum(dh * xh * g), _colsum(dh * xh * (1.0 + sc))]

    def norm2_bwd(dhv, rows, vecs):
        (x1v, dov, av), (g, sc, gt) = rows, vecs
        dxv, sums = norm_bwd(dhv, x1v, g, sc)
        dx1v = dov + dxv
        return [dx1v, dx1v * gt], sums + [_colsum(dx1v * av)]

    dx1, datt, dsh2, dsc2, dg2, dgt1 = _mm_parts_rows(
        [(da, w_ffn_a), (dup, w_ffn_up)], norm2_bwd, [x1, dout, att], [g_norm2 + tok_ffn, sc2, gt1],
        [(d, F32), (d, MXU_DTYPE)], [d] * 4, name="dgrad_ffn_in_norm2_bwd")
    dw_o = _mm(merged, datt, ta=True, out_dtype=WIRE_DTYPE, name="wgrad_o")
    def gate_bwd(dm, rows, vecs):
        gv, pav, pbv = rows
        sg = jax.nn.sigmoid(gv + vecs[0])
        ga, gb = sg[:, :d], sg[:, d:]
        dgp = jnp.concatenate([dm * pav * ga * (1.0 - ga), dm * pbv * gb * (1.0 - gb)], axis=1)
        return [dm * ga, dm * gb, dgp], [_colsum(dgp)]

    dpa, dpb, dgates, db_gate = _mm_parts_rows(
        [(datt, w_o_f.T)], gate_bwd, [gates, pa, pb], [b_gate],
        [(d, MXU_DTYPE), (d, MXU_DTYPE), (2 * d, MXU_DTYPE)], [2 * d], name="dgrad_o_gate_bwd")
    dw_pa_t = _mm(dpa, o_a, ta=True, out_dtype=WIRE_DTYPE, name="wgrad_proj_a")
    dw_pb_t = _mm(dpb, o_b, ta=True, out_dtype=WIRE_DTYPE, name="wgrad_proj_b")
    g_out = [dw_pa_t, dw_pb_t, dw_o]
    h_out, tok_out = _split_start(g_out, [land7(a) for a in g_out], gather=False, name="exchange_out_start")
    def delta_a(doa, rows, vecs):
        return [doa, _headsum(doa * rows[0], vecs[0])], []

    do_a, dterm_a = _mm_parts_rows([(dpa, w_pa_t)], delta_a, [o_a], [ones + tok_out.astype(ones.dtype)],
                                   [(WA, F32), (WA, F32)], [], name="dgrad_proj_a_delta")
    dqa, dka, dva, dbias = _attn_bwd(qa, ka, va, do_a, dterm_a, lse_a, kind="na", bias=bias_tab, name="na_bwd")
    g_rpb = _na_bias_grad(dbias)

    def merge_bwd(dob, rows, vecs):
        o0, o1, o2, l0, l1, l2 = rows
        on = vecs[0]
        mx = jnp.maximum(jnp.maximum(l0, l1), l2)
        e0, e1, e2 = jnp.exp(l0 - mx), jnp.exp(l1 - mx), jnp.exp(l2 - mx)
        s = e0 + e1 + e2
        ws = [e0 / s, e1 / s, e2 / s]
        dws = [_headsum(dob * o, on) for o in (o0, o1, o2)]
        mean = ws[0] * dws[0] + ws[1] * dws[1] + ws[2] * dws[2]
        return [w * dob for w in ws] + [w * mean for w in ws], []

    mb = _mm_parts_rows([(dpb, w_pb_t)], merge_bwd, [gr["o"] for gr in grp] + [gr["lse"] for gr in grp], [ones],
                        [(WB_OUT, F32)] * 6, [], name="dgrad_proj_b_merge_bwd")
    dqb, dkb, dvb = [], [], []
    for g, gr in enumerate(grp):
        dil = gr["dil"]
        dq, dk, dv = _attn_bwd(gr["q"], gr["k"], gr["v"], _to_class(mb[g], dil), _to_class(mb[3 + g], dil), gr["lse_c"],
                               kind="dil", seg=t // dil, name=f"dil_bwd{g}")
        dqb.append(_from_class(dq, dil))
        dkb.append(_from_class(dk[0][0][0], dil))
        dvb.append(_from_class(dv[0][0][0], dil))

    def qk_bwd(width, rotate, nparts):
        def fn(rows, vecs):
            gq, gk, on = vecs
            xv = rows[0]
            pos = 1
            if rotate:
                rc, rlo, rhi = rows[1:4]
                pos = 4
            cat = lambda parts: parts[0] if len(parts) == 1 else jnp.concatenate(parts, axis=1)
            ends = np.cumsum((pos,) + nparts)
            dq, dk, dv = [cat(rows[ends[i]:ends[i + 1]]) for i in range(3)]
            outs, sums = [], []
            for i, (dy, g) in enumerate(((dq, gq), (dk, gk))):
                if rotate:
                    dy = _rot(dy, rc, -rlo, -rhi)
                xi = xv[:, i * width:(i + 1) * width]
                r = lax.rsqrt(_headsum(xi * xi, on) * (1.0 / HEAD_DIM) + EPS)
                xh = xi * r
                dxh = dy * g
                outs.append(r * (dxh - xh * (_headsum(dxh * xh, on) * (1.0 / HEAD_DIM))))
                sums.append(_colsum(dy * xh))
            return [jnp.concatenate(outs + [dv], axis=1)], sums
        return fn

    dqkv_a, dg_qa, dg_ka = _rowmap(qk_bwd(WA, False, (1, len(dka), len(dva))), [qkv_a, dqa] + dka + dva,
                                   [tile_g(g_qa, NA_HEADS), tile_g(g_ka, NA_HEADS), ones],
                                   [(3 * WA, MXU_DTYPE)], [WA, WA], name="qknorm_a_bwd")
    dqkv_b, dg_qb, dg_kb = _rowmap(qk_bwd(WB, True, (3, 3, 3)), [qkv_b, rot_c, rot_lo, rot_hi] + dqb + dkb + dvb,
                                   [tile_g(g_qb, DIL_HEADS), tile_g(g_kb, DIL_HEADS), ones],
                                   [(3 * WB, MXU_DTYPE)], [WB, WB], name="qknorm_b_bwd")

    dw_in_t = jnp.concatenate([
        _mm(dqkv_a, h, ta=True, out_dtype=WIRE_DTYPE, name="wgrad_in_a"),
        _mm(dqkv_b, h, ta=True, out_dtype=WIRE_DTYPE, name="wgrad_in_b"),
        _mm(dgates, h, ta=True, out_dtype=WIRE_DTYPE, name="wgrad_in_gates")], axis=0)
    h_in, tok_in = _split_start([dw_in_t], [land7(dw_in_t)], gather=False, name="exchange_in_start")
    def norm1_bwd(dhv, rows, vecs):
        xv, dx1v = rows
        dxv, sums = norm_bwd(dhv, xv, vecs[0], vecs[1])
        return [dx1v + dxv], sums

    grad_x, dsh1, dsc1, dg1 = _mm_parts_rows(
        [(dqkv_a, w_in_a), (dqkv_b, w_in_b), (dgates, w_in_g)], norm1_bwd, [xt, dx1], [g_norm1 + tok_in, sc1],
        [(d, F32)], [d] * 3, name="dgrad_in_norm1_bwd")

    heads_sum = lambda a, heads: a.reshape(heads, HEAD_DIM).sum(axis=0)
    dmod = jnp.concatenate([dsh1, dsc1, dgt1, dsh2, dsc2, dgt2], axis=1)
    local_small = _pack_small(dict(
        b_ada=dmod, g_norm1=dg1, g_norm2=dg2, b_gate=db_gate, g_qa=heads_sum(dg_qa, NA_HEADS),
        g_ka=heads_sum(dg_ka, NA_HEADS), g_qb=heads_sum(dg_qb, DIL_HEADS), g_kb=heads_sum(dg_kb, DIL_HEADS),
        rpb=g_rpb, loss=(0.5 / d) * jnp.sum(err2)))
    srows = local_small.shape[0]
    small_all = _all_gather([local_small], name="gather_small")[0].reshape(N_DEV, srows, LANES)
    small_sum = _sum8(small_all[0], small_all[1:], name="sum_small")
    small_shapes = dict(b_ada=b_ada.shape, g_norm1=g_norm1.shape, g_norm2=g_norm2.shape, b_gate=b_gate.shape,
                        g_qa=g_qa.shape, g_ka=g_ka.shape, g_qb=g_qb.shape, g_kb=g_kb.shape, rpb=rpb.shape, loss=())
    small_w = dict(b_ada=b_ada, g_norm1=g_norm1, g_norm2=g_norm2, b_gate=b_gate, g_qa=g_qa, g_ka=g_ka, g_qb=g_qb,
                   g_kb=g_kb, rpb=rpb, loss=jnp.zeros((), F32))
    small_m = dict(b_ada=m_b_ada, g_norm1=m_g_norm1, g_norm2=m_g_norm2, b_gate=m_b_gate, g_qa=m_g_qa, g_ka=m_g_ka,
                   g_qb=m_g_qb, g_kb=m_g_kb, rpb=m_rpb, loss=jnp.zeros((), F32))
    small_v = dict(b_ada=v_b_ada, g_norm1=v_g_norm1, g_norm2=v_g_norm2, b_gate=v_b_gate, g_qa=v_g_qa, g_ka=v_g_ka,
                   g_qb=v_g_qb, g_kb=v_g_kb, rpb=v_rpb, loss=jnp.zeros((), F32))
    s_delta, s_m, s_v = _adamw(_pack_small(small_w), small_sum, _pack_small(small_m), _pack_small(small_v), name="adamw_small")
    gs = _unpack_small(small_sum, small_shapes)
    ds_, ms_, vs_ = [_unpack_small(a, small_shapes) for a in (s_delta, s_m, s_v)]

    dmod_all = small_all[:, :6 * d // LANES].reshape(N_DEV, 6 * d)
    dmod_mine = jnp.pad(lax.dynamic_slice(dmod_all, (0, me * ncol), (N_DEV, ncol)), ((0, LANES - N_DEV), (0, 0)))

    def wada_body(c_ref, dm_ref, o_ref):
        chi, cmid, clo = _split3(c_ref[...])
        dhi, dmid, dlo = _split3(dm_ref[...])
        o_ref[...] = (_dot_tn(chi, dhi) + (_dot_tn(chi, dmid) + _dot_tn(cmid, dhi))
                      + (_dot_tn(chi, dlo) + _dot_tn(cmid, dmid) + _dot_tn(clo, dhi)))

    g_w_ada = pl.pallas_call(
        wada_body,
        name="wgrad_ada",
        out_shape=jax.ShapeDtypeStruct((d, ncol), F32),
        compiler_params=pltpu.CompilerParams(vmem_limit_bytes=_vmem(4 * d * ncol * 4)),
    )(c_act, dmod_mine)

    sent, recv = _split_wait(h_in + h_ffn + h_out, small_sum, gather=False, name="exchange_wait")
    names = ("w_in", "w_ffn_in", "w_ffn_out", "w_proj_a", "w_proj_b", "w_o")
    transposed = (True, True, False, True, True, False)
    big_g = {}
    for nme, own, r, tr in zip(names, sent, recv, transposed):
        s = _sum8(own_block(own), r, name=f"sum_{nme}")
        big_g[nme] = s.T if tr else s
    big_g["w_ada"] = g_w_ada
    big_w = dict(w_ada=w_ada, w_in=w_in, w_proj_a=w_proj_a, w_proj_b=w_proj_b, w_o=w_o, w_ffn_in=w_ffn_in, w_ffn_out=w_ffn_out)
    big_m = dict(w_ada=m_w_ada, w_in=m_w_in, w_proj_a=m_w_proj_a, w_proj_b=m_w_proj_b, w_o=m_w_o, w_ffn_in=m_w_ffn_in, w_ffn_out=m_w_ffn_out)
    big_v = dict(w_ada=v_w_ada, w_in=v_w_in, w_proj_a=v_w_proj_a, w_proj_b=v_w_proj_b, w_o=v_w_o, w_ffn_in=v_w_ffn_in, w_ffn_out=v_w_ffn_out)
    grads, deltas, new_m, new_v = {}, {}, {}, {}
    for nme in big_w:
        dl, m1, v1 = _adamw(big_w[nme][0], big_g[nme], big_m[nme][0], big_v[nme][0], name=f"adamw_{nme}")
        grads[nme], deltas[nme], new_m[nme], new_v[nme] = big_g[nme][None], dl[None], m1[None], v1[None]
    for nme in _SMALL[:-1]:
        grads[nme], deltas[nme], new_m[nme], new_v[nme] = gs[nme], ds_[nme], ms_[nme], vs_[nme]

    order = ("w_ada", "b_ada", "g_norm1", "g_norm2", "w_in", "b_gate", "g_qa", "g_ka", "g_qb", "g_kb", "rpb",
             "w_proj_a", "w_proj_b", "w_o", "w_ffn_in", "w_ffn_out")
    return (gs["loss"], grad_x[None], *[grads[n] for n in order], *[deltas[n] for n in order],
            *[new_m[n] for n in order], *[new_v[n] for n in order])
```

```python
import functools

import numpy as np
import jax
import jax.numpy as jnp
from jax import lax
from jax.experimental import pallas as pl
from jax.experimental.pallas import tpu as pltpu

F32 = jnp.float32
MXU_DTYPE = jnp.bfloat16
WIRE_DTYPE = jnp.bfloat16
ACT_DTYPE = jnp.bfloat16

HEAD_DIM = 64
GRID_W = 64
NA_HEADS = 8
NA_KH = 8
NA_KW = 16
DIL_CONFIGS = ((128, 1), (512, 4), (2048, 16))
DIL_HEADS_PER_GROUP = 4
DIL_HEADS = DIL_HEADS_PER_GROUP * len(DIL_CONFIGS)
DIL_HALF = 64
ROT_DIM = HEAD_DIM // 4
ROPE_THETA = 500000.0
EPS = 1e-6
NEG_INF = -1e30
WA = NA_HEADS * HEAD_DIM
WB = DIL_HEADS * HEAD_DIM
WB_OUT = DIL_HEADS_PER_GROUP * HEAD_DIM
ADAM_LR = 0.001
ADAM_B1 = 0.9
ADAM_B2 = 0.999
ADAM_EPS = 1e-08
ADAM_WD = 0.01
ADAM_STEP = 10

N_DEV = 8
LANES = 128
VMEM_CAP = 60 * 2**20
VMEM_FLOOR = 56 * 2**20
MESH = pl.DeviceIdType.MESH
ANY = pl.BlockSpec(memory_space=pl.ANY)


def _vmem(nbytes):
    return int(min(VMEM_CAP, max(VMEM_FLOOR, nbytes * 5 // 4 + 4 * 2**20)))


def _pick(dim, cands):
    for c in cands:
        if c <= dim and dim % c == 0:
            return c
    return dim


def _nbytes(shape, dtype):
    return int(np.prod(shape)) * jnp.dtype(dtype).itemsize


def _dot(a, b, dims=((1,), (0,))):
    return lax.dot_general(a.astype(MXU_DTYPE), b.astype(MXU_DTYPE), (dims, ((), ())), preferred_element_type=F32)


def _dot_nt(a, b):
    return _dot(a, b, ((1,), (1,)))


def _dot_tn(a, b):
    return _dot(a, b, ((0,), (0,)))


def _split3(a):
    hi = a.astype(jnp.bfloat16)
    r1 = a - hi.astype(F32)
    mid = r1.astype(jnp.bfloat16)
    lo = (r1 - mid.astype(F32)).astype(jnp.bfloat16)
    return hi, mid, lo


def _silu(x):
    return x * jax.nn.sigmoid(x)


def _divisors(dim, unit):
    return [c for c in range(unit, dim + 1, unit) if dim % c == 0] or [dim]


def _mm_tiles(m, n, kdim, a_item, b_item, o_item, row_off=0):
    step_us, hbm_bytes_per_us, flops_per_us, budget = 0.35, 3.0e6, 8.0e8, 40 * 2**20
    best = None
    for tm in _divisors(m, LANES):
        for tn in _divisors(n, LANES):
            for tk in _divisors(kdim, LANES):
                if row_off % tm:
                    continue
                gm, gn, gk = m // tm, n // tn, kdim // tk
                vmem = 2 * (tm * tk * a_item + tk * tn * b_item + tm * tn * o_item) + 2 * (tm * tk + tk * tn)
                vmem += tm * tn * 4 * ((1 if gk > 1 else 0) + 1)
                if vmem > budget:
                    continue
                a_reads = m * kdim * a_item * (gn if gk > 1 else 1)
                traffic = a_reads + kdim * n * b_item * gm + m * n * o_item
                cost = gm * gn * gk * step_us + max(traffic / hbm_bytes_per_us, 2.0 * m * n * kdim / flops_per_us)
                if best is None or cost < best[0]:
                    best = (cost, tm, tn, tk)
    return best[1:]


def _mm(a, b, *, name, ta=False, tb=False, out_dtype=F32, into=None):
    if ta:
        kdim, m = a.shape
    else:
        m, kdim = a.shape
    n = b.shape[0] if tb else b.shape[1]
    assert b.shape[1 if tb else 0] == kdim
    buf, row_off = into if into is not None else (None, 0)
    if buf is not None:
        out_dtype = buf.dtype
    tm, tn, tk = _mm_tiles(m, n, kdim, a.dtype.itemsize, b.dtype.itemsize, jnp.dtype(out_dtype).itemsize, row_off)
    gm, gn, gk = m // tm, n // tn, kdim // tk
    ob = row_off // tm

    a_spec = pl.BlockSpec((tk, tm), lambda i, j, k: (k, i)) if ta else pl.BlockSpec((tm, tk), lambda i, j, k: (i, k))
    b_spec = pl.BlockSpec((tn, tk), lambda i, j, k: (j, k)) if tb else pl.BlockSpec((tk, tn), lambda i, j, k: (k, j))
    o_spec = pl.BlockSpec((tm, tn), lambda i, j, k: (i + ob, j))
    a_dims = (0,) if ta else (1,)
    b_dims = (1,) if tb else (0,)

    def body(a_ref, b_ref, *rest):
        o_ref, scratch = rest[-1 - (gk > 1)], rest[-(gk > 1):] if gk > 1 else ()
        if gk == 1:
            o_ref[...] = _dot(a_ref[...], b_ref[...], (a_dims, b_dims)).astype(o_ref.dtype)
            return
        (acc_ref,) = scratch
        k = pl.program_id(2)

        @pl.when(k == 0)
        def _():
            acc_ref[...] = jnp.zeros_like(acc_ref)

        acc_ref[...] += _dot(a_ref[...], b_ref[...], (a_dims, b_dims))

        @pl.when(k == gk - 1)
        def _():
            o_ref[...] = acc_ref[...].astype(o_ref.dtype)

    est = 2 * (tm * tk * a.dtype.itemsize + tk * tn * b.dtype.itemsize + tm * tn * jnp.dtype(out_dtype).itemsize)
    est += tm * tn * 4 + 2 * (tm * tk + tk * tn) * 2
    return pl.pallas_call(
        body,
        name=name,
        grid=(gm, gn, gk),
        in_specs=[a_spec, b_spec] + ([ANY] if buf is not None else []),
        out_specs=o_spec,
        out_shape=jax.ShapeDtypeStruct((m, n) if buf is None else buf.shape, out_dtype),
        input_output_aliases={2: 0} if buf is not None else {},
        scratch_shapes=[pltpu.VMEM((tm, tn), F32)] if gk > 1 else [],
        compiler_params=pltpu.CompilerParams(
            dimension_semantics=("parallel", "parallel", "arbitrary"), vmem_limit_bytes=_vmem(est)
        ),
    )(*((a, b) if buf is None else (a, b, buf)))


def _resident(shape):
    return pl.BlockSpec(shape, lambda i: (0,) * len(shape), pipeline_mode=pl.Buffered(1))


def _row_tile(m, fixed_bytes, bytes_per_row, budget=50 * 2**20):
    fits = [tm for tm in _divisors(m, LANES) if fixed_bytes + tm * bytes_per_row <= budget]
    return max(fits) if fits else _divisors(m, LANES)[0]


def _mm_parts_rows(parts, fn, rows, vecs, outs, reds, *, name):
    parts = [(p[0], p[1], len(p) > 2) for p in parts]
    m, n = parts[0][0].shape[0], parts[0][1].shape[0 if parts[0][2] else 1]
    npart, nr, nv, no = len(parts), len(rows), len(vecs), len(outs)
    row_bytes = sum(r.shape[1] * r.dtype.itemsize for r in rows) + sum(w * jnp.dtype(dt).itemsize for (w, dt) in outs)
    a_row_bytes = sum(a.shape[1] * a.dtype.itemsize for a, _, _ in parts)
    fixed = sum(_nbytes(b.shape, b.dtype) for _, b, _ in parts)
    per_row = 2 * (a_row_bytes + row_bytes) + n * 4 * 5
    tm = _row_tile(m, fixed, per_row)
    sub = min(tm, 2 * LANES)

    def body(*refs):
        ab = refs[:2 * npart]
        row_refs, vec_refs = refs[2 * npart:2 * npart + nr], refs[2 * npart + nr:2 * npart + nr + nv]
        out_refs = refs[2 * npart + nr + nv:2 * npart + nr + nv + no]
        red_refs = refs[2 * npart + nr + nv + no:]
        if red_refs:
            @pl.when(pl.program_id(0) == 0)
            def _():
                for ref in red_refs:
                    ref[...] = jnp.zeros_like(ref)

        vecs_v = [v[...] for v in vec_refs]
        for s0 in range(0, tm, sub):
            sl = slice(s0, s0 + sub)
            r = None
            for p, (_, _, nt) in enumerate(parts):
                term = (_dot_nt if nt else _dot)(ab[2 * p][sl, :], ab[2 * p + 1][...])
                r = term if r is None else r + term
            o, rd = fn(r, [x[sl, :].astype(F32) for x in row_refs], vecs_v)
            for ref, val in zip(out_refs, o):
                ref[sl, :] = val.astype(ref.dtype)
            for ref, val in zip(red_refs, rd):
                ref[...] += val

    in_specs, operands = [], []
    for a, b, _ in parts:
        in_specs += [pl.BlockSpec((tm, a.shape[1]), lambda i: (i, 0)), _resident(b.shape)]
        operands += [a, b]
    in_specs += [pl.BlockSpec((tm, r.shape[1]), lambda i: (i, 0)) for r in rows]
    in_specs += [pl.BlockSpec(v.shape, functools.partial(lambda nd, i: (0,) * nd, v.ndim)) for v in vecs]
    out_specs = [pl.BlockSpec((tm, w), lambda i: (i, 0)) for (w, _) in outs]
    out_specs += [pl.BlockSpec((1, w), lambda i: (0, 0)) for w in reds]
    out_shape = [jax.ShapeDtypeStruct((m, w), dt) for (w, dt) in outs] + [jax.ShapeDtypeStruct((1, w), F32) for w in reds]
    return pl.pallas_call(
        body,
        name=name,
        grid=(m // tm,),
        in_specs=in_specs,
        out_specs=out_specs,
        out_shape=out_shape,
        compiler_params=pltpu.CompilerParams(dimension_semantics=("arbitrary",), vmem_limit_bytes=_vmem(fixed + tm * per_row)),
    )(*operands, *rows, *vecs)


def _mm_ew(a, bs, fn, rows, outs, *, name):
    m, kdim = a.shape
    n = bs[0].shape[0]
    nb, nr, no = len(bs), len(rows), len(outs)
    cw = _pick(n, (2 * LANES, LANES))
    fixed = nb * n * kdim * bs[0].dtype.itemsize
    per_row = 2 * (kdim * a.dtype.itemsize + n * (sum(r.dtype.itemsize for r in rows) + sum(jnp.dtype(dt).itemsize for dt in outs)))
    per_row += cw * 4 * 4 * (nb + 4)
    tm = _row_tile(m, fixed, per_row)

    def body(*refs):
        a_ref, b_refs = refs[0], refs[1:1 + nb]
        row_refs, out_refs = refs[1 + nb:1 + nb + nr], refs[1 + nb + nr:]
        av = a_ref[...]
        for c0 in range(0, n, cw):
            cols = slice(c0, c0 + cw)
            o = fn([_dot_nt(av, b[cols, :]) for b in b_refs], [x[:, cols].astype(F32) for x in row_refs])
            for ref, val in zip(out_refs, o):
                ref[:, cols] = val.astype(ref.dtype)

    tile = pl.BlockSpec((tm, n), lambda i: (i, 0))
    return pl.pallas_call(
        body,
        name=name,
        grid=(m // tm,),
        in_specs=[pl.BlockSpec((tm, kdim), lambda i: (i, 0))] + [_resident((n, kdim))] * nb + [tile] * nr,
        out_specs=[tile] * no,
        out_shape=[jax.ShapeDtypeStruct((m, n), dt) for dt in outs],
        compiler_params=pltpu.CompilerParams(dimension_semantics=("parallel",), vmem_limit_bytes=_vmem(fixed + tm * per_row)),
    )(a, *bs, *rows)


def _rowmap(fn, rows, vecs, outs, reds, *, name, tm=None):
    norm = []
    for r in rows:
        if not isinstance(r, tuple):
            norm.append((r, r.shape[1], 0, None))
        elif len(r) == 2:
            norm.append((r[0], r[0].shape[2], 0, r[1]))
        else:
            norm.append((r[0], r[1], r[2], None))
    rows = norm
    t = rows[0][0].shape[-2]
    if tm is None:
        per_row = 2 * sum(w * a.dtype.itemsize for (a, w, _, _) in rows) + 2 * sum(w * jnp.dtype(d).itemsize for (w, d) in outs)
        per_row += 3 * 4 * max([w for (_, w, _, _) in rows] + [w for (w, _) in outs])
        tm = max(8, min(1024, (40 * 2**20) // per_row))
    tm = _pick(t, tuple(c for c in (1024, 512, 256, 128, 64, 32, 16, 8) if c <= tm))
    nr, nv, no = len(rows), len(vecs), len(outs)

    def body(*refs):
        row_refs, vec_refs = refs[:nr], refs[nr:nr + nv]
        out_refs, red_refs = refs[nr + nv:nr + nv + no], refs[nr + nv + no:]
        o, rd = fn([r[...].astype(F32) for r in row_refs], [v[...] for v in vec_refs])
        for ref, val in zip(out_refs, o):
            ref[...] = val.astype(ref.dtype)
        if red_refs:
            @pl.when(pl.program_id(0) == 0)
            def _():
                for ref in red_refs:
                    ref[...] = jnp.zeros_like(ref)

            for ref, val in zip(red_refs, rd):
                ref[...] += val

    in_specs = [pl.BlockSpec((tm, w), functools.partial(lambda cb, i: (i, cb), cb)) if lead is None
                else pl.BlockSpec((None, tm, w), functools.partial(lambda ld, i: (ld, i, 0), lead)) for (_, w, cb, lead) in rows]
    in_specs += [pl.BlockSpec(v.shape, functools.partial(lambda nd, i: (0,) * nd, v.ndim)) for v in vecs]
    out_specs = [pl.BlockSpec((tm, w), lambda i: (i, 0)) for (w, _) in outs]
    out_specs += [pl.BlockSpec((1, w), lambda i: (0, 0)) for w in reds]
    out_shape = [jax.ShapeDtypeStruct((t, w), d) for (w, d) in outs]
    out_shape += [jax.ShapeDtypeStruct((1, w), F32) for w in reds]
    est = 2 * sum(tm * w * a.dtype.itemsize for (a, w, _, _) in rows) + 2 * sum(_nbytes(v.shape, v.dtype) for v in vecs)
    est += 2 * sum(tm * w * jnp.dtype(d).itemsize for (w, d) in outs)
    est += 6 * tm * max([w for (_, w, _, _) in rows] + [w for (w, _) in outs]) * 4
    return pl.pallas_call(
        body,
        name=name,
        grid=(t // tm,),
        in_specs=in_specs,
        out_specs=out_specs,
        out_shape=out_shape,
        compiler_params=pltpu.CompilerParams(dimension_semantics=("arbitrary",), vmem_limit_bytes=_vmem(est)),
    )(*[r[0] for r in rows], *vecs)


def _colsum(v):
    return jnp.sum(v, axis=0, keepdims=True)


def _head_ones():
    i = np.arange(LANES)
    return jnp.asarray((i[:, None] // HEAD_DIM) == (i[None, :] // HEAD_DIM), MXU_DTYPE)


def _headsum(y, ones):
    parts = []
    for j in range(y.shape[1] // LANES):
        c = y[:, j * LANES:(j + 1) * LANES]
        hi = c.astype(MXU_DTYPE)
        lo = c - hi.astype(F32)
        parts.append(_dot(hi, ones) + _dot(lo, ones))
    return parts[0] if len(parts) == 1 else jnp.concatenate(parts, axis=1)


def _rot(y, c, s_lo, s_hi):
    parts = []
    for j in range(y.shape[1] // LANES):
        yc = y[:, j * LANES:(j + 1) * LANES]
        parts.append(yc * c + pltpu.roll(yc, LANES - ROT_DIM // 2, 1) * s_lo + pltpu.roll(yc, ROT_DIM // 2, 1) * s_hi)
    return parts[0] if len(parts) == 1 else jnp.concatenate(parts, axis=1)


def _rot_tables(t):
    half = ROT_DIM // 2
    inv_freq = ROPE_THETA ** (-(jnp.arange(half, dtype=F32) * 2.0) / ROT_DIM)
    ang = jnp.arange(t).astype(F32)[:, None] * inv_freq[None, :]
    cos, sin = jnp.cos(ang), jnp.sin(ang)
    z = lambda w: jnp.zeros((t, w), F32)
    c = jnp.concatenate([cos, cos, jnp.ones((t, HEAD_DIM - ROT_DIM), F32)], axis=1)
    s_lo = jnp.concatenate([-sin, z(HEAD_DIM - half)], axis=1)
    s_hi = jnp.concatenate([z(half), sin, z(HEAD_DIM - ROT_DIM)], axis=1)
    return [jnp.tile(a, (1, LANES // HEAD_DIM)) for a in (c, s_lo, s_hi)]


def _rms(x):
    return lax.rsqrt(jnp.mean(x * x, axis=-1, keepdims=True) + EPS)


def _window(kind, n, bq, t, seg):
    if kind == "na":
        rows = t // GRID_W
        rs = jnp.clip(n - NA_KH // 2, 0, rows - NA_KH)
        return rs
    nk = bq + 2 * DIL_HALF
    return jnp.clip(n * bq - DIL_HALF, 0, t - nk)


def _dil_mask(n, bq, nk, ws, seg):
    qi = n * bq + lax.broadcasted_iota(jnp.int32, (bq, nk), 0)
    ki = ws + lax.broadcasted_iota(jnp.int32, (bq, nk), 1)
    shift = int(np.log2(seg))
    return (jnp.abs(ki - qi) <= DIL_HALF) & ((ki >> shift) == (qi >> shift))


HS = 4
QW = HS * HEAD_DIM


def _head_of_lane(width=QW):
    return lax.broadcasted_iota(jnp.int32, (1, width), 1) // HEAD_DIM


def _stack_heads(a):
    head = _head_of_lane()
    return jnp.concatenate([jnp.where(head == e, a, jnp.zeros_like(a)) for e in range(HS)], axis=0)


def _unstack_heads(a, bq):
    head = _head_of_lane()
    out = jnp.zeros((bq, QW), a.dtype)
    for e in range(HS):
        out = jnp.where(head == e, a[e * bq:(e + 1) * bq], out)
    return out


def _stack_cols(blk, bq):
    head = _head_of_lane()
    return jnp.concatenate(
        [jnp.max(jnp.where(head == e, blk, -jnp.inf), axis=1, keepdims=True) for e in range(HS)], axis=0)


def _attn_geometry(kind):
    if kind == "na":
        return GRID_W, NA_KH * GRID_W, 16
    bq = 128
    return bq, bq + 2 * DIL_HALF, 8


def _attn_scores(kind, n, bq, nk, t, seg, qs, k_ref, b_ref):
    scale = HEAD_DIM ** -0.5
    if kind == "na":
        rs = _window(kind, n, bq, t, seg)
        ws = pl.multiple_of(rs * GRID_W, GRID_W)
        ro0 = rs - n + (NA_KH - 1)
        s = _dot_nt(qs, k_ref[pl.ds(ws, nk), :]) * scale
        s = s + jnp.concatenate(
            [jnp.concatenate([b_ref[e, ro0 + 2 * i] for i in range(NA_KH // 2)], axis=1) for e in range(HS)], axis=0)
        return s, ws, ro0
    ws = pl.multiple_of(_window(kind, n, bq, t, seg), DIL_HALF)
    mask = _dil_mask(n, bq, nk, ws, seg)
    s = _dot_nt(qs, k_ref[pl.ds(ws, nk), :]) * scale
    s = jnp.where(jnp.concatenate([mask] * HS, axis=0), s, NEG_INF)
    return s, ws, None


def _attn_fwd(q, k, v, *, kind, name, bias=None, seg=None):
    t, w = q.shape
    quads = w // QW
    bq, nk, sub = _attn_geometry(kind)
    nq = t // (bq * sub)

    def body(*refs):
        if kind == "na":
            q_ref, k_ref, v_ref, b_ref, o_ref, l_ref = refs
        else:
            (q_ref, k_ref, v_ref, o_ref, l_ref), b_ref = refs, None
        for i in range(sub):
            n = pl.program_id(1) * sub + i
            rows = slice(i * bq, (i + 1) * bq)
            s, ws, _ = _attn_scores(kind, n, bq, nk, t, seg, _stack_heads(q_ref[rows, :]), k_ref, b_ref)
            m = jnp.max(s, axis=1, keepdims=True)
            p = jnp.exp(s - m)
            l = jnp.sum(p, axis=1, keepdims=True)
            o_ref[rows, :] = _unstack_heads(_dot(p / l, v_ref[pl.ds(ws, nk), :]), bq)
            l_ref[rows, :] = _unstack_heads(jnp.broadcast_to(m + jnp.log(l), (HS * bq, QW)), bq)

    blk = pl.BlockSpec((bq * sub, QW), lambda j, n: (n, j))
    res = pl.BlockSpec((t, QW), lambda j, n: (0, j))
    in_specs = [blk, res, res]
    operands = [q, k, v]
    est = 4 * t * QW * q.dtype.itemsize + 12 * sub * HS * bq * nk * 4
    if kind == "na":
        in_specs.append(pl.BlockSpec((HS,) + bias.shape[1:], lambda j, n: (j, 0, 0, 0)))
        operands.append(bias)
        est += 2 * _nbytes((HS,) + bias.shape[1:], F32)
    return pl.pallas_call(
        body,
        name=name,
        grid=(quads, nq),
        in_specs=in_specs,
        out_specs=[blk, blk],
        out_shape=[jax.ShapeDtypeStruct((t, w), F32)] * 2,
        compiler_params=pltpu.CompilerParams(dimension_semantics=("arbitrary", "arbitrary"), vmem_limit_bytes=_vmem(est)),
    )(*operands)


def _attn_bwd(q, k, v, do, dterm, lse, *, kind, name, bias=None, seg=None):
    t, w = q.shape
    quads = w // QW
    bq, nk, sub = _attn_geometry(kind)
    nq = t // (bq * sub)
    scale = HEAD_DIM ** -0.5

    def body(*refs):
        if kind == "na":
            q_ref, k_ref, v_ref, do_ref, dt_ref, l_ref, b_ref, dq_ref, dk_hbm, dv_hbm, db_ref, dk_acc, dv_acc, sem = refs
        else:
            q_ref, k_ref, v_ref, do_ref, dt_ref, l_ref, dq_ref, dk_hbm, dv_hbm, dk_acc, dv_acc, sem = refs
            b_ref = None
        j, step = pl.program_id(0), pl.program_id(1)

        @pl.when(step == 0)
        def _():
            dk_acc[...] = jnp.zeros_like(dk_acc)
            dv_acc[...] = jnp.zeros_like(dv_acc)
            if kind == "na":
                db_ref[...] = jnp.zeros_like(db_ref)

        for b in range(sub):
            n = step * sub + b
            rows = slice(b * bq, (b + 1) * bq)
            qs = _stack_heads(q_ref[rows, :])
            dos = _stack_heads(do_ref[rows, :])
            s, ws, ro0 = _attn_scores(kind, n, bq, nk, t, seg, qs, k_ref, b_ref)
            p = jnp.exp(s - _stack_cols(l_ref[rows, :], bq))
            dp = _dot_nt(dos, v_ref[pl.ds(ws, nk), :])
            ds = p * (dp - _stack_cols(dt_ref[rows, :], bq))
            if kind == "na":
                for e in range(HS):
                    for i in range(NA_KH // 2):
                        db_ref[e, ro0 + 2 * i] += ds[e * bq:(e + 1) * bq, i * LANES:(i + 1) * LANES]
            dsc = ds * scale
            dq_ref[rows, :] = _unstack_heads(_dot(dsc, k_ref[pl.ds(ws, nk), :]), bq)
            dk_acc[pl.ds(ws, nk), :] += _dot_tn(dsc, qs)
            dv_acc[pl.ds(ws, nk), :] += _dot_tn(p, dos)

        @pl.when(step == nq - 1)
        def _():
            ck = pltpu.make_async_copy(dk_acc, dk_hbm.at[j], sem.at[0])
            cv = pltpu.make_async_copy(dv_acc, dv_hbm.at[j], sem.at[1])
            ck.start()
            cv.start()
            ck.wait()
            cv.wait()

    blk = pl.BlockSpec((bq * sub, QW), lambda j, n: (n, j))
    res = pl.BlockSpec((t, QW), lambda j, n: (0, j))
    in_specs = [blk, res, res, blk, blk, blk]
    operands = [q, k, v, do, dterm, lse]
    out_specs = [blk, ANY, ANY]
    out_shape = [jax.ShapeDtypeStruct((t, w), F32)] + [jax.ShapeDtypeStruct((quads, t, QW), F32)] * 2
    est = 4 * t * QW * q.dtype.itemsize + 2 * t * QW * 4 + 16 * sub * HS * bq * nk * 4
    if kind == "na":
        bspec = pl.BlockSpec((HS,) + bias.shape[1:], lambda j, n: (j, 0, 0, 0))
        in_specs.append(bspec)
        operands.append(bias)
        out_specs.append(bspec)
        out_shape.append(jax.ShapeDtypeStruct(bias.shape, F32))
        est += 4 * _nbytes((HS,) + bias.shape[1:], F32)
    res_ = pl.pallas_call(
        body,
        name=name,
        grid=(quads, nq),
        in_specs=in_specs,
        out_specs=out_specs,
        out_shape=out_shape,
        scratch_shapes=[pltpu.VMEM((t, QW), F32), pltpu.VMEM((t, QW), F32), pltpu.SemaphoreType.DMA((2,))],
        compiler_params=pltpu.CompilerParams(dimension_semantics=("arbitrary", "arbitrary"), vmem_limit_bytes=_vmem(est)),
    )(*operands)
    unquad = lambda a: [(a, i) for i in range(quads)]
    return (res_[0], unquad(res_[1]), unquad(res_[2])) + tuple(res_[3:])


def _na_onehot():
    qc = np.arange(GRID_W)[:, None]
    kc = np.arange(GRID_W)[None, :]
    start = np.clip(qc - NA_KW // 2, 0, GRID_W - NA_KW)
    inwin = (kc >= start) & (kc < start + NA_KW)
    off = kc - qc + (NA_KW - 1)
    e_mat = np.zeros((2, 32, GRID_W, 2, GRID_W), np.float32)
    for e in range(2):
        for c in range(2 * NA_KW - 1):
            e_mat[e, c, :, e, :] = (off == c) & inwin
    neg = np.where(inwin, 0.0, NEG_INF).astype(np.float32)
    neg = np.broadcast_to(neg[:, None, :], (GRID_W, 2, GRID_W)).reshape(1, GRID_W * LANES)
    return jnp.asarray(e_mat.reshape(64, GRID_W * LANES), MXU_DTYPE), jnp.asarray(neg)


def _na_rowpairs(rpb):
    p = jnp.pad(rpb, ((0, 0), (0, 0), (0, 1)))
    return jnp.concatenate([p[:, :-1], p[:, 1:]], axis=-1).reshape(NA_HEADS * (2 * NA_KH - 2), 64)


def _na_bias_table(rpb):
    r2 = _na_rowpairs(rpb)
    e_mat, neg = _na_onehot()

    def body(r_ref, e_ref, n_ref, o_ref):
        hi, mid, lo = _split3(r_ref[...])
        e = e_ref[...]
        o_ref[...] = _dot(hi, e) + _dot(mid, e) + _dot(lo, e) + n_ref[...]

    out = pl.pallas_call(
        body,
        name="na_bias_table",
        out_shape=jax.ShapeDtypeStruct((r2.shape[0], GRID_W * LANES), F32),
        compiler_params=pltpu.CompilerParams(vmem_limit_bytes=_vmem(6 * r2.shape[0] * GRID_W * LANES * 4)),
    )(r2, e_mat, neg)
    return out.reshape(NA_HEADS, 2 * NA_KH - 2, GRID_W, LANES)


def _na_bias_grad(dbt):
    e_mat, _ = _na_onehot()
    flat = dbt.reshape(NA_HEADS * (2 * NA_KH - 2), GRID_W * LANES)

    def body(d_ref, e_ref, o_ref):
        hi, mid, lo = _split3(d_ref[...])
        e = e_ref[...]
        o_ref[...] = _dot_nt(hi, e) + _dot_nt(mid, e) + _dot_nt(lo, e)

    g = pl.pallas_call(
        body,
        name="na_bias_grad",
        out_shape=jax.ShapeDtypeStruct((flat.shape[0], 64), F32),
        compiler_params=pltpu.CompilerParams(vmem_limit_bytes=_vmem(6 * flat.shape[0] * GRID_W * LANES * 4)),
    )(flat, e_mat)
    g = g.reshape(NA_HEADS, 2 * NA_KH - 2, 2, 32)[..., :2 * NA_KW - 1]
    first = jnp.pad(g[:, :, 0], ((0, 0), (0, 1), (0, 0)))
    second = jnp.pad(g[:, :, 1], ((0, 0), (1, 0), (0, 0)))
    return first + second


def _all_gather(arrs, *, name):
    na = len(arrs)

    def body(*refs):
        ins, outs = refs[:na], refs[na:2 * na]
        send_sems, recv_sems, local_sems = refs[2 * na:]
        x, y, c = lax.axis_index("x"), lax.axis_index("y"), lax.axis_index("c")
        me, sibling = (x, y, c), (x, y, 1 - c)
        chips = [(1 - x, y), (x, 1 - y), (1 - x, 1 - y)]

        def rows(a, px, py, pc):
            r = ins[a].shape[0]
            return outs[a].at[pl.ds((4 * px + 2 * py + pc) * r, r), :]

        def copy(a, k, block, to, src=None):
            return pltpu.make_async_remote_copy(
                src_ref=rows(a, *block) if src is None else src, dst_ref=rows(a, *block),
                send_sem=send_sems.at[a, k], recv_sem=recv_sems.at[a, k], device_id=to, device_id_type=MESH)

        mine = [pltpu.make_async_copy(ins[a], rows(a, *me), local_sems.at[a]) for a in range(na)]
        for cp in mine:
            cp.start()
        first = []
        for a in range(na):
            first.append(copy(a, 0, me, sibling, src=ins[a]))
            first += [copy(a, 1 + j, me, (*chip, c), src=ins[a]) for j, chip in enumerate(chips)]
        for cp in first:
            cp.start()
        passed = []
        for j, chip in enumerate(chips):
            for a in range(na):
                copy(a, 1 + j, (*chip, c), me).wait_recv()
                cp = copy(a, 4 + j, (*chip, c), sibling)
                cp.start()
                passed.append(cp)
        for a in range(na):
            copy(a, 0, sibling, me).wait_recv()
        for j, chip in enumerate(chips):
            for a in range(na):
                copy(a, 4 + j, (*chip, 1 - c), me).wait_recv()
        for cp in first + passed:
            cp.wait_send()
        for cp in mine:
            cp.wait()

    return pl.pallas_call(
        body,
        name=name,
        in_specs=[ANY] * na,
        out_specs=[ANY] * na,
        out_shape=[jax.ShapeDtypeStruct((N_DEV * a.shape[0], a.shape[1]), a.dtype) for a in arrs],
        scratch_shapes=[pltpu.SemaphoreType.DMA((na, 7)), pltpu.SemaphoreType.DMA((na, 7)), pltpu.SemaphoreType.DMA((na,))],
    )(*arrs)


HBM = pl.BlockSpec(memory_space=pltpu.HBM)
SEM = pl.BlockSpec(memory_space=pltpu.SEMAPHORE)
EFFECT = pltpu.SideEffectType.DATAFLOW_SIDE_EFFECTING


def _peer_of(k):
    x, y, c = lax.axis_index("x"), lax.axis_index("y"), lax.axis_index("c")
    return x ^ ((k >> 2) & 1), y ^ ((k >> 1) & 1), c ^ (k & 1)


def _split_copies(gather, src_ref, land_ref, send_sems, recv_sems):
    x, y, c = lax.axis_index("x"), lax.axis_index("y"), lax.axis_index("c")
    my = 4 * x + 2 * y + c
    r = src_ref.shape[0] if gather else src_ref.shape[0] // N_DEV
    copies = []
    for k in range(1, N_DEV):
        px, py, pc = _peer_of(k)
        if gather:
            src, dst = src_ref, land_ref.at[pl.ds(my * r, r), :]
        else:
            src, dst = src_ref.at[pl.ds((4 * px + 2 * py + pc) * r, r), :], land_ref.at[k - 1]
        copies.append(pltpu.make_async_remote_copy(
            src_ref=src, dst_ref=dst, send_sem=send_sems.at[k - 1], recv_sem=recv_sems.at[k - 1],
            device_id=(px, py, pc), device_id_type=MESH))
    return copies


def _split_start(srcs, lands, *, gather, name, after=None):
    na = len(srcs)
    extra = [] if after is None else [after]

    def body(*refs):
        src_refs, land_refs = refs[:na], refs[na:2 * na]
        outs = refs[2 * na + len(extra):]
        for a in range(na):
            for cp in _split_copies(gather, src_refs[a], land_refs[a], outs[4 * a], outs[4 * a + 1]):
                cp.start()
        outs[4 * na][...] = jnp.zeros_like(outs[4 * na])

    out_shape, out_specs, aliases = [], [], {}
    for a in range(na):
        out_shape += [pltpu.SemaphoreType.DMA((N_DEV - 1,)), pltpu.SemaphoreType.DMA((N_DEV - 1,)),
                      pltpu.HBM(srcs[a].shape, srcs[a].dtype), pltpu.HBM(lands[a].shape, lands[a].dtype)]
        out_specs += [SEM, SEM, HBM, HBM]
        aliases[a] = 4 * a + 2
        aliases[na + a] = 4 * a + 3
    out_shape.append(jax.ShapeDtypeStruct((8, LANES), F32))
    out_specs.append(pl.BlockSpec(memory_space=pltpu.VMEM))
    res = pl.pallas_call(
        body,
        name=name,
        out_shape=tuple(out_shape),
        in_specs=[HBM] * (2 * na) + [ANY] * len(extra),
        out_specs=tuple(out_specs),
        input_output_aliases=aliases,
        compiler_params=pltpu.CompilerParams(has_side_effects=EFFECT),
    )(*[pltpu.with_memory_space_constraint(a, pltpu.HBM) for a in list(srcs) + list(lands)], *extra)
    return [tuple(res[4 * a:4 * a + 4]) for a in range(na)], res[4 * na][0, 0]


def _split_wait(handles, after, *, gather, name):
    na = len(handles)

    def body(*refs):
        src_refs, land_refs = refs[:na], refs[na:2 * na]
        sems = refs[2 * na:4 * na]
        for a in range(na):
            for cp in _split_copies(gather, src_refs[a], land_refs[a], sems[2 * a], sems[2 * a + 1]):
                cp.wait_send()
                cp.wait_recv()

    srcs = [h[2] for h in handles]
    lands = [h[3] for h in handles]
    sems = [s for h in handles for s in h[:2]]
    res = pl.pallas_call(
        body,
        name=name,
        out_shape=tuple(pltpu.HBM(a.shape, a.dtype) for a in srcs + lands),
        in_specs=[HBM] * (2 * na) + [SEM] * (2 * na) + [ANY],
        out_specs=tuple([HBM] * (2 * na)),
        input_output_aliases={i: i for i in range(2 * na)},
        compiler_params=pltpu.CompilerParams(has_side_effects=EFFECT),
    )(*srcs, *lands, *sems, after)
    return list(res[:na]), list(res[na:])


def _sum8(own, recv, *, name):
    _, r, w = recv.shape
    tr = _pick(r, (256, 128, 64, 32, 16, 8))

    def body(own_ref, a_ref, o_ref):
        acc = own_ref[...].astype(F32)
        for i in range(N_DEV - 1):
            acc = acc + a_ref[i].astype(F32)
        o_ref[...] = acc

    return pl.pallas_call(
        body,
        name=name,
        grid=(r // tr,),
        in_specs=[pl.BlockSpec((tr, w), lambda i: (i, 0)), pl.BlockSpec((N_DEV - 1, tr, w), lambda i: (0, i, 0))],
        out_specs=pl.BlockSpec((tr, w), lambda i: (i, 0)),
        out_shape=jax.ShapeDtypeStruct((r, w), F32),
        compiler_params=pltpu.CompilerParams(dimension_semantics=("parallel",), vmem_limit_bytes=_vmem(4 * N_DEV * tr * w * 4)),
    )(own, recv)


def _adamw(w, g, m, v, *, name):
    def fn(rows, _):
        wv, gv, mv, vv = rows
        m1 = ADAM_B1 * mv + (1.0 - ADAM_B1) * gv
        v1 = ADAM_B2 * vv + (1.0 - ADAM_B2) * jnp.square(gv)
        m_hat = m1 / (1.0 - ADAM_B1 ** ADAM_STEP)
        v_hat = v1 / (1.0 - ADAM_B2 ** ADAM_STEP)
        delta = -ADAM_LR * (m_hat / (jnp.sqrt(v_hat) + ADAM_EPS) + ADAM_WD * wv)
        return [delta, m1, v1], []

    c = w.shape[1]
    return _rowmap(fn, [w, g, m, v], [], [(c, F32)] * 3, [], name=name)


_SMALL = ("b_ada", "g_norm1", "g_norm2", "b_gate", "g_qa", "g_ka", "g_qb", "g_kb", "rpb", "loss")


def _pack_small(parts):
    flat = []
    for nme in _SMALL:
        a = parts[nme].reshape(-1).astype(F32)
        flat.append(jnp.pad(a, (0, (-a.shape[0]) % LANES)))
    flat = jnp.concatenate(flat)
    flat = jnp.pad(flat, (0, (-flat.shape[0]) % (LANES * LANES)))
    return flat.reshape(-1, LANES)


def _unpack_small(packed, shapes):
    flat = packed.reshape(-1)
    out, pos = {}, 0
    for nme in _SMALL:
        n = int(np.prod(shapes[nme]))
        out[nme] = flat[pos:pos + n].reshape(shapes[nme])
        pos += n + (-n) % LANES
    return out


def _to_class(a, d):
    t, w = a.shape
    return a if d == 1 else a.reshape(t // d, d, w).transpose(1, 0, 2).reshape(t, w)


def _from_class(a, d):
    t, w = a.shape
    return a if d == 1 else a.reshape(d, t // d, w).transpose(1, 0, 2).reshape(t, w)


def kernel(x, c, w_ada, b_ada, g_norm1, g_norm2, w_in, b_gate, g_qa, g_ka, g_qb, g_kb, rpb, w_proj_a, w_proj_b, w_o, w_ffn_in, w_ffn_out, loss_target, m_w_ada, m_b_ada, m_g_norm1, m_g_norm2, m_w_in, m_b_gate, m_g_qa, m_g_ka, m_g_qb, m_g_kb, m_rpb, m_w_proj_a, m_w_proj_b, m_w_o, m_w_ffn_in, m_w_ffn_out, v_w_ada, v_b_ada, v_g_norm1, v_g_norm2, v_w_in, v_b_gate, v_g_qa, v_g_ka, v_g_qb, v_g_kb, v_rpb, v_w_proj_a, v_w_proj_b, v_w_o, v_w_ffn_in, v_w_ffn_out):
    t, d = x.shape[1], x.shape[2]
    d_ff = w_ffn_out.shape[1] * N_DEV
    me = 4 * lax.axis_index("x") + 2 * lax.axis_index("y") + lax.axis_index("c")
    xt, tgt = x.reshape(t, d), loss_target.reshape(t, d)
    ones = _head_ones()

    shards = [s.astype(WIRE_DTYPE) for s in (w_in[0].T, w_ffn_in[0].T, w_proj_a[0].T, w_proj_b[0].T, w_o[0], w_ffn_out[0])]
    lands = [lax.dynamic_update_slice(lax.empty((N_DEV * s.shape[0], s.shape[1]), s.dtype), s, (me * s.shape[0], 0))
             for s in shards]

    c_all = _all_gather([jnp.pad(c, ((0, 7), (0, 0)))], name="gather_c")[0][::8]
    c_all = jnp.pad(c_all, ((0, LANES - N_DEV), (0, 0)))

    def mod_body(c_ref, w_ref, b_ref, o_ref, act_ref):
        act = _silu(c_ref[...])
        act_ref[...] = act
        hi, mid, lo = _split3(act)
        w = w_ref[...]
        whi, wmid, wlo = _split3(w)
        acc = _dot(hi, whi) + (_dot(hi, wmid) + _dot(mid, whi)) + (_dot(hi, wlo) + _dot(mid, wmid) + _dot(lo, whi))
        o_ref[...] = acc + b_ref[...]

    ncol = w_ada.shape[2]
    b_ada_mine = lax.dynamic_slice(b_ada, (0, me * ncol), (1, ncol))
    mod_part, c_act = pl.pallas_call(
        mod_body,
        name="ada_mod",
        out_shape=[jax.ShapeDtypeStruct((LANES, ncol), F32), jax.ShapeDtypeStruct((LANES, d), F32)],
        compiler_params=pltpu.CompilerParams(vmem_limit_bytes=_vmem(6 * d * ncol * 4)),
    )(c_all, w_ada[0], b_ada_mine)
    mod_all = _all_gather([mod_part[:N_DEV]], name="gather_mod")[0].reshape(N_DEV, N_DEV, ncol)
    mod = lax.dynamic_index_in_dim(mod_all, me, axis=1, keepdims=False).reshape(6, d)
    sh1, sc1, gt1, sh2, sc2, gt2 = [mod[i:i + 1] for i in range(6)]

    def norm_fwd(rows, vecs):
        (xv,), (g, sc, sh) = rows, vecs
        return [xv * _rms(xv) * g * (1.0 + sc) + sh], []

    w_handles, w_token = _split_start(shards, lands, gather=True, after=mod, name="gather_weights_start")
    (h,) = _rowmap(norm_fwd, [xt], [g_norm1 + w_token, sc1, sh1], [(d, MXU_DTYPE)], [], name="norm1")
    n_a, n_b = 3 * WA, 3 * WB
    (w_in_t,) = _split_wait(w_handles[:1], h, gather=True, name="gather_w_in_wait")[1]
    w_in_a, w_in_b, w_in_g = w_in_t[:n_a], w_in_t[n_a:n_a + n_b], w_in_t[n_a + n_b:]
    gates = _mm(h, w_in_g, tb=True, out_dtype=ACT_DTYPE, name="proj_gates")

    rot_c, rot_lo, rot_hi = _rot_tables(t)
    tile_g = lambda g, heads: jnp.tile(g, (1, heads))

    def qk_fwd(width, rotate):
        def fn(xv, rows, vecs):
            gq, gk, on = vecs
            outs = [xv]
            for i, g in enumerate((gq, gk)):
                xi = xv[:, i * width:(i + 1) * width]
                r = lax.rsqrt(_headsum(xi * xi, on) * (1.0 / HEAD_DIM) + EPS)
                yi = xi * r * g
                if rotate:
                    yi = _rot(yi, rows[0], rows[1], rows[2])
                outs.append(yi)
            outs.append(xv[:, 2 * width:])
            return outs, []
        return fn

    qkv_a, qa, ka, va = _mm_parts_rows(
        [(h, w_in_a, "nt")], qk_fwd(WA, False), [], [tile_g(g_qa, NA_HEADS), tile_g(g_ka, NA_HEADS), ones],
        [(3 * WA, ACT_DTYPE)] + [(WA, MXU_DTYPE)] * 3, [], name="proj_a_qknorm")
    qkv_b, qb, kb, vb = _mm_parts_rows(
        [(h, w_in_b, "nt")], qk_fwd(WB, True), [rot_c, rot_lo, rot_hi],
        [tile_g(g_qb, DIL_HEADS), tile_g(g_kb, DIL_HEADS), ones],
        [(3 * WB, ACT_DTYPE)] + [(WB, MXU_DTYPE)] * 3, [], name="proj_b_qknorm")

    bias_tab = _na_bias_table(rpb[0])
    o_a, lse_a = _attn_fwd(qa, ka, va, kind="na", bias=bias_tab, name="na_fwd")

    grp = []
    for g, (_, dil) in enumerate(DIL_CONFIGS):
        sl = slice(g * WB_OUT, (g + 1) * WB_OUT)
        qg, kg, vg = [_to_class(a[:, sl], dil) for a in (qb, kb, vb)]
        og, lg = _attn_fwd(qg, kg, vg, kind="dil", seg=t // dil, name=f"dil_fwd{g}")
        grp.append(dict(q=qg, k=kg, v=vg, o=_from_class(og, dil), lse=_from_class(lg, dil), lse_c=lg, dil=dil))

    def merge_fwd(rows, _):
        o0, o1, o2, l0, l1, l2 = rows
        mx = jnp.maximum(jnp.maximum(l0, l1), l2)
        e0, e1, e2 = jnp.exp(l0 - mx), jnp.exp(l1 - mx), jnp.exp(l2 - mx)
        s = e0 + e1 + e2
        return [(e0 / s) * o0 + (e1 / s) * o1 + (e2 / s) * o2], []

    (o_b,) = _rowmap(merge_fwd, [gr["o"] for gr in grp] + [gr["lse"] for gr in grp], [], [(WB_OUT, F32)], [], name="dil_merge")

    w_pa_t, w_pb_t, w_o_f = _split_wait(w_handles[2:5], o_b, gather=True, name="gather_w_out_wait")[1]
    pa = _mm(o_a, w_pa_t, tb=True, out_dtype=ACT_DTYPE, name="proj_out_a")
    pb = _mm(o_b, w_pb_t, tb=True, out_dtype=ACT_DTYPE, name="proj_out_b")

    def gate_fwd(rows, vecs):
        gv, pav, pbv = rows
        sg = jax.nn.sigmoid(gv + vecs[0])
        return [sg[:, :d] * pav + sg[:, d:] * pbv], []

    (merged,) = _rowmap(gate_fwd, [gates, pa, pb], [b_gate], [(d, MXU_DTYPE)], [], name="gate_merge")
    def resid_norm(av, rows, vecs):
        (xv,), (gt, g, sc, sh) = rows, vecs
        x1v = xv + gt * av
        return [av, x1v, x1v * _rms(x1v) * g * (1.0 + sc) + sh], []

    att, x1, h2 = _mm_parts_rows([(merged, w_o_f)], resid_norm, [xt], [gt1, g_norm2, sc2, sh2],
                           [(d, F32), (d, F32), (d, MXU_DTYPE)], [], name="proj_o_resid_norm2")

    w_ffn_in_t, w_ffn_out_f = _split_wait([w_handles[1], w_handles[5]], h2, gather=True, name="gather_w_ffn_wait")[1]
    w_ffn_a, w_ffn_up = w_ffn_in_t[:d_ff], w_ffn_in_t[d_ff:]

    def swiglu_fwd(prods, _):
        a, up = prods
        return [a, up, _silu(a) * up]

    ua, uu, f = _mm_ew(h2, [w_ffn_a, w_ffn_up], swiglu_fwd, [], [ACT_DTYPE, ACT_DTYPE, MXU_DTYPE], name="ffn_in_swiglu")

    def loss_fn(yv, rows, vecs):
        (x1v, tv), gt = rows, vecs[0]
        err = x1v + gt * yv - tv
        dout = err * (1.0 / d)
        return [dout, dout * gt], [_colsum(err * err), _colsum(dout * yv)]

    dout, dy2, err2, dgt2 = _mm_parts_rows([(f, w_ffn_out_f)], loss_fn, [x1, tgt], [gt2], [(d, F32), (d, MXU_DTYPE)],
                                           [d, d], name="ffn_out_loss")

    dw_ffn_out = _mm(f, dy2, ta=True, out_dtype=WIRE_DTYPE, name="wgrad_ffn_out")
    def swiglu_bwd(prods, rows):
        (dfv,), (a, up) = prods, rows
        sg = jax.nn.sigmoid(a)
        return [dfv * up * (sg * (1.0 + a * (1.0 - sg))), dfv * (a * sg)]

    da, dup = _mm_ew(dy2, [w_ffn_out_f], swiglu_bwd, [ua, uu], [MXU_DTYPE, MXU_DTYPE], name="dgrad_ffn_out_swiglu_bwd")
    dw_ffn_in_t = _mm(da, h2, ta=True, into=(lax.empty((2 * d_ff, d), WIRE_DTYPE), 0), name="wgrad_ffn_in_a")
    dw_ffn_in_t = _mm(dup, h2, ta=True, into=(dw_ffn_in_t, d_ff), name="wgrad_ffn_in_up")
    land7 = lambda a: lax.empty((N_DEV - 1, a.shape[0] // N_DEV, a.shape[1]), a.dtype)
    own_block = lambda a: lax.dynamic_slice(a, (me * (a.shape[0] // N_DEV), 0), (a.shape[0] // N_DEV, a.shape[1]))
    g_ffn = [dw_ffn_in_t, dw_ffn_out]
    h_ffn, tok_ffn = _split_start(g_ffn, [land7(a) for a in g_ffn], gather=False, name="exchange_ffn_start")
    def norm_bwd(dh, xv, g, sc):
        r = _rms(xv)
        xh = xv * r
        dxh = dh * g * (1.0 + sc)
        dxv = r * (dxh - xh * jnp.mean(dxh * xh, axis=-1, keepdims=True))
        return dxv, [_colsum(dh), _colsum(dh * xh * g), _colsum(dh * xh * (1.0 + sc))]

    def norm2_bwd(dhv, rows, vecs):
        (x1v, dov, av), (g, sc, gt) = rows, vecs
        dxv, sums = norm_bwd(dhv, x1v, g, sc)
        dx1v = dov + dxv
        return [dx1v, dx1v * gt], sums + [_colsum(dx1v * av)]

    dx1, datt, dsh2, dsc2, dg2, dgt1 = _mm_parts_rows(
        [(da, w_ffn_a), (dup, w_ffn_up)], norm2_bwd, [x1, dout, att], [g_norm2 + tok_ffn, sc2, gt1],
        [(d, F32), (d, MXU_DTYPE)], [d] * 4, name="dgrad_ffn_in_norm2_bwd")
    dw_o = _mm(merged, datt, ta=True, out_dtype=WIRE_DTYPE, name="wgrad_o")
    def gate_bwd(dm, rows, vecs):
        gv, pav, pbv = rows
        sg = jax.nn.sigmoid(gv + vecs[0])
        ga, gb = sg[:, :d], sg[:, d:]
        dgp = jnp.concatenate([dm * pav * ga * (1.0 - ga), dm * pbv * gb * (1.0 - gb)], axis=1)
        return [dm * ga, dm * gb, dgp], [_colsum(dgp)]

    dpa, dpb, dgates, db_gate = _mm_parts_rows(
        [(datt, w_o_f.T)], gate_bwd, [gates, pa, pb], [b_gate],
        [(d, MXU_DTYPE), (d, MXU_DTYPE), (2 * d, MXU_DTYPE)], [2 * d], name="dgrad_o_gate_bwd")
    dw_pa_t = _mm(dpa, o_a, ta=True, out_dtype=WIRE_DTYPE, name="wgrad_proj_a")
    dw_pb_t = _mm(dpb, o_b, ta=True, out_dtype=WIRE_DTYPE, name="wgrad_proj_b")
    g_out = [dw_pa_t, dw_pb_t, dw_o]
    h_out, tok_out = _split_start(g_out, [land7(a) for a in g_out], gather=False, name="exchange_out_start")
    def delta_a(doa, rows, vecs):
        return [doa, _headsum(doa * rows[0], vecs[0])], []

    do_a, dterm_a = _mm_parts_rows([(dpa, w_pa_t)], delta_a, [o_a], [ones + tok_out.astype(ones.dtype)],
                                   [(WA, F32), (WA, F32)], [], name="dgrad_proj_a_delta")
    dqa, dka, dva, dbias = _attn_bwd(qa, ka, va, do_a, dterm_a, lse_a, kind="na", bias=bias_tab, name="na_bwd")
    g_rpb = _na_bias_grad(dbias)

    def merge_bwd(dob, rows, vecs):
        o0, o1, o2, l0, l1, l2 = rows
        on = vecs[0]
        mx = jnp.maximum(jnp.maximum(l0, l1), l2)
        e0, e1, e2 = jnp.exp(l0 - mx), jnp.exp(l1 - mx), jnp.exp(l2 - mx)
        s = e0 + e1 + e2
        ws = [e0 / s, e1 / s, e2 / s]
        dws = [_headsum(dob * o, on) for o in (o0, o1, o2)]
        mean = ws[0] * dws[0] + ws[1] * dws[1] + ws[2] * dws[2]
        return [w * dob for w in ws] + [w * mean for w in ws], []

    mb = _mm_parts_rows([(dpb, w_pb_t)], merge_bwd, [gr["o"] for gr in grp] + [gr["lse"] for gr in grp], [ones],
                        [(WB_OUT, F32)] * 6, [], name="dgrad_proj_b_merge_bwd")
    dqb, dkb, dvb = [], [], []
    for g, gr in enumerate(grp):
        dil = gr["dil"]
        dq, dk, dv = _attn_bwd(gr["q"], gr["k"], gr["v"], _to_class(mb[g], dil), _to_class(mb[3 + g], dil), gr["lse_c"],
                               kind="dil", seg=t // dil, name=f"dil_bwd{g}")
        dqb.append(_from_class(dq, dil))
        dkb.append(_from_class(dk[0][0][0], dil))
        dvb.append(_from_class(dv[0][0][0], dil))

    def qk_bwd(width, rotate, nparts):
        def fn(rows, vecs):
            gq, gk, on = vecs
            xv = rows[0]
            pos = 1
            if rotate:
                rc, rlo, rhi = rows[1:4]
                pos = 4
            cat = lambda parts: parts[0] if len(parts) == 1 else jnp.concatenate(parts, axis=1)
            ends = np.cumsum((pos,) + nparts)
            dq, dk, dv = [cat(rows[ends[i]:ends[i + 1]]) for i in range(3)]
            outs, sums = [], []
            for i, (dy, g) in enumerate(((dq, gq), (dk, gk))):
                if rotate:
                    dy = _rot(dy, rc, -rlo, -rhi)
                xi = xv[:, i * width:(i + 1) * width]
                r = lax.rsqrt(_headsum(xi * xi, on) * (1.0 / HEAD_DIM) + EPS)
                xh = xi * r
                dxh = dy * g
                outs.append(r * (dxh - xh * (_headsum(dxh * xh, on) * (1.0 / HEAD_DIM))))
                sums.append(_colsum(dy * xh))
            return [jnp.concatenate(outs + [dv], axis=1)], sums
        return fn

    dqkv_a, dg_qa, dg_ka = _rowmap(qk_bwd(WA, False, (1, len(dka), len(dva))), [qkv_a, dqa] + dka + dva,
                                   [tile_g(g_qa, NA_HEADS), tile_g(g_ka, NA_HEADS), ones],
                                   [(3 * WA, MXU_DTYPE)], [WA, WA], name="qknorm_a_bwd")
    dqkv_b, dg_qb, dg_kb = _rowmap(qk_bwd(WB, True, (3, 3, 3)), [qkv_b, rot_c, rot_lo, rot_hi] + dqb + dkb + dvb,
                                   [tile_g(g_qb, DIL_HEADS), tile_g(g_kb, DIL_HEADS), ones],
                                   [(3 * WB, MXU_DTYPE)], [WB, WB], name="qknorm_b_bwd")

    dw_in_t = _mm(dqkv_a, h, ta=True, into=(lax.empty((n_a + n_b + 2 * d, d), WIRE_DTYPE), 0), name="wgrad_in_a")
    dw_in_t = _mm(dqkv_b, h, ta=True, into=(dw_in_t, n_a), name="wgrad_in_b")
    dw_in_t = _mm(dgates, h, ta=True, into=(dw_in_t, n_a + n_b), name="wgrad_in_gates")
    h_in, tok_in = _split_start([dw_in_t], [land7(dw_in_t)], gather=False, name="exchange_in_start")
    def norm1_bwd(dhv, rows, vecs):
        xv, dx1v = rows
        dxv, sums = norm_bwd(dhv, xv, vecs[0], vecs[1])
        return [dx1v + dxv], sums

    grad_x, dsh1, dsc1, dg1 = _mm_parts_rows(
        [(dqkv_a, w_in_a), (dqkv_b, w_in_b), (dgates, w_in_g)], norm1_bwd, [xt, dx1], [g_norm1 + tok_in, sc1],
        [(d, F32)], [d] * 3, name="dgrad_in_norm1_bwd")

    heads_sum = lambda a, heads: a.reshape(heads, HEAD_DIM).sum(axis=0)
    dmod = jnp.concatenate([dsh1, dsc1, dgt1, dsh2, dsc2, dgt2], axis=1)
    local_small = _pack_small(dict(
        b_ada=dmod, g_norm1=dg1, g_norm2=dg2, b_gate=db_gate, g_qa=heads_sum(dg_qa, NA_HEADS),
        g_ka=heads_sum(dg_ka, NA_HEADS), g_qb=heads_sum(dg_qb, DIL_HEADS), g_kb=heads_sum(dg_kb, DIL_HEADS),
        rpb=g_rpb, loss=(0.5 / d) * jnp.sum(err2)))
    srows = local_small.shape[0]
    small_all = _all_gather([local_small], name="gather_small")[0].reshape(N_DEV, srows, LANES)
    small_sum = _sum8(small_all[0], small_all[1:], name="sum_small")
    small_shapes = dict(b_ada=b_ada.shape, g_norm1=g_norm1.shape, g_norm2=g_norm2.shape, b_gate=b_gate.shape,
                        g_qa=g_qa.shape, g_ka=g_ka.shape, g_qb=g_qb.shape, g_kb=g_kb.shape, rpb=rpb.shape, loss=())
    small_w = dict(b_ada=b_ada, g_norm1=g_norm1, g_norm2=g_norm2, b_gate=b_gate, g_qa=g_qa, g_ka=g_ka, g_qb=g_qb,
                   g_kb=g_kb, rpb=rpb, loss=jnp.zeros((), F32))
    small_m = dict(b_ada=m_b_ada, g_norm1=m_g_norm1, g_norm2=m_g_norm2, b_gate=m_b_gate, g_qa=m_g_qa, g_ka=m_g_ka,
                   g_qb=m_g_qb, g_kb=m_g_kb, rpb=m_rpb, loss=jnp.zeros((), F32))
    small_v = dict(b_ada=v_b_ada, g_norm1=v_g_norm1, g_norm2=v_g_norm2, b_gate=v_b_gate, g_qa=v_g_qa, g_ka=v_g_ka,
                   g_qb=v_g_qb, g_kb=v_g_kb, rpb=v_rpb, loss=jnp.zeros((), F32))
    s_delta, s_m, s_v = _adamw(_pack_small(small_w), small_sum, _pack_small(small_m), _pack_small(small_v), name="adamw_small")
    gs = _unpack_small(small_sum, small_shapes)
    ds_, ms_, vs_ = [_unpack_small(a, small_shapes) for a in (s_delta, s_m, s_v)]

    dmod_all = small_all[:, :6 * d // LANES].reshape(N_DEV, 6 * d)
    dmod_mine = jnp.pad(lax.dynamic_slice(dmod_all, (0, me * ncol), (N_DEV, ncol)), ((0, LANES - N_DEV), (0, 0)))

    def wada_body(c_ref, dm_ref, o_ref):
        chi, cmid, clo = _split3(c_ref[...])
        dhi, dmid, dlo = _split3(dm_ref[...])
        o_ref[...] = (_dot_tn(chi, dhi) + (_dot_tn(chi, dmid) + _dot_tn(cmid, dhi))
                      + (_dot_tn(chi, dlo) + _dot_tn(cmid, dmid) + _dot_tn(clo, dhi)))

    g_w_ada = pl.pallas_call(
        wada_body,
        name="wgrad_ada",
        out_shape=jax.ShapeDtypeStruct((d, ncol), F32),
        compiler_params=pltpu.CompilerParams(vmem_limit_bytes=_vmem(4 * d * ncol * 4)),
    )(c_act, dmod_mine)

    sent, recv = _split_wait(h_in + h_ffn + h_out, small_sum, gather=False, name="exchange_wait")
    names = ("w_in", "w_ffn_in", "w_ffn_out", "w_proj_a", "w_proj_b", "w_o")
    transposed = (True, True, False, True, True, False)
    big_g = {}
    for nme, own, r, tr in zip(names, sent, recv, transposed):
        s = _sum8(own_block(own), r, name=f"sum_{nme}")
        big_g[nme] = s.T if tr else s
    big_g["w_ada"] = g_w_ada
    big_w = dict(w_ada=w_ada, w_in=w_in, w_proj_a=w_proj_a, w_proj_b=w_proj_b, w_o=w_o, w_ffn_in=w_ffn_in, w_ffn_out=w_ffn_out)
    big_m = dict(w_ada=m_w_ada, w_in=m_w_in, w_proj_a=m_w_proj_a, w_proj_b=m_w_proj_b, w_o=m_w_o, w_ffn_in=m_w_ffn_in, w_ffn_out=m_w_ffn_out)
    big_v = dict(w_ada=v_w_ada, w_in=v_w_in, w_proj_a=v_w_proj_a, w_proj_b=v_w_proj_b, w_o=v_w_o, w_ffn_in=v_w_ffn_in, w_ffn_out=v_w_ffn_out)
    grads, deltas, new_m, new_v = {}, {}, {}, {}
    for nme in big_w:
        dl, m1, v1 = _adamw(big_w[nme][0], big_g[nme], big_m[nme][0], big_v[nme][0], name=f"adamw_{nme}")
        grads[nme], deltas[nme], new_m[nme], new_v[nme] = big_g[nme][None], dl[None], m1[None], v1[None]
    for nme in _SMALL[:-1]:
        grads[nme], deltas[nme], new_m[nme], new_v[nme] = gs[nme], ds_[nme], ms_[nme], vs_[nme]

    order = ("w_ada", "b_ada", "g_norm1", "g_norm2", "w_in", "b_gate", "g_qa", "g_ka", "g_qb", "g_kb", "rpb",
             "w_proj_a", "w_proj_b", "w_o", "w_ffn_in", "w_ffn_out")
    return (gs["loss"], grad_x[None], *[grads[n] for n in order], *[deltas[n] for n in order],
            *[new_m[n] for n in order], *[new_v[n] for n in order])
```

```python
import functools

import numpy as np
import jax
import jax.numpy as jnp
from jax import lax
from jax.experimental import pallas as pl
from jax.experimental.pallas import tpu as pltpu

F32 = jnp.float32
MXU_DTYPE = jnp.bfloat16
WIRE_DTYPE = jnp.bfloat16
ACT_DTYPE = jnp.bfloat16

HEAD_DIM = 64
GRID_W = 64
NA_HEADS = 8
NA_KH = 8
NA_KW = 16
DIL_CONFIGS = ((128, 1), (512, 4), (2048, 16))
DIL_HEADS_PER_GROUP = 4
DIL_HEADS = DIL_HEADS_PER_GROUP * len(DIL_CONFIGS)
DIL_HALF = 64
ROT_DIM = HEAD_DIM // 4
ROPE_THETA = 500000.0
EPS = 1e-6
NEG_INF = -1e30
WA = NA_HEADS * HEAD_DIM
WB = DIL_HEADS * HEAD_DIM
WB_OUT = DIL_HEADS_PER_GROUP * HEAD_DIM
ADAM_LR = 0.001
ADAM_B1 = 0.9
ADAM_B2 = 0.999
ADAM_EPS = 1e-08
ADAM_WD = 0.01
ADAM_STEP = 10

N_DEV = 8
LANES = 128
VMEM_CAP = 60 * 2**20
VMEM_FLOOR = 56 * 2**20
MESH = pl.DeviceIdType.MESH
ANY = pl.BlockSpec(memory_space=pl.ANY)


def _vmem(nbytes):
    return int(min(VMEM_CAP, max(VMEM_FLOOR, nbytes * 5 // 4 + 4 * 2**20)))


def _pick(dim, cands):
    for c in cands:
        if c <= dim and dim % c == 0:
            return c
    return dim


def _nbytes(shape, dtype):
    return int(np.prod(shape)) * jnp.dtype(dtype).itemsize


def _dot(a, b, dims=((1,), (0,))):
    return lax.dot_general(a.astype(MXU_DTYPE), b.astype(MXU_DTYPE), (dims, ((), ())), preferred_element_type=F32)


def _dot_nt(a, b):
    return _dot(a, b, ((1,), (1,)))


def _dot_tn(a, b):
    return _dot(a, b, ((0,), (0,)))


def _split3(a):
    hi = a.astype(jnp.bfloat16)
    r1 = a - hi.astype(F32)
    mid = r1.astype(jnp.bfloat16)
    lo = (r1 - mid.astype(F32)).astype(jnp.bfloat16)
    return hi, mid, lo


def _silu(x):
    return x * jax.nn.sigmoid(x)


def _divisors(dim, unit):
    return [c for c in range(unit, dim + 1, unit) if dim % c == 0] or [dim]


def _mm_tiles(m, n, kdim, a_item, b_item, o_item, row_off=0):
    step_us, hbm_bytes_per_us, flops_per_us, budget = 0.35, 3.0e6, 8.0e8, 40 * 2**20
    best = None
    for tm in _divisors(m, LANES):
        for tn in _divisors(n, LANES):
            for tk in _divisors(kdim, LANES):
                if row_off % tm:
                    continue
                gm, gn, gk = m // tm, n // tn, kdim // tk
                vmem = 2 * (tm * tk * a_item + tk * tn * b_item + tm * tn * o_item) + 2 * (tm * tk + tk * tn)
                vmem += tm * tn * 4 * ((1 if gk > 1 else 0) + 1)
                if vmem > budget:
                    continue
                a_reads = m * kdim * a_item * (gn if gk > 1 else 1)
                traffic = a_reads + kdim * n * b_item * gm + m * n * o_item
                cost = gm * gn * gk * step_us + max(traffic / hbm_bytes_per_us, 2.0 * m * n * kdim / flops_per_us)
                if best is None or cost < best[0]:
                    best = (cost, tm, tn, tk)
    return best[1:]


def _mm(a, b, *, name, ta=False, tb=False, out_dtype=F32, into=None):
    if ta:
        kdim, m = a.shape
    else:
        m, kdim = a.shape
    n = b.shape[0] if tb else b.shape[1]
    assert b.shape[1 if tb else 0] == kdim
    buf, row_off = into if into is not None else (None, 0)
    if buf is not None:
        out_dtype = buf.dtype
    tm, tn, tk = _mm_tiles(m, n, kdim, a.dtype.itemsize, b.dtype.itemsize, jnp.dtype(out_dtype).itemsize, row_off)
    gm, gn, gk = m // tm, n // tn, kdim // tk
    ob = row_off // tm

    a_spec = pl.BlockSpec((tk, tm), lambda i, j, k: (k, i)) if ta else pl.BlockSpec((tm, tk), lambda i, j, k: (i, k))
    b_spec = pl.BlockSpec((tn, tk), lambda i, j, k: (j, k)) if tb else pl.BlockSpec((tk, tn), lambda i, j, k: (k, j))
    o_spec = pl.BlockSpec((tm, tn), lambda i, j, k: (i + ob, j))
    a_dims = (0,) if ta else (1,)
    b_dims = (1,) if tb else (0,)

    def body(a_ref, b_ref, *rest):
        o_ref, scratch = rest[-1 - (gk > 1)], rest[-(gk > 1):] if gk > 1 else ()
        if gk == 1:
            o_ref[...] = _dot(a_ref[...], b_ref[...], (a_dims, b_dims)).astype(o_ref.dtype)
            return
        (acc_ref,) = scratch
        k = pl.program_id(2)

        @pl.when(k == 0)
        def _():
            acc_ref[...] = jnp.zeros_like(acc_ref)

        acc_ref[...] += _dot(a_ref[...], b_ref[...], (a_dims, b_dims))

        @pl.when(k == gk - 1)
        def _():
            o_ref[...] = acc_ref[...].astype(o_ref.dtype)

    est = 2 * (tm * tk * a.dtype.itemsize + tk * tn * b.dtype.itemsize + tm * tn * jnp.dtype(out_dtype).itemsize)
    est += tm * tn * 4 + 2 * (tm * tk + tk * tn) * 2
    return pl.pallas_call(
        body,
        name=name,
        grid=(gm, gn, gk),
        in_specs=[a_spec, b_spec] + ([ANY] if buf is not None else []),
        out_specs=o_spec,
        out_shape=jax.ShapeDtypeStruct((m, n) if buf is None else buf.shape, out_dtype),
        input_output_aliases={2: 0} if buf is not None else {},
        scratch_shapes=[pltpu.VMEM((tm, tn), F32)] if gk > 1 else [],
        compiler_params=pltpu.CompilerParams(
            dimension_semantics=("parallel", "parallel", "arbitrary"), vmem_limit_bytes=_vmem(est)
        ),
    )(*((a, b) if buf is None else (a, b, buf)))


def _resident(shape):
    return pl.BlockSpec(shape, lambda i: (0,) * len(shape), pipeline_mode=pl.Buffered(1))


def _row_tile(m, fixed_bytes, bytes_per_row, budget=50 * 2**20):
    fits = [tm for tm in _divisors(m, LANES) if fixed_bytes + tm * bytes_per_row <= budget]
    return max(fits) if fits else _divisors(m, LANES)[0]


def _mm_parts_rows(parts, fn, rows, vecs, outs, reds, *, name):
    parts = [(p[0], p[1], len(p) > 2) for p in parts]
    m, n = parts[0][0].shape[0], parts[0][1].shape[0 if parts[0][2] else 1]
    npart, nr, nv, no = len(parts), len(rows), len(vecs), len(outs)
    row_bytes = sum(r.shape[1] * r.dtype.itemsize for r in rows) + sum(w * jnp.dtype(dt).itemsize for (w, dt) in outs)
    a_row_bytes = sum(a.shape[1] * a.dtype.itemsize for a, _, _ in parts)
    fixed = sum(_nbytes(b.shape, b.dtype) for _, b, _ in parts)
    per_row = 2 * (a_row_bytes + row_bytes) + n * 4 * 5
    tm = _row_tile(m, fixed, per_row)
    sub = min(tm, 2 * LANES)

    def body(*refs):
        ab = refs[:2 * npart]
        row_refs, vec_refs = refs[2 * npart:2 * npart + nr], refs[2 * npart + nr:2 * npart + nr + nv]
        out_refs = refs[2 * npart + nr + nv:2 * npart + nr + nv + no]
        red_refs = refs[2 * npart + nr + nv + no:]
        if red_refs:
            @pl.when(pl.program_id(0) == 0)
            def _():
                for ref in red_refs:
                    ref[...] = jnp.zeros_like(ref)

        vecs_v = [v[...] for v in vec_refs]
        for s0 in range(0, tm, sub):
            sl = slice(s0, s0 + sub)
            r = None
            for p, (_, _, nt) in enumerate(parts):
                term = (_dot_nt if nt else _dot)(ab[2 * p][sl, :], ab[2 * p + 1][...])
                r = term if r is None else r + term
            o, rd = fn(r, [x[sl, :].astype(F32) for x in row_refs], vecs_v)
            for ref, val in zip(out_refs, o):
                ref[sl, :] = val.astype(ref.dtype)
            for ref, val in zip(red_refs, rd):
                ref[...] += val

    in_specs, operands = [], []
    for a, b, _ in parts:
        in_specs += [pl.BlockSpec((tm, a.shape[1]), lambda i: (i, 0)), _resident(b.shape)]
        operands += [a, b]
    in_specs += [pl.BlockSpec((tm, r.shape[1]), lambda i: (i, 0)) for r in rows]
    in_specs += [pl.BlockSpec(v.shape, functools.partial(lambda nd, i: (0,) * nd, v.ndim)) for v in vecs]
    out_specs = [pl.BlockSpec((tm, w), lambda i: (i, 0)) for (w, _) in outs]
    out_specs += [pl.BlockSpec((1, w), lambda i: (0, 0)) for w in reds]
    out_shape = [jax.ShapeDtypeStruct((m, w), dt) for (w, dt) in outs] + [jax.ShapeDtypeStruct((1, w), F32) for w in reds]
    return pl.pallas_call(
        body,
        name=name,
        grid=(m // tm,),
        in_specs=in_specs,
        out_specs=out_specs,
        out_shape=out_shape,
        compiler_params=pltpu.CompilerParams(dimension_semantics=("arbitrary",), vmem_limit_bytes=_vmem(fixed + tm * per_row)),
    )(*operands, *rows, *vecs)


def _mm_ew(a, bs, fn, rows, outs, *, name):
    m, kdim = a.shape
    n = bs[0].shape[0]
    nb, nr, no = len(bs), len(rows), len(outs)
    cw = _pick(n, (2 * LANES, LANES))
    fixed = nb * n * kdim * bs[0].dtype.itemsize
    per_row = 2 * (kdim * a.dtype.itemsize + n * (sum(r.dtype.itemsize for r in rows) + sum(jnp.dtype(dt).itemsize for dt in outs)))
    per_row += cw * 4 * 4 * (nb + 4)
    tm = _row_tile(m, fixed, per_row)

    def body(*refs):
        a_ref, b_refs = refs[0], refs[1:1 + nb]
        row_refs, out_refs = refs[1 + nb:1 + nb + nr], refs[1 + nb + nr:]
        av = a_ref[...]
        for c0 in range(0, n, cw):
            cols = slice(c0, c0 + cw)
            o = fn([_dot_nt(av, b[cols, :]) for b in b_refs], [x[:, cols].astype(F32) for x in row_refs])
            for ref, val in zip(out_refs, o):
                ref[:, cols] = val.astype(ref.dtype)

    tile = pl.BlockSpec((tm, n), lambda i: (i, 0))
    return pl.pallas_call(
        body,
        name=name,
        grid=(m // tm,),
        in_specs=[pl.BlockSpec((tm, kdim), lambda i: (i, 0))] + [_resident((n, kdim))] * nb + [tile] * nr,
        out_specs=[tile] * no,
        out_shape=[jax.ShapeDtypeStruct((m, n), dt) for dt in outs],
        compiler_params=pltpu.CompilerParams(dimension_semantics=("parallel",), vmem_limit_bytes=_vmem(fixed + tm * per_row)),
    )(a, *bs, *rows)


def _rowmap(fn, rows, vecs, outs, reds, *, name, tm=None):
    norm = []
    for r in rows:
        if not isinstance(r, tuple):
            norm.append((r, r.shape[1], 0, None))
        elif len(r) == 2:
            norm.append((r[0], r[0].shape[2], 0, r[1]))
        else:
            norm.append((r[0], r[1], r[2], None))
    rows = norm
    t = rows[0][0].shape[-2]
    if tm is None:
        per_row = 2 * sum(w * a.dtype.itemsize for (a, w, _, _) in rows) + 2 * sum(w * jnp.dtype(d).itemsize for (w, d) in outs)
        per_row += 3 * 4 * max([w for (_, w, _, _) in rows] + [w for (w, _) in outs])
        tm = max(8, min(1024, (40 * 2**20) // per_row))
    tm = _pick(t, tuple(c for c in (1024, 512, 256, 128, 64, 32, 16, 8) if c <= tm))
    nr, nv, no = len(rows), len(vecs), len(outs)

    def body(*refs):
        row_refs, vec_refs = refs[:nr], refs[nr:nr + nv]
        out_refs, red_refs = refs[nr + nv:nr + nv + no], refs[nr + nv + no:]
        o, rd = fn([r[...].astype(F32) for r in row_refs], [v[...] for v in vec_refs])
        for ref, val in zip(out_refs, o):
            ref[...] = val.astype(ref.dtype)
        if red_refs:
            @pl.when(pl.program_id(0) == 0)
            def _():
                for ref in red_refs:
                    ref[...] = jnp.zeros_like(ref)

            for ref, val in zip(red_refs, rd):
                ref[...] += val

    in_specs = [pl.BlockSpec((tm, w), functools.partial(lambda cb, i: (i, cb), cb)) if lead is None
                else pl.BlockSpec((None, tm, w), functools.partial(lambda ld, i: (ld, i, 0), lead)) for (_, w, cb, lead) in rows]
    in_specs += [pl.BlockSpec(v.shape, functools.partial(lambda nd, i: (0,) * nd, v.ndim)) for v in vecs]
    out_specs = [pl.BlockSpec((tm, w), lambda i: (i, 0)) for (w, _) in outs]
    out_specs += [pl.BlockSpec((1, w), lambda i: (0, 0)) for w in reds]
    out_shape = [jax.ShapeDtypeStruct((t, w), d) for (w, d) in outs]
    out_shape += [jax.ShapeDtypeStruct((1, w), F32) for w in reds]
    est = 2 * sum(tm * w * a.dtype.itemsize for (a, w, _, _) in rows) + 2 * sum(_nbytes(v.shape, v.dtype) for v in vecs)
    est += 2 * sum(tm * w * jnp.dtype(d).itemsize for (w, d) in outs)
    est += 6 * tm * max([w for (_, w, _, _) in rows] + [w for (w, _) in outs]) * 4
    return pl.pallas_call(
        body,
        name=name,
        grid=(t // tm,),
        in_specs=in_specs,
        out_specs=out_specs,
        out_shape=out_shape,
        compiler_params=pltpu.CompilerParams(dimension_semantics=("arbitrary",), vmem_limit_bytes=_vmem(est)),
    )(*[r[0] for r in rows], *vecs)


def _colsum(v):
    return jnp.sum(v, axis=0, keepdims=True)


def _head_ones():
    i = np.arange(LANES)
    return jnp.asarray((i[:, None] // HEAD_DIM) == (i[None, :] // HEAD_DIM), MXU_DTYPE)


def _headsum(y, ones):
    parts = []
    for j in range(y.shape[1] // LANES):
        c = y[:, j * LANES:(j + 1) * LANES]
        hi = c.astype(MXU_DTYPE)
        lo = c - hi.astype(F32)
        parts.append(_dot(hi, ones) + _dot(lo, ones))
    return parts[0] if len(parts) == 1 else jnp.concatenate(parts, axis=1)


def _rot(y, c, s_lo, s_hi):
    parts = []
    for j in range(y.shape[1] // LANES):
        yc = y[:, j * LANES:(j + 1) * LANES]
        parts.append(yc * c + pltpu.roll(yc, LANES - ROT_DIM // 2, 1) * s_lo + pltpu.roll(yc, ROT_DIM // 2, 1) * s_hi)
    return parts[0] if len(parts) == 1 else jnp.concatenate(parts, axis=1)


def _rot_tables(t):
    half = ROT_DIM // 2
    inv_freq = ROPE_THETA ** (-(jnp.arange(half, dtype=F32) * 2.0) / ROT_DIM)
    ang = jnp.arange(t).astype(F32)[:, None] * inv_freq[None, :]
    cos, sin = jnp.cos(ang), jnp.sin(ang)
    z = lambda w: jnp.zeros((t, w), F32)
    c = jnp.concatenate([cos, cos, jnp.ones((t, HEAD_DIM - ROT_DIM), F32)], axis=1)
    s_lo = jnp.concatenate([-sin, z(HEAD_DIM - half)], axis=1)
    s_hi = jnp.concatenate([z(half), sin, z(HEAD_DIM - ROT_DIM)], axis=1)
    return [jnp.tile(a, (1, LANES // HEAD_DIM)) for a in (c, s_lo, s_hi)]


def _rms(x):
    return lax.rsqrt(jnp.mean(x * x, axis=-1, keepdims=True) + EPS)


def _window(kind, n, bq, t, seg):
    if kind == "na":
        rows = t // GRID_W
        rs = jnp.clip(n - NA_KH // 2, 0, rows - NA_KH)
        return rs
    nk = bq + 2 * DIL_HALF
    return jnp.clip(n * bq - DIL_HALF, 0, t - nk)


def _dil_mask(n, bq, nk, ws, seg):
    qi = n * bq + lax.broadcasted_iota(jnp.int32, (bq, nk), 0)
    ki = ws + lax.broadcasted_iota(jnp.int32, (bq, nk), 1)
    shift = int(np.log2(seg))
    return (jnp.abs(ki - qi) <= DIL_HALF) & ((ki >> shift) == (qi >> shift))


HS = 4
QW = HS * HEAD_DIM


def _head_of_lane(width=QW):
    return lax.broadcasted_iota(jnp.int32, (1, width), 1) // HEAD_DIM


def _stack_heads(a):
    head = _head_of_lane()
    return jnp.concatenate([jnp.where(head == e, a, jnp.zeros_like(a)) for e in range(HS)], axis=0)


def _unstack_heads(a, bq):
    head = _head_of_lane()
    out = jnp.zeros((bq, QW), a.dtype)
    for e in range(HS):
        out = jnp.where(head == e, a[e * bq:(e + 1) * bq], out)
    return out


def _stack_cols(blk, bq):
    head = _head_of_lane()
    return jnp.concatenate(
        [jnp.max(jnp.where(head == e, blk, -jnp.inf), axis=1, keepdims=True) for e in range(HS)], axis=0)


def _attn_geometry(kind):
    if kind == "na":
        return GRID_W, NA_KH * GRID_W, 16
    bq = 128
    return bq, bq + 2 * DIL_HALF, 8


def _attn_scores(kind, n, bq, nk, t, seg, qs, k_ref, b_ref):
    scale = HEAD_DIM ** -0.5
    if kind == "na":
        rs = _window(kind, n, bq, t, seg)
        ws = pl.multiple_of(rs * GRID_W, GRID_W)
        ro0 = rs - n + (NA_KH - 1)
        s = _dot_nt(qs, k_ref[pl.ds(ws, nk), :]) * scale
        s = s + jnp.concatenate(
            [jnp.concatenate([b_ref[e, ro0 + 2 * i] for i in range(NA_KH // 2)], axis=1) for e in range(HS)], axis=0)
        return s, ws, ro0
    ws = pl.multiple_of(_window(kind, n, bq, t, seg), DIL_HALF)
    mask = _dil_mask(n, bq, nk, ws, seg)
    s = _dot_nt(qs, k_ref[pl.ds(ws, nk), :]) * scale
    s = jnp.where(jnp.concatenate([mask] * HS, axis=0), s, NEG_INF)
    return s, ws, None


def _attn_fwd(q, k, v, *, kind, name, bias=None, seg=None):
    t, w = q.shape
    quads = w // QW
    bq, nk, sub = _attn_geometry(kind)
    nq = t // (bq * sub)

    def body(*refs):
        if kind == "na":
            q_ref, k_ref, v_ref, b_ref, o_ref, l_ref = refs
        else:
            (q_ref, k_ref, v_ref, o_ref, l_ref), b_ref = refs, None
        for i in range(sub):
            n = pl.program_id(1) * sub + i
            rows = slice(i * bq, (i + 1) * bq)
            s, ws, _ = _attn_scores(kind, n, bq, nk, t, seg, _stack_heads(q_ref[rows, :]), k_ref, b_ref)
            m = jnp.max(s, axis=1, keepdims=True)
            p = jnp.exp(s - m)
            l = jnp.sum(p, axis=1, keepdims=True)
            o_ref[rows, :] = _unstack_heads(_dot(p / l, v_ref[pl.ds(ws, nk), :]), bq)
            l_ref[rows, :] = _unstack_heads(jnp.broadcast_to(m + jnp.log(l), (HS * bq, QW)), bq)

    blk = pl.BlockSpec((bq * sub, QW), lambda j, n: (n, j))
    res = pl.BlockSpec((t, QW), lambda j, n: (0, j))
    in_specs = [blk, res, res]
    operands = [q, k, v]
    est = 4 * t * QW * q.dtype.itemsize + 12 * sub * HS * bq * nk * 4
    if kind == "na":
        in_specs.append(pl.BlockSpec((HS,) + bias.shape[1:], lambda j, n: (j, 0, 0, 0)))
        operands.append(bias)
        est += 2 * _nbytes((HS,) + bias.shape[1:], F32)
    return pl.pallas_call(
        body,
        name=name,
        grid=(quads, nq),
        in_specs=in_specs,
        out_specs=[blk, blk],
        out_shape=[jax.ShapeDtypeStruct((t, w), F32)] * 2,
        compiler_params=pltpu.CompilerParams(dimension_semantics=("arbitrary", "arbitrary"), vmem_limit_bytes=_vmem(est)),
    )(*operands)


def _attn_bwd(q, k, v, do, dterm, lse, *, kind, name, bias=None, seg=None):
    t, w = q.shape
    quads = w // QW
    bq, nk, sub = _attn_geometry(kind)
    nq = t // (bq * sub)
    scale = HEAD_DIM ** -0.5

    def body(*refs):
        if kind == "na":
            q_ref, k_ref, v_ref, do_ref, dt_ref, l_ref, b_ref, dq_ref, dk_hbm, dv_hbm, db_ref, dk_acc, dv_acc, sem = refs
        else:
            q_ref, k_ref, v_ref, do_ref, dt_ref, l_ref, dq_ref, dk_hbm, dv_hbm, dk_acc, dv_acc, sem = refs
            b_ref = None
        j, step = pl.program_id(0), pl.program_id(1)

        @pl.when(step == 0)
        def _():
            dk_acc[...] = jnp.zeros_like(dk_acc)
            dv_acc[...] = jnp.zeros_like(dv_acc)
            if kind == "na":
                db_ref[...] = jnp.zeros_like(db_ref)

        for b in range(sub):
            n = step * sub + b
            rows = slice(b * bq, (b + 1) * bq)
            qs = _stack_heads(q_ref[rows, :])
            dos = _stack_heads(do_ref[rows, :])
            s, ws, ro0 = _attn_scores(kind, n, bq, nk, t, seg, qs, k_ref, b_ref)
            p = jnp.exp(s - _stack_cols(l_ref[rows, :], bq))
            dp = _dot_nt(dos, v_ref[pl.ds(ws, nk), :])
            ds = p * (dp - _stack_cols(dt_ref[rows, :], bq))
            if kind == "na":
                for e in range(HS):
                    for i in range(NA_KH // 2):
                        db_ref[e, ro0 + 2 * i] += ds[e * bq:(e + 1) * bq, i * LANES:(i + 1) * LANES]
            dsc = ds * scale
            dq_ref[rows, :] = _unstack_heads(_dot(dsc, k_ref[pl.ds(ws, nk), :]), bq)
            dk_acc[pl.ds(ws, nk), :] += _dot_tn(dsc, qs)
            dv_acc[pl.ds(ws, nk), :] += _dot_tn(p, dos)

        @pl.when(step == nq - 1)
        def _():
            ck = pltpu.make_async_copy(dk_acc, dk_hbm.at[j], sem.at[0])
            cv = pltpu.make_async_copy(dv_acc, dv_hbm.at[j], sem.at[1])
            ck.start()
            cv.start()
            ck.wait()
            cv.wait()

    blk = pl.BlockSpec((bq * sub, QW), lambda j, n: (n, j))
    res = pl.BlockSpec((t, QW), lambda j, n: (0, j))
    in_specs = [blk, res, res, blk, blk, blk]
    operands = [q, k, v, do, dterm, lse]
    out_specs = [blk, ANY, ANY]
    out_shape = [jax.ShapeDtypeStruct((t, w), F32)] + [jax.ShapeDtypeStruct((quads, t, QW), F32)] * 2
    est = 4 * t * QW * q.dtype.itemsize + 2 * t * QW * 4 + 16 * sub * HS * bq * nk * 4
    if kind == "na":
        bspec = pl.BlockSpec((HS,) + bias.shape[1:], lambda j, n: (j, 0, 0, 0))
        in_specs.append(bspec)
        operands.append(bias)
        out_specs.append(bspec)
        out_shape.append(jax.ShapeDtypeStruct(bias.shape, F32))
        est += 4 * _nbytes((HS,) + bias.shape[1:], F32)
    res_ = pl.pallas_call(
        body,
        name=name,
        grid=(quads, nq),
        in_specs=in_specs,
        out_specs=out_specs,
        out_shape=out_shape,
        scratch_shapes=[pltpu.VMEM((t, QW), F32), pltpu.VMEM((t, QW), F32), pltpu.SemaphoreType.DMA((2,))],
        compiler_params=pltpu.CompilerParams(dimension_semantics=("arbitrary", "arbitrary"), vmem_limit_bytes=_vmem(est)),
    )(*operands)
    unquad = lambda a: [(a, i) for i in range(quads)]
    return (res_[0], unquad(res_[1]), unquad(res_[2])) + tuple(res_[3:])


def _na_onehot():
    qc = np.arange(GRID_W)[:, None]
    kc = np.arange(GRID_W)[None, :]
    start = np.clip(qc - NA_KW // 2, 0, GRID_W - NA_KW)
    inwin = (kc >= start) & (kc < start + NA_KW)
    off = kc - qc + (NA_KW - 1)
    e_mat = np.zeros((2, 32, GRID_W, 2, GRID_W), np.float32)
    for e in range(2):
        for c in range(2 * NA_KW - 1):
            e_mat[e, c, :, e, :] = (off == c) & inwin
    neg = np.where(inwin, 0.0, NEG_INF).astype(np.float32)
    neg = np.broadcast_to(neg[:, None, :], (GRID_W, 2, GRID_W)).reshape(1, GRID_W * LANES)
    return jnp.asarray(e_mat.reshape(64, GRID_W * LANES), MXU_DTYPE), jnp.asarray(neg)


def _na_rowpairs(rpb):
    p = jnp.pad(rpb, ((0, 0), (0, 0), (0, 1)))
    return jnp.concatenate([p[:, :-1], p[:, 1:]], axis=-1).reshape(NA_HEADS * (2 * NA_KH - 2), 64)


def _na_bias_table(rpb):
    r2 = _na_rowpairs(rpb)
    e_mat, neg = _na_onehot()

    def body(r_ref, e_ref, n_ref, o_ref):
        hi, mid, lo = _split3(r_ref[...])
        e = e_ref[...]
        o_ref[...] = _dot(hi, e) + _dot(mid, e) + _dot(lo, e) + n_ref[...]

    out = pl.pallas_call(
        body,
        name="na_bias_table",
        out_shape=jax.ShapeDtypeStruct((r2.shape[0], GRID_W * LANES), F32),
        compiler_params=pltpu.CompilerParams(vmem_limit_bytes=_vmem(6 * r2.shape[0] * GRID_W * LANES * 4)),
    )(r2, e_mat, neg)
    return out.reshape(NA_HEADS, 2 * NA_KH - 2, GRID_W, LANES)


def _na_bias_grad(dbt):
    e_mat, _ = _na_onehot()
    flat = dbt.reshape(NA_HEADS * (2 * NA_KH - 2), GRID_W * LANES)

    def body(d_ref, e_ref, o_ref):
        hi, mid, lo = _split3(d_ref[...])
        e = e_ref[...]
        o_ref[...] = _dot_nt(hi, e) + _dot_nt(mid, e) + _dot_nt(lo, e)

    g = pl.pallas_call(
        body,
        name="na_bias_grad",
        out_shape=jax.ShapeDtypeStruct((flat.shape[0], 64), F32),
        compiler_params=pltpu.CompilerParams(vmem_limit_bytes=_vmem(6 * flat.shape[0] * GRID_W * LANES * 4)),
    )(flat, e_mat)
    g = g.reshape(NA_HEADS, 2 * NA_KH - 2, 2, 32)[..., :2 * NA_KW - 1]
    first = jnp.pad(g[:, :, 0], ((0, 0), (0, 1), (0, 0)))
    second = jnp.pad(g[:, :, 1], ((0, 0), (1, 0), (0, 0)))
    return first + second


def _all_gather(arrs, *, name):
    na = len(arrs)

    def body(*refs):
        ins, outs = refs[:na], refs[na:2 * na]
        send_sems, recv_sems, local_sems = refs[2 * na:]
        x, y, c = lax.axis_index("x"), lax.axis_index("y"), lax.axis_index("c")
        me, sibling = (x, y, c), (x, y, 1 - c)
        chips = [(1 - x, y), (x, 1 - y), (1 - x, 1 - y)]

        def rows(a, px, py, pc):
            r = ins[a].shape[0]
            return outs[a].at[pl.ds((4 * px + 2 * py + pc) * r, r), :]

        def copy(a, k, block, to, src=None):
            return pltpu.make_async_remote_copy(
                src_ref=rows(a, *block) if src is None else src, dst_ref=rows(a, *block),
                send_sem=send_sems.at[a, k], recv_sem=recv_sems.at[a, k], device_id=to, device_id_type=MESH)

        mine = [pltpu.make_async_copy(ins[a], rows(a, *me), local_sems.at[a]) for a in range(na)]
        for cp in mine:
            cp.start()
        first = []
        for a in range(na):
            first.append(copy(a, 0, me, sibling, src=ins[a]))
            first += [copy(a, 1 + j, me, (*chip, c), src=ins[a]) for j, chip in enumerate(chips)]
        for cp in first:
            cp.start()
        passed = []
        for j, chip in enumerate(chips):
            for a in range(na):
                copy(a, 1 + j, (*chip, c), me).wait_recv()
                cp = copy(a, 4 + j, (*chip, c), sibling)
                cp.start()
                passed.append(cp)
        for a in range(na):
            copy(a, 0, sibling, me).wait_recv()
        for j, chip in enumerate(chips):
            for a in range(na):
                copy(a, 4 + j, (*chip, 1 - c), me).wait_recv()
        for cp in first + passed:
            cp.wait_send()
        for cp in mine:
            cp.wait()

    return pl.pallas_call(
        body,
        name=name,
        in_specs=[ANY] * na,
        out_specs=[ANY] * na,
        out_shape=[jax.ShapeDtypeStruct((N_DEV * a.shape[0], a.shape[1]), a.dtype) for a in arrs],
        scratch_shapes=[pltpu.SemaphoreType.DMA((na, 7)), pltpu.SemaphoreType.DMA((na, 7)), pltpu.SemaphoreType.DMA((na,))],
    )(*arrs)


HBM = pl.BlockSpec(memory_space=pltpu.HBM)
SEM = pl.BlockSpec(memory_space=pltpu.SEMAPHORE)
EFFECT = pltpu.SideEffectType.DATAFLOW_SIDE_EFFECTING


def _peer_of(k):
    x, y, c = lax.axis_index("x"), lax.axis_index("y"), lax.axis_index("c")
    return x ^ ((k >> 2) & 1), y ^ ((k >> 1) & 1), c ^ (k & 1)


def _split_copies(gather, src_ref, land_ref, send_sems, recv_sems):
    x, y, c = lax.axis_index("x"), lax.axis_index("y"), lax.axis_index("c")
    my = 4 * x + 2 * y + c
    r = src_ref.shape[0] if gather else src_ref.shape[0] // N_DEV
    copies = []
    for k in ((1, 2, 4, 6) if gather == "chip" else range(1, N_DEV)):
        px, py, pc = _peer_of(k)
        if gather:
            src, dst = src_ref, land_ref.at[pl.ds(my * r, r), :]
        else:
            src, dst = src_ref.at[pl.ds((4 * px + 2 * py + pc) * r, r), :], land_ref.at[k - 1]
        copies.append(pltpu.make_async_remote_copy(
            src_ref=src, dst_ref=dst, send_sem=send_sems.at[k - 1], recv_sem=recv_sems.at[k - 1],
            device_id=(px, py, pc), device_id_type=MESH))
    return copies


def _split_start(srcs, lands, *, gather, name, after=None):
    na = len(srcs)
    extra = [] if after is None else [after]

    def body(*refs):
        src_refs, land_refs = refs[:na], refs[na:2 * na]
        outs = refs[2 * na + len(extra):]
        for a in range(na):
            for cp in _split_copies(gather, src_refs[a], land_refs[a], outs[4 * a], outs[4 * a + 1]):
                cp.start()
        outs[4 * na][...] = jnp.zeros_like(outs[4 * na])

    out_shape, out_specs, aliases = [], [], {}
    for a in range(na):
        out_shape += [pltpu.SemaphoreType.DMA((N_DEV - 1,)), pltpu.SemaphoreType.DMA((N_DEV - 1,)),
                      pltpu.HBM(srcs[a].shape, srcs[a].dtype), pltpu.HBM(lands[a].shape, lands[a].dtype)]
        out_specs += [SEM, SEM, HBM, HBM]
        aliases[a] = 4 * a + 2
        aliases[na + a] = 4 * a + 3
    out_shape.append(jax.ShapeDtypeStruct((8, LANES), F32))
    out_specs.append(pl.BlockSpec(memory_space=pltpu.VMEM))
    res = pl.pallas_call(
        body,
        name=name,
        out_shape=tuple(out_shape),
        in_specs=[HBM] * (2 * na) + [ANY] * len(extra),
        out_specs=tuple(out_specs),
        input_output_aliases=aliases,
        compiler_params=pltpu.CompilerParams(has_side_effects=EFFECT),
    )(*[pltpu.with_memory_space_constraint(a, pltpu.HBM) for a in list(srcs) + list(lands)], *extra)
    return [tuple(res[4 * a:4 * a + 4]) for a in range(na)], res[4 * na][0, 0]


def _split_wait(handles, after, *, gather, name):
    na = len(handles)

    def body(*refs):
        src_refs, land_refs = refs[:na], refs[na:2 * na]
        sems = refs[2 * na:4 * na]
        for a in range(na):
            for cp in _split_copies(gather, src_refs[a], land_refs[a], sems[2 * a], sems[2 * a + 1]):
                cp.wait_send()
                cp.wait_recv()

    srcs = [h[2] for h in handles]
    lands = [h[3] for h in handles]
    sems = [s for h in handles for s in h[:2]]
    res = pl.pallas_call(
        body,
        name=name,
        out_shape=tuple(pltpu.HBM(a.shape, a.dtype) for a in srcs + lands),
        in_specs=[HBM] * (2 * na) + [SEM] * (2 * na) + [ANY],
        out_specs=tuple([HBM] * (2 * na)),
        input_output_aliases={i: i for i in range(2 * na)},
        compiler_params=pltpu.CompilerParams(has_side_effects=EFFECT),
    )(*srcs, *lands, *sems, after)
    return list(res[:na]), list(res[na:])


def _forward_copies(land_ref, send_sems, recv_sems):
    x, y, c = lax.axis_index("x"), lax.axis_index("y"), lax.axis_index("c")
    r = land_ref.shape[0] // N_DEV
    copies = []
    for j, k in enumerate((2, 4, 6)):
        px, py, pc = _peer_of(k)
        rows = land_ref.at[pl.ds((4 * px + 2 * py + pc) * r, r), :]
        copies.append(pltpu.make_async_remote_copy(
            src_ref=rows, dst_ref=rows, send_sem=send_sems.at[j], recv_sem=recv_sems.at[j],
            device_id=(x, y, 1 - c), device_id_type=MESH))
    return copies


def _forward_start(land, *, name):
    def body(land_ref, send_sems, recv_sems, land_thru, token):
        for cp in _forward_copies(land_ref, send_sems, recv_sems):
            cp.start()
        token[...] = jnp.zeros_like(token)

    res = pl.pallas_call(
        body,
        name=name,
        out_shape=(pltpu.SemaphoreType.DMA((3,)), pltpu.SemaphoreType.DMA((3,)), pltpu.HBM(land.shape, land.dtype),
                   jax.ShapeDtypeStruct((8, LANES), F32)),
        in_specs=[HBM],
        out_specs=(SEM, SEM, HBM, pl.BlockSpec(memory_space=pltpu.VMEM)),
        input_output_aliases={0: 2},
        compiler_params=pltpu.CompilerParams(has_side_effects=EFFECT),
    )(pltpu.with_memory_space_constraint(land, pltpu.HBM))
    return res[:3]


def _forward_wait(handle, *, name):
    send_sems, recv_sems, land = handle

    def body(land_ref, send_ref, recv_ref, land_out):
        for cp in _forward_copies(land_ref, send_ref, recv_ref):
            cp.wait_send()
            cp.wait_recv()

    return pl.pallas_call(
        body,
        name=name,
        out_shape=pltpu.HBM(land.shape, land.dtype),
        in_specs=[HBM, SEM, SEM],
        out_specs=HBM,
        input_output_aliases={0: 0},
        compiler_params=pltpu.CompilerParams(has_side_effects=EFFECT),
    )(land, send_sems, recv_sems)


def _sum8(own, recv, *, name):
    _, r, w = recv.shape
    tr = _pick(r, (256, 128, 64, 32, 16, 8))

    def body(own_ref, a_ref, o_ref):
        acc = own_ref[...].astype(F32)
        for i in range(N_DEV - 1):
            acc = acc + a_ref[i].astype(F32)
        o_ref[...] = acc

    return pl.pallas_call(
        body,
        name=name,
        grid=(r // tr,),
        in_specs=[pl.BlockSpec((tr, w), lambda i: (i, 0)), pl.BlockSpec((N_DEV - 1, tr, w), lambda i: (0, i, 0))],
        out_specs=pl.BlockSpec((tr, w), lambda i: (i, 0)),
        out_shape=jax.ShapeDtypeStruct((r, w), F32),
        compiler_params=pltpu.CompilerParams(dimension_semantics=("parallel",), vmem_limit_bytes=_vmem(4 * N_DEV * tr * w * 4)),
    )(own, recv)


def _adamw(w, g, m, v, *, name):
    def fn(rows, _):
        wv, gv, mv, vv = rows
        m1 = ADAM_B1 * mv + (1.0 - ADAM_B1) * gv
        v1 = ADAM_B2 * vv + (1.0 - ADAM_B2) * jnp.square(gv)
        m_hat = m1 / (1.0 - ADAM_B1 ** ADAM_STEP)
        v_hat = v1 / (1.0 - ADAM_B2 ** ADAM_STEP)
        delta = -ADAM_LR * (m_hat / (jnp.sqrt(v_hat) + ADAM_EPS) + ADAM_WD * wv)
        return [delta, m1, v1], []

    c = w.shape[1]
    return _rowmap(fn, [w, g, m, v], [], [(c, F32)] * 3, [], name=name)


_SMALL = ("b_ada", "g_norm1", "g_norm2", "b_gate", "g_qa", "g_ka", "g_qb", "g_kb", "rpb", "loss")


def _pack_small(parts):
    flat = []
    for nme in _SMALL:
        a = parts[nme].reshape(-1).astype(F32)
        flat.append(jnp.pad(a, (0, (-a.shape[0]) % LANES)))
    flat = jnp.concatenate(flat)
    flat = jnp.pad(flat, (0, (-flat.shape[0]) % (LANES * LANES)))
    return flat.reshape(-1, LANES)


def _unpack_small(packed, shapes):
    flat = packed.reshape(-1)
    out, pos = {}, 0
    for nme in _SMALL:
        n = int(np.prod(shapes[nme]))
        out[nme] = flat[pos:pos + n].reshape(shapes[nme])
        pos += n + (-n) % LANES
    return out


def _to_class(a, d):
    t, w = a.shape
    return a if d == 1 else a.reshape(t // d, d, w).transpose(1, 0, 2).reshape(t, w)


def _from_class(a, d):
    t, w = a.shape
    return a if d == 1 else a.reshape(d, t // d, w).transpose(1, 0, 2).reshape(t, w)


def kernel(x, c, w_ada, b_ada, g_norm1, g_norm2, w_in, b_gate, g_qa, g_ka, g_qb, g_kb, rpb, w_proj_a, w_proj_b, w_o, w_ffn_in, w_ffn_out, loss_target, m_w_ada, m_b_ada, m_g_norm1, m_g_norm2, m_w_in, m_b_gate, m_g_qa, m_g_ka, m_g_qb, m_g_kb, m_rpb, m_w_proj_a, m_w_proj_b, m_w_o, m_w_ffn_in, m_w_ffn_out, v_w_ada, v_b_ada, v_g_norm1, v_g_norm2, v_w_in, v_b_gate, v_g_qa, v_g_ka, v_g_qb, v_g_kb, v_rpb, v_w_proj_a, v_w_proj_b, v_w_o, v_w_ffn_in, v_w_ffn_out):
    t, d = x.shape[1], x.shape[2]
    d_ff = w_ffn_out.shape[1] * N_DEV
    me = 4 * lax.axis_index("x") + 2 * lax.axis_index("y") + lax.axis_index("c")
    xt, tgt = x.reshape(t, d), loss_target.reshape(t, d)
    ones = _head_ones()

    shards = [s.astype(WIRE_DTYPE) for s in (w_in[0].T, w_ffn_in[0].T, w_proj_a[0].T, w_proj_b[0].T, w_o[0], w_ffn_out[0])]
    lands = [lax.dynamic_update_slice(lax.empty((N_DEV * s.shape[0], s.shape[1]), s.dtype), s, (me * s.shape[0], 0))
             for s in shards]

    c_all = _all_gather([jnp.pad(c, ((0, 7), (0, 0)))], name="gather_c")[0][::8]
    c_all = jnp.pad(c_all, ((0, LANES - N_DEV), (0, 0)))

    def mod_body(c_ref, w_ref, b_ref, o_ref, act_ref):
        act = _silu(c_ref[...])
        act_ref[...] = act
        hi, mid, lo = _split3(act)
        w = w_ref[...]
        whi, wmid, wlo = _split3(w)
        acc = _dot(hi, whi) + (_dot(hi, wmid) + _dot(mid, whi)) + (_dot(hi, wlo) + _dot(mid, wmid) + _dot(lo, whi))
        o_ref[...] = acc + b_ref[...]

    ncol = w_ada.shape[2]
    b_ada_mine = lax.dynamic_slice(b_ada, (0, me * ncol), (1, ncol))
    mod_part, c_act = pl.pallas_call(
        mod_body,
        name="ada_mod",
        out_shape=[jax.ShapeDtypeStruct((LANES, ncol), F32), jax.ShapeDtypeStruct((LANES, d), F32)],
        compiler_params=pltpu.CompilerParams(vmem_limit_bytes=_vmem(6 * d * ncol * 4)),
    )(c_all, w_ada[0], b_ada_mine)
    mod_all = _all_gather([mod_part[:N_DEV]], name="gather_mod")[0].reshape(N_DEV, N_DEV, ncol)
    mod = lax.dynamic_index_in_dim(mod_all, me, axis=1, keepdims=False).reshape(6, d)
    sh1, sc1, gt1, sh2, sc2, gt2 = [mod[i:i + 1] for i in range(6)]

    def norm_fwd(rows, vecs):
        (xv,), (g, sc, sh) = rows, vecs
        return [xv * _rms(xv) * g * (1.0 + sc) + sh], []

    w_in_handle, w_token = _split_start(shards[:1], lands[:1], gather="chip", after=mod, name="gather_w_in_start")
    (h,) = _rowmap(norm_fwd, [xt], [g_norm1 + w_token, sc1, sh1], [(d, MXU_DTYPE)], [], name="norm1")
    n_a, n_b = 3 * WA, 3 * WB
    (w_in_t,) = _split_wait(w_in_handle, h, gather="chip", name="gather_w_in_wait")[1]
    w_in_t = _forward_wait(_forward_start(w_in_t, name="gather_w_in_forward_start"), name="gather_w_in_forward_wait")
    w_handles, w_token = _split_start(shards[1:], lands[1:], gather=True, after=w_in_t, name="gather_weights_start")
    w_in_a, w_in_b, w_in_g = w_in_t[:n_a], w_in_t[n_a:n_a + n_b], w_in_t[n_a + n_b:]
    gates = _mm(h, w_in_g, tb=True, out_dtype=ACT_DTYPE, name="proj_gates")

    rot_c, rot_lo, rot_hi = _rot_tables(t)
    tile_g = lambda g, heads: jnp.tile(g, (1, heads))

    def qk_fwd(width, rotate):
        def fn(xv, rows, vecs):
            gq, gk, on = vecs
            outs = [xv]
            for i, g in enumerate((gq, gk)):
                xi = xv[:, i * width:(i + 1) * width]
                r = lax.rsqrt(_headsum(xi * xi, on) * (1.0 / HEAD_DIM) + EPS)
                yi = xi * r * g
                if rotate:
                    yi = _rot(yi, rows[0], rows[1], rows[2])
                outs.append(yi)
            outs.append(xv[:, 2 * width:])
            return outs, []
        return fn

    qkv_a, qa, ka, va = _mm_parts_rows(
        [(h, w_in_a, "nt")], qk_fwd(WA, False), [], [tile_g(g_qa, NA_HEADS) + w_token, tile_g(g_ka, NA_HEADS), ones],
        [(3 * WA, ACT_DTYPE)] + [(WA, MXU_DTYPE)] * 3, [], name="proj_a_qknorm")
    qkv_b, qb, kb, vb = _mm_parts_rows(
        [(h, w_in_b, "nt")], qk_fwd(WB, True), [rot_c, rot_lo, rot_hi],
        [tile_g(g_qb, DIL_HEADS), tile_g(g_kb, DIL_HEADS), ones],
        [(3 * WB, ACT_DTYPE)] + [(WB, MXU_DTYPE)] * 3, [], name="proj_b_qknorm")

    bias_tab = _na_bias_table(rpb[0])
    o_a, lse_a = _attn_fwd(qa, ka, va, kind="na", bias=bias_tab, name="na_fwd")

    grp = []
    for g, (_, dil) in enumerate(DIL_CONFIGS):
        sl = slice(g * WB_OUT, (g + 1) * WB_OUT)
        qg, kg, vg = [_to_class(a[:, sl], dil) for a in (qb, kb, vb)]
        og, lg = _attn_fwd(qg, kg, vg, kind="dil", seg=t // dil, name=f"dil_fwd{g}")
        grp.append(dict(q=qg, k=kg, v=vg, o=_from_class(og, dil), lse=_from_class(lg, dil), lse_c=lg, dil=dil))

    def merge_fwd(rows, _):
        o0, o1, o2, l0, l1, l2 = rows
        mx = jnp.maximum(jnp.maximum(l0, l1), l2)
        e0, e1, e2 = jnp.exp(l0 - mx), jnp.exp(l1 - mx), jnp.exp(l2 - mx)
        s = e0 + e1 + e2
        return [(e0 / s) * o0 + (e1 / s) * o1 + (e2 / s) * o2], []

    (o_b,) = _rowmap(merge_fwd, [gr["o"] for gr in grp] + [gr["lse"] for gr in grp], [], [(WB_OUT, F32)], [], name="dil_merge")

    w_pa_t, w_pb_t, w_o_f = _split_wait(w_handles[1:4], o_b, gather=True, name="gather_w_out_wait")[1]
    pa = _mm(o_a, w_pa_t, tb=True, out_dtype=ACT_DTYPE, name="proj_out_a")
    pb = _mm(o_b, w_pb_t, tb=True, out_dtype=ACT_DTYPE, name="proj_out_b")

    def gate_fwd(rows, vecs):
        gv, pav, pbv = rows
        sg = jax.nn.sigmoid(gv + vecs[0])
        return [sg[:, :d] * pav + sg[:, d:] * pbv], []

    (merged,) = _rowmap(gate_fwd, [gates, pa, pb], [b_gate], [(d, MXU_DTYPE)], [], name="gate_merge")
    def resid_norm(av, rows, vecs):
        (xv,), (gt, g, sc, sh) = rows, vecs
        x1v = xv + gt * av
        return [av, x1v, x1v * _rms(x1v) * g * (1.0 + sc) + sh], []

    att, x1, h2 = _mm_parts_rows([(merged, w_o_f)], resid_norm, [xt], [gt1, g_norm2, sc2, sh2],
                           [(d, F32), (d, F32), (d, MXU_DTYPE)], [], name="proj_o_resid_norm2")

    w_ffn_in_t, w_ffn_out_f = _split_wait([w_handles[0], w_handles[4]], h2, gather=True, name="gather_w_ffn_wait")[1]
    w_ffn_a, w_ffn_up = w_ffn_in_t[:d_ff], w_ffn_in_t[d_ff:]

    def swiglu_fwd(prods, _):
        a, up = prods
        return [a, up, _silu(a) * up]

    ua, uu, f = _mm_ew(h2, [w_ffn_a, w_ffn_up], swiglu_fwd, [], [ACT_DTYPE, ACT_DTYPE, MXU_DTYPE], name="ffn_in_swiglu")

    def loss_fn(yv, rows, vecs):
        (x1v, tv), gt = rows, vecs[0]
        err = x1v + gt * yv - tv
        dout = err * (1.0 / d)
        return [dout, dout * gt], [_colsum(err * err), _colsum(dout * yv)]

    dout, dy2, err2, dgt2 = _mm_parts_rows([(f, w_ffn_out_f)], loss_fn, [x1, tgt], [gt2], [(d, F32), (d, MXU_DTYPE)],
                                           [d, d], name="ffn_out_loss")

    dw_ffn_out = _mm(f, dy2, ta=True, out_dtype=WIRE_DTYPE, name="wgrad_ffn_out")
    def swiglu_bwd(prods, rows):
        (dfv,), (a, up) = prods, rows
        sg = jax.nn.sigmoid(a)
        return [dfv * up * (sg * (1.0 + a * (1.0 - sg))), dfv * (a * sg)]

    da, dup = _mm_ew(dy2, [w_ffn_out_f], swiglu_bwd, [ua, uu], [MXU_DTYPE, MXU_DTYPE], name="dgrad_ffn_out_swiglu_bwd")
    dw_ffn_in_t = _mm(da, h2, ta=True, into=(lax.empty((2 * d_ff, d), WIRE_DTYPE), 0), name="wgrad_ffn_in_a")
    dw_ffn_in_t = _mm(dup, h2, ta=True, into=(dw_ffn_in_t, d_ff), name="wgrad_ffn_in_up")
    land7 = lambda a: lax.empty((N_DEV - 1, a.shape[0] // N_DEV, a.shape[1]), a.dtype)
    own_block = lambda a: lax.dynamic_slice(a, (me * (a.shape[0] // N_DEV), 0), (a.shape[0] // N_DEV, a.shape[1]))
    g_ffn = [dw_ffn_in_t, dw_ffn_out]
    h_ffn, tok_ffn = _split_start(g_ffn, [land7(a) for a in g_ffn], gather=False, name="exchange_ffn_start")
    def norm_bwd(dh, xv, g, sc):
        r = _rms(xv)
        xh = xv * r
        dxh = dh * g * (1.0 + sc)
        dxv = r * (dxh - xh * jnp.mean(dxh * xh, axis=-1, keepdims=True))
        return dxv, [_colsum(dh), _colsum(dh * xh * g), _colsum(dh * xh * (1.0 + sc))]

    def norm2_bwd(dhv, rows, vecs):
        (x1v, dov, av), (g, sc, gt) = rows, vecs
        dxv, sums = norm_bwd(dhv, x1v, g, sc)
        dx1v = dov + dxv
        return [dx1v, dx1v * gt], sums + [_colsum(dx1v * av)]

    dx1, datt, dsh2, dsc2, dg2, dgt1 = _mm_parts_rows(
        [(da, w_ffn_a), (dup, w_ffn_up)], norm2_bwd, [x1, dout, att], [g_norm2 + tok_ffn, sc2, gt1],
        [(d, F32), (d, MXU_DTYPE)], [d] * 4, name="dgrad_ffn_in_norm2_bwd")
    dw_o = _mm(merged, datt, ta=True, out_dtype=WIRE_DTYPE, name="wgrad_o")
    def gate_bwd(dm, rows, vecs):
        gv, pav, pbv = rows
        sg = jax.nn.sigmoid(gv + vecs[0])
        ga, gb = sg[:, :d], sg[:, d:]
        dgp = jnp.concatenate([dm * pav * ga * (1.0 - ga), dm * pbv * gb * (1.0 - gb)], axis=1)
        return [dm * ga, dm * gb, dgp], [_colsum(dgp)]

    dpa, dpb, dgates, db_gate = _mm_parts_rows(
        [(datt, w_o_f.T)], gate_bwd, [gates, pa, pb], [b_gate],
        [(d, MXU_DTYPE), (d, MXU_DTYPE), (2 * d, MXU_DTYPE)], [2 * d], name="dgrad_o_gate_bwd")
    dw_pa_t = _mm(dpa, o_a, ta=True, out_dtype=WIRE_DTYPE, name="wgrad_proj_a")
    dw_pb_t = _mm(dpb, o_b, ta=True, out_dtype=WIRE_DTYPE, name="wgrad_proj_b")
    g_out = [dw_pa_t, dw_pb_t, dw_o]
    h_out, tok_out = _split_start(g_out, [land7(a) for a in g_out], gather=False, name="exchange_out_start")
    def delta_a(doa, rows, vecs):
        return [doa, _headsum(doa * rows[0], vecs[0])], []

    do_a, dterm_a = _mm_parts_rows([(dpa, w_pa_t)], delta_a, [o_a], [ones + tok_out.astype(ones.dtype)],
                                   [(WA, F32), (WA, F32)], [], name="dgrad_proj_a_delta")
    dqa, dka, dva, dbias = _attn_bwd(qa, ka, va, do_a, dterm_a, lse_a, kind="na", bias=bias_tab, name="na_bwd")
    g_rpb = _na_bias_grad(dbias)

    def merge_bwd(dob, rows, vecs):
        o0, o1, o2, l0, l1, l2 = rows
        on = vecs[0]
        mx = jnp.maximum(jnp.maximum(l0, l1), l2)
        e0, e1, e2 = jnp.exp(l0 - mx), jnp.exp(l1 - mx), jnp.exp(l2 - mx)
        s = e0 + e1 + e2
        ws = [e0 / s, e1 / s, e2 / s]
        dws = [_headsum(dob * o, on) for o in (o0, o1, o2)]
        mean = ws[0] * dws[0] + ws[1] * dws[1] + ws[2] * dws[2]
        return [w * dob for w in ws] + [w * mean for w in ws], []

    mb = _mm_parts_rows([(dpb, w_pb_t)], merge_bwd, [gr["o"] for gr in grp] + [gr["lse"] for gr in grp], [ones],
                        [(WB_OUT, F32)] * 6, [], name="dgrad_proj_b_merge_bwd")
    dqb, dkb, dvb = [], [], []
    for g, gr in enumerate(grp):
        dil = gr["dil"]
        dq, dk, dv = _attn_bwd(gr["q"], gr["k"], gr["v"], _to_class(mb[g], dil), _to_class(mb[3 + g], dil), gr["lse_c"],
                               kind="dil", seg=t // dil, name=f"dil_bwd{g}")
        dqb.append(_from_class(dq, dil))
        dkb.append(_from_class(dk[0][0][0], dil))
        dvb.append(_from_class(dv[0][0][0], dil))

    def qk_bwd(width, rotate, nparts):
        def fn(rows, vecs):
            gq, gk, on = vecs
            xv = rows[0]
            pos = 1
            if rotate:
                rc, rlo, rhi = rows[1:4]
                pos = 4
            cat = lambda parts: parts[0] if len(parts) == 1 else jnp.concatenate(parts, axis=1)
            ends = np.cumsum((pos,) + nparts)
            dq, dk, dv = [cat(rows[ends[i]:ends[i + 1]]) for i in range(3)]
            outs, sums = [], []
            for i, (dy, g) in enumerate(((dq, gq), (dk, gk))):
                if rotate:
                    dy = _rot(dy, rc, -rlo, -rhi)
                xi = xv[:, i * width:(i + 1) * width]
                r = lax.rsqrt(_headsum(xi * xi, on) * (1.0 / HEAD_DIM) + EPS)
                xh = xi * r
                dxh = dy * g
                outs.append(r * (dxh - xh * (_headsum(dxh * xh, on) * (1.0 / HEAD_DIM))))
                sums.append(_colsum(dy * xh))
            return [jnp.concatenate(outs + [dv], axis=1)], sums
        return fn

    dqkv_a, dg_qa, dg_ka = _rowmap(qk_bwd(WA, False, (1, len(dka), len(dva))), [qkv_a, dqa] + dka + dva,
                                   [tile_g(g_qa, NA_HEADS), tile_g(g_ka, NA_HEADS), ones],
                                   [(3 * WA, MXU_DTYPE)], [WA, WA], name="qknorm_a_bwd")
    dqkv_b, dg_qb, dg_kb = _rowmap(qk_bwd(WB, True, (3, 3, 3)), [qkv_b, rot_c, rot_lo, rot_hi] + dqb + dkb + dvb,
                                   [tile_g(g_qb, DIL_HEADS), tile_g(g_kb, DIL_HEADS), ones],
                                   [(3 * WB, MXU_DTYPE)], [WB, WB], name="qknorm_b_bwd")

    dw_in_t = jnp.concatenate([
        _mm(dqkv_a, h, ta=True, out_dtype=WIRE_DTYPE, name="wgrad_in_a"),
        _mm(dqkv_b, h, ta=True, out_dtype=WIRE_DTYPE, name="wgrad_in_b"),
        _mm(dgates, h, ta=True, out_dtype=WIRE_DTYPE, name="wgrad_in_gates")], axis=0)
    h_in, tok_in = _split_start([dw_in_t], [land7(dw_in_t)], gather=False, name="exchange_in_start")
    def norm1_bwd(dhv, rows, vecs):
        xv, dx1v = rows
        dxv, sums = norm_bwd(dhv, xv, vecs[0], vecs[1])
        return [dx1v + dxv], sums

    grad_x, dsh1, dsc1, dg1 = _mm_parts_rows(
        [(dqkv_a, w_in_a), (dqkv_b, w_in_b), (dgates, w_in_g)], norm1_bwd, [xt, dx1], [g_norm1 + tok_in, sc1],
        [(d, F32)], [d] * 3, name="dgrad_in_norm1_bwd")

    heads_sum = lambda a, heads: a.reshape(heads, HEAD_DIM).sum(axis=0)
    dmod = jnp.concatenate([dsh1, dsc1, dgt1, dsh2, dsc2, dgt2], axis=1)
    local_small = _pack_small(dict(
        b_ada=dmod, g_norm1=dg1, g_norm2=dg2, b_gate=db_gate, g_qa=heads_sum(dg_qa, NA_HEADS),
        g_ka=heads_sum(dg_ka, NA_HEADS), g_qb=heads_sum(dg_qb, DIL_HEADS), g_kb=heads_sum(dg_kb, DIL_HEADS),
        rpb=g_rpb, loss=(0.5 / d) * jnp.sum(err2)))
    srows = local_small.shape[0]
    small_all = _all_gather([local_small], name="gather_small")[0].reshape(N_DEV, srows, LANES)
    small_sum = _sum8(small_all[0], small_all[1:], name="sum_small")
    small_shapes = dict(b_ada=b_ada.shape, g_norm1=g_norm1.shape, g_norm2=g_norm2.shape, b_gate=b_gate.shape,
                        g_qa=g_qa.shape, g_ka=g_ka.shape, g_qb=g_qb.shape, g_kb=g_kb.shape, rpb=rpb.shape, loss=())
    small_w = dict(b_ada=b_ada, g_norm1=g_norm1, g_norm2=g_norm2, b_gate=b_gate, g_qa=g_qa, g_ka=g_ka, g_qb=g_qb,
                   g_kb=g_kb, rpb=rpb, loss=jnp.zeros((), F32))
    small_m = dict(b_ada=m_b_ada, g_norm1=m_g_norm1, g_norm2=m_g_norm2, b_gate=m_b_gate, g_qa=m_g_qa, g_ka=m_g_ka,
                   g_qb=m_g_qb, g_kb=m_g_kb, rpb=m_rpb, loss=jnp.zeros((), F32))
    small_v = dict(b_ada=v_b_ada, g_norm1=v_g_norm1, g_norm2=v_g_norm2, b_gate=v_b_gate, g_qa=v_g_qa, g_ka=v_g_ka,
                   g_qb=v_g_qb, g_kb=v_g_kb, rpb=v_rpb, loss=jnp.zeros((), F32))
    s_delta, s_m, s_v = _adamw(_pack_small(small_w), small_sum, _pack_small(small_m), _pack_small(small_v), name="adamw_small")
    gs = _unpack_small(small_sum, small_shapes)
    ds_, ms_, vs_ = [_unpack_small(a, small_shapes) for a in (s_delta, s_m, s_v)]

    dmod_all = small_all[:, :6 * d // LANES].reshape(N_DEV, 6 * d)
    dmod_mine = jnp.pad(lax.dynamic_slice(dmod_all, (0, me * ncol), (N_DEV, ncol)), ((0, LANES - N_DEV), (0, 0)))

    def wada_body(c_ref, dm_ref, o_ref):
        chi, cmid, clo = _split3(c_ref[...])
        dhi, dmid, dlo = _split3(dm_ref[...])
        o_ref[...] = (_dot_tn(chi, dhi) + (_dot_tn(chi, dmid) + _dot_tn(cmid, dhi))
                      + (_dot_tn(chi, dlo) + _dot_tn(cmid, dmid) + _dot_tn(clo, dhi)))

    g_w_ada = pl.pallas_call(
        wada_body,
        name="wgrad_ada",
        out_shape=jax.ShapeDtypeStruct((d, ncol), F32),
        compiler_params=pltpu.CompilerParams(vmem_limit_bytes=_vmem(4 * d * ncol * 4)),
    )(c_act, dmod_mine)

    sent, recv = _split_wait(h_in + h_ffn + h_out, small_sum, gather=False, name="exchange_wait")
    names = ("w_in", "w_ffn_in", "w_ffn_out", "w_proj_a", "w_proj_b", "w_o")
    transposed = (True, True, False, True, True, False)
    big_g = {}
    for nme, own, r, tr in zip(names, sent, recv, transposed):
        s = _sum8(own_block(own), r, name=f"sum_{nme}")
        big_g[nme] = s.T if tr else s
    big_g["w_ada"] = g_w_ada
    big_w = dict(w_ada=w_ada, w_in=w_in, w_proj_a=w_proj_a, w_proj_b=w_proj_b, w_o=w_o, w_ffn_in=w_ffn_in, w_ffn_out=w_ffn_out)
    big_m = dict(w_ada=m_w_ada, w_in=m_w_in, w_proj_a=m_w_proj_a, w_proj_b=m_w_proj_b, w_o=m_w_o, w_ffn_in=m_w_ffn_in, w_ffn_out=m_w_ffn_out)
    big_v = dict(w_ada=v_w_ada, w_in=v_w_in, w_proj_a=v_w_proj_a, w_proj_b=v_w_proj_b, w_o=v_w_o, w_ffn_in=v_w_ffn_in, w_ffn_out=v_w_ffn_out)
    grads, deltas, new_m, new_v = {}, {}, {}, {}
    for nme in big_w:
        dl, m1, v1 = _adamw(big_w[nme][0], big_g[nme], big_m[nme][0], big_v[nme][0], name=f"adamw_{nme}")
        grads[nme], deltas[nme], new_m[nme], new_v[nme] = big_g[nme][None], dl[None], m1[None], v1[None]
    for nme in _SMALL[:-1]:
        grads[nme], deltas[nme], new_m[nme], new_v[nme] = gs[nme], ds_[nme], ms_[nme], vs_[nme]

    order = ("w_ada", "b_ada", "g_norm1", "g_norm2", "w_in", "b_gate", "g_qa", "g_ka", "g_qb", "g_kb", "rpb",
             "w_proj_a", "w_proj_b", "w_o", "w_ffn_in", "w_ffn_out")
    return (gs["loss"], grad_x[None], *[grads[n] for n in order], *[deltas[n] for n in order],
            *[new_m[n] for n in order], *[new_v[n] for n in order])
```

```python
import functools

import numpy as np
import jax
import jax.numpy as jnp
from jax import lax
from jax.experimental import pallas as pl
from jax.experimental.pallas import tpu as pltpu

F32 = jnp.float32
MXU_DTYPE = jnp.bfloat16
WIRE_DTYPE = jnp.bfloat16
ACT_DTYPE = jnp.bfloat16

HEAD_DIM = 64
GRID_W = 64
NA_HEADS = 8
NA_KH = 8
NA_KW = 16
DIL_CONFIGS = ((128, 1), (512, 4), (2048, 16))
DIL_HEADS_PER_GROUP = 4
DIL_HEADS = DIL_HEADS_PER_GROUP * len(DIL_CONFIGS)
DIL_HALF = 64
ROT_DIM = HEAD_DIM // 4
ROPE_THETA = 500000.0
EPS = 1e-6
NEG_INF = -1e30
WA = NA_HEADS * HEAD_DIM
WB = DIL_HEADS * HEAD_DIM
WB_OUT = DIL_HEADS_PER_GROUP * HEAD_DIM
ADAM_LR = 0.001
ADAM_B1 = 0.9
ADAM_B2 = 0.999
ADAM_EPS = 1e-08
ADAM_WD = 0.01
ADAM_STEP = 10

N_DEV = 8
LANES = 128
VMEM_CAP = 60 * 2**20
VMEM_FLOOR = 56 * 2**20
MESH = pl.DeviceIdType.MESH
ANY = pl.BlockSpec(memory_space=pl.ANY)


def _vmem(nbytes):
    return int(min(VMEM_CAP, max(VMEM_FLOOR, nbytes * 5 // 4 + 4 * 2**20)))


def _pick(dim, cands):
    for c in cands:
        if c <= dim and dim % c == 0:
            return c
    return dim


def _nbytes(shape, dtype):
    return int(np.prod(shape)) * jnp.dtype(dtype).itemsize


def _dot(a, b, dims=((1,), (0,))):
    return lax.dot_general(a.astype(MXU_DTYPE), b.astype(MXU_DTYPE), (dims, ((), ())), preferred_element_type=F32)


def _dot_nt(a, b):
    return _dot(a, b, ((1,), (1,)))


def _dot_tn(a, b):
    return _dot(a, b, ((0,), (0,)))


def _split3(a):
    hi = a.astype(jnp.bfloat16)
    r1 = a - hi.astype(F32)
    mid = r1.astype(jnp.bfloat16)
    lo = (r1 - mid.astype(F32)).astype(jnp.bfloat16)
    return hi, mid, lo


def _silu(x):
    return x * jax.nn.sigmoid(x)


def _divisors(dim, unit):
    return [c for c in range(unit, dim + 1, unit) if dim % c == 0] or [dim]


def _mm_tiles(m, n, kdim, a_item, b_item, o_item, row_off=0):
    step_us, hbm_bytes_per_us, flops_per_us, budget = 0.35, 3.0e6, 8.0e8, 40 * 2**20
    best = None
    for tm in _divisors(m, LANES):
        for tn in _divisors(n, LANES):
            for tk in _divisors(kdim, LANES):
                if row_off % tm:
                    continue
                gm, gn, gk = m // tm, n // tn, kdim // tk
                vmem = 2 * (tm * tk * a_item + tk * tn * b_item + tm * tn * o_item) + 2 * (tm * tk + tk * tn)
                vmem += tm * tn * 4 * ((1 if gk > 1 else 0) + 1)
                if vmem > budget:
                    continue
                a_reads = m * kdim * a_item * (gn if gk > 1 else 1)
                traffic = a_reads + kdim * n * b_item * gm + m * n * o_item
                cost = gm * gn * gk * step_us + max(traffic / hbm_bytes_per_us, 2.0 * m * n * kdim / flops_per_us)
                if best is None or cost < best[0]:
                    best = (cost, tm, tn, tk)
    return best[1:]


def _mm(a, b, *, name, ta=False, tb=False, out_dtype=F32, into=None):
    if ta:
        kdim, m = a.shape
    else:
        m, kdim = a.shape
    n = b.shape[0] if tb else b.shape[1]
    assert b.shape[1 if tb else 0] == kdim
    buf, row_off = into if into is not None else (None, 0)
    if buf is not None:
        out_dtype = buf.dtype
    tm, tn, tk = _mm_tiles(m, n, kdim, a.dtype.itemsize, b.dtype.itemsize, jnp.dtype(out_dtype).itemsize, row_off)
    gm, gn, gk = m // tm, n // tn, kdim // tk
    ob = row_off // tm

    a_spec = pl.BlockSpec((tk, tm), lambda i, j, k: (k, i)) if ta else pl.BlockSpec((tm, tk), lambda i, j, k: (i, k))
    b_spec = pl.BlockSpec((tn, tk), lambda i, j, k: (j, k)) if tb else pl.BlockSpec((tk, tn), lambda i, j, k: (k, j))
    o_spec = pl.BlockSpec((tm, tn), lambda i, j, k: (i + ob, j))
    a_dims = (0,) if ta else (1,)
    b_dims = (1,) if tb else (0,)

    def body(a_ref, b_ref, *rest):
        o_ref, scratch = rest[-1 - (gk > 1)], rest[-(gk > 1):] if gk > 1 else ()
        if gk == 1:
            o_ref[...] = _dot(a_ref[...], b_ref[...], (a_dims, b_dims)).astype(o_ref.dtype)
            return
        (acc_ref,) = scratch
        k = pl.program_id(2)

        @pl.when(k == 0)
        def _():
            acc_ref[...] = jnp.zeros_like(acc_ref)

        acc_ref[...] += _dot(a_ref[...], b_ref[...], (a_dims, b_dims))

        @pl.when(k == gk - 1)
        def _():
            o_ref[...] = acc_ref[...].astype(o_ref.dtype)

    est = 2 * (tm * tk * a.dtype.itemsize + tk * tn * b.dtype.itemsize + tm * tn * jnp.dtype(out_dtype).itemsize)
    est += tm * tn * 4 + 2 * (tm * tk + tk * tn) * 2
    return pl.pallas_call(
        body,
        name=name,
        grid=(gm, gn, gk),
        in_specs=[a_spec, b_spec] + ([ANY] if buf is not None else []),
        out_specs=o_spec,
        out_shape=jax.ShapeDtypeStruct((m, n) if buf is None else buf.shape, out_dtype),
        input_output_aliases={2: 0} if buf is not None else {},
        scratch_shapes=[pltpu.VMEM((tm, tn), F32)] if gk > 1 else [],
        compiler_params=pltpu.CompilerParams(
            dimension_semantics=("parallel", "parallel", "arbitrary"), vmem_limit_bytes=_vmem(est)
        ),
    )(*((a, b) if buf is None else (a, b, buf)))


def _resident(shape):
    return pl.BlockSpec(shape, lambda i: (0,) * len(shape), pipeline_mode=pl.Buffered(1))


def _row_tile(m, fixed_bytes, bytes_per_row, budget=50 * 2**20):
    fits = [tm for tm in _divisors(m, LANES) if fixed_bytes + tm * bytes_per_row <= budget]
    return max(fits) if fits else _divisors(m, LANES)[0]


def _mm_parts_rows(parts, fn, rows, vecs, outs, reds, *, name):
    parts = [(p[0], p[1], len(p) > 2) for p in parts]
    rows = [r if isinstance(r, tuple) else (r, r.shape[1], 0) for r in rows]
    m, n = parts[0][0].shape[0], parts[0][1].shape[0 if parts[0][2] else 1]
    npart, nr, nv, no = len(parts), len(rows), len(vecs), len(outs)
    row_bytes = sum(w * r.dtype.itemsize for r, w, _ in rows) + sum(w * jnp.dtype(dt).itemsize for (w, dt) in outs)
    a_row_bytes = sum(a.shape[1] * a.dtype.itemsize for a, _, _ in parts)
    fixed = sum(_nbytes(b.shape, b.dtype) for _, b, _ in parts)
    per_row = 2 * (a_row_bytes + row_bytes) + n * 4 * 5
    tm = _row_tile(m, fixed, per_row)
    sub = min(tm, 2 * LANES)

    def body(*refs):
        ab = refs[:2 * npart]
        row_refs, vec_refs = refs[2 * npart:2 * npart + nr], refs[2 * npart + nr:2 * npart + nr + nv]
        out_refs = refs[2 * npart + nr + nv:2 * npart + nr + nv + no]
        red_refs = refs[2 * npart + nr + nv + no:]
        if red_refs:
            @pl.when(pl.program_id(0) == 0)
            def _():
                for ref in red_refs:
                    ref[...] = jnp.zeros_like(ref)

        vecs_v = [v[...] for v in vec_refs]
        for s0 in range(0, tm, sub):
            sl = slice(s0, s0 + sub)
            r = None
            for p, (_, _, nt) in enumerate(parts):
                term = (_dot_nt if nt else _dot)(ab[2 * p][sl, :], ab[2 * p + 1][...])
                r = term if r is None else r + term
            o, rd = fn(r, [x[sl, :].astype(F32) for x in row_refs], vecs_v)
            for ref, val in zip(out_refs, o):
                ref[sl, :] = val.astype(ref.dtype)
            for ref, val in zip(red_refs, rd):
                ref[...] += val

    in_specs, operands = [], []
    for a, b, _ in parts:
        in_specs += [pl.BlockSpec((tm, a.shape[1]), lambda i: (i, 0)), _resident(b.shape)]
        operands += [a, b]
    in_specs += [pl.BlockSpec((tm, w), functools.partial(lambda cb, i: (i, cb), cb)) for _, w, cb in rows]
    in_specs += [pl.BlockSpec(v.shape, functools.partial(lambda nd, i: (0,) * nd, v.ndim)) for v in vecs]
    out_specs = [pl.BlockSpec((tm, w), lambda i: (i, 0)) for (w, _) in outs]
    out_specs += [pl.BlockSpec((1, w), lambda i: (0, 0)) for w in reds]
    out_shape = [jax.ShapeDtypeStruct((m, w), dt) for (w, dt) in outs] + [jax.ShapeDtypeStruct((1, w), F32) for w in reds]
    return pl.pallas_call(
        body,
        name=name,
        grid=(m // tm,),
        in_specs=in_specs,
        out_specs=out_specs,
        out_shape=out_shape,
        compiler_params=pltpu.CompilerParams(dimension_semantics=("arbitrary",), vmem_limit_bytes=_vmem(fixed + tm * per_row)),
    )(*operands, *[r for r, _, _ in rows], *vecs)


def _mm_ew(a, bs, fn, rows, outs, *, name):
    m, kdim = a.shape
    n = bs[0].shape[0]
    nb, nr, no = len(bs), len(rows), len(outs)
    cw = _pick(n, (2 * LANES, LANES))
    fixed = nb * n * kdim * bs[0].dtype.itemsize
    per_row = 2 * (kdim * a.dtype.itemsize + n * (sum(r.dtype.itemsize for r in rows) + sum(jnp.dtype(dt).itemsize for dt in outs)))
    per_row += cw * 4 * 4 * (nb + 4)
    tm = _row_tile(m, fixed, per_row)

    def body(*refs):
        a_ref, b_refs = refs[0], refs[1:1 + nb]
        row_refs, out_refs = refs[1 + nb:1 + nb + nr], refs[1 + nb + nr:]
        av = a_ref[...]
        for c0 in range(0, n, cw):
            cols = slice(c0, c0 + cw)
            o = fn([_dot_nt(av, b[cols, :]) for b in b_refs], [x[:, cols].astype(F32) for x in row_refs])
            for ref, val in zip(out_refs, o):
                ref[:, cols] = val.astype(ref.dtype)

    tile = pl.BlockSpec((tm, n), lambda i: (i, 0))
    return pl.pallas_call(
        body,
        name=name,
        grid=(m // tm,),
        in_specs=[pl.BlockSpec((tm, kdim), lambda i: (i, 0))] + [_resident((n, kdim))] * nb + [tile] * nr,
        out_specs=[tile] * no,
        out_shape=[jax.ShapeDtypeStruct((m, n), dt) for dt in outs],
        compiler_params=pltpu.CompilerParams(dimension_semantics=("parallel",), vmem_limit_bytes=_vmem(fixed + tm * per_row)),
    )(a, *bs, *rows)


def _rowmap(fn, rows, vecs, outs, reds, *, name, tm=None):
    norm = []
    for r in rows:
        if not isinstance(r, tuple):
            norm.append((r, r.shape[1], 0, None))
        elif len(r) == 2:
            norm.append((r[0], r[0].shape[2], 0, r[1]))
        else:
            norm.append((r[0], r[1], r[2], None))
    rows = norm
    t = rows[0][0].shape[-2]
    if tm is None:
        per_row = 2 * sum(w * a.dtype.itemsize for (a, w, _, _) in rows) + 2 * sum(w * jnp.dtype(d).itemsize for (w, d) in outs)
        per_row += 3 * 4 * max([w for (_, w, _, _) in rows] + [w for (w, _) in outs])
        tm = max(8, min(1024, (40 * 2**20) // per_row))
    tm = _pick(t, tuple(c for c in (1024, 512, 256, 128, 64, 32, 16, 8) if c <= tm))
    nr, nv, no = len(rows), len(vecs), len(outs)

    def body(*refs):
        row_refs, vec_refs = refs[:nr], refs[nr:nr + nv]
        out_refs, red_refs = refs[nr + nv:nr + nv + no], refs[nr + nv + no:]
        o, rd = fn([r[...].astype(F32) for r in row_refs], [v[...] for v in vec_refs])
        for ref, val in zip(out_refs, o):
            ref[...] = val.astype(ref.dtype)
        if red_refs:
            @pl.when(pl.program_id(0) == 0)
            def _():
                for ref in red_refs:
                    ref[...] = jnp.zeros_like(ref)

            for ref, val in zip(red_refs, rd):
                ref[...] += val

    in_specs = [pl.BlockSpec((tm, w), functools.partial(lambda cb, i: (i, cb), cb)) if lead is None
                else pl.BlockSpec((None, tm, w), functools.partial(lambda ld, i: (ld, i, 0), lead)) for (_, w, cb, lead) in rows]
    in_specs += [pl.BlockSpec(v.shape, functools.partial(lambda nd, i: (0,) * nd, v.ndim)) for v in vecs]
    out_specs = [pl.BlockSpec((tm, w), lambda i: (i, 0)) for (w, _) in outs]
    out_specs += [pl.BlockSpec((1, w), lambda i: (0, 0)) for w in reds]
    out_shape = [jax.ShapeDtypeStruct((t, w), d) for (w, d) in outs]
    out_shape += [jax.ShapeDtypeStruct((1, w), F32) for w in reds]
    est = 2 * sum(tm * w * a.dtype.itemsize for (a, w, _, _) in rows) + 2 * sum(_nbytes(v.shape, v.dtype) for v in vecs)
    est += 2 * sum(tm * w * jnp.dtype(d).itemsize for (w, d) in outs)
    est += 6 * tm * max([w for (_, w, _, _) in rows] + [w for (w, _) in outs]) * 4
    return pl.pallas_call(
        body,
        name=name,
        grid=(t // tm,),
        in_specs=in_specs,
        out_specs=out_specs,
        out_shape=out_shape,
        compiler_params=pltpu.CompilerParams(dimension_semantics=("arbitrary",), vmem_limit_bytes=_vmem(est)),
    )(*[r[0] for r in rows], *vecs)


def _colsum(v):
    return jnp.sum(v, axis=0, keepdims=True)


def _head_ones():
    i = np.arange(LANES)
    return jnp.asarray((i[:, None] // HEAD_DIM) == (i[None, :] // HEAD_DIM), MXU_DTYPE)


def _headsum(y, ones):
    parts = []
    for j in range(y.shape[1] // LANES):
        c = y[:, j * LANES:(j + 1) * LANES]
        hi = c.astype(MXU_DTYPE)
        lo = c - hi.astype(F32)
        parts.append(_dot(hi, ones) + _dot(lo, ones))
    return parts[0] if len(parts) == 1 else jnp.concatenate(parts, axis=1)


def _rot(y, c, s_lo, s_hi):
    parts = []
    for j in range(y.shape[1] // LANES):
        yc = y[:, j * LANES:(j + 1) * LANES]
        parts.append(yc * c + pltpu.roll(yc, LANES - ROT_DIM // 2, 1) * s_lo + pltpu.roll(yc, ROT_DIM // 2, 1) * s_hi)
    return parts[0] if len(parts) == 1 else jnp.concatenate(parts, axis=1)


def _rot_tables(t):
    half = ROT_DIM // 2
    inv_freq = ROPE_THETA ** (-(jnp.arange(half, dtype=F32) * 2.0) / ROT_DIM)
    ang = jnp.arange(t).astype(F32)[:, None] * inv_freq[None, :]
    cos, sin = jnp.cos(ang), jnp.sin(ang)
    z = lambda w: jnp.zeros((t, w), F32)
    c = jnp.concatenate([cos, cos, jnp.ones((t, HEAD_DIM - ROT_DIM), F32)], axis=1)
    s_lo = jnp.concatenate([-sin, z(HEAD_DIM - half)], axis=1)
    s_hi = jnp.concatenate([z(half), sin, z(HEAD_DIM - ROT_DIM)], axis=1)
    return [jnp.tile(a, (1, LANES // HEAD_DIM)) for a in (c, s_lo, s_hi)]


def _rms(x):
    return lax.rsqrt(jnp.mean(x * x, axis=-1, keepdims=True) + EPS)


def _window(kind, n, bq, t, seg):
    if kind == "na":
        rows = t // GRID_W
        rs = jnp.clip(n - NA_KH // 2, 0, rows - NA_KH)
        return rs
    nk = bq + 2 * DIL_HALF
    return jnp.clip(n * bq - DIL_HALF, 0, t - nk)


def _dil_mask(n, bq, nk, ws, seg):
    qi = n * bq + lax.broadcasted_iota(jnp.int32, (bq, nk), 0)
    ki = ws + lax.broadcasted_iota(jnp.int32, (bq, nk), 1)
    shift = int(np.log2(seg))
    return (jnp.abs(ki - qi) <= DIL_HALF) & ((ki >> shift) == (qi >> shift))


HS = 4
QW = HS * HEAD_DIM


def _head_of_lane(width=QW):
    return lax.broadcasted_iota(jnp.int32, (1, width), 1) // HEAD_DIM


def _stack_heads(a):
    head = _head_of_lane()
    return jnp.concatenate([jnp.where(head == e, a, jnp.zeros_like(a)) for e in range(HS)], axis=0)


def _unstack_heads(a, bq):
    head = _head_of_lane()
    out = jnp.zeros((bq, QW), a.dtype)
    for e in range(HS):
        out = jnp.where(head == e, a[e * bq:(e + 1) * bq], out)
    return out


def _stack_cols(blk, bq):
    head = _head_of_lane()
    return jnp.concatenate(
        [jnp.max(jnp.where(head == e, blk, -jnp.inf), axis=1, keepdims=True) for e in range(HS)], axis=0)


def _attn_geometry(kind):
    if kind == "na":
        return GRID_W, NA_KH * GRID_W, 16
    bq = 128
    return bq, bq + 2 * DIL_HALF, 8


def _attn_scores(kind, n, bq, nk, t, seg, qs, k_ref, b_ref):
    scale = HEAD_DIM ** -0.5
    if kind == "na":
        rs = _window(kind, n, bq, t, seg)
        ws = pl.multiple_of(rs * GRID_W, GRID_W)
        ro0 = rs - n + (NA_KH - 1)
        s = _dot_nt(qs, k_ref[pl.ds(ws, nk), :]) * scale
        s = s + jnp.concatenate(
            [jnp.concatenate([b_ref[e, ro0 + 2 * i] for i in range(NA_KH // 2)], axis=1) for e in range(HS)], axis=0)
        return s, ws, ro0
    ws = pl.multiple_of(_window(kind, n, bq, t, seg), DIL_HALF)
    mask = _dil_mask(n, bq, nk, ws, seg)
    s = _dot_nt(qs, k_ref[pl.ds(ws, nk), :]) * scale
    s = jnp.where(jnp.concatenate([mask] * HS, axis=0), s, NEG_INF)
    return s, ws, None


def _col_operands(*ops):
    pairs = [op if isinstance(op, tuple) else (op, 0) for op in ops]
    width = QW if isinstance(ops[0], tuple) else ops[0].shape[1]
    return (*pairs, width)


def _q_block(rows, col):
    return pl.BlockSpec((rows, QW), lambda j, n: (n, j + col))


def _kv_resident(t, col):
    return pl.BlockSpec((t, QW), lambda j, n: (0, j + col))


def _attn_fwd(q, k, v, *, kind, name, bias=None, seg=None):
    (q, cq), (k, ck), (v, cv), w = _col_operands(q, k, v)
    t = q.shape[0]
    quads = w // QW
    bq, nk, sub = _attn_geometry(kind)
    nq = t // (bq * sub)

    def body(*refs):
        if kind == "na":
            q_ref, k_ref, v_ref, b_ref, o_ref, l_ref = refs
        else:
            (q_ref, k_ref, v_ref, o_ref, l_ref), b_ref = refs, None
        for i in range(sub):
            n = pl.program_id(1) * sub + i
            rows = slice(i * bq, (i + 1) * bq)
            s, ws, _ = _attn_scores(kind, n, bq, nk, t, seg, _stack_heads(q_ref[rows, :]), k_ref, b_ref)
            m = jnp.max(s, axis=1, keepdims=True)
            p = jnp.exp(s - m)
            l = jnp.sum(p, axis=1, keepdims=True)
            o_ref[rows, :] = _unstack_heads(_dot(p / l, v_ref[pl.ds(ws, nk), :]), bq)
            l_ref[rows, :] = _unstack_heads(jnp.broadcast_to(m + jnp.log(l), (HS * bq, QW)), bq)

    blk = pl.BlockSpec((bq * sub, QW), lambda j, n: (n, j))
    in_specs = [_q_block(bq * sub, cq), _kv_resident(t, ck), _kv_resident(t, cv)]
    operands = [q, k, v]
    est = 4 * t * QW * q.dtype.itemsize + 12 * sub * HS * bq * nk * 4
    if kind == "na":
        in_specs.append(pl.BlockSpec((HS,) + bias.shape[1:], lambda j, n: (j, 0, 0, 0)))
        operands.append(bias)
        est += 2 * _nbytes((HS,) + bias.shape[1:], F32)
    return pl.pallas_call(
        body,
        name=name,
        grid=(quads, nq),
        in_specs=in_specs,
        out_specs=[blk, blk],
        out_shape=[jax.ShapeDtypeStruct((t, w), F32)] * 2,
        compiler_params=pltpu.CompilerParams(dimension_semantics=("arbitrary", "arbitrary"), vmem_limit_bytes=_vmem(est)),
    )(*operands)


def _attn_bwd(q, k, v, do, dterm, lse, *, kind, name, bias=None, seg=None):
    (q, cq), (k, ck), (v, cv), (do, cdo), (dterm, cdt), (lse, cl), w = _col_operands(q, k, v, do, dterm, lse)
    t = q.shape[0]
    quads = w // QW
    bq, nk, sub = _attn_geometry(kind)
    nq = t // (bq * sub)
    scale = HEAD_DIM ** -0.5

    def body(*refs):
        if kind == "na":
            q_ref, k_ref, v_ref, do_ref, dt_ref, l_ref, b_ref, dq_ref, dk_hbm, dv_hbm, db_ref, dk_acc, dv_acc, sem = refs
        else:
            q_ref, k_ref, v_ref, do_ref, dt_ref, l_ref, dq_ref, dk_hbm, dv_hbm, dk_acc, dv_acc, sem = refs
            b_ref = None
        j, step = pl.program_id(0), pl.program_id(1)

        @pl.when(step == 0)
        def _():
            dk_acc[...] = jnp.zeros_like(dk_acc)
            dv_acc[...] = jnp.zeros_like(dv_acc)
            if kind == "na":
                db_ref[...] = jnp.zeros_like(db_ref)

        for b in range(sub):
            n = step * sub + b
            rows = slice(b * bq, (b + 1) * bq)
            qs = _stack_heads(q_ref[rows, :])
            dos = _stack_heads(do_ref[rows, :])
            s, ws, ro0 = _attn_scores(kind, n, bq, nk, t, seg, qs, k_ref, b_ref)
            p = jnp.exp(s - _stack_cols(l_ref[rows, :], bq))
            dp = _dot_nt(dos, v_ref[pl.ds(ws, nk), :])
            ds = p * (dp - _stack_cols(dt_ref[rows, :], bq))
            if kind == "na":
                for e in range(HS):
                    for i in range(NA_KH // 2):
                        db_ref[e, ro0 + 2 * i] += ds[e * bq:(e + 1) * bq, i * LANES:(i + 1) * LANES]
            dsc = ds * scale
            dq_ref[rows, :] = _unstack_heads(_dot(dsc, k_ref[pl.ds(ws, nk), :]), bq)
            dk_acc[pl.ds(ws, nk), :] += _dot_tn(dsc, qs)
            dv_acc[pl.ds(ws, nk), :] += _dot_tn(p, dos)

        @pl.when(step == nq - 1)
        def _():
            ck = pltpu.make_async_copy(dk_acc, dk_hbm.at[j], sem.at[0])
            cv = pltpu.make_async_copy(dv_acc, dv_hbm.at[j], sem.at[1])
            ck.start()
            cv.start()
            ck.wait()
            cv.wait()

    blk = pl.BlockSpec((bq * sub, QW), lambda j, n: (n, j))
    in_specs = [_q_block(bq * sub, cq), _kv_resident(t, ck), _kv_resident(t, cv)] + [_q_block(bq * sub, c) for c in (cdo, cdt, cl)]
    operands = [q, k, v, do, dterm, lse]
    out_specs = [blk, ANY, ANY]
    out_shape = [jax.ShapeDtypeStruct((t, w), F32)] + [jax.ShapeDtypeStruct((quads, t, QW), F32)] * 2
    est = 4 * t * QW * q.dtype.itemsize + 2 * t * QW * 4 + 16 * sub * HS * bq * nk * 4
    if kind == "na":
        bspec = pl.BlockSpec((HS,) + bias.shape[1:], lambda j, n: (j, 0, 0, 0))
        in_specs.append(bspec)
        operands.append(bias)
        out_specs.append(bspec)
        out_shape.append(jax.ShapeDtypeStruct(bias.shape, F32))
        est += 4 * _nbytes((HS,) + bias.shape[1:], F32)
    res_ = pl.pallas_call(
        body,
        name=name,
        grid=(quads, nq),
        in_specs=in_specs,
        out_specs=out_specs,
        out_shape=out_shape,
        scratch_shapes=[pltpu.VMEM((t, QW), F32), pltpu.VMEM((t, QW), F32), pltpu.SemaphoreType.DMA((2,))],
        compiler_params=pltpu.CompilerParams(dimension_semantics=("arbitrary", "arbitrary"), vmem_limit_bytes=_vmem(est)),
    )(*operands)
    unquad = lambda a: [(a, i) for i in range(quads)]
    return (res_[0], unquad(res_[1]), unquad(res_[2])) + tuple(res_[3:])


def _na_onehot():
    qc = np.arange(GRID_W)[:, None]
    kc = np.arange(GRID_W)[None, :]
    start = np.clip(qc - NA_KW // 2, 0, GRID_W - NA_KW)
    inwin = (kc >= start) & (kc < start + NA_KW)
    off = kc - qc + (NA_KW - 1)
    e_mat = np.zeros((2, 32, GRID_W, 2, GRID_W), np.float32)
    for e in range(2):
        for c in range(2 * NA_KW - 1):
            e_mat[e, c, :, e, :] = (off == c) & inwin
    neg = np.where(inwin, 0.0, NEG_INF).astype(np.float32)
    neg = np.broadcast_to(neg[:, None, :], (GRID_W, 2, GRID_W)).reshape(1, GRID_W * LANES)
    return jnp.asarray(e_mat.reshape(64, GRID_W * LANES), MXU_DTYPE), jnp.asarray(neg)


def _na_rowpairs(rpb):
    p = jnp.pad(rpb, ((0, 0), (0, 0), (0, 1)))
    return jnp.concatenate([p[:, :-1], p[:, 1:]], axis=-1).reshape(NA_HEADS * (2 * NA_KH - 2), 64)


def _na_bias_table(rpb):
    r2 = _na_rowpairs(rpb)
    e_mat, neg = _na_onehot()

    def body(r_ref, e_ref, n_ref, o_ref):
        hi, mid, lo = _split3(r_ref[...])
        e = e_ref[...]
        o_ref[...] = _dot(hi, e) + _dot(mid, e) + _dot(lo, e) + n_ref[...]

    out = pl.pallas_call(
        body,
        name="na_bias_table",
        out_shape=jax.ShapeDtypeStruct((r2.shape[0], GRID_W * LANES), F32),
        compiler_params=pltpu.CompilerParams(vmem_limit_bytes=_vmem(6 * r2.shape[0] * GRID_W * LANES * 4)),
    )(r2, e_mat, neg)
    return out.reshape(NA_HEADS, 2 * NA_KH - 2, GRID_W, LANES)


def _na_bias_grad(dbt):
    e_mat, _ = _na_onehot()
    flat = dbt.reshape(NA_HEADS * (2 * NA_KH - 2), GRID_W * LANES)

    def body(d_ref, e_ref, o_ref):
        hi, mid, lo = _split3(d_ref[...])
        e = e_ref[...]
        o_ref[...] = _dot_nt(hi, e) + _dot_nt(mid, e) + _dot_nt(lo, e)

    g = pl.pallas_call(
        body,
        name="na_bias_grad",
        out_shape=jax.ShapeDtypeStruct((flat.shape[0], 64), F32),
        compiler_params=pltpu.CompilerParams(vmem_limit_bytes=_vmem(6 * flat.shape[0] * GRID_W * LANES * 4)),
    )(flat, e_mat)
    g = g.reshape(NA_HEADS, 2 * NA_KH - 2, 2, 32)[..., :2 * NA_KW - 1]
    first = jnp.pad(g[:, :, 0], ((0, 0), (0, 1), (0, 0)))
    second = jnp.pad(g[:, :, 1], ((0, 0), (1, 0), (0, 0)))
    return first + second


def _all_gather(arrs, *, name):
    na = len(arrs)

    def body(*refs):
        ins, outs = refs[:na], refs[na:2 * na]
        send_sems, recv_sems, local_sems = refs[2 * na:]
        x, y, c = lax.axis_index("x"), lax.axis_index("y"), lax.axis_index("c")
        me, sibling = (x, y, c), (x, y, 1 - c)
        chips = [(1 - x, y), (x, 1 - y), (1 - x, 1 - y)]

        def rows(a, px, py, pc):
            r = ins[a].shape[0]
            return outs[a].at[pl.ds((4 * px + 2 * py + pc) * r, r), :]

        def copy(a, k, block, to, src=None):
            return pltpu.make_async_remote_copy(
                src_ref=rows(a, *block) if src is None else src, dst_ref=rows(a, *block),
                send_sem=send_sems.at[a, k], recv_sem=recv_sems.at[a, k], device_id=to, device_id_type=MESH)

        mine = [pltpu.make_async_copy(ins[a], rows(a, *me), local_sems.at[a]) for a in range(na)]
        for cp in mine:
            cp.start()
        first = []
        for a in range(na):
            first.append(copy(a, 0, me, sibling, src=ins[a]))
            first += [copy(a, 1 + j, me, (*chip, c), src=ins[a]) for j, chip in enumerate(chips)]
        for cp in first:
            cp.start()
        passed = []
        for j, chip in enumerate(chips):
            for a in range(na):
                copy(a, 1 + j, (*chip, c), me).wait_recv()
                cp = copy(a, 4 + j, (*chip, c), sibling)
                cp.start()
                passed.append(cp)
        for a in range(na):
            copy(a, 0, sibling, me).wait_recv()
        for j, chip in enumerate(chips):
            for a in range(na):
                copy(a, 4 + j, (*chip, 1 - c), me).wait_recv()
        for cp in first + passed:
            cp.wait_send()
        for cp in mine:
            cp.wait()

    return pl.pallas_call(
        body,
        name=name,
        in_specs=[ANY] * na,
        out_specs=[ANY] * na,
        out_shape=[jax.ShapeDtypeStruct((N_DEV * a.shape[0], a.shape[1]), a.dtype) for a in arrs],
        scratch_shapes=[pltpu.SemaphoreType.DMA((na, 7)), pltpu.SemaphoreType.DMA((na, 7)), pltpu.SemaphoreType.DMA((na,))],
    )(*arrs)


HBM = pl.BlockSpec(memory_space=pltpu.HBM)
SEM = pl.BlockSpec(memory_space=pltpu.SEMAPHORE)
EFFECT = pltpu.SideEffectType.DATAFLOW_SIDE_EFFECTING


def _peer_of(k):
    x, y, c = lax.axis_index("x"), lax.axis_index("y"), lax.axis_index("c")
    return x ^ ((k >> 2) & 1), y ^ ((k >> 1) & 1), c ^ (k & 1)


def _split_copies(gather, src_ref, land_ref, send_sems, recv_sems):
    x, y, c = lax.axis_index("x"), lax.axis_index("y"), lax.axis_index("c")
    my = 4 * x + 2 * y + c
    r = src_ref.shape[0] if gather else src_ref.shape[0] // N_DEV
    copies = []
    for k in ((1, 2, 4, 6) if gather == "chip" else range(1, N_DEV)):
        px, py, pc = _peer_of(k)
        if gather:
            src, dst = src_ref, land_ref.at[pl.ds(my * r, r), :]
        else:
            src, dst = src_ref.at[pl.ds((4 * px + 2 * py + pc) * r, r), :], land_ref.at[k - 1]
        copies.append(pltpu.make_async_remote_copy(
            src_ref=src, dst_ref=dst, send_sem=send_sems.at[k - 1], recv_sem=recv_sems.at[k - 1],
            device_id=(px, py, pc), device_id_type=MESH))
    return copies


def _split_start(srcs, lands, *, gather, name, after=None):
    na = len(srcs)
    extra = [] if after is None else [after]

    def body(*refs):
        src_refs, land_refs = refs[:na], refs[na:2 * na]
        outs = refs[2 * na + len(extra):]
        for a in range(na):
            for cp in _split_copies(gather, src_refs[a], land_refs[a], outs[4 * a], outs[4 * a + 1]):
                cp.start()
        outs[4 * na][...] = jnp.zeros_like(outs[4 * na])

    out_shape, out_specs, aliases = [], [], {}
    for a in range(na):
        out_shape += [pltpu.SemaphoreType.DMA((N_DEV - 1,)), pltpu.SemaphoreType.DMA((N_DEV - 1,)),
                      pltpu.HBM(srcs[a].shape, srcs[a].dtype), pltpu.HBM(lands[a].shape, lands[a].dtype)]
        out_specs += [SEM, SEM, HBM, HBM]
        aliases[a] = 4 * a + 2
        aliases[na + a] = 4 * a + 3
    out_shape.append(jax.ShapeDtypeStruct((8, LANES), F32))
    out_specs.append(pl.BlockSpec(memory_space=pltpu.VMEM))
    res = pl.pallas_call(
        body,
        name=name,
        out_shape=tuple(out_shape),
        in_specs=[HBM] * (2 * na) + [ANY] * len(extra),
        out_specs=tuple(out_specs),
        input_output_aliases=aliases,
        compiler_params=pltpu.CompilerParams(has_side_effects=EFFECT),
    )(*[pltpu.with_memory_space_constraint(a, pltpu.HBM) for a in list(srcs) + list(lands)], *extra)
    return [tuple(res[4 * a:4 * a + 4]) for a in range(na)], res[4 * na][0, 0]


def _split_wait(handles, after, *, gather, name):
    na = len(handles)

    def body(*refs):
        src_refs, land_refs = refs[:na], refs[na:2 * na]
        sems = refs[2 * na:4 * na]
        for a in range(na):
            for cp in _split_copies(gather, src_refs[a], land_refs[a], sems[2 * a], sems[2 * a + 1]):
                cp.wait_send()
                cp.wait_recv()

    srcs = [h[2] for h in handles]
    lands = [h[3] for h in handles]
    sems = [s for h in handles for s in h[:2]]
    res = pl.pallas_call(
        body,
        name=name,
        out_shape=tuple(pltpu.HBM(a.shape, a.dtype) for a in srcs + lands),
        in_specs=[HBM] * (2 * na) + [SEM] * (2 * na) + [ANY],
        out_specs=tuple([HBM] * (2 * na)),
        input_output_aliases={i: i for i in range(2 * na)},
        compiler_params=pltpu.CompilerParams(has_side_effects=EFFECT),
    )(*srcs, *lands, *sems, after)
    return list(res[:na]), list(res[na:])


def _forward_copies(land_ref, send_sems, recv_sems):
    x, y, c = lax.axis_index("x"), lax.axis_index("y"), lax.axis_index("c")
    r = land_ref.shape[0] // N_DEV
    copies = []
    for j, k in enumerate((2, 4, 6)):
        px, py, pc = _peer_of(k)
        rows = land_ref.at[pl.ds((4 * px + 2 * py + pc) * r, r), :]
        copies.append(pltpu.make_async_remote_copy(
            src_ref=rows, dst_ref=rows, send_sem=send_sems.at[j], recv_sem=recv_sems.at[j],
            device_id=(x, y, 1 - c), device_id_type=MESH))
    return copies


def _forward_start(land, *, name):
    def body(land_ref, send_sems, recv_sems, land_thru, token):
        for cp in _forward_copies(land_ref, send_sems, recv_sems):
            cp.start()
        token[...] = jnp.zeros_like(token)

    res = pl.pallas_call(
        body,
        name=name,
        out_shape=(pltpu.SemaphoreType.DMA((3,)), pltpu.SemaphoreType.DMA((3,)), pltpu.HBM(land.shape, land.dtype),
                   jax.ShapeDtypeStruct((8, LANES), F32)),
        in_specs=[HBM],
        out_specs=(SEM, SEM, HBM, pl.BlockSpec(memory_space=pltpu.VMEM)),
        input_output_aliases={0: 2},
        compiler_params=pltpu.CompilerParams(has_side_effects=EFFECT),
    )(pltpu.with_memory_space_constraint(land, pltpu.HBM))
    return res[:3]


def _forward_wait(handle, *, name):
    send_sems, recv_sems, land = handle

    def body(land_ref, send_ref, recv_ref, land_out):
        for cp in _forward_copies(land_ref, send_ref, recv_ref):
            cp.wait_send()
            cp.wait_recv()

    return pl.pallas_call(
        body,
        name=name,
        out_shape=pltpu.HBM(land.shape, land.dtype),
        in_specs=[HBM, SEM, SEM],
        out_specs=HBM,
        input_output_aliases={0: 0},
        compiler_params=pltpu.CompilerParams(has_side_effects=EFFECT),
    )(land, send_sems, recv_sems)


def _sum8(own, recv, *, name):
    _, r, w = recv.shape
    tr = _pick(r, (256, 128, 64, 32, 16, 8))

    def body(own_ref, a_ref, o_ref):
        acc = own_ref[...].astype(F32)
        for i in range(N_DEV - 1):
            acc = acc + a_ref[i].astype(F32)
        o_ref[...] = acc

    return pl.pallas_call(
        body,
        name=name,
        grid=(r // tr,),
        in_specs=[pl.BlockSpec((tr, w), lambda i: (i, 0)), pl.BlockSpec((N_DEV - 1, tr, w), lambda i: (0, i, 0))],
        out_specs=pl.BlockSpec((tr, w), lambda i: (i, 0)),
        out_shape=jax.ShapeDtypeStruct((r, w), F32),
        compiler_params=pltpu.CompilerParams(dimension_semantics=("parallel",), vmem_limit_bytes=_vmem(4 * N_DEV * tr * w * 4)),
    )(own, recv)


def _adamw(w, g, m, v, *, name):
    def fn(rows, _):
        wv, gv, mv, vv = rows
        m1 = ADAM_B1 * mv + (1.0 - ADAM_B1) * gv
        v1 = ADAM_B2 * vv + (1.0 - ADAM_B2) * jnp.square(gv)
        m_hat = m1 / (1.0 - ADAM_B1 ** ADAM_STEP)
        v_hat = v1 / (1.0 - ADAM_B2 ** ADAM_STEP)
        delta = -ADAM_LR * (m_hat / (jnp.sqrt(v_hat) + ADAM_EPS) + ADAM_WD * wv)
        return [delta, m1, v1], []

    c = w.shape[1]
    return _rowmap(fn, [w, g, m, v], [], [(c, F32)] * 3, [], name=name)


_SMALL = ("b_ada", "g_norm1", "g_norm2", "b_gate", "g_qa", "g_ka", "g_qb", "g_kb", "rpb", "loss")


def _pack_small(parts):
    flat = []
    for nme in _SMALL:
        a = parts[nme].reshape(-1).astype(F32)
        flat.append(jnp.pad(a, (0, (-a.shape[0]) % LANES)))
    flat = jnp.concatenate(flat)
    flat = jnp.pad(flat, (0, (-flat.shape[0]) % (LANES * LANES)))
    return flat.reshape(-1, LANES)


def _unpack_small(packed, shapes):
    flat = packed.reshape(-1)
    out, pos = {}, 0
    for nme in _SMALL:
        n = int(np.prod(shapes[nme]))
        out[nme] = flat[pos:pos + n].reshape(shapes[nme])
        pos += n + (-n) % LANES
    return out


def _to_class(a, d):
    t, w = a.shape
    return a if d == 1 else a.reshape(t // d, d, w).transpose(1, 0, 2).reshape(t, w)


def _from_class(a, d):
    t, w = a.shape
    return a if d == 1 else a.reshape(d, t // d, w).transpose(1, 0, 2).reshape(t, w)


def kernel(x, c, w_ada, b_ada, g_norm1, g_norm2, w_in, b_gate, g_qa, g_ka, g_qb, g_kb, rpb, w_proj_a, w_proj_b, w_o, w_ffn_in, w_ffn_out, loss_target, m_w_ada, m_b_ada, m_g_norm1, m_g_norm2, m_w_in, m_b_gate, m_g_qa, m_g_ka, m_g_qb, m_g_kb, m_rpb, m_w_proj_a, m_w_proj_b, m_w_o, m_w_ffn_in, m_w_ffn_out, v_w_ada, v_b_ada, v_g_norm1, v_g_norm2, v_w_in, v_b_gate, v_g_qa, v_g_ka, v_g_qb, v_g_kb, v_rpb, v_w_proj_a, v_w_proj_b, v_w_o, v_w_ffn_in, v_w_ffn_out):
    t, d = x.shape[1], x.shape[2]
    d_ff = w_ffn_out.shape[1] * N_DEV
    me = 4 * lax.axis_index("x") + 2 * lax.axis_index("y") + lax.axis_index("c")
    xt, tgt = x.reshape(t, d), loss_target.reshape(t, d)
    ones = _head_ones()

    shards = [s.astype(WIRE_DTYPE) for s in (w_in[0].T, w_ffn_in[0].T, w_proj_a[0].T, w_proj_b[0].T, w_o[0], w_ffn_out[0])]
    lands = [lax.dynamic_update_slice(lax.empty((N_DEV * s.shape[0], s.shape[1]), s.dtype), s, (me * s.shape[0], 0))
             for s in shards]

    c_all = _all_gather([jnp.pad(c, ((0, 7), (0, 0)))], name="gather_c")[0][::8]
    c_all = jnp.pad(c_all, ((0, LANES - N_DEV), (0, 0)))

    def mod_body(c_ref, w_ref, b_ref, o_ref, act_ref):
        act = _silu(c_ref[...])
        act_ref[...] = act
        hi, mid, lo = _split3(act)
        w = w_ref[...]
        whi, wmid, wlo = _split3(w)
        acc = _dot(hi, whi) + (_dot(hi, wmid) + _dot(mid, whi)) + (_dot(hi, wlo) + _dot(mid, wmid) + _dot(lo, whi))
        o_ref[...] = acc + b_ref[...]

    ncol = w_ada.shape[2]
    b_ada_mine = lax.dynamic_slice(b_ada, (0, me * ncol), (1, ncol))
    mod_part, c_act = pl.pallas_call(
        mod_body,
        name="ada_mod",
        out_shape=[jax.ShapeDtypeStruct((LANES, ncol), F32), jax.ShapeDtypeStruct((LANES, d), F32)],
        compiler_params=pltpu.CompilerParams(vmem_limit_bytes=_vmem(6 * d * ncol * 4)),
    )(c_all, w_ada[0], b_ada_mine)
    mod_all = _all_gather([mod_part[:N_DEV]], name="gather_mod")[0].reshape(N_DEV, N_DEV, ncol)
    mod = lax.dynamic_index_in_dim(mod_all, me, axis=1, keepdims=False).reshape(6, d)
    sh1, sc1, gt1, sh2, sc2, gt2 = [mod[i:i + 1] for i in range(6)]

    def norm_fwd(rows, vecs):
        (xv,), (g, sc, sh) = rows, vecs
        return [xv * _rms(xv) * g * (1.0 + sc) + sh], []

    w_in_handle, w_token = _split_start(shards[:1], lands[:1], gather="chip", after=mod, name="gather_w_in_start")
    (h,) = _rowmap(norm_fwd, [xt], [g_norm1 + w_token, sc1, sh1], [(d, MXU_DTYPE)], [], name="norm1")
    n_a, n_b = 3 * WA, 3 * WB
    (w_in_t,) = _split_wait(w_in_handle, h, gather="chip", name="gather_w_in_wait")[1]
    w_in_t = _forward_wait(_forward_start(w_in_t, name="gather_w_in_forward_start"), name="gather_w_in_forward_wait")
    w_handles, w_token = _split_start(shards[1:], lands[1:], gather=True, after=w_in_t, name="gather_weights_start")
    w_in_a, w_in_b, w_in_g = w_in_t[:n_a], w_in_t[n_a:n_a + n_b], w_in_t[n_a + n_b:]
    gates = _mm(h, w_in_g, tb=True, out_dtype=ACT_DTYPE, name="proj_gates")

    rot_c, rot_lo, rot_hi = _rot_tables(t)
    tile_g = lambda g, heads: jnp.tile(g, (1, heads))

    def qk_fwd(width, rotate):
        def fn(xv, rows, vecs):
            gq, gk, on = vecs
            qkv = []
            for i, g in enumerate((gq, gk)):
                xi = xv[:, i * width:(i + 1) * width]
                r = lax.rsqrt(_headsum(xi * xi, on) * (1.0 / HEAD_DIM) + EPS)
                yi = xi * r * g
                if rotate:
                    yi = _rot(yi, rows[0], rows[1], rows[2])
                qkv.append(yi)
            qkv.append(xv[:, 2 * width:])
            if not rotate:
                return [xv] + qkv, []
            groups = [jnp.concatenate([a[:, g * WB_OUT:(g + 1) * WB_OUT] for a in qkv], axis=1)
                      for g in range(len(DIL_CONFIGS))]
            return [xv] + groups, []
        return fn

    qkv_a, qa, ka, va = _mm_parts_rows(
        [(h, w_in_a, "nt")], qk_fwd(WA, False), [], [tile_g(g_qa, NA_HEADS) + w_token, tile_g(g_ka, NA_HEADS), ones],
        [(3 * WA, ACT_DTYPE)] + [(WA, MXU_DTYPE)] * 3, [], name="proj_a_qknorm")
    qkv_b, *qkv_groups = _mm_parts_rows(
        [(h, w_in_b, "nt")], qk_fwd(WB, True), [rot_c, rot_lo, rot_hi],
        [tile_g(g_qb, DIL_HEADS), tile_g(g_kb, DIL_HEADS), ones],
        [(3 * WB, ACT_DTYPE)] + [(3 * WB_OUT, MXU_DTYPE)] * len(DIL_CONFIGS), [], name="proj_b_qknorm")

    bias_tab = _na_bias_table(rpb[0])
    o_a, lse_a = _attn_fwd(qa, ka, va, kind="na", bias=bias_tab, name="na_fwd")

    grp = []
    for g, (_, dil) in enumerate(DIL_CONFIGS):
        qkv_c = _to_class(qkv_groups[g], dil)
        og, lg = _attn_fwd((qkv_c, 0), (qkv_c, 1), (qkv_c, 2), kind="dil", seg=t // dil, name=f"dil_fwd{g}")
        if dil == 1:
            o_tok, lse_tok = og, lg
        else:
            ol = _from_class(jnp.concatenate([og, lg], axis=1), dil)
            o_tok, lse_tok = (ol, WB_OUT, 0), (ol, WB_OUT, 1)
        grp.append(dict(qkv=qkv_c, o=o_tok, lse=lse_tok, lse_c=lg, dil=dil))

    def merge_fwd(rows, _):
        o0, o1, o2, l0, l1, l2 = rows
        mx = jnp.maximum(jnp.maximum(l0, l1), l2)
        e0, e1, e2 = jnp.exp(l0 - mx), jnp.exp(l1 - mx), jnp.exp(l2 - mx)
        s = e0 + e1 + e2
        return [(e0 / s) * o0 + (e1 / s) * o1 + (e2 / s) * o2], []

    (o_b,) = _rowmap(merge_fwd, [gr["o"] for gr in grp] + [gr["lse"] for gr in grp], [], [(WB_OUT, F32)], [], name="dil_merge")

    w_pa_t, w_pb_t, w_o_f = _split_wait(w_handles[1:4], o_b, gather=True, name="gather_w_out_wait")[1]
    pa = _mm(o_a, w_pa_t, tb=True, out_dtype=ACT_DTYPE, name="proj_out_a")
    pb = _mm(o_b, w_pb_t, tb=True, out_dtype=ACT_DTYPE, name="proj_out_b")

    def gate_fwd(rows, vecs):
        gv, pav, pbv = rows
        sg = jax.nn.sigmoid(gv + vecs[0])
        return [sg[:, :d] * pav + sg[:, d:] * pbv], []

    (merged,) = _rowmap(gate_fwd, [gates, pa, pb], [b_gate], [(d, MXU_DTYPE)], [], name="gate_merge")
    def resid_norm(av, rows, vecs):
        (xv,), (gt, g, sc, sh) = rows, vecs
        x1v = xv + gt * av
        return [av, x1v, x1v * _rms(x1v) * g * (1.0 + sc) + sh], []

    att, x1, h2 = _mm_parts_rows([(merged, w_o_f)], resid_norm, [xt], [gt1, g_norm2, sc2, sh2],
                           [(d, F32), (d, F32), (d, MXU_DTYPE)], [], name="proj_o_resid_norm2")

    w_ffn_in_t, w_ffn_out_f = _split_wait([w_handles[0], w_handles[4]], h2, gather=True, name="gather_w_ffn_wait")[1]
    w_ffn_a, w_ffn_up = w_ffn_in_t[:d_ff], w_ffn_in_t[d_ff:]

    def swiglu_fwd(prods, _):
        a, up = prods
        return [a, up, _silu(a) * up]

    ua, uu, f = _mm_ew(h2, [w_ffn_a, w_ffn_up], swiglu_fwd, [], [ACT_DTYPE, ACT_DTYPE, MXU_DTYPE], name="ffn_in_swiglu")

    def loss_fn(yv, rows, vecs):
        (x1v, tv), gt = rows, vecs[0]
        err = x1v + gt * yv - tv
        dout = err * (1.0 / d)
        return [dout, dout * gt], [_colsum(err * err), _colsum(dout * yv)]

    dout, dy2, err2, dgt2 = _mm_parts_rows([(f, w_ffn_out_f)], loss_fn, [x1, tgt], [gt2], [(d, F32), (d, MXU_DTYPE)],
                                           [d, d], name="ffn_out_loss")

    dw_ffn_out = _mm(f, dy2, ta=True, out_dtype=WIRE_DTYPE, name="wgrad_ffn_out")
    def swiglu_bwd(prods, rows):
        (dfv,), (a, up) = prods, rows
        sg = jax.nn.sigmoid(a)
        return [dfv * up * (sg * (1.0 + a * (1.0 - sg))), dfv * (a * sg)]

    da, dup = _mm_ew(dy2, [w_ffn_out_f], swiglu_bwd, [ua, uu], [MXU_DTYPE, MXU_DTYPE], name="dgrad_ffn_out_swiglu_bwd")
    dw_ffn_in_t = _mm(da, h2, ta=True, into=(lax.empty((2 * d_ff, d), WIRE_DTYPE), 0), name="wgrad_ffn_in_a")
    dw_ffn_in_t = _mm(dup, h2, ta=True, into=(dw_ffn_in_t, d_ff), name="wgrad_ffn_in_up")
    land7 = lambda a: lax.empty((N_DEV - 1, a.shape[0] // N_DEV, a.shape[1]), a.dtype)
    own_block = lambda a: lax.dynamic_slice(a, (me * (a.shape[0] // N_DEV), 0), (a.shape[0] // N_DEV, a.shape[1]))
    g_ffn = [dw_ffn_in_t, dw_ffn_out]
    h_ffn, tok_ffn = _split_start(g_ffn, [land7(a) for a in g_ffn], gather=False, name="exchange_ffn_start")
    def norm_bwd(dh, xv, g, sc):
        r = _rms(xv)
        xh = xv * r
        dxh = dh * g * (1.0 + sc)
        dxv = r * (dxh - xh * jnp.mean(dxh * xh, axis=-1, keepdims=True))
        return dxv, [_colsum(dh), _colsum(dh * xh * g), _colsum(dh * xh * (1.0 + sc))]

    def norm2_bwd(dhv, rows, vecs):
        (x1v, dov, av), (g, sc, gt) = rows, vecs
        dxv, sums = norm_bwd(dhv, x1v, g, sc)
        dx1v = dov + dxv
        return [dx1v, dx1v * gt], sums + [_colsum(dx1v * av)]

    dx1, datt, dsh2, dsc2, dg2, dgt1 = _mm_parts_rows(
        [(da, w_ffn_a), (dup, w_ffn_up)], norm2_bwd, [x1, dout, att], [g_norm2 + tok_ffn, sc2, gt1],
        [(d, F32), (d, MXU_DTYPE)], [d] * 4, name="dgrad_ffn_in_norm2_bwd")
    dw_o = _mm(merged, datt, ta=True, out_dtype=WIRE_DTYPE, name="wgrad_o")
    def gate_bwd(dm, rows, vecs):
        gv, pav, pbv = rows
        sg = jax.nn.sigmoid(gv + vecs[0])
        ga, gb = sg[:, :d], sg[:, d:]
        dgp = jnp.concatenate([dm * pav * ga * (1.0 - ga), dm * pbv * gb * (1.0 - gb)], axis=1)
        return [dm * ga, dm * gb, dgp], [_colsum(dgp)]

    dpa, dpb, dgates, db_gate = _mm_parts_rows(
        [(datt, w_o_f.T)], gate_bwd, [gates, pa, pb], [b_gate],
        [(d, MXU_DTYPE), (d, MXU_DTYPE), (2 * d, MXU_DTYPE)], [2 * d], name="dgrad_o_gate_bwd")
    dw_pa_t = _mm(dpa, o_a, ta=True, out_dtype=WIRE_DTYPE, name="wgrad_proj_a")
    dw_pb_t = _mm(dpb, o_b, ta=True, out_dtype=WIRE_DTYPE, name="wgrad_proj_b")
    g_out = [dw_pa_t, dw_pb_t, dw_o]
    h_out, tok_out = _split_start(g_out, [land7(a) for a in g_out], gather=False, name="exchange_out_start")
    def delta_a(doa, rows, vecs):
        return [doa, _headsum(doa * rows[0], vecs[0])], []

    do_a, dterm_a = _mm_parts_rows([(dpa, w_pa_t)], delta_a, [o_a], [ones + tok_out.astype(ones.dtype)],
                                   [(WA, F32), (WA, F32)], [], name="dgrad_proj_a_delta")
    dqa, dka, dva, dbias = _attn_bwd(qa, ka, va, do_a, dterm_a, lse_a, kind="na", bias=bias_tab, name="na_bwd")
    g_rpb = _na_bias_grad(dbias)

    def merge_bwd(dob, rows, vecs):
        o0, o1, o2, l0, l1, l2 = rows
        on = vecs[0]
        mx = jnp.maximum(jnp.maximum(l0, l1), l2)
        e0, e1, e2 = jnp.exp(l0 - mx), jnp.exp(l1 - mx), jnp.exp(l2 - mx)
        s = e0 + e1 + e2
        ws = [e0 / s, e1 / s, e2 / s]
        dws = [_headsum(dob * o, on) for o in (o0, o1, o2)]
        mean = ws[0] * dws[0] + ws[1] * dws[1] + ws[2] * dws[2]
        return [jnp.concatenate([w * dob, w * mean], axis=1) for w in ws], []

    mb = _mm_parts_rows([(dpb, w_pb_t)], merge_bwd, [gr["o"] for gr in grp] + [gr["lse"] for gr in grp], [ones],
                        [(2 * WB_OUT, F32)] * len(grp), [], name="dgrad_proj_b_merge_bwd")
    dqb, dkb, dvb = [], [], []
    for g, gr in enumerate(grp):
        dil, qkv_c = gr["dil"], gr["qkv"]
        dd_c = _to_class(mb[g], dil)
        dq, dk, dv = _attn_bwd((qkv_c, 0), (qkv_c, 1), (qkv_c, 2), (dd_c, 0), (dd_c, 1), (gr["lse_c"], 0),
                               kind="dil", seg=t // dil, name=f"dil_bwd{g}")
        if dil == 1:
            parts = [dq, dk[0], dv[0]]
        else:
            dqkv_tok = _from_class(jnp.concatenate([dq, dk[0][0][0], dv[0][0][0]], axis=1), dil)
            parts = [(dqkv_tok, WB_OUT, i) for i in range(3)]
        dqb.append(parts[0])
        dkb.append(parts[1])
        dvb.append(parts[2])

    def qk_bwd(width, rotate, nparts):
        def fn(rows, vecs):
            gq, gk, on = vecs
            xv = rows[0]
            pos = 1
            if rotate:
                rc, rlo, rhi = rows[1:4]
                pos = 4
            cat = lambda parts: parts[0] if len(parts) == 1 else jnp.concatenate(parts, axis=1)
            ends = np.cumsum((pos,) + nparts)
            dq, dk, dv = [cat(rows[ends[i]:ends[i + 1]]) for i in range(3)]
            outs, sums = [], []
            for i, (dy, g) in enumerate(((dq, gq), (dk, gk))):
                if rotate:
                    dy = _rot(dy, rc, -rlo, -rhi)
                xi = xv[:, i * width:(i + 1) * width]
                r = lax.rsqrt(_headsum(xi * xi, on) * (1.0 / HEAD_DIM) + EPS)
                xh = xi * r
                dxh = dy * g
                outs.append(r * (dxh - xh * (_headsum(dxh * xh, on) * (1.0 / HEAD_DIM))))
                sums.append(_colsum(dy * xh))
            return [jnp.concatenate(outs + [dv], axis=1)], sums
        return fn

    dqkv_a, dg_qa, dg_ka = _rowmap(qk_bwd(WA, False, (1, len(dka), len(dva))), [qkv_a, dqa] + dka + dva,
                                   [tile_g(g_qa, NA_HEADS), tile_g(g_ka, NA_HEADS), ones],
                                   [(3 * WA, MXU_DTYPE)], [WA, WA], name="qknorm_a_bwd")
    dqkv_b, dg_qb, dg_kb = _rowmap(qk_bwd(WB, True, (3, 3, 3)), [qkv_b, rot_c, rot_lo, rot_hi] + dqb + dkb + dvb,
                                   [tile_g(g_qb, DIL_HEADS), tile_g(g_kb, DIL_HEADS), ones],
                                   [(3 * WB, MXU_DTYPE)], [WB, WB], name="qknorm_b_bwd")

    dw_in_t = jnp.concatenate([
        _mm(dqkv_a, h, ta=True, out_dtype=WIRE_DTYPE, name="wgrad_in_a"),
        _mm(dqkv_b, h, ta=True, out_dtype=WIRE_DTYPE, name="wgrad_in_b"),
        _mm(dgates, h, ta=True, out_dtype=WIRE_DTYPE, name="wgrad_in_gates")], axis=0)
    h_in, tok_in = _split_start([dw_in_t], [land7(dw_in_t)], gather=False, name="exchange_in_start")
    def norm1_bwd(dhv, rows, vecs):
        xv, dx1v = rows
        dxv, sums = norm_bwd(dhv, xv, vecs[0], vecs[1])
        return [dx1v + dxv], sums

    grad_x, dsh1, dsc1, dg1 = _mm_parts_rows(
        [(dqkv_a, w_in_a), (dqkv_b, w_in_b), (dgates, w_in_g)], norm1_bwd, [xt, dx1], [g_norm1 + tok_in, sc1],
        [(d, F32)], [d] * 3, name="dgrad_in_norm1_bwd")

    heads_sum = lambda a, heads: a.reshape(heads, HEAD_DIM).sum(axis=0)
    dmod = jnp.concatenate([dsh1, dsc1, dgt1, dsh2, dsc2, dgt2], axis=1)
    local_small = _pack_small(dict(
        b_ada=dmod, g_norm1=dg1, g_norm2=dg2, b_gate=db_gate, g_qa=heads_sum(dg_qa, NA_HEADS),
        g_ka=heads_sum(dg_ka, NA_HEADS), g_qb=heads_sum(dg_qb, DIL_HEADS), g_kb=heads_sum(dg_kb, DIL_HEADS),
        rpb=g_rpb, loss=(0.5 / d) * jnp.sum(err2)))
    srows = local_small.shape[0]
    small_all = _all_gather([local_small], name="gather_small")[0].reshape(N_DEV, srows, LANES)
    small_sum = _sum8(small_all[0], small_all[1:], name="sum_small")
    small_shapes = dict(b_ada=b_ada.shape, g_norm1=g_norm1.shape, g_norm2=g_norm2.shape, b_gate=b_gate.shape,
                        g_qa=g_qa.shape, g_ka=g_ka.shape, g_qb=g_qb.shape, g_kb=g_kb.shape, rpb=rpb.shape, loss=())
    small_w = dict(b_ada=b_ada, g_norm1=g_norm1, g_norm2=g_norm2, b_gate=b_gate, g_qa=g_qa, g_ka=g_ka, g_qb=g_qb,
                   g_kb=g_kb, rpb=rpb, loss=jnp.zeros((), F32))
    small_m = dict(b_ada=m_b_ada, g_norm1=m_g_norm1, g_norm2=m_g_norm2, b_gate=m_b_gate, g_qa=m_g_qa, g_ka=m_g_ka,
                   g_qb=m_g_qb, g_kb=m_g_kb, rpb=m_rpb, loss=jnp.zeros((), F32))
    small_v = dict(b_ada=v_b_ada, g_norm1=v_g_norm1, g_norm2=v_g_norm2, b_gate=v_b_gate, g_qa=v_g_qa, g_ka=v_g_ka,
                   g_qb=v_g_qb, g_kb=v_g_kb, rpb=v_rpb, loss=jnp.zeros((), F32))
    s_delta, s_m, s_v = _adamw(_pack_small(small_w), small_sum, _pack_small(small_m), _pack_small(small_v), name="adamw_small")
    gs = _unpack_small(small_sum, small_shapes)
    ds_, ms_, vs_ = [_unpack_small(a, small_shapes) for a in (s_delta, s_m, s_v)]

    dmod_all = small_all[:, :6 * d // LANES].reshape(N_DEV, 6 * d)
    dmod_mine = jnp.pad(lax.dynamic_slice(dmod_all, (0, me * ncol), (N_DEV, ncol)), ((0, LANES - N_DEV), (0, 0)))

    def wada_body(c_ref, dm_ref, o_ref):
        chi, cmid, clo = _split3(c_ref[...])
        dhi, dmid, dlo = _split3(dm_ref[...])
        o_ref[...] = (_dot_tn(chi, dhi) + (_dot_tn(chi, dmid) + _dot_tn(cmid, dhi))
                      + (_dot_tn(chi, dlo) + _dot_tn(cmid, dmid) + _dot_tn(clo, dhi)))

    g_w_ada = pl.pallas_call(
        wada_body,
        name="wgrad_ada",
        out_shape=jax.ShapeDtypeStruct((d, ncol), F32),
        compiler_params=pltpu.CompilerParams(vmem_limit_bytes=_vmem(4 * d * ncol * 4)),
    )(c_act, dmod_mine)

    sent, recv = _split_wait(h_in + h_ffn + h_out, small_sum, gather=False, name="exchange_wait")
    names = ("w_in", "w_ffn_in", "w_ffn_out", "w_proj_a", "w_proj_b", "w_o")
    transposed = (True, True, False, True, True, False)
    big_g = {}
    for nme, own, r, tr in zip(names, sent, recv, transposed):
        s = _sum8(own_block(own), r, name=f"sum_{nme}")
        big_g[nme] = s.T if tr else s
    big_g["w_ada"] = g_w_ada
    big_w = dict(w_ada=w_ada, w_in=w_in, w_proj_a=w_proj_a, w_proj_b=w_proj_b, w_o=w_o, w_ffn_in=w_ffn_in, w_ffn_out=w_ffn_out)
    big_m = dict(w_ada=m_w_ada, w_in=m_w_in, w_proj_a=m_w_proj_a, w_proj_b=m_w_proj_b, w_o=m_w_o, w_ffn_in=m_w_ffn_in, w_ffn_out=m_w_ffn_out)
    big_v = dict(w_ada=v_w_ada, w_in=v_w_in, w_proj_a=v_w_proj_a, w_proj_b=v_w_proj_b, w_o=v_w_o, w_ffn_in=v_w_ffn_in, w_ffn_out=v_w_ffn_out)
    grads, deltas, new_m, new_v = {}, {}, {}, {}
    for nme in big_w:
        dl, m1, v1 = _adamw(big_w[nme][0], big_g[nme], big_m[nme][0], big_v[nme][0], name=f"adamw_{nme}")
        grads[nme], deltas[nme], new_m[nme], new_v[nme] = big_g[nme][None], dl[None], m1[None], v1[None]
    for nme in _SMALL[:-1]:
        grads[nme], deltas[nme], new_m[nme], new_v[nme] = gs[nme], ds_[nme], ms_[nme], vs_[nme]

    order = ("w_ada", "b_ada", "g_norm1", "g_norm2", "w_in", "b_gate", "g_qa", "g_ka", "g_qb", "g_kb", "rpb",
             "w_proj_a", "w_proj_b", "w_o", "w_ffn_in", "w_ffn_out")
    return (gs["loss"], grad_x[None], *[grads[n] for n in order], *[deltas[n] for n in order],
            *[new_m[n] for n in order], *[new_v[n] for n in order])
```

```python
import functools

import numpy as np
import jax
import jax.numpy as jnp
from jax import lax
from jax.experimental import pallas as pl
from jax.experimental.pallas import tpu as pltpu

F32 = jnp.float32
MXU_DTYPE = jnp.bfloat16
WIRE_DTYPE = jnp.bfloat16
ACT_DTYPE = jnp.bfloat16

HEAD_DIM = 64
GRID_W = 64
NA_HEADS = 8
NA_KH = 8
NA_KW = 16
DIL_CONFIGS = ((128, 1), (512, 4), (2048, 16))
DIL_HEADS_PER_GROUP = 4
DIL_HEADS = DIL_HEADS_PER_GROUP * len(DIL_CONFIGS)
DIL_HALF = 64
ROT_DIM = HEAD_DIM // 4
ROPE_THETA = 500000.0
EPS = 1e-6
NEG_INF = -1e30
WA = NA_HEADS * HEAD_DIM
WB = DIL_HEADS * HEAD_DIM
WB_OUT = DIL_HEADS_PER_GROUP * HEAD_DIM
ADAM_LR = 0.001
ADAM_B1 = 0.9
ADAM_B2 = 0.999
ADAM_EPS = 1e-08
ADAM_WD = 0.01
ADAM_STEP = 10

N_DEV = 8
LANES = 128
VMEM_CAP = 60 * 2**20
VMEM_FLOOR = 56 * 2**20
MESH = pl.DeviceIdType.MESH
ANY = pl.BlockSpec(memory_space=pl.ANY)


def _vmem(nbytes):
    return int(min(VMEM_CAP, max(VMEM_FLOOR, nbytes * 5 // 4 + 4 * 2**20)))


def _pick(dim, cands):
    for c in cands:
        if c <= dim and dim % c == 0:
            return c
    return dim


def _nbytes(shape, dtype):
    return int(np.prod(shape)) * jnp.dtype(dtype).itemsize


def _dot(a, b, dims=((1,), (0,))):
    return lax.dot_general(a.astype(MXU_DTYPE), b.astype(MXU_DTYPE), (dims, ((), ())), preferred_element_type=F32)


def _dot_nt(a, b):
    return _dot(a, b, ((1,), (1,)))


def _dot_tn(a, b):
    return _dot(a, b, ((0,), (0,)))


def _split3(a):
    hi = a.astype(jnp.bfloat16)
    r1 = a - hi.astype(F32)
    mid = r1.astype(jnp.bfloat16)
    lo = (r1 - mid.astype(F32)).astype(jnp.bfloat16)
    return hi, mid, lo


def _silu(x):
    return x * jax.nn.sigmoid(x)


def _divisors(dim, unit):
    return [c for c in range(unit, dim + 1, unit) if dim % c == 0] or [dim]


def _mm_tiles(m, n, kdim, a_item, b_item, o_item, row_off=0):
    step_us, hbm_bytes_per_us, flops_per_us, budget = 0.35, 3.0e6, 8.0e8, 40 * 2**20
    best = None
    for tm in _divisors(m, LANES):
        for tn in _divisors(n, LANES):
            for tk in _divisors(kdim, LANES):
                if row_off % tm:
                    continue
                gm, gn, gk = m // tm, n // tn, kdim // tk
                vmem = 2 * (tm * tk * a_item + tk * tn * b_item + tm * tn * o_item) + 2 * (tm * tk + tk * tn)
                vmem += tm * tn * 4 * ((1 if gk > 1 else 0) + 1)
                if vmem > budget:
                    continue
                a_reads = m * kdim * a_item * (gn if gk > 1 else 1)
                traffic = a_reads + kdim * n * b_item * gm + m * n * o_item
                cost = gm * gn * gk * step_us + max(traffic / hbm_bytes_per_us, 2.0 * m * n * kdim / flops_per_us)
                if best is None or cost < best[0]:
                    best = (cost, tm, tn, tk)
    return best[1:]


def _mm(a, b, *, name, ta=False, tb=False, out_dtype=F32, into=None):
    if ta:
        kdim, m = a.shape
    else:
        m, kdim = a.shape
    n = b.shape[0] if tb else b.shape[1]
    assert b.shape[1 if tb else 0] == kdim
    buf, row_off = into if into is not None else (None, 0)
    if buf is not None:
        out_dtype = buf.dtype
    tm, tn, tk = _mm_tiles(m, n, kdim, a.dtype.itemsize, b.dtype.itemsize, jnp.dtype(out_dtype).itemsize, row_off)
    gm, gn, gk = m // tm, n // tn, kdim // tk
    ob = row_off // tm

    a_spec = pl.BlockSpec((tk, tm), lambda i, j, k: (k, i)) if ta else pl.BlockSpec((tm, tk), lambda i, j, k: (i, k))
    b_spec = pl.BlockSpec((tn, tk), lambda i, j, k: (j, k)) if tb else pl.BlockSpec((tk, tn), lambda i, j, k: (k, j))
    o_spec = pl.BlockSpec((tm, tn), lambda i, j, k: (i + ob, j))
    a_dims = (0,) if ta else (1,)
    b_dims = (1,) if tb else (0,)

    def body(a_ref, b_ref, *rest):
        o_ref, scratch = rest[-1 - (gk > 1)], rest[-(gk > 1):] if gk > 1 else ()
        if gk == 1:
            o_ref[...] = _dot(a_ref[...], b_ref[...], (a_dims, b_dims)).astype(o_ref.dtype)
            return
        (acc_ref,) = scratch
        k = pl.program_id(2)

        @pl.when(k == 0)
        def _():
            acc_ref[...] = jnp.zeros_like(acc_ref)

        acc_ref[...] += _dot(a_ref[...], b_ref[...], (a_dims, b_dims))

        @pl.when(k == gk - 1)
        def _():
            o_ref[...] = acc_ref[...].astype(o_ref.dtype)

    est = 2 * (tm * tk * a.dtype.itemsize + tk * tn * b.dtype.itemsize + tm * tn * jnp.dtype(out_dtype).itemsize)
    est += tm * tn * 4 + 2 * (tm * tk + tk * tn) * 2
    return pl.pallas_call(
        body,
        name=name,
        grid=(gm, gn, gk),
        in_specs=[a_spec, b_spec] + ([ANY] if buf is not None else []),
        out_specs=o_spec,
        out_shape=jax.ShapeDtypeStruct((m, n) if buf is None else buf.shape, out_dtype),
        input_output_aliases={2: 0} if buf is not None else {},
        scratch_shapes=[pltpu.VMEM((tm, tn), F32)] if gk > 1 else [],
        compiler_params=pltpu.CompilerParams(
            dimension_semantics=("parallel", "parallel", "arbitrary"), vmem_limit_bytes=_vmem(est)
        ),
    )(*((a, b) if buf is None else (a, b, buf)))


def _resident(shape):
    return pl.BlockSpec(shape, lambda i: (0,) * len(shape), pipeline_mode=pl.Buffered(1))


def _row_tile(m, fixed_bytes, bytes_per_row, budget=50 * 2**20):
    fits = [tm for tm in _divisors(m, LANES) if fixed_bytes + tm * bytes_per_row <= budget]
    return max(fits) if fits else _divisors(m, LANES)[0]


def _mm_parts_rows(parts, fn, rows, vecs, outs, reds, *, name):
    parts = [(p[0], p[1], len(p) > 2) for p in parts]
    rows = [r if isinstance(r, tuple) else (r, r.shape[1], 0) for r in rows]
    m, n = parts[0][0].shape[0], parts[0][1].shape[0 if parts[0][2] else 1]
    npart, nr, nv, no = len(parts), len(rows), len(vecs), len(outs)
    row_bytes = sum(w * r.dtype.itemsize for r, w, _ in rows) + sum(w * jnp.dtype(dt).itemsize for (w, dt) in outs)
    a_row_bytes = sum(a.shape[1] * a.dtype.itemsize for a, _, _ in parts)
    fixed = sum(_nbytes(b.shape, b.dtype) for _, b, _ in parts)
    per_row = 2 * (a_row_bytes + row_bytes) + n * 4 * 5
    tm = _row_tile(m, fixed, per_row)
    sub = min(tm, 2 * LANES)

    def body(*refs):
        ab = refs[:2 * npart]
        row_refs, vec_refs = refs[2 * npart:2 * npart + nr], refs[2 * npart + nr:2 * npart + nr + nv]
        out_refs = refs[2 * npart + nr + nv:2 * npart + nr + nv + no]
        red_refs = refs[2 * npart + nr + nv + no:]
        if red_refs:
            @pl.when(pl.program_id(0) == 0)
            def _():
                for ref in red_refs:
                    ref[...] = jnp.zeros_like(ref)

        vecs_v = [v[...] for v in vec_refs]
        for s0 in range(0, tm, sub):
            sl = slice(s0, s0 + sub)
            r = None
            for p, (_, _, nt) in enumerate(parts):
                term = (_dot_nt if nt else _dot)(ab[2 * p][sl, :], ab[2 * p + 1][...])
                r = term if r is None else r + term
            o, rd = fn(r, [x[sl, :].astype(F32) for x in row_refs], vecs_v)
            for ref, val in zip(out_refs, o):
                ref[sl, :] = val.astype(ref.dtype)
            for ref, val in zip(red_refs, rd):
                ref[...] += val

    in_specs, operands = [], []
    for a, b, _ in parts:
        in_specs += [pl.BlockSpec((tm, a.shape[1]), lambda i: (i, 0)), _resident(b.shape)]
        operands += [a, b]
    in_specs += [pl.BlockSpec((tm, w), functools.partial(lambda cb, i: (i, cb), cb)) for _, w, cb in rows]
    in_specs += [pl.BlockSpec(v.shape, functools.partial(lambda nd, i: (0,) * nd, v.ndim)) for v in vecs]
    out_specs = [pl.BlockSpec((tm, w), lambda i: (i, 0)) for (w, _) in outs]
    out_specs += [pl.BlockSpec((1, w), lambda i: (0, 0)) for w in reds]
    out_shape = [jax.ShapeDtypeStruct((m, w), dt) for (w, dt) in outs] + [jax.ShapeDtypeStruct((1, w), F32) for w in reds]
    return pl.pallas_call(
        body,
        name=name,
        grid=(m // tm,),
        in_specs=in_specs,
        out_specs=out_specs,
        out_shape=out_shape,
        compiler_params=pltpu.CompilerParams(dimension_semantics=("arbitrary",), vmem_limit_bytes=_vmem(fixed + tm * per_row)),
    )(*operands, *[r for r, _, _ in rows], *vecs)


def _mm_ew(a, bs, fn, rows, outs, *, name):
    m, kdim = a.shape
    n = bs[0].shape[0]
    nb, nr, no = len(bs), len(rows), len(outs)
    cw = _pick(n, (2 * LANES, LANES))
    fixed = nb * n * kdim * bs[0].dtype.itemsize
    per_row = 2 * (kdim * a.dtype.itemsize + n * (sum(r.dtype.itemsize for r in rows) + sum(jnp.dtype(dt).itemsize for dt in outs)))
    per_row += cw * 4 * 4 * (nb + 4)
    tm = _row_tile(m, fixed, per_row)

    def body(*refs):
        a_ref, b_refs = refs[0], refs[1:1 + nb]
        row_refs, out_refs = refs[1 + nb:1 + nb + nr], refs[1 + nb + nr:]
        av = a_ref[...]
        for c0 in range(0, n, cw):
            cols = slice(c0, c0 + cw)
            o = fn([_dot_nt(av, b[cols, :]) for b in b_refs], [x[:, cols].astype(F32) for x in row_refs])
            for ref, val in zip(out_refs, o):
                ref[:, cols] = val.astype(ref.dtype)

    tile = pl.BlockSpec((tm, n), lambda i: (i, 0))
    return pl.pallas_call(
        body,
        name=name,
        grid=(m // tm,),
        in_specs=[pl.BlockSpec((tm, kdim), lambda i: (i, 0))] + [_resident((n, kdim))] * nb + [tile] * nr,
        out_specs=[tile] * no,
        out_shape=[jax.ShapeDtypeStruct((m, n), dt) for dt in outs],
        compiler_params=pltpu.CompilerParams(dimension_semantics=("parallel",), vmem_limit_bytes=_vmem(fixed + tm * per_row)),
    )(a, *bs, *rows)


def _rowmap(fn, rows, vecs, outs, reds, *, name, tm=None):
    norm = []
    for r in rows:
        if not isinstance(r, tuple):
            norm.append((r, r.shape[1], 0, None))
        elif len(r) == 2:
            norm.append((r[0], r[0].shape[2], 0, r[1]))
        else:
            norm.append((r[0], r[1], r[2], None))
    rows = norm
    t = rows[0][0].shape[-2]
    if tm is None:
        per_row = 2 * sum(w * a.dtype.itemsize for (a, w, _, _) in rows) + 2 * sum(w * jnp.dtype(d).itemsize for (w, d) in outs)
        per_row += 3 * 4 * max([w for (_, w, _, _) in rows] + [w for (w, _) in outs])
        tm = max(8, min(1024, (40 * 2**20) // per_row))
    tm = _pick(t, tuple(c for c in (1024, 512, 256, 128, 64, 32, 16, 8) if c <= tm))
    nr, nv, no = len(rows), len(vecs), len(outs)

    def body(*refs):
        row_refs, vec_refs = refs[:nr], refs[nr:nr + nv]
        out_refs, red_refs = refs[nr + nv:nr + nv + no], refs[nr + nv + no:]
        o, rd = fn([r[...].astype(F32) for r in row_refs], [v[...] for v in vec_refs])
        for ref, val in zip(out_refs, o):
            ref[...] = val.astype(ref.dtype)
        if red_refs:
            @pl.when(pl.program_id(0) == 0)
            def _():
                for ref in red_refs:
                    ref[...] = jnp.zeros_like(ref)

            for ref, val in zip(red_refs, rd):
                ref[...] += val

    in_specs = [pl.BlockSpec((tm, w), functools.partial(lambda cb, i: (i, cb), cb)) if lead is None
                else pl.BlockSpec((None, tm, w), functools.partial(lambda ld, i: (ld, i, 0), lead)) for (_, w, cb, lead) in rows]
    in_specs += [pl.BlockSpec(v.shape, functools.partial(lambda nd, i: (0,) * nd, v.ndim)) for v in vecs]
    out_specs = [pl.BlockSpec((tm, w), lambda i: (i, 0)) for (w, _) in outs]
    out_specs += [pl.BlockSpec((1, w), lambda i: (0, 0)) for w in reds]
    out_shape = [jax.ShapeDtypeStruct((t, w), d) for (w, d) in outs]
    out_shape += [jax.ShapeDtypeStruct((1, w), F32) for w in reds]
    est = 2 * sum(tm * w * a.dtype.itemsize for (a, w, _, _) in rows) + 2 * sum(_nbytes(v.shape, v.dtype) for v in vecs)
    est += 2 * sum(tm * w * jnp.dtype(d).itemsize for (w, d) in outs)
    est += 6 * tm * max([w for (_, w, _, _) in rows] + [w for (w, _) in outs]) * 4
    return pl.pallas_call(
        body,
        name=name,
        grid=(t // tm,),
        in_specs=in_specs,
        out_specs=out_specs,
        out_shape=out_shape,
        compiler_params=pltpu.CompilerParams(dimension_semantics=("arbitrary",), vmem_limit_bytes=_vmem(est)),
    )(*[r[0] for r in rows], *vecs)


def _colsum(v):
    return jnp.sum(v, axis=0, keepdims=True)


def _head_ones():
    i = np.arange(LANES)
    return jnp.asarray((i[:, None] // HEAD_DIM) == (i[None, :] // HEAD_DIM), MXU_DTYPE)


def _headsum(y, ones):
    parts = []
    for j in range(y.shape[1] // LANES):
        c = y[:, j * LANES:(j + 1) * LANES]
        hi = c.astype(MXU_DTYPE)
        lo = c - hi.astype(F32)
        parts.append(_dot(hi, ones) + _dot(lo, ones))
    return parts[0] if len(parts) == 1 else jnp.concatenate(parts, axis=1)


def _rot(y, c, s_lo, s_hi):
    parts = []
    for j in range(y.shape[1] // LANES):
        yc = y[:, j * LANES:(j + 1) * LANES]
        parts.append(yc * c + pltpu.roll(yc, LANES - ROT_DIM // 2, 1) * s_lo + pltpu.roll(yc, ROT_DIM // 2, 1) * s_hi)
    return parts[0] if len(parts) == 1 else jnp.concatenate(parts, axis=1)


def _rot_tables(t):
    half = ROT_DIM // 2
    inv_freq = ROPE_THETA ** (-(jnp.arange(half, dtype=F32) * 2.0) / ROT_DIM)
    ang = jnp.arange(t).astype(F32)[:, None] * inv_freq[None, :]
    cos, sin = jnp.cos(ang), jnp.sin(ang)
    z = lambda w: jnp.zeros((t, w), F32)
    c = jnp.concatenate([cos, cos, jnp.ones((t, HEAD_DIM - ROT_DIM), F32)], axis=1)
    s_lo = jnp.concatenate([-sin, z(HEAD_DIM - half)], axis=1)
    s_hi = jnp.concatenate([z(half), sin, z(HEAD_DIM - ROT_DIM)], axis=1)
    return [jnp.tile(a, (1, LANES // HEAD_DIM)) for a in (c, s_lo, s_hi)]


def _rms(x):
    return lax.rsqrt(jnp.mean(x * x, axis=-1, keepdims=True) + EPS)


def _window(kind, n, bq, t, seg):
    if kind == "na":
        rows = t // GRID_W
        rs = jnp.clip(n - NA_KH // 2, 0, rows - NA_KH)
        return rs
    nk = bq + 2 * DIL_HALF
    return jnp.clip(n * bq - DIL_HALF, 0, t - nk)


def _dil_mask(n, bq, nk, ws, seg):
    qi = n * bq + lax.broadcasted_iota(jnp.int32, (bq, nk), 0)
    ki = ws + lax.broadcasted_iota(jnp.int32, (bq, nk), 1)
    shift = int(np.log2(seg))
    return (jnp.abs(ki - qi) <= DIL_HALF) & ((ki >> shift) == (qi >> shift))


HS = 4
QW = HS * HEAD_DIM


def _head_of_lane(width=QW):
    return lax.broadcasted_iota(jnp.int32, (1, width), 1) // HEAD_DIM


def _stack_heads(a):
    head = _head_of_lane()
    return jnp.concatenate([jnp.where(head == e, a, jnp.zeros_like(a)) for e in range(HS)], axis=0)


def _unstack_heads(a, bq):
    head = _head_of_lane()
    out = jnp.zeros((bq, QW), a.dtype)
    for e in range(HS):
        out = jnp.where(head == e, a[e * bq:(e + 1) * bq], out)
    return out


def _stack_cols(blk, bq):
    head = _head_of_lane()
    return jnp.concatenate(
        [jnp.max(jnp.where(head == e, blk, -jnp.inf), axis=1, keepdims=True) for e in range(HS)], axis=0)


def _attn_geometry(kind):
    if kind == "na":
        return GRID_W, NA_KH * GRID_W, 16
    bq = 128
    return bq, bq + 2 * DIL_HALF, 8


def _attn_scores(kind, n, bq, nk, t, seg, qs, k_ref, b_ref):
    scale = HEAD_DIM ** -0.5
    if kind == "na":
        rs = _window(kind, n, bq, t, seg)
        ws = pl.multiple_of(rs * GRID_W, GRID_W)
        ro0 = rs - n + (NA_KH - 1)
        s = _dot_nt(qs, k_ref[pl.ds(ws, nk), :]) * scale
        s = s + jnp.concatenate(
            [jnp.concatenate([b_ref[e, ro0 + 2 * i] for i in range(NA_KH // 2)], axis=1) for e in range(HS)], axis=0)
        return s, ws, ro0
    ws = pl.multiple_of(_window(kind, n, bq, t, seg), DIL_HALF)
    mask = _dil_mask(n, bq, nk, ws, seg)
    s = _dot_nt(qs, k_ref[pl.ds(ws, nk), :]) * scale
    s = jnp.where(jnp.concatenate([mask] * HS, axis=0), s, NEG_INF)
    return s, ws, None


def _col_operands(*ops):
    pairs = [op if isinstance(op, tuple) else (op, 0) for op in ops]
    width = QW if isinstance(ops[0], tuple) else ops[0].shape[1]
    return (*pairs, width)


def _q_block(rows, col):
    return pl.BlockSpec((rows, QW), lambda j, n: (n, j + col))


def _kv_resident(t, col):
    return pl.BlockSpec((t, QW), lambda j, n: (0, j + col))


def _attn_fwd(q, k, v, *, kind, name, bias=None, seg=None):
    (q, cq), (k, ck), (v, cv), w = _col_operands(q, k, v)
    t = q.shape[0]
    quads = w // QW
    bq, nk, sub = _attn_geometry(kind)
    nq = t // (bq * sub)

    def body(*refs):
        if kind == "na":
            q_ref, k_ref, v_ref, b_ref, o_ref, l_ref = refs
        else:
            (q_ref, k_ref, v_ref, o_ref, l_ref), b_ref = refs, None
        for i in range(sub):
            n = pl.program_id(1) * sub + i
            rows = slice(i * bq, (i + 1) * bq)
            s, ws, _ = _attn_scores(kind, n, bq, nk, t, seg, _stack_heads(q_ref[rows, :]), k_ref, b_ref)
            m = jnp.max(s, axis=1, keepdims=True)
            p = jnp.exp(s - m)
            l = jnp.sum(p, axis=1, keepdims=True)
            o_ref[rows, :] = _unstack_heads(_dot(p / l, v_ref[pl.ds(ws, nk), :]), bq)
            l_ref[rows, :] = _unstack_heads(jnp.broadcast_to(m + jnp.log(l), (HS * bq, QW)), bq)

    blk = pl.BlockSpec((bq * sub, QW), lambda j, n: (n, j))
    in_specs = [_q_block(bq * sub, cq), _kv_resident(t, ck), _kv_resident(t, cv)]
    operands = [q, k, v]
    est = 4 * t * QW * q.dtype.itemsize + 12 * sub * HS * bq * nk * 4
    if kind == "na":
        in_specs.append(pl.BlockSpec((HS,) + bias.shape[1:], lambda j, n: (j, 0, 0, 0)))
        operands.append(bias)
        est += 2 * _nbytes((HS,) + bias.shape[1:], F32)
    return pl.pallas_call(
        body,
        name=name,
        grid=(quads, nq),
        in_specs=in_specs,
        out_specs=[blk, blk],
        out_shape=[jax.ShapeDtypeStruct((t, w), F32)] * 2,
        compiler_params=pltpu.CompilerParams(dimension_semantics=("arbitrary", "arbitrary"), vmem_limit_bytes=_vmem(est)),
    )(*operands)


def _attn_bwd(q, k, v, do, dterm, lse, *, kind, name, bias=None, seg=None):
    (q, cq), (k, ck), (v, cv), (do, cdo), (dterm, cdt), (lse, cl), w = _col_operands(q, k, v, do, dterm, lse)
    t = q.shape[0]
    quads = w // QW
    bq, nk, sub = _attn_geometry(kind)
    nq = t // (bq * sub)
    scale = HEAD_DIM ** -0.5

    def body(*refs):
        if kind == "na":
            q_ref, k_ref, v_ref, do_ref, dt_ref, l_ref, b_ref, dq_ref, dk_hbm, dv_hbm, db_ref, dk_acc, dv_acc, sem = refs
        else:
            q_ref, k_ref, v_ref, do_ref, dt_ref, l_ref, dq_ref, dk_hbm, dv_hbm, dk_acc, dv_acc, sem = refs
            b_ref = None
        j, step = pl.program_id(0), pl.program_id(1)

        @pl.when(step == 0)
        def _():
            dk_acc[...] = jnp.zeros_like(dk_acc)
            dv_acc[...] = jnp.zeros_like(dv_acc)
            if kind == "na":
                db_ref[...] = jnp.zeros_like(db_ref)

        for b in range(sub):
            n = step * sub + b
            rows = slice(b * bq, (b + 1) * bq)
            qs = _stack_heads(q_ref[rows, :])
            dos = _stack_heads(do_ref[rows, :])
            s, ws, ro0 = _attn_scores(kind, n, bq, nk, t, seg, qs, k_ref, b_ref)
            p = jnp.exp(s - _stack_cols(l_ref[rows, :], bq))
            dp = _dot_nt(dos, v_ref[pl.ds(ws, nk), :])
            ds = p * (dp - _stack_cols(dt_ref[rows, :], bq))
            if kind == "na":
                for e in range(HS):
                    for i in range(NA_KH // 2):
                        db_ref[e, ro0 + 2 * i] += ds[e * bq:(e + 1) * bq, i * LANES:(i + 1) * LANES]
            dsc = ds * scale
            dq_ref[rows, :] = _unstack_heads(_dot(dsc, k_ref[pl.ds(ws, nk), :]), bq)
            dk_acc[pl.ds(ws, nk), :] += _dot_tn(dsc, qs)
            dv_acc[pl.ds(ws, nk), :] += _dot_tn(p, dos)

        @pl.when(step == nq - 1)
        def _():
            ck = pltpu.make_async_copy(dk_acc, dk_hbm.at[j], sem.at[0])
            cv = pltpu.make_async_copy(dv_acc, dv_hbm.at[j], sem.at[1])
            ck.start()
            cv.start()
            ck.wait()
            cv.wait()

    blk = pl.BlockSpec((bq * sub, QW), lambda j, n: (n, j))
    in_specs = [_q_block(bq * sub, cq), _kv_resident(t, ck), _kv_resident(t, cv)] + [_q_block(bq * sub, c) for c in (cdo, cdt, cl)]
    operands = [q, k, v, do, dterm, lse]
    out_specs = [blk, ANY, ANY]
    out_shape = [jax.ShapeDtypeStruct((t, w), F32)] + [jax.ShapeDtypeStruct((quads, t, QW), F32)] * 2
    est = 4 * t * QW * q.dtype.itemsize + 2 * t * QW * 4 + 16 * sub * HS * bq * nk * 4
    if kind == "na":
        bspec = pl.BlockSpec((HS,) + bias.shape[1:], lambda j, n: (j, 0, 0, 0))
        in_specs.append(bspec)
        operands.append(bias)
        out_specs.append(bspec)
        out_shape.append(jax.ShapeDtypeStruct(bias.shape, F32))
        est += 4 * _nbytes((HS,) + bias.shape[1:], F32)
    res_ = pl.pallas_call(
        body,
        name=name,
        grid=(quads, nq),
        in_specs=in_specs,
        out_specs=out_specs,
        out_shape=out_shape,
        scratch_shapes=[pltpu.VMEM((t, QW), F32), pltpu.VMEM((t, QW), F32), pltpu.SemaphoreType.DMA((2,))],
        compiler_params=pltpu.CompilerParams(dimension_semantics=("arbitrary", "arbitrary"), vmem_limit_bytes=_vmem(est)),
    )(*operands)
    unquad = lambda a: [(a, i) for i in range(quads)]
    return (res_[0], unquad(res_[1]), unquad(res_[2])) + tuple(res_[3:])


def _na_onehot():
    qc = np.arange(GRID_W)[:, None]
    kc = np.arange(GRID_W)[None, :]
    start = np.clip(qc - NA_KW // 2, 0, GRID_W - NA_KW)
    inwin = (kc >= start) & (kc < start + NA_KW)
    off = kc - qc + (NA_KW - 1)
    e_mat = np.zeros((2, 32, GRID_W, 2, GRID_W), np.float32)
    for e in range(2):
        for c in range(2 * NA_KW - 1):
            e_mat[e, c, :, e, :] = (off == c) & inwin
    neg = np.where(inwin, 0.0, NEG_INF).astype(np.float32)
    neg = np.broadcast_to(neg[:, None, :], (GRID_W, 2, GRID_W)).reshape(1, GRID_W * LANES)
    return jnp.asarray(e_mat.reshape(64, GRID_W * LANES), MXU_DTYPE), jnp.asarray(neg)


def _na_rowpairs(rpb):
    p = jnp.pad(rpb, ((0, 0), (0, 0), (0, 1)))
    return jnp.concatenate([p[:, :-1], p[:, 1:]], axis=-1).reshape(NA_HEADS * (2 * NA_KH - 2), 64)


def _na_bias_table(rpb):
    r2 = _na_rowpairs(rpb)
    e_mat, neg = _na_onehot()

    def body(r_ref, e_ref, n_ref, o_ref):
        hi, mid, lo = _split3(r_ref[...])
        e = e_ref[...]
        o_ref[...] = _dot(hi, e) + _dot(mid, e) + _dot(lo, e) + n_ref[...]

    out = pl.pallas_call(
        body,
        name="na_bias_table",
        out_shape=jax.ShapeDtypeStruct((r2.shape[0], GRID_W * LANES), F32),
        compiler_params=pltpu.CompilerParams(vmem_limit_bytes=_vmem(6 * r2.shape[0] * GRID_W * LANES * 4)),
    )(r2, e_mat, neg)
    return out.reshape(NA_HEADS, 2 * NA_KH - 2, GRID_W, LANES)


def _na_bias_grad(dbt):
    e_mat, _ = _na_onehot()
    flat = dbt.reshape(NA_HEADS * (2 * NA_KH - 2), GRID_W * LANES)

    def body(d_ref, e_ref, o_ref):
        hi, mid, lo = _split3(d_ref[...])
        e = e_ref[...]
        o_ref[...] = _dot_nt(hi, e) + _dot_nt(mid, e) + _dot_nt(lo, e)

    g = pl.pallas_call(
        body,
        name="na_bias_grad",
        out_shape=jax.ShapeDtypeStruct((flat.shape[0], 64), F32),
        compiler_params=pltpu.CompilerParams(vmem_limit_bytes=_vmem(6 * flat.shape[0] * GRID_W * LANES * 4)),
    )(flat, e_mat)
    g = g.reshape(NA_HEADS, 2 * NA_KH - 2, 2, 32)[..., :2 * NA_KW - 1]
    first = jnp.pad(g[:, :, 0], ((0, 0), (0, 1), (0, 0)))
    second = jnp.pad(g[:, :, 1], ((0, 0), (1, 0), (0, 0)))
    return first + second


def _all_gather(arrs, *, name):
    na = len(arrs)

    def body(*refs):
        ins, outs = refs[:na], refs[na:2 * na]
        send_sems, recv_sems, local_sems = refs[2 * na:]
        x, y, c = lax.axis_index("x"), lax.axis_index("y"), lax.axis_index("c")
        me, sibling = (x, y, c), (x, y, 1 - c)
        chips = [(1 - x, y), (x, 1 - y), (1 - x, 1 - y)]

        def rows(a, px, py, pc):
            r = ins[a].shape[0]
            return outs[a].at[pl.ds((4 * px + 2 * py + pc) * r, r), :]

        def copy(a, k, block, to, src=None):
            return pltpu.make_async_remote_copy(
                src_ref=rows(a, *block) if src is None else src, dst_ref=rows(a, *block),
                send_sem=send_sems.at[a, k], recv_sem=recv_sems.at[a, k], device_id=to, device_id_type=MESH)

        mine = [pltpu.make_async_copy(ins[a], rows(a, *me), local_sems.at[a]) for a in range(na)]
        for cp in mine:
            cp.start()
        first = []
        for a in range(na):
            first.append(copy(a, 0, me, sibling, src=ins[a]))
            first += [copy(a, 1 + j, me, (*chip, c), src=ins[a]) for j, chip in enumerate(chips)]
        for cp in first:
            cp.start()
        passed = []
        for j, chip in enumerate(chips):
            for a in range(na):
                copy(a, 1 + j, (*chip, c), me).wait_recv()
                cp = copy(a, 4 + j, (*chip, c), sibling)
                cp.start()
                passed.append(cp)
        for a in range(na):
            copy(a, 0, sibling, me).wait_recv()
        for j, chip in enumerate(chips):
            for a in range(na):
                copy(a, 4 + j, (*chip, 1 - c), me).wait_recv()
        for cp in first + passed:
            cp.wait_send()
        for cp in mine:
            cp.wait()

    return pl.pallas_call(
        body,
        name=name,
        in_specs=[ANY] * na,
        out_specs=[ANY] * na,
        out_shape=[jax.ShapeDtypeStruct((N_DEV * a.shape[0], a.shape[1]), a.dtype) for a in arrs],
        scratch_shapes=[pltpu.SemaphoreType.DMA((na, 7)), pltpu.SemaphoreType.DMA((na, 7)), pltpu.SemaphoreType.DMA((na,))],
    )(*arrs)


HBM = pl.BlockSpec(memory_space=pltpu.HBM)
SEM = pl.BlockSpec(memory_space=pltpu.SEMAPHORE)
EFFECT = pltpu.SideEffectType.DATAFLOW_SIDE_EFFECTING


def _peer_of(k):
    x, y, c = lax.axis_index("x"), lax.axis_index("y"), lax.axis_index("c")
    return x ^ ((k >> 2) & 1), y ^ ((k >> 1) & 1), c ^ (k & 1)


def _split_copies(gather, src_ref, land_ref, send_sems, recv_sems):
    x, y, c = lax.axis_index("x"), lax.axis_index("y"), lax.axis_index("c")
    my = 4 * x + 2 * y + c
    r = src_ref.shape[0] if gather else src_ref.shape[0] // N_DEV
    copies = []
    for k in ((1, 2, 4, 6) if gather == "chip" else range(1, N_DEV)):
        px, py, pc = _peer_of(k)
        if gather:
            src, dst = src_ref, land_ref.at[pl.ds(my * r, r), :]
        else:
            src, dst = src_ref.at[pl.ds((4 * px + 2 * py + pc) * r, r), :], land_ref.at[k - 1]
        copies.append(pltpu.make_async_remote_copy(
            src_ref=src, dst_ref=dst, send_sem=send_sems.at[k - 1], recv_sem=recv_sems.at[k - 1],
            device_id=(px, py, pc), device_id_type=MESH))
    return copies


def _split_start(srcs, lands, *, gather, name, after=None):
    na = len(srcs)
    extra = [] if after is None else [after]

    def body(*refs):
        src_refs, land_refs = refs[:na], refs[na:2 * na]
        outs = refs[2 * na + len(extra):]
        for a in range(na):
            for cp in _split_copies(gather, src_refs[a], land_refs[a], outs[4 * a], outs[4 * a + 1]):
                cp.start()
        outs[4 * na][...] = jnp.zeros_like(outs[4 * na])

    out_shape, out_specs, aliases = [], [], {}
    for a in range(na):
        out_shape += [pltpu.SemaphoreType.DMA((N_DEV - 1,)), pltpu.SemaphoreType.DMA((N_DEV - 1,)),
                      pltpu.HBM(srcs[a].shape, srcs[a].dtype), pltpu.HBM(lands[a].shape, lands[a].dtype)]
        out_specs += [SEM, SEM, HBM, HBM]
        aliases[a] = 4 * a + 2
        aliases[na + a] = 4 * a + 3
    out_shape.append(jax.ShapeDtypeStruct((8, LANES), F32))
    out_specs.append(pl.BlockSpec(memory_space=pltpu.VMEM))
    res = pl.pallas_call(
        body,
        name=name,
        out_shape=tuple(out_shape),
        in_specs=[HBM] * (2 * na) + [ANY] * len(extra),
        out_specs=tuple(out_specs),
        input_output_aliases=aliases,
        compiler_params=pltpu.CompilerParams(has_side_effects=EFFECT),
    )(*[pltpu.with_memory_space_constraint(a, pltpu.HBM) for a in list(srcs) + list(lands)], *extra)
    return [tuple(res[4 * a:4 * a + 4]) for a in range(na)], res[4 * na][0, 0]


def _split_wait(handles, after, *, gather, name):
    na = len(handles)

    def body(*refs):
        src_refs, land_refs = refs[:na], refs[na:2 * na]
        sems = refs[2 * na:4 * na]
        for a in range(na):
            for cp in _split_copies(gather, src_refs[a], land_refs[a], sems[2 * a], sems[2 * a + 1]):
                cp.wait_send()
                cp.wait_recv()

    srcs = [h[2] for h in handles]
    lands = [h[3] for h in handles]
    sems = [s for h in handles for s in h[:2]]
    res = pl.pallas_call(
        body,
        name=name,
        out_shape=tuple(pltpu.HBM(a.shape, a.dtype) for a in srcs + lands),
        in_specs=[HBM] * (2 * na) + [SEM] * (2 * na) + [ANY],
        out_specs=tuple([HBM] * (2 * na)),
        input_output_aliases={i: i for i in range(2 * na)},
        compiler_params=pltpu.CompilerParams(has_side_effects=EFFECT),
    )(*srcs, *lands, *sems, after)
    return list(res[:na]), list(res[na:])


def _forward_copies(land_ref, send_sems, recv_sems):
    x, y, c = lax.axis_index("x"), lax.axis_index("y"), lax.axis_index("c")
    r = land_ref.shape[0] // N_DEV
    copies = []
    for j, k in enumerate((2, 4, 6)):
        px, py, pc = _peer_of(k)
        rows = land_ref.at[pl.ds((4 * px + 2 * py + pc) * r, r), :]
        copies.append(pltpu.make_async_remote_copy(
            src_ref=rows, dst_ref=rows, send_sem=send_sems.at[j], recv_sem=recv_sems.at[j],
            device_id=(x, y, 1 - c), device_id_type=MESH))
    return copies


def _forward_start(land, *, name):
    def body(land_ref, send_sems, recv_sems, land_thru, token):
        for cp in _forward_copies(land_ref, send_sems, recv_sems):
            cp.start()
        token[...] = jnp.zeros_like(token)

    res = pl.pallas_call(
        body,
        name=name,
        out_shape=(pltpu.SemaphoreType.DMA((3,)), pltpu.SemaphoreType.DMA((3,)), pltpu.HBM(land.shape, land.dtype),
                   jax.ShapeDtypeStruct((8, LANES), F32)),
        in_specs=[HBM],
        out_specs=(SEM, SEM, HBM, pl.BlockSpec(memory_space=pltpu.VMEM)),
        input_output_aliases={0: 2},
        compiler_params=pltpu.CompilerParams(has_side_effects=EFFECT),
    )(pltpu.with_memory_space_constraint(land, pltpu.HBM))
    return res[:3]


def _forward_wait(handle, *, name):
    send_sems, recv_sems, land = handle

    def body(land_ref, send_ref, recv_ref, land_out):
        for cp in _forward_copies(land_ref, send_ref, recv_ref):
            cp.wait_send()
            cp.wait_recv()

    return pl.pallas_call(
        body,
        name=name,
        out_shape=pltpu.HBM(land.shape, land.dtype),
        in_specs=[HBM, SEM, SEM],
        out_specs=HBM,
        input_output_aliases={0: 0},
        compiler_params=pltpu.CompilerParams(has_side_effects=EFFECT),
    )(land, send_sems, recv_sems)


def _sum8(own, recv, *, name):
    _, r, w = recv.shape
    fits = lambda c: 2 * c * w * (N_DEV * recv.dtype.itemsize + 4) <= 32 * 2**20
    tr = _pick(r, tuple(c for c in (r // 2, r // 4, 256, 128, 64, 32, 16, 8) if c % 16 == 0 and fits(c)))

    def body(own_ref, a_ref, o_ref):
        acc = own_ref[...].astype(F32)
        for i in range(N_DEV - 1):
            acc = acc + a_ref[i].astype(F32)
        o_ref[...] = acc

    return pl.pallas_call(
        body,
        name=name,
        grid=(r // tr,),
        in_specs=[pl.BlockSpec((tr, w), lambda i: (i, 0)), pl.BlockSpec((N_DEV - 1, tr, w), lambda i: (0, i, 0))],
        out_specs=pl.BlockSpec((tr, w), lambda i: (i, 0)),
        out_shape=jax.ShapeDtypeStruct((r, w), F32),
        compiler_params=pltpu.CompilerParams(dimension_semantics=("parallel",), vmem_limit_bytes=_vmem(4 * N_DEV * tr * w * 4)),
    )(own, recv)


def _adamw(w, g, m, v, *, name):
    def fn(rows, _):
        wv, gv, mv, vv = rows
        m1 = ADAM_B1 * mv + (1.0 - ADAM_B1) * gv
        v1 = ADAM_B2 * vv + (1.0 - ADAM_B2) * jnp.square(gv)
        m_hat = m1 / (1.0 - ADAM_B1 ** ADAM_STEP)
        v_hat = v1 / (1.0 - ADAM_B2 ** ADAM_STEP)
        delta = -ADAM_LR * (m_hat / (jnp.sqrt(v_hat) + ADAM_EPS) + ADAM_WD * wv)
        return [delta, m1, v1], []

    c = w.shape[1]
    return _rowmap(fn, [w, g, m, v], [], [(c, F32)] * 3, [], name=name)


_SMALL = ("b_ada", "g_norm1", "g_norm2", "b_gate", "g_qa", "g_ka", "g_qb", "g_kb", "rpb", "loss")


def _pack_small(parts):
    flat = []
    for nme in _SMALL:
        a = parts[nme].reshape(-1).astype(F32)
        flat.append(jnp.pad(a, (0, (-a.shape[0]) % LANES)))
    flat = jnp.concatenate(flat)
    flat = jnp.pad(flat, (0, (-flat.shape[0]) % (LANES * LANES)))
    return flat.reshape(-1, LANES)


def _unpack_small(packed, shapes):
    flat = packed.reshape(-1)
    out, pos = {}, 0
    for nme in _SMALL:
        n = int(np.prod(shapes[nme]))
        out[nme] = flat[pos:pos + n].reshape(shapes[nme])
        pos += n + (-n) % LANES
    return out


def _to_class(a, d):
    t, w = a.shape
    return a if d == 1 else a.reshape(t // d, d, w).transpose(1, 0, 2).reshape(t, w)


def _from_class(a, d):
    t, w = a.shape
    return a if d == 1 else a.reshape(d, t // d, w).transpose(1, 0, 2).reshape(t, w)


def kernel(x, c, w_ada, b_ada, g_norm1, g_norm2, w_in, b_gate, g_qa, g_ka, g_qb, g_kb, rpb, w_proj_a, w_proj_b, w_o, w_ffn_in, w_ffn_out, loss_target, m_w_ada, m_b_ada, m_g_norm1, m_g_norm2, m_w_in, m_b_gate, m_g_qa, m_g_ka, m_g_qb, m_g_kb, m_rpb, m_w_proj_a, m_w_proj_b, m_w_o, m_w_ffn_in, m_w_ffn_out, v_w_ada, v_b_ada, v_g_norm1, v_g_norm2, v_w_in, v_b_gate, v_g_qa, v_g_ka, v_g_qb, v_g_kb, v_rpb, v_w_proj_a, v_w_proj_b, v_w_o, v_w_ffn_in, v_w_ffn_out):
    t, d = x.shape[1], x.shape[2]
    d_ff = w_ffn_out.shape[1] * N_DEV
    me = 4 * lax.axis_index("x") + 2 * lax.axis_index("y") + lax.axis_index("c")
    xt, tgt = x.reshape(t, d), loss_target.reshape(t, d)
    ones = _head_ones()

    shards = [s.astype(WIRE_DTYPE) for s in (w_in[0].T, w_ffn_in[0].T, w_proj_a[0].T, w_proj_b[0].T, w_o[0], w_ffn_out[0])]
    lands = [lax.dynamic_update_slice(lax.empty((N_DEV * s.shape[0], s.shape[1]), s.dtype), s, (me * s.shape[0], 0))
             for s in shards]

    c_all = _all_gather([jnp.pad(c, ((0, 7), (0, 0)))], name="gather_c")[0][::8]
    c_all = jnp.pad(c_all, ((0, LANES - N_DEV), (0, 0)))

    def mod_body(c_ref, w_ref, b_ref, o_ref, act_ref):
        act = _silu(c_ref[...])
        act_ref[...] = act
        hi, mid, lo = _split3(act)
        w = w_ref[...]
        whi, wmid, wlo = _split3(w)
        acc = _dot(hi, whi) + (_dot(hi, wmid) + _dot(mid, whi)) + (_dot(hi, wlo) + _dot(mid, wmid) + _dot(lo, whi))
        o_ref[...] = acc + b_ref[...]

    ncol = w_ada.shape[2]
    b_ada_mine = lax.dynamic_slice(b_ada, (0, me * ncol), (1, ncol))
    mod_part, c_act = pl.pallas_call(
        mod_body,
        name="ada_mod",
        out_shape=[jax.ShapeDtypeStruct((LANES, ncol), F32), jax.ShapeDtypeStruct((LANES, d), F32)],
        compiler_params=pltpu.CompilerParams(vmem_limit_bytes=_vmem(6 * d * ncol * 4)),
    )(c_all, w_ada[0], b_ada_mine)
    mod_all = _all_gather([mod_part[:N_DEV]], name="gather_mod")[0].reshape(N_DEV, N_DEV, ncol)
    mod = lax.dynamic_index_in_dim(mod_all, me, axis=1, keepdims=False).reshape(6, d)
    sh1, sc1, gt1, sh2, sc2, gt2 = [mod[i:i + 1] for i in range(6)]

    def norm_fwd(rows, vecs):
        (xv,), (g, sc, sh) = rows, vecs
        return [xv * _rms(xv) * g * (1.0 + sc) + sh], []

    w_in_handle, w_token = _split_start(shards[:1], lands[:1], gather="chip", after=mod, name="gather_w_in_start")
    (h,) = _rowmap(norm_fwd, [xt], [g_norm1 + w_token, sc1, sh1], [(d, MXU_DTYPE)], [], name="norm1")
    n_a, n_b = 3 * WA, 3 * WB
    (w_in_t,) = _split_wait(w_in_handle, h, gather="chip", name="gather_w_in_wait")[1]
    w_in_t = _forward_wait(_forward_start(w_in_t, name="gather_w_in_forward_start"), name="gather_w_in_forward_wait")
    w_handles, w_token = _split_start(shards[1:], lands[1:], gather=True, after=w_in_t, name="gather_weights_start")
    w_in_a, w_in_b, w_in_g = w_in_t[:n_a], w_in_t[n_a:n_a + n_b], w_in_t[n_a + n_b:]
    gates = _mm(h, w_in_g, tb=True, out_dtype=ACT_DTYPE, name="proj_gates")

    rot_c, rot_lo, rot_hi = _rot_tables(t)
    tile_g = lambda g, heads: jnp.tile(g, (1, heads))

    def qk_fwd(width, rotate):
        def fn(xv, rows, vecs):
            gq, gk, on = vecs
            qkv = []
            for i, g in enumerate((gq, gk)):
                xi = xv[:, i * width:(i + 1) * width]
                r = lax.rsqrt(_headsum(xi * xi, on) * (1.0 / HEAD_DIM) + EPS)
                yi = xi * r * g
                if rotate:
                    yi = _rot(yi, rows[0], rows[1], rows[2])
                qkv.append(yi)
            qkv.append(xv[:, 2 * width:])
            if not rotate:
                return [xv] + qkv, []
            groups = [jnp.concatenate([a[:, g * WB_OUT:(g + 1) * WB_OUT] for a in qkv], axis=1)
                      for g in range(len(DIL_CONFIGS))]
            return [xv] + groups, []
        return fn

    qkv_a, qa, ka, va = _mm_parts_rows(
        [(h, w_in_a, "nt")], qk_fwd(WA, False), [], [tile_g(g_qa, NA_HEADS) + w_token, tile_g(g_ka, NA_HEADS), ones],
        [(3 * WA, ACT_DTYPE)] + [(WA, MXU_DTYPE)] * 3, [], name="proj_a_qknorm")
    qkv_b, *qkv_groups = _mm_parts_rows(
        [(h, w_in_b, "nt")], qk_fwd(WB, True), [rot_c, rot_lo, rot_hi],
        [tile_g(g_qb, DIL_HEADS), tile_g(g_kb, DIL_HEADS), ones],
        [(3 * WB, ACT_DTYPE)] + [(3 * WB_OUT, MXU_DTYPE)] * len(DIL_CONFIGS), [], name="proj_b_qknorm")

    bias_tab = _na_bias_table(rpb[0])
    o_a, lse_a = _attn_fwd(qa, ka, va, kind="na", bias=bias_tab, name="na_fwd")

    grp = []
    for g, (_, dil) in enumerate(DIL_CONFIGS):
        qkv_c = _to_class(qkv_groups[g], dil)
        og, lg = _attn_fwd((qkv_c, 0), (qkv_c, 1), (qkv_c, 2), kind="dil", seg=t // dil, name=f"dil_fwd{g}")
        grp.append(dict(qkv=qkv_c, o=_from_class(og, dil), lse=_from_class(lg, dil), lse_c=lg, dil=dil))

    def merge_fwd(rows, _):
        o0, o1, o2, l0, l1, l2 = rows
        mx = jnp.maximum(jnp.maximum(l0, l1), l2)
        e0, e1, e2 = jnp.exp(l0 - mx), jnp.exp(l1 - mx), jnp.exp(l2 - mx)
        s = e0 + e1 + e2
        return [(e0 / s) * o0 + (e1 / s) * o1 + (e2 / s) * o2], []

    (o_b,) = _rowmap(merge_fwd, [gr["o"] for gr in grp] + [gr["lse"] for gr in grp], [], [(WB_OUT, F32)], [], name="dil_merge")

    w_pa_t, w_pb_t, w_o_f = _split_wait(w_handles[1:4], o_b, gather=True, name="gather_w_out_wait")[1]
    pa = _mm(o_a, w_pa_t, tb=True, out_dtype=ACT_DTYPE, name="proj_out_a")
    pb = _mm(o_b, w_pb_t, tb=True, out_dtype=ACT_DTYPE, name="proj_out_b")

    def gate_fwd(rows, vecs):
        gv, pav, pbv = rows
        sg = jax.nn.sigmoid(gv + vecs[0])
        return [sg[:, :d] * pav + sg[:, d:] * pbv], []

    (merged,) = _rowmap(gate_fwd, [gates, pa, pb], [b_gate], [(d, MXU_DTYPE)], [], name="gate_merge")
    def resid_norm(av, rows, vecs):
        (xv,), (gt, g, sc, sh) = rows, vecs
        x1v = xv + gt * av
        return [av, x1v, x1v * _rms(x1v) * g * (1.0 + sc) + sh], []

    att, x1, h2 = _mm_parts_rows([(merged, w_o_f)], resid_norm, [xt], [gt1, g_norm2, sc2, sh2],
                           [(d, F32), (d, F32), (d, MXU_DTYPE)], [], name="proj_o_resid_norm2")

    w_ffn_in_t, w_ffn_out_f = _split_wait([w_handles[0], w_handles[4]], h2, gather=True, name="gather_w_ffn_wait")[1]
    w_ffn_a, w_ffn_up = w_ffn_in_t[:d_ff], w_ffn_in_t[d_ff:]

    def swiglu_fwd(prods, _):
        a, up = prods
        return [a, up, _silu(a) * up]

    ua, uu, f = _mm_ew(h2, [w_ffn_a, w_ffn_up], swiglu_fwd, [], [ACT_DTYPE, ACT_DTYPE, MXU_DTYPE], name="ffn_in_swiglu")

    def loss_fn(yv, rows, vecs):
        (x1v, tv), gt = rows, vecs[0]
        err = x1v + gt * yv - tv
        dout = err * (1.0 / d)
        return [dout, dout * gt], [_colsum(err * err), _colsum(dout * yv)]

    dout, dy2, err2, dgt2 = _mm_parts_rows([(f, w_ffn_out_f)], loss_fn, [x1, tgt], [gt2], [(d, F32), (d, MXU_DTYPE)],
                                           [d, d], name="ffn_out_loss")

    dw_ffn_out = _mm(f, dy2, ta=True, out_dtype=WIRE_DTYPE, name="wgrad_ffn_out")
    def swiglu_bwd(prods, rows):
        (dfv,), (a, up) = prods, rows
        sg = jax.nn.sigmoid(a)
        return [dfv * up * (sg * (1.0 + a * (1.0 - sg))), dfv * (a * sg)]

    da, dup = _mm_ew(dy2, [w_ffn_out_f], swiglu_bwd, [ua, uu], [MXU_DTYPE, MXU_DTYPE], name="dgrad_ffn_out_swiglu_bwd")
    dw_ffn_in_t = _mm(da, h2, ta=True, into=(lax.empty((2 * d_ff, d), WIRE_DTYPE), 0), name="wgrad_ffn_in_a")
    dw_ffn_in_t = _mm(dup, h2, ta=True, into=(dw_ffn_in_t, d_ff), name="wgrad_ffn_in_up")
    land7 = lambda a: lax.empty((N_DEV - 1, a.shape[0] // N_DEV, a.shape[1]), a.dtype)
    own_block = lambda a: lax.dynamic_slice(a, (me * (a.shape[0] // N_DEV), 0), (a.shape[0] // N_DEV, a.shape[1]))
    g_ffn = [dw_ffn_in_t, dw_ffn_out]
    h_ffn, tok_ffn = _split_start(g_ffn, [land7(a) for a in g_ffn], gather=False, name="exchange_ffn_start")
    def norm_bwd(dh, xv, g, sc):
        r = _rms(xv)
        xh = xv * r
        dxh = dh * g * (1.0 + sc)
        dxv = r * (dxh - xh * jnp.mean(dxh * xh, axis=-1, keepdims=True))
        return dxv, [_colsum(dh), _colsum(dh * xh * g), _colsum(dh * xh * (1.0 + sc))]

    def norm2_bwd(dhv, rows, vecs):
        (x1v, dov, av), (g, sc, gt) = rows, vecs
        dxv, sums = norm_bwd(dhv, x1v, g, sc)
        dx1v = dov + dxv
        return [dx1v, dx1v * gt], sums + [_colsum(dx1v * av)]

    dx1, datt, dsh2, dsc2, dg2, dgt1 = _mm_parts_rows(
        [(da, w_ffn_a), (dup, w_ffn_up)], norm2_bwd, [x1, dout, att], [g_norm2 + tok_ffn, sc2, gt1],
        [(d, F32), (d, MXU_DTYPE)], [d] * 4, name="dgrad_ffn_in_norm2_bwd")
    dw_o = _mm(merged, datt, ta=True, out_dtype=WIRE_DTYPE, name="wgrad_o")
    def gate_bwd(dm, rows, vecs):
        gv, pav, pbv = rows
        sg = jax.nn.sigmoid(gv + vecs[0])
        ga, gb = sg[:, :d], sg[:, d:]
        dgp = jnp.concatenate([dm * pav * ga * (1.0 - ga), dm * pbv * gb * (1.0 - gb)], axis=1)
        return [dm * ga, dm * gb, dgp], [_colsum(dgp)]

    dpa, dpb, dgates, db_gate = _mm_parts_rows(
        [(datt, w_o_f.T)], gate_bwd, [gates, pa, pb], [b_gate],
        [(d, MXU_DTYPE), (d, MXU_DTYPE), (2 * d, MXU_DTYPE)], [2 * d], name="dgrad_o_gate_bwd")
    dw_pa_t = _mm(dpa, o_a, ta=True, out_dtype=WIRE_DTYPE, name="wgrad_proj_a")
    dw_pb_t = _mm(dpb, o_b, ta=True, out_dtype=WIRE_DTYPE, name="wgrad_proj_b")
    g_out = [dw_pa_t, dw_pb_t, dw_o]
    h_out, tok_out = _split_start(g_out, [land7(a) for a in g_out], gather=False, name="exchange_out_start")
    def delta_a(doa, rows, vecs):
        return [doa, _headsum(doa * rows[0], vecs[0])], []

    do_a, dterm_a = _mm_parts_rows([(dpa, w_pa_t)], delta_a, [o_a], [ones + tok_out.astype(ones.dtype)],
                                   [(WA, F32), (WA, F32)], [], name="dgrad_proj_a_delta")
    dqa, dka, dva, dbias = _attn_bwd(qa, ka, va, do_a, dterm_a, lse_a, kind="na", bias=bias_tab, name="na_bwd")
    g_rpb = _na_bias_grad(dbias)

    def merge_bwd(dob, rows, vecs):
        o0, o1, o2, l0, l1, l2 = rows
        on = vecs[0]
        mx = jnp.maximum(jnp.maximum(l0, l1), l2)
        e0, e1, e2 = jnp.exp(l0 - mx), jnp.exp(l1 - mx), jnp.exp(l2 - mx)
        s = e0 + e1 + e2
        ws = [e0 / s, e1 / s, e2 / s]
        dws = [_headsum(dob * o, on) for o in (o0, o1, o2)]
        mean = ws[0] * dws[0] + ws[1] * dws[1] + ws[2] * dws[2]
        return [jnp.concatenate([w * dob, w * mean], axis=1) for w in ws], []

    mb = _mm_parts_rows([(dpb, w_pb_t)], merge_bwd, [gr["o"] for gr in grp] + [gr["lse"] for gr in grp], [ones],
                        [(2 * WB_OUT, F32)] * len(grp), [], name="dgrad_proj_b_merge_bwd")
    dqb, dkb, dvb = [], [], []
    for g, gr in enumerate(grp):
        dil, qkv_c = gr["dil"], gr["qkv"]
        dd_c = _to_class(mb[g], dil)
        dq, dk, dv = _attn_bwd((qkv_c, 0), (qkv_c, 1), (qkv_c, 2), (dd_c, 0), (dd_c, 1), (gr["lse_c"], 0),
                               kind="dil", seg=t // dil, name=f"dil_bwd{g}")
        dqb.append(_from_class(dq, dil))
        dkb.append(dk[0] if dil == 1 else _from_class(dk[0][0][0], dil))
        dvb.append(dv[0] if dil == 1 else _from_class(dv[0][0][0], dil))

    def qk_bwd(width, rotate, nparts):
        def fn(rows, vecs):
            gq, gk, on = vecs
            xv = rows[0]
            pos = 1
            if rotate:
                rc, rlo, rhi = rows[1:4]
                pos = 4
            cat = lambda parts: parts[0] if len(parts) == 1 else jnp.concatenate(parts, axis=1)
            ends = np.cumsum((pos,) + nparts)
            dq, dk, dv = [cat(rows[ends[i]:ends[i + 1]]) for i in range(3)]
            outs, sums = [], []
            for i, (dy, g) in enumerate(((dq, gq), (dk, gk))):
                if rotate:
                    dy = _rot(dy, rc, -rlo, -rhi)
                xi = xv[:, i * width:(i + 1) * width]
                r = lax.rsqrt(_headsum(xi * xi, on) * (1.0 / HEAD_DIM) + EPS)
                xh = xi * r
                dxh = dy * g
                outs.append(r * (dxh - xh * (_headsum(dxh * xh, on) * (1.0 / HEAD_DIM))))
                sums.append(_colsum(dy * xh))
            return [jnp.concatenate(outs + [dv], axis=1)], sums
        return fn

    dqkv_a, dg_qa, dg_ka = _rowmap(qk_bwd(WA, False, (1, len(dka), len(dva))), [qkv_a, dqa] + dka + dva,
                                   [tile_g(g_qa, NA_HEADS), tile_g(g_ka, NA_HEADS), ones],
                                   [(3 * WA, MXU_DTYPE)], [WA, WA], name="qknorm_a_bwd")
    dqkv_b, dg_qb, dg_kb = _rowmap(qk_bwd(WB, True, (3, 3, 3)), [qkv_b, rot_c, rot_lo, rot_hi] + dqb + dkb + dvb,
                                   [tile_g(g_qb, DIL_HEADS), tile_g(g_kb, DIL_HEADS), ones],
                                   [(3 * WB, MXU_DTYPE)], [WB, WB], name="qknorm_b_bwd")

    dw_in_t = jnp.concatenate([
        _mm(dqkv_a, h, ta=True, out_dtype=WIRE_DTYPE, name="wgrad_in_a"),
        _mm(dqkv_b, h, ta=True, out_dtype=WIRE_DTYPE, name="wgrad_in_b"),
        _mm(dgates, h, ta=True, out_dtype=WIRE_DTYPE, name="wgrad_in_gates")], axis=0)
    h_in, tok_in = _split_start([dw_in_t], [land7(dw_in_t)], gather=False, name="exchange_in_start")
    def norm1_bwd(dhv, rows, vecs):
        xv, dx1v = rows
        dxv, sums = norm_bwd(dhv, xv, vecs[0], vecs[1])
        return [dx1v + dxv], sums

    grad_x, dsh1, dsc1, dg1 = _mm_parts_rows(
        [(dqkv_a, w_in_a), (dqkv_b, w_in_b), (dgates, w_in_g)], norm1_bwd, [xt, dx1], [g_norm1 + tok_in, sc1],
        [(d, F32)], [d] * 3, name="dgrad_in_norm1_bwd")

    heads_sum = lambda a, heads: a.reshape(heads, HEAD_DIM).sum(axis=0)
    dmod = jnp.concatenate([dsh1, dsc1, dgt1, dsh2, dsc2, dgt2], axis=1)
    local_small = _pack_small(dict(
        b_ada=dmod, g_norm1=dg1, g_norm2=dg2, b_gate=db_gate, g_qa=heads_sum(dg_qa, NA_HEADS),
        g_ka=heads_sum(dg_ka, NA_HEADS), g_qb=heads_sum(dg_qb, DIL_HEADS), g_kb=heads_sum(dg_kb, DIL_HEADS),
        rpb=g_rpb, loss=(0.5 / d) * jnp.sum(err2)))
    srows = local_small.shape[0]
    small_all = _all_gather([local_small], name="gather_small")[0].reshape(N_DEV, srows, LANES)
    small_sum = _sum8(small_all[0], small_all[1:], name="sum_small")
    small_shapes = dict(b_ada=b_ada.shape, g_norm1=g_norm1.shape, g_norm2=g_norm2.shape, b_gate=b_gate.shape,
                        g_qa=g_qa.shape, g_ka=g_ka.shape, g_qb=g_qb.shape, g_kb=g_kb.shape, rpb=rpb.shape, loss=())
    small_w = dict(b_ada=b_ada, g_norm1=g_norm1, g_norm2=g_norm2, b_gate=b_gate, g_qa=g_qa, g_ka=g_ka, g_qb=g_qb,
                   g_kb=g_kb, rpb=rpb, loss=jnp.zeros((), F32))
    small_m = dict(b_ada=m_b_ada, g_norm1=m_g_norm1, g_norm2=m_g_norm2, b_gate=m_b_gate, g_qa=m_g_qa, g_ka=m_g_ka,
                   g_qb=m_g_qb, g_kb=m_g_kb, rpb=m_rpb, loss=jnp.zeros((), F32))
    small_v = dict(b_ada=v_b_ada, g_norm1=v_g_norm1, g_norm2=v_g_norm2, b_gate=v_b_gate, g_qa=v_g_qa, g_ka=v_g_ka,
                   g_qb=v_g_qb, g_kb=v_g_kb, rpb=v_rpb, loss=jnp.zeros((), F32))
    s_delta, s_m, s_v = _adamw(_pack_small(small_w), small_sum, _pack_small(small_m), _pack_small(small_v), name="adamw_small")
    gs = _unpack_small(small_sum, small_shapes)
    ds_, ms_, vs_ = [_unpack_small(a, small_shapes) for a in (s_delta, s_m, s_v)]

    dmod_all = small_all[:, :6 * d // LANES].reshape(N_DEV, 6 * d)
    dmod_mine = jnp.pad(lax.dynamic_slice(dmod_all, (0, me * ncol), (N_DEV, ncol)), ((0, LANES - N_DEV), (0, 0)))

    def wada_body(c_ref, dm_ref, o_ref):
        chi, cmid, clo = _split3(c_ref[...])
        dhi, dmid, dlo = _split3(dm_ref[...])
        o_ref[...] = (_dot_tn(chi, dhi) + (_dot_tn(chi, dmid) + _dot_tn(cmid, dhi))
                      + (_dot_tn(chi, dlo) + _dot_tn(cmid, dmid) + _dot_tn(clo, dhi)))

    g_w_ada = pl.pallas_call(
        wada_body,
        name="wgrad_ada",
        out_shape=jax.ShapeDtypeStruct((d, ncol), F32),
        compiler_params=pltpu.CompilerParams(vmem_limit_bytes=_vmem(4 * d * ncol * 4)),
    )(c_act, dmod_mine)

    sent, recv = _split_wait(h_in + h_ffn + h_out, small_sum, gather=False, name="exchange_wait")
    names = ("w_in", "w_ffn_in", "w_ffn_out", "w_proj_a", "w_proj_b", "w_o")
    transposed = (True, True, False, True, True, False)
    big_g = {}
    for nme, own, r, tr in zip(names, sent, recv, transposed):
        s = _sum8(own_block(own), r, name=f"sum_{nme}")
        big_g[nme] = s.T if tr else s
    big_g["w_ada"] = g_w_ada
    big_w = dict(w_ada=w_ada, w_in=w_in, w_proj_a=w_proj_a, w_proj_b=w_proj_b, w_o=w_o, w_ffn_in=w_ffn_in, w_ffn_out=w_ffn_out)
    big_m = dict(w_ada=m_w_ada, w_in=m_w_in, w_proj_a=m_w_proj_a, w_proj_b=m_w_proj_b, w_o=m_w_o, w_ffn_in=m_w_ffn_in, w_ffn_out=m_w_ffn_out)
    big_v = dict(w_ada=v_w_ada, w_in=v_w_in, w_proj_a=v_w_proj_a, w_proj_b=v_w_proj_b, w_o=v_w_o, w_ffn_in=v_w_ffn_in, w_ffn_out=v_w_ffn_out)
    grads, deltas, new_m, new_v = {}, {}, {}, {}
    for nme in big_w:
        dl, m1, v1 = _adamw(big_w[nme][0], big_g[nme], big_m[nme][0], big_v[nme][0], name=f"adamw_{nme}")
        grads[nme], deltas[nme], new_m[nme], new_v[nme] = big_g[nme][None], dl[None], m1[None], v1[None]
    for nme in _SMALL[:-1]:
        grads[nme], deltas[nme], new_m[nme], new_v[nme] = gs[nme], ds_[nme], ms_[nme], vs_[nme]

    order = ("w_ada", "b_ada", "g_norm1", "g_norm2", "w_in", "b_gate", "g_qa", "g_ka", "g_qb", "g_kb", "rpb",
             "w_proj_a", "w_proj_b", "w_o", "w_ffn_in", "w_ffn_out")
    return (gs["loss"], grad_x[None], *[grads[n] for n in order], *[deltas[n] for n in order],
            *[new_m[n] for n in order], *[new_v[n] for n in order])
```

```python
import functools

import numpy as np
import jax
import jax.numpy as jnp
from jax import lax
from jax.experimental import pallas as pl
from jax.experimental.pallas import tpu as pltpu

F32 = jnp.float32
MXU_DTYPE = jnp.bfloat16
WIRE_DTYPE = jnp.bfloat16
ACT_DTYPE = jnp.bfloat16

HEAD_DIM = 64
GRID_W = 64
NA_HEADS = 8
NA_KH = 8
NA_KW = 16
DIL_CONFIGS = ((128, 1), (512, 4), (2048, 16))
DIL_HEADS_PER_GROUP = 4
DIL_HEADS = DIL_HEADS_PER_GROUP * len(DIL_CONFIGS)
DIL_HALF = 64
ROT_DIM = HEAD_DIM // 4
ROPE_THETA = 500000.0
EPS = 1e-6
NEG_INF = -1e30
WA = NA_HEADS * HEAD_DIM
WB = DIL_HEADS * HEAD_DIM
WB_OUT = DIL_HEADS_PER_GROUP * HEAD_DIM
ADAM_LR = 0.001
ADAM_B1 = 0.9
ADAM_B2 = 0.999
ADAM_EPS = 1e-08
ADAM_WD = 0.01
ADAM_STEP = 10

N_DEV = 8
LANES = 128
VMEM_CAP = 60 * 2**20
VMEM_FLOOR = 56 * 2**20
MESH = pl.DeviceIdType.MESH
ANY = pl.BlockSpec(memory_space=pl.ANY)


def _vmem(nbytes):
    return int(min(VMEM_CAP, max(VMEM_FLOOR, nbytes * 5 // 4 + 4 * 2**20)))


def _pick(dim, cands):
    for c in cands:
        if c <= dim and dim % c == 0:
            return c
    return dim


def _nbytes(shape, dtype):
    return int(np.prod(shape)) * jnp.dtype(dtype).itemsize


def _dot(a, b, dims=((1,), (0,))):
    return lax.dot_general(a.astype(MXU_DTYPE), b.astype(MXU_DTYPE), (dims, ((), ())), preferred_element_type=F32)


def _dot_nt(a, b):
    return _dot(a, b, ((1,), (1,)))


def _dot_tn(a, b):
    return _dot(a, b, ((0,), (0,)))


def _split3(a):
    hi = a.astype(jnp.bfloat16)
    r1 = a - hi.astype(F32)
    mid = r1.astype(jnp.bfloat16)
    lo = (r1 - mid.astype(F32)).astype(jnp.bfloat16)
    return hi, mid, lo


def _silu(x):
    return x * jax.nn.sigmoid(x)


def _divisors(dim, unit):
    return [c for c in range(unit, dim + 1, unit) if dim % c == 0] or [dim]


def _mm_tiles(m, n, kdim, a_item, b_item, o_item, row_off=0):
    step_us, hbm_bytes_per_us, flops_per_us, budget = 0.35, 3.0e6, 8.0e8, 40 * 2**20
    best = None
    for tm in _divisors(m, LANES):
        for tn in _divisors(n, LANES):
            for tk in _divisors(kdim, LANES):
                if row_off % tm:
                    continue
                gm, gn, gk = m // tm, n // tn, kdim // tk
                vmem = 2 * (tm * tk * a_item + tk * tn * b_item + tm * tn * o_item) + 2 * (tm * tk + tk * tn)
                vmem += tm * tn * 4 * ((1 if gk > 1 else 0) + 1)
                if vmem > budget:
                    continue
                a_reads = m * kdim * a_item * (gn if gk > 1 else 1)
                traffic = a_reads + kdim * n * b_item * gm + m * n * o_item
                cost = gm * gn * gk * step_us + max(traffic / hbm_bytes_per_us, 2.0 * m * n * kdim / flops_per_us)
                if best is None or cost < best[0]:
                    best = (cost, tm, tn, tk)
    return best[1:]


def _mm(a, b, *, name, ta=False, tb=False, out_dtype=F32, into=None):
    if ta:
        kdim, m = a.shape
    else:
        m, kdim = a.shape
    n = b.shape[0] if tb else b.shape[1]
    assert b.shape[1 if tb else 0] == kdim
    buf, row_off = into if into is not None else (None, 0)
    if buf is not None:
        out_dtype = buf.dtype
    tm, tn, tk = _mm_tiles(m, n, kdim, a.dtype.itemsize, b.dtype.itemsize, jnp.dtype(out_dtype).itemsize, row_off)
    gm, gn, gk = m // tm, n // tn, kdim // tk
    ob = row_off // tm

    a_spec = pl.BlockSpec((tk, tm), lambda i, j, k: (k, i)) if ta else pl.BlockSpec((tm, tk), lambda i, j, k: (i, k))
    b_spec = pl.BlockSpec((tn, tk), lambda i, j, k: (j, k)) if tb else pl.BlockSpec((tk, tn), lambda i, j, k: (k, j))
    o_spec = pl.BlockSpec((tm, tn), lambda i, j, k: (i + ob, j))
    a_dims = (0,) if ta else (1,)
    b_dims = (1,) if tb else (0,)

    def body(a_ref, b_ref, *rest):
        o_ref, scratch = rest[-1 - (gk > 1)], rest[-(gk > 1):] if gk > 1 else ()
        if gk == 1:
            o_ref[...] = _dot(a_ref[...], b_ref[...], (a_dims, b_dims)).astype(o_ref.dtype)
            return
        (acc_ref,) = scratch
        k = pl.program_id(2)

        @pl.when(k == 0)
        def _():
            acc_ref[...] = jnp.zeros_like(acc_ref)

        acc_ref[...] += _dot(a_ref[...], b_ref[...], (a_dims, b_dims))

        @pl.when(k == gk - 1)
        def _():
            o_ref[...] = acc_ref[...].astype(o_ref.dtype)

    est = 2 * (tm * tk * a.dtype.itemsize + tk * tn * b.dtype.itemsize + tm * tn * jnp.dtype(out_dtype).itemsize)
    est += tm * tn * 4 + 2 * (tm * tk + tk * tn) * 2
    return pl.pallas_call(
        body,
        name=name,
        grid=(gm, gn, gk),
        in_specs=[a_spec, b_spec] + ([ANY] if buf is not None else []),
        out_specs=o_spec,
        out_shape=jax.ShapeDtypeStruct((m, n) if buf is None else buf.shape, out_dtype),
        input_output_aliases={2: 0} if buf is not None else {},
        scratch_shapes=[pltpu.VMEM((tm, tn), F32)] if gk > 1 else [],
        compiler_params=pltpu.CompilerParams(
            dimension_semantics=("parallel", "parallel", "arbitrary"), vmem_limit_bytes=_vmem(est)
        ),
    )(*((a, b) if buf is None else (a, b, buf)))


def _resident(shape):
    return pl.BlockSpec(shape, lambda i: (0,) * len(shape), pipeline_mode=pl.Buffered(1))


def _row_tile(m, fixed_bytes, bytes_per_row, budget=50 * 2**20):
    fits = [tm for tm in _divisors(m, LANES) if fixed_bytes + tm * bytes_per_row <= budget]
    return max(fits) if fits else _divisors(m, LANES)[0]


def _mm_parts_rows(parts, fn, rows, vecs, outs, reds, *, name, separate=False):
    parts = [(p[0], p[1], len(p) > 2) for p in parts]
    rows = [r if isinstance(r, tuple) else (r, r.shape[1], 0) for r in rows]
    m, n = parts[0][0].shape[0], parts[0][1].shape[0 if parts[0][2] else 1]
    npart, nr, nv, no = len(parts), len(rows), len(vecs), len(outs)
    row_bytes = sum(w * r.dtype.itemsize for r, w, _ in rows) + sum(w * jnp.dtype(dt).itemsize for (w, dt) in outs)
    a_row_bytes = sum(a.shape[1] * a.dtype.itemsize for a, _, _ in parts)
    fixed = sum(_nbytes(b.shape, b.dtype) for _, b, _ in parts)
    per_row = 2 * (a_row_bytes + row_bytes) + n * 4 * 5
    tm = _row_tile(m, fixed, per_row)
    sub = min(tm, 2 * LANES)

    def body(*refs):
        ab = refs[:2 * npart]
        row_refs, vec_refs = refs[2 * npart:2 * npart + nr], refs[2 * npart + nr:2 * npart + nr + nv]
        out_refs = refs[2 * npart + nr + nv:2 * npart + nr + nv + no]
        red_refs = refs[2 * npart + nr + nv + no:]
        if red_refs:
            @pl.when(pl.program_id(0) == 0)
            def _():
                for ref in red_refs:
                    ref[...] = jnp.zeros_like(ref)

        vecs_v = [v[...] for v in vec_refs]
        for s0 in range(0, tm, sub):
            sl = slice(s0, s0 + sub)
            prods = [(_dot_nt if nt else _dot)(ab[2 * p][sl, :], ab[2 * p + 1][...]) for p, (_, _, nt) in enumerate(parts)]
            r = prods if separate else functools.reduce(lambda u, v: u + v, prods)
            o, rd = fn(r, [x[sl, :].astype(F32) for x in row_refs], vecs_v)
            for ref, val in zip(out_refs, o):
                ref[sl, :] = val.astype(ref.dtype)
            for ref, val in zip(red_refs, rd):
                ref[...] += val

    in_specs, operands = [], []
    for a, b, _ in parts:
        in_specs += [pl.BlockSpec((tm, a.shape[1]), lambda i: (i, 0)), _resident(b.shape)]
        operands += [a, b]
    in_specs += [pl.BlockSpec((tm, w), functools.partial(lambda cb, i: (i, cb), cb)) for _, w, cb in rows]
    in_specs += [pl.BlockSpec(v.shape, functools.partial(lambda nd, i: (0,) * nd, v.ndim)) for v in vecs]
    out_specs = [pl.BlockSpec((tm, w), lambda i: (i, 0)) for (w, _) in outs]
    out_specs += [pl.BlockSpec((1, w), lambda i: (0, 0)) for w in reds]
    out_shape = [jax.ShapeDtypeStruct((m, w), dt) for (w, dt) in outs] + [jax.ShapeDtypeStruct((1, w), F32) for w in reds]
    return pl.pallas_call(
        body,
        name=name,
        grid=(m // tm,),
        in_specs=in_specs,
        out_specs=out_specs,
        out_shape=out_shape,
        compiler_params=pltpu.CompilerParams(dimension_semantics=("arbitrary",), vmem_limit_bytes=_vmem(fixed + tm * per_row)),
    )(*operands, *[r for r, _, _ in rows], *vecs)


def _mm_ew(a, bs, fn, rows, outs, *, name):
    m, kdim = a.shape
    n = bs[0].shape[0]
    nb, nr, no = len(bs), len(rows), len(outs)
    cw = _pick(n, (2 * LANES, LANES))
    fixed = nb * n * kdim * bs[0].dtype.itemsize
    per_row = 2 * (kdim * a.dtype.itemsize + n * (sum(r.dtype.itemsize for r in rows) + sum(jnp.dtype(dt).itemsize for dt in outs)))
    per_row += cw * 4 * 4 * (nb + 4)
    tm = _row_tile(m, fixed, per_row)

    def body(*refs):
        a_ref, b_refs = refs[0], refs[1:1 + nb]
        row_refs, out_refs = refs[1 + nb:1 + nb + nr], refs[1 + nb + nr:]
        av = a_ref[...]
        for c0 in range(0, n, cw):
            cols = slice(c0, c0 + cw)
            o = fn([_dot_nt(av, b[cols, :]) for b in b_refs], [x[:, cols].astype(F32) for x in row_refs])
            for ref, val in zip(out_refs, o):
                ref[:, cols] = val.astype(ref.dtype)

    tile = pl.BlockSpec((tm, n), lambda i: (i, 0))
    return pl.pallas_call(
        body,
        name=name,
        grid=(m // tm,),
        in_specs=[pl.BlockSpec((tm, kdim), lambda i: (i, 0))] + [_resident((n, kdim))] * nb + [tile] * nr,
        out_specs=[tile] * no,
        out_shape=[jax.ShapeDtypeStruct((m, n), dt) for dt in outs],
        compiler_params=pltpu.CompilerParams(dimension_semantics=("parallel",), vmem_limit_bytes=_vmem(fixed + tm * per_row)),
    )(a, *bs, *rows)


def _rowmap(fn, rows, vecs, outs, reds, *, name, tm=None):
    norm = []
    for r in rows:
        if not isinstance(r, tuple):
            norm.append((r, r.shape[1], 0, None))
        elif len(r) == 2:
            norm.append((r[0], r[0].shape[2], 0, r[1]))
        else:
            norm.append((r[0], r[1], r[2], None))
    rows = norm
    t = rows[0][0].shape[-2]
    if tm is None:
        per_row = 2 * sum(w * a.dtype.itemsize for (a, w, _, _) in rows) + 2 * sum(w * jnp.dtype(d).itemsize for (w, d) in outs)
        per_row += 3 * 4 * max([w for (_, w, _, _) in rows] + [w for (w, _) in outs])
        tm = max(8, min(1024, (40 * 2**20) // per_row))
    tm = _pick(t, tuple(c for c in (1024, 512, 256, 128, 64, 32, 16, 8) if c <= tm))
    nr, nv, no = len(rows), len(vecs), len(outs)

    def body(*refs):
        row_refs, vec_refs = refs[:nr], refs[nr:nr + nv]
        out_refs, red_refs = refs[nr + nv:nr + nv + no], refs[nr + nv + no:]
        o, rd = fn([r[...].astype(F32) for r in row_refs], [v[...] for v in vec_refs])
        for ref, val in zip(out_refs, o):
            ref[...] = val.astype(ref.dtype)
        if red_refs:
            @pl.when(pl.program_id(0) == 0)
            def _():
                for ref in red_refs:
                    ref[...] = jnp.zeros_like(ref)

            for ref, val in zip(red_refs, rd):
                ref[...] += val

    in_specs = [pl.BlockSpec((tm, w), functools.partial(lambda cb, i: (i, cb), cb)) if lead is None
                else pl.BlockSpec((None, tm, w), functools.partial(lambda ld, i: (ld, i, 0), lead)) for (_, w, cb, lead) in rows]
    in_specs += [pl.BlockSpec(v.shape, functools.partial(lambda nd, i: (0,) * nd, v.ndim)) for v in vecs]
    out_specs = [pl.BlockSpec((tm, w), lambda i: (i, 0)) for (w, _) in outs]
    out_specs += [pl.BlockSpec((1, w), lambda i: (0, 0)) for w in reds]
    out_shape = [jax.ShapeDtypeStruct((t, w), d) for (w, d) in outs]
    out_shape += [jax.ShapeDtypeStruct((1, w), F32) for w in reds]
    est = 2 * sum(tm * w * a.dtype.itemsize for (a, w, _, _) in rows) + 2 * sum(_nbytes(v.shape, v.dtype) for v in vecs)
    est += 2 * sum(tm * w * jnp.dtype(d).itemsize for (w, d) in outs)
    est += 6 * tm * max([w for (_, w, _, _) in rows] + [w for (w, _) in outs]) * 4
    return pl.pallas_call(
        body,
        name=name,
        grid=(t // tm,),
        in_specs=in_specs,
        out_specs=out_specs,
        out_shape=out_shape,
        compiler_params=pltpu.CompilerParams(dimension_semantics=("arbitrary",), vmem_limit_bytes=_vmem(est)),
    )(*[r[0] for r in rows], *vecs)


def _colsum(v):
    return jnp.sum(v, axis=0, keepdims=True)


def _head_ones():
    i = np.arange(LANES)
    return jnp.asarray((i[:, None] // HEAD_DIM) == (i[None, :] // HEAD_DIM), MXU_DTYPE)


def _headsum(y, ones):
    parts = []
    for j in range(y.shape[1] // LANES):
        c = y[:, j * LANES:(j + 1) * LANES]
        hi = c.astype(MXU_DTYPE)
        lo = c - hi.astype(F32)
        parts.append(_dot(hi, ones) + _dot(lo, ones))
    return parts[0] if len(parts) == 1 else jnp.concatenate(parts, axis=1)


def _rot(y, c, s_lo, s_hi):
    parts = []
    for j in range(y.shape[1] // LANES):
        yc = y[:, j * LANES:(j + 1) * LANES]
        parts.append(yc * c + pltpu.roll(yc, LANES - ROT_DIM // 2, 1) * s_lo + pltpu.roll(yc, ROT_DIM // 2, 1) * s_hi)
    return parts[0] if len(parts) == 1 else jnp.concatenate(parts, axis=1)


def _rot_tables(t):
    half = ROT_DIM // 2
    inv_freq = ROPE_THETA ** (-(jnp.arange(half, dtype=F32) * 2.0) / ROT_DIM)
    ang = jnp.arange(t).astype(F32)[:, None] * inv_freq[None, :]
    cos, sin = jnp.cos(ang), jnp.sin(ang)
    z = lambda w: jnp.zeros((t, w), F32)
    c = jnp.concatenate([cos, cos, jnp.ones((t, HEAD_DIM - ROT_DIM), F32)], axis=1)
    s_lo = jnp.concatenate([-sin, z(HEAD_DIM - half)], axis=1)
    s_hi = jnp.concatenate([z(half), sin, z(HEAD_DIM - ROT_DIM)], axis=1)
    return [jnp.tile(a, (1, LANES // HEAD_DIM)) for a in (c, s_lo, s_hi)]


def _rms(x):
    return lax.rsqrt(jnp.mean(x * x, axis=-1, keepdims=True) + EPS)


def _window(kind, n, bq, t, seg):
    if kind == "na":
        rows = t // GRID_W
        rs = jnp.clip(n - NA_KH // 2, 0, rows - NA_KH)
        return rs
    nk = bq + 2 * DIL_HALF
    return jnp.clip(n * bq - DIL_HALF, 0, t - nk)


def _dil_mask(n, bq, nk, ws, seg):
    qi = n * bq + lax.broadcasted_iota(jnp.int32, (bq, nk), 0)
    ki = ws + lax.broadcasted_iota(jnp.int32, (bq, nk), 1)
    shift = int(np.log2(seg))
    return (jnp.abs(ki - qi) <= DIL_HALF) & ((ki >> shift) == (qi >> shift))


HS = 4
QW = HS * HEAD_DIM


def _head_of_lane(width=QW):
    return lax.broadcasted_iota(jnp.int32, (1, width), 1) // HEAD_DIM


def _stack_heads(a):
    head = _head_of_lane()
    return jnp.concatenate([jnp.where(head == e, a, jnp.zeros_like(a)) for e in range(HS)], axis=0)


def _unstack_heads(a, bq):
    head = _head_of_lane()
    out = jnp.zeros((bq, QW), a.dtype)
    for e in range(HS):
        out = jnp.where(head == e, a[e * bq:(e + 1) * bq], out)
    return out


def _stack_cols(blk, bq):
    head = _head_of_lane()
    return jnp.concatenate(
        [jnp.max(jnp.where(head == e, blk, -jnp.inf), axis=1, keepdims=True) for e in range(HS)], axis=0)


def _attn_geometry(kind):
    if kind == "na":
        return GRID_W, NA_KH * GRID_W, 16
    bq = 128
    return bq, bq + 2 * DIL_HALF, 8


def _attn_scores(kind, n, bq, nk, t, seg, qs, k_ref, b_ref):
    scale = HEAD_DIM ** -0.5
    if kind == "na":
        rs = _window(kind, n, bq, t, seg)
        ws = pl.multiple_of(rs * GRID_W, GRID_W)
        ro0 = rs - n + (NA_KH - 1)
        s = _dot_nt(qs, k_ref[pl.ds(ws, nk), :]) * scale
        s = s + jnp.concatenate(
            [jnp.concatenate([b_ref[e, ro0 + 2 * i] for i in range(NA_KH // 2)], axis=1) for e in range(HS)], axis=0)
        return s, ws, ro0
    ws = pl.multiple_of(_window(kind, n, bq, t, seg), DIL_HALF)
    mask = _dil_mask(n, bq, nk, ws, seg)
    s = _dot_nt(qs, k_ref[pl.ds(ws, nk), :]) * scale
    s = jnp.where(jnp.concatenate([mask] * HS, axis=0), s, NEG_INF)
    return s, ws, None


def _col_operands(*ops):
    pairs = [op if isinstance(op, tuple) else (op, 0) for op in ops]
    width = QW if isinstance(ops[0], tuple) else ops[0].shape[1]
    return (*pairs, width)


def _q_block(rows, col):
    return pl.BlockSpec((rows, QW), lambda j, n: (n, j + col))


def _kv_resident(t, col):
    return pl.BlockSpec((t, QW), lambda j, n: (0, j + col))


def _attn_fwd(q, k, v, *, kind, name, bias=None, seg=None):
    (q, cq), (k, ck), (v, cv), w = _col_operands(q, k, v)
    t = q.shape[0]
    quads = w // QW
    bq, nk, sub = _attn_geometry(kind)
    nq = t // (bq * sub)

    def body(*refs):
        if kind == "na":
            q_ref, k_ref, v_ref, b_ref, o_ref, l_ref = refs
        else:
            (q_ref, k_ref, v_ref, o_ref, l_ref), b_ref = refs, None
        for i in range(sub):
            n = pl.program_id(1) * sub + i
            rows = slice(i * bq, (i + 1) * bq)
            s, ws, _ = _attn_scores(kind, n, bq, nk, t, seg, _stack_heads(q_ref[rows, :]), k_ref, b_ref)
            m = jnp.max(s, axis=1, keepdims=True)
            p = jnp.exp(s - m)
            l = jnp.sum(p, axis=1, keepdims=True)
            o_ref[rows, :] = _unstack_heads(_dot(p / l, v_ref[pl.ds(ws, nk), :]), bq)
            l_ref[rows, :] = _unstack_heads(jnp.broadcast_to(m + jnp.log(l), (HS * bq, QW)), bq)

    blk = pl.BlockSpec((bq * sub, QW), lambda j, n: (n, j))
    in_specs = [_q_block(bq * sub, cq), _kv_resident(t, ck), _kv_resident(t, cv)]
    operands = [q, k, v]
    est = 4 * t * QW * q.dtype.itemsize + 12 * sub * HS * bq * nk * 4
    if kind == "na":
        in_specs.append(pl.BlockSpec((HS,) + bias.shape[1:], lambda j, n: (j, 0, 0, 0)))
        operands.append(bias)
        est += 2 * _nbytes((HS,) + bias.shape[1:], F32)
    return pl.pallas_call(
        body,
        name=name,
        grid=(quads, nq),
        in_specs=in_specs,
        out_specs=[blk, blk],
        out_shape=[jax.ShapeDtypeStruct((t, w), F32)] * 2,
        compiler_params=pltpu.CompilerParams(dimension_semantics=("arbitrary", "arbitrary"), vmem_limit_bytes=_vmem(est)),
    )(*operands)


def _attn_bwd(q, k, v, do, dterm, lse, *, kind, name, bias=None, seg=None):
    (q, cq), (k, ck), (v, cv), (do, cdo), (dterm, cdt), (lse, cl), w = _col_operands(q, k, v, do, dterm, lse)
    t = q.shape[0]
    quads = w // QW
    bq, nk, sub = _attn_geometry(kind)
    nq = t // (bq * sub)
    scale = HEAD_DIM ** -0.5

    def body(*refs):
        if kind == "na":
            q_ref, k_ref, v_ref, do_ref, dt_ref, l_ref, b_ref, dq_ref, dk_hbm, dv_hbm, db_ref, dk_acc, dv_acc, sem = refs
        else:
            q_ref, k_ref, v_ref, do_ref, dt_ref, l_ref, dq_ref, dk_hbm, dv_hbm, dk_acc, dv_acc, sem = refs
            b_ref = None
        j, step = pl.program_id(0), pl.program_id(1)

        @pl.when(step == 0)
        def _():
            dk_acc[...] = jnp.zeros_like(dk_acc)
            dv_acc[...] = jnp.zeros_like(dv_acc)
            if kind == "na":
                db_ref[...] = jnp.zeros_like(db_ref)

        for b in range(sub):
            n = step * sub + b
            rows = slice(b * bq, (b + 1) * bq)
            qs = _stack_heads(q_ref[rows, :])
            dos = _stack_heads(do_ref[rows, :])
            s, ws, ro0 = _attn_scores(kind, n, bq, nk, t, seg, qs, k_ref, b_ref)
            p = jnp.exp(s - _stack_cols(l_ref[rows, :], bq))
            dp = _dot_nt(dos, v_ref[pl.ds(ws, nk), :])
            ds = p * (dp - _stack_cols(dt_ref[rows, :], bq))
            if kind == "na":
                for e in range(HS):
                    for i in range(NA_KH // 2):
                        db_ref[e, ro0 + 2 * i] += ds[e * bq:(e + 1) * bq, i * LANES:(i + 1) * LANES]
            dsc = ds * scale
            dq_ref[rows, :] = _unstack_heads(_dot(dsc, k_ref[pl.ds(ws, nk), :]), bq)
            dk_acc[pl.ds(ws, nk), :] += _dot_tn(dsc, qs)
            dv_acc[pl.ds(ws, nk), :] += _dot_tn(p, dos)

        @pl.when(step == nq - 1)
        def _():
            ck = pltpu.make_async_copy(dk_acc, dk_hbm.at[j], sem.at[0])
            cv = pltpu.make_async_copy(dv_acc, dv_hbm.at[j], sem.at[1])
            ck.start()
            cv.start()
            ck.wait()
            cv.wait()

    blk = pl.BlockSpec((bq * sub, QW), lambda j, n: (n, j))
    in_specs = [_q_block(bq * sub, cq), _kv_resident(t, ck), _kv_resident(t, cv)] + [_q_block(bq * sub, c) for c in (cdo, cdt, cl)]
    operands = [q, k, v, do, dterm, lse]
    out_specs = [blk, ANY, ANY]
    out_shape = [jax.ShapeDtypeStruct((t, w), F32)] + [jax.ShapeDtypeStruct((quads, t, QW), F32)] * 2
    est = 4 * t * QW * q.dtype.itemsize + 2 * t * QW * 4 + 16 * sub * HS * bq * nk * 4
    if kind == "na":
        bspec = pl.BlockSpec((HS,) + bias.shape[1:], lambda j, n: (j, 0, 0, 0))
        in_specs.append(bspec)
        operands.append(bias)
        out_specs.append(bspec)
        out_shape.append(jax.ShapeDtypeStruct(bias.shape, F32))
        est += 4 * _nbytes((HS,) + bias.shape[1:], F32)
    res_ = pl.pallas_call(
        body,
        name=name,
        grid=(quads, nq),
        in_specs=in_specs,
        out_specs=out_specs,
        out_shape=out_shape,
        scratch_shapes=[pltpu.VMEM((t, QW), F32), pltpu.VMEM((t, QW), F32), pltpu.SemaphoreType.DMA((2,))],
        compiler_params=pltpu.CompilerParams(dimension_semantics=("arbitrary", "arbitrary"), vmem_limit_bytes=_vmem(est)),
    )(*operands)
    unquad = lambda a: [(a, i) for i in range(quads)]
    return (res_[0], unquad(res_[1]), unquad(res_[2])) + tuple(res_[3:])


def _na_onehot():
    qc = np.arange(GRID_W)[:, None]
    kc = np.arange(GRID_W)[None, :]
    start = np.clip(qc - NA_KW // 2, 0, GRID_W - NA_KW)
    inwin = (kc >= start) & (kc < start + NA_KW)
    off = kc - qc + (NA_KW - 1)
    e_mat = np.zeros((2, 32, GRID_W, 2, GRID_W), np.float32)
    for e in range(2):
        for c in range(2 * NA_KW - 1):
            e_mat[e, c, :, e, :] = (off == c) & inwin
    neg = np.where(inwin, 0.0, NEG_INF).astype(np.float32)
    neg = np.broadcast_to(neg[:, None, :], (GRID_W, 2, GRID_W)).reshape(1, GRID_W * LANES)
    return jnp.asarray(e_mat.reshape(64, GRID_W * LANES), MXU_DTYPE), jnp.asarray(neg)


def _na_rowpairs(rpb):
    p = jnp.pad(rpb, ((0, 0), (0, 0), (0, 1)))
    return jnp.concatenate([p[:, :-1], p[:, 1:]], axis=-1).reshape(NA_HEADS * (2 * NA_KH - 2), 64)


def _na_bias_table(rpb):
    r2 = _na_rowpairs(rpb)
    e_mat, neg = _na_onehot()

    def body(r_ref, e_ref, n_ref, o_ref):
        hi, mid, lo = _split3(r_ref[...])
        e = e_ref[...]
        o_ref[...] = _dot(hi, e) + _dot(mid, e) + _dot(lo, e) + n_ref[...]

    out = pl.pallas_call(
        body,
        name="na_bias_table",
        out_shape=jax.ShapeDtypeStruct((r2.shape[0], GRID_W * LANES), F32),
        compiler_params=pltpu.CompilerParams(vmem_limit_bytes=_vmem(6 * r2.shape[0] * GRID_W * LANES * 4)),
    )(r2, e_mat, neg)
    return out.reshape(NA_HEADS, 2 * NA_KH - 2, GRID_W, LANES)


def _na_bias_grad(dbt):
    e_mat, _ = _na_onehot()
    flat = dbt.reshape(NA_HEADS * (2 * NA_KH - 2), GRID_W * LANES)

    def body(d_ref, e_ref, o_ref):
        hi, mid, lo = _split3(d_ref[...])
        e = e_ref[...]
        o_ref[...] = _dot_nt(hi, e) + _dot_nt(mid, e) + _dot_nt(lo, e)

    g = pl.pallas_call(
        body,
        name="na_bias_grad",
        out_shape=jax.ShapeDtypeStruct((flat.shape[0], 64), F32),
        compiler_params=pltpu.CompilerParams(vmem_limit_bytes=_vmem(6 * flat.shape[0] * GRID_W * LANES * 4)),
    )(flat, e_mat)
    g = g.reshape(NA_HEADS, 2 * NA_KH - 2, 2, 32)[..., :2 * NA_KW - 1]
    first = jnp.pad(g[:, :, 0], ((0, 0), (0, 1), (0, 0)))
    second = jnp.pad(g[:, :, 1], ((0, 0), (1, 0), (0, 0)))
    return first + second


def _all_gather(arrs, *, name):
    na = len(arrs)

    def body(*refs):
        ins, outs = refs[:na], refs[na:2 * na]
        send_sems, recv_sems, local_sems = refs[2 * na:]
        x, y, c = lax.axis_index("x"), lax.axis_index("y"), lax.axis_index("c")
        me, sibling = (x, y, c), (x, y, 1 - c)
        chips = [(1 - x, y), (x, 1 - y), (1 - x, 1 - y)]

        def rows(a, px, py, pc):
            r = ins[a].shape[0]
            return outs[a].at[pl.ds((4 * px + 2 * py + pc) * r, r), :]

        def copy(a, k, block, to, src=None):
            return pltpu.make_async_remote_copy(
                src_ref=rows(a, *block) if src is None else src, dst_ref=rows(a, *block),
                send_sem=send_sems.at[a, k], recv_sem=recv_sems.at[a, k], device_id=to, device_id_type=MESH)

        mine = [pltpu.make_async_copy(ins[a], rows(a, *me), local_sems.at[a]) for a in range(na)]
        for cp in mine:
            cp.start()
        first = []
        for a in range(na):
            first.append(copy(a, 0, me, sibling, src=ins[a]))
            first += [copy(a, 1 + j, me, (*chip, c), src=ins[a]) for j, chip in enumerate(chips)]
        for cp in first:
            cp.start()
        passed = []
        for j, chip in enumerate(chips):
            for a in range(na):
                copy(a, 1 + j, (*chip, c), me).wait_recv()
                cp = copy(a, 4 + j, (*chip, c), sibling)
                cp.start()
                passed.append(cp)
        for a in range(na):
            copy(a, 0, sibling, me).wait_recv()
        for j, chip in enumerate(chips):
            for a in range(na):
                copy(a, 4 + j, (*chip, 1 - c), me).wait_recv()
        for cp in first + passed:
            cp.wait_send()
        for cp in mine:
            cp.wait()

    return pl.pallas_call(
        body,
        name=name,
        in_specs=[ANY] * na,
        out_specs=[ANY] * na,
        out_shape=[jax.ShapeDtypeStruct((N_DEV * a.shape[0], a.shape[1]), a.dtype) for a in arrs],
        scratch_shapes=[pltpu.SemaphoreType.DMA((na, 7)), pltpu.SemaphoreType.DMA((na, 7)), pltpu.SemaphoreType.DMA((na,))],
    )(*arrs)


HBM = pl.BlockSpec(memory_space=pltpu.HBM)
SEM = pl.BlockSpec(memory_space=pltpu.SEMAPHORE)
EFFECT = pltpu.SideEffectType.DATAFLOW_SIDE_EFFECTING


def _peer_of(k):
    x, y, c = lax.axis_index("x"), lax.axis_index("y"), lax.axis_index("c")
    return x ^ ((k >> 2) & 1), y ^ ((k >> 1) & 1), c ^ (k & 1)


def _split_copies(gather, src_ref, land_ref, send_sems, recv_sems):
    x, y, c = lax.axis_index("x"), lax.axis_index("y"), lax.axis_index("c")
    my = 4 * x + 2 * y + c
    r = src_ref.shape[0] if gather else src_ref.shape[0] // N_DEV
    copies = []
    for k in ((1, 2, 4, 6) if gather == "chip" else range(1, N_DEV)):
        px, py, pc = _peer_of(k)
        if gather:
            src, dst = src_ref, land_ref.at[pl.ds(my * r, r), :]
        else:
            src, dst = src_ref.at[pl.ds((4 * px + 2 * py + pc) * r, r), :], land_ref.at[k - 1]
        copies.append(pltpu.make_async_remote_copy(
            src_ref=src, dst_ref=dst, send_sem=send_sems.at[k - 1], recv_sem=recv_sems.at[k - 1],
            device_id=(px, py, pc), device_id_type=MESH))
    return copies


def _split_start(srcs, lands, *, gather, name, after=None):
    na = len(srcs)
    extra = [] if after is None else [after]

    def body(*refs):
        src_refs, land_refs = refs[:na], refs[na:2 * na]
        outs = refs[2 * na + len(extra):]
        for a in range(na):
            for cp in _split_copies(gather, src_refs[a], land_refs[a], outs[4 * a], outs[4 * a + 1]):
                cp.start()
        outs[4 * na][...] = jnp.zeros_like(outs[4 * na])

    out_shape, out_specs, aliases = [], [], {}
    for a in range(na):
        out_shape += [pltpu.SemaphoreType.DMA((N_DEV - 1,)), pltpu.SemaphoreType.DMA((N_DEV - 1,)),
                      pltpu.HBM(srcs[a].shape, srcs[a].dtype), pltpu.HBM(lands[a].shape, lands[a].dtype)]
        out_specs += [SEM, SEM, HBM, HBM]
        aliases[a] = 4 * a + 2
        aliases[na + a] = 4 * a + 3
    out_shape.append(jax.ShapeDtypeStruct((8, LANES), F32))
    out_specs.append(pl.BlockSpec(memory_space=pltpu.VMEM))
    res = pl.pallas_call(
        body,
        name=name,
        out_shape=tuple(out_shape),
        in_specs=[HBM] * (2 * na) + [ANY] * len(extra),
        out_specs=tuple(out_specs),
        input_output_aliases=aliases,
        compiler_params=pltpu.CompilerParams(has_side_effects=EFFECT),
    )(*[pltpu.with_memory_space_constraint(a, pltpu.HBM) for a in list(srcs) + list(lands)], *extra)
    return [tuple(res[4 * a:4 * a + 4]) for a in range(na)], res[4 * na][0, 0]


def _split_wait(handles, after, *, gather, name):
    na = len(handles)

    def body(*refs):
        src_refs, land_refs = refs[:na], refs[na:2 * na]
        sems = refs[2 * na:4 * na]
        for a in range(na):
            for cp in _split_copies(gather, src_refs[a], land_refs[a], sems[2 * a], sems[2 * a + 1]):
                cp.wait_send()
                cp.wait_recv()

    srcs = [h[2] for h in handles]
    lands = [h[3] for h in handles]
    sems = [s for h in handles for s in h[:2]]
    res = pl.pallas_call(
        body,
        name=name,
        out_shape=tuple(pltpu.HBM(a.shape, a.dtype) for a in srcs + lands),
        in_specs=[HBM] * (2 * na) + [SEM] * (2 * na) + [ANY],
        out_specs=tuple([HBM] * (2 * na)),
        input_output_aliases={i: i for i in range(2 * na)},
        compiler_params=pltpu.CompilerParams(has_side_effects=EFFECT),
    )(*srcs, *lands, *sems, after)
    return list(res[:na]), list(res[na:])


def _forward_copies(land_ref, send_sems, recv_sems):
    x, y, c = lax.axis_index("x"), lax.axis_index("y"), lax.axis_index("c")
    r = land_ref.shape[0] // N_DEV
    copies = []
    for j, k in enumerate((2, 4, 6)):
        px, py, pc = _peer_of(k)
        rows = land_ref.at[pl.ds((4 * px + 2 * py + pc) * r, r), :]
        copies.append(pltpu.make_async_remote_copy(
            src_ref=rows, dst_ref=rows, send_sem=send_sems.at[j], recv_sem=recv_sems.at[j],
            device_id=(x, y, 1 - c), device_id_type=MESH))
    return copies


def _forward_start(land, *, name):
    def body(land_ref, send_sems, recv_sems, land_thru, token):
        for cp in _forward_copies(land_ref, send_sems, recv_sems):
            cp.start()
        token[...] = jnp.zeros_like(token)

    res = pl.pallas_call(
        body,
        name=name,
        out_shape=(pltpu.SemaphoreType.DMA((3,)), pltpu.SemaphoreType.DMA((3,)), pltpu.HBM(land.shape, land.dtype),
                   jax.ShapeDtypeStruct((8, LANES), F32)),
        in_specs=[HBM],
        out_specs=(SEM, SEM, HBM, pl.BlockSpec(memory_space=pltpu.VMEM)),
        input_output_aliases={0: 2},
        compiler_params=pltpu.CompilerParams(has_side_effects=EFFECT),
    )(pltpu.with_memory_space_constraint(land, pltpu.HBM))
    return res[:3]


def _forward_wait(handle, *, name):
    send_sems, recv_sems, land = handle

    def body(land_ref, send_ref, recv_ref, land_out):
        for cp in _forward_copies(land_ref, send_ref, recv_ref):
            cp.wait_send()
            cp.wait_recv()

    return pl.pallas_call(
        body,
        name=name,
        out_shape=pltpu.HBM(land.shape, land.dtype),
        in_specs=[HBM, SEM, SEM],
        out_specs=HBM,
        input_output_aliases={0: 0},
        compiler_params=pltpu.CompilerParams(has_side_effects=EFFECT),
    )(land, send_sems, recv_sems)


def _sum8(own, recv, *, name):
    _, r, w = recv.shape
    fits = lambda c: 2 * c * w * (N_DEV * recv.dtype.itemsize + 4) <= 32 * 2**20
    tr = _pick(r, tuple(c for c in (r // 2, r // 4, 256, 128, 64, 32, 16, 8) if c % 16 == 0 and fits(c)))

    def body(own_ref, a_ref, o_ref):
        acc = own_ref[...].astype(F32)
        for i in range(N_DEV - 1):
            acc = acc + a_ref[i].astype(F32)
        o_ref[...] = acc

    return pl.pallas_call(
        body,
        name=name,
        grid=(r // tr,),
        in_specs=[pl.BlockSpec((tr, w), lambda i: (i, 0)), pl.BlockSpec((N_DEV - 1, tr, w), lambda i: (0, i, 0))],
        out_specs=pl.BlockSpec((tr, w), lambda i: (i, 0)),
        out_shape=jax.ShapeDtypeStruct((r, w), F32),
        compiler_params=pltpu.CompilerParams(dimension_semantics=("parallel",), vmem_limit_bytes=_vmem(4 * N_DEV * tr * w * 4)),
    )(own, recv)


def _adamw(w, g, m, v, *, name):
    def fn(rows, _):
        wv, gv, mv, vv = rows
        m1 = ADAM_B1 * mv + (1.0 - ADAM_B1) * gv
        v1 = ADAM_B2 * vv + (1.0 - ADAM_B2) * jnp.square(gv)
        m_hat = m1 / (1.0 - ADAM_B1 ** ADAM_STEP)
        v_hat = v1 / (1.0 - ADAM_B2 ** ADAM_STEP)
        delta = -ADAM_LR * (m_hat / (jnp.sqrt(v_hat) + ADAM_EPS) + ADAM_WD * wv)
        return [delta, m1, v1], []

    c = w.shape[1]
    return _rowmap(fn, [w, g, m, v], [], [(c, F32)] * 3, [], name=name)


_SMALL = ("b_ada", "g_norm1", "g_norm2", "b_gate", "g_qa", "g_ka", "g_qb", "g_kb", "rpb", "loss")


def _pack_small(parts):
    flat = []
    for nme in _SMALL:
        a = parts[nme].reshape(-1).astype(F32)
        flat.append(jnp.pad(a, (0, (-a.shape[0]) % LANES)))
    flat = jnp.concatenate(flat)
    flat = jnp.pad(flat, (0, (-flat.shape[0]) % (LANES * LANES)))
    return flat.reshape(-1, LANES)


def _unpack_small(packed, shapes):
    flat = packed.reshape(-1)
    out, pos = {}, 0
    for nme in _SMALL:
        n = int(np.prod(shapes[nme]))
        out[nme] = flat[pos:pos + n].reshape(shapes[nme])
        pos += n + (-n) % LANES
    return out


def _to_class(a, d):
    t, w = a.shape
    return a if d == 1 else a.reshape(t // d, d, w).transpose(1, 0, 2).reshape(t, w)


def _from_class(a, d):
    t, w = a.shape
    return a if d == 1 else a.reshape(d, t // d, w).transpose(1, 0, 2).reshape(t, w)


def kernel(x, c, w_ada, b_ada, g_norm1, g_norm2, w_in, b_gate, g_qa, g_ka, g_qb, g_kb, rpb, w_proj_a, w_proj_b, w_o, w_ffn_in, w_ffn_out, loss_target, m_w_ada, m_b_ada, m_g_norm1, m_g_norm2, m_w_in, m_b_gate, m_g_qa, m_g_ka, m_g_qb, m_g_kb, m_rpb, m_w_proj_a, m_w_proj_b, m_w_o, m_w_ffn_in, m_w_ffn_out, v_w_ada, v_b_ada, v_g_norm1, v_g_norm2, v_w_in, v_b_gate, v_g_qa, v_g_ka, v_g_qb, v_g_kb, v_rpb, v_w_proj_a, v_w_proj_b, v_w_o, v_w_ffn_in, v_w_ffn_out):
    t, d = x.shape[1], x.shape[2]
    d_ff = w_ffn_out.shape[1] * N_DEV
    me = 4 * lax.axis_index("x") + 2 * lax.axis_index("y") + lax.axis_index("c")
    xt, tgt = x.reshape(t, d), loss_target.reshape(t, d)
    ones = _head_ones()

    shards = [s.astype(WIRE_DTYPE) for s in (w_in[0].T, w_ffn_in[0].T, w_proj_a[0].T, w_proj_b[0].T, w_o[0], w_ffn_out[0])]
    lands = [lax.dynamic_update_slice(lax.empty((N_DEV * s.shape[0], s.shape[1]), s.dtype), s, (me * s.shape[0], 0))
             for s in shards]

    c_all = _all_gather([jnp.pad(c, ((0, 7), (0, 0)))], name="gather_c")[0][::8]
    c_all = jnp.pad(c_all, ((0, LANES - N_DEV), (0, 0)))

    def mod_body(c_ref, w_ref, b_ref, o_ref, act_ref):
        act = _silu(c_ref[...])
        act_ref[...] = act
        hi, mid, lo = _split3(act)
        w = w_ref[...]
        whi, wmid, wlo = _split3(w)
        acc = _dot(hi, whi) + (_dot(hi, wmid) + _dot(mid, whi)) + (_dot(hi, wlo) + _dot(mid, wmid) + _dot(lo, whi))
        o_ref[...] = acc + b_ref[...]

    ncol = w_ada.shape[2]
    b_ada_mine = lax.dynamic_slice(b_ada, (0, me * ncol), (1, ncol))
    mod_part, c_act = pl.pallas_call(
        mod_body,
        name="ada_mod",
        out_shape=[jax.ShapeDtypeStruct((LANES, ncol), F32), jax.ShapeDtypeStruct((LANES, d), F32)],
        compiler_params=pltpu.CompilerParams(vmem_limit_bytes=_vmem(6 * d * ncol * 4)),
    )(c_all, w_ada[0], b_ada_mine)
    mod_all = _all_gather([mod_part[:N_DEV]], name="gather_mod")[0].reshape(N_DEV, N_DEV, ncol)
    mod = lax.dynamic_index_in_dim(mod_all, me, axis=1, keepdims=False).reshape(6, d)
    sh1, sc1, gt1, sh2, sc2, gt2 = [mod[i:i + 1] for i in range(6)]

    def norm_fwd(rows, vecs):
        (xv,), (g, sc, sh) = rows, vecs
        return [xv * _rms(xv) * g * (1.0 + sc) + sh], []

    w_in_handle, w_token = _split_start(shards[:1], lands[:1], gather="chip", after=mod, name="gather_w_in_start")
    (h,) = _rowmap(norm_fwd, [xt], [g_norm1 + w_token, sc1, sh1], [(d, MXU_DTYPE)], [], name="norm1")
    n_a, n_b = 3 * WA, 3 * WB
    (w_in_t,) = _split_wait(w_in_handle, h, gather="chip", name="gather_w_in_wait")[1]
    w_in_t = _forward_wait(_forward_start(w_in_t, name="gather_w_in_forward_start"), name="gather_w_in_forward_wait")
    w_handles, w_token = _split_start(shards[1:], lands[1:], gather=True, after=w_in_t, name="gather_weights_start")
    w_in_a, w_in_b, w_in_g = w_in_t[:n_a], w_in_t[n_a:n_a + n_b], w_in_t[n_a + n_b:]

    rot_c, rot_lo, rot_hi = _rot_tables(t)
    tile_g = lambda g, heads: jnp.tile(g, (1, heads))

    def qk_fwd(width, rotate):
        def fn(xv, rows, vecs):
            gq, gk, on = vecs
            qkv = []
            for i, g in enumerate((gq, gk)):
                xi = xv[:, i * width:(i + 1) * width]
                r = lax.rsqrt(_headsum(xi * xi, on) * (1.0 / HEAD_DIM) + EPS)
                yi = xi * r * g
                if rotate:
                    yi = _rot(yi, rows[0], rows[1], rows[2])
                qkv.append(yi)
            qkv.append(xv[:, 2 * width:])
            if not rotate:
                return [xv] + qkv, []
            groups = [jnp.concatenate([a[:, g * WB_OUT:(g + 1) * WB_OUT] for a in qkv], axis=1)
                      for g in range(len(DIL_CONFIGS))]
            return [xv] + groups, []
        return fn

    qkv_a, qa, ka, va = _mm_parts_rows(
        [(h, w_in_a, "nt")], qk_fwd(WA, False), [], [tile_g(g_qa, NA_HEADS) + w_token, tile_g(g_ka, NA_HEADS), ones],
        [(3 * WA, ACT_DTYPE)] + [(WA, MXU_DTYPE)] * 3, [], name="proj_a_qknorm")
    qkv_b, *qkv_groups = _mm_parts_rows(
        [(h, w_in_b, "nt")], qk_fwd(WB, True), [rot_c, rot_lo, rot_hi],
        [tile_g(g_qb, DIL_HEADS), tile_g(g_kb, DIL_HEADS), ones],
        [(3 * WB, ACT_DTYPE)] + [(3 * WB_OUT, MXU_DTYPE)] * len(DIL_CONFIGS), [], name="proj_b_qknorm")

    bias_tab = _na_bias_table(rpb[0])
    o_a, lse_a = _attn_fwd(qa, ka, va, kind="na", bias=bias_tab, name="na_fwd")

    grp = []
    for g, (_, dil) in enumerate(DIL_CONFIGS):
        qkv_c = _to_class(qkv_groups[g], dil)
        og, lg = _attn_fwd((qkv_c, 0), (qkv_c, 1), (qkv_c, 2), kind="dil", seg=t // dil, name=f"dil_fwd{g}")
        grp.append(dict(qkv=qkv_c, o=_from_class(og, dil), lse=_from_class(lg, dil), lse_c=lg, dil=dil))

    def merge_fwd(rows, _):
        o0, o1, o2, l0, l1, l2 = rows
        mx = jnp.maximum(jnp.maximum(l0, l1), l2)
        e0, e1, e2 = jnp.exp(l0 - mx), jnp.exp(l1 - mx), jnp.exp(l2 - mx)
        s = e0 + e1 + e2
        return [(e0 / s) * o0 + (e1 / s) * o1 + (e2 / s) * o2], []

    (o_b,) = _rowmap(merge_fwd, [gr["o"] for gr in grp] + [gr["lse"] for gr in grp], [], [(WB_OUT, F32)], [], name="dil_merge")

    w_pa_t, w_pb_t, w_o_f = _split_wait(w_handles[1:4], o_b, gather=True, name="gather_w_out_wait")[1]
    def gate_fwd(prods, _, vecs):
        gv, pav, pbv = prods
        sg = jax.nn.sigmoid(gv + vecs[0])
        return [gv, pav, pbv, sg[:, :d] * pav + sg[:, d:] * pbv], []

    gates, pa, pb, merged = _mm_parts_rows(
        [(h, w_in_g, "nt"), (o_a, w_pa_t, "nt"), (o_b, w_pb_t, "nt")], gate_fwd, [], [b_gate],
        [(2 * d, ACT_DTYPE), (d, ACT_DTYPE), (d, ACT_DTYPE), (d, MXU_DTYPE)], [], separate=True, name="proj_gates_out_merge")
    def resid_norm(av, rows, vecs):
        (xv,), (gt, g, sc, sh) = rows, vecs
        x1v = xv + gt * av
        return [av, x1v, x1v * _rms(x1v) * g * (1.0 + sc) + sh], []

    att, x1, h2 = _mm_parts_rows([(merged, w_o_f)], resid_norm, [xt], [gt1, g_norm2, sc2, sh2],
                           [(d, F32), (d, F32), (d, MXU_DTYPE)], [], name="proj_o_resid_norm2")

    w_ffn_in_t, w_ffn_out_f = _split_wait([w_handles[0], w_handles[4]], h2, gather=True, name="gather_w_ffn_wait")[1]
    w_ffn_a, w_ffn_up = w_ffn_in_t[:d_ff], w_ffn_in_t[d_ff:]

    def swiglu_fwd(prods, _):
        a, up = prods
        return [a, up, _silu(a) * up]

    ua, uu, f = _mm_ew(h2, [w_ffn_a, w_ffn_up], swiglu_fwd, [], [ACT_DTYPE, ACT_DTYPE, MXU_DTYPE], name="ffn_in_swiglu")

    def loss_fn(yv, rows, vecs):
        (x1v, tv), gt = rows, vecs[0]
        err = x1v + gt * yv - tv
        dout = err * (1.0 / d)
        return [dout, dout * gt], [_colsum(err * err), _colsum(dout * yv)]

    dout, dy2, err2, dgt2 = _mm_parts_rows([(f, w_ffn_out_f)], loss_fn, [x1, tgt], [gt2], [(d, F32), (d, MXU_DTYPE)],
                                           [d, d], name="ffn_out_loss")

    dw_ffn_out = _mm(f, dy2, ta=True, out_dtype=WIRE_DTYPE, name="wgrad_ffn_out")
    def swiglu_bwd(prods, rows):
        (dfv,), (a, up) = prods, rows
        sg = jax.nn.sigmoid(a)
        return [dfv * up * (sg * (1.0 + a * (1.0 - sg))), dfv * (a * sg)]

    da, dup = _mm_ew(dy2, [w_ffn_out_f], swiglu_bwd, [ua, uu], [MXU_DTYPE, MXU_DTYPE], name="dgrad_ffn_out_swiglu_bwd")
    dw_ffn_in_t = _mm(da, h2, ta=True, into=(lax.empty((2 * d_ff, d), WIRE_DTYPE), 0), name="wgrad_ffn_in_a")
    dw_ffn_in_t = _mm(dup, h2, ta=True, into=(dw_ffn_in_t, d_ff), name="wgrad_ffn_in_up")
    land7 = lambda a: lax.empty((N_DEV - 1, a.shape[0] // N_DEV, a.shape[1]), a.dtype)
    own_block = lambda a: lax.dynamic_slice(a, (me * (a.shape[0] // N_DEV), 0), (a.shape[0] // N_DEV, a.shape[1]))
    g_ffn = [dw_ffn_in_t, dw_ffn_out]
    h_ffn, tok_ffn = _split_start(g_ffn, [land7(a) for a in g_ffn], gather=False, name="exchange_ffn_start")
    def norm_bwd(dh, xv, g, sc):
        r = _rms(xv)
        xh = xv * r
        dxh = dh * g * (1.0 + sc)
        dxv = r * (dxh - xh * jnp.mean(dxh * xh, axis=-1, keepdims=True))
        return dxv, [_colsum(dh), _colsum(dh * xh * g), _colsum(dh * xh * (1.0 + sc))]

    def norm2_bwd(dhv, rows, vecs):
        (x1v, dov, av), (g, sc, gt) = rows, vecs
        dxv, sums = norm_bwd(dhv, x1v, g, sc)
        dx1v = dov + dxv
        return [dx1v, dx1v * gt], sums + [_colsum(dx1v * av)]

    dx1, datt, dsh2, dsc2, dg2, dgt1 = _mm_parts_rows(
        [(da, w_ffn_a), (dup, w_ffn_up)], norm2_bwd, [x1, dout, att], [g_norm2 + tok_ffn, sc2, gt1],
        [(d, F32), (d, MXU_DTYPE)], [d] * 4, name="dgrad_ffn_in_norm2_bwd")
    dw_o = _mm(merged, datt, ta=True, out_dtype=WIRE_DTYPE, name="wgrad_o")
    def gate_bwd(dm, rows, vecs):
        gv, pav, pbv = rows
        sg = jax.nn.sigmoid(gv + vecs[0])
        ga, gb = sg[:, :d], sg[:, d:]
        dgp = jnp.concatenate([dm * pav * ga * (1.0 - ga), dm * pbv * gb * (1.0 - gb)], axis=1)
        return [dm * ga, dm * gb, dgp], [_colsum(dgp)]

    dpa, dpb, dgates, db_gate = _mm_parts_rows(
        [(datt, w_o_f.T)], gate_bwd, [gates, pa, pb], [b_gate],
        [(d, MXU_DTYPE), (d, MXU_DTYPE), (2 * d, MXU_DTYPE)], [2 * d], name="dgrad_o_gate_bwd")
    dw_pa_t = _mm(dpa, o_a, ta=True, out_dtype=WIRE_DTYPE, name="wgrad_proj_a")
    dw_pb_t = _mm(dpb, o_b, ta=True, out_dtype=WIRE_DTYPE, name="wgrad_proj_b")
    g_out = [dw_pa_t, dw_pb_t, dw_o]
    h_out, tok_out = _split_start(g_out, [land7(a) for a in g_out], gather=False, name="exchange_out_start")
    def delta_a(doa, rows, vecs):
        return [doa, _headsum(doa * rows[0], vecs[0])], []

    do_a, dterm_a = _mm_parts_rows([(dpa, w_pa_t)], delta_a, [o_a], [ones + tok_out.astype(ones.dtype)],
                                   [(WA, F32), (WA, F32)], [], name="dgrad_proj_a_delta")
    dqa, dka, dva, dbias = _attn_bwd(qa, ka, va, do_a, dterm_a, lse_a, kind="na", bias=bias_tab, name="na_bwd")
    g_rpb = _na_bias_grad(dbias)

    def merge_bwd(dob, rows, vecs):
        o0, o1, o2, l0, l1, l2 = rows
        on = vecs[0]
        mx = jnp.maximum(jnp.maximum(l0, l1), l2)
        e0, e1, e2 = jnp.exp(l0 - mx), jnp.exp(l1 - mx), jnp.exp(l2 - mx)
        s = e0 + e1 + e2
        ws = [e0 / s, e1 / s, e2 / s]
        dws = [_headsum(dob * o, on) for o in (o0, o1, o2)]
        mean = ws[0] * dws[0] + ws[1] * dws[1] + ws[2] * dws[2]
        return [jnp.concatenate([w * dob, w * mean], axis=1) for w in ws], []

    mb = _mm_parts_rows([(dpb, w_pb_t)], merge_bwd, [gr["o"] for gr in grp] + [gr["lse"] for gr in grp], [ones],
                        [(2 * WB_OUT, F32)] * len(grp), [], name="dgrad_proj_b_merge_bwd")
    dqb, dkb, dvb = [], [], []
    for g, gr in enumerate(grp):
        dil, qkv_c = gr["dil"], gr["qkv"]
        dd_c = _to_class(mb[g], dil)
        dq, dk, dv = _attn_bwd((qkv_c, 0), (qkv_c, 1), (qkv_c, 2), (dd_c, 0), (dd_c, 1), (gr["lse_c"], 0),
                               kind="dil", seg=t // dil, name=f"dil_bwd{g}")
        dqb.append(_from_class(dq, dil))
        dkb.append(dk[0] if dil == 1 else _from_class(dk[0][0][0], dil))
        dvb.append(dv[0] if dil == 1 else _from_class(dv[0][0][0], dil))

    def qk_bwd(width, rotate, nparts):
        def fn(rows, vecs):
            gq, gk, on = vecs
            xv = rows[0]
            pos = 1
            if rotate:
                rc, rlo, rhi = rows[1:4]
                pos = 4
            cat = lambda parts: parts[0] if len(parts) == 1 else jnp.concatenate(parts, axis=1)
            ends = np.cumsum((pos,) + nparts)
            dq, dk, dv = [cat(rows[ends[i]:ends[i + 1]]) for i in range(3)]
            outs, sums = [], []
            for i, (dy, g) in enumerate(((dq, gq), (dk, gk))):
                if rotate:
                    dy = _rot(dy, rc, -rlo, -rhi)
                xi = xv[:, i * width:(i + 1) * width]
                r = lax.rsqrt(_headsum(xi * xi, on) * (1.0 / HEAD_DIM) + EPS)
                xh = xi * r
                dxh = dy * g
                outs.append(r * (dxh - xh * (_headsum(dxh * xh, on) * (1.0 / HEAD_DIM))))
                sums.append(_colsum(dy * xh))
            return [jnp.concatenate(outs + [dv], axis=1)], sums
        return fn

    dqkv_a, dg_qa, dg_ka = _rowmap(qk_bwd(WA, False, (1, len(dka), len(dva))), [qkv_a, dqa] + dka + dva,
                                   [tile_g(g_qa, NA_HEADS), tile_g(g_ka, NA_HEADS), ones],
                                   [(3 * WA, MXU_DTYPE)], [WA, WA], name="qknorm_a_bwd")
    dqkv_b, dg_qb, dg_kb = _rowmap(qk_bwd(WB, True, (3, 3, 3)), [qkv_b, rot_c, rot_lo, rot_hi] + dqb + dkb + dvb,
                                   [tile_g(g_qb, DIL_HEADS), tile_g(g_kb, DIL_HEADS), ones],
                                   [(3 * WB, MXU_DTYPE)], [WB, WB], name="qknorm_b_bwd")

    dw_in_t = jnp.concatenate([
        _mm(dqkv_a, h, ta=True, out_dtype=WIRE_DTYPE, name="wgrad_in_a"),
        _mm(dqkv_b, h, ta=True, out_dtype=WIRE_DTYPE, name="wgrad_in_b"),
        _mm(dgates, h, ta=True, out_dtype=WIRE_DTYPE, name="wgrad_in_gates")], axis=0)
    h_in, tok_in = _split_start([dw_in_t], [land7(dw_in_t)], gather=False, name="exchange_in_start")
    def norm1_bwd(dhv, rows, vecs):
        xv, dx1v = rows
        dxv, sums = norm_bwd(dhv, xv, vecs[0], vecs[1])
        return [dx1v + dxv], sums

    grad_x, dsh1, dsc1, dg1 = _mm_parts_rows(
        [(dqkv_a, w_in_a), (dqkv_b, w_in_b), (dgates, w_in_g)], norm1_bwd, [xt, dx1], [g_norm1 + tok_in, sc1],
        [(d, F32)], [d] * 3, name="dgrad_in_norm1_bwd")

    heads_sum = lambda a, heads: a.reshape(heads, HEAD_DIM).sum(axis=0)
    dmod = jnp.concatenate([dsh1, dsc1, dgt1, dsh2, dsc2, dgt2], axis=1)
    local_small = _pack_small(dict(
        b_ada=dmod, g_norm1=dg1, g_norm2=dg2, b_gate=db_gate, g_qa=heads_sum(dg_qa, NA_HEADS),
        g_ka=heads_sum(dg_ka, NA_HEADS), g_qb=heads_sum(dg_qb, DIL_HEADS), g_kb=heads_sum(dg_kb, DIL_HEADS),
        rpb=g_rpb, loss=(0.5 / d) * jnp.sum(err2)))
    srows = local_small.shape[0]
    small_all = _all_gather([local_small], name="gather_small")[0].reshape(N_DEV, srows, LANES)
    small_sum = _sum8(small_all[0], small_all[1:], name="sum_small")
    small_shapes = dict(b_ada=b_ada.shape, g_norm1=g_norm1.shape, g_norm2=g_norm2.shape, b_gate=b_gate.shape,
                        g_qa=g_qa.shape, g_ka=g_ka.shape, g_qb=g_qb.shape, g_kb=g_kb.shape, rpb=rpb.shape, loss=())
    small_w = dict(b_ada=b_ada, g_norm1=g_norm1, g_norm2=g_norm2, b_gate=b_gate, g_qa=g_qa, g_ka=g_ka, g_qb=g_qb,
                   g_kb=g_kb, rpb=rpb, loss=jnp.zeros((), F32))
    small_m = dict(b_ada=m_b_ada, g_norm1=m_g_norm1, g_norm2=m_g_norm2, b_gate=m_b_gate, g_qa=m_g_qa, g_ka=m_g_ka,
                   g_qb=m_g_qb, g_kb=m_g_kb, rpb=m_rpb, loss=jnp.zeros((), F32))
    small_v = dict(b_ada=v_b_ada, g_norm1=v_g_norm1, g_norm2=v_g_norm2, b_gate=v_b_gate, g_qa=v_g_qa, g_ka=v_g_ka,
                   g_qb=v_g_qb, g_kb=v_g_kb, rpb=v_rpb, loss=jnp.zeros((), F32))
    s_delta, s_m, s_v = _adamw(_pack_small(small_w), small_sum, _pack_small(small_m), _pack_small(small_v), name="adamw_small")
    gs = _unpack_small(small_sum, small_shapes)
    ds_, ms_, vs_ = [_unpack_small(a, small_shapes) for a in (s_delta, s_m, s_v)]

    dmod_all = small_all[:, :6 * d // LANES].reshape(N_DEV, 6 * d)
    dmod_mine = jnp.pad(lax.dynamic_slice(dmod_all, (0, me * ncol), (N_DEV, ncol)), ((0, LANES - N_DEV), (0, 0)))

    def wada_body(c_ref, dm_ref, o_ref):
        chi, cmid, clo = _split3(c_ref[...])
        dhi, dmid, dlo = _split3(dm_ref[...])
        o_ref[...] = (_dot_tn(chi, dhi) + (_dot_tn(chi, dmid) + _dot_tn(cmid, dhi))
                      + (_dot_tn(chi, dlo) + _dot_tn(cmid, dmid) + _dot_tn(clo, dhi)))

    g_w_ada = pl.pallas_call(
        wada_body,
        name="wgrad_ada",
        out_shape=jax.ShapeDtypeStruct((d, ncol), F32),
        compiler_params=pltpu.CompilerParams(vmem_limit_bytes=_vmem(4 * d * ncol * 4)),
    )(c_act, dmod_mine)

    sent, recv = _split_wait(h_in + h_ffn + h_out, small_sum, gather=False, name="exchange_wait")
    names = ("w_in", "w_ffn_in", "w_ffn_out", "w_proj_a", "w_proj_b", "w_o")
    transposed = (True, True, False, True, True, False)
    big_g = {}
    for nme, own, r, tr in zip(names, sent, recv, transposed):
        s = _sum8(own_block(own), r, name=f"sum_{nme}")
        big_g[nme] = s.T if tr else s
    big_g["w_ada"] = g_w_ada
    big_w = dict(w_ada=w_ada, w_in=w_in, w_proj_a=w_proj_a, w_proj_b=w_proj_b, w_o=w_o, w_ffn_in=w_ffn_in, w_ffn_out=w_ffn_out)
    big_m = dict(w_ada=m_w_ada, w_in=m_w_in, w_proj_a=m_w_proj_a, w_proj_b=m_w_proj_b, w_o=m_w_o, w_ffn_in=m_w_ffn_in, w_ffn_out=m_w_ffn_out)
    big_v = dict(w_ada=v_w_ada, w_in=v_w_in, w_proj_a=v_w_proj_a, w_proj_b=v_w_proj_b, w_o=v_w_o, w_ffn_in=v_w_ffn_in, w_ffn_out=v_w_ffn_out)
    grads, deltas, new_m, new_v = {}, {}, {}, {}
    for nme in big_w:
        dl, m1, v1 = _adamw(big_w[nme][0], big_g[nme], big_m[nme][0], big_v[nme][0], name=f"adamw_{nme}")
        grads[nme], deltas[nme], new_m[nme], new_v[nme] = big_g[nme][None], dl[None], m1[None], v1[None]
    for nme in _SMALL[:-1]:
        grads[nme], deltas[nme], new_m[nme], new_v[nme] = gs[nme], ds_[nme], ms_[nme], vs_[nme]

    order = ("w_ada", "b_ada", "g_norm1", "g_norm2", "w_in", "b_gate", "g_qa", "g_ka", "g_qb", "g_kb", "rpb",
             "w_proj_a", "w_proj_b", "w_o", "w_ffn_in", "w_ffn_out")
    return (gs["loss"], grad_x[None], *[grads[n] for n in order], *[deltas[n] for n in order],
            *[new_m[n] for n in order], *[new_v[n] for n in order])
```

```python
import functools

import numpy as np
import jax
import jax.numpy as jnp
from jax import lax
from jax.experimental import pallas as pl
from jax.experimental.pallas import tpu as pltpu

F32 = jnp.float32
MXU_DTYPE = jnp.bfloat16
WIRE_DTYPE = jnp.bfloat16
ACT_DTYPE = jnp.bfloat16

HEAD_DIM = 64
GRID_W = 64
NA_HEADS = 8
NA_KH = 8
NA_KW = 16
DIL_CONFIGS = ((128, 1), (512, 4), (2048, 16))
DIL_HEADS_PER_GROUP = 4
DIL_HEADS = DIL_HEADS_PER_GROUP * len(DIL_CONFIGS)
DIL_HALF = 64
ROT_DIM = HEAD_DIM // 4
ROPE_THETA = 500000.0
EPS = 1e-6
NEG_INF = -1e30
WA = NA_HEADS * HEAD_DIM
WB = DIL_HEADS * HEAD_DIM
WB_OUT = DIL_HEADS_PER_GROUP * HEAD_DIM
ADAM_LR = 0.001
ADAM_B1 = 0.9
ADAM_B2 = 0.999
ADAM_EPS = 1e-08
ADAM_WD = 0.01
ADAM_STEP = 10

N_DEV = 8
LANES = 128
VMEM_CAP = 60 * 2**20
VMEM_FLOOR = 56 * 2**20
MESH = pl.DeviceIdType.MESH
ANY = pl.BlockSpec(memory_space=pl.ANY)


def _vmem(nbytes):
    return int(min(VMEM_CAP, max(VMEM_FLOOR, nbytes * 5 // 4 + 4 * 2**20)))


def _pick(dim, cands):
    for c in cands:
        if c <= dim and dim % c == 0:
            return c
    return dim


def _nbytes(shape, dtype):
    return int(np.prod(shape)) * jnp.dtype(dtype).itemsize


def _dot(a, b, dims=((1,), (0,))):
    return lax.dot_general(a.astype(MXU_DTYPE), b.astype(MXU_DTYPE), (dims, ((), ())), preferred_element_type=F32)


def _dot_nt(a, b):
    return _dot(a, b, ((1,), (1,)))


def _dot_tn(a, b):
    return _dot(a, b, ((0,), (0,)))


def _split3(a):
    hi = a.astype(jnp.bfloat16)
    r1 = a - hi.astype(F32)
    mid = r1.astype(jnp.bfloat16)
    lo = (r1 - mid.astype(F32)).astype(jnp.bfloat16)
    return hi, mid, lo


def _silu(x):
    return x * jax.nn.sigmoid(x)


def _divisors(dim, unit):
    return [c for c in range(unit, dim + 1, unit) if dim % c == 0] or [dim]


def _mm_tiles(m, n, kdim, a_item, b_item, o_item, row_off=0):
    step_us, hbm_bytes_per_us, flops_per_us, budget = 0.35, 3.0e6, 8.0e8, 40 * 2**20
    best = None
    for tm in _divisors(m, LANES):
        for tn in _divisors(n, LANES):
            for tk in _divisors(kdim, LANES):
                if row_off % tm:
                    continue
                gm, gn, gk = m // tm, n // tn, kdim // tk
                vmem = 2 * (tm * tk * a_item + tk * tn * b_item + tm * tn * o_item) + 2 * (tm * tk + tk * tn)
                vmem += tm * tn * 4 * ((1 if gk > 1 else 0) + 1)
                if vmem > budget:
                    continue
                a_reads = m * kdim * a_item * (gn if gk > 1 else 1)
                traffic = a_reads + kdim * n * b_item * gm + m * n * o_item
                cost = gm * gn * gk * step_us + max(traffic / hbm_bytes_per_us, 2.0 * m * n * kdim / flops_per_us)
                if best is None or cost < best[0]:
                    best = (cost, tm, tn, tk)
    return best[1:]


def _mm(a, b, *, name, ta=False, tb=False, out_dtype=F32, into=None):
    if ta:
        kdim, m = a.shape
    else:
        m, kdim = a.shape
    n = b.shape[0] if tb else b.shape[1]
    assert b.shape[1 if tb else 0] == kdim
    buf, row_off = into if into is not None else (None, 0)
    if buf is not None:
        out_dtype = buf.dtype
    tm, tn, tk = _mm_tiles(m, n, kdim, a.dtype.itemsize, b.dtype.itemsize, jnp.dtype(out_dtype).itemsize, row_off)
    gm, gn, gk = m // tm, n // tn, kdim // tk
    ob = row_off // tm

    a_spec = pl.BlockSpec((tk, tm), lambda i, j, k: (k, i)) if ta else pl.BlockSpec((tm, tk), lambda i, j, k: (i, k))
    b_spec = pl.BlockSpec((tn, tk), lambda i, j, k: (j, k)) if tb else pl.BlockSpec((tk, tn), lambda i, j, k: (k, j))
    o_spec = pl.BlockSpec((tm, tn), lambda i, j, k: (i + ob, j))
    a_dims = (0,) if ta else (1,)
    b_dims = (1,) if tb else (0,)

    def body(a_ref, b_ref, *rest):
        o_ref, scratch = rest[-1 - (gk > 1)], rest[-(gk > 1):] if gk > 1 else ()
        if gk == 1:
            o_ref[...] = _dot(a_ref[...], b_ref[...], (a_dims, b_dims)).astype(o_ref.dtype)
            return
        (acc_ref,) = scratch
        k = pl.program_id(2)

        @pl.when(k == 0)
        def _():
            acc_ref[...] = jnp.zeros_like(acc_ref)

        acc_ref[...] += _dot(a_ref[...], b_ref[...], (a_dims, b_dims))

        @pl.when(k == gk - 1)
        def _():
            o_ref[...] = acc_ref[...].astype(o_ref.dtype)

    est = 2 * (tm * tk * a.dtype.itemsize + tk * tn * b.dtype.itemsize + tm * tn * jnp.dtype(out_dtype).itemsize)
    est += tm * tn * 4 + 2 * (tm * tk + tk * tn) * 2
    return pl.pallas_call(
        body,
        name=name,
        grid=(gm, gn, gk),
        in_specs=[a_spec, b_spec] + ([ANY] if buf is not None else []),
        out_specs=o_spec,
        out_shape=jax.ShapeDtypeStruct((m, n) if buf is None else buf.shape, out_dtype),
        input_output_aliases={2: 0} if buf is not None else {},
        scratch_shapes=[pltpu.VMEM((tm, tn), F32)] if gk > 1 else [],
        compiler_params=pltpu.CompilerParams(
            dimension_semantics=("parallel", "parallel", "arbitrary"), vmem_limit_bytes=_vmem(est)
        ),
    )(*((a, b) if buf is None else (a, b, buf)))


def _resident(shape):
    return pl.BlockSpec(shape, lambda i: (0,) * len(shape), pipeline_mode=pl.Buffered(1))


def _row_tile(m, fixed_bytes, bytes_per_row, budget=50 * 2**20):
    fits = [tm for tm in _divisors(m, LANES) if fixed_bytes + tm * bytes_per_row <= budget]
    return max(fits) if fits else _divisors(m, LANES)[0]


def _mm_parts_rows(parts, fn, rows, vecs, outs, reds, *, name, separate=False):
    parts = [(p[0], p[1], len(p) > 2) for p in parts]
    rows = [r if isinstance(r, tuple) else (r, r.shape[1], 0) for r in rows]
    m, n = parts[0][0].shape[0], parts[0][1].shape[0 if parts[0][2] else 1]
    npart, nr, nv, no = len(parts), len(rows), len(vecs), len(outs)
    row_bytes = sum(w * r.dtype.itemsize for r, w, _ in rows) + sum(w * jnp.dtype(dt).itemsize for (w, dt) in outs)
    a_row_bytes = sum(a.shape[1] * a.dtype.itemsize for a, _, _ in parts)
    fixed = sum(_nbytes(b.shape, b.dtype) for _, b, _ in parts)
    per_row = 2 * (a_row_bytes + row_bytes) + n * 4 * 5
    tm = _row_tile(m, fixed, per_row)
    sub = min(tm, 2 * LANES)

    def body(*refs):
        ab = refs[:2 * npart]
        row_refs, vec_refs = refs[2 * npart:2 * npart + nr], refs[2 * npart + nr:2 * npart + nr + nv]
        out_refs = refs[2 * npart + nr + nv:2 * npart + nr + nv + no]
        red_refs = refs[2 * npart + nr + nv + no:]
        if red_refs:
            @pl.when(pl.program_id(0) == 0)
            def _():
                for ref in red_refs:
                    ref[...] = jnp.zeros_like(ref)

        vecs_v = [v[...] for v in vec_refs]
        for s0 in range(0, tm, sub):
            sl = slice(s0, s0 + sub)
            prods = [(_dot_nt if nt else _dot)(ab[2 * p][sl, :], ab[2 * p + 1][...]) for p, (_, _, nt) in enumerate(parts)]
            r = prods if separate else functools.reduce(lambda u, v: u + v, prods)
            o, rd = fn(r, [x[sl, :].astype(F32) for x in row_refs], vecs_v)
            for ref, val in zip(out_refs, o):
                ref[sl, :] = val.astype(ref.dtype)
            for ref, val in zip(red_refs, rd):
                ref[...] += val

    in_specs, operands = [], []
    for a, b, _ in parts:
        in_specs += [pl.BlockSpec((tm, a.shape[1]), lambda i: (i, 0)), _resident(b.shape)]
        operands += [a, b]
    in_specs += [pl.BlockSpec((tm, w), functools.partial(lambda cb, i: (i, cb), cb)) for _, w, cb in rows]
    in_specs += [pl.BlockSpec(v.shape, functools.partial(lambda nd, i: (0,) * nd, v.ndim)) for v in vecs]
    out_specs = [pl.BlockSpec((tm, w), lambda i: (i, 0)) for (w, _) in outs]
    out_specs += [pl.BlockSpec((1, w), lambda i: (0, 0)) for w in reds]
    out_shape = [jax.ShapeDtypeStruct((m, w), dt) for (w, dt) in outs] + [jax.ShapeDtypeStruct((1, w), F32) for w in reds]
    return pl.pallas_call(
        body,
        name=name,
        grid=(m // tm,),
        in_specs=in_specs,
        out_specs=out_specs,
        out_shape=out_shape,
        compiler_params=pltpu.CompilerParams(dimension_semantics=("arbitrary",), vmem_limit_bytes=_vmem(fixed + tm * per_row)),
    )(*operands, *[r for r, _, _ in rows], *vecs)


def _mm_ew(a, bs, fn, rows, outs, *, name):
    m, kdim = a.shape
    n = bs[0].shape[0]
    nb, nr, no = len(bs), len(rows), len(outs)
    cw = _pick(n, (2 * LANES, LANES))
    fixed = nb * n * kdim * bs[0].dtype.itemsize
    per_row = 2 * (kdim * a.dtype.itemsize + n * (sum(r.dtype.itemsize for r in rows) + sum(jnp.dtype(dt).itemsize for dt in outs)))
    per_row += cw * 4 * 4 * (nb + 4)
    tm = _row_tile(m, fixed, per_row)

    def body(*refs):
        a_ref, b_refs = refs[0], refs[1:1 + nb]
        row_refs, out_refs = refs[1 + nb:1 + nb + nr], refs[1 + nb + nr:]
        av = a_ref[...]
        for c0 in range(0, n, cw):
            cols = slice(c0, c0 + cw)
            o = fn([_dot_nt(av, b[cols, :]) for b in b_refs], [x[:, cols].astype(F32) for x in row_refs])
            for ref, val in zip(out_refs, o):
                ref[:, cols] = val.astype(ref.dtype)

    tile = pl.BlockSpec((tm, n), lambda i: (i, 0))
    return pl.pallas_call(
        body,
        name=name,
        grid=(m // tm,),
        in_specs=[pl.BlockSpec((tm, kdim), lambda i: (i, 0))] + [_resident((n, kdim))] * nb + [tile] * nr,
        out_specs=[tile] * no,
        out_shape=[jax.ShapeDtypeStruct((m, n), dt) for dt in outs],
        compiler_params=pltpu.CompilerParams(dimension_semantics=("parallel",), vmem_limit_bytes=_vmem(fixed + tm * per_row)),
    )(a, *bs, *rows)


def _rowmap(fn, rows, vecs, outs, reds, *, name, tm=None):
    norm = []
    for r in rows:
        if not isinstance(r, tuple):
            norm.append((r, r.shape[1], 0, None))
        elif len(r) == 2:
            norm.append((r[0], r[0].shape[2], 0, r[1]))
        else:
            norm.append((r[0], r[1], r[2], None))
    rows = norm
    t = rows[0][0].shape[-2]
    if tm is None:
        per_row = 2 * sum(w * a.dtype.itemsize for (a, w, _, _) in rows) + 2 * sum(w * jnp.dtype(d).itemsize for (w, d) in outs)
        per_row += 3 * 4 * max([w for (_, w, _, _) in rows] + [w for (w, _) in outs])
        tm = max(8, min(1024, (40 * 2**20) // per_row))
    tm = _pick(t, tuple(c for c in (1024, 512, 256, 128, 64, 32, 16, 8) if c <= tm))
    nr, nv, no = len(rows), len(vecs), len(outs)

    def body(*refs):
        row_refs, vec_refs = refs[:nr], refs[nr:nr + nv]
        out_refs, red_refs = refs[nr + nv:nr + nv + no], refs[nr + nv + no:]
        o, rd = fn([r[...].astype(F32) for r in row_refs], [v[...] for v in vec_refs])
        for ref, val in zip(out_refs, o):
            ref[...] = val.astype(ref.dtype)
        if red_refs:
            @pl.when(pl.program_id(0) == 0)
            def _():
                for ref in red_refs:
                    ref[...] = jnp.zeros_like(ref)

            for ref, val in zip(red_refs, rd):
                ref[...] += val

    in_specs = [pl.BlockSpec((tm, w), functools.partial(lambda cb, i: (i, cb), cb)) if lead is None
                else pl.BlockSpec((None, tm, w), functools.partial(lambda ld, i: (ld, i, 0), lead)) for (_, w, cb, lead) in rows]
    in_specs += [pl.BlockSpec(v.shape, functools.partial(lambda nd, i: (0,) * nd, v.ndim)) for v in vecs]
    out_specs = [pl.BlockSpec((tm, w), lambda i: (i, 0)) for (w, _) in outs]
    out_specs += [pl.BlockSpec((1, w), lambda i: (0, 0)) for w in reds]
    out_shape = [jax.ShapeDtypeStruct((t, w), d) for (w, d) in outs]
    out_shape += [jax.ShapeDtypeStruct((1, w), F32) for w in reds]
    est = 2 * sum(tm * w * a.dtype.itemsize for (a, w, _, _) in rows) + 2 * sum(_nbytes(v.shape, v.dtype) for v in vecs)
    est += 2 * sum(tm * w * jnp.dtype(d).itemsize for (w, d) in outs)
    est += 6 * tm * max([w for (_, w, _, _) in rows] + [w for (w, _) in outs]) * 4
    return pl.pallas_call(
        body,
        name=name,
        grid=(t // tm,),
        in_specs=in_specs,
        out_specs=out_specs,
        out_shape=out_shape,
        compiler_params=pltpu.CompilerParams(dimension_semantics=("arbitrary",), vmem_limit_bytes=_vmem(est)),
    )(*[r[0] for r in rows], *vecs)


def _colsum(v):
    return jnp.sum(v, axis=0, keepdims=True)


def _head_ones():
    i = np.arange(LANES)
    return jnp.asarray((i[:, None] // HEAD_DIM) == (i[None, :] // HEAD_DIM), MXU_DTYPE)


def _headsum(y, ones):
    parts = []
    for j in range(y.shape[1] // LANES):
        c = y[:, j * LANES:(j + 1) * LANES]
        hi = c.astype(MXU_DTYPE)
        lo = c - hi.astype(F32)
        parts.append(_dot(hi, ones) + _dot(lo, ones))
    return parts[0] if len(parts) == 1 else jnp.concatenate(parts, axis=1)


def _rot(y, c, s_lo, s_hi):
    parts = []
    for j in range(y.shape[1] // LANES):
        yc = y[:, j * LANES:(j + 1) * LANES]
        parts.append(yc * c + pltpu.roll(yc, LANES - ROT_DIM // 2, 1) * s_lo + pltpu.roll(yc, ROT_DIM // 2, 1) * s_hi)
    return parts[0] if len(parts) == 1 else jnp.concatenate(parts, axis=1)


def _rot_tables(t):
    half = ROT_DIM // 2
    inv_freq = ROPE_THETA ** (-(jnp.arange(half, dtype=F32) * 2.0) / ROT_DIM)
    ang = jnp.arange(t).astype(F32)[:, None] * inv_freq[None, :]
    cos, sin = jnp.cos(ang), jnp.sin(ang)
    z = lambda w: jnp.zeros((t, w), F32)
    c = jnp.concatenate([cos, cos, jnp.ones((t, HEAD_DIM - ROT_DIM), F32)], axis=1)
    s_lo = jnp.concatenate([-sin, z(HEAD_DIM - half)], axis=1)
    s_hi = jnp.concatenate([z(half), sin, z(HEAD_DIM - ROT_DIM)], axis=1)
    return [jnp.tile(a, (1, LANES // HEAD_DIM)) for a in (c, s_lo, s_hi)]


def _rms(x):
    return lax.rsqrt(jnp.mean(x * x, axis=-1, keepdims=True) + EPS)


def _window(kind, n, bq, t, seg):
    if kind == "na":
        rows = t // GRID_W
        rs = jnp.clip(n - NA_KH // 2, 0, rows - NA_KH)
        return rs
    nk = bq + 2 * DIL_HALF
    return jnp.clip(n * bq - DIL_HALF, 0, t - nk)


def _dil_mask(n, bq, nk, ws, seg):
    qi = n * bq + lax.broadcasted_iota(jnp.int32, (bq, nk), 0)
    ki = ws + lax.broadcasted_iota(jnp.int32, (bq, nk), 1)
    shift = int(np.log2(seg))
    return (jnp.abs(ki - qi) <= DIL_HALF) & ((ki >> shift) == (qi >> shift))


HS = 4
QW = HS * HEAD_DIM


def _head_of_lane(width=QW):
    return lax.broadcasted_iota(jnp.int32, (1, width), 1) // HEAD_DIM


def _stack_heads(a):
    head = _head_of_lane()
    return jnp.concatenate([jnp.where(head == e, a, jnp.zeros_like(a)) for e in range(HS)], axis=0)


def _unstack_heads(a, bq):
    head = _head_of_lane()
    out = jnp.zeros((bq, QW), a.dtype)
    for e in range(HS):
        out = jnp.where(head == e, a[e * bq:(e + 1) * bq], out)
    return out


def _stack_cols(blk, bq):
    head = _head_of_lane()
    return jnp.concatenate(
        [jnp.max(jnp.where(head == e, blk, -jnp.inf), axis=1, keepdims=True) for e in range(HS)], axis=0)


def _attn_geometry(kind, t):
    if kind == "na":
        return GRID_W, NA_KH * GRID_W, min(32, t // GRID_W)
    bq = 128
    return bq, bq + 2 * DIL_HALF, min(16, t // bq)


def _attn_scores(kind, n, bq, nk, t, seg, qs, k_ref, b_ref):
    scale = HEAD_DIM ** -0.5
    if kind == "na":
        rs = _window(kind, n, bq, t, seg)
        ws = pl.multiple_of(rs * GRID_W, GRID_W)
        ro0 = rs - n + (NA_KH - 1)
        s = _dot_nt(qs, k_ref[pl.ds(ws, nk), :]) * scale
        s = s + jnp.concatenate(
            [jnp.concatenate([b_ref[e, ro0 + 2 * i] for i in range(NA_KH // 2)], axis=1) for e in range(HS)], axis=0)
        return s, ws, ro0
    ws = pl.multiple_of(_window(kind, n, bq, t, seg), DIL_HALF)
    mask = _dil_mask(n, bq, nk, ws, seg)
    s = _dot_nt(qs, k_ref[pl.ds(ws, nk), :]) * scale
    s = jnp.where(jnp.concatenate([mask] * HS, axis=0), s, NEG_INF)
    return s, ws, None


def _col_operands(*ops):
    pairs = [op if isinstance(op, tuple) else (op, 0) for op in ops]
    width = QW if isinstance(ops[0], tuple) else ops[0].shape[1]
    return (*pairs, width)


def _q_block(rows, col):
    return pl.BlockSpec((rows, QW), lambda j, n: (n, j + col))


def _kv_resident(t, col):
    return pl.BlockSpec((t, QW), lambda j, n: (0, j + col))


def _attn_fwd(q, k, v, *, kind, name, bias=None, seg=None):
    (q, cq), (k, ck), (v, cv), w = _col_operands(q, k, v)
    t = q.shape[0]
    quads = w // QW
    bq, nk, sub = _attn_geometry(kind, t)
    nq = t // (bq * sub)

    def body(*refs):
        if kind == "na":
            q_ref, k_ref, v_ref, b_ref, o_ref, l_ref = refs
        else:
            (q_ref, k_ref, v_ref, o_ref, l_ref), b_ref = refs, None
        for i in range(sub):
            n = pl.program_id(1) * sub + i
            rows = slice(i * bq, (i + 1) * bq)
            s, ws, _ = _attn_scores(kind, n, bq, nk, t, seg, _stack_heads(q_ref[rows, :]), k_ref, b_ref)
            m = jnp.max(s, axis=1, keepdims=True)
            p = jnp.exp(s - m)
            l = jnp.sum(p, axis=1, keepdims=True)
            o_ref[rows, :] = _unstack_heads(_dot(p / l, v_ref[pl.ds(ws, nk), :]), bq)
            l_ref[rows, :] = _unstack_heads(jnp.broadcast_to(m + jnp.log(l), (HS * bq, QW)), bq)

    blk = pl.BlockSpec((bq * sub, QW), lambda j, n: (n, j))
    in_specs = [_q_block(bq * sub, cq), _kv_resident(t, ck), _kv_resident(t, cv)]
    operands = [q, k, v]
    est = 4 * t * QW * q.dtype.itemsize + 12 * sub * HS * bq * nk * 4
    if kind == "na":
        in_specs.append(pl.BlockSpec((HS,) + bias.shape[1:], lambda j, n: (j, 0, 0, 0)))
        operands.append(bias)
        est += 2 * _nbytes((HS,) + bias.shape[1:], F32)
    return pl.pallas_call(
        body,
        name=name,
        grid=(quads, nq),
        in_specs=in_specs,
        out_specs=[blk, blk],
        out_shape=[jax.ShapeDtypeStruct((t, w), F32)] * 2,
        compiler_params=pltpu.CompilerParams(dimension_semantics=("arbitrary", "arbitrary"), vmem_limit_bytes=_vmem(est)),
    )(*operands)


def _attn_bwd(q, k, v, do, dterm, lse, *, kind, name, bias=None, seg=None):
    (q, cq), (k, ck), (v, cv), (do, cdo), (dterm, cdt), (lse, cl), w = _col_operands(q, k, v, do, dterm, lse)
    t = q.shape[0]
    quads = w // QW
    bq, nk, sub = _attn_geometry(kind, t)
    nq = t // (bq * sub)
    scale = HEAD_DIM ** -0.5

    def body(*refs):
        if kind == "na":
            q_ref, k_ref, v_ref, do_ref, dt_ref, l_ref, b_ref, dq_ref, dk_hbm, dv_hbm, db_ref, dk_acc, dv_acc, sem = refs
        else:
            q_ref, k_ref, v_ref, do_ref, dt_ref, l_ref, dq_ref, dk_hbm, dv_hbm, dk_acc, dv_acc, sem = refs
            b_ref = None
        j, step = pl.program_id(0), pl.program_id(1)

        @pl.when(step == 0)
        def _():
            dk_acc[...] = jnp.zeros_like(dk_acc)
            dv_acc[...] = jnp.zeros_like(dv_acc)
            if kind == "na":
                db_ref[...] = jnp.zeros_like(db_ref)

        for b in range(sub):
            n = step * sub + b
            rows = slice(b * bq, (b + 1) * bq)
            qs = _stack_heads(q_ref[rows, :])
            dos = _stack_heads(do_ref[rows, :])
            s, ws, ro0 = _attn_scores(kind, n, bq, nk, t, seg, qs, k_ref, b_ref)
            p = jnp.exp(s - _stack_cols(l_ref[rows, :], bq))
            dp = _dot_nt(dos, v_ref[pl.ds(ws, nk), :])
            ds = p * (dp - _stack_cols(dt_ref[rows, :], bq))
            if kind == "na":
                for e in range(HS):
                    for i in range(NA_KH // 2):
                        db_ref[e, ro0 + 2 * i] += ds[e * bq:(e + 1) * bq, i * LANES:(i + 1) * LANES]
            dsc = ds * scale
            dq_ref[rows, :] = _unstack_heads(_dot(dsc, k_ref[pl.ds(ws, nk), :]), bq).astype(dq_ref.dtype)
            dk_acc[pl.ds(ws, nk), :] += _dot_tn(dsc, qs)
            dv_acc[pl.ds(ws, nk), :] += _dot_tn(p, dos)

        @pl.when(step == nq - 1)
        def _():
            ck = pltpu.make_async_copy(dk_acc, dk_hbm.at[j], sem.at[0])
            cv = pltpu.make_async_copy(dv_acc, dv_hbm.at[j], sem.at[1])
            ck.start()
            cv.start()
            ck.wait()
            cv.wait()

    blk = pl.BlockSpec((bq * sub, QW), lambda j, n: (n, j))
    in_specs = [_q_block(bq * sub, cq), _kv_resident(t, ck), _kv_resident(t, cv)] + [_q_block(bq * sub, c) for c in (cdo, cdt, cl)]
    operands = [q, k, v, do, dterm, lse]
    out_specs = [blk, ANY, ANY]
    out_shape = [jax.ShapeDtypeStruct((t, w), ACT_DTYPE)] + [jax.ShapeDtypeStruct((quads, t, QW), F32)] * 2
    est = 4 * t * QW * q.dtype.itemsize + 2 * t * QW * 4 + 16 * sub * HS * bq * nk * 4
    if kind == "na":
        bspec = pl.BlockSpec((HS,) + bias.shape[1:], lambda j, n: (j, 0, 0, 0))
        in_specs.append(bspec)
        operands.append(bias)
        out_specs.append(bspec)
        out_shape.append(jax.ShapeDtypeStruct(bias.shape, F32))
        est += 4 * _nbytes((HS,) + bias.shape[1:], F32)
    res_ = pl.pallas_call(
        body,
        name=name,
        grid=(quads, nq),
        in_specs=in_specs,
        out_specs=out_specs,
        out_shape=out_shape,
        scratch_shapes=[pltpu.VMEM((t, QW), F32), pltpu.VMEM((t, QW), F32), pltpu.SemaphoreType.DMA((2,))],
        compiler_params=pltpu.CompilerParams(dimension_semantics=("arbitrary", "arbitrary"), vmem_limit_bytes=_vmem(est)),
    )(*operands)
    unquad = lambda a: [(a, i) for i in range(quads)]
    return (res_[0], unquad(res_[1]), unquad(res_[2])) + tuple(res_[3:])


def _na_onehot():
    qc = np.arange(GRID_W)[:, None]
    kc = np.arange(GRID_W)[None, :]
    start = np.clip(qc - NA_KW // 2, 0, GRID_W - NA_KW)
    inwin = (kc >= start) & (kc < start + NA_KW)
    off = kc - qc + (NA_KW - 1)
    e_mat = np.zeros((2, 32, GRID_W, 2, GRID_W), np.float32)
    for e in range(2):
        for c in range(2 * NA_KW - 1):
            e_mat[e, c, :, e, :] = (off == c) & inwin
    neg = np.where(inwin, 0.0, NEG_INF).astype(np.float32)
    neg = np.broadcast_to(neg[:, None, :], (GRID_W, 2, GRID_W)).reshape(1, GRID_W * LANES)
    return jnp.asarray(e_mat.reshape(64, GRID_W * LANES), MXU_DTYPE), jnp.asarray(neg)


def _na_rowpairs(rpb):
    p = jnp.pad(rpb, ((0, 0), (0, 0), (0, 1)))
    return jnp.concatenate([p[:, :-1], p[:, 1:]], axis=-1).reshape(NA_HEADS * (2 * NA_KH - 2), 64)


def _na_bias_table(rpb):
    r2 = _na_rowpairs(rpb)
    e_mat, neg = _na_onehot()

    def body(r_ref, e_ref, n_ref, o_ref):
        hi, mid, lo = _split3(r_ref[...])
        e = e_ref[...]
        o_ref[...] = _dot(hi, e) + _dot(mid, e) + _dot(lo, e) + n_ref[...]

    out = pl.pallas_call(
        body,
        name="na_bias_table",
        out_shape=jax.ShapeDtypeStruct((r2.shape[0], GRID_W * LANES), F32),
        compiler_params=pltpu.CompilerParams(vmem_limit_bytes=_vmem(6 * r2.shape[0] * GRID_W * LANES * 4)),
    )(r2, e_mat, neg)
    return out.reshape(NA_HEADS, 2 * NA_KH - 2, GRID_W, LANES)


def _na_bias_grad(dbt):
    e_mat, _ = _na_onehot()
    flat = dbt.reshape(NA_HEADS * (2 * NA_KH - 2), GRID_W * LANES)

    def body(d_ref, e_ref, o_ref):
        hi, mid, lo = _split3(d_ref[...])
        e = e_ref[...]
        o_ref[...] = _dot_nt(hi, e) + _dot_nt(mid, e) + _dot_nt(lo, e)

    g = pl.pallas_call(
        body,
        name="na_bias_grad",
        out_shape=jax.ShapeDtypeStruct((flat.shape[0], 64), F32),
        compiler_params=pltpu.CompilerParams(vmem_limit_bytes=_vmem(6 * flat.shape[0] * GRID_W * LANES * 4)),
    )(flat, e_mat)
    g = g.reshape(NA_HEADS, 2 * NA_KH - 2, 2, 32)[..., :2 * NA_KW - 1]
    first = jnp.pad(g[:, :, 0], ((0, 0), (0, 1), (0, 0)))
    second = jnp.pad(g[:, :, 1], ((0, 0), (1, 0), (0, 0)))
    return first + second


def _all_gather(arrs, *, name):
    na = len(arrs)

    def body(*refs):
        ins, outs = refs[:na], refs[na:2 * na]
        send_sems, recv_sems, local_sems = refs[2 * na:]
        x, y, c = lax.axis_index("x"), lax.axis_index("y"), lax.axis_index("c")
        me, sibling = (x, y, c), (x, y, 1 - c)
        chips = [(1 - x, y), (x, 1 - y), (1 - x, 1 - y)]

        def rows(a, px, py, pc):
            r = ins[a].shape[0]
            return outs[a].at[pl.ds((4 * px + 2 * py + pc) * r, r), :]

        def copy(a, k, block, to, src=None):
            return pltpu.make_async_remote_copy(
                src_ref=rows(a, *block) if src is None else src, dst_ref=rows(a, *block),
                send_sem=send_sems.at[a, k], recv_sem=recv_sems.at[a, k], device_id=to, device_id_type=MESH)

        mine = [pltpu.make_async_copy(ins[a], rows(a, *me), local_sems.at[a]) for a in range(na)]
        for cp in mine:
            cp.start()
        first = []
        for a in range(na):
            first.append(copy(a, 0, me, sibling, src=ins[a]))
            first += [copy(a, 1 + j, me, (*chip, c), src=ins[a]) for j, chip in enumerate(chips)]
        for cp in first:
            cp.start()
        passed = []
        for j, chip in enumerate(chips):
            for a in range(na):
                copy(a, 1 + j, (*chip, c), me).wait_recv()
                cp = copy(a, 4 + j, (*chip, c), sibling)
                cp.start()
                passed.append(cp)
        for a in range(na):
            copy(a, 0, sibling, me).wait_recv()
        for j, chip in enumerate(chips):
            for a in range(na):
                copy(a, 4 + j, (*chip, 1 - c), me).wait_recv()
        for cp in first + passed:
            cp.wait_send()
        for cp in mine:
            cp.wait()

    return pl.pallas_call(
        body,
        name=name,
        in_specs=[ANY] * na,
        out_specs=[ANY] * na,
        out_shape=[jax.ShapeDtypeStruct((N_DEV * a.shape[0], a.shape[1]), a.dtype) for a in arrs],
        scratch_shapes=[pltpu.SemaphoreType.DMA((na, 7)), pltpu.SemaphoreType.DMA((na, 7)), pltpu.SemaphoreType.DMA((na,))],
    )(*arrs)


HBM = pl.BlockSpec(memory_space=pltpu.HBM)
SEM = pl.BlockSpec(memory_space=pltpu.SEMAPHORE)
EFFECT = pltpu.SideEffectType.DATAFLOW_SIDE_EFFECTING


def _peer_of(k):
    x, y, c = lax.axis_index("x"), lax.axis_index("y"), lax.axis_index("c")
    return x ^ ((k >> 2) & 1), y ^ ((k >> 1) & 1), c ^ (k & 1)


def _split_copies(gather, src_ref, land_ref, send_sems, recv_sems):
    x, y, c = lax.axis_index("x"), lax.axis_index("y"), lax.axis_index("c")
    my = 4 * x + 2 * y + c
    r = src_ref.shape[0] if gather else src_ref.shape[0] // N_DEV
    copies = []
    for k in ((1, 2, 4, 6) if gather == "chip" else range(1, N_DEV)):
        px, py, pc = _peer_of(k)
        if gather:
            src, dst = src_ref, land_ref.at[pl.ds(my * r, r), :]
        else:
            src, dst = src_ref.at[pl.ds((4 * px + 2 * py + pc) * r, r), :], land_ref.at[k - 1]
        copies.append(pltpu.make_async_remote_copy(
            src_ref=src, dst_ref=dst, send_sem=send_sems.at[k - 1], recv_sem=recv_sems.at[k - 1],
            device_id=(px, py, pc), device_id_type=MESH))
    return copies


def _split_start(srcs, lands, *, gather, name, after=None):
    na = len(srcs)
    extra = [] if after is None else [after]

    def body(*refs):
        src_refs, land_refs = refs[:na], refs[na:2 * na]
        outs = refs[2 * na + len(extra):]
        for a in range(na):
            for cp in _split_copies(gather, src_refs[a], land_refs[a], outs[4 * a], outs[4 * a + 1]):
                cp.start()
        outs[4 * na][...] = jnp.zeros_like(outs[4 * na])

    out_shape, out_specs, aliases = [], [], {}
    for a in range(na):
        out_shape += [pltpu.SemaphoreType.DMA((N_DEV - 1,)), pltpu.SemaphoreType.DMA((N_DEV - 1,)),
                      pltpu.HBM(srcs[a].shape, srcs[a].dtype), pltpu.HBM(lands[a].shape, lands[a].dtype)]
        out_specs += [SEM, SEM, HBM, HBM]
        aliases[a] = 4 * a + 2
        aliases[na + a] = 4 * a + 3
    out_shape.append(jax.ShapeDtypeStruct((8, LANES), F32))
    out_specs.append(pl.BlockSpec(memory_space=pltpu.VMEM))
    res = pl.pallas_call(
        body,
        name=name,
        out_shape=tuple(out_shape),
        in_specs=[HBM] * (2 * na) + [ANY] * len(extra),
        out_specs=tuple(out_specs),
        input_output_aliases=aliases,
        compiler_params=pltpu.CompilerParams(has_side_effects=EFFECT),
    )(*[pltpu.with_memory_space_constraint(a, pltpu.HBM) for a in list(srcs) + list(lands)], *extra)
    return [tuple(res[4 * a:4 * a + 4]) for a in range(na)], res[4 * na][0, 0]


def _split_wait(handles, after, *, gather, name):
    na = len(handles)

    def body(*refs):
        src_refs, land_refs = refs[:na], refs[na:2 * na]
        sems = refs[2 * na:4 * na]
        for a in range(na):
            for cp in _split_copies(gather, src_refs[a], land_refs[a], sems[2 * a], sems[2 * a + 1]):
                cp.wait_send()
                cp.wait_recv()

    srcs = [h[2] for h in handles]
    lands = [h[3] for h in handles]
    sems = [s for h in handles for s in h[:2]]
    res = pl.pallas_call(
        body,
        name=name,
        out_shape=tuple(pltpu.HBM(a.shape, a.dtype) for a in srcs + lands),
        in_specs=[HBM] * (2 * na) + [SEM] * (2 * na) + [ANY],
        out_specs=tuple([HBM] * (2 * na)),
        input_output_aliases={i: i for i in range(2 * na)},
        compiler_params=pltpu.CompilerParams(has_side_effects=EFFECT),
    )(*srcs, *lands, *sems, after)
    return list(res[:na]), list(res[na:])


def _forward_copies(land_ref, send_sems, recv_sems):
    x, y, c = lax.axis_index("x"), lax.axis_index("y"), lax.axis_index("c")
    r = land_ref.shape[0] // N_DEV
    copies = []
    for j, k in enumerate((2, 4, 6)):
        px, py, pc = _peer_of(k)
        rows = land_ref.at[pl.ds((4 * px + 2 * py + pc) * r, r), :]
        copies.append(pltpu.make_async_remote_copy(
            src_ref=rows, dst_ref=rows, send_sem=send_sems.at[j], recv_sem=recv_sems.at[j],
            device_id=(x, y, 1 - c), device_id_type=MESH))
    return copies


def _forward_start(land, *, name):
    def body(land_ref, send_sems, recv_sems, land_thru, token):
        for cp in _forward_copies(land_ref, send_sems, recv_sems):
            cp.start()
        token[...] = jnp.zeros_like(token)

    res = pl.pallas_call(
        body,
        name=name,
        out_shape=(pltpu.SemaphoreType.DMA((3,)), pltpu.SemaphoreType.DMA((3,)), pltpu.HBM(land.shape, land.dtype),
                   jax.ShapeDtypeStruct((8, LANES), F32)),
        in_specs=[HBM],
        out_specs=(SEM, SEM, HBM, pl.BlockSpec(memory_space=pltpu.VMEM)),
        input_output_aliases={0: 2},
        compiler_params=pltpu.CompilerParams(has_side_effects=EFFECT),
    )(pltpu.with_memory_space_constraint(land, pltpu.HBM))
    return res[:3]


def _forward_wait(handle, *, name):
    send_sems, recv_sems, land = handle

    def body(land_ref, send_ref, recv_ref, land_out):
        for cp in _forward_copies(land_ref, send_ref, recv_ref):
            cp.wait_send()
            cp.wait_recv()

    return pl.pallas_call(
        body,
        name=name,
        out_shape=pltpu.HBM(land.shape, land.dtype),
        in_specs=[HBM, SEM, SEM],
        out_specs=HBM,
        input_output_aliases={0: 0},
        compiler_params=pltpu.CompilerParams(has_side_effects=EFFECT),
    )(land, send_sems, recv_sems)


def _sum8(own, recv, *, name):
    _, r, w = recv.shape
    fits = lambda c: 2 * c * w * (N_DEV * recv.dtype.itemsize + 4) <= 32 * 2**20
    tr = _pick(r, tuple(c for c in (r // 2, r // 4, 256, 128, 64, 32, 16, 8) if c % 16 == 0 and fits(c)))

    def body(own_ref, a_ref, o_ref):
        acc = own_ref[...].astype(F32)
        for i in range(N_DEV - 1):
            acc = acc + a_ref[i].astype(F32)
        o_ref[...] = acc

    return pl.pallas_call(
        body,
        name=name,
        grid=(r // tr,),
        in_specs=[pl.BlockSpec((tr, w), lambda i: (i, 0)), pl.BlockSpec((N_DEV - 1, tr, w), lambda i: (0, i, 0))],
        out_specs=pl.BlockSpec((tr, w), lambda i: (i, 0)),
        out_shape=jax.ShapeDtypeStruct((r, w), F32),
        compiler_params=pltpu.CompilerParams(dimension_semantics=("parallel",), vmem_limit_bytes=_vmem(4 * N_DEV * tr * w * 4)),
    )(own, recv)


def _adamw(w, g, m, v, *, name):
    def fn(rows, _):
        wv, gv, mv, vv = rows
        m1 = ADAM_B1 * mv + (1.0 - ADAM_B1) * gv
        v1 = ADAM_B2 * vv + (1.0 - ADAM_B2) * jnp.square(gv)
        m_hat = m1 / (1.0 - ADAM_B1 ** ADAM_STEP)
        v_hat = v1 / (1.0 - ADAM_B2 ** ADAM_STEP)
        delta = -ADAM_LR * (m_hat / (jnp.sqrt(v_hat) + ADAM_EPS) + ADAM_WD * wv)
        return [delta, m1, v1], []

    c = w.shape[1]
    return _rowmap(fn, [w, g, m, v], [], [(c, F32)] * 3, [], name=name)


_SMALL = ("b_ada", "g_norm1", "g_norm2", "b_gate", "g_qa", "g_ka", "g_qb", "g_kb", "rpb", "loss")


def _pack_small(parts):
    flat = []
    for nme in _SMALL:
        a = parts[nme].reshape(-1).astype(F32)
        flat.append(jnp.pad(a, (0, (-a.shape[0]) % LANES)))
    flat = jnp.concatenate(flat)
    flat = jnp.pad(flat, (0, (-flat.shape[0]) % (LANES * LANES)))
    return flat.reshape(-1, LANES)


def _unpack_small(packed, shapes):
    flat = packed.reshape(-1)
    out, pos = {}, 0
    for nme in _SMALL:
        n = int(np.prod(shapes[nme]))
        out[nme] = flat[pos:pos + n].reshape(shapes[nme])
        pos += n + (-n) % LANES
    return out


def _to_class(a, d):
    t, w = a.shape
    return a if d == 1 else a.reshape(t // d, d, w).transpose(1, 0, 2).reshape(t, w)


def _from_class(a, d):
    t, w = a.shape
    return a if d == 1 else a.reshape(d, t // d, w).transpose(1, 0, 2).reshape(t, w)


def kernel(x, c, w_ada, b_ada, g_norm1, g_norm2, w_in, b_gate, g_qa, g_ka, g_qb, g_kb, rpb, w_proj_a, w_proj_b, w_o, w_ffn_in, w_ffn_out, loss_target, m_w_ada, m_b_ada, m_g_norm1, m_g_norm2, m_w_in, m_b_gate, m_g_qa, m_g_ka, m_g_qb, m_g_kb, m_rpb, m_w_proj_a, m_w_proj_b, m_w_o, m_w_ffn_in, m_w_ffn_out, v_w_ada, v_b_ada, v_g_norm1, v_g_norm2, v_w_in, v_b_gate, v_g_qa, v_g_ka, v_g_qb, v_g_kb, v_rpb, v_w_proj_a, v_w_proj_b, v_w_o, v_w_ffn_in, v_w_ffn_out):
    t, d = x.shape[1], x.shape[2]
    d_ff = w_ffn_out.shape[1] * N_DEV
    me = 4 * lax.axis_index("x") + 2 * lax.axis_index("y") + lax.axis_index("c")
    xt, tgt = x.reshape(t, d), loss_target.reshape(t, d)
    ones = _head_ones()

    shards = [s.astype(WIRE_DTYPE) for s in (w_in[0].T, w_ffn_in[0].T, w_proj_a[0].T, w_proj_b[0].T, w_o[0], w_ffn_out[0])]
    lands = [lax.dynamic_update_slice(lax.empty((N_DEV * s.shape[0], s.shape[1]), s.dtype), s, (me * s.shape[0], 0))
             for s in shards]

    c_all = _all_gather([jnp.pad(c, ((0, 7), (0, 0)))], name="gather_c")[0][::8]
    c_all = jnp.pad(c_all, ((0, LANES - N_DEV), (0, 0)))

    def mod_body(c_ref, w_ref, b_ref, o_ref, act_ref):
        act = _silu(c_ref[...])
        act_ref[...] = act
        hi, mid, lo = _split3(act)
        w = w_ref[...]
        whi, wmid, wlo = _split3(w)
        acc = _dot(hi, whi) + (_dot(hi, wmid) + _dot(mid, whi)) + (_dot(hi, wlo) + _dot(mid, wmid) + _dot(lo, whi))
        o_ref[...] = acc + b_ref[...]

    ncol = w_ada.shape[2]
    b_ada_mine = lax.dynamic_slice(b_ada, (0, me * ncol), (1, ncol))
    mod_part, c_act = pl.pallas_call(
        mod_body,
        name="ada_mod",
        out_shape=[jax.ShapeDtypeStruct((LANES, ncol), F32), jax.ShapeDtypeStruct((LANES, d), F32)],
        compiler_params=pltpu.CompilerParams(vmem_limit_bytes=_vmem(6 * d * ncol * 4)),
    )(c_all, w_ada[0], b_ada_mine)
    mod_all = _all_gather([mod_part[:N_DEV]], name="gather_mod")[0].reshape(N_DEV, N_DEV, ncol)
    mod = lax.dynamic_index_in_dim(mod_all, me, axis=1, keepdims=False).reshape(6, d)
    sh1, sc1, gt1, sh2, sc2, gt2 = [mod[i:i + 1] for i in range(6)]

    def norm_fwd(rows, vecs):
        (xv,), (g, sc, sh) = rows, vecs
        return [xv * _rms(xv) * g * (1.0 + sc) + sh], []

    w_in_handle, w_token = _split_start(shards[:1], lands[:1], gather="chip", after=mod, name="gather_w_in_start")
    (h,) = _rowmap(norm_fwd, [xt], [g_norm1 + w_token, sc1, sh1], [(d, MXU_DTYPE)], [], name="norm1")
    n_a, n_b = 3 * WA, 3 * WB
    (w_in_t,) = _split_wait(w_in_handle, h, gather="chip", name="gather_w_in_wait")[1]
    w_in_t = _forward_wait(_forward_start(w_in_t, name="gather_w_in_forward_start"), name="gather_w_in_forward_wait")
    w_handles, w_token = _split_start(shards[1:], lands[1:], gather=True, after=w_in_t, name="gather_weights_start")
    w_in_a, w_in_b, w_in_g = w_in_t[:n_a], w_in_t[n_a:n_a + n_b], w_in_t[n_a + n_b:]

    rot_c, rot_lo, rot_hi = _rot_tables(t)
    tile_g = lambda g, heads: jnp.tile(g, (1, heads))

    def qk_fwd(width, rotate):
        def fn(xv, rows, vecs):
            gq, gk, on = vecs
            qkv = []
            for i, g in enumerate((gq, gk)):
                xi = xv[:, i * width:(i + 1) * width]
                r = lax.rsqrt(_headsum(xi * xi, on) * (1.0 / HEAD_DIM) + EPS)
                yi = xi * r * g
                if rotate:
                    yi = _rot(yi, rows[0], rows[1], rows[2])
                qkv.append(yi)
            qkv.append(xv[:, 2 * width:])
            if not rotate:
                return [xv] + qkv, []
            groups = [jnp.concatenate([a[:, g * WB_OUT:(g + 1) * WB_OUT] for a in qkv], axis=1)
                      for g in range(len(DIL_CONFIGS))]
            return [xv] + groups, []
        return fn

    qkv_a, qa, ka, va = _mm_parts_rows(
        [(h, w_in_a, "nt")], qk_fwd(WA, False), [], [tile_g(g_qa, NA_HEADS) + w_token, tile_g(g_ka, NA_HEADS), ones],
        [(3 * WA, ACT_DTYPE)] + [(WA, MXU_DTYPE)] * 3, [], name="proj_a_qknorm")
    qkv_b, *qkv_groups = _mm_parts_rows(
        [(h, w_in_b, "nt")], qk_fwd(WB, True), [rot_c, rot_lo, rot_hi],
        [tile_g(g_qb, DIL_HEADS), tile_g(g_kb, DIL_HEADS), ones],
        [(3 * WB, ACT_DTYPE)] + [(3 * WB_OUT, MXU_DTYPE)] * len(DIL_CONFIGS), [], name="proj_b_qknorm")

    bias_tab = _na_bias_table(rpb[0])
    o_a, lse_a = _attn_fwd(qa, ka, va, kind="na", bias=bias_tab, name="na_fwd")

    grp = []
    for g, (_, dil) in enumerate(DIL_CONFIGS):
        qkv_c = _to_class(qkv_groups[g], dil)
        og, lg = _attn_fwd((qkv_c, 0), (qkv_c, 1), (qkv_c, 2), kind="dil", seg=t // dil, name=f"dil_fwd{g}")
        grp.append(dict(qkv=qkv_c, o=_from_class(og, dil), lse=_from_class(lg, dil), lse_c=lg, dil=dil))

    def merge_fwd(rows, _):
        o0, o1, o2, l0, l1, l2 = rows
        mx = jnp.maximum(jnp.maximum(l0, l1), l2)
        e0, e1, e2 = jnp.exp(l0 - mx), jnp.exp(l1 - mx), jnp.exp(l2 - mx)
        s = e0 + e1 + e2
        return [(e0 / s) * o0 + (e1 / s) * o1 + (e2 / s) * o2], []

    (o_b,) = _rowmap(merge_fwd, [gr["o"] for gr in grp] + [gr["lse"] for gr in grp], [], [(WB_OUT, F32)], [], name="dil_merge")

    w_pa_t, w_pb_t, w_o_f = _split_wait(w_handles[1:4], o_b, gather=True, name="gather_w_out_wait")[1]
    def gate_fwd(prods, _, vecs):
        gv, pav, pbv = prods
        sg = jax.nn.sigmoid(gv + vecs[0])
        return [gv, pav, pbv, sg[:, :d] * pav + sg[:, d:] * pbv], []

    gates, pa, pb, merged = _mm_parts_rows(
        [(h, w_in_g, "nt"), (o_a, w_pa_t, "nt"), (o_b, w_pb_t, "nt")], gate_fwd, [], [b_gate],
        [(2 * d, ACT_DTYPE), (d, ACT_DTYPE), (d, ACT_DTYPE), (d, MXU_DTYPE)], [], separate=True, name="proj_gates_out_merge")
    def resid_norm(av, rows, vecs):
        (xv,), (gt, g, sc, sh) = rows, vecs
        x1v = xv + gt * av
        return [av, x1v, x1v * _rms(x1v) * g * (1.0 + sc) + sh], []

    att, x1, h2 = _mm_parts_rows([(merged, w_o_f)], resid_norm, [xt], [gt1, g_norm2, sc2, sh2],
                           [(d, F32), (d, F32), (d, MXU_DTYPE)], [], name="proj_o_resid_norm2")

    w_ffn_in_t, w_ffn_out_f = _split_wait([w_handles[0], w_handles[4]], h2, gather=True, name="gather_w_ffn_wait")[1]
    w_ffn_a, w_ffn_up = w_ffn_in_t[:d_ff], w_ffn_in_t[d_ff:]

    def swiglu_fwd(prods, _):
        a, up = prods
        return [a, up, _silu(a) * up]

    ua, uu, f = _mm_ew(h2, [w_ffn_a, w_ffn_up], swiglu_fwd, [], [ACT_DTYPE, ACT_DTYPE, MXU_DTYPE], name="ffn_in_swiglu")

    def loss_fn(yv, rows, vecs):
        (x1v, tv), gt = rows, vecs[0]
        err = x1v + gt * yv - tv
        dout = err * (1.0 / d)
        return [dout, dout * gt], [_colsum(err * err), _colsum(dout * yv)]

    dout, dy2, err2, dgt2 = _mm_parts_rows([(f, w_ffn_out_f)], loss_fn, [x1, tgt], [gt2], [(d, F32), (d, MXU_DTYPE)],
                                           [d, d], name="ffn_out_loss")

    dw_ffn_out = _mm(f, dy2, ta=True, out_dtype=WIRE_DTYPE, name="wgrad_ffn_out")
    def swiglu_bwd(prods, rows):
        (dfv,), (a, up) = prods, rows
        sg = jax.nn.sigmoid(a)
        return [dfv * up * (sg * (1.0 + a * (1.0 - sg))), dfv * (a * sg)]

    da, dup = _mm_ew(dy2, [w_ffn_out_f], swiglu_bwd, [ua, uu], [MXU_DTYPE, MXU_DTYPE], name="dgrad_ffn_out_swiglu_bwd")
    dw_ffn_in_t = _mm(da, h2, ta=True, into=(lax.empty((2 * d_ff, d), WIRE_DTYPE), 0), name="wgrad_ffn_in_a")
    dw_ffn_in_t = _mm(dup, h2, ta=True, into=(dw_ffn_in_t, d_ff), name="wgrad_ffn_in_up")
    land7 = lambda a: lax.empty((N_DEV - 1, a.shape[0] // N_DEV, a.shape[1]), a.dtype)
    own_block = lambda a: lax.dynamic_slice(a, (me * (a.shape[0] // N_DEV), 0), (a.shape[0] // N_DEV, a.shape[1]))
    g_ffn = [dw_ffn_in_t, dw_ffn_out]
    h_ffn, tok_ffn = _split_start(g_ffn, [land7(a) for a in g_ffn], gather=False, name="exchange_ffn_start")
    def norm_bwd(dh, xv, g, sc):
        r = _rms(xv)
        xh = xv * r
        dxh = dh * g * (1.0 + sc)
        dxv = r * (dxh - xh * jnp.mean(dxh * xh, axis=-1, keepdims=True))
        return dxv, [_colsum(dh), _colsum(dh * xh * g), _colsum(dh * xh * (1.0 + sc))]

    def norm2_bwd(dhv, rows, vecs):
        (x1v, dov, av), (g, sc, gt) = rows, vecs
        dxv, sums = norm_bwd(dhv, x1v, g, sc)
        dx1v = dov + dxv
        return [dx1v, dx1v * gt], sums + [_colsum(dx1v * av)]

    dx1, datt, dsh2, dsc2, dg2, dgt1 = _mm_parts_rows(
        [(da, w_ffn_a), (dup, w_ffn_up)], norm2_bwd, [x1, dout, att], [g_norm2 + tok_ffn, sc2, gt1],
        [(d, F32), (d, MXU_DTYPE)], [d] * 4, name="dgrad_ffn_in_norm2_bwd")
    dw_o = _mm(merged, datt, ta=True, out_dtype=WIRE_DTYPE, name="wgrad_o")
    def gate_bwd(dm, rows, vecs):
        gv, pav, pbv = rows
        sg = jax.nn.sigmoid(gv + vecs[0])
        ga, gb = sg[:, :d], sg[:, d:]
        dgp = jnp.concatenate([dm * pav * ga * (1.0 - ga), dm * pbv * gb * (1.0 - gb)], axis=1)
        return [dm * ga, dm * gb, dgp], [_colsum(dgp)]

    dpa, dpb, dgates, db_gate = _mm_parts_rows(
        [(datt, w_o_f.T)], gate_bwd, [gates, pa, pb], [b_gate],
        [(d, MXU_DTYPE), (d, MXU_DTYPE), (2 * d, MXU_DTYPE)], [2 * d], name="dgrad_o_gate_bwd")
    dw_pa_t = _mm(dpa, o_a, ta=True, out_dtype=WIRE_DTYPE, name="wgrad_proj_a")
    dw_pb_t = _mm(dpb, o_b, ta=True, out_dtype=WIRE_DTYPE, name="wgrad_proj_b")
    g_out = [dw_pa_t, dw_pb_t, dw_o]
    h_out, tok_out = _split_start(g_out, [land7(a) for a in g_out], gather=False, name="exchange_out_start")
    def delta_a(doa, rows, vecs):
        return [doa, _headsum(doa * rows[0], vecs[0])], []

    do_a, dterm_a = _mm_parts_rows([(dpa, w_pa_t)], delta_a, [o_a], [ones + tok_out.astype(ones.dtype)],
                                   [(WA, F32), (WA, F32)], [], name="dgrad_proj_a_delta")
    dqa, dka, dva, dbias = _attn_bwd(qa, ka, va, do_a, dterm_a, lse_a, kind="na", bias=bias_tab, name="na_bwd")
    g_rpb = _na_bias_grad(dbias)

    def merge_bwd(dob, rows, vecs):
        o0, o1, o2, l0, l1, l2 = rows
        on = vecs[0]
        mx = jnp.maximum(jnp.maximum(l0, l1), l2)
        e0, e1, e2 = jnp.exp(l0 - mx), jnp.exp(l1 - mx), jnp.exp(l2 - mx)
        s = e0 + e1 + e2
        ws = [e0 / s, e1 / s, e2 / s]
        dws = [_headsum(dob * o, on) for o in (o0, o1, o2)]
        mean = ws[0] * dws[0] + ws[1] * dws[1] + ws[2] * dws[2]
        return [jnp.concatenate([w * dob, w * mean], axis=1) for w in ws], []

    mb = _mm_parts_rows([(dpb, w_pb_t)], merge_bwd, [gr["o"] for gr in grp] + [gr["lse"] for gr in grp], [ones],
                        [(2 * WB_OUT, F32)] * len(grp), [], name="dgrad_proj_b_merge_bwd")
    dqb, dkb, dvb = [], [], []
    for g, gr in enumerate(grp):
        dil, qkv_c = gr["dil"], gr["qkv"]
        dd_c = _to_class(mb[g], dil)
        dq, dk, dv = _attn_bwd((qkv_c, 0), (qkv_c, 1), (qkv_c, 2), (dd_c, 0), (dd_c, 1), (gr["lse_c"], 0),
                               kind="dil", seg=t // dil, name=f"dil_bwd{g}")
        dqb.append(_from_class(dq, dil))
        dkb.append(dk[0] if dil == 1 else _from_class(dk[0][0][0], dil))
        dvb.append(dv[0] if dil == 1 else _from_class(dv[0][0][0], dil))

    def qk_bwd(width, rotate, nparts):
        def fn(rows, vecs):
            gq, gk, on = vecs
            xv = rows[0]
            pos = 1
            if rotate:
                rc, rlo, rhi = rows[1:4]
                pos = 4
            cat = lambda parts: parts[0] if len(parts) == 1 else jnp.concatenate(parts, axis=1)
            ends = np.cumsum((pos,) + nparts)
            dq, dk, dv = [cat(rows[ends[i]:ends[i + 1]]) for i in range(3)]
            outs, sums = [], []
            for i, (dy, g) in enumerate(((dq, gq), (dk, gk))):
                if rotate:
                    dy = _rot(dy, rc, -rlo, -rhi)
                xi = xv[:, i * width:(i + 1) * width]
                r = lax.rsqrt(_headsum(xi * xi, on) * (1.0 / HEAD_DIM) + EPS)
                xh = xi * r
                dxh = dy * g
                outs.append(r * (dxh - xh * (_headsum(dxh * xh, on) * (1.0 / HEAD_DIM))))
                sums.append(_colsum(dy * xh))
            return [jnp.concatenate(outs + [dv], axis=1)], sums
        return fn

    dqkv_a, dg_qa, dg_ka = _rowmap(qk_bwd(WA, False, (1, len(dka), len(dva))), [qkv_a, dqa] + dka + dva,
                                   [tile_g(g_qa, NA_HEADS), tile_g(g_ka, NA_HEADS), ones],
                                   [(3 * WA, MXU_DTYPE)], [WA, WA], name="qknorm_a_bwd")
    dqkv_b, dg_qb, dg_kb = _rowmap(qk_bwd(WB, True, (3, 3, 3)), [qkv_b, rot_c, rot_lo, rot_hi] + dqb + dkb + dvb,
                                   [tile_g(g_qb, DIL_HEADS), tile_g(g_kb, DIL_HEADS), ones],
                                   [(3 * WB, MXU_DTYPE)], [WB, WB], name="qknorm_b_bwd")

    dw_in_t = jnp.concatenate([
        _mm(dqkv_a, h, ta=True, out_dtype=WIRE_DTYPE, name="wgrad_in_a"),
        _mm(dqkv_b, h, ta=True, out_dtype=WIRE_DTYPE, name="wgrad_in_b"),
        _mm(dgates, h, ta=True, out_dtype=WIRE_DTYPE, name="wgrad_in_gates")], axis=0)
    h_in, tok_in = _split_start([dw_in_t], [land7(dw_in_t)], gather=False, name="exchange_in_start")
    def norm1_bwd(dhv, rows, vecs):
        xv, dx1v = rows
        dxv, sums = norm_bwd(dhv, xv, vecs[0], vecs[1])
        return [dx1v + dxv], sums

    grad_x, dsh1, dsc1, dg1 = _mm_parts_rows(
        [(dqkv_a, w_in_a), (dqkv_b, w_in_b), (dgates, w_in_g)], norm1_bwd, [xt, dx1], [g_norm1 + tok_in, sc1],
        [(d, F32)], [d] * 3, name="dgrad_in_norm1_bwd")

    heads_sum = lambda a, heads: a.reshape(heads, HEAD_DIM).sum(axis=0)
    dmod = jnp.concatenate([dsh1, dsc1, dgt1, dsh2, dsc2, dgt2], axis=1)
    local_small = _pack_small(dict(
        b_ada=dmod, g_norm1=dg1, g_norm2=dg2, b_gate=db_gate, g_qa=heads_sum(dg_qa, NA_HEADS),
        g_ka=heads_sum(dg_ka, NA_HEADS), g_qb=heads_sum(dg_qb, DIL_HEADS), g_kb=heads_sum(dg_kb, DIL_HEADS),
        rpb=g_rpb, loss=(0.5 / d) * jnp.sum(err2)))
    srows = local_small.shape[0]
    small_all = _all_gather([local_small], name="gather_small")[0].reshape(N_DEV, srows, LANES)
    small_sum = _sum8(small_all[0], small_all[1:], name="sum_small")
    small_shapes = dict(b_ada=b_ada.shape, g_norm1=g_norm1.shape, g_norm2=g_norm2.shape, b_gate=b_gate.shape,
                        g_qa=g_qa.shape, g_ka=g_ka.shape, g_qb=g_qb.shape, g_kb=g_kb.shape, rpb=rpb.shape, loss=())
    small_w = dict(b_ada=b_ada, g_norm1=g_norm1, g_norm2=g_norm2, b_gate=b_gate, g_qa=g_qa, g_ka=g_ka, g_qb=g_qb,
                   g_kb=g_kb, rpb=rpb, loss=jnp.zeros((), F32))
    small_m = dict(b_ada=m_b_ada, g_norm1=m_g_norm1, g_norm2=m_g_norm2, b_gate=m_b_gate, g_qa=m_g_qa, g_ka=m_g_ka,
                   g_qb=m_g_qb, g_kb=m_g_kb, rpb=m_rpb, loss=jnp.zeros((), F32))
    small_v = dict(b_ada=v_b_ada, g_norm1=v_g_norm1, g_norm2=v_g_norm2, b_gate=v_b_gate, g_qa=v_g_qa, g_ka=v_g_ka,
                   g_qb=v_g_qb, g_kb=v_g_kb, rpb=v_rpb, loss=jnp.zeros((), F32))
    s_delta, s_m, s_v = _adamw(_pack_small(small_w), small_sum, _pack_small(small_m), _pack_small(small_v), name="adamw_small")
    gs = _unpack_small(small_sum, small_shapes)
    ds_, ms_, vs_ = [_unpack_small(a, small_shapes) for a in (s_delta, s_m, s_v)]

    dmod_all = small_all[:, :6 * d // LANES].reshape(N_DEV, 6 * d)
    dmod_mine = jnp.pad(lax.dynamic_slice(dmod_all, (0, me * ncol), (N_DEV, ncol)), ((0, LANES - N_DEV), (0, 0)))

    def wada_body(c_ref, dm_ref, o_ref):
        chi, cmid, clo = _split3(c_ref[...])
        dhi, dmid, dlo = _split3(dm_ref[...])
        o_ref[...] = (_dot_tn(chi, dhi) + (_dot_tn(chi, dmid) + _dot_tn(cmid, dhi))
                      + (_dot_tn(chi, dlo) + _dot_tn(cmid, dmid) + _dot_tn(clo, dhi)))

    g_w_ada = pl.pallas_call(
        wada_body,
        name="wgrad_ada",
        out_shape=jax.ShapeDtypeStruct((d, ncol), F32),
        compiler_params=pltpu.CompilerParams(vmem_limit_bytes=_vmem(4 * d * ncol * 4)),
    )(c_act, dmod_mine)

    sent, recv = _split_wait(h_in + h_ffn + h_out, small_sum, gather=False, name="exchange_wait")
    names = ("w_in", "w_ffn_in", "w_ffn_out", "w_proj_a", "w_proj_b", "w_o")
    transposed = (True, True, False, True, True, False)
    big_g = {}
    for nme, own, r, tr in zip(names, sent, recv, transposed):
        s = _sum8(own_block(own), r, name=f"sum_{nme}")
        big_g[nme] = s.T if tr else s
    big_g["w_ada"] = g_w_ada
    big_w = dict(w_ada=w_ada, w_in=w_in, w_proj_a=w_proj_a, w_proj_b=w_proj_b, w_o=w_o, w_ffn_in=w_ffn_in, w_ffn_out=w_ffn_out)
    big_m = dict(w_ada=m_w_ada, w_in=m_w_in, w_proj_a=m_w_proj_a, w_proj_b=m_w_proj_b, w_o=m_w_o, w_ffn_in=m_w_ffn_in, w_ffn_out=m_w_ffn_out)
    big_v = dict(w_ada=v_w_ada, w_in=v_w_in, w_proj_a=v_w_proj_a, w_proj_b=v_w_proj_b, w_o=v_w_o, w_ffn_in=v_w_ffn_in, w_ffn_out=v_w_ffn_out)
    grads, deltas, new_m, new_v = {}, {}, {}, {}
    for nme in big_w:
        dl, m1, v1 = _adamw(big_w[nme][0], big_g[nme], big_m[nme][0], big_v[nme][0], name=f"adamw_{nme}")
        grads[nme], deltas[nme], new_m[nme], new_v[nme] = big_g[nme][None], dl[None], m1[None], v1[None]
    for nme in _SMALL[:-1]:
        grads[nme], deltas[nme], new_m[nme], new_v[nme] = gs[nme], ds_[nme], ms_[nme], vs_[nme]

    order = ("w_ada", "b_ada", "g_norm1", "g_norm2", "w_in", "b_gate", "g_qa", "g_ka", "g_qb", "g_kb", "rpb",
             "w_proj_a", "w_proj_b", "w_o", "w_ffn_in", "w_ffn_out")
    return (gs["loss"], grad_x[None], *[grads[n] for n in order], *[deltas[n] for n in order],
            *[new_m[n] for n in order], *[new_v[n] for n in order])
```

```python
import functools

import numpy as np
import jax
import jax.numpy as jnp
from jax import lax
from jax.experimental import pallas as pl
from jax.experimental.pallas import tpu as pltpu

F32 = jnp.float32
MXU_DTYPE = jnp.bfloat16
WIRE_DTYPE = jnp.bfloat16
ACT_DTYPE = jnp.bfloat16

HEAD_DIM = 64
GRID_W = 64
NA_HEADS = 8
NA_KH = 8
NA_KW = 16
DIL_CONFIGS = ((128, 1), (512, 4), (2048, 16))
DIL_HEADS_PER_GROUP = 4
DIL_HEADS = DIL_HEADS_PER_GROUP * len(DIL_CONFIGS)
DIL_HALF = 64
ROT_DIM = HEAD_DIM // 4
ROPE_THETA = 500000.0
EPS = 1e-6
NEG_INF = -1e30
WA = NA_HEADS * HEAD_DIM
WB = DIL_HEADS * HEAD_DIM
WB_OUT = DIL_HEADS_PER_GROUP * HEAD_DIM
ADAM_LR = 0.001
ADAM_B1 = 0.9
ADAM_B2 = 0.999
ADAM_EPS = 1e-08
ADAM_WD = 0.01
ADAM_STEP = 10

N_DEV = 8
LANES = 128
VMEM_CAP = 60 * 2**20
VMEM_FLOOR = 56 * 2**20
MESH = pl.DeviceIdType.MESH
ANY = pl.BlockSpec(memory_space=pl.ANY)


def _vmem(nbytes):
    return int(min(VMEM_CAP, max(VMEM_FLOOR, nbytes * 5 // 4 + 4 * 2**20)))


def _pick(dim, cands):
    for c in cands:
        if c <= dim and dim % c == 0:
            return c
    return dim


def _nbytes(shape, dtype):
    return int(np.prod(shape)) * jnp.dtype(dtype).itemsize


def _dot(a, b, dims=((1,), (0,))):
    return lax.dot_general(a.astype(MXU_DTYPE), b.astype(MXU_DTYPE), (dims, ((), ())), preferred_element_type=F32)


def _dot_nt(a, b):
    return _dot(a, b, ((1,), (1,)))


def _dot_tn(a, b):
    return _dot(a, b, ((0,), (0,)))


def _split3(a):
    hi = a.astype(jnp.bfloat16)
    r1 = a - hi.astype(F32)
    mid = r1.astype(jnp.bfloat16)
    lo = (r1 - mid.astype(F32)).astype(jnp.bfloat16)
    return hi, mid, lo


def _silu(x):
    return x * jax.nn.sigmoid(x)


def _divisors(dim, unit):
    return [c for c in range(unit, dim + 1, unit) if dim % c == 0] or [dim]


def _mm_tiles(m, n, kdim, a_item, b_item, o_item, row_off=0):
    step_us, hbm_bytes_per_us, flops_per_us, budget = 0.35, 3.0e6, 8.0e8, 40 * 2**20
    best = None
    for tm in _divisors(m, LANES):
        for tn in _divisors(n, LANES):
            for tk in _divisors(kdim, LANES):
                if row_off % tm:
                    continue
                gm, gn, gk = m // tm, n // tn, kdim // tk
                vmem = 2 * (tm * tk * a_item + tk * tn * b_item + tm * tn * o_item) + 2 * (tm * tk + tk * tn)
                vmem += tm * tn * 4 * ((1 if gk > 1 else 0) + 1)
                if vmem > budget:
                    continue
                a_reads = m * kdim * a_item * (gn if gk > 1 else 1)
                traffic = a_reads + kdim * n * b_item * gm + m * n * o_item
                cost = gm * gn * gk * step_us + max(traffic / hbm_bytes_per_us, 2.0 * m * n * kdim / flops_per_us)
                if best is None or cost < best[0]:
                    best = (cost, tm, tn, tk)
    return best[1:]


def _mm(a, b, *, name, ta=False, tb=False, out_dtype=F32, into=None):
    if ta:
        kdim, m = a.shape
    else:
        m, kdim = a.shape
    n = b.shape[0] if tb else b.shape[1]
    assert b.shape[1 if tb else 0] == kdim
    buf, row_off = into if into is not None else (None, 0)
    if buf is not None:
        out_dtype = buf.dtype
    tm, tn, tk = _mm_tiles(m, n, kdim, a.dtype.itemsize, b.dtype.itemsize, jnp.dtype(out_dtype).itemsize, row_off)
    gm, gn, gk = m // tm, n // tn, kdim // tk
    ob = row_off // tm

    a_spec = pl.BlockSpec((tk, tm), lambda i, j, k: (k, i)) if ta else pl.BlockSpec((tm, tk), lambda i, j, k: (i, k))
    b_spec = pl.BlockSpec((tn, tk), lambda i, j, k: (j, k)) if tb else pl.BlockSpec((tk, tn), lambda i, j, k: (k, j))
    o_spec = pl.BlockSpec((tm, tn), lambda i, j, k: (i + ob, j))
    a_dims = (0,) if ta else (1,)
    b_dims = (1,) if tb else (0,)

    def body(a_ref, b_ref, *rest):
        o_ref, scratch = rest[-1 - (gk > 1)], rest[-(gk > 1):] if gk > 1 else ()
        if gk == 1:
            o_ref[...] = _dot(a_ref[...], b_ref[...], (a_dims, b_dims)).astype(o_ref.dtype)
            return
        (acc_ref,) = scratch
        k = pl.program_id(2)

        @pl.when(k == 0)
        def _():
            acc_ref[...] = jnp.zeros_like(acc_ref)

        acc_ref[...] += _dot(a_ref[...], b_ref[...], (a_dims, b_dims))

        @pl.when(k == gk - 1)
        def _():
            o_ref[...] = acc_ref[...].astype(o_ref.dtype)

    est = 2 * (tm * tk * a.dtype.itemsize + tk * tn * b.dtype.itemsize + tm * tn * jnp.dtype(out_dtype).itemsize)
    est += tm * tn * 4 + 2 * (tm * tk + tk * tn) * 2
    return pl.pallas_call(
        body,
        name=name,
        grid=(gm, gn, gk),
        in_specs=[a_spec, b_spec] + ([ANY] if buf is not None else []),
        out_specs=o_spec,
        out_shape=jax.ShapeDtypeStruct((m, n) if buf is None else buf.shape, out_dtype),
        input_output_aliases={2: 0} if buf is not None else {},
        scratch_shapes=[pltpu.VMEM((tm, tn), F32)] if gk > 1 else [],
        compiler_params=pltpu.CompilerParams(
            dimension_semantics=("parallel", "parallel", "arbitrary"), vmem_limit_bytes=_vmem(est)
        ),
    )(*((a, b) if buf is None else (a, b, buf)))


def _resident(shape):
    return pl.BlockSpec(shape, lambda i: (0,) * len(shape), pipeline_mode=pl.Buffered(1))


def _row_tile(m, fixed_bytes, bytes_per_row, budget=50 * 2**20):
    fits = [tm for tm in _divisors(m, LANES) if fixed_bytes + tm * bytes_per_row <= budget]
    return max(fits) if fits else _divisors(m, LANES)[0]


def _mm_parts_rows(parts, fn, rows, vecs, outs, reds, *, name, separate=False):
    parts = [(p[0], p[1], len(p) > 2) for p in parts]
    rows = [r if isinstance(r, tuple) else (r, r.shape[1], 0) for r in rows]
    m = next(a for a, _, _ in parts if not callable(a)).shape[0]
    n = parts[0][1].shape[0 if parts[0][2] else 1]
    npart, nr, nv, no = len(parts), len(rows), len(vecs), len(outs)
    n_op = sum(1 if callable(a) else 2 for a, _, _ in parts)
    row_bytes = sum(w * r.dtype.itemsize for r, w, _ in rows) + sum(w * jnp.dtype(dt).itemsize for (w, dt) in outs)
    a_row_bytes = sum(a.shape[1] * a.dtype.itemsize for a, _, _ in parts if not callable(a))
    fixed = sum(_nbytes(b.shape, b.dtype) for _, b, _ in parts)
    per_row = 2 * (a_row_bytes + row_bytes) + n * 4 * 5
    tm = _row_tile(m, fixed, per_row)
    sub = min(tm, 2 * LANES)

    def body(*refs):
        it = iter(refs[:n_op])
        a_refs, b_refs = [], []
        for a, _, _ in parts:
            a_refs.append(None if callable(a) else next(it))
            b_refs.append(next(it))
        row_refs, vec_refs = refs[n_op:n_op + nr], refs[n_op + nr:n_op + nr + nv]
        out_refs = refs[n_op + nr + nv:n_op + nr + nv + no]
        red_refs = refs[n_op + nr + nv + no:]
        if red_refs:
            @pl.when(pl.program_id(0) == 0)
            def _():
                for ref in red_refs:
                    ref[...] = jnp.zeros_like(ref)

        vecs_v = [v[...] for v in vec_refs]
        for s0 in range(0, tm, sub):
            sl = slice(s0, s0 + sub)
            rows_v = [x[sl, :].astype(F32) for x in row_refs]
            lhs = [parts[p][0](rows_v, vecs_v) if a_refs[p] is None else a_refs[p][sl, :] for p in range(npart)]
            prods = [(_dot_nt if nt else _dot)(lhs[p], b_refs[p][...]) for p, (_, _, nt) in enumerate(parts)]
            r = prods if separate else functools.reduce(lambda u, v: u + v, prods)
            made = [lhs[p] for p in range(npart) if a_refs[p] is None]
            o, rd = fn(r, rows_v, vecs_v, *([made] if made else []))
            for ref, val in zip(out_refs, o):
                ref[sl, :] = val.astype(ref.dtype)
            for ref, val in zip(red_refs, rd):
                ref[...] += val

    in_specs, operands = [], []
    for a, b, _ in parts:
        if not callable(a):
            in_specs.append(pl.BlockSpec((tm, a.shape[1]), lambda i: (i, 0)))
            operands.append(a)
        in_specs.append(_resident(b.shape))
        operands.append(b)
    in_specs += [pl.BlockSpec((tm, w), functools.partial(lambda cb, i: (i, cb), cb)) for _, w, cb in rows]
    in_specs += [pl.BlockSpec(v.shape, functools.partial(lambda nd, i: (0,) * nd, v.ndim)) for v in vecs]
    out_specs = [pl.BlockSpec((tm, w), lambda i: (i, 0)) for (w, _) in outs]
    out_specs += [pl.BlockSpec((1, w), lambda i: (0, 0)) for w in reds]
    out_shape = [jax.ShapeDtypeStruct((m, w), dt) for (w, dt) in outs] + [jax.ShapeDtypeStruct((1, w), F32) for w in reds]
    return pl.pallas_call(
        body,
        name=name,
        grid=(m // tm,),
        in_specs=in_specs,
        out_specs=out_specs,
        out_shape=out_shape,
        compiler_params=pltpu.CompilerParams(dimension_semantics=("arbitrary",), vmem_limit_bytes=_vmem(fixed + tm * per_row)),
    )(*operands, *[r for r, _, _ in rows], *vecs)


def _mm_ew(a, bs, fn, rows, outs, *, name):
    m, kdim = a.shape
    n = bs[0].shape[0]
    nb, nr, no = len(bs), len(rows), len(outs)
    cw = _pick(n, (2 * LANES, LANES))
    fixed = nb * n * kdim * bs[0].dtype.itemsize
    per_row = 2 * (kdim * a.dtype.itemsize + n * (sum(r.dtype.itemsize for r in rows) + sum(jnp.dtype(dt).itemsize for dt in outs)))
    per_row += cw * 4 * 4 * (nb + 4)
    tm = _row_tile(m, fixed, per_row)

    def body(*refs):
        a_ref, b_refs = refs[0], refs[1:1 + nb]
        row_refs, out_refs = refs[1 + nb:1 + nb + nr], refs[1 + nb + nr:]
        av = a_ref[...]
        for c0 in range(0, n, cw):
            cols = slice(c0, c0 + cw)
            o = fn([_dot_nt(av, b[cols, :]) for b in b_refs], [x[:, cols].astype(F32) for x in row_refs])
            for ref, val in zip(out_refs, o):
                ref[:, cols] = val.astype(ref.dtype)

    tile = pl.BlockSpec((tm, n), lambda i: (i, 0))
    return pl.pallas_call(
        body,
        name=name,
        grid=(m // tm,),
        in_specs=[pl.BlockSpec((tm, kdim), lambda i: (i, 0))] + [_resident((n, kdim))] * nb + [tile] * nr,
        out_specs=[tile] * no,
        out_shape=[jax.ShapeDtypeStruct((m, n), dt) for dt in outs],
        compiler_params=pltpu.CompilerParams(dimension_semantics=("parallel",), vmem_limit_bytes=_vmem(fixed + tm * per_row)),
    )(a, *bs, *rows)


def _rowmap(fn, rows, vecs, outs, reds, *, name, tm=None):
    norm = []
    for r in rows:
        if not isinstance(r, tuple):
            norm.append((r, r.shape[1], 0, None))
        elif len(r) == 2:
            norm.append((r[0], r[0].shape[2], 0, r[1]))
        else:
            norm.append((r[0], r[1], r[2], None))
    rows = norm
    t = rows[0][0].shape[-2]
    if tm is None:
        per_row = 2 * sum(w * a.dtype.itemsize for (a, w, _, _) in rows) + 2 * sum(w * jnp.dtype(d).itemsize for (w, d) in outs)
        per_row += 3 * 4 * max([w for (_, w, _, _) in rows] + [w for (w, _) in outs])
        tm = max(8, min(1024, (40 * 2**20) // per_row))
    tm = _pick(t, tuple(c for c in (1024, 512, 256, 128, 64, 32, 16, 8) if c <= tm))
    nr, nv, no = len(rows), len(vecs), len(outs)

    def body(*refs):
        row_refs, vec_refs = refs[:nr], refs[nr:nr + nv]
        out_refs, red_refs = refs[nr + nv:nr + nv + no], refs[nr + nv + no:]
        o, rd = fn([r[...].astype(F32) for r in row_refs], [v[...] for v in vec_refs])
        for ref, val in zip(out_refs, o):
            ref[...] = val.astype(ref.dtype)
        if red_refs:
            @pl.when(pl.program_id(0) == 0)
            def _():
                for ref in red_refs:
                    ref[...] = jnp.zeros_like(ref)

            for ref, val in zip(red_refs, rd):
                ref[...] += val

    in_specs = [pl.BlockSpec((tm, w), functools.partial(lambda cb, i: (i, cb), cb)) if lead is None
                else pl.BlockSpec((None, tm, w), functools.partial(lambda ld, i: (ld, i, 0), lead)) for (_, w, cb, lead) in rows]
    in_specs += [pl.BlockSpec(v.shape, functools.partial(lambda nd, i: (0,) * nd, v.ndim)) for v in vecs]
    out_specs = [pl.BlockSpec((tm, w), lambda i: (i, 0)) for (w, _) in outs]
    out_specs += [pl.BlockSpec((1, w), lambda i: (0, 0)) for w in reds]
    out_shape = [jax.ShapeDtypeStruct((t, w), d) for (w, d) in outs]
    out_shape += [jax.ShapeDtypeStruct((1, w), F32) for w in reds]
    est = 2 * sum(tm * w * a.dtype.itemsize for (a, w, _, _) in rows) + 2 * sum(_nbytes(v.shape, v.dtype) for v in vecs)
    est += 2 * sum(tm * w * jnp.dtype(d).itemsize for (w, d) in outs)
    est += 6 * tm * max([w for (_, w, _, _) in rows] + [w for (w, _) in outs]) * 4
    return pl.pallas_call(
        body,
        name=name,
        grid=(t // tm,),
        in_specs=in_specs,
        out_specs=out_specs,
        out_shape=out_shape,
        compiler_params=pltpu.CompilerParams(dimension_semantics=("arbitrary",), vmem_limit_bytes=_vmem(est)),
    )(*[r[0] for r in rows], *vecs)


def _colsum(v):
    return jnp.sum(v, axis=0, keepdims=True)


def _head_ones():
    i = np.arange(LANES)
    return jnp.asarray((i[:, None] // HEAD_DIM) == (i[None, :] // HEAD_DIM), MXU_DTYPE)


def _headsum(y, ones):
    parts = []
    for j in range(y.shape[1] // LANES):
        c = y[:, j * LANES:(j + 1) * LANES]
        hi = c.astype(MXU_DTYPE)
        lo = c - hi.astype(F32)
        parts.append(_dot(hi, ones) + _dot(lo, ones))
    return parts[0] if len(parts) == 1 else jnp.concatenate(parts, axis=1)


def _rot(y, c, s_lo, s_hi):
    parts = []
    for j in range(y.shape[1] // LANES):
        yc = y[:, j * LANES:(j + 1) * LANES]
        parts.append(yc * c + pltpu.roll(yc, LANES - ROT_DIM // 2, 1) * s_lo + pltpu.roll(yc, ROT_DIM // 2, 1) * s_hi)
    return parts[0] if len(parts) == 1 else jnp.concatenate(parts, axis=1)


def _rot_tables(t):
    half = ROT_DIM // 2
    inv_freq = ROPE_THETA ** (-(jnp.arange(half, dtype=F32) * 2.0) / ROT_DIM)
    ang = jnp.arange(t).astype(F32)[:, None] * inv_freq[None, :]
    cos, sin = jnp.cos(ang), jnp.sin(ang)
    z = lambda w: jnp.zeros((t, w), F32)
    c = jnp.concatenate([cos, cos, jnp.ones((t, HEAD_DIM - ROT_DIM), F32)], axis=1)
    s_lo = jnp.concatenate([-sin, z(HEAD_DIM - half)], axis=1)
    s_hi = jnp.concatenate([z(half), sin, z(HEAD_DIM - ROT_DIM)], axis=1)
    return [jnp.tile(a, (1, LANES // HEAD_DIM)) for a in (c, s_lo, s_hi)]


def _rms(x):
    return lax.rsqrt(jnp.mean(x * x, axis=-1, keepdims=True) + EPS)


def _window(kind, n, bq, t, seg):
    if kind == "na":
        rows = t // GRID_W
        rs = jnp.clip(n - NA_KH // 2, 0, rows - NA_KH)
        return rs
    nk = bq + 2 * DIL_HALF
    return jnp.clip(n * bq - DIL_HALF, 0, t - nk)


def _dil_mask(n, bq, nk, ws, seg):
    qi = n * bq + lax.broadcasted_iota(jnp.int32, (bq, nk), 0)
    ki = ws + lax.broadcasted_iota(jnp.int32, (bq, nk), 1)
    shift = int(np.log2(seg))
    return (jnp.abs(ki - qi) <= DIL_HALF) & ((ki >> shift) == (qi >> shift))


HS = 4
QW = HS * HEAD_DIM


def _head_of_lane(width=QW):
    return lax.broadcasted_iota(jnp.int32, (1, width), 1) // HEAD_DIM


def _stack_heads(a):
    head = _head_of_lane()
    return jnp.concatenate([jnp.where(head == e, a, jnp.zeros_like(a)) for e in range(HS)], axis=0)


def _unstack_heads(a, bq):
    head = _head_of_lane()
    out = jnp.zeros((bq, QW), a.dtype)
    for e in range(HS):
        out = jnp.where(head == e, a[e * bq:(e + 1) * bq], out)
    return out


def _stack_cols(blk, bq):
    head = _head_of_lane()
    return jnp.concatenate(
        [jnp.max(jnp.where(head == e, blk, -jnp.inf), axis=1, keepdims=True) for e in range(HS)], axis=0)


def _attn_geometry(kind):
    if kind == "na":
        return GRID_W, NA_KH * GRID_W, 16
    bq = 128
    return bq, bq + 2 * DIL_HALF, 8


def _attn_scores(kind, n, bq, nk, t, seg, qs, k_ref, b_ref):
    scale = HEAD_DIM ** -0.5
    if kind == "na":
        rs = _window(kind, n, bq, t, seg)
        ws = pl.multiple_of(rs * GRID_W, GRID_W)
        ro0 = rs - n + (NA_KH - 1)
        s = _dot_nt(qs, k_ref[pl.ds(ws, nk), :]) * scale
        s = s + jnp.concatenate(
            [jnp.concatenate([b_ref[e, ro0 + 2 * i] for i in range(NA_KH // 2)], axis=1) for e in range(HS)], axis=0)
        return s, ws, ro0
    ws = pl.multiple_of(_window(kind, n, bq, t, seg), DIL_HALF)
    mask = _dil_mask(n, bq, nk, ws, seg)
    s = _dot_nt(qs, k_ref[pl.ds(ws, nk), :]) * scale
    s = jnp.where(jnp.concatenate([mask] * HS, axis=0), s, NEG_INF)
    return s, ws, None


def _col_operands(*ops):
    pairs = [op if isinstance(op, tuple) else (op, 0) for op in ops]
    width = QW if isinstance(ops[0], tuple) else ops[0].shape[1]
    return (*pairs, width)


def _q_block(rows, col):
    return pl.BlockSpec((rows, QW), lambda j, n: (n, j + col))


def _kv_resident(t, col):
    return pl.BlockSpec((t, QW), lambda j, n: (0, j + col))


def _attn_fwd(q, k, v, *, kind, name, bias=None, seg=None):
    (q, cq), (k, ck), (v, cv), w = _col_operands(q, k, v)
    t = q.shape[0]
    quads = w // QW
    bq, nk, sub = _attn_geometry(kind)
    nq = t // (bq * sub)

    def body(*refs):
        if kind == "na":
            q_ref, k_ref, v_ref, b_ref, o_ref, l_ref = refs
        else:
            (q_ref, k_ref, v_ref, o_ref, l_ref), b_ref = refs, None
        for i in range(sub):
            n = pl.program_id(1) * sub + i
            rows = slice(i * bq, (i + 1) * bq)
            s, ws, _ = _attn_scores(kind, n, bq, nk, t, seg, _stack_heads(q_ref[rows, :]), k_ref, b_ref)
            m = jnp.max(s, axis=1, keepdims=True)
            p = jnp.exp(s - m)
            l = jnp.sum(p, axis=1, keepdims=True)
            o_ref[rows, :] = _unstack_heads(_dot(p / l, v_ref[pl.ds(ws, nk), :]), bq)
            l_ref[rows, :] = _unstack_heads(jnp.broadcast_to(m + jnp.log(l), (HS * bq, QW)), bq)

    blk = pl.BlockSpec((bq * sub, QW), lambda j, n: (n, j))
    in_specs = [_q_block(bq * sub, cq), _kv_resident(t, ck), _kv_resident(t, cv)]
    operands = [q, k, v]
    est = 4 * t * QW * q.dtype.itemsize + 12 * sub * HS * bq * nk * 4
    if kind == "na":
        in_specs.append(pl.BlockSpec((HS,) + bias.shape[1:], lambda j, n: (j, 0, 0, 0)))
        operands.append(bias)
        est += 2 * _nbytes((HS,) + bias.shape[1:], F32)
    return pl.pallas_call(
        body,
        name=name,
        grid=(quads, nq),
        in_specs=in_specs,
        out_specs=[blk, blk],
        out_shape=[jax.ShapeDtypeStruct((t, w), F32)] * 2,
        compiler_params=pltpu.CompilerParams(dimension_semantics=("arbitrary", "arbitrary"), vmem_limit_bytes=_vmem(est)),
    )(*operands)


def _attn_bwd(q, k, v, do, dterm, lse, *, kind, name, bias=None, seg=None):
    (q, cq), (k, ck), (v, cv), (do, cdo), (dterm, cdt), (lse, cl), w = _col_operands(q, k, v, do, dterm, lse)
    t = q.shape[0]
    quads = w // QW
    bq, nk, sub = _attn_geometry(kind)
    nq = t // (bq * sub)
    scale = HEAD_DIM ** -0.5

    def body(*refs):
        if kind == "na":
            q_ref, k_ref, v_ref, do_ref, dt_ref, l_ref, b_ref, dq_ref, dk_hbm, dv_hbm, db_ref, dk_acc, dv_acc, sem = refs
        else:
            q_ref, k_ref, v_ref, do_ref, dt_ref, l_ref, dq_ref, dk_hbm, dv_hbm, dk_acc, dv_acc, sem = refs
            b_ref = None
        j, step = pl.program_id(0), pl.program_id(1)

        @pl.when(step == 0)
        def _():
            dk_acc[...] = jnp.zeros_like(dk_acc)
            dv_acc[...] = jnp.zeros_like(dv_acc)
            if kind == "na":
                db_ref[...] = jnp.zeros_like(db_ref)

        for b in range(sub):
            n = step * sub + b
            rows = slice(b * bq, (b + 1) * bq)
            qs = _stack_heads(q_ref[rows, :])
            dos = _stack_heads(do_ref[rows, :])
            s, ws, ro0 = _attn_scores(kind, n, bq, nk, t, seg, qs, k_ref, b_ref)
            p = jnp.exp(s - _stack_cols(l_ref[rows, :], bq))
            dp = _dot_nt(dos, v_ref[pl.ds(ws, nk), :])
            ds = p * (dp - _stack_cols(dt_ref[rows, :], bq))
            if kind == "na":
                for e in range(HS):
                    for i in range(NA_KH // 2):
                        db_ref[e, ro0 + 2 * i] += ds[e * bq:(e + 1) * bq, i * LANES:(i + 1) * LANES]
            dsc = ds * scale
            dq_ref[rows, :] = _unstack_heads(_dot(dsc, k_ref[pl.ds(ws, nk), :]), bq)
            dk_acc[pl.ds(ws, nk), :] += _dot_tn(dsc, qs)
            dv_acc[pl.ds(ws, nk), :] += _dot_tn(p, dos)

        @pl.when(step == nq - 1)
        def _():
            ck = pltpu.make_async_copy(dk_acc, dk_hbm.at[j], sem.at[0])
            cv = pltpu.make_async_copy(dv_acc, dv_hbm.at[j], sem.at[1])
            ck.start()
            cv.start()
            ck.wait()
            cv.wait()

    blk = pl.BlockSpec((bq * sub, QW), lambda j, n: (n, j))
    in_specs = [_q_block(bq * sub, cq), _kv_resident(t, ck), _kv_resident(t, cv)] + [_q_block(bq * sub, c) for c in (cdo, cdt, cl)]
    operands = [q, k, v, do, dterm, lse]
    out_specs = [blk, ANY, ANY]
    out_shape = [jax.ShapeDtypeStruct((t, w), F32)] + [jax.ShapeDtypeStruct((quads, t, QW), F32)] * 2
    est = 4 * t * QW * q.dtype.itemsize + 2 * t * QW * 4 + 16 * sub * HS * bq * nk * 4
    if kind == "na":
        bspec = pl.BlockSpec((HS,) + bias.shape[1:], lambda j, n: (j, 0, 0, 0))
        in_specs.append(bspec)
        operands.append(bias)
        out_specs.append(bspec)
        out_shape.append(jax.ShapeDtypeStruct(bias.shape, F32))
        est += 4 * _nbytes((HS,) + bias.shape[1:], F32)
    res_ = pl.pallas_call(
        body,
        name=name,
        grid=(quads, nq),
        in_specs=in_specs,
        out_specs=out_specs,
        out_shape=out_shape,
        scratch_shapes=[pltpu.VMEM((t, QW), F32), pltpu.VMEM((t, QW), F32), pltpu.SemaphoreType.DMA((2,))],
        compiler_params=pltpu.CompilerParams(dimension_semantics=("arbitrary", "arbitrary"), vmem_limit_bytes=_vmem(est)),
    )(*operands)
    unquad = lambda a: [(a, i) for i in range(quads)]
    return (res_[0], unquad(res_[1]), unquad(res_[2])) + tuple(res_[3:])


def _na_onehot():
    qc = np.arange(GRID_W)[:, None]
    kc = np.arange(GRID_W)[None, :]
    start = np.clip(qc - NA_KW // 2, 0, GRID_W - NA_KW)
    inwin = (kc >= start) & (kc < start + NA_KW)
    off = kc - qc + (NA_KW - 1)
    e_mat = np.zeros((2, 32, GRID_W, 2, GRID_W), np.float32)
    for e in range(2):
        for c in range(2 * NA_KW - 1):
            e_mat[e, c, :, e, :] = (off == c) & inwin
    neg = np.where(inwin, 0.0, NEG_INF).astype(np.float32)
    neg = np.broadcast_to(neg[:, None, :], (GRID_W, 2, GRID_W)).reshape(1, GRID_W * LANES)
    return jnp.asarray(e_mat.reshape(64, GRID_W * LANES), MXU_DTYPE), jnp.asarray(neg)


def _na_rowpairs(rpb):
    p = jnp.pad(rpb, ((0, 0), (0, 0), (0, 1)))
    return jnp.concatenate([p[:, :-1], p[:, 1:]], axis=-1).reshape(NA_HEADS * (2 * NA_KH - 2), 64)


def _na_bias_table(rpb):
    r2 = _na_rowpairs(rpb)
    e_mat, neg = _na_onehot()

    def body(r_ref, e_ref, n_ref, o_ref):
        hi, mid, lo = _split3(r_ref[...])
        e = e_ref[...]
        o_ref[...] = _dot(hi, e) + _dot(mid, e) + _dot(lo, e) + n_ref[...]

    out = pl.pallas_call(
        body,
        name="na_bias_table",
        out_shape=jax.ShapeDtypeStruct((r2.shape[0], GRID_W * LANES), F32),
        compiler_params=pltpu.CompilerParams(vmem_limit_bytes=_vmem(6 * r2.shape[0] * GRID_W * LANES * 4)),
    )(r2, e_mat, neg)
    return out.reshape(NA_HEADS, 2 * NA_KH - 2, GRID_W, LANES)


def _na_bias_grad(dbt):
    e_mat, _ = _na_onehot()
    flat = dbt.reshape(NA_HEADS * (2 * NA_KH - 2), GRID_W * LANES)

    def body(d_ref, e_ref, o_ref):
        hi, mid, lo = _split3(d_ref[...])
        e = e_ref[...]
        o_ref[...] = _dot_nt(hi, e) + _dot_nt(mid, e) + _dot_nt(lo, e)

    g = pl.pallas_call(
        body,
        name="na_bias_grad",
        out_shape=jax.ShapeDtypeStruct((flat.shape[0], 64), F32),
        compiler_params=pltpu.CompilerParams(vmem_limit_bytes=_vmem(6 * flat.shape[0] * GRID_W * LANES * 4)),
    )(flat, e_mat)
    g = g.reshape(NA_HEADS, 2 * NA_KH - 2, 2, 32)[..., :2 * NA_KW - 1]
    first = jnp.pad(g[:, :, 0], ((0, 0), (0, 1), (0, 0)))
    second = jnp.pad(g[:, :, 1], ((0, 0), (1, 0), (0, 0)))
    return first + second


def _all_gather(arrs, *, name):
    na = len(arrs)

    def body(*refs):
        ins, outs = refs[:na], refs[na:2 * na]
        send_sems, recv_sems, local_sems = refs[2 * na:]
        x, y, c = lax.axis_index("x"), lax.axis_index("y"), lax.axis_index("c")
        me, sibling = (x, y, c), (x, y, 1 - c)
        chips = [(1 - x, y), (x, 1 - y), (1 - x, 1 - y)]

        def rows(a, px, py, pc):
            r = ins[a].shape[0]
            return outs[a].at[pl.ds((4 * px + 2 * py + pc) * r, r), :]

        def copy(a, k, block, to, src=None):
            return pltpu.make_async_remote_copy(
                src_ref=rows(a, *block) if src is None else src, dst_ref=rows(a, *block),
                send_sem=send_sems.at[a, k], recv_sem=recv_sems.at[a, k], device_id=to, device_id_type=MESH)

        mine = [pltpu.make_async_copy(ins[a], rows(a, *me), local_sems.at[a]) for a in range(na)]
        for cp in mine:
            cp.start()
        first = []
        for a in range(na):
            first.append(copy(a, 0, me, sibling, src=ins[a]))
            first += [copy(a, 1 + j, me, (*chip, c), src=ins[a]) for j, chip in enumerate(chips)]
        for cp in first:
            cp.start()
        passed = []
        for j, chip in enumerate(chips):
            for a in range(na):
                copy(a, 1 + j, (*chip, c), me).wait_recv()
                cp = copy(a, 4 + j, (*chip, c), sibling)
                cp.start()
                passed.append(cp)
        for a in range(na):
            copy(a, 0, sibling, me).wait_recv()
        for j, chip in enumerate(chips):
            for a in range(na):
                copy(a, 4 + j, (*chip, 1 - c), me).wait_recv()
        for cp in first + passed:
            cp.wait_send()
        for cp in mine:
            cp.wait()

    return pl.pallas_call(
        body,
        name=name,
        in_specs=[ANY] * na,
        out_specs=[ANY] * na,
        out_shape=[jax.ShapeDtypeStruct((N_DEV * a.shape[0], a.shape[1]), a.dtype) for a in arrs],
        scratch_shapes=[pltpu.SemaphoreType.DMA((na, 7)), pltpu.SemaphoreType.DMA((na, 7)), pltpu.SemaphoreType.DMA((na,))],
    )(*arrs)


HBM = pl.BlockSpec(memory_space=pltpu.HBM)
SEM = pl.BlockSpec(memory_space=pltpu.SEMAPHORE)
EFFECT = pltpu.SideEffectType.DATAFLOW_SIDE_EFFECTING


def _peer_of(k):
    x, y, c = lax.axis_index("x"), lax.axis_index("y"), lax.axis_index("c")
    return x ^ ((k >> 2) & 1), y ^ ((k >> 1) & 1), c ^ (k & 1)


def _split_copies(gather, src_ref, land_ref, send_sems, recv_sems):
    x, y, c = lax.axis_index("x"), lax.axis_index("y"), lax.axis_index("c")
    my = 4 * x + 2 * y + c
    r = src_ref.shape[0] if gather else src_ref.shape[0] // N_DEV
    copies = []
    for k in ((1, 2, 4, 6) if gather == "chip" else range(1, N_DEV)):
        px, py, pc = _peer_of(k)
        if gather:
            src, dst = src_ref, land_ref.at[pl.ds(my * r, r), :]
        else:
            src, dst = src_ref.at[pl.ds((4 * px + 2 * py + pc) * r, r), :], land_ref.at[k - 1]
        copies.append(pltpu.make_async_remote_copy(
            src_ref=src, dst_ref=dst, send_sem=send_sems.at[k - 1], recv_sem=recv_sems.at[k - 1],
            device_id=(px, py, pc), device_id_type=MESH))
    return copies


def _split_start(srcs, lands, *, gather, name, after=None):
    na = len(srcs)
    extra = [] if after is None else [after]

    def body(*refs):
        src_refs, land_refs = refs[:na], refs[na:2 * na]
        outs = refs[2 * na + len(extra):]
        for a in range(na):
            for cp in _split_copies(gather, src_refs[a], land_refs[a], outs[4 * a], outs[4 * a + 1]):
                cp.start()
        outs[4 * na][...] = jnp.zeros_like(outs[4 * na])

    out_shape, out_specs, aliases = [], [], {}
    for a in range(na):
        out_shape += [pltpu.SemaphoreType.DMA((N_DEV - 1,)), pltpu.SemaphoreType.DMA((N_DEV - 1,)),
                      pltpu.HBM(srcs[a].shape, srcs[a].dtype), pltpu.HBM(lands[a].shape, lands[a].dtype)]
        out_specs += [SEM, SEM, HBM, HBM]
        aliases[a] = 4 * a + 2
        aliases[na + a] = 4 * a + 3
    out_shape.append(jax.ShapeDtypeStruct((8, LANES), F32))
    out_specs.append(pl.BlockSpec(memory_space=pltpu.VMEM))
    res = pl.pallas_call(
        body,
        name=name,
        out_shape=tuple(out_shape),
        in_specs=[HBM] * (2 * na) + [ANY] * len(extra),
        out_specs=tuple(out_specs),
        input_output_aliases=aliases,
        compiler_params=pltpu.CompilerParams(has_side_effects=EFFECT),
    )(*[pltpu.with_memory_space_constraint(a, pltpu.HBM) for a in list(srcs) + list(lands)], *extra)
    return [tuple(res[4 * a:4 * a + 4]) for a in range(na)], res[4 * na][0, 0]


def _split_wait(handles, after, *, gather, name):
    na = len(handles)

    def body(*refs):
        src_refs, land_refs = refs[:na], refs[na:2 * na]
        sems = refs[2 * na:4 * na]
        for a in range(na):
            for cp in _split_copies(gather, src_refs[a], land_refs[a], sems[2 * a], sems[2 * a + 1]):
                cp.wait_send()
                cp.wait_recv()

    srcs = [h[2] for h in handles]
    lands = [h[3] for h in handles]
    sems = [s for h in handles for s in h[:2]]
    res = pl.pallas_call(
        body,
        name=name,
        out_shape=tuple(pltpu.HBM(a.shape, a.dtype) for a in srcs + lands),
        in_specs=[HBM] * (2 * na) + [SEM] * (2 * na) + [ANY],
        out_specs=tuple([HBM] * (2 * na)),
        input_output_aliases={i: i for i in range(2 * na)},
        compiler_params=pltpu.CompilerParams(has_side_effects=EFFECT),
    )(*srcs, *lands, *sems, after)
    return list(res[:na]), list(res[na:])


def _forward_copies(land_ref, send_sems, recv_sems):
    x, y, c = lax.axis_index("x"), lax.axis_index("y"), lax.axis_index("c")
    r = land_ref.shape[0] // N_DEV
    copies = []
    for j, k in enumerate((2, 4, 6)):
        px, py, pc = _peer_of(k)
        rows = land_ref.at[pl.ds((4 * px + 2 * py + pc) * r, r), :]
        copies.append(pltpu.make_async_remote_copy(
            src_ref=rows, dst_ref=rows, send_sem=send_sems.at[j], recv_sem=recv_sems.at[j],
            device_id=(x, y, 1 - c), device_id_type=MESH))
    return copies


def _forward_start(land, *, name):
    def body(land_ref, send_sems, recv_sems, land_thru, token):
        for cp in _forward_copies(land_ref, send_sems, recv_sems):
            cp.start()
        token[...] = jnp.zeros_like(token)

    res = pl.pallas_call(
        body,
        name=name,
        out_shape=(pltpu.SemaphoreType.DMA((3,)), pltpu.SemaphoreType.DMA((3,)), pltpu.HBM(land.shape, land.dtype),
                   jax.ShapeDtypeStruct((8, LANES), F32)),
        in_specs=[HBM],
        out_specs=(SEM, SEM, HBM, pl.BlockSpec(memory_space=pltpu.VMEM)),
        input_output_aliases={0: 2},
        compiler_params=pltpu.CompilerParams(has_side_effects=EFFECT),
    )(pltpu.with_memory_space_constraint(land, pltpu.HBM))
    return res[:3]


def _forward_wait(handle, *, name):
    send_sems, recv_sems, land = handle

    def body(land_ref, send_ref, recv_ref, land_out):
        for cp in _forward_copies(land_ref, send_ref, recv_ref):
            cp.wait_send()
            cp.wait_recv()

    return pl.pallas_call(
        body,
        name=name,
        out_shape=pltpu.HBM(land.shape, land.dtype),
        in_specs=[HBM, SEM, SEM],
        out_specs=HBM,
        input_output_aliases={0: 0},
        compiler_params=pltpu.CompilerParams(has_side_effects=EFFECT),
    )(land, send_sems, recv_sems)


def _sum8(own, recv, *, name):
    _, r, w = recv.shape
    fits = lambda c: 2 * c * w * (N_DEV * recv.dtype.itemsize + 4) <= 32 * 2**20
    tr = _pick(r, tuple(c for c in (r // 2, r // 4, 256, 128, 64, 32, 16, 8) if c % 16 == 0 and fits(c)))

    def body(own_ref, a_ref, o_ref):
        acc = own_ref[...].astype(F32)
        for i in range(N_DEV - 1):
            acc = acc + a_ref[i].astype(F32)
        o_ref[...] = acc

    return pl.pallas_call(
        body,
        name=name,
        grid=(r // tr,),
        in_specs=[pl.BlockSpec((tr, w), lambda i: (i, 0)), pl.BlockSpec((N_DEV - 1, tr, w), lambda i: (0, i, 0))],
        out_specs=pl.BlockSpec((tr, w), lambda i: (i, 0)),
        out_shape=jax.ShapeDtypeStruct((r, w), F32),
        compiler_params=pltpu.CompilerParams(dimension_semantics=("parallel",), vmem_limit_bytes=_vmem(4 * N_DEV * tr * w * 4)),
    )(own, recv)


def _adamw(w, g, m, v, *, name):
    def fn(rows, _):
        wv, gv, mv, vv = rows
        m1 = ADAM_B1 * mv + (1.0 - ADAM_B1) * gv
        v1 = ADAM_B2 * vv + (1.0 - ADAM_B2) * jnp.square(gv)
        m_hat = m1 / (1.0 - ADAM_B1 ** ADAM_STEP)
        v_hat = v1 / (1.0 - ADAM_B2 ** ADAM_STEP)
        delta = -ADAM_LR * (m_hat / (jnp.sqrt(v_hat) + ADAM_EPS) + ADAM_WD * wv)
        return [delta, m1, v1], []

    c = w.shape[1]
    return _rowmap(fn, [w, g, m, v], [], [(c, F32)] * 3, [], name=name)


_SMALL = ("b_ada", "g_norm1", "g_norm2", "b_gate", "g_qa", "g_ka", "g_qb", "g_kb", "rpb", "loss")


def _pack_small(parts):
    flat = []
    for nme in _SMALL:
        a = parts[nme].reshape(-1).astype(F32)
        flat.append(jnp.pad(a, (0, (-a.shape[0]) % LANES)))
    flat = jnp.concatenate(flat)
    flat = jnp.pad(flat, (0, (-flat.shape[0]) % (LANES * LANES)))
    return flat.reshape(-1, LANES)


def _unpack_small(packed, shapes):
    flat = packed.reshape(-1)
    out, pos = {}, 0
    for nme in _SMALL:
        n = int(np.prod(shapes[nme]))
        out[nme] = flat[pos:pos + n].reshape(shapes[nme])
        pos += n + (-n) % LANES
    return out


def _to_class(a, d):
    t, w = a.shape
    return a if d == 1 else a.reshape(t // d, d, w).transpose(1, 0, 2).reshape(t, w)


def _from_class(a, d):
    t, w = a.shape
    return a if d == 1 else a.reshape(d, t // d, w).transpose(1, 0, 2).reshape(t, w)


def kernel(x, c, w_ada, b_ada, g_norm1, g_norm2, w_in, b_gate, g_qa, g_ka, g_qb, g_kb, rpb, w_proj_a, w_proj_b, w_o, w_ffn_in, w_ffn_out, loss_target, m_w_ada, m_b_ada, m_g_norm1, m_g_norm2, m_w_in, m_b_gate, m_g_qa, m_g_ka, m_g_qb, m_g_kb, m_rpb, m_w_proj_a, m_w_proj_b, m_w_o, m_w_ffn_in, m_w_ffn_out, v_w_ada, v_b_ada, v_g_norm1, v_g_norm2, v_w_in, v_b_gate, v_g_qa, v_g_ka, v_g_qb, v_g_kb, v_rpb, v_w_proj_a, v_w_proj_b, v_w_o, v_w_ffn_in, v_w_ffn_out):
    t, d = x.shape[1], x.shape[2]
    d_ff = w_ffn_out.shape[1] * N_DEV
    me = 4 * lax.axis_index("x") + 2 * lax.axis_index("y") + lax.axis_index("c")
    xt, tgt = x.reshape(t, d), loss_target.reshape(t, d)
    ones = _head_ones()

    shards = [s.astype(WIRE_DTYPE) for s in (w_in[0].T, w_ffn_in[0].T, w_proj_a[0].T, w_proj_b[0].T, w_o[0], w_ffn_out[0])]
    lands = [lax.dynamic_update_slice(lax.empty((N_DEV * s.shape[0], s.shape[1]), s.dtype), s, (me * s.shape[0], 0))
             for s in shards]

    c_all = _all_gather([jnp.pad(c, ((0, 7), (0, 0)))], name="gather_c")[0][::8]
    c_all = jnp.pad(c_all, ((0, LANES - N_DEV), (0, 0)))

    def mod_body(c_ref, w_ref, b_ref, o_ref, act_ref):
        act = _silu(c_ref[...])
        act_ref[...] = act
        hi, mid, lo = _split3(act)
        w = w_ref[...]
        whi, wmid, wlo = _split3(w)
        acc = _dot(hi, whi) + (_dot(hi, wmid) + _dot(mid, whi)) + (_dot(hi, wlo) + _dot(mid, wmid) + _dot(lo, whi))
        o_ref[...] = acc + b_ref[...]

    ncol = w_ada.shape[2]
    b_ada_mine = lax.dynamic_slice(b_ada, (0, me * ncol), (1, ncol))
    mod_part, c_act = pl.pallas_call(
        mod_body,
        name="ada_mod",
        out_shape=[jax.ShapeDtypeStruct((LANES, ncol), F32), jax.ShapeDtypeStruct((LANES, d), F32)],
        compiler_params=pltpu.CompilerParams(vmem_limit_bytes=_vmem(6 * d * ncol * 4)),
    )(c_all, w_ada[0], b_ada_mine)
    mod_all = _all_gather([mod_part[:N_DEV]], name="gather_mod")[0].reshape(N_DEV, N_DEV, ncol)
    mod = lax.dynamic_index_in_dim(mod_all, me, axis=1, keepdims=False).reshape(6, d)
    sh1, sc1, gt1, sh2, sc2, gt2 = [mod[i:i + 1] for i in range(6)]

    def norm_fwd(rows, vecs):
        (xv,), (g, sc, sh) = rows, vecs
        return [xv * _rms(xv) * g * (1.0 + sc) + sh], []

    w_in_handle, w_token = _split_start(shards[:1], lands[:1], gather="chip", after=mod, name="gather_w_in_start")
    (h,) = _rowmap(norm_fwd, [xt], [g_norm1 + w_token, sc1, sh1], [(d, MXU_DTYPE)], [], name="norm1")
    n_a, n_b = 3 * WA, 3 * WB
    (w_in_t,) = _split_wait(w_in_handle, h, gather="chip", name="gather_w_in_wait")[1]
    w_in_t = _forward_wait(_forward_start(w_in_t, name="gather_w_in_forward_start"), name="gather_w_in_forward_wait")
    w_handles, w_token = _split_start(shards[1:], lands[1:], gather=True, after=w_in_t, name="gather_weights_start")
    w_in_a, w_in_b, w_in_g = w_in_t[:n_a], w_in_t[n_a:n_a + n_b], w_in_t[n_a + n_b:]

    rot_c, rot_lo, rot_hi = _rot_tables(t)
    tile_g = lambda g, heads: jnp.tile(g, (1, heads))

    def qk_fwd(width, rotate):
        def fn(xv, rows, vecs):
            gq, gk, on = vecs
            qkv = []
            for i, g in enumerate((gq, gk)):
                xi = xv[:, i * width:(i + 1) * width]
                r = lax.rsqrt(_headsum(xi * xi, on) * (1.0 / HEAD_DIM) + EPS)
                yi = xi * r * g
                if rotate:
                    yi = _rot(yi, rows[0], rows[1], rows[2])
                qkv.append(yi)
            qkv.append(xv[:, 2 * width:])
            if not rotate:
                return [xv] + qkv, []
            groups = [jnp.concatenate([a[:, g * WB_OUT:(g + 1) * WB_OUT] for a in qkv], axis=1)
                      for g in range(len(DIL_CONFIGS))]
            return [xv] + groups, []
        return fn

    qkv_a, qa, ka, va = _mm_parts_rows(
        [(h, w_in_a, "nt")], qk_fwd(WA, False), [], [tile_g(g_qa, NA_HEADS) + w_token, tile_g(g_ka, NA_HEADS), ones],
        [(3 * WA, ACT_DTYPE)] + [(WA, MXU_DTYPE)] * 3, [], name="proj_a_qknorm")
    qkv_b, *qkv_groups = _mm_parts_rows(
        [(h, w_in_b, "nt")], qk_fwd(WB, True), [rot_c, rot_lo, rot_hi],
        [tile_g(g_qb, DIL_HEADS), tile_g(g_kb, DIL_HEADS), ones],
        [(3 * WB, ACT_DTYPE)] + [(3 * WB_OUT, MXU_DTYPE)] * len(DIL_CONFIGS), [], name="proj_b_qknorm")

    bias_tab = _na_bias_table(rpb[0])
    o_a, lse_a = _attn_fwd(qa, ka, va, kind="na", bias=bias_tab, name="na_fwd")

    grp = []
    for g, (_, dil) in enumerate(DIL_CONFIGS):
        qkv_c = _to_class(qkv_groups[g], dil)
        og, lg = _attn_fwd((qkv_c, 0), (qkv_c, 1), (qkv_c, 2), kind="dil", seg=t // dil, name=f"dil_fwd{g}")
        grp.append(dict(qkv=qkv_c, o=_from_class(og, dil), lse=_from_class(lg, dil), lse_c=lg, dil=dil))

    def merge_fwd(rows, _):
        o0, o1, o2, l0, l1, l2 = rows
        mx = jnp.maximum(jnp.maximum(l0, l1), l2)
        e0, e1, e2 = jnp.exp(l0 - mx), jnp.exp(l1 - mx), jnp.exp(l2 - mx)
        s = e0 + e1 + e2
        return (e0 / s) * o0 + (e1 / s) * o1 + (e2 / s) * o2

    w_pa_t, w_pb_t, w_o_f = _split_wait(w_handles[1:4], grp[-1]["lse_c"], gather=True, name="gather_w_out_wait")[1]
    def gate_fwd(prods, _, vecs, made):
        gv, pav, pbv = prods
        sg = jax.nn.sigmoid(gv + vecs[0])
        return [sg, pav, pbv, sg[:, :d] * pav + sg[:, d:] * pbv, made[0]], []

    gate_sig, pa, pb, merged, o_b = _mm_parts_rows(
        [(h, w_in_g, "nt"), (o_a, w_pa_t, "nt"), (merge_fwd, w_pb_t, "nt")], gate_fwd,
        [gr["o"] for gr in grp] + [gr["lse"] for gr in grp], [b_gate],
        [(2 * d, ACT_DTYPE), (d, ACT_DTYPE), (d, ACT_DTYPE), (d, MXU_DTYPE), (WB_OUT, F32)], [], separate=True,
        name="proj_gates_out_merge")
    def resid_norm(av, rows, vecs):
        (xv,), (gt, g, sc, sh) = rows, vecs
        x1v = xv + gt * av
        return [av, x1v, x1v * _rms(x1v) * g * (1.0 + sc) + sh], []

    att, x1, h2 = _mm_parts_rows([(merged, w_o_f)], resid_norm, [xt], [gt1, g_norm2, sc2, sh2],
                           [(d, F32), (d, F32), (d, MXU_DTYPE)], [], name="proj_o_resid_norm2")

    w_ffn_in_t, w_ffn_out_f = _split_wait([w_handles[0], w_handles[4]], h2, gather=True, name="gather_w_ffn_wait")[1]
    w_ffn_a, w_ffn_up = w_ffn_in_t[:d_ff], w_ffn_in_t[d_ff:]

    def swiglu_fwd(prods, _):
        a, up = prods
        return [a, up, _silu(a) * up]

    ua, uu, f = _mm_ew(h2, [w_ffn_a, w_ffn_up], swiglu_fwd, [], [ACT_DTYPE, ACT_DTYPE, MXU_DTYPE], name="ffn_in_swiglu")

    def loss_fn(yv, rows, vecs):
        (x1v, tv), gt = rows, vecs[0]
        err = x1v + gt * yv - tv
        dout = err * (1.0 / d)
        return [dout, dout * gt], [_colsum(err * err), _colsum(dout * yv)]

    dout, dy2, err2, dgt2 = _mm_parts_rows([(f, w_ffn_out_f)], loss_fn, [x1, tgt], [gt2], [(d, F32), (d, MXU_DTYPE)],
                                           [d, d], name="ffn_out_loss")

    dw_ffn_out = _mm(f, dy2, ta=True, out_dtype=WIRE_DTYPE, name="wgrad_ffn_out")
    def swiglu_bwd(prods, rows):
        (dfv,), (a, up) = prods, rows
        sg = jax.nn.sigmoid(a)
        return [dfv * up * (sg * (1.0 + a * (1.0 - sg))), dfv * (a * sg)]

    da, dup = _mm_ew(dy2, [w_ffn_out_f], swiglu_bwd, [ua, uu], [MXU_DTYPE, MXU_DTYPE], name="dgrad_ffn_out_swiglu_bwd")
    dw_ffn_in_t = _mm(da, h2, ta=True, into=(lax.empty((2 * d_ff, d), WIRE_DTYPE), 0), name="wgrad_ffn_in_a")
    dw_ffn_in_t = _mm(dup, h2, ta=True, into=(dw_ffn_in_t, d_ff), name="wgrad_ffn_in_up")
    land7 = lambda a: lax.empty((N_DEV - 1, a.shape[0] // N_DEV, a.shape[1]), a.dtype)
    own_block = lambda a: lax.dynamic_slice(a, (me * (a.shape[0] // N_DEV), 0), (a.shape[0] // N_DEV, a.shape[1]))
    g_ffn = [dw_ffn_in_t, dw_ffn_out]
    h_ffn, tok_ffn = _split_start(g_ffn, [land7(a) for a in g_ffn], gather=False, name="exchange_ffn_start")
    def norm_bwd(dh, xv, g, sc):
        r = _rms(xv)
        xh = xv * r
        dxh = dh * g * (1.0 + sc)
        dxv = r * (dxh - xh * jnp.mean(dxh * xh, axis=-1, keepdims=True))
        return dxv, [_colsum(dh), _colsum(dh * xh * g), _colsum(dh * xh * (1.0 + sc))]

    def norm2_bwd(dhv, rows, vecs):
        (x1v, dov, av), (g, sc, gt) = rows, vecs
        dxv, sums = norm_bwd(dhv, x1v, g, sc)
        dx1v = dov + dxv
        return [dx1v, dx1v * gt], sums + [_colsum(dx1v * av)]

    dx1, datt, dsh2, dsc2, dg2, dgt1 = _mm_parts_rows(
        [(da, w_ffn_a), (dup, w_ffn_up)], norm2_bwd, [x1, dout, att], [g_norm2 + tok_ffn, sc2, gt1],
        [(d, F32), (d, MXU_DTYPE)], [d] * 4, name="dgrad_ffn_in_norm2_bwd")
    dw_o = _mm(merged, datt, ta=True, out_dtype=WIRE_DTYPE, name="wgrad_o")
    def gate_bwd(dm, rows, _):
        sg, pav, pbv = rows
        ga, gb = sg[:, :d], sg[:, d:]
        dgp = jnp.concatenate([dm * pav * ga * (1.0 - ga), dm * pbv * gb * (1.0 - gb)], axis=1)
        return [dm * ga, dm * gb, dgp], [_colsum(dgp)]

    dpa, dpb, dgates, db_gate = _mm_parts_rows(
        [(datt, w_o_f.T)], gate_bwd, [gate_sig, pa, pb], [],
        [(d, MXU_DTYPE), (d, MXU_DTYPE), (2 * d, MXU_DTYPE)], [2 * d], name="dgrad_o_gate_bwd")
    dw_pa_t = _mm(dpa, o_a, ta=True, out_dtype=WIRE_DTYPE, name="wgrad_proj_a")
    dw_pb_t = _mm(dpb, o_b, ta=True, out_dtype=WIRE_DTYPE, name="wgrad_proj_b")
    g_out = [dw_pa_t, dw_pb_t, dw_o]
    h_out, tok_out = _split_start(g_out, [land7(a) for a in g_out], gather=False, name="exchange_out_start")
    def delta_a(doa, rows, vecs):
        return [doa, _headsum(doa * rows[0], vecs[0])], []

    do_a, dterm_a = _mm_parts_rows([(dpa, w_pa_t)], delta_a, [o_a], [ones + tok_out.astype(ones.dtype)],
                                   [(WA, F32), (WA, F32)], [], name="dgrad_proj_a_delta")
    dqa, dka, dva, dbias = _attn_bwd(qa, ka, va, do_a, dterm_a, lse_a, kind="na", bias=bias_tab, name="na_bwd")
    g_rpb = _na_bias_grad(dbias)

    def merge_bwd(dob, rows, vecs):
        o0, o1, o2, l0, l1, l2 = rows
        on = vecs[0]
        mx = jnp.maximum(jnp.maximum(l0, l1), l2)
        e0, e1, e2 = jnp.exp(l0 - mx), jnp.exp(l1 - mx), jnp.exp(l2 - mx)
        s = e0 + e1 + e2
        ws = [e0 / s, e1 / s, e2 / s]
        dws = [_headsum(dob * o, on) for o in (o0, o1, o2)]
        mean = ws[0] * dws[0] + ws[1] * dws[1] + ws[2] * dws[2]
        return [jnp.concatenate([w * dob, w * mean], axis=1) for w in ws], []

    mb = _mm_parts_rows([(dpb, w_pb_t)], merge_bwd, [gr["o"] for gr in grp] + [gr["lse"] for gr in grp], [ones],
                        [(2 * WB_OUT, F32)] * len(grp), [], name="dgrad_proj_b_merge_bwd")
    dqb, dkb, dvb = [], [], []
    for g, gr in enumerate(grp):
        dil, qkv_c = gr["dil"], gr["qkv"]
        dd_c = _to_class(mb[g], dil)
        dq, dk, dv = _attn_bwd((qkv_c, 0), (qkv_c, 1), (qkv_c, 2), (dd_c, 0), (dd_c, 1), (gr["lse_c"], 0),
                               kind="dil", seg=t // dil, name=f"dil_bwd{g}")
        dqb.append(_from_class(dq, dil))
        dkb.append(dk[0] if dil == 1 else _from_class(dk[0][0][0], dil))
        dvb.append(dv[0] if dil == 1 else _from_class(dv[0][0][0], dil))

    def qk_bwd(width, rotate, nparts):
        def fn(rows, vecs):
            gq, gk, on = vecs
            xv = rows[0]
            pos = 1
            if rotate:
                rc, rlo, rhi = rows[1:4]
                pos = 4
            cat = lambda parts: parts[0] if len(parts) == 1 else jnp.concatenate(parts, axis=1)
            ends = np.cumsum((pos,) + nparts)
            dq, dk, dv = [cat(rows[ends[i]:ends[i + 1]]) for i in range(3)]
            outs, sums = [], []
            for i, (dy, g) in enumerate(((dq, gq), (dk, gk))):
                if rotate:
                    dy = _rot(dy, rc, -rlo, -rhi)
                xi = xv[:, i * width:(i + 1) * width]
                r = lax.rsqrt(_headsum(xi * xi, on) * (1.0 / HEAD_DIM) + EPS)
                xh = xi * r
                dxh = dy * g
                outs.append(r * (dxh - xh * (_headsum(dxh * xh, on) * (1.0 / HEAD_DIM))))
                sums.append(_colsum(dy * xh))
            return [jnp.concatenate(outs + [dv], axis=1)], sums
        return fn

    dqkv_a, dg_qa, dg_ka = _rowmap(qk_bwd(WA, False, (1, len(dka), len(dva))), [qkv_a, dqa] + dka + dva,
                                   [tile_g(g_qa, NA_HEADS), tile_g(g_ka, NA_HEADS), ones],
                                   [(3 * WA, MXU_DTYPE)], [WA, WA], name="qknorm_a_bwd")
    dqkv_b, dg_qb, dg_kb = _rowmap(qk_bwd(WB, True, (3, 3, 3)), [qkv_b, rot_c, rot_lo, rot_hi] + dqb + dkb + dvb,
                                   [tile_g(g_qb, DIL_HEADS), tile_g(g_kb, DIL_HEADS), ones],
                                   [(3 * WB, MXU_DTYPE)], [WB, WB], name="qknorm_b_bwd")

    dw_in_t = jnp.concatenate([
        _mm(dqkv_a, h, ta=True, out_dtype=WIRE_DTYPE, name="wgrad_in_a"),
        _mm(dqkv_b, h, ta=True, out_dtype=WIRE_DTYPE, name="wgrad_in_b"),
        _mm(dgates, h, ta=True, out_dtype=WIRE_DTYPE, name="wgrad_in_gates")], axis=0)
    h_in, tok_in = _split_start([dw_in_t], [land7(dw_in_t)], gather=False, name="exchange_in_start")
    def norm1_bwd(dhv, rows, vecs):
        xv, dx1v = rows
        dxv, sums = norm_bwd(dhv, xv, vecs[0], vecs[1])
        return [dx1v + dxv], sums

    grad_x, dsh1, dsc1, dg1 = _mm_parts_rows(
        [(dqkv_a, w_in_a), (dqkv_b, w_in_b), (dgates, w_in_g)], norm1_bwd, [xt, dx1], [g_norm1 + tok_in, sc1],
        [(d, F32)], [d] * 3, name="dgrad_in_norm1_bwd")

    heads_sum = lambda a, heads: a.reshape(heads, HEAD_DIM).sum(axis=0)
    dmod = jnp.concatenate([dsh1, dsc1, dgt1, dsh2, dsc2, dgt2], axis=1)
    local_small = _pack_small(dict(
        b_ada=dmod, g_norm1=dg1, g_norm2=dg2, b_gate=db_gate, g_qa=heads_sum(dg_qa, NA_HEADS),
        g_ka=heads_sum(dg_ka, NA_HEADS), g_qb=heads_sum(dg_qb, DIL_HEADS), g_kb=heads_sum(dg_kb, DIL_HEADS),
        rpb=g_rpb, loss=(0.5 / d) * jnp.sum(err2)))
    srows = local_small.shape[0]
    small_all = _all_gather([local_small], name="gather_small")[0].reshape(N_DEV, srows, LANES)
    small_sum = _sum8(small_all[0], small_all[1:], name="sum_small")
    small_shapes = dict(b_ada=b_ada.shape, g_norm1=g_norm1.shape, g_norm2=g_norm2.shape, b_gate=b_gate.shape,
                        g_qa=g_qa.shape, g_ka=g_ka.shape, g_qb=g_qb.shape, g_kb=g_kb.shape, rpb=rpb.shape, loss=())
    small_w = dict(b_ada=b_ada, g_norm1=g_norm1, g_norm2=g_norm2, b_gate=b_gate, g_qa=g_qa, g_ka=g_ka, g_qb=g_qb,
                   g_kb=g_kb, rpb=rpb, loss=jnp.zeros((), F32))
    small_m = dict(b_ada=m_b_ada, g_norm1=m_g_norm1, g_norm2=m_g_norm2, b_gate=m_b_gate, g_qa=m_g_qa, g_ka=m_g_ka,
                   g_qb=m_g_qb, g_kb=m_g_kb, rpb=m_rpb, loss=jnp.zeros((), F32))
    small_v = dict(b_ada=v_b_ada, g_norm1=v_g_norm1, g_norm2=v_g_norm2, b_gate=v_b_gate, g_qa=v_g_qa, g_ka=v_g_ka,
                   g_qb=v_g_qb, g_kb=v_g_kb, rpb=v_rpb, loss=jnp.zeros((), F32))
    s_delta, s_m, s_v = _adamw(_pack_small(small_w), small_sum, _pack_small(small_m), _pack_small(small_v), name="adamw_small")
    gs = _unpack_small(small_sum, small_shapes)
    ds_, ms_, vs_ = [_unpack_small(a, small_shapes) for a in (s_delta, s_m, s_v)]

    dmod_all = small_all[:, :6 * d // LANES].reshape(N_DEV, 6 * d)
    dmod_mine = jnp.pad(lax.dynamic_slice(dmod_all, (0, me * ncol), (N_DEV, ncol)), ((0, LANES - N_DEV), (0, 0)))

    def wada_body(c_ref, dm_ref, o_ref):
        chi, cmid, clo = _split3(c_ref[...])
        dhi, dmid, dlo = _split3(dm_ref[...])
        o_ref[...] = (_dot_tn(chi, dhi) + (_dot_tn(chi, dmid) + _dot_tn(cmid, dhi))
                      + (_dot_tn(chi, dlo) + _dot_tn(cmid, dmid) + _dot_tn(clo, dhi)))

    g_w_ada = pl.pallas_call(
        wada_body,
        name="wgrad_ada",
        out_shape=jax.ShapeDtypeStruct((d, ncol), F32),
        compiler_params=pltpu.CompilerParams(vmem_limit_bytes=_vmem(4 * d * ncol * 4)),
    )(c_act, dmod_mine)

    sent, recv = _split_wait(h_in + h_ffn + h_out, small_sum, gather=False, name="exchange_wait")
    names = ("w_in", "w_ffn_in", "w_ffn_out", "w_proj_a", "w_proj_b", "w_o")
    transposed = (True, True, False, True, True, False)
    big_g = {}
    for nme, own, r, tr in zip(names, sent, recv, transposed):
        s = _sum8(own_block(own), r, name=f"sum_{nme}")
        big_g[nme] = s.T if tr else s
    big_g["w_ada"] = g_w_ada
    big_w = dict(w_ada=w_ada, w_in=w_in, w_proj_a=w_proj_a, w_proj_b=w_proj_b, w_o=w_o, w_ffn_in=w_ffn_in, w_ffn_out=w_ffn_out)
    big_m = dict(w_ada=m_w_ada, w_in=m_w_in, w_proj_a=m_w_proj_a, w_proj_b=m_w_proj_b, w_o=m_w_o, w_ffn_in=m_w_ffn_in, w_ffn_out=m_w_ffn_out)
    big_v = dict(w_ada=v_w_ada, w_in=v_w_in, w_proj_a=v_w_proj_a, w_proj_b=v_w_proj_b, w_o=v_w_o, w_ffn_in=v_w_ffn_in, w_ffn_out=v_w_ffn_out)
    grads, deltas, new_m, new_v = {}, {}, {}, {}
    for nme in big_w:
        dl, m1, v1 = _adamw(big_w[nme][0], big_g[nme], big_m[nme][0], big_v[nme][0], name=f"adamw_{nme}")
        grads[nme], deltas[nme], new_m[nme], new_v[nme] = big_g[nme][None], dl[None], m1[None], v1[None]
    for nme in _SMALL[:-1]:
        grads[nme], deltas[nme], new_m[nme], new_v[nme] = gs[nme], ds_[nme], ms_[nme], vs_[nme]

    order = ("w_ada", "b_ada", "g_norm1", "g_norm2", "w_in", "b_gate", "g_qa", "g_ka", "g_qb", "g_kb", "rpb",
             "w_proj_a", "w_proj_b", "w_o", "w_ffn_in", "w_ffn_out")
    return (gs["loss"], grad_x[None], *[grads[n] for n in order], *[deltas[n] for n in order],
            *[new_m[n] for n in order], *[new_v[n] for n in order])
```

```python
import functools

import numpy as np
import jax
import jax.numpy as jnp
from jax import lax
from jax.experimental import pallas as pl
from jax.experimental.pallas import tpu as pltpu

F32 = jnp.float32
MXU_DTYPE = jnp.bfloat16
WIRE_DTYPE = jnp.bfloat16
ACT_DTYPE = jnp.bfloat16

HEAD_DIM = 64
GRID_W = 64
NA_HEADS = 8
NA_KH = 8
NA_KW = 16
DIL_CONFIGS = ((128, 1), (512, 4), (2048, 16))
DIL_HEADS_PER_GROUP = 4
DIL_HEADS = DIL_HEADS_PER_GROUP * len(DIL_CONFIGS)
DIL_HALF = 64
ROT_DIM = HEAD_DIM // 4
ROPE_THETA = 500000.0
EPS = 1e-6
NEG_INF = -1e30
WA = NA_HEADS * HEAD_DIM
WB = DIL_HEADS * HEAD_DIM
WB_OUT = DIL_HEADS_PER_GROUP * HEAD_DIM
ADAM_LR = 0.001
ADAM_B1 = 0.9
ADAM_B2 = 0.999
ADAM_EPS = 1e-08
ADAM_WD = 0.01
ADAM_STEP = 10

N_DEV = 8
LANES = 128
VMEM_CAP = 60 * 2**20
VMEM_FLOOR = 56 * 2**20
MESH = pl.DeviceIdType.MESH
ANY = pl.BlockSpec(memory_space=pl.ANY)


def _vmem(nbytes):
    return int(min(VMEM_CAP, max(VMEM_FLOOR, nbytes * 5 // 4 + 4 * 2**20)))


def _pick(dim, cands):
    for c in cands:
        if c <= dim and dim % c == 0:
            return c
    return dim


def _nbytes(shape, dtype):
    return int(np.prod(shape)) * jnp.dtype(dtype).itemsize


def _dot(a, b, dims=((1,), (0,))):
    return lax.dot_general(a.astype(MXU_DTYPE), b.astype(MXU_DTYPE), (dims, ((), ())), preferred_element_type=F32)


def _dot_nt(a, b):
    return _dot(a, b, ((1,), (1,)))


def _dot_tn(a, b):
    return _dot(a, b, ((0,), (0,)))


def _split3(a):
    hi = a.astype(jnp.bfloat16)
    r1 = a - hi.astype(F32)
    mid = r1.astype(jnp.bfloat16)
    lo = (r1 - mid.astype(F32)).astype(jnp.bfloat16)
    return hi, mid, lo


def _silu(x):
    return x * jax.nn.sigmoid(x)


def _divisors(dim, unit):
    return [c for c in range(unit, dim + 1, unit) if dim % c == 0] or [dim]


def _mm_tiles(m, n, kdim, a_item, b_item, o_item, row_off=0, whole_n=False):
    step_us, hbm_bytes_per_us, flops_per_us, budget = 0.35, 3.0e6, 8.0e8, 40 * 2**20
    best = None
    for tm in _divisors(m, LANES):
        for tn in ([n] if whole_n else _divisors(n, LANES)):
            for tk in _divisors(kdim, LANES):
                if row_off % tm:
                    continue
                gm, gn, gk = m // tm, n // tn, kdim // tk
                vmem = 2 * (tm * tk * a_item + tk * tn * b_item + tm * tn * o_item) + 2 * (tm * tk + tk * tn)
                vmem += tm * tn * 4 * ((1 if gk > 1 else 0) + 1)
                if vmem > budget:
                    continue
                a_reads = m * kdim * a_item * (gn if gk > 1 else 1)
                traffic = a_reads + kdim * n * b_item * gm + m * n * o_item
                cost = gm * gn * gk * step_us + max(traffic / hbm_bytes_per_us, 2.0 * m * n * kdim / flops_per_us)
                if best is None or cost < best[0]:
                    best = (cost, tm, tn, tk)
    return best[1:]


def _mm(a, b, *, name, ta=False, tb=False, out_dtype=F32, into=None):
    if ta:
        kdim, m = a.shape
    else:
        m, kdim = a.shape
    n = b.shape[0] if tb else b.shape[1]
    assert b.shape[1 if tb else 0] == kdim
    buf, row_off = into if into is not None else (None, 0)
    if buf is not None:
        out_dtype = buf.dtype
    tm, tn, tk = _mm_tiles(m, n, kdim, a.dtype.itemsize, b.dtype.itemsize, jnp.dtype(out_dtype).itemsize, row_off,
                           whole_n=ta and n <= 8 * LANES)
    gm, gn, gk = m // tm, n // tn, kdim // tk
    ob = row_off // tm

    a_spec = pl.BlockSpec((tk, tm), lambda i, j, k: (k, i)) if ta else pl.BlockSpec((tm, tk), lambda i, j, k: (i, k))
    b_spec = pl.BlockSpec((tn, tk), lambda i, j, k: (j, k)) if tb else pl.BlockSpec((tk, tn), lambda i, j, k: (k, j))
    o_spec = pl.BlockSpec((tm, tn), lambda i, j, k: (i + ob, j))
    a_dims = (0,) if ta else (1,)
    b_dims = (1,) if tb else (0,)

    def body(a_ref, b_ref, *rest):
        o_ref, scratch = rest[-1 - (gk > 1)], rest[-(gk > 1):] if gk > 1 else ()
        if gk == 1:
            o_ref[...] = _dot(a_ref[...], b_ref[...], (a_dims, b_dims)).astype(o_ref.dtype)
            return
        (acc_ref,) = scratch
        k = pl.program_id(2)

        @pl.when(k == 0)
        def _():
            acc_ref[...] = jnp.zeros_like(acc_ref)

        acc_ref[...] += _dot(a_ref[...], b_ref[...], (a_dims, b_dims))

        @pl.when(k == gk - 1)
        def _():
            o_ref[...] = acc_ref[...].astype(o_ref.dtype)

    est = 2 * (tm * tk * a.dtype.itemsize + tk * tn * b.dtype.itemsize + tm * tn * jnp.dtype(out_dtype).itemsize)
    est += tm * tn * 4 + 2 * (tm * tk + tk * tn) * 2
    return pl.pallas_call(
        body,
        name=name,
        grid=(gm, gn, gk),
        in_specs=[a_spec, b_spec] + ([ANY] if buf is not None else []),
        out_specs=o_spec,
        out_shape=jax.ShapeDtypeStruct((m, n) if buf is None else buf.shape, out_dtype),
        input_output_aliases={2: 0} if buf is not None else {},
        scratch_shapes=[pltpu.VMEM((tm, tn), F32)] if gk > 1 else [],
        compiler_params=pltpu.CompilerParams(
            dimension_semantics=("parallel", "parallel", "arbitrary"), vmem_limit_bytes=_vmem(est)
        ),
    )(*((a, b) if buf is None else (a, b, buf)))


def _resident(shape):
    return pl.BlockSpec(shape, lambda i: (0,) * len(shape), pipeline_mode=pl.Buffered(1))


def _row_tile(m, fixed_bytes, bytes_per_row, budget=50 * 2**20):
    fits = [tm for tm in _divisors(m, LANES) if fixed_bytes + tm * bytes_per_row <= budget]
    return max(fits) if fits else _divisors(m, LANES)[0]


def _mm_parts_rows(parts, fn, rows, vecs, outs, reds, *, name, separate=False):
    parts = [(p[0], p[1], len(p) > 2) for p in parts]
    rows = [r if isinstance(r, tuple) else (r, r.shape[1], 0) for r in rows]
    m, n = parts[0][0].shape[0], parts[0][1].shape[0 if parts[0][2] else 1]
    npart, nr, nv, no = len(parts), len(rows), len(vecs), len(outs)
    row_bytes = sum(w * r.dtype.itemsize for r, w, _ in rows) + sum(w * jnp.dtype(dt).itemsize for (w, dt) in outs)
    a_row_bytes = sum(a.shape[1] * a.dtype.itemsize for a, _, _ in parts)
    fixed = sum(_nbytes(b.shape, b.dtype) for _, b, _ in parts)
    per_row = 2 * (a_row_bytes + row_bytes) + n * 4 * 5
    tm = _row_tile(m, fixed, per_row)
    sub = min(tm, 2 * LANES)

    def body(*refs):
        ab = refs[:2 * npart]
        row_refs, vec_refs = refs[2 * npart:2 * npart + nr], refs[2 * npart + nr:2 * npart + nr + nv]
        out_refs = refs[2 * npart + nr + nv:2 * npart + nr + nv + no]
        red_refs = refs[2 * npart + nr + nv + no:]
        if red_refs:
            @pl.when(pl.program_id(0) == 0)
            def _():
                for ref in red_refs:
                    ref[...] = jnp.zeros_like(ref)

        vecs_v = [v[...] for v in vec_refs]
        for s0 in range(0, tm, sub):
            sl = slice(s0, s0 + sub)
            prods = [(_dot_nt if nt else _dot)(ab[2 * p][sl, :], ab[2 * p + 1][...]) for p, (_, _, nt) in enumerate(parts)]
            r = prods if separate else functools.reduce(lambda u, v: u + v, prods)
            o, rd = fn(r, [x[sl, :].astype(F32) for x in row_refs], vecs_v)
            for ref, val in zip(out_refs, o):
                ref[sl, :] = val.astype(ref.dtype)
            for ref, val in zip(red_refs, rd):
                ref[...] += val

    in_specs, operands = [], []
    for a, b, _ in parts:
        in_specs += [pl.BlockSpec((tm, a.shape[1]), lambda i: (i, 0)), _resident(b.shape)]
        operands += [a, b]
    in_specs += [pl.BlockSpec((tm, w), functools.partial(lambda cb, i: (i, cb), cb)) for _, w, cb in rows]
    in_specs += [pl.BlockSpec(v.shape, functools.partial(lambda nd, i: (0,) * nd, v.ndim)) for v in vecs]
    out_specs = [pl.BlockSpec((tm, w), lambda i: (i, 0)) for (w, _) in outs]
    out_specs += [pl.BlockSpec((1, w), lambda i: (0, 0)) for w in reds]
    out_shape = [jax.ShapeDtypeStruct((m, w), dt) for (w, dt) in outs] + [jax.ShapeDtypeStruct((1, w), F32) for w in reds]
    return pl.pallas_call(
        body,
        name=name,
        grid=(m // tm,),
        in_specs=in_specs,
        out_specs=out_specs,
        out_shape=out_shape,
        compiler_params=pltpu.CompilerParams(dimension_semantics=("arbitrary",), vmem_limit_bytes=_vmem(fixed + tm * per_row)),
    )(*operands, *[r for r, _, _ in rows], *vecs)


def _mm_ew(a, bs, fn, rows, outs, *, name):
    m, kdim = a.shape
    n = bs[0].shape[0]
    nb, nr, no = len(bs), len(rows), len(outs)
    cw = _pick(n, (2 * LANES, LANES))
    fixed = nb * n * kdim * bs[0].dtype.itemsize
    per_row = 2 * (kdim * a.dtype.itemsize + n * (sum(r.dtype.itemsize for r in rows) + sum(jnp.dtype(dt).itemsize for dt in outs)))
    per_row += cw * 4 * 4 * (nb + 4)
    tm = _row_tile(m, fixed, per_row)

    def body(*refs):
        a_ref, b_refs = refs[0], refs[1:1 + nb]
        row_refs, out_refs = refs[1 + nb:1 + nb + nr], refs[1 + nb + nr:]
        av = a_ref[...]
        for c0 in range(0, n, cw):
            cols = slice(c0, c0 + cw)
            o = fn([_dot_nt(av, b[cols, :]) for b in b_refs], [x[:, cols].astype(F32) for x in row_refs])
            for ref, val in zip(out_refs, o):
                ref[:, cols] = val.astype(ref.dtype)

    tile = pl.BlockSpec((tm, n), lambda i: (i, 0))
    return pl.pallas_call(
        body,
        name=name,
        grid=(m // tm,),
        in_specs=[pl.BlockSpec((tm, kdim), lambda i: (i, 0))] + [_resident((n, kdim))] * nb + [tile] * nr,
        out_specs=[tile] * no,
        out_shape=[jax.ShapeDtypeStruct((m, n), dt) for dt in outs],
        compiler_params=pltpu.CompilerParams(dimension_semantics=("parallel",), vmem_limit_bytes=_vmem(fixed + tm * per_row)),
    )(a, *bs, *rows)


def _rowmap(fn, rows, vecs, outs, reds, *, name, tm=None):
    norm = []
    for r in rows:
        if not isinstance(r, tuple):
            norm.append((r, r.shape[1], 0, None))
        elif len(r) == 2:
            norm.append((r[0], r[0].shape[2], 0, r[1]))
        else:
            norm.append((r[0], r[1], r[2], None))
    rows = norm
    t = rows[0][0].shape[-2]
    if tm is None:
        per_row = 2 * sum(w * a.dtype.itemsize for (a, w, _, _) in rows) + 2 * sum(w * jnp.dtype(d).itemsize for (w, d) in outs)
        per_row += 3 * 4 * max([w for (_, w, _, _) in rows] + [w for (w, _) in outs])
        tm = max(8, min(1024, (40 * 2**20) // per_row))
    tm = _pick(t, tuple(c for c in (1024, 512, 256, 128, 64, 32, 16, 8) if c <= tm))
    nr, nv, no = len(rows), len(vecs), len(outs)

    def body(*refs):
        row_refs, vec_refs = refs[:nr], refs[nr:nr + nv]
        out_refs, red_refs = refs[nr + nv:nr + nv + no], refs[nr + nv + no:]
        o, rd = fn([r[...].astype(F32) for r in row_refs], [v[...] for v in vec_refs])
        for ref, val in zip(out_refs, o):
            ref[...] = val.astype(ref.dtype)
        if red_refs:
            @pl.when(pl.program_id(0) == 0)
            def _():
                for ref in red_refs:
                    ref[...] = jnp.zeros_like(ref)

            for ref, val in zip(red_refs, rd):
                ref[...] += val

    in_specs = [pl.BlockSpec((tm, w), functools.partial(lambda cb, i: (i, cb), cb)) if lead is None
                else pl.BlockSpec((None, tm, w), functools.partial(lambda ld, i: (ld, i, 0), lead)) for (_, w, cb, lead) in rows]
    in_specs += [pl.BlockSpec(v.shape, functools.partial(lambda nd, i: (0,) * nd, v.ndim)) for v in vecs]
    out_specs = [pl.BlockSpec((tm, w), lambda i: (i, 0)) for (w, _) in outs]
    out_specs += [pl.BlockSpec((1, w), lambda i: (0, 0)) for w in reds]
    out_shape = [jax.ShapeDtypeStruct((t, w), d) for (w, d) in outs]
    out_shape += [jax.ShapeDtypeStruct((1, w), F32) for w in reds]
    est = 2 * sum(tm * w * a.dtype.itemsize for (a, w, _, _) in rows) + 2 * sum(_nbytes(v.shape, v.dtype) for v in vecs)
    est += 2 * sum(tm * w * jnp.dtype(d).itemsize for (w, d) in outs)
    est += 6 * tm * max([w for (_, w, _, _) in rows] + [w for (w, _) in outs]) * 4
    return pl.pallas_call(
        body,
        name=name,
        grid=(t // tm,),
        in_specs=in_specs,
        out_specs=out_specs,
        out_shape=out_shape,
        compiler_params=pltpu.CompilerParams(dimension_semantics=("arbitrary",), vmem_limit_bytes=_vmem(est)),
    )(*[r[0] for r in rows], *vecs)


def _colsum(v):
    return jnp.sum(v, axis=0, keepdims=True)


def _head_ones():
    i = np.arange(LANES)
    return jnp.asarray((i[:, None] // HEAD_DIM) == (i[None, :] // HEAD_DIM), MXU_DTYPE)


def _headsum(y, ones):
    parts = []
    for j in range(y.shape[1] // LANES):
        c = y[:, j * LANES:(j + 1) * LANES]
        hi = c.astype(MXU_DTYPE)
        lo = c - hi.astype(F32)
        parts.append(_dot(hi, ones) + _dot(lo, ones))
    return parts[0] if len(parts) == 1 else jnp.concatenate(parts, axis=1)


def _rot(y, c, s_lo, s_hi):
    parts = []
    for j in range(y.shape[1] // LANES):
        yc = y[:, j * LANES:(j + 1) * LANES]
        parts.append(yc * c + pltpu.roll(yc, LANES - ROT_DIM // 2, 1) * s_lo + pltpu.roll(yc, ROT_DIM // 2, 1) * s_hi)
    return parts[0] if len(parts) == 1 else jnp.concatenate(parts, axis=1)


def _rot_tables(t):
    half = ROT_DIM // 2
    inv_freq = ROPE_THETA ** (-(jnp.arange(half, dtype=F32) * 2.0) / ROT_DIM)
    ang = jnp.arange(t).astype(F32)[:, None] * inv_freq[None, :]
    cos, sin = jnp.cos(ang), jnp.sin(ang)
    z = lambda w: jnp.zeros((t, w), F32)
    c = jnp.concatenate([cos, cos, jnp.ones((t, HEAD_DIM - ROT_DIM), F32)], axis=1)
    s_lo = jnp.concatenate([-sin, z(HEAD_DIM - half)], axis=1)
    s_hi = jnp.concatenate([z(half), sin, z(HEAD_DIM - ROT_DIM)], axis=1)
    return [jnp.tile(a, (1, LANES // HEAD_DIM)) for a in (c, s_lo, s_hi)]


def _rms(x):
    return lax.rsqrt(jnp.mean(x * x, axis=-1, keepdims=True) + EPS)


def _window(kind, n, bq, t, seg):
    if kind == "na":
        rows = t // GRID_W
        rs = jnp.clip(n - NA_KH // 2, 0, rows - NA_KH)
        return rs
    nk = bq + 2 * DIL_HALF
    return jnp.clip(n * bq - DIL_HALF, 0, t - nk)


def _dil_mask(n, bq, nk, ws, seg):
    qi = n * bq + lax.broadcasted_iota(jnp.int32, (bq, nk), 0)
    ki = ws + lax.broadcasted_iota(jnp.int32, (bq, nk), 1)
    shift = int(np.log2(seg))
    return (jnp.abs(ki - qi) <= DIL_HALF) & ((ki >> shift) == (qi >> shift))


HS = 4
QW = HS * HEAD_DIM


def _head_of_lane(width=QW):
    return lax.broadcasted_iota(jnp.int32, (1, width), 1) // HEAD_DIM


def _stack_heads(a):
    head = _head_of_lane()
    return jnp.concatenate([jnp.where(head == e, a, jnp.zeros_like(a)) for e in range(HS)], axis=0)


def _unstack_heads(a, bq):
    head = _head_of_lane()
    out = jnp.zeros((bq, QW), a.dtype)
    for e in range(HS):
        out = jnp.where(head == e, a[e * bq:(e + 1) * bq], out)
    return out


def _stack_cols(blk, bq):
    head = _head_of_lane()
    return jnp.concatenate(
        [jnp.max(jnp.where(head == e, blk, -jnp.inf), axis=1, keepdims=True) for e in range(HS)], axis=0)


def _attn_geometry(kind):
    if kind == "na":
        return GRID_W, NA_KH * GRID_W, 16
    bq = 128
    return bq, bq + 2 * DIL_HALF, 8


def _attn_scores(kind, n, bq, nk, t, seg, qs, k_ref, b_ref):
    scale = HEAD_DIM ** -0.5
    if kind == "na":
        rs = _window(kind, n, bq, t, seg)
        ws = pl.multiple_of(rs * GRID_W, GRID_W)
        ro0 = rs - n + (NA_KH - 1)
        s = _dot_nt(qs, k_ref[pl.ds(ws, nk), :]) * scale
        s = s + jnp.concatenate(
            [jnp.concatenate([b_ref[e, ro0 + 2 * i] for i in range(NA_KH // 2)], axis=1) for e in range(HS)], axis=0)
        return s, ws, ro0
    ws = pl.multiple_of(_window(kind, n, bq, t, seg), DIL_HALF)
    mask = _dil_mask(n, bq, nk, ws, seg)
    s = _dot_nt(qs, k_ref[pl.ds(ws, nk), :]) * scale
    s = jnp.where(jnp.concatenate([mask] * HS, axis=0), s, NEG_INF)
    return s, ws, None


def _col_operands(*ops):
    pairs = [op if isinstance(op, tuple) else (op, 0) for op in ops]
    width = QW if isinstance(ops[0], tuple) else ops[0].shape[1]
    return (*pairs, width)


def _q_block(rows, col):
    return pl.BlockSpec((rows, QW), lambda j, n: (n, j + col))


def _kv_resident(t, col):
    return pl.BlockSpec((t, QW), lambda j, n: (0, j + col))


def _attn_fwd(q, k, v, *, kind, name, bias=None, seg=None):
    (q, cq), (k, ck), (v, cv), w = _col_operands(q, k, v)
    t = q.shape[0]
    quads = w // QW
    bq, nk, sub = _attn_geometry(kind)
    nq = t // (bq * sub)

    def body(*refs):
        if kind == "na":
            q_ref, k_ref, v_ref, b_ref, o_ref, l_ref = refs
        else:
            (q_ref, k_ref, v_ref, o_ref, l_ref), b_ref = refs, None
        for i in range(sub):
            n = pl.program_id(1) * sub + i
            rows = slice(i * bq, (i + 1) * bq)
            s, ws, _ = _attn_scores(kind, n, bq, nk, t, seg, _stack_heads(q_ref[rows, :]), k_ref, b_ref)
            m = jnp.max(s, axis=1, keepdims=True)
            p = jnp.exp(s - m)
            l = jnp.sum(p, axis=1, keepdims=True)
            o_ref[rows, :] = _unstack_heads(_dot(p / l, v_ref[pl.ds(ws, nk), :]), bq)
            l_ref[rows, :] = _unstack_heads(jnp.broadcast_to(m + jnp.log(l), (HS * bq, QW)), bq)

    blk = pl.BlockSpec((bq * sub, QW), lambda j, n: (n, j))
    in_specs = [_q_block(bq * sub, cq), _kv_resident(t, ck), _kv_resident(t, cv)]
    operands = [q, k, v]
    est = 4 * t * QW * q.dtype.itemsize + 12 * sub * HS * bq * nk * 4
    if kind == "na":
        in_specs.append(pl.BlockSpec((HS,) + bias.shape[1:], lambda j, n: (j, 0, 0, 0)))
        operands.append(bias)
        est += 2 * _nbytes((HS,) + bias.shape[1:], F32)
    return pl.pallas_call(
        body,
        name=name,
        grid=(quads, nq),
        in_specs=in_specs,
        out_specs=[blk, blk],
        out_shape=[jax.ShapeDtypeStruct((t, w), F32)] * 2,
        compiler_params=pltpu.CompilerParams(dimension_semantics=("arbitrary", "arbitrary"), vmem_limit_bytes=_vmem(est)),
    )(*operands)


def _attn_bwd(q, k, v, do, dterm, lse, *, kind, name, bias=None, seg=None):
    (q, cq), (k, ck), (v, cv), (do, cdo), (dterm, cdt), (lse, cl), w = _col_operands(q, k, v, do, dterm, lse)
    t = q.shape[0]
    quads = w // QW
    bq, nk, sub = _attn_geometry(kind)
    nq = t // (bq * sub)
    scale = HEAD_DIM ** -0.5

    def body(*refs):
        if kind == "na":
            q_ref, k_ref, v_ref, do_ref, dt_ref, l_ref, b_ref, dq_ref, dk_hbm, dv_hbm, db_ref, dk_acc, dv_acc, sem = refs
        else:
            q_ref, k_ref, v_ref, do_ref, dt_ref, l_ref, dq_ref, dk_hbm, dv_hbm, dk_acc, dv_acc, sem = refs
            b_ref = None
        j, step = pl.program_id(0), pl.program_id(1)

        @pl.when(step == 0)
        def _():
            dk_acc[...] = jnp.zeros_like(dk_acc)
            dv_acc[...] = jnp.zeros_like(dv_acc)
            if kind == "na":
                db_ref[...] = jnp.zeros_like(db_ref)

        for b in range(sub):
            n = step * sub + b
            rows = slice(b * bq, (b + 1) * bq)
            qs = _stack_heads(q_ref[rows, :])
            dos = _stack_heads(do_ref[rows, :])
            s, ws, ro0 = _attn_scores(kind, n, bq, nk, t, seg, qs, k_ref, b_ref)
            p = jnp.exp(s - _stack_cols(l_ref[rows, :], bq))
            dp = _dot_nt(dos, v_ref[pl.ds(ws, nk), :])
            ds = p * (dp - _stack_cols(dt_ref[rows, :], bq))
            if kind == "na":
                for e in range(HS):
                    for i in range(NA_KH // 2):
                        db_ref[e, ro0 + 2 * i] += ds[e * bq:(e + 1) * bq, i * LANES:(i + 1) * LANES]
            dsc = ds * scale
            dq_ref[rows, :] = _unstack_heads(_dot(dsc, k_ref[pl.ds(ws, nk), :]), bq)
            dk_acc[pl.ds(ws, nk), :] += _dot_tn(dsc, qs)
            dv_acc[pl.ds(ws, nk), :] += _dot_tn(p, dos)

        @pl.when(step == nq - 1)
        def _():
            ck = pltpu.make_async_copy(dk_acc, dk_hbm.at[j], sem.at[0])
            cv = pltpu.make_async_copy(dv_acc, dv_hbm.at[j], sem.at[1])
            ck.start()
            cv.start()
            ck.wait()
            cv.wait()

    blk = pl.BlockSpec((bq * sub, QW), lambda j, n: (n, j))
    in_specs = [_q_block(bq * sub, cq), _kv_resident(t, ck), _kv_resident(t, cv)] + [_q_block(bq * sub, c) for c in (cdo, cdt, cl)]
    operands = [q, k, v, do, dterm, lse]
    out_specs = [blk, ANY, ANY]
    out_shape = [jax.ShapeDtypeStruct((t, w), F32)] + [jax.ShapeDtypeStruct((quads, t, QW), F32)] * 2
    est = 4 * t * QW * q.dtype.itemsize + 2 * t * QW * 4 + 16 * sub * HS * bq * nk * 4
    if kind == "na":
        bspec = pl.BlockSpec((HS,) + bias.shape[1:], lambda j, n: (j, 0, 0, 0))
        in_specs.append(bspec)
        operands.append(bias)
        out_specs.append(bspec)
        out_shape.append(jax.ShapeDtypeStruct(bias.shape, F32))
        est += 4 * _nbytes((HS,) + bias.shape[1:], F32)
    res_ = pl.pallas_call(
        body,
        name=name,
        grid=(quads, nq),
        in_specs=in_specs,
        out_specs=out_specs,
        out_shape=out_shape,
        scratch_shapes=[pltpu.VMEM((t, QW), F32), pltpu.VMEM((t, QW), F32), pltpu.SemaphoreType.DMA((2,))],
        compiler_params=pltpu.CompilerParams(dimension_semantics=("arbitrary", "arbitrary"), vmem_limit_bytes=_vmem(est)),
    )(*operands)
    unquad = lambda a: [(a, i) for i in range(quads)]
    return (res_[0], unquad(res_[1]), unquad(res_[2])) + tuple(res_[3:])


def _na_onehot():
    qc = np.arange(GRID_W)[:, None]
    kc = np.arange(GRID_W)[None, :]
    start = np.clip(qc - NA_KW // 2, 0, GRID_W - NA_KW)
    inwin = (kc >= start) & (kc < start + NA_KW)
    off = kc - qc + (NA_KW - 1)
    e_mat = np.zeros((2, 32, GRID_W, 2, GRID_W), np.float32)
    for e in range(2):
        for c in range(2 * NA_KW - 1):
            e_mat[e, c, :, e, :] = (off == c) & inwin
    neg = np.where(inwin, 0.0, NEG_INF).astype(np.float32)
    neg = np.broadcast_to(neg[:, None, :], (GRID_W, 2, GRID_W)).reshape(1, GRID_W * LANES)
    return jnp.asarray(e_mat.reshape(64, GRID_W * LANES), MXU_DTYPE), jnp.asarray(neg)


def _na_rowpairs(rpb):
    p = jnp.pad(rpb, ((0, 0), (0, 0), (0, 1)))
    return jnp.concatenate([p[:, :-1], p[:, 1:]], axis=-1).reshape(NA_HEADS * (2 * NA_KH - 2), 64)


def _na_bias_table(rpb):
    r2 = _na_rowpairs(rpb)
    e_mat, neg = _na_onehot()

    def body(r_ref, e_ref, n_ref, o_ref):
        hi, mid, lo = _split3(r_ref[...])
        e = e_ref[...]
        o_ref[...] = _dot(hi, e) + _dot(mid, e) + _dot(lo, e) + n_ref[...]

    out = pl.pallas_call(
        body,
        name="na_bias_table",
        out_shape=jax.ShapeDtypeStruct((r2.shape[0], GRID_W * LANES), F32),
        compiler_params=pltpu.CompilerParams(vmem_limit_bytes=_vmem(6 * r2.shape[0] * GRID_W * LANES * 4)),
    )(r2, e_mat, neg)
    return out.reshape(NA_HEADS, 2 * NA_KH - 2, GRID_W, LANES)


def _na_bias_grad(dbt):
    e_mat, _ = _na_onehot()
    flat = dbt.reshape(NA_HEADS * (2 * NA_KH - 2), GRID_W * LANES)

    def body(d_ref, e_ref, o_ref):
        hi, mid, lo = _split3(d_ref[...])
        e = e_ref[...]
        o_ref[...] = _dot_nt(hi, e) + _dot_nt(mid, e) + _dot_nt(lo, e)

    g = pl.pallas_call(
        body,
        name="na_bias_grad",
        out_shape=jax.ShapeDtypeStruct((flat.shape[0], 64), F32),
        compiler_params=pltpu.CompilerParams(vmem_limit_bytes=_vmem(6 * flat.shape[0] * GRID_W * LANES * 4)),
    )(flat, e_mat)
    g = g.reshape(NA_HEADS, 2 * NA_KH - 2, 2, 32)[..., :2 * NA_KW - 1]
    first = jnp.pad(g[:, :, 0], ((0, 0), (0, 1), (0, 0)))
    second = jnp.pad(g[:, :, 1], ((0, 0), (1, 0), (0, 0)))
    return first + second


def _all_gather(arrs, *, name):
    na = len(arrs)

    def body(*refs):
        ins, outs = refs[:na], refs[na:2 * na]
        send_sems, recv_sems, local_sems = refs[2 * na:]
        x, y, c = lax.axis_index("x"), lax.axis_index("y"), lax.axis_index("c")
        me, sibling = (x, y, c), (x, y, 1 - c)
        chips = [(1 - x, y), (x, 1 - y), (1 - x, 1 - y)]

        def rows(a, px, py, pc):
            r = ins[a].shape[0]
            return outs[a].at[pl.ds((4 * px + 2 * py + pc) * r, r), :]

        def copy(a, k, block, to, src=None):
            return pltpu.make_async_remote_copy(
                src_ref=rows(a, *block) if src is None else src, dst_ref=rows(a, *block),
                send_sem=send_sems.at[a, k], recv_sem=recv_sems.at[a, k], device_id=to, device_id_type=MESH)

        mine = [pltpu.make_async_copy(ins[a], rows(a, *me), local_sems.at[a]) for a in range(na)]
        for cp in mine:
            cp.start()
        first = []
        for a in range(na):
            first.append(copy(a, 0, me, sibling, src=ins[a]))
            first += [copy(a, 1 + j, me, (*chip, c), src=ins[a]) for j, chip in enumerate(chips)]
        for cp in first:
            cp.start()
        passed = []
        for j, chip in enumerate(chips):
            for a in range(na):
                copy(a, 1 + j, (*chip, c), me).wait_recv()
                cp = copy(a, 4 + j, (*chip, c), sibling)
                cp.start()
                passed.append(cp)
        for a in range(na):
            copy(a, 0, sibling, me).wait_recv()
        for j, chip in enumerate(chips):
            for a in range(na):
                copy(a, 4 + j, (*chip, 1 - c), me).wait_recv()
        for cp in first + passed:
            cp.wait_send()
        for cp in mine:
            cp.wait()

    return pl.pallas_call(
        body,
        name=name,
        in_specs=[ANY] * na,
        out_specs=[ANY] * na,
        out_shape=[jax.ShapeDtypeStruct((N_DEV * a.shape[0], a.shape[1]), a.dtype) for a in arrs],
        scratch_shapes=[pltpu.SemaphoreType.DMA((na, 7)), pltpu.SemaphoreType.DMA((na, 7)), pltpu.SemaphoreType.DMA((na,))],
    )(*arrs)


HBM = pl.BlockSpec(memory_space=pltpu.HBM)
SEM = pl.BlockSpec(memory_space=pltpu.SEMAPHORE)
EFFECT = pltpu.SideEffectType.DATAFLOW_SIDE_EFFECTING


def _peer_of(k):
    x, y, c = lax.axis_index("x"), lax.axis_index("y"), lax.axis_index("c")
    return x ^ ((k >> 2) & 1), y ^ ((k >> 1) & 1), c ^ (k & 1)


def _split_copies(gather, src_ref, land_ref, send_sems, recv_sems):
    x, y, c = lax.axis_index("x"), lax.axis_index("y"), lax.axis_index("c")
    my = 4 * x + 2 * y + c
    r = src_ref.shape[0] if gather else src_ref.shape[0] // N_DEV
    copies = []
    for k in ((1, 2, 4, 6) if gather == "chip" else range(1, N_DEV)):
        px, py, pc = _peer_of(k)
        if gather:
            src, dst = src_ref, land_ref.at[pl.ds(my * r, r), :]
        else:
            src, dst = src_ref.at[pl.ds((4 * px + 2 * py + pc) * r, r), :], land_ref.at[k - 1]
        copies.append(pltpu.make_async_remote_copy(
            src_ref=src, dst_ref=dst, send_sem=send_sems.at[k - 1], recv_sem=recv_sems.at[k - 1],
            device_id=(px, py, pc), device_id_type=MESH))
    return copies


def _split_start(srcs, lands, *, gather, name, after=None):
    na = len(srcs)
    extra = [] if after is None else [after]

    def body(*refs):
        src_refs, land_refs = refs[:na], refs[na:2 * na]
        outs = refs[2 * na + len(extra):]
        for a in range(na):
            for cp in _split_copies(gather, src_refs[a], land_refs[a], outs[4 * a], outs[4 * a + 1]):
                cp.start()
        outs[4 * na][...] = jnp.zeros_like(outs[4 * na])

    out_shape, out_specs, aliases = [], [], {}
    for a in range(na):
        out_shape += [pltpu.SemaphoreType.DMA((N_DEV - 1,)), pltpu.SemaphoreType.DMA((N_DEV - 1,)),
                      pltpu.HBM(srcs[a].shape, srcs[a].dtype), pltpu.HBM(lands[a].shape, lands[a].dtype)]
        out_specs += [SEM, SEM, HBM, HBM]
        aliases[a] = 4 * a + 2
        aliases[na + a] = 4 * a + 3
    out_shape.append(jax.ShapeDtypeStruct((8, LANES), F32))
    out_specs.append(pl.BlockSpec(memory_space=pltpu.VMEM))
    res = pl.pallas_call(
        body,
        name=name,
        out_shape=tuple(out_shape),
        in_specs=[HBM] * (2 * na) + [ANY] * len(extra),
        out_specs=tuple(out_specs),
        input_output_aliases=aliases,
        compiler_params=pltpu.CompilerParams(has_side_effects=EFFECT),
    )(*[pltpu.with_memory_space_constraint(a, pltpu.HBM) for a in list(srcs) + list(lands)], *extra)
    return [tuple(res[4 * a:4 * a + 4]) for a in range(na)], res[4 * na][0, 0]


def _split_wait(handles, after, *, gather, name):
    na = len(handles)

    def body(*refs):
        src_refs, land_refs = refs[:na], refs[na:2 * na]
        sems = refs[2 * na:4 * na]
        for a in range(na):
            for cp in _split_copies(gather, src_refs[a], land_refs[a], sems[2 * a], sems[2 * a + 1]):
                cp.wait_send()
                cp.wait_recv()

    srcs = [h[2] for h in handles]
    lands = [h[3] for h in handles]
    sems = [s for h in handles for s in h[:2]]
    res = pl.pallas_call(
        body,
        name=name,
        out_shape=tuple(pltpu.HBM(a.shape, a.dtype) for a in srcs + lands),
        in_specs=[HBM] * (2 * na) + [SEM] * (2 * na) + [ANY],
        out_specs=tuple([HBM] * (2 * na)),
        input_output_aliases={i: i for i in range(2 * na)},
        compiler_params=pltpu.CompilerParams(has_side_effects=EFFECT),
    )(*srcs, *lands, *sems, after)
    return list(res[:na]), list(res[na:])


def _forward_copies(land_ref, send_sems, recv_sems):
    x, y, c = lax.axis_index("x"), lax.axis_index("y"), lax.axis_index("c")
    r = land_ref.shape[0] // N_DEV
    copies = []
    for j, k in enumerate((2, 4, 6)):
        px, py, pc = _peer_of(k)
        rows = land_ref.at[pl.ds((4 * px + 2 * py + pc) * r, r), :]
        copies.append(pltpu.make_async_remote_copy(
            src_ref=rows, dst_ref=rows, send_sem=send_sems.at[j], recv_sem=recv_sems.at[j],
            device_id=(x, y, 1 - c), device_id_type=MESH))
    return copies


def _forward_start(land, *, name):
    def body(land_ref, send_sems, recv_sems, land_thru, token):
        for cp in _forward_copies(land_ref, send_sems, recv_sems):
            cp.start()
        token[...] = jnp.zeros_like(token)

    res = pl.pallas_call(
        body,
        name=name,
        out_shape=(pltpu.SemaphoreType.DMA((3,)), pltpu.SemaphoreType.DMA((3,)), pltpu.HBM(land.shape, land.dtype),
                   jax.ShapeDtypeStruct((8, LANES), F32)),
        in_specs=[HBM],
        out_specs=(SEM, SEM, HBM, pl.BlockSpec(memory_space=pltpu.VMEM)),
        input_output_aliases={0: 2},
        compiler_params=pltpu.CompilerParams(has_side_effects=EFFECT),
    )(pltpu.with_memory_space_constraint(land, pltpu.HBM))
    return res[:3]


def _forward_wait(handle, *, name):
    send_sems, recv_sems, land = handle

    def body(land_ref, send_ref, recv_ref, land_out):
        for cp in _forward_copies(land_ref, send_ref, recv_ref):
            cp.wait_send()
            cp.wait_recv()

    return pl.pallas_call(
        body,
        name=name,
        out_shape=pltpu.HBM(land.shape, land.dtype),
        in_specs=[HBM, SEM, SEM],
        out_specs=HBM,
        input_output_aliases={0: 0},
        compiler_params=pltpu.CompilerParams(has_side_effects=EFFECT),
    )(land, send_sems, recv_sems)


def _sum8(own, recv, *, name):
    _, r, w = recv.shape
    fits = lambda c: 2 * c * w * (N_DEV * recv.dtype.itemsize + 4) <= 32 * 2**20
    tr = _pick(r, tuple(c for c in (r // 2, r // 4, 256, 128, 64, 32, 16, 8) if c % 16 == 0 and fits(c)))

    def body(own_ref, a_ref, o_ref):
        acc = own_ref[...].astype(F32)
        for i in range(N_DEV - 1):
            acc = acc + a_ref[i].astype(F32)
        o_ref[...] = acc

    return pl.pallas_call(
        body,
        name=name,
        grid=(r // tr,),
        in_specs=[pl.BlockSpec((tr, w), lambda i: (i, 0)), pl.BlockSpec((N_DEV - 1, tr, w), lambda i: (0, i, 0))],
        out_specs=pl.BlockSpec((tr, w), lambda i: (i, 0)),
        out_shape=jax.ShapeDtypeStruct((r, w), F32),
        compiler_params=pltpu.CompilerParams(dimension_semantics=("parallel",), vmem_limit_bytes=_vmem(4 * N_DEV * tr * w * 4)),
    )(own, recv)


def _adamw(w, g, m, v, *, name):
    def fn(rows, _):
        wv, gv, mv, vv = rows
        m1 = ADAM_B1 * mv + (1.0 - ADAM_B1) * gv
        v1 = ADAM_B2 * vv + (1.0 - ADAM_B2) * jnp.square(gv)
        m_hat = m1 / (1.0 - ADAM_B1 ** ADAM_STEP)
        v_hat = v1 / (1.0 - ADAM_B2 ** ADAM_STEP)
        delta = -ADAM_LR * (m_hat / (jnp.sqrt(v_hat) + ADAM_EPS) + ADAM_WD * wv)
        return [delta, m1, v1], []

    c = w.shape[1]
    return _rowmap(fn, [w, g, m, v], [], [(c, F32)] * 3, [], name=name)


_SMALL = ("b_ada", "g_norm1", "g_norm2", "b_gate", "g_qa", "g_ka", "g_qb", "g_kb", "rpb", "loss")


def _pack_small(parts):
    flat = []
    for nme in _SMALL:
        a = parts[nme].reshape(-1).astype(F32)
        flat.append(jnp.pad(a, (0, (-a.shape[0]) % LANES)))
    flat = jnp.concatenate(flat)
    flat = jnp.pad(flat, (0, (-flat.shape[0]) % (LANES * LANES)))
    return flat.reshape(-1, LANES)


def _unpack_small(packed, shapes):
    flat = packed.reshape(-1)
    out, pos = {}, 0
    for nme in _SMALL:
        n = int(np.prod(shapes[nme]))
        out[nme] = flat[pos:pos + n].reshape(shapes[nme])
        pos += n + (-n) % LANES
    return out


def _to_class(a, d):
    t, w = a.shape
    return a if d == 1 else a.reshape(t // d, d, w).transpose(1, 0, 2).reshape(t, w)


def _from_class(a, d):
    t, w = a.shape
    return a if d == 1 else a.reshape(d, t // d, w).transpose(1, 0, 2).reshape(t, w)


def kernel(x, c, w_ada, b_ada, g_norm1, g_norm2, w_in, b_gate, g_qa, g_ka, g_qb, g_kb, rpb, w_proj_a, w_proj_b, w_o, w_ffn_in, w_ffn_out, loss_target, m_w_ada, m_b_ada, m_g_norm1, m_g_norm2, m_w_in, m_b_gate, m_g_qa, m_g_ka, m_g_qb, m_g_kb, m_rpb, m_w_proj_a, m_w_proj_b, m_w_o, m_w_ffn_in, m_w_ffn_out, v_w_ada, v_b_ada, v_g_norm1, v_g_norm2, v_w_in, v_b_gate, v_g_qa, v_g_ka, v_g_qb, v_g_kb, v_rpb, v_w_proj_a, v_w_proj_b, v_w_o, v_w_ffn_in, v_w_ffn_out):
    t, d = x.shape[1], x.shape[2]
    d_ff = w_ffn_out.shape[1] * N_DEV
    me = 4 * lax.axis_index("x") + 2 * lax.axis_index("y") + lax.axis_index("c")
    xt, tgt = x.reshape(t, d), loss_target.reshape(t, d)
    ones = _head_ones()

    shards = [s.astype(WIRE_DTYPE) for s in (w_in[0].T, w_ffn_in[0].T, w_proj_a[0].T, w_proj_b[0].T, w_o[0], w_ffn_out[0])]
    lands = [lax.dynamic_update_slice(lax.empty((N_DEV * s.shape[0], s.shape[1]), s.dtype), s, (me * s.shape[0], 0))
             for s in shards]

    c_all = _all_gather([jnp.pad(c, ((0, 7), (0, 0)))], name="gather_c")[0][::8]
    c_all = jnp.pad(c_all, ((0, LANES - N_DEV), (0, 0)))

    def mod_body(c_ref, w_ref, b_ref, o_ref, act_ref):
        act = _silu(c_ref[...])
        act_ref[...] = act
        hi, mid, lo = _split3(act)
        w = w_ref[...]
        whi, wmid, wlo = _split3(w)
        acc = _dot(hi, whi) + (_dot(hi, wmid) + _dot(mid, whi)) + (_dot(hi, wlo) + _dot(mid, wmid) + _dot(lo, whi))
        o_ref[...] = acc + b_ref[...]

    ncol = w_ada.shape[2]
    b_ada_mine = lax.dynamic_slice(b_ada, (0, me * ncol), (1, ncol))
    mod_part, c_act = pl.pallas_call(
        mod_body,
        name="ada_mod",
        out_shape=[jax.ShapeDtypeStruct((LANES, ncol), F32), jax.ShapeDtypeStruct((LANES, d), F32)],
        compiler_params=pltpu.CompilerParams(vmem_limit_bytes=_vmem(6 * d * ncol * 4)),
    )(c_all, w_ada[0], b_ada_mine)
    mod_all = _all_gather([mod_part[:N_DEV]], name="gather_mod")[0].reshape(N_DEV, N_DEV, ncol)
    mod = lax.dynamic_index_in_dim(mod_all, me, axis=1, keepdims=False).reshape(6, d)
    sh1, sc1, gt1, sh2, sc2, gt2 = [mod[i:i + 1] for i in range(6)]

    def norm_fwd(rows, vecs):
        (xv,), (g, sc, sh) = rows, vecs
        return [xv * _rms(xv) * g * (1.0 + sc) + sh], []

    w_in_handle, w_token = _split_start(shards[:1], lands[:1], gather="chip", after=mod, name="gather_w_in_start")
    (h,) = _rowmap(norm_fwd, [xt], [g_norm1 + w_token, sc1, sh1], [(d, MXU_DTYPE)], [], name="norm1")
    n_a, n_b = 3 * WA, 3 * WB
    (w_in_t,) = _split_wait(w_in_handle, h, gather="chip", name="gather_w_in_wait")[1]
    w_in_t = _forward_wait(_forward_start(w_in_t, name="gather_w_in_forward_start"), name="gather_w_in_forward_wait")
    w_handles, w_token = _split_start(shards[1:], lands[1:], gather=True, after=w_in_t, name="gather_weights_start")
    w_in_a, w_in_b, w_in_g = w_in_t[:n_a], w_in_t[n_a:n_a + n_b], w_in_t[n_a + n_b:]

    rot_c, rot_lo, rot_hi = _rot_tables(t)
    tile_g = lambda g, heads: jnp.tile(g, (1, heads))

    def qk_fwd(width, rotate):
        def fn(xv, rows, vecs):
            gq, gk, on = vecs
            qkv = []
            for i, g in enumerate((gq, gk)):
                xi = xv[:, i * width:(i + 1) * width]
                r = lax.rsqrt(_headsum(xi * xi, on) * (1.0 / HEAD_DIM) + EPS)
                yi = xi * r * g
                if rotate:
                    yi = _rot(yi, rows[0], rows[1], rows[2])
                qkv.append(yi)
            qkv.append(xv[:, 2 * width:])
            if not rotate:
                return [xv] + qkv, []
            groups = [jnp.concatenate([a[:, g * WB_OUT:(g + 1) * WB_OUT] for a in qkv], axis=1)
                      for g in range(len(DIL_CONFIGS))]
            return [xv] + groups, []
        return fn

    qkv_a, qa, ka, va = _mm_parts_rows(
        [(h, w_in_a, "nt")], qk_fwd(WA, False), [], [tile_g(g_qa, NA_HEADS) + w_token, tile_g(g_ka, NA_HEADS), ones],
        [(3 * WA, ACT_DTYPE)] + [(WA, MXU_DTYPE)] * 3, [], name="proj_a_qknorm")
    qkv_b, *qkv_groups = _mm_parts_rows(
        [(h, w_in_b, "nt")], qk_fwd(WB, True), [rot_c, rot_lo, rot_hi],
        [tile_g(g_qb, DIL_HEADS), tile_g(g_kb, DIL_HEADS), ones],
        [(3 * WB, ACT_DTYPE)] + [(3 * WB_OUT, MXU_DTYPE)] * len(DIL_CONFIGS), [], name="proj_b_qknorm")

    bias_tab = _na_bias_table(rpb[0])
    o_a, lse_a = _attn_fwd(qa, ka, va, kind="na", bias=bias_tab, name="na_fwd")

    grp = []
    for g, (_, dil) in enumerate(DIL_CONFIGS):
        qkv_c = _to_class(qkv_groups[g], dil)
        og, lg = _attn_fwd((qkv_c, 0), (qkv_c, 1), (qkv_c, 2), kind="dil", seg=t // dil, name=f"dil_fwd{g}")
        grp.append(dict(qkv=qkv_c, o=_from_class(og, dil), lse=_from_class(lg, dil), lse_c=lg, dil=dil))

    def merge_fwd(rows, _):
        o0, o1, o2, l0, l1, l2 = rows
        mx = jnp.maximum(jnp.maximum(l0, l1), l2)
        e0, e1, e2 = jnp.exp(l0 - mx), jnp.exp(l1 - mx), jnp.exp(l2 - mx)
        s = e0 + e1 + e2
        return [(e0 / s) * o0 + (e1 / s) * o1 + (e2 / s) * o2], []

    (o_b,) = _rowmap(merge_fwd, [gr["o"] for gr in grp] + [gr["lse"] for gr in grp], [], [(WB_OUT, F32)], [], name="dil_merge")

    w_pa_t, w_pb_t, w_o_f = _split_wait(w_handles[1:4], o_b, gather=True, name="gather_w_out_wait")[1]
    def gate_fwd(prods, _, vecs):
        gv, pav, pbv = prods
        sg = jax.nn.sigmoid(gv + vecs[0])
        return [gv, pav, pbv, sg[:, :d] * pav + sg[:, d:] * pbv], []

    gates, pa, pb, merged = _mm_parts_rows(
        [(h, w_in_g, "nt"), (o_a, w_pa_t, "nt"), (o_b, w_pb_t, "nt")], gate_fwd, [], [b_gate],
        [(2 * d, ACT_DTYPE), (d, ACT_DTYPE), (d, ACT_DTYPE), (d, MXU_DTYPE)], [], separate=True, name="proj_gates_out_merge")
    def resid_norm(av, rows, vecs):
        (xv,), (gt, g, sc, sh) = rows, vecs
        x1v = xv + gt * av
        return [av, x1v, x1v * _rms(x1v) * g * (1.0 + sc) + sh], []

    att, x1, h2 = _mm_parts_rows([(merged, w_o_f)], resid_norm, [xt], [gt1, g_norm2, sc2, sh2],
                           [(d, F32), (d, F32), (d, MXU_DTYPE)], [], name="proj_o_resid_norm2")

    w_ffn_in_t, w_ffn_out_f = _split_wait([w_handles[0], w_handles[4]], h2, gather=True, name="gather_w_ffn_wait")[1]
    w_ffn_a, w_ffn_up = w_ffn_in_t[:d_ff], w_ffn_in_t[d_ff:]

    def swiglu_fwd(prods, _):
        a, up = prods
        return [a, up, _silu(a) * up]

    ua, uu, f = _mm_ew(h2, [w_ffn_a, w_ffn_up], swiglu_fwd, [], [ACT_DTYPE, ACT_DTYPE, MXU_DTYPE], name="ffn_in_swiglu")

    def loss_fn(yv, rows, vecs):
        (x1v, tv), gt = rows, vecs[0]
        err = x1v + gt * yv - tv
        dout = err * (1.0 / d)
        return [dout, dout * gt], [_colsum(err * err), _colsum(dout * yv)]

    dout, dy2, err2, dgt2 = _mm_parts_rows([(f, w_ffn_out_f)], loss_fn, [x1, tgt], [gt2], [(d, F32), (d, MXU_DTYPE)],
                                           [d, d], name="ffn_out_loss")

    dw_ffn_out = _mm(f, dy2, ta=True, out_dtype=WIRE_DTYPE, name="wgrad_ffn_out")
    def swiglu_bwd(prods, rows):
        (dfv,), (a, up) = prods, rows
        sg = jax.nn.sigmoid(a)
        return [dfv * up * (sg * (1.0 + a * (1.0 - sg))), dfv * (a * sg)]

    da, dup = _mm_ew(dy2, [w_ffn_out_f], swiglu_bwd, [ua, uu], [MXU_DTYPE, MXU_DTYPE], name="dgrad_ffn_out_swiglu_bwd")
    dw_ffn_in_t = _mm(da, h2, ta=True, into=(lax.empty((2 * d_ff, d), WIRE_DTYPE), 0), name="wgrad_ffn_in_a")
    dw_ffn_in_t = _mm(dup, h2, ta=True, into=(dw_ffn_in_t, d_ff), name="wgrad_ffn_in_up")
    land7 = lambda a: lax.empty((N_DEV - 1, a.shape[0] // N_DEV, a.shape[1]), a.dtype)
    own_block = lambda a: lax.dynamic_slice(a, (me * (a.shape[0] // N_DEV), 0), (a.shape[0] // N_DEV, a.shape[1]))
    g_ffn = [dw_ffn_in_t, dw_ffn_out]
    h_ffn, tok_ffn = _split_start(g_ffn, [land7(a) for a in g_ffn], gather=False, name="exchange_ffn_start")
    def norm_bwd(dh, xv, g, sc):
        r = _rms(xv)
        xh = xv * r
        dxh = dh * g * (1.0 + sc)
        dxv = r * (dxh - xh * jnp.mean(dxh * xh, axis=-1, keepdims=True))
        return dxv, [_colsum(dh), _colsum(dh * xh * g), _colsum(dh * xh * (1.0 + sc))]

    def norm2_bwd(dhv, rows, vecs):
        (x1v, dov, av), (g, sc, gt) = rows, vecs
        dxv, sums = norm_bwd(dhv, x1v, g, sc)
        dx1v = dov + dxv
        return [dx1v, dx1v * gt], sums + [_colsum(dx1v * av)]

    dx1, datt, dsh2, dsc2, dg2, dgt1 = _mm_parts_rows(
        [(da, w_ffn_a), (dup, w_ffn_up)], norm2_bwd, [x1, dout, att], [g_norm2 + tok_ffn, sc2, gt1],
        [(d, F32), (d, MXU_DTYPE)], [d] * 4, name="dgrad_ffn_in_norm2_bwd")
    dw_o = _mm(merged, datt, ta=True, out_dtype=WIRE_DTYPE, name="wgrad_o")
    def gate_bwd(dm, rows, vecs):
        gv, pav, pbv = rows
        sg = jax.nn.sigmoid(gv + vecs[0])
        ga, gb = sg[:, :d], sg[:, d:]
        dgp = jnp.concatenate([dm * pav * ga * (1.0 - ga), dm * pbv * gb * (1.0 - gb)], axis=1)
        return [dm * ga, dm * gb, dgp], [_colsum(dgp)]

    dpa, dpb, dgates, db_gate = _mm_parts_rows(
        [(datt, w_o_f.T)], gate_bwd, [gates, pa, pb], [b_gate],
        [(d, MXU_DTYPE), (d, MXU_DTYPE), (2 * d, MXU_DTYPE)], [2 * d], name="dgrad_o_gate_bwd")
    dw_pa_t = _mm(dpa, o_a, ta=True, out_dtype=WIRE_DTYPE, name="wgrad_proj_a")
    dw_pb_t = _mm(dpb, o_b, ta=True, out_dtype=WIRE_DTYPE, name="wgrad_proj_b")
    g_out = [dw_pa_t, dw_pb_t, dw_o]
    h_out, tok_out = _split_start(g_out, [land7(a) for a in g_out], gather=False, name="exchange_out_start")
    def delta_a(doa, rows, vecs):
        return [doa, _headsum(doa * rows[0], vecs[0])], []

    do_a, dterm_a = _mm_parts_rows([(dpa, w_pa_t)], delta_a, [o_a], [ones + tok_out.astype(ones.dtype)],
                                   [(WA, F32), (WA, F32)], [], name="dgrad_proj_a_delta")
    dqa, dka, dva, dbias = _attn_bwd(qa, ka, va, do_a, dterm_a, lse_a, kind="na", bias=bias_tab, name="na_bwd")
    g_rpb = _na_bias_grad(dbias)

    def merge_bwd(dob, rows, vecs):
        o0, o1, o2, l0, l1, l2 = rows
        on = vecs[0]
        mx = jnp.maximum(jnp.maximum(l0, l1), l2)
        e0, e1, e2 = jnp.exp(l0 - mx), jnp.exp(l1 - mx), jnp.exp(l2 - mx)
        s = e0 + e1 + e2
        ws = [e0 / s, e1 / s, e2 / s]
        dws = [_headsum(dob * o, on) for o in (o0, o1, o2)]
        mean = ws[0] * dws[0] + ws[1] * dws[1] + ws[2] * dws[2]
        return [jnp.concatenate([w * dob, w * mean], axis=1) for w in ws], []

    mb = _mm_parts_rows([(dpb, w_pb_t)], merge_bwd, [gr["o"] for gr in grp] + [gr["lse"] for gr in grp], [ones],
                        [(2 * WB_OUT, F32)] * len(grp), [], name="dgrad_proj_b_merge_bwd")
    dqb, dkb, dvb = [], [], []
    for g, gr in enumerate(grp):
        dil, qkv_c = gr["dil"], gr["qkv"]
        dd_c = _to_class(mb[g], dil)
        dq, dk, dv = _attn_bwd((qkv_c, 0), (qkv_c, 1), (qkv_c, 2), (dd_c, 0), (dd_c, 1), (gr["lse_c"], 0),
                               kind="dil", seg=t // dil, name=f"dil_bwd{g}")
        dqb.append(_from_class(dq, dil))
        dkb.append(dk[0] if dil == 1 else _from_class(dk[0][0][0], dil))
        dvb.append(dv[0] if dil == 1 else _from_class(dv[0][0][0], dil))

    def qk_bwd(width, rotate, nparts):
        def fn(rows, vecs):
            gq, gk, on = vecs
            xv = rows[0]
            pos = 1
            if rotate:
                rc, rlo, rhi = rows[1:4]
                pos = 4
            cat = lambda parts: parts[0] if len(parts) == 1 else jnp.concatenate(parts, axis=1)
            ends = np.cumsum((pos,) + nparts)
            dq, dk, dv = [cat(rows[ends[i]:ends[i + 1]]) for i in range(3)]
            outs, sums = [], []
            for i, (dy, g) in enumerate(((dq, gq), (dk, gk))):
                if rotate:
                    dy = _rot(dy, rc, -rlo, -rhi)
                xi = xv[:, i * width:(i + 1) * width]
                r = lax.rsqrt(_headsum(xi * xi, on) * (1.0 / HEAD_DIM) + EPS)
                xh = xi * r
                dxh = dy * g
                outs.append(r * (dxh - xh * (_headsum(dxh * xh, on) * (1.0 / HEAD_DIM))))
                sums.append(_colsum(dy * xh))
            return [jnp.concatenate(outs + [dv], axis=1)], sums
        return fn

    dqkv_a, dg_qa, dg_ka = _rowmap(qk_bwd(WA, False, (1, len(dka), len(dva))), [qkv_a, dqa] + dka + dva,
                                   [tile_g(g_qa, NA_HEADS), tile_g(g_ka, NA_HEADS), ones],
                                   [(3 * WA, MXU_DTYPE)], [WA, WA], name="qknorm_a_bwd")
    dqkv_b, dg_qb, dg_kb = _rowmap(qk_bwd(WB, True, (3, 3, 3)), [qkv_b, rot_c, rot_lo, rot_hi] + dqb + dkb + dvb,
                                   [tile_g(g_qb, DIL_HEADS), tile_g(g_kb, DIL_HEADS), ones],
                                   [(3 * WB, MXU_DTYPE)], [WB, WB], name="qknorm_b_bwd")

    dw_in_t = jnp.concatenate([
        _mm(dqkv_a, h, ta=True, out_dtype=WIRE_DTYPE, name="wgrad_in_a"),
        _mm(dqkv_b, h, ta=True, out_dtype=WIRE_DTYPE, name="wgrad_in_b"),
        _mm(dgates, h, ta=True, out_dtype=WIRE_DTYPE, name="wgrad_in_gates")], axis=0)
    h_in, tok_in = _split_start([dw_in_t], [land7(dw_in_t)], gather=False, name="exchange_in_start")
    def norm1_bwd(dhv, rows, vecs):
        xv, dx1v = rows
        dxv, sums = norm_bwd(dhv, xv, vecs[0], vecs[1])
        return [dx1v + dxv], sums

    grad_x, dsh1, dsc1, dg1 = _mm_parts_rows(
        [(dqkv_a, w_in_a), (dqkv_b, w_in_b), (dgates, w_in_g)], norm1_bwd, [xt, dx1], [g_norm1 + tok_in, sc1],
        [(d, F32)], [d] * 3, name="dgrad_in_norm1_bwd")

    heads_sum = lambda a, heads: a.reshape(heads, HEAD_DIM).sum(axis=0)
    dmod = jnp.concatenate([dsh1, dsc1, dgt1, dsh2, dsc2, dgt2], axis=1)
    local_small = _pack_small(dict(
        b_ada=dmod, g_norm1=dg1, g_norm2=dg2, b_gate=db_gate, g_qa=heads_sum(dg_qa, NA_HEADS),
        g_ka=heads_sum(dg_ka, NA_HEADS), g_qb=heads_sum(dg_qb, DIL_HEADS), g_kb=heads_sum(dg_kb, DIL_HEADS),
        rpb=g_rpb, loss=(0.5 / d) * jnp.sum(err2)))
    srows = local_small.shape[0]
    small_all = _all_gather([local_small], name="gather_small")[0].reshape(N_DEV, srows, LANES)
    small_sum = _sum8(small_all[0], small_all[1:], name="sum_small")
    small_shapes = dict(b_ada=b_ada.shape, g_norm1=g_norm1.shape, g_norm2=g_norm2.shape, b_gate=b_gate.shape,
                        g_qa=g_qa.shape, g_ka=g_ka.shape, g_qb=g_qb.shape, g_kb=g_kb.shape, rpb=rpb.shape, loss=())
    small_w = dict(b_ada=b_ada, g_norm1=g_norm1, g_norm2=g_norm2, b_gate=b_gate, g_qa=g_qa, g_ka=g_ka, g_qb=g_qb,
                   g_kb=g_kb, rpb=rpb, loss=jnp.zeros((), F32))
    small_m = dict(b_ada=m_b_ada, g_norm1=m_g_norm1, g_norm2=m_g_norm2, b_gate=m_b_gate, g_qa=m_g_qa, g_ka=m_g_ka,
                   g_qb=m_g_qb, g_kb=m_g_kb, rpb=m_rpb, loss=jnp.zeros((), F32))
    small_v = dict(b_ada=v_b_ada, g_norm1=v_g_norm1, g_norm2=v_g_norm2, b_gate=v_b_gate, g_qa=v_g_qa, g_ka=v_g_ka,
                   g_qb=v_g_qb, g_kb=v_g_kb, rpb=v_rpb, loss=jnp.zeros((), F32))
    s_delta, s_m, s_v = _adamw(_pack_small(small_w), small_sum, _pack_small(small_m), _pack_small(small_v), name="adamw_small")
    gs = _unpack_small(small_sum, small_shapes)
    ds_, ms_, vs_ = [_unpack_small(a, small_shapes) for a in (s_delta, s_m, s_v)]

    dmod_all = small_all[:, :6 * d // LANES].reshape(N_DEV, 6 * d)
    dmod_mine = jnp.pad(lax.dynamic_slice(dmod_all, (0, me * ncol), (N_DEV, ncol)), ((0, LANES - N_DEV), (0, 0)))

    def wada_body(c_ref, dm_ref, o_ref):
        chi, cmid, clo = _split3(c_ref[...])
        dhi, dmid, dlo = _split3(dm_ref[...])
        o_ref[...] = (_dot_tn(chi, dhi) + (_dot_tn(chi, dmid) + _dot_tn(cmid, dhi))
                      + (_dot_tn(chi, dlo) + _dot_tn(cmid, dmid) + _dot_tn(clo, dhi)))

    g_w_ada = pl.pallas_call(
        wada_body,
        name="wgrad_ada",
        out_shape=jax.ShapeDtypeStruct((d, ncol), F32),
        compiler_params=pltpu.CompilerParams(vmem_limit_bytes=_vmem(4 * d * ncol * 4)),
    )(c_act, dmod_mine)

    sent, recv = _split_wait(h_in + h_ffn + h_out, small_sum, gather=False, name="exchange_wait")
    names = ("w_in", "w_ffn_in", "w_ffn_out", "w_proj_a", "w_proj_b", "w_o")
    transposed = (True, True, False, True, True, False)
    big_g = {}
    for nme, own, r, tr in zip(names, sent, recv, transposed):
        s = _sum8(own_block(own), r, name=f"sum_{nme}")
        big_g[nme] = s.T if tr else s
    big_g["w_ada"] = g_w_ada
    big_w = dict(w_ada=w_ada, w_in=w_in, w_proj_a=w_proj_a, w_proj_b=w_proj_b, w_o=w_o, w_ffn_in=w_ffn_in, w_ffn_out=w_ffn_out)
    big_m = dict(w_ada=m_w_ada, w_in=m_w_in, w_proj_a=m_w_proj_a, w_proj_b=m_w_proj_b, w_o=m_w_o, w_ffn_in=m_w_ffn_in, w_ffn_out=m_w_ffn_out)
    big_v = dict(w_ada=v_w_ada, w_in=v_w_in, w_proj_a=v_w_proj_a, w_proj_b=v_w_proj_b, w_o=v_w_o, w_ffn_in=v_w_ffn_in, w_ffn_out=v_w_ffn_out)
    grads, deltas, new_m, new_v = {}, {}, {}, {}
    for nme in big_w:
        dl, m1, v1 = _adamw(big_w[nme][0], big_g[nme], big_m[nme][0], big_v[nme][0], name=f"adamw_{nme}")
        grads[nme], deltas[nme], new_m[nme], new_v[nme] = big_g[nme][None], dl[None], m1[None], v1[None]
    for nme in _SMALL[:-1]:
        grads[nme], deltas[nme], new_m[nme], new_v[nme] = gs[nme], ds_[nme], ms_[nme], vs_[nme]

    order = ("w_ada", "b_ada", "g_norm1", "g_norm2", "w_in", "b_gate", "g_qa", "g_ka", "g_qb", "g_kb", "rpb",
             "w_proj_a", "w_proj_b", "w_o", "w_ffn_in", "w_ffn_out")
    return (gs["loss"], grad_x[None], *[grads[n] for n in order], *[deltas[n] for n in order],
            *[new_m[n] for n in order], *[new_v[n] for n in order])
```

```python
import functools

import numpy as np
import jax
import jax.numpy as jnp
from jax import lax
from jax.experimental import pallas as pl
from jax.experimental.pallas import tpu as pltpu

F32 = jnp.float32
MXU_DTYPE = jnp.bfloat16
WIRE_DTYPE = jnp.bfloat16
ACT_DTYPE = jnp.bfloat16

HEAD_DIM = 64
GRID_W = 64
NA_HEADS = 8
NA_KH = 8
NA_KW = 16
DIL_CONFIGS = ((128, 1), (512, 4), (2048, 16))
DIL_HEADS_PER_GROUP = 4
DIL_HEADS = DIL_HEADS_PER_GROUP * len(DIL_CONFIGS)
DIL_HALF = 64
ROT_DIM = HEAD_DIM // 4
ROPE_THETA = 500000.0
EPS = 1e-6
NEG_INF = -1e30
WA = NA_HEADS * HEAD_DIM
WB = DIL_HEADS * HEAD_DIM
WB_OUT = DIL_HEADS_PER_GROUP * HEAD_DIM
ADAM_LR = 0.001
ADAM_B1 = 0.9
ADAM_B2 = 0.999
ADAM_EPS = 1e-08
ADAM_WD = 0.01
ADAM_STEP = 10

N_DEV = 8
LANES = 128
VMEM_CAP = 60 * 2**20
VMEM_FLOOR = 56 * 2**20
MESH = pl.DeviceIdType.MESH
ANY = pl.BlockSpec(memory_space=pl.ANY)


def _vmem(nbytes):
    return int(min(VMEM_CAP, max(VMEM_FLOOR, nbytes * 5 // 4 + 4 * 2**20)))


def _pick(dim, cands):
    for c in cands:
        if c <= dim and dim % c == 0:
            return c
    return dim


def _nbytes(shape, dtype):
    return int(np.prod(shape)) * jnp.dtype(dtype).itemsize


def _dot(a, b, dims=((1,), (0,))):
    return lax.dot_general(a.astype(MXU_DTYPE), b.astype(MXU_DTYPE), (dims, ((), ())), preferred_element_type=F32)


def _dot_nt(a, b):
    return _dot(a, b, ((1,), (1,)))


def _dot_tn(a, b):
    return _dot(a, b, ((0,), (0,)))


def _split3(a):
    hi = a.astype(jnp.bfloat16)
    r1 = a - hi.astype(F32)
    mid = r1.astype(jnp.bfloat16)
    lo = (r1 - mid.astype(F32)).astype(jnp.bfloat16)
    return hi, mid, lo


def _silu(x):
    return x * jax.nn.sigmoid(x)


def _divisors(dim, unit):
    return [c for c in range(unit, dim + 1, unit) if dim % c == 0] or [dim]


def _mm_tiles(m, n, kdim, a_item, b_item, o_item, row_off=0, whole_n=False):
    step_us, hbm_bytes_per_us, flops_per_us, budget = 0.35, 3.0e6, 8.0e8, 48 * 2**20
    best = None
    for tm in _divisors(m, LANES):
        for tn in ([n] if whole_n else _divisors(n, LANES)):
            for tk in _divisors(kdim, LANES):
                if row_off % tm:
                    continue
                gm, gn, gk = m // tm, n // tn, kdim // tk
                vmem = 2 * (tm * tk * a_item + tk * tn * b_item + tm * tn * o_item) + 2 * (tm * tk + tk * tn)
                vmem += tm * tn * 4 * ((1 if gk > 1 else 0) + 1)
                if vmem > budget:
                    continue
                a_reads = m * kdim * a_item * (gn if gk > 1 else 1)
                traffic = a_reads + kdim * n * b_item * gm + m * n * o_item
                cost = gm * gn * gk * step_us + max(traffic / hbm_bytes_per_us, 2.0 * m * n * kdim / flops_per_us)
                if best is None or cost < best[0]:
                    best = (cost, tm, tn, tk)
    return best[1:]


def _mm(a, b, *, name, ta=False, tb=False, out_dtype=F32, into=None):
    if ta:
        kdim, m = a.shape
    else:
        m, kdim = a.shape
    n = b.shape[0] if tb else b.shape[1]
    assert b.shape[1 if tb else 0] == kdim
    buf, row_off = into if into is not None else (None, 0)
    if buf is not None:
        out_dtype = buf.dtype
    tm, tn, tk = _mm_tiles(m, n, kdim, a.dtype.itemsize, b.dtype.itemsize, jnp.dtype(out_dtype).itemsize, row_off,
                           whole_n=ta and n <= 8 * LANES)
    gm, gn, gk = m // tm, n // tn, kdim // tk
    ob = row_off // tm

    a_spec = pl.BlockSpec((tk, tm), lambda i, j, k: (k, i)) if ta else pl.BlockSpec((tm, tk), lambda i, j, k: (i, k))
    b_spec = pl.BlockSpec((tn, tk), lambda i, j, k: (j, k)) if tb else pl.BlockSpec((tk, tn), lambda i, j, k: (k, j))
    o_spec = pl.BlockSpec((tm, tn), lambda i, j, k: (i + ob, j))
    a_dims = (0,) if ta else (1,)
    b_dims = (1,) if tb else (0,)

    def body(a_ref, b_ref, *rest):
        o_ref, scratch = rest[-1 - (gk > 1)], rest[-(gk > 1):] if gk > 1 else ()
        if gk == 1:
            o_ref[...] = _dot(a_ref[...], b_ref[...], (a_dims, b_dims)).astype(o_ref.dtype)
            return
        (acc_ref,) = scratch
        k = pl.program_id(2)

        @pl.when(k == 0)
        def _():
            acc_ref[...] = jnp.zeros_like(acc_ref)

        acc_ref[...] += _dot(a_ref[...], b_ref[...], (a_dims, b_dims))

        @pl.when(k == gk - 1)
        def _():
            o_ref[...] = acc_ref[...].astype(o_ref.dtype)

    est = 2 * (tm * tk * a.dtype.itemsize + tk * tn * b.dtype.itemsize + tm * tn * jnp.dtype(out_dtype).itemsize)
    est += tm * tn * 4 + 2 * (tm * tk + tk * tn) * 2
    return pl.pallas_call(
        body,
        name=name,
        grid=(gm, gn, gk),
        in_specs=[a_spec, b_spec] + ([ANY] if buf is not None else []),
        out_specs=o_spec,
        out_shape=jax.ShapeDtypeStruct((m, n) if buf is None else buf.shape, out_dtype),
        input_output_aliases={2: 0} if buf is not None else {},
        scratch_shapes=[pltpu.VMEM((tm, tn), F32)] if gk > 1 else [],
        compiler_params=pltpu.CompilerParams(
            dimension_semantics=("parallel", "parallel", "arbitrary"), vmem_limit_bytes=_vmem(est)
        ),
    )(*((a, b) if buf is None else (a, b, buf)))


def _resident(shape):
    return pl.BlockSpec(shape, lambda i: (0,) * len(shape), pipeline_mode=pl.Buffered(1))


def _row_tile(m, fixed_bytes, bytes_per_row, budget=50 * 2**20):
    fits = [tm for tm in _divisors(m, LANES) if fixed_bytes + tm * bytes_per_row <= budget]
    return max(fits) if fits else _divisors(m, LANES)[0]


def _mm_parts_rows(parts, fn, rows, vecs, outs, reds, *, name, separate=False):
    parts = [(p[0], p[1], len(p) > 2) for p in parts]
    rows = [r if isinstance(r, tuple) else (r, r.shape[1], 0) for r in rows]
    m, n = parts[0][0].shape[0], parts[0][1].shape[0 if parts[0][2] else 1]
    npart, nr, nv, no = len(parts), len(rows), len(vecs), len(outs)
    row_bytes = sum(w * r.dtype.itemsize for r, w, _ in rows) + sum(w * jnp.dtype(dt).itemsize for (w, dt) in outs)
    a_row_bytes = sum(a.shape[1] * a.dtype.itemsize for a, _, _ in parts)
    fixed = sum(_nbytes(b.shape, b.dtype) for _, b, _ in parts)
    per_row = 2 * (a_row_bytes + row_bytes) + n * 4 * 5
    tm = _row_tile(m, fixed, per_row)
    sub = min(tm, 2 * LANES)

    def body(*refs):
        ab = refs[:2 * npart]
        row_refs, vec_refs = refs[2 * npart:2 * npart + nr], refs[2 * npart + nr:2 * npart + nr + nv]
        out_refs = refs[2 * npart + nr + nv:2 * npart + nr + nv + no]
        red_refs = refs[2 * npart + nr + nv + no:]
        if red_refs:
            @pl.when(pl.program_id(0) == 0)
            def _():
                for ref in red_refs:
                    ref[...] = jnp.zeros_like(ref)

        vecs_v = [v[...] for v in vec_refs]
        for s0 in range(0, tm, sub):
            sl = slice(s0, s0 + sub)
            prods = [(_dot_nt if nt else _dot)(ab[2 * p][sl, :], ab[2 * p + 1][...]) for p, (_, _, nt) in enumerate(parts)]
            r = prods if separate else functools.reduce(lambda u, v: u + v, prods)
            o, rd = fn(r, [x[sl, :].astype(F32) for x in row_refs], vecs_v)
            for ref, val in zip(out_refs, o):
                ref[sl, :] = val.astype(ref.dtype)
            for ref, val in zip(red_refs, rd):
                ref[...] += val

    in_specs, operands = [], []
    for a, b, _ in parts:
        in_specs += [pl.BlockSpec((tm, a.shape[1]), lambda i: (i, 0)), _resident(b.shape)]
        operands += [a, b]
    in_specs += [pl.BlockSpec((tm, w), functools.partial(lambda cb, i: (i, cb), cb)) for _, w, cb in rows]
    in_specs += [pl.BlockSpec(v.shape, functools.partial(lambda nd, i: (0,) * nd, v.ndim)) for v in vecs]
    out_specs = [pl.BlockSpec((tm, w), lambda i: (i, 0)) for (w, _) in outs]
    out_specs += [pl.BlockSpec((1, w), lambda i: (0, 0)) for w in reds]
    out_shape = [jax.ShapeDtypeStruct((m, w), dt) for (w, dt) in outs] + [jax.ShapeDtypeStruct((1, w), F32) for w in reds]
    return pl.pallas_call(
        body,
        name=name,
        grid=(m // tm,),
        in_specs=in_specs,
        out_specs=out_specs,
        out_shape=out_shape,
        compiler_params=pltpu.CompilerParams(dimension_semantics=("arbitrary",), vmem_limit_bytes=_vmem(fixed + tm * per_row)),
    )(*operands, *[r for r, _, _ in rows], *vecs)


def _mm_ew(a, bs, fn, rows, outs, *, name):
    m, kdim = a.shape
    n = bs[0].shape[0]
    nb, nr, no = len(bs), len(rows), len(outs)
    cw = _pick(n, (2 * LANES, LANES))
    fixed = nb * n * kdim * bs[0].dtype.itemsize
    per_row = 2 * (kdim * a.dtype.itemsize + n * (sum(r.dtype.itemsize for r in rows) + sum(jnp.dtype(dt).itemsize for dt in outs)))
    per_row += cw * 4 * 4 * (nb + 4)
    tm = _row_tile(m, fixed, per_row)

    def body(*refs):
        a_ref, b_refs = refs[0], refs[1:1 + nb]
        row_refs, out_refs = refs[1 + nb:1 + nb + nr], refs[1 + nb + nr:]
        av = a_ref[...]
        for c0 in range(0, n, cw):
            cols = slice(c0, c0 + cw)
            o = fn([_dot_nt(av, b[cols, :]) for b in b_refs], [x[:, cols].astype(F32) for x in row_refs])
            for ref, val in zip(out_refs, o):
                ref[:, cols] = val.astype(ref.dtype)

    tile = pl.BlockSpec((tm, n), lambda i: (i, 0))
    return pl.pallas_call(
        body,
        name=name,
        grid=(m // tm,),
        in_specs=[pl.BlockSpec((tm, kdim), lambda i: (i, 0))] + [_resident((n, kdim))] * nb + [tile] * nr,
        out_specs=[tile] * no,
        out_shape=[jax.ShapeDtypeStruct((m, n), dt) for dt in outs],
        compiler_params=pltpu.CompilerParams(dimension_semantics=("parallel",), vmem_limit_bytes=_vmem(fixed + tm * per_row)),
    )(a, *bs, *rows)


def _rowmap(fn, rows, vecs, outs, reds, *, name, tm=None):
    norm = []
    for r in rows:
        if not isinstance(r, tuple):
            norm.append((r, r.shape[1], 0, None))
        elif len(r) == 2:
            norm.append((r[0], r[0].shape[2], 0, r[1]))
        else:
            norm.append((r[0], r[1], r[2], None))
    rows = norm
    t = rows[0][0].shape[-2]
    if tm is None:
        per_row = 2 * sum(w * a.dtype.itemsize for (a, w, _, _) in rows) + 2 * sum(w * jnp.dtype(d).itemsize for (w, d) in outs)
        per_row += 3 * 4 * max([w for (_, w, _, _) in rows] + [w for (w, _) in outs])
        tm = max(8, min(1024, (40 * 2**20) // per_row))
    tm = _pick(t, tuple(c for c in (1024, 512, 256, 128, 64, 32, 16, 8) if c <= tm))
    nr, nv, no = len(rows), len(vecs), len(outs)

    def body(*refs):
        row_refs, vec_refs = refs[:nr], refs[nr:nr + nv]
        out_refs, red_refs = refs[nr + nv:nr + nv + no], refs[nr + nv + no:]
        o, rd = fn([r[...].astype(F32) for r in row_refs], [v[...] for v in vec_refs])
        for ref, val in zip(out_refs, o):
            ref[...] = val.astype(ref.dtype)
        if red_refs:
            @pl.when(pl.program_id(0) == 0)
            def _():
                for ref in red_refs:
                    ref[...] = jnp.zeros_like(ref)

            for ref, val in zip(red_refs, rd):
                ref[...] += val

    in_specs = [pl.BlockSpec((tm, w), functools.partial(lambda cb, i: (i, cb), cb)) if lead is None
                else pl.BlockSpec((None, tm, w), functools.partial(lambda ld, i: (ld, i, 0), lead)) for (_, w, cb, lead) in rows]
    in_specs += [pl.BlockSpec(v.shape, functools.partial(lambda nd, i: (0,) * nd, v.ndim)) for v in vecs]
    out_specs = [pl.BlockSpec((tm, w), lambda i: (i, 0)) for (w, _) in outs]
    out_specs += [pl.BlockSpec((1, w), lambda i: (0, 0)) for w in reds]
    out_shape = [jax.ShapeDtypeStruct((t, w), d) for (w, d) in outs]
    out_shape += [jax.ShapeDtypeStruct((1, w), F32) for w in reds]
    est = 2 * sum(tm * w * a.dtype.itemsize for (a, w, _, _) in rows) + 2 * sum(_nbytes(v.shape, v.dtype) for v in vecs)
    est += 2 * sum(tm * w * jnp.dtype(d).itemsize for (w, d) in outs)
    est += 6 * tm * max([w for (_, w, _, _) in rows] + [w for (w, _) in outs]) * 4
    return pl.pallas_call(
        body,
        name=name,
        grid=(t // tm,),
        in_specs=in_specs,
        out_specs=out_specs,
        out_shape=out_shape,
        compiler_params=pltpu.CompilerParams(dimension_semantics=("arbitrary",), vmem_limit_bytes=_vmem(est)),
    )(*[r[0] for r in rows], *vecs)


def _colsum(v):
    return jnp.sum(v, axis=0, keepdims=True)


def _head_ones():
    i = np.arange(LANES)
    return jnp.asarray((i[:, None] // HEAD_DIM) == (i[None, :] // HEAD_DIM), MXU_DTYPE)


def _headsum(y, ones):
    parts = []
    for j in range(y.shape[1] // LANES):
        c = y[:, j * LANES:(j + 1) * LANES]
        hi = c.astype(MXU_DTYPE)
        lo = c - hi.astype(F32)
        parts.append(_dot(hi, ones) + _dot(lo, ones))
    return parts[0] if len(parts) == 1 else jnp.concatenate(parts, axis=1)


def _rot(y, c, s_lo, s_hi):
    parts = []
    for j in range(y.shape[1] // LANES):
        yc = y[:, j * LANES:(j + 1) * LANES]
        parts.append(yc * c + pltpu.roll(yc, LANES - ROT_DIM // 2, 1) * s_lo + pltpu.roll(yc, ROT_DIM // 2, 1) * s_hi)
    return parts[0] if len(parts) == 1 else jnp.concatenate(parts, axis=1)


def _rot_tables(t):
    half = ROT_DIM // 2
    inv_freq = ROPE_THETA ** (-(jnp.arange(half, dtype=F32) * 2.0) / ROT_DIM)
    ang = jnp.arange(t).astype(F32)[:, None] * inv_freq[None, :]
    cos, sin = jnp.cos(ang), jnp.sin(ang)
    z = lambda w: jnp.zeros((t, w), F32)
    c = jnp.concatenate([cos, cos, jnp.ones((t, HEAD_DIM - ROT_DIM), F32)], axis=1)
    s_lo = jnp.concatenate([-sin, z(HEAD_DIM - half)], axis=1)
    s_hi = jnp.concatenate([z(half), sin, z(HEAD_DIM - ROT_DIM)], axis=1)
    return [jnp.tile(a, (1, LANES // HEAD_DIM)) for a in (c, s_lo, s_hi)]


def _rms(x):
    return lax.rsqrt(jnp.mean(x * x, axis=-1, keepdims=True) + EPS)


def _window(kind, n, bq, t, seg):
    if kind == "na":
        rows = t // GRID_W
        rs = jnp.clip(n - NA_KH // 2, 0, rows - NA_KH)
        return rs
    nk = bq + 2 * DIL_HALF
    return jnp.clip(n * bq - DIL_HALF, 0, t - nk)


def _dil_mask(n, bq, nk, ws, seg):
    qi = n * bq + lax.broadcasted_iota(jnp.int32, (bq, nk), 0)
    ki = ws + lax.broadcasted_iota(jnp.int32, (bq, nk), 1)
    shift = int(np.log2(seg))
    return (jnp.abs(ki - qi) <= DIL_HALF) & ((ki >> shift) == (qi >> shift))


HS = 4
QW = HS * HEAD_DIM


def _head_of_lane(width=QW):
    return lax.broadcasted_iota(jnp.int32, (1, width), 1) // HEAD_DIM


def _stack_heads(a):
    head = _head_of_lane()
    return jnp.concatenate([jnp.where(head == e, a, jnp.zeros_like(a)) for e in range(HS)], axis=0)


def _unstack_heads(a, bq):
    head = _head_of_lane()
    out = jnp.zeros((bq, QW), a.dtype)
    for e in range(HS):
        out = jnp.where(head == e, a[e * bq:(e + 1) * bq], out)
    return out


def _stack_cols(blk, bq):
    head = _head_of_lane()
    return jnp.concatenate(
        [jnp.max(jnp.where(head == e, blk, -jnp.inf), axis=1, keepdims=True) for e in range(HS)], axis=0)


def _attn_geometry(kind):
    if kind == "na":
        return GRID_W, NA_KH * GRID_W, 16
    bq = 128
    return bq, bq + 2 * DIL_HALF, 8


def _attn_scores(kind, n, bq, nk, t, seg, qs, k_ref, b_ref):
    scale = HEAD_DIM ** -0.5
    if kind == "na":
        rs = _window(kind, n, bq, t, seg)
        ws = pl.multiple_of(rs * GRID_W, GRID_W)
        ro0 = rs - n + (NA_KH - 1)
        s = _dot_nt(qs, k_ref[pl.ds(ws, nk), :]) * scale
        s = s + jnp.concatenate(
            [jnp.concatenate([b_ref[e, ro0 + 2 * i] for i in range(NA_KH // 2)], axis=1) for e in range(HS)], axis=0)
        return s, ws, ro0
    ws = pl.multiple_of(_window(kind, n, bq, t, seg), DIL_HALF)
    mask = _dil_mask(n, bq, nk, ws, seg)
    s = _dot_nt(qs, k_ref[pl.ds(ws, nk), :]) * scale
    s = jnp.where(jnp.concatenate([mask] * HS, axis=0), s, NEG_INF)
    return s, ws, None


def _col_operands(*ops):
    pairs = [op if isinstance(op, tuple) else (op, 0) for op in ops]
    width = QW if isinstance(ops[0], tuple) else ops[0].shape[1]
    return (*pairs, width)


def _q_block(rows, col):
    return pl.BlockSpec((rows, QW), lambda j, n: (n, j + col))


def _kv_resident(t, col):
    return pl.BlockSpec((t, QW), lambda j, n: (0, j + col))


def _attn_fwd(q, k, v, *, kind, name, bias=None, seg=None):
    (q, cq), (k, ck), (v, cv), w = _col_operands(q, k, v)
    t = q.shape[0]
    quads = w // QW
    bq, nk, sub = _attn_geometry(kind)
    nq = t // (bq * sub)

    def body(*refs):
        if kind == "na":
            q_ref, k_ref, v_ref, b_ref, o_ref, l_ref = refs
        else:
            (q_ref, k_ref, v_ref, o_ref, l_ref), b_ref = refs, None
        for i in range(sub):
            n = pl.program_id(1) * sub + i
            rows = slice(i * bq, (i + 1) * bq)
            s, ws, _ = _attn_scores(kind, n, bq, nk, t, seg, _stack_heads(q_ref[rows, :]), k_ref, b_ref)
            m = jnp.max(s, axis=1, keepdims=True)
            p = jnp.exp(s - m)
            l = jnp.sum(p, axis=1, keepdims=True)
            o_ref[rows, :] = _unstack_heads(_dot(p / l, v_ref[pl.ds(ws, nk), :]), bq)
            l_ref[rows, :] = _unstack_heads(jnp.broadcast_to(m + jnp.log(l), (HS * bq, QW)), bq)

    blk = pl.BlockSpec((bq * sub, QW), lambda j, n: (n, j))
    in_specs = [_q_block(bq * sub, cq), _kv_resident(t, ck), _kv_resident(t, cv)]
    operands = [q, k, v]
    est = 4 * t * QW * q.dtype.itemsize + 12 * sub * HS * bq * nk * 4
    if kind == "na":
        in_specs.append(pl.BlockSpec((HS,) + bias.shape[1:], lambda j, n: (j, 0, 0, 0)))
        operands.append(bias)
        est += 2 * _nbytes((HS,) + bias.shape[1:], F32)
    return pl.pallas_call(
        body,
        name=name,
        grid=(quads, nq),
        in_specs=in_specs,
        out_specs=[blk, blk],
        out_shape=[jax.ShapeDtypeStruct((t, w), F32)] * 2,
        compiler_params=pltpu.CompilerParams(dimension_semantics=("arbitrary", "arbitrary"), vmem_limit_bytes=_vmem(est)),
    )(*operands)


def _attn_bwd(q, k, v, do, dterm, lse, *, kind, name, bias=None, seg=None):
    (q, cq), (k, ck), (v, cv), (do, cdo), (dterm, cdt), (lse, cl), w = _col_operands(q, k, v, do, dterm, lse)
    t = q.shape[0]
    quads = w // QW
    bq, nk, sub = _attn_geometry(kind)
    nq = t // (bq * sub)
    scale = HEAD_DIM ** -0.5

    def body(*refs):
        if kind == "na":
            q_ref, k_ref, v_ref, do_ref, dt_ref, l_ref, b_ref, dq_ref, dk_hbm, dv_hbm, db_ref, dk_acc, dv_acc, sem = refs
        else:
            q_ref, k_ref, v_ref, do_ref, dt_ref, l_ref, dq_ref, dk_hbm, dv_hbm, dk_acc, dv_acc, sem = refs
            b_ref = None
        j, step = pl.program_id(0), pl.program_id(1)

        @pl.when(step == 0)
        def _():
            dk_acc[...] = jnp.zeros_like(dk_acc)
            dv_acc[...] = jnp.zeros_like(dv_acc)
            if kind == "na":
                db_ref[...] = jnp.zeros_like(db_ref)

        for b in range(sub):
            n = step * sub + b
            rows = slice(b * bq, (b + 1) * bq)
            qs = _stack_heads(q_ref[rows, :])
            dos = _stack_heads(do_ref[rows, :])
            s, ws, ro0 = _attn_scores(kind, n, bq, nk, t, seg, qs, k_ref, b_ref)
            p = jnp.exp(s - _stack_cols(l_ref[rows, :], bq))
            dp = _dot_nt(dos, v_ref[pl.ds(ws, nk), :])
            ds = p * (dp - _stack_cols(dt_ref[rows, :], bq))
            if kind == "na":
                for e in range(HS):
                    for i in range(NA_KH // 2):
                        db_ref[e, ro0 + 2 * i] += ds[e * bq:(e + 1) * bq, i * LANES:(i + 1) * LANES]
            dsc = ds * scale
            dq_ref[rows, :] = _unstack_heads(_dot(dsc, k_ref[pl.ds(ws, nk), :]), bq)
            dk_acc[pl.ds(ws, nk), :] += _dot_tn(dsc, qs)
            dv_acc[pl.ds(ws, nk), :] += _dot_tn(p, dos)

        @pl.when(step == nq - 1)
        def _():
            ck = pltpu.make_async_copy(dk_acc, dk_hbm.at[j], sem.at[0])
            cv = pltpu.make_async_copy(dv_acc, dv_hbm.at[j], sem.at[1])
            ck.start()
            cv.start()
            ck.wait()
            cv.wait()

    blk = pl.BlockSpec((bq * sub, QW), lambda j, n: (n, j))
    in_specs = [_q_block(bq * sub, cq), _kv_resident(t, ck), _kv_resident(t, cv)] + [_q_block(bq * sub, c) for c in (cdo, cdt, cl)]
    operands = [q, k, v, do, dterm, lse]
    out_specs = [blk, ANY, ANY]
    out_shape = [jax.ShapeDtypeStruct((t, w), F32)] + [jax.ShapeDtypeStruct((quads, t, QW), F32)] * 2
    est = 4 * t * QW * q.dtype.itemsize + 2 * t * QW * 4 + 16 * sub * HS * bq * nk * 4
    if kind == "na":
        bspec = pl.BlockSpec((HS,) + bias.shape[1:], lambda j, n: (j, 0, 0, 0))
        in_specs.append(bspec)
        operands.append(bias)
        out_specs.append(bspec)
        out_shape.append(jax.ShapeDtypeStruct(bias.shape, F32))
        est += 4 * _nbytes((HS,) + bias.shape[1:], F32)
    res_ = pl.pallas_call(
        body,
        name=name,
        grid=(quads, nq),
        in_specs=in_specs,
        out_specs=out_specs,
        out_shape=out_shape,
        scratch_shapes=[pltpu.VMEM((t, QW), F32), pltpu.VMEM((t, QW), F32), pltpu.SemaphoreType.DMA((2,))],
        compiler_params=pltpu.CompilerParams(dimension_semantics=("arbitrary", "arbitrary"), vmem_limit_bytes=_vmem(est)),
    )(*operands)
    unquad = lambda a: [(a, i) for i in range(quads)]
    return (res_[0], unquad(res_[1]), unquad(res_[2])) + tuple(res_[3:])


def _na_onehot():
    qc = np.arange(GRID_W)[:, None]
    kc = np.arange(GRID_W)[None, :]
    start = np.clip(qc - NA_KW // 2, 0, GRID_W - NA_KW)
    inwin = (kc >= start) & (kc < start + NA_KW)
    off = kc - qc + (NA_KW - 1)
    e_mat = np.zeros((2, 32, GRID_W, 2, GRID_W), np.float32)
    for e in range(2):
        for c in range(2 * NA_KW - 1):
            e_mat[e, c, :, e, :] = (off == c) & inwin
    neg = np.where(inwin, 0.0, NEG_INF).astype(np.float32)
    neg = np.broadcast_to(neg[:, None, :], (GRID_W, 2, GRID_W)).reshape(1, GRID_W * LANES)
    return jnp.asarray(e_mat.reshape(64, GRID_W * LANES), MXU_DTYPE), jnp.asarray(neg)


def _na_rowpairs(rpb):
    p = jnp.pad(rpb, ((0, 0), (0, 0), (0, 1)))
    return jnp.concatenate([p[:, :-1], p[:, 1:]], axis=-1).reshape(NA_HEADS * (2 * NA_KH - 2), 64)


def _na_bias_table(rpb):
    r2 = _na_rowpairs(rpb)
    e_mat, neg = _na_onehot()

    def body(r_ref, e_ref, n_ref, o_ref):
        hi, mid, lo = _split3(r_ref[...])
        e = e_ref[...]
        o_ref[...] = _dot(hi, e) + _dot(mid, e) + _dot(lo, e) + n_ref[...]

    out = pl.pallas_call(
        body,
        name="na_bias_table",
        out_shape=jax.ShapeDtypeStruct((r2.shape[0], GRID_W * LANES), F32),
        compiler_params=pltpu.CompilerParams(vmem_limit_bytes=_vmem(6 * r2.shape[0] * GRID_W * LANES * 4)),
    )(r2, e_mat, neg)
    return out.reshape(NA_HEADS, 2 * NA_KH - 2, GRID_W, LANES)


def _na_bias_grad(dbt):
    e_mat, _ = _na_onehot()
    flat = dbt.reshape(NA_HEADS * (2 * NA_KH - 2), GRID_W * LANES)

    def body(d_ref, e_ref, o_ref):
        hi, mid, lo = _split3(d_ref[...])
        e = e_ref[...]
        o_ref[...] = _dot_nt(hi, e) + _dot_nt(mid, e) + _dot_nt(lo, e)

    g = pl.pallas_call(
        body,
        name="na_bias_grad",
        out_shape=jax.ShapeDtypeStruct((flat.shape[0], 64), F32),
        compiler_params=pltpu.CompilerParams(vmem_limit_bytes=_vmem(6 * flat.shape[0] * GRID_W * LANES * 4)),
    )(flat, e_mat)
    g = g.reshape(NA_HEADS, 2 * NA_KH - 2, 2, 32)[..., :2 * NA_KW - 1]
    first = jnp.pad(g[:, :, 0], ((0, 0), (0, 1), (0, 0)))
    second = jnp.pad(g[:, :, 1], ((0, 0), (1, 0), (0, 0)))
    return first + second


def _all_gather(arrs, *, name):
    na = len(arrs)

    def body(*refs):
        ins, outs = refs[:na], refs[na:2 * na]
        send_sems, recv_sems, local_sems = refs[2 * na:]
        x, y, c = lax.axis_index("x"), lax.axis_index("y"), lax.axis_index("c")
        me, sibling = (x, y, c), (x, y, 1 - c)
        chips = [(1 - x, y), (x, 1 - y), (1 - x, 1 - y)]

        def rows(a, px, py, pc):
            r = ins[a].shape[0]
            return outs[a].at[pl.ds((4 * px + 2 * py + pc) * r, r), :]

        def copy(a, k, block, to, src=None):
            return pltpu.make_async_remote_copy(
                src_ref=rows(a, *block) if src is None else src, dst_ref=rows(a, *block),
                send_sem=send_sems.at[a, k], recv_sem=recv_sems.at[a, k], device_id=to, device_id_type=MESH)

        mine = [pltpu.make_async_copy(ins[a], rows(a, *me), local_sems.at[a]) for a in range(na)]
        for cp in mine:
            cp.start()
        first = []
        for a in range(na):
            first.append(copy(a, 0, me, sibling, src=ins[a]))
            first += [copy(a, 1 + j, me, (*chip, c), src=ins[a]) for j, chip in enumerate(chips)]
        for cp in first:
            cp.start()
        passed = []
        for j, chip in enumerate(chips):
            for a in range(na):
                copy(a, 1 + j, (*chip, c), me).wait_recv()
                cp = copy(a, 4 + j, (*chip, c), sibling)
                cp.start()
                passed.append(cp)
        for a in range(na):
            copy(a, 0, sibling, me).wait_recv()
        for j, chip in enumerate(chips):
            for a in range(na):
                copy(a, 4 + j, (*chip, 1 - c), me).wait_recv()
        for cp in first + passed:
            cp.wait_send()
        for cp in mine:
            cp.wait()

    return pl.pallas_call(
        body,
        name=name,
        in_specs=[ANY] * na,
        out_specs=[ANY] * na,
        out_shape=[jax.ShapeDtypeStruct((N_DEV * a.shape[0], a.shape[1]), a.dtype) for a in arrs],
        scratch_shapes=[pltpu.SemaphoreType.DMA((na, 7)), pltpu.SemaphoreType.DMA((na, 7)), pltpu.SemaphoreType.DMA((na,))],
    )(*arrs)


HBM = pl.BlockSpec(memory_space=pltpu.HBM)
SEM = pl.BlockSpec(memory_space=pltpu.SEMAPHORE)
EFFECT = pltpu.SideEffectType.DATAFLOW_SIDE_EFFECTING


def _peer_of(k):
    x, y, c = lax.axis_index("x"), lax.axis_index("y"), lax.axis_index("c")
    return x ^ ((k >> 2) & 1), y ^ ((k >> 1) & 1), c ^ (k & 1)


def _split_copies(gather, src_ref, land_ref, send_sems, recv_sems):
    x, y, c = lax.axis_index("x"), lax.axis_index("y"), lax.axis_index("c")
    my = 4 * x + 2 * y + c
    r = src_ref.shape[0] if gather else src_ref.shape[0] // N_DEV
    copies = []
    for k in ((1, 2, 4, 6) if gather == "chip" else range(1, N_DEV)):
        px, py, pc = _peer_of(k)
        if gather:
            src, dst = src_ref, land_ref.at[pl.ds(my * r, r), :]
        else:
            src, dst = src_ref.at[pl.ds((4 * px + 2 * py + pc) * r, r), :], land_ref.at[k - 1]
        copies.append(pltpu.make_async_remote_copy(
            src_ref=src, dst_ref=dst, send_sem=send_sems.at[k - 1], recv_sem=recv_sems.at[k - 1],
            device_id=(px, py, pc), device_id_type=MESH))
    return copies


def _split_start(srcs, lands, *, gather, name, after=None):
    na = len(srcs)
    extra = [] if after is None else [after]

    def body(*refs):
        src_refs, land_refs = refs[:na], refs[na:2 * na]
        outs = refs[2 * na + len(extra):]
        for a in range(na):
            for cp in _split_copies(gather, src_refs[a], land_refs[a], outs[4 * a], outs[4 * a + 1]):
                cp.start()
        outs[4 * na][...] = jnp.zeros_like(outs[4 * na])

    out_shape, out_specs, aliases = [], [], {}
    for a in range(na):
        out_shape += [pltpu.SemaphoreType.DMA((N_DEV - 1,)), pltpu.SemaphoreType.DMA((N_DEV - 1,)),
                      pltpu.HBM(srcs[a].shape, srcs[a].dtype), pltpu.HBM(lands[a].shape, lands[a].dtype)]
        out_specs += [SEM, SEM, HBM, HBM]
        aliases[a] = 4 * a + 2
        aliases[na + a] = 4 * a + 3
    out_shape.append(jax.ShapeDtypeStruct((8, LANES), F32))
    out_specs.append(pl.BlockSpec(memory_space=pltpu.VMEM))
    res = pl.pallas_call(
        body,
        name=name,
        out_shape=tuple(out_shape),
        in_specs=[HBM] * (2 * na) + [ANY] * len(extra),
        out_specs=tuple(out_specs),
        input_output_aliases=aliases,
        compiler_params=pltpu.CompilerParams(has_side_effects=EFFECT),
    )(*[pltpu.with_memory_space_constraint(a, pltpu.HBM) for a in list(srcs) + list(lands)], *extra)
    return [tuple(res[4 * a:4 * a + 4]) for a in range(na)], res[4 * na][0, 0]


def _split_wait(handles, after, *, gather, name):
    na = len(handles)

    def body(*refs):
        src_refs, land_refs = refs[:na], refs[na:2 * na]
        sems = refs[2 * na:4 * na]
        for a in range(na):
            for cp in _split_copies(gather, src_refs[a], land_refs[a], sems[2 * a], sems[2 * a + 1]):
                cp.wait_send()
                cp.wait_recv()

    srcs = [h[2] for h in handles]
    lands = [h[3] for h in handles]
    sems = [s for h in handles for s in h[:2]]
    res = pl.pallas_call(
        body,
        name=name,
        out_shape=tuple(pltpu.HBM(a.shape, a.dtype) for a in srcs + lands),
        in_specs=[HBM] * (2 * na) + [SEM] * (2 * na) + [ANY],
        out_specs=tuple([HBM] * (2 * na)),
        input_output_aliases={i: i for i in range(2 * na)},
        compiler_params=pltpu.CompilerParams(has_side_effects=EFFECT),
    )(*srcs, *lands, *sems, after)
    return list(res[:na]), list(res[na:])


def _forward_copies(land_ref, send_sems, recv_sems):
    x, y, c = lax.axis_index("x"), lax.axis_index("y"), lax.axis_index("c")
    r = land_ref.shape[0] // N_DEV
    copies = []
    for j, k in enumerate((2, 4, 6)):
        px, py, pc = _peer_of(k)
        rows = land_ref.at[pl.ds((4 * px + 2 * py + pc) * r, r), :]
        copies.append(pltpu.make_async_remote_copy(
            src_ref=rows, dst_ref=rows, send_sem=send_sems.at[j], recv_sem=recv_sems.at[j],
            device_id=(x, y, 1 - c), device_id_type=MESH))
    return copies


def _forward_start(land, *, name):
    def body(land_ref, send_sems, recv_sems, land_thru, token):
        for cp in _forward_copies(land_ref, send_sems, recv_sems):
            cp.start()
        token[...] = jnp.zeros_like(token)

    res = pl.pallas_call(
        body,
        name=name,
        out_shape=(pltpu.SemaphoreType.DMA((3,)), pltpu.SemaphoreType.DMA((3,)), pltpu.HBM(land.shape, land.dtype),
                   jax.ShapeDtypeStruct((8, LANES), F32)),
        in_specs=[HBM],
        out_specs=(SEM, SEM, HBM, pl.BlockSpec(memory_space=pltpu.VMEM)),
        input_output_aliases={0: 2},
        compiler_params=pltpu.CompilerParams(has_side_effects=EFFECT),
    )(pltpu.with_memory_space_constraint(land, pltpu.HBM))
    return res[:3]


def _forward_wait(handle, *, name):
    send_sems, recv_sems, land = handle

    def body(land_ref, send_ref, recv_ref, land_out):
        for cp in _forward_copies(land_ref, send_ref, recv_ref):
            cp.wait_send()
            cp.wait_recv()

    return pl.pallas_call(
        body,
        name=name,
        out_shape=pltpu.HBM(land.shape, land.dtype),
        in_specs=[HBM, SEM, SEM],
        out_specs=HBM,
        input_output_aliases={0: 0},
        compiler_params=pltpu.CompilerParams(has_side_effects=EFFECT),
    )(land, send_sems, recv_sems)


def _sum8(own, recv, *, name):
    _, r, w = recv.shape
    fits = lambda c: 2 * c * w * (N_DEV * recv.dtype.itemsize + 4) <= 32 * 2**20
    tr = _pick(r, tuple(c for c in (r // 2, r // 4, 256, 128, 64, 32, 16, 8) if c % 16 == 0 and fits(c)))

    def body(own_ref, a_ref, o_ref):
        acc = own_ref[...].astype(F32)
        for i in range(N_DEV - 1):
            acc = acc + a_ref[i].astype(F32)
        o_ref[...] = acc

    return pl.pallas_call(
        body,
        name=name,
        grid=(r // tr,),
        in_specs=[pl.BlockSpec((tr, w), lambda i: (i, 0)), pl.BlockSpec((N_DEV - 1, tr, w), lambda i: (0, i, 0))],
        out_specs=pl.BlockSpec((tr, w), lambda i: (i, 0)),
        out_shape=jax.ShapeDtypeStruct((r, w), F32),
        compiler_params=pltpu.CompilerParams(dimension_semantics=("parallel",), vmem_limit_bytes=_vmem(4 * N_DEV * tr * w * 4)),
    )(own, recv)


def _adamw(w, g, m, v, *, name):
    def fn(rows, _):
        wv, gv, mv, vv = rows
        m1 = ADAM_B1 * mv + (1.0 - ADAM_B1) * gv
        v1 = ADAM_B2 * vv + (1.0 - ADAM_B2) * jnp.square(gv)
        m_hat = m1 / (1.0 - ADAM_B1 ** ADAM_STEP)
        v_hat = v1 / (1.0 - ADAM_B2 ** ADAM_STEP)
        delta = -ADAM_LR * (m_hat / (jnp.sqrt(v_hat) + ADAM_EPS) + ADAM_WD * wv)
        return [delta, m1, v1], []

    c = w.shape[1]
    return _rowmap(fn, [w, g, m, v], [], [(c, F32)] * 3, [], name=name)


_SMALL = ("b_ada", "g_norm1", "g_norm2", "b_gate", "g_qa", "g_ka", "g_qb", "g_kb", "rpb", "loss")


def _pack_small(parts):
    flat = []
    for nme in _SMALL:
        a = parts[nme].reshape(-1).astype(F32)
        flat.append(jnp.pad(a, (0, (-a.shape[0]) % LANES)))
    flat = jnp.concatenate(flat)
    flat = jnp.pad(flat, (0, (-flat.shape[0]) % (LANES * LANES)))
    return flat.reshape(-1, LANES)


def _unpack_small(packed, shapes):
    flat = packed.reshape(-1)
    out, pos = {}, 0
    for nme in _SMALL:
        n = int(np.prod(shapes[nme]))
        out[nme] = flat[pos:pos + n].reshape(shapes[nme])
        pos += n + (-n) % LANES
    return out


def _to_class(a, d):
    t, w = a.shape
    return a if d == 1 else a.reshape(t // d, d, w).transpose(1, 0, 2).reshape(t, w)


def _from_class(a, d):
    t, w = a.shape
    return a if d == 1 else a.reshape(d, t // d, w).transpose(1, 0, 2).reshape(t, w)


def kernel(x, c, w_ada, b_ada, g_norm1, g_norm2, w_in, b_gate, g_qa, g_ka, g_qb, g_kb, rpb, w_proj_a, w_proj_b, w_o, w_ffn_in, w_ffn_out, loss_target, m_w_ada, m_b_ada, m_g_norm1, m_g_norm2, m_w_in, m_b_gate, m_g_qa, m_g_ka, m_g_qb, m_g_kb, m_rpb, m_w_proj_a, m_w_proj_b, m_w_o, m_w_ffn_in, m_w_ffn_out, v_w_ada, v_b_ada, v_g_norm1, v_g_norm2, v_w_in, v_b_gate, v_g_qa, v_g_ka, v_g_qb, v_g_kb, v_rpb, v_w_proj_a, v_w_proj_b, v_w_o, v_w_ffn_in, v_w_ffn_out):
    t, d = x.shape[1], x.shape[2]
    d_ff = w_ffn_out.shape[1] * N_DEV
    me = 4 * lax.axis_index("x") + 2 * lax.axis_index("y") + lax.axis_index("c")
    xt, tgt = x.reshape(t, d), loss_target.reshape(t, d)
    ones = _head_ones()

    shards = [s.astype(WIRE_DTYPE) for s in (w_in[0].T, w_ffn_in[0].T, w_proj_a[0].T, w_proj_b[0].T, w_o[0], w_ffn_out[0])]
    lands = [lax.dynamic_update_slice(lax.empty((N_DEV * s.shape[0], s.shape[1]), s.dtype), s, (me * s.shape[0], 0))
             for s in shards]

    c_all = _all_gather([jnp.pad(c, ((0, 7), (0, 0)))], name="gather_c")[0][::8]
    c_all = jnp.pad(c_all, ((0, LANES - N_DEV), (0, 0)))

    def mod_body(c_ref, w_ref, b_ref, o_ref, act_ref):
        act = _silu(c_ref[...])
        act_ref[...] = act
        hi, mid, lo = _split3(act)
        w = w_ref[...]
        whi, wmid, wlo = _split3(w)
        acc = _dot(hi, whi) + (_dot(hi, wmid) + _dot(mid, whi)) + (_dot(hi, wlo) + _dot(mid, wmid) + _dot(lo, whi))
        o_ref[...] = acc + b_ref[...]

    ncol = w_ada.shape[2]
    b_ada_mine = lax.dynamic_slice(b_ada, (0, me * ncol), (1, ncol))
    mod_part, c_act = pl.pallas_call(
        mod_body,
        name="ada_mod",
        out_shape=[jax.ShapeDtypeStruct((LANES, ncol), F32), jax.ShapeDtypeStruct((LANES, d), F32)],
        compiler_params=pltpu.CompilerParams(vmem_limit_bytes=_vmem(6 * d * ncol * 4)),
    )(c_all, w_ada[0], b_ada_mine)
    mod_all = _all_gather([mod_part[:N_DEV]], name="gather_mod")[0].reshape(N_DEV, N_DEV, ncol)
    mod = lax.dynamic_index_in_dim(mod_all, me, axis=1, keepdims=False).reshape(6, d)
    sh1, sc1, gt1, sh2, sc2, gt2 = [mod[i:i + 1] for i in range(6)]

    def norm_fwd(rows, vecs):
        (xv,), (g, sc, sh) = rows, vecs
        return [xv * _rms(xv) * g * (1.0 + sc) + sh], []

    w_in_handle, w_token = _split_start(shards[:1], lands[:1], gather="chip", after=mod, name="gather_w_in_start")
    (h,) = _rowmap(norm_fwd, [xt], [g_norm1 + w_token, sc1, sh1], [(d, MXU_DTYPE)], [], name="norm1")
    n_a, n_b = 3 * WA, 3 * WB
    (w_in_t,) = _split_wait(w_in_handle, h, gather="chip", name="gather_w_in_wait")[1]
    w_in_t = _forward_wait(_forward_start(w_in_t, name="gather_w_in_forward_start"), name="gather_w_in_forward_wait")
    w_handles, w_token = _split_start(shards[1:], lands[1:], gather=True, after=w_in_t, name="gather_weights_start")
    w_in_a, w_in_b, w_in_g = w_in_t[:n_a], w_in_t[n_a:n_a + n_b], w_in_t[n_a + n_b:]

    rot_c, rot_lo, rot_hi = _rot_tables(t)
    tile_g = lambda g, heads: jnp.tile(g, (1, heads))

    def qk_fwd(width, rotate):
        def fn(xv, rows, vecs):
            gq, gk, on = vecs
            qkv = []
            for i, g in enumerate((gq, gk)):
                xi = xv[:, i * width:(i + 1) * width]
                r = lax.rsqrt(_headsum(xi * xi, on) * (1.0 / HEAD_DIM) + EPS)
                yi = xi * r * g
                if rotate:
                    yi = _rot(yi, rows[0], rows[1], rows[2])
                qkv.append(yi)
            qkv.append(xv[:, 2 * width:])
            if not rotate:
                return [xv] + qkv, []
            groups = [jnp.concatenate([a[:, g * WB_OUT:(g + 1) * WB_OUT] for a in qkv], axis=1)
                      for g in range(len(DIL_CONFIGS))]
            return [xv] + groups, []
        return fn

    qkv_a, qa, ka, va = _mm_parts_rows(
        [(h, w_in_a, "nt")], qk_fwd(WA, False), [], [tile_g(g_qa, NA_HEADS) + w_token, tile_g(g_ka, NA_HEADS), ones],
        [(3 * WA, ACT_DTYPE)] + [(WA, MXU_DTYPE)] * 3, [], name="proj_a_qknorm")
    qkv_b, *qkv_groups = _mm_parts_rows(
        [(h, w_in_b, "nt")], qk_fwd(WB, True), [rot_c, rot_lo, rot_hi],
        [tile_g(g_qb, DIL_HEADS), tile_g(g_kb, DIL_HEADS), ones],
        [(3 * WB, ACT_DTYPE)] + [(3 * WB_OUT, MXU_DTYPE)] * len(DIL_CONFIGS), [], name="proj_b_qknorm")

    bias_tab = _na_bias_table(rpb[0])
    o_a, lse_a = _attn_fwd(qa, ka, va, kind="na", bias=bias_tab, name="na_fwd")

    grp = []
    for g, (_, dil) in enumerate(DIL_CONFIGS):
        qkv_c = _to_class(qkv_groups[g], dil)
        og, lg = _attn_fwd((qkv_c, 0), (qkv_c, 1), (qkv_c, 2), kind="dil", seg=t // dil, name=f"dil_fwd{g}")
        grp.append(dict(qkv=qkv_c, o=_from_class(og, dil), lse=_from_class(lg, dil), lse_c=lg, dil=dil))

    def merge_fwd(rows, _):
        o0, o1, o2, l0, l1, l2 = rows
        mx = jnp.maximum(jnp.maximum(l0, l1), l2)
        e0, e1, e2 = jnp.exp(l0 - mx), jnp.exp(l1 - mx), jnp.exp(l2 - mx)
        s = e0 + e1 + e2
        return [(e0 / s) * o0 + (e1 / s) * o1 + (e2 / s) * o2], []

    (o_b,) = _rowmap(merge_fwd, [gr["o"] for gr in grp] + [gr["lse"] for gr in grp], [], [(WB_OUT, F32)], [], name="dil_merge")

    w_pa_t, w_pb_t, w_o_f = _split_wait(w_handles[1:4], o_b, gather=True, name="gather_w_out_wait")[1]
    def gate_fwd(prods, _, vecs):
        gv, pav, pbv = prods
        sg = jax.nn.sigmoid(gv + vecs[0])
        return [gv, pav, pbv, sg[:, :d] * pav + sg[:, d:] * pbv], []

    gates, pa, pb, merged = _mm_parts_rows(
        [(h, w_in_g, "nt"), (o_a, w_pa_t, "nt"), (o_b, w_pb_t, "nt")], gate_fwd, [], [b_gate],
        [(2 * d, ACT_DTYPE), (d, ACT_DTYPE), (d, ACT_DTYPE), (d, MXU_DTYPE)], [], separate=True, name="proj_gates_out_merge")
    def resid_norm(av, rows, vecs):
        (xv,), (gt, g, sc, sh) = rows, vecs
        x1v = xv + gt * av
        return [av, x1v, x1v * _rms(x1v) * g * (1.0 + sc) + sh], []

    att, x1, h2 = _mm_parts_rows([(merged, w_o_f)], resid_norm, [xt], [gt1, g_norm2, sc2, sh2],
                           [(d, F32), (d, F32), (d, MXU_DTYPE)], [], name="proj_o_resid_norm2")

    w_ffn_in_t, w_ffn_out_f = _split_wait([w_handles[0], w_handles[4]], h2, gather=True, name="gather_w_ffn_wait")[1]
    w_ffn_a, w_ffn_up = w_ffn_in_t[:d_ff], w_ffn_in_t[d_ff:]

    def swiglu_fwd(prods, _):
        a, up = prods
        return [a, up, _silu(a) * up]

    ua, uu, f = _mm_ew(h2, [w_ffn_a, w_ffn_up], swiglu_fwd, [], [ACT_DTYPE, ACT_DTYPE, MXU_DTYPE], name="ffn_in_swiglu")

    def loss_fn(yv, rows, vecs):
        (x1v, tv), gt = rows, vecs[0]
        err = x1v + gt * yv - tv
        dout = err * (1.0 / d)
        return [dout, dout * gt], [_colsum(err * err), _colsum(dout * yv)]

    dout, dy2, err2, dgt2 = _mm_parts_rows([(f, w_ffn_out_f)], loss_fn, [x1, tgt], [gt2], [(d, F32), (d, MXU_DTYPE)],
                                           [d, d], name="ffn_out_loss")

    dw_ffn_out = _mm(f, dy2, ta=True, out_dtype=WIRE_DTYPE, name="wgrad_ffn_out")
    def swiglu_bwd(prods, rows):
        (dfv,), (a, up) = prods, rows
        sg = jax.nn.sigmoid(a)
        return [dfv * up * (sg * (1.0 + a * (1.0 - sg))), dfv * (a * sg)]

    da, dup = _mm_ew(dy2, [w_ffn_out_f], swiglu_bwd, [ua, uu], [MXU_DTYPE, MXU_DTYPE], name="dgrad_ffn_out_swiglu_bwd")
    dw_ffn_in_t = _mm(da, h2, ta=True, into=(lax.empty((2 * d_ff, d), WIRE_DTYPE), 0), name="wgrad_ffn_in_a")
    dw_ffn_in_t = _mm(dup, h2, ta=True, into=(dw_ffn_in_t, d_ff), name="wgrad_ffn_in_up")
    land7 = lambda a: lax.empty((N_DEV - 1, a.shape[0] // N_DEV, a.shape[1]), a.dtype)
    own_block = lambda a: lax.dynamic_slice(a, (me * (a.shape[0] // N_DEV), 0), (a.shape[0] // N_DEV, a.shape[1]))
    g_ffn = [dw_ffn_in_t, dw_ffn_out]
    h_ffn, tok_ffn = _split_start(g_ffn, [land7(a) for a in g_ffn], gather=False, name="exchange_ffn_start")
    def norm_bwd(dh, xv, g, sc):
        r = _rms(xv)
        xh = xv * r
        dxh = dh * g * (1.0 + sc)
        dxv = r * (dxh - xh * jnp.mean(dxh * xh, axis=-1, keepdims=True))
        return dxv, [_colsum(dh), _colsum(dh * xh * g), _colsum(dh * xh * (1.0 + sc))]

    def norm2_bwd(dhv, rows, vecs):
        (x1v, dov, av), (g, sc, gt) = rows, vecs
        dxv, sums = norm_bwd(dhv, x1v, g, sc)
        dx1v = dov + dxv
        return [dx1v, dx1v * gt], sums + [_colsum(dx1v * av)]

    dx1, datt, dsh2, dsc2, dg2, dgt1 = _mm_parts_rows(
        [(da, w_ffn_a), (dup, w_ffn_up)], norm2_bwd, [x1, dout, att], [g_norm2 + tok_ffn, sc2, gt1],
        [(d, F32), (d, MXU_DTYPE)], [d] * 4, name="dgrad_ffn_in_norm2_bwd")
    dw_o = _mm(merged, datt, ta=True, out_dtype=WIRE_DTYPE, name="wgrad_o")
    def gate_bwd(dm, rows, vecs):
        gv, pav, pbv = rows
        sg = jax.nn.sigmoid(gv + vecs[0])
        ga, gb = sg[:, :d], sg[:, d:]
        dgp = jnp.concatenate([dm * pav * ga * (1.0 - ga), dm * pbv * gb * (1.0 - gb)], axis=1)
        return [dm * ga, dm * gb, dgp], [_colsum(dgp)]

    dpa, dpb, dgates, db_gate = _mm_parts_rows(
        [(datt, w_o_f.T)], gate_bwd, [gates, pa, pb], [b_gate],
        [(d, MXU_DTYPE), (d, MXU_DTYPE), (2 * d, MXU_DTYPE)], [2 * d], name="dgrad_o_gate_bwd")
    dw_pa_t = _mm(dpa, o_a, ta=True, out_dtype=WIRE_DTYPE, name="wgrad_proj_a")
    dw_pb_t = _mm(dpb, o_b, ta=True, out_dtype=WIRE_DTYPE, name="wgrad_proj_b")
    g_out = [dw_pa_t, dw_pb_t, dw_o]
    h_out, tok_out = _split_start(g_out, [land7(a) for a in g_out], gather=False, name="exchange_out_start")
    def delta_a(doa, rows, vecs):
        return [doa, _headsum(doa * rows[0], vecs[0])], []

    do_a, dterm_a = _mm_parts_rows([(dpa, w_pa_t)], delta_a, [o_a], [ones + tok_out.astype(ones.dtype)],
                                   [(WA, F32), (WA, F32)], [], name="dgrad_proj_a_delta")
    dqa, dka, dva, dbias = _attn_bwd(qa, ka, va, do_a, dterm_a, lse_a, kind="na", bias=bias_tab, name="na_bwd")
    g_rpb = _na_bias_grad(dbias)

    def merge_bwd(dob, rows, vecs):
        o0, o1, o2, l0, l1, l2 = rows
        on = vecs[0]
        mx = jnp.maximum(jnp.maximum(l0, l1), l2)
        e0, e1, e2 = jnp.exp(l0 - mx), jnp.exp(l1 - mx), jnp.exp(l2 - mx)
        s = e0 + e1 + e2
        ws = [e0 / s, e1 / s, e2 / s]
        dws = [_headsum(dob * o, on) for o in (o0, o1, o2)]
        mean = ws[0] * dws[0] + ws[1] * dws[1] + ws[2] * dws[2]
        return [jnp.concatenate([w * dob, w * mean], axis=1) for w in ws], []

    mb = _mm_parts_rows([(dpb, w_pb_t)], merge_bwd, [gr["o"] for gr in grp] + [gr["lse"] for gr in grp], [ones],
                        [(2 * WB_OUT, F32)] * len(grp), [], name="dgrad_proj_b_merge_bwd")
    dqb, dkb, dvb = [], [], []
    for g, gr in enumerate(grp):
        dil, qkv_c = gr["dil"], gr["qkv"]
        dd_c = _to_class(mb[g], dil)
        dq, dk, dv = _attn_bwd((qkv_c, 0), (qkv_c, 1), (qkv_c, 2), (dd_c, 0), (dd_c, 1), (gr["lse_c"], 0),
                               kind="dil", seg=t // dil, name=f"dil_bwd{g}")
        dqb.append(_from_class(dq, dil))
        dkb.append(dk[0] if dil == 1 else _from_class(dk[0][0][0], dil))
        dvb.append(dv[0] if dil == 1 else _from_class(dv[0][0][0], dil))

    def qk_bwd(width, rotate, nparts):
        def fn(rows, vecs):
            gq, gk, on = vecs
            xv = rows[0]
            pos = 1
            if rotate:
                rc, rlo, rhi = rows[1:4]
                pos = 4
            cat = lambda parts: parts[0] if len(parts) == 1 else jnp.concatenate(parts, axis=1)
            ends = np.cumsum((pos,) + nparts)
            dq, dk, dv = [cat(rows[ends[i]:ends[i + 1]]) for i in range(3)]
            outs, sums = [], []
            for i, (dy, g) in enumerate(((dq, gq), (dk, gk))):
                if rotate:
                    dy = _rot(dy, rc, -rlo, -rhi)
                xi = xv[:, i * width:(i + 1) * width]
                r = lax.rsqrt(_headsum(xi * xi, on) * (1.0 / HEAD_DIM) + EPS)
                xh = xi * r
                dxh = dy * g
                outs.append(r * (dxh - xh * (_headsum(dxh * xh, on) * (1.0 / HEAD_DIM))))
                sums.append(_colsum(dy * xh))
            return [jnp.concatenate(outs + [dv], axis=1)], sums
        return fn

    dqkv_a, dg_qa, dg_ka = _rowmap(qk_bwd(WA, False, (1, len(dka), len(dva))), [qkv_a, dqa] + dka + dva,
                                   [tile_g(g_qa, NA_HEADS), tile_g(g_ka, NA_HEADS), ones],
                                   [(3 * WA, MXU_DTYPE)], [WA, WA], name="qknorm_a_bwd")
    dqkv_b, dg_qb, dg_kb = _rowmap(qk_bwd(WB, True, (3, 3, 3)), [qkv_b, rot_c, rot_lo, rot_hi] + dqb + dkb + dvb,
                                   [tile_g(g_qb, DIL_HEADS), tile_g(g_kb, DIL_HEADS), ones],
                                   [(3 * WB, MXU_DTYPE)], [WB, WB], name="qknorm_b_bwd")

    dw_in_t = jnp.concatenate([
        _mm(dqkv_a, h, ta=True, out_dtype=WIRE_DTYPE, name="wgrad_in_a"),
        _mm(dqkv_b, h, ta=True, out_dtype=WIRE_DTYPE, name="wgrad_in_b"),
        _mm(dgates, h, ta=True, out_dtype=WIRE_DTYPE, name="wgrad_in_gates")], axis=0)
    h_in, tok_in = _split_start([dw_in_t], [land7(dw_in_t)], gather=False, name="exchange_in_start")
    def norm1_bwd(dhv, rows, vecs):
        xv, dx1v = rows
        dxv, sums = norm_bwd(dhv, xv, vecs[0], vecs[1])
        return [dx1v + dxv], sums

    grad_x, dsh1, dsc1, dg1 = _mm_parts_rows(
        [(dqkv_a, w_in_a), (dqkv_b, w_in_b), (dgates, w_in_g)], norm1_bwd, [xt, dx1], [g_norm1 + tok_in, sc1],
        [(d, F32)], [d] * 3, name="dgrad_in_norm1_bwd")

    heads_sum = lambda a, heads: a.reshape(heads, HEAD_DIM).sum(axis=0)
    dmod = jnp.concatenate([dsh1, dsc1, dgt1, dsh2, dsc2, dgt2], axis=1)
    local_small = _pack_small(dict(
        b_ada=dmod, g_norm1=dg1, g_norm2=dg2, b_gate=db_gate, g_qa=heads_sum(dg_qa, NA_HEADS),
        g_ka=heads_sum(dg_ka, NA_HEADS), g_qb=heads_sum(dg_qb, DIL_HEADS), g_kb=heads_sum(dg_kb, DIL_HEADS),
        rpb=g_rpb, loss=(0.5 / d) * jnp.sum(err2)))
    srows = local_small.shape[0]
    small_all = _all_gather([local_small], name="gather_small")[0].reshape(N_DEV, srows, LANES)
    small_sum = _sum8(small_all[0], small_all[1:], name="sum_small")
    small_shapes = dict(b_ada=b_ada.shape, g_norm1=g_norm1.shape, g_norm2=g_norm2.shape, b_gate=b_gate.shape,
                        g_qa=g_qa.shape, g_ka=g_ka.shape, g_qb=g_qb.shape, g_kb=g_kb.shape, rpb=rpb.shape, loss=())
    small_w = dict(b_ada=b_ada, g_norm1=g_norm1, g_norm2=g_norm2, b_gate=b_gate, g_qa=g_qa, g_ka=g_ka, g_qb=g_qb,
                   g_kb=g_kb, rpb=rpb, loss=jnp.zeros((), F32))
    small_m = dict(b_ada=m_b_ada, g_norm1=m_g_norm1, g_norm2=m_g_norm2, b_gate=m_b_gate, g_qa=m_g_qa, g_ka=m_g_ka,
                   g_qb=m_g_qb, g_kb=m_g_kb, rpb=m_rpb, loss=jnp.zeros((), F32))
    small_v = dict(b_ada=v_b_ada, g_norm1=v_g_norm1, g_norm2=v_g_norm2, b_gate=v_b_gate, g_qa=v_g_qa, g_ka=v_g_ka,
                   g_qb=v_g_qb, g_kb=v_g_kb, rpb=v_rpb, loss=jnp.zeros((), F32))
    s_delta, s_m, s_v = _adamw(_pack_small(small_w), small_sum, _pack_small(small_m), _pack_small(small_v), name="adamw_small")
    gs = _unpack_small(small_sum, small_shapes)
    ds_, ms_, vs_ = [_unpack_small(a, small_shapes) for a in (s_delta, s_m, s_v)]

    dmod_all = small_all[:, :6 * d // LANES].reshape(N_DEV, 6 * d)
    dmod_mine = jnp.pad(lax.dynamic_slice(dmod_all, (0, me * ncol), (N_DEV, ncol)), ((0, LANES - N_DEV), (0, 0)))

    def wada_body(c_ref, dm_ref, o_ref):
        chi, cmid, clo = _split3(c_ref[...])
        dhi, dmid, dlo = _split3(dm_ref[...])
        o_ref[...] = (_dot_tn(chi, dhi) + (_dot_tn(chi, dmid) + _dot_tn(cmid, dhi))
                      + (_dot_tn(chi, dlo) + _dot_tn(cmid, dmid) + _dot_tn(clo, dhi)))

    g_w_ada = pl.pallas_call(
        wada_body,
        name="wgrad_ada",
        out_shape=jax.ShapeDtypeStruct((d, ncol), F32),
        compiler_params=pltpu.CompilerParams(vmem_limit_bytes=_vmem(4 * d * ncol * 4)),
    )(c_act, dmod_mine)

    sent, recv = _split_wait(h_in + h_ffn + h_out, small_sum, gather=False, name="exchange_wait")
    names = ("w_in", "w_ffn_in", "w_ffn_out", "w_proj_a", "w_proj_b", "w_o")
    transposed = (True, True, False, True, True, False)
    big_g = {}
    for nme, own, r, tr in zip(names, sent, recv, transposed):
        s = _sum8(own_block(own), r, name=f"sum_{nme}")
        big_g[nme] = s.T if tr else s
    big_g["w_ada"] = g_w_ada
    big_w = dict(w_ada=w_ada, w_in=w_in, w_proj_a=w_proj_a, w_proj_b=w_proj_b, w_o=w_o, w_ffn_in=w_ffn_in, w_ffn_out=w_ffn_out)
    big_m = dict(w_ada=m_w_ada, w_in=m_w_in, w_proj_a=m_w_proj_a, w_proj_b=m_w_proj_b, w_o=m_w_o, w_ffn_in=m_w_ffn_in, w_ffn_out=m_w_ffn_out)
    big_v = dict(w_ada=v_w_ada, w_in=v_w_in, w_proj_a=v_w_proj_a, w_proj_b=v_w_proj_b, w_o=v_w_o, w_ffn_in=v_w_ffn_in, w_ffn_out=v_w_ffn_out)
    grads, deltas, new_m, new_v = {}, {}, {}, {}
    for nme in big_w:
        dl, m1, v1 = _adamw(big_w[nme][0], big_g[nme], big_m[nme][0], big_v[nme][0], name=f"adamw_{nme}")
        grads[nme], deltas[nme], new_m[nme], new_v[nme] = big_g[nme][None], dl[None], m1[None], v1[None]
    for nme in _SMALL[:-1]:
        grads[nme], deltas[nme], new_m[nme], new_v[nme] = gs[nme], ds_[nme], ms_[nme], vs_[nme]

    order = ("w_ada", "b_ada", "g_norm1", "g_norm2", "w_in", "b_gate", "g_qa", "g_ka", "g_qb", "g_kb", "rpb",
             "w_proj_a", "w_proj_b", "w_o", "w_ffn_in", "w_ffn_out")
    return (gs["loss"], grad_x[None], *[grads[n] for n in order], *[deltas[n] for n in order],
            *[new_m[n] for n in order], *[new_v[n] for n in order])
```

```python
import functools

import numpy as np
import jax
import jax.numpy as jnp
from jax import lax
from jax.experimental import pallas as pl
from jax.experimental.pallas import tpu as pltpu

F32 = jnp.float32
MXU_DTYPE = jnp.bfloat16
WIRE_DTYPE = jnp.bfloat16
ACT_DTYPE = jnp.bfloat16

HEAD_DIM = 64
GRID_W = 64
NA_HEADS = 8
NA_KH = 8
NA_KW = 16
DIL_CONFIGS = ((128, 1), (512, 4), (2048, 16))
DIL_HEADS_PER_GROUP = 4
DIL_HEADS = DIL_HEADS_PER_GROUP * len(DIL_CONFIGS)
DIL_HALF = 64
ROT_DIM = HEAD_DIM // 4
ROPE_THETA = 500000.0
EPS = 1e-6
NEG_INF = -1e30
WA = NA_HEADS * HEAD_DIM
WB = DIL_HEADS * HEAD_DIM
WB_OUT = DIL_HEADS_PER_GROUP * HEAD_DIM
ADAM_LR = 0.001
ADAM_B1 = 0.9
ADAM_B2 = 0.999
ADAM_EPS = 1e-08
ADAM_WD = 0.01
ADAM_STEP = 10

N_DEV = 8
LANES = 128
VMEM_CAP = 60 * 2**20
VMEM_FLOOR = 56 * 2**20
MESH = pl.DeviceIdType.MESH
ANY = pl.BlockSpec(memory_space=pl.ANY)


def _vmem(nbytes):
    return int(min(VMEM_CAP, max(VMEM_FLOOR, nbytes * 5 // 4 + 4 * 2**20)))


def _pick(dim, cands):
    for c in cands:
        if c <= dim and dim % c == 0:
            return c
    return dim


def _nbytes(shape, dtype):
    return int(np.prod(shape)) * jnp.dtype(dtype).itemsize


def _dot(a, b, dims=((1,), (0,))):
    return lax.dot_general(a.astype(MXU_DTYPE), b.astype(MXU_DTYPE), (dims, ((), ())), preferred_element_type=F32)


def _dot_nt(a, b):
    return _dot(a, b, ((1,), (1,)))


def _dot_tn(a, b):
    return _dot(a, b, ((0,), (0,)))


def _split3(a):
    hi = a.astype(jnp.bfloat16)
    r1 = a - hi.astype(F32)
    mid = r1.astype(jnp.bfloat16)
    lo = (r1 - mid.astype(F32)).astype(jnp.bfloat16)
    return hi, mid, lo


def _silu(x):
    return x * jax.nn.sigmoid(x)


def _divisors(dim, unit):
    return [c for c in range(unit, dim + 1, unit) if dim % c == 0] or [dim]


def _mm_tiles(m, n, kdim, a_item, b_item, o_item, row_off=0, whole_n=False):
    step_us, hbm_bytes_per_us, flops_per_us, budget = 0.35, 3.0e6, 8.0e8, 48 * 2**20
    best = None
    for tm in _divisors(m, LANES):
        for tn in ([n] if whole_n else _divisors(n, LANES)):
            for tk in _divisors(kdim, LANES):
                if row_off % tm:
                    continue
                gm, gn, gk = m // tm, n // tn, kdim // tk
                vmem = 2 * (tm * tk * a_item + tk * tn * b_item + tm * tn * o_item) + 2 * (tm * tk + tk * tn)
                vmem += tm * tn * 4 * ((1 if gk > 1 else 0) + 1)
                if vmem > budget:
                    continue
                a_reads = m * kdim * a_item * (gn if gk > 1 else 1)
                traffic = a_reads + kdim * n * b_item * gm + m * n * o_item
                cost = gm * gn * gk * step_us + max(traffic / hbm_bytes_per_us, 2.0 * m * n * kdim / flops_per_us)
                if best is None or cost < best[0]:
                    best = (cost, tm, tn, tk)
    return best[1:]


def _mm(a, b, *, name, ta=False, tb=False, out_dtype=F32, into=None):
    if ta:
        kdim, m = a.shape
    else:
        m, kdim = a.shape
    n = b.shape[0] if tb else b.shape[1]
    assert b.shape[1 if tb else 0] == kdim
    buf, row_off = into if into is not None else (None, 0)
    if buf is not None:
        out_dtype = buf.dtype
    tm, tn, tk = _mm_tiles(m, n, kdim, a.dtype.itemsize, b.dtype.itemsize, jnp.dtype(out_dtype).itemsize, row_off,
                           whole_n=ta and n <= 8 * LANES)
    gm, gn, gk = m // tm, n // tn, kdim // tk
    ob = row_off // tm

    a_spec = pl.BlockSpec((tk, tm), lambda i, j, k: (k, i)) if ta else pl.BlockSpec((tm, tk), lambda i, j, k: (i, k))
    b_spec = pl.BlockSpec((tn, tk), lambda i, j, k: (j, k)) if tb else pl.BlockSpec((tk, tn), lambda i, j, k: (k, j))
    o_spec = pl.BlockSpec((tm, tn), lambda i, j, k: (i + ob, j))
    a_dims = (0,) if ta else (1,)
    b_dims = (1,) if tb else (0,)

    def body(a_ref, b_ref, *rest):
        o_ref, scratch = rest[-1 - (gk > 1)], rest[-(gk > 1):] if gk > 1 else ()
        if gk == 1:
            o_ref[...] = _dot(a_ref[...], b_ref[...], (a_dims, b_dims)).astype(o_ref.dtype)
            return
        (acc_ref,) = scratch
        k = pl.program_id(2)

        @pl.when(k == 0)
        def _():
            acc_ref[...] = jnp.zeros_like(acc_ref)

        acc_ref[...] += _dot(a_ref[...], b_ref[...], (a_dims, b_dims))

        @pl.when(k == gk - 1)
        def _():
            o_ref[...] = acc_ref[...].astype(o_ref.dtype)

    est = 2 * (tm * tk * a.dtype.itemsize + tk * tn * b.dtype.itemsize + tm * tn * jnp.dtype(out_dtype).itemsize)
    est += tm * tn * 4 + 2 * (tm * tk + tk * tn) * 2
    return pl.pallas_call(
        body,
        name=name,
        grid=(gm, gn, gk),
        in_specs=[a_spec, b_spec] + ([ANY] if buf is not None else []),
        out_specs=o_spec,
        out_shape=jax.ShapeDtypeStruct((m, n) if buf is None else buf.shape, out_dtype),
        input_output_aliases={2: 0} if buf is not None else {},
        scratch_shapes=[pltpu.VMEM((tm, tn), F32)] if gk > 1 else [],
        compiler_params=pltpu.CompilerParams(
            dimension_semantics=("parallel", "parallel", "arbitrary"), vmem_limit_bytes=_vmem(est)
        ),
    )(*((a, b) if buf is None else (a, b, buf)))


def _resident(shape):
    return pl.BlockSpec(shape, lambda i: (0,) * len(shape), pipeline_mode=pl.Buffered(1))


def _row_tile(m, fixed_bytes, bytes_per_row, budget=50 * 2**20):
    fits = [tm for tm in _divisors(m, LANES) if fixed_bytes + tm * bytes_per_row <= budget]
    return max(fits) if fits else _divisors(m, LANES)[0]


def _mm_parts_rows(parts, fn, rows, vecs, outs, reds, *, name, separate=False):
    parts = [(p[0], p[1], len(p) > 2) for p in parts]
    rows = [r if isinstance(r, tuple) else (r, r.shape[1], 0) for r in rows]
    m, n = parts[0][0].shape[0], parts[0][1].shape[0 if parts[0][2] else 1]
    npart, nr, nv, no = len(parts), len(rows), len(vecs), len(outs)
    row_bytes = sum(w * r.dtype.itemsize for r, w, _ in rows) + sum(w * jnp.dtype(dt).itemsize for (w, dt) in outs)
    a_row_bytes = sum(a.shape[1] * a.dtype.itemsize for a, _, _ in parts)
    fixed = sum(_nbytes(b.shape, b.dtype) for _, b, _ in parts)
    per_row = 2 * (a_row_bytes + row_bytes) + n * 4 * 5
    tm = _row_tile(m, fixed, per_row)
    sub = min(tm, 2 * LANES)

    def body(*refs):
        ab = refs[:2 * npart]
        row_refs, vec_refs = refs[2 * npart:2 * npart + nr], refs[2 * npart + nr:2 * npart + nr + nv]
        out_refs = refs[2 * npart + nr + nv:2 * npart + nr + nv + no]
        red_refs = refs[2 * npart + nr + nv + no:]
        if red_refs:
            @pl.when(pl.program_id(0) == 0)
            def _():
                for ref in red_refs:
                    ref[...] = jnp.zeros_like(ref)

        vecs_v = [v[...] for v in vec_refs]
        for s0 in range(0, tm, sub):
            sl = slice(s0, s0 + sub)
            prods = [(_dot_nt if nt else _dot)(ab[2 * p][sl, :], ab[2 * p + 1][...]) for p, (_, _, nt) in enumerate(parts)]
            r = prods if separate else functools.reduce(lambda u, v: u + v, prods)
            o, rd = fn(r, [x[sl, :].astype(F32) for x in row_refs], vecs_v)
            for ref, val in zip(out_refs, o):
                ref[sl, :] = val.astype(ref.dtype)
            for ref, val in zip(red_refs, rd):
                ref[...] += val

    in_specs, operands = [], []
    for a, b, _ in parts:
        in_specs += [pl.BlockSpec((tm, a.shape[1]), lambda i: (i, 0)), _resident(b.shape)]
        operands += [a, b]
    in_specs += [pl.BlockSpec((tm, w), functools.partial(lambda cb, i: (i, cb), cb)) for _, w, cb in rows]
    in_specs += [pl.BlockSpec(v.shape, functools.partial(lambda nd, i: (0,) * nd, v.ndim)) for v in vecs]
    out_specs = [pl.BlockSpec((tm, w), lambda i: (i, 0)) for (w, _) in outs]
    out_specs += [pl.BlockSpec((1, w), lambda i: (0, 0)) for w in reds]
    out_shape = [jax.ShapeDtypeStruct((m, w), dt) for (w, dt) in outs] + [jax.ShapeDtypeStruct((1, w), F32) for w in reds]
    return pl.pallas_call(
        body,
        name=name,
        grid=(m // tm,),
        in_specs=in_specs,
        out_specs=out_specs,
        out_shape=out_shape,
        compiler_params=pltpu.CompilerParams(dimension_semantics=("arbitrary",), vmem_limit_bytes=_vmem(fixed + tm * per_row)),
    )(*operands, *[r for r, _, _ in rows], *vecs)


def _mm_ew(a, bs, fn, rows, outs, *, name):
    m, kdim = a.shape
    n = bs[0].shape[0]
    nb, nr, no = len(bs), len(rows), len(outs)
    cw = _pick(n, (2 * LANES, LANES))
    fixed = nb * n * kdim * bs[0].dtype.itemsize
    per_row = 2 * (kdim * a.dtype.itemsize + n * (sum(r.dtype.itemsize for r in rows) + sum(jnp.dtype(dt).itemsize for dt in outs)))
    per_row += cw * 4 * 4 * (nb + 4)
    tm = _row_tile(m, fixed, per_row)

    def body(*refs):
        a_ref, b_refs = refs[0], refs[1:1 + nb]
        row_refs, out_refs = refs[1 + nb:1 + nb + nr], refs[1 + nb + nr:]
        av = a_ref[...]
        for c0 in range(0, n, cw):
            cols = slice(c0, c0 + cw)
            o = fn([_dot_nt(av, b[cols, :]) for b in b_refs], [x[:, cols].astype(F32) for x in row_refs])
            for ref, val in zip(out_refs, o):
                ref[:, cols] = val.astype(ref.dtype)

    tile = pl.BlockSpec((tm, n), lambda i: (i, 0))
    return pl.pallas_call(
        body,
        name=name,
        grid=(m // tm,),
        in_specs=[pl.BlockSpec((tm, kdim), lambda i: (i, 0))] + [_resident((n, kdim))] * nb + [tile] * nr,
        out_specs=[tile] * no,
        out_shape=[jax.ShapeDtypeStruct((m, n), dt) for dt in outs],
        compiler_params=pltpu.CompilerParams(dimension_semantics=("parallel",), vmem_limit_bytes=_vmem(fixed + tm * per_row)),
    )(a, *bs, *rows)


def _rowmap(fn, rows, vecs, outs, reds, *, name, tm=None):
    norm = []
    for r in rows:
        if not isinstance(r, tuple):
            norm.append((r, r.shape[1], 0, None))
        elif len(r) == 2:
            norm.append((r[0], r[0].shape[2], 0, r[1]))
        else:
            norm.append((r[0], r[1], r[2], None))
    rows = norm
    t = rows[0][0].shape[-2]
    if tm is None:
        per_row = 2 * sum(w * a.dtype.itemsize for (a, w, _, _) in rows) + 2 * sum(w * jnp.dtype(d).itemsize for (w, d) in outs)
        per_row += 3 * 4 * max([w for (_, w, _, _) in rows] + [w for (w, _) in outs])
        tm = max(8, min(1024, (40 * 2**20) // per_row))
    tm = _pick(t, tuple(c for c in (1024, 512, 256, 128, 64, 32, 16, 8) if c <= tm))
    nr, nv, no = len(rows), len(vecs), len(outs)

    def body(*refs):
        row_refs, vec_refs = refs[:nr], refs[nr:nr + nv]
        out_refs, red_refs = refs[nr + nv:nr + nv + no], refs[nr + nv + no:]
        o, rd = fn([r[...].astype(F32) for r in row_refs], [v[...] for v in vec_refs])
        for ref, val in zip(out_refs, o):
            ref[...] = val.astype(ref.dtype)
        if red_refs:
            @pl.when(pl.program_id(0) == 0)
            def _():
                for ref in red_refs:
                    ref[...] = jnp.zeros_like(ref)

            for ref, val in zip(red_refs, rd):
                ref[...] += val

    in_specs = [pl.BlockSpec((tm, w), functools.partial(lambda cb, i: (i, cb), cb)) if lead is None
                else pl.BlockSpec((None, tm, w), functools.partial(lambda ld, i: (ld, i, 0), lead)) for (_, w, cb, lead) in rows]
    in_specs += [pl.BlockSpec(v.shape, functools.partial(lambda nd, i: (0,) * nd, v.ndim)) for v in vecs]
    out_specs = [pl.BlockSpec((tm, w), lambda i: (i, 0)) for (w, _) in outs]
    out_specs += [pl.BlockSpec((1, w), lambda i: (0, 0)) for w in reds]
    out_shape = [jax.ShapeDtypeStruct((t, w), d) for (w, d) in outs]
    out_shape += [jax.ShapeDtypeStruct((1, w), F32) for w in reds]
    est = 2 * sum(tm * w * a.dtype.itemsize for (a, w, _, _) in rows) + 2 * sum(_nbytes(v.shape, v.dtype) for v in vecs)
    est += 2 * sum(tm * w * jnp.dtype(d).itemsize for (w, d) in outs)
    est += 6 * tm * max([w for (_, w, _, _) in rows] + [w for (w, _) in outs]) * 4
    return pl.pallas_call(
        body,
        name=name,
        grid=(t // tm,),
        in_specs=in_specs,
        out_specs=out_specs,
        out_shape=out_shape,
        compiler_params=pltpu.CompilerParams(dimension_semantics=("arbitrary",), vmem_limit_bytes=_vmem(est)),
    )(*[r[0] for r in rows], *vecs)


def _colsum(v):
    return jnp.sum(v, axis=0, keepdims=True)


def _head_ones():
    i = np.arange(LANES)
    return jnp.asarray((i[:, None] // HEAD_DIM) == (i[None, :] // HEAD_DIM), MXU_DTYPE)


def _headsum(y, ones):
    parts = []
    for j in range(y.shape[1] // LANES):
        c = y[:, j * LANES:(j + 1) * LANES]
        hi = c.astype(MXU_DTYPE)
        lo = c - hi.astype(F32)
        parts.append(_dot(hi, ones) + _dot(lo, ones))
    return parts[0] if len(parts) == 1 else jnp.concatenate(parts, axis=1)


def _rot(y, c, s_lo, s_hi):
    parts = []
    for j in range(y.shape[1] // LANES):
        yc = y[:, j * LANES:(j + 1) * LANES]
        parts.append(yc * c + pltpu.roll(yc, LANES - ROT_DIM // 2, 1) * s_lo + pltpu.roll(yc, ROT_DIM // 2, 1) * s_hi)
    return parts[0] if len(parts) == 1 else jnp.concatenate(parts, axis=1)


def _rot_tables(t):
    half = ROT_DIM // 2
    inv_freq = ROPE_THETA ** (-(jnp.arange(half, dtype=F32) * 2.0) / ROT_DIM)
    ang = jnp.arange(t).astype(F32)[:, None] * inv_freq[None, :]
    cos, sin = jnp.cos(ang), jnp.sin(ang)
    z = lambda w: jnp.zeros((t, w), F32)
    c = jnp.concatenate([cos, cos, jnp.ones((t, HEAD_DIM - ROT_DIM), F32)], axis=1)
    s_lo = jnp.concatenate([-sin, z(HEAD_DIM - half)], axis=1)
    s_hi = jnp.concatenate([z(half), sin, z(HEAD_DIM - ROT_DIM)], axis=1)
    return [jnp.tile(a, (1, LANES // HEAD_DIM)) for a in (c, s_lo, s_hi)]


def _rms(x):
    return lax.rsqrt(jnp.mean(x * x, axis=-1, keepdims=True) + EPS)


def _window(kind, n, bq, t, seg):
    if kind == "na":
        rows = t // GRID_W
        rs = jnp.clip(n - NA_KH // 2, 0, rows - NA_KH)
        return rs
    nk = bq + 2 * DIL_HALF
    return jnp.clip(n * bq - DIL_HALF, 0, t - nk)


def _dil_mask(n, bq, nk, ws, seg):
    qi = n * bq + lax.broadcasted_iota(jnp.int32, (bq, nk), 0)
    ki = ws + lax.broadcasted_iota(jnp.int32, (bq, nk), 1)
    shift = int(np.log2(seg))
    return (jnp.abs(ki - qi) <= DIL_HALF) & ((ki >> shift) == (qi >> shift))


HS = 4
QW = HS * HEAD_DIM


def _head_of_lane(width=QW):
    return lax.broadcasted_iota(jnp.int32, (1, width), 1) // HEAD_DIM


def _stack_heads(a):
    head = _head_of_lane()
    return jnp.concatenate([jnp.where(head == e, a, jnp.zeros_like(a)) for e in range(HS)], axis=0)


def _unstack_heads(a, bq):
    head = _head_of_lane()
    out = jnp.zeros((bq, QW), a.dtype)
    for e in range(HS):
        out = jnp.where(head == e, a[e * bq:(e + 1) * bq], out)
    return out


def _stack_cols(blk, bq):
    head = _head_of_lane()
    return jnp.concatenate(
        [jnp.max(jnp.where(head == e, blk, -jnp.inf), axis=1, keepdims=True) for e in range(HS)], axis=0)


def _attn_geometry(kind):
    if kind == "na":
        return GRID_W, NA_KH * GRID_W, 16
    bq = 128
    return bq, bq + 2 * DIL_HALF, 8


def _attn_scores(kind, n, bq, nk, t, seg, qs, k_ref, b_ref):
    scale = HEAD_DIM ** -0.5
    if kind == "na":
        rs = _window(kind, n, bq, t, seg)
        ws = pl.multiple_of(rs * GRID_W, GRID_W)
        ro0 = rs - n + (NA_KH - 1)
        s = _dot_nt(qs, k_ref[pl.ds(ws, nk), :]) * scale
        s = s + jnp.concatenate(
            [jnp.concatenate([b_ref[e, ro0 + 2 * i] for i in range(NA_KH // 2)], axis=1) for e in range(HS)], axis=0)
        return s, ws, ro0
    ws = pl.multiple_of(_window(kind, n, bq, t, seg), DIL_HALF)
    mask = _dil_mask(n, bq, nk, ws, seg)
    s = _dot_nt(qs, k_ref[pl.ds(ws, nk), :]) * scale
    s = jnp.where(jnp.concatenate([mask] * HS, axis=0), s, NEG_INF)
    return s, ws, None


def _col_operands(*ops):
    pairs = [op if isinstance(op, tuple) else (op, 0) for op in ops]
    width = QW if isinstance(ops[0], tuple) else ops[0].shape[1]
    return (*pairs, width)


def _q_block(rows, col):
    return pl.BlockSpec((rows, QW), lambda j, n: (n, j + col))


def _kv_resident(t, col):
    return pl.BlockSpec((t, QW), lambda j, n: (0, j + col))


def _attn_fwd(q, k, v, *, kind, name, bias=None, seg=None):
    (q, cq), (k, ck), (v, cv), w = _col_operands(q, k, v)
    t = q.shape[0]
    quads = w // QW
    bq, nk, sub = _attn_geometry(kind)
    nq = t // (bq * sub)

    def body(*refs):
        if kind == "na":
            q_ref, k_ref, v_ref, b_ref, o_ref, l_ref = refs
        else:
            (q_ref, k_ref, v_ref, o_ref, l_ref), b_ref = refs, None
        for i in range(sub):
            n = pl.program_id(1) * sub + i
            rows = slice(i * bq, (i + 1) * bq)
            s, ws, _ = _attn_scores(kind, n, bq, nk, t, seg, _stack_heads(q_ref[rows, :]), k_ref, b_ref)
            m = jnp.max(s, axis=1, keepdims=True)
            p = jnp.exp(s - m)
            l = jnp.sum(p, axis=1, keepdims=True)
            o_ref[rows, :] = _unstack_heads(_dot(p / l, v_ref[pl.ds(ws, nk), :]), bq)
            l_ref[rows, :] = _unstack_heads(jnp.broadcast_to(m + jnp.log(l), (HS * bq, QW)), bq)

    blk = pl.BlockSpec((bq * sub, QW), lambda j, n: (n, j))
    in_specs = [_q_block(bq * sub, cq), _kv_resident(t, ck), _kv_resident(t, cv)]
    operands = [q, k, v]
    est = 4 * t * QW * q.dtype.itemsize + 12 * sub * HS * bq * nk * 4
    if kind == "na":
        in_specs.append(pl.BlockSpec((HS,) + bias.shape[1:], lambda j, n: (j, 0, 0, 0)))
        operands.append(bias)
        est += 2 * _nbytes((HS,) + bias.shape[1:], F32)
    return pl.pallas_call(
        body,
        name=name,
        grid=(quads, nq),
        in_specs=in_specs,
        out_specs=[blk, blk],
        out_shape=[jax.ShapeDtypeStruct((t, w), F32)] * 2,
        compiler_params=pltpu.CompilerParams(dimension_semantics=("arbitrary", "arbitrary"), vmem_limit_bytes=_vmem(est)),
    )(*operands)


def _attn_bwd(q, k, v, do, dterm, lse, *, kind, name, bias=None, seg=None):
    (q, cq), (k, ck), (v, cv), (do, cdo), (dterm, cdt), (lse, cl), w = _col_operands(q, k, v, do, dterm, lse)
    t = q.shape[0]
    quads = w // QW
    bq, nk, sub = _attn_geometry(kind)
    nq = t // (bq * sub)
    scale = HEAD_DIM ** -0.5

    def body(*refs):
        if kind == "na":
            q_ref, k_ref, v_ref, do_ref, dt_ref, l_ref, b_ref, dq_ref, dk_hbm, dv_hbm, db_ref, dk_acc, dv_acc, sem = refs
        else:
            q_ref, k_ref, v_ref, do_ref, dt_ref, l_ref, dq_ref, dk_hbm, dv_hbm, dk_acc, dv_acc, sem = refs
            b_ref = None
        j, step = pl.program_id(0), pl.program_id(1)

        @pl.when(step == 0)
        def _():
            dk_acc[...] = jnp.zeros_like(dk_acc)
            dv_acc[...] = jnp.zeros_like(dv_acc)
            if kind == "na":
                db_ref[...] = jnp.zeros_like(db_ref)

        for b in range(sub):
            n = step * sub + b
            rows = slice(b * bq, (b + 1) * bq)
            qs = _stack_heads(q_ref[rows, :])
            dos = _stack_heads(do_ref[rows, :])
            s, ws, ro0 = _attn_scores(kind, n, bq, nk, t, seg, qs, k_ref, b_ref)
            p = jnp.exp(s - _stack_cols(l_ref[rows, :], bq))
            dp = _dot_nt(dos, v_ref[pl.ds(ws, nk), :])
            ds = p * (dp - _stack_cols(dt_ref[rows, :], bq))
            if kind == "na":
                for e in range(HS):
                    for i in range(NA_KH // 2):
                        db_ref[e, ro0 + 2 * i] += ds[e * bq:(e + 1) * bq, i * LANES:(i + 1) * LANES]
            dsc = ds * scale
            dq_ref[rows, :] = _unstack_heads(_dot(dsc, k_ref[pl.ds(ws, nk), :]), bq)
            dk_acc[pl.ds(ws, nk), :] += _dot_tn(dsc, qs)
            dv_acc[pl.ds(ws, nk), :] += _dot_tn(p, dos)

        @pl.when(step == nq - 1)
        def _():
            ck = pltpu.make_async_copy(dk_acc, dk_hbm.at[j], sem.at[0])
            cv = pltpu.make_async_copy(dv_acc, dv_hbm.at[j], sem.at[1])
            ck.start()
            cv.start()
            ck.wait()
            cv.wait()

    blk = pl.BlockSpec((bq * sub, QW), lambda j, n: (n, j))
    in_specs = [_q_block(bq * sub, cq), _kv_resident(t, ck), _kv_resident(t, cv)] + [_q_block(bq * sub, c) for c in (cdo, cdt, cl)]
    operands = [q, k, v, do, dterm, lse]
    out_specs = [blk, ANY, ANY]
    out_shape = [jax.ShapeDtypeStruct((t, w), F32)] + [jax.ShapeDtypeStruct((quads, t, QW), F32)] * 2
    est = 4 * t * QW * q.dtype.itemsize + 2 * t * QW * 4 + 16 * sub * HS * bq * nk * 4
    if kind == "na":
        bspec = pl.BlockSpec((HS,) + bias.shape[1:], lambda j, n: (j, 0, 0, 0))
        in_specs.append(bspec)
        operands.append(bias)
        out_specs.append(bspec)
        out_shape.append(jax.ShapeDtypeStruct(bias.shape, F32))
        est += 4 * _nbytes((HS,) + bias.shape[1:], F32)
    res_ = pl.pallas_call(
        body,
        name=name,
        grid=(quads, nq),
        in_specs=in_specs,
        out_specs=out_specs,
        out_shape=out_shape,
        scratch_shapes=[pltpu.VMEM((t, QW), F32), pltpu.VMEM((t, QW), F32), pltpu.SemaphoreType.DMA((2,))],
        compiler_params=pltpu.CompilerParams(dimension_semantics=("arbitrary", "arbitrary"), vmem_limit_bytes=_vmem(est)),
    )(*operands)
    unquad = lambda a: [(a, i) for i in range(quads)]
    return (res_[0], unquad(res_[1]), unquad(res_[2])) + tuple(res_[3:])


def _na_onehot():
    qc = np.arange(GRID_W)[:, None]
    kc = np.arange(GRID_W)[None, :]
    start = np.clip(qc - NA_KW // 2, 0, GRID_W - NA_KW)
    inwin = (kc >= start) & (kc < start + NA_KW)
    off = kc - qc + (NA_KW - 1)
    e_mat = np.zeros((2, 32, GRID_W, 2, GRID_W), np.float32)
    for e in range(2):
        for c in range(2 * NA_KW - 1):
            e_mat[e, c, :, e, :] = (off == c) & inwin
    neg = np.where(inwin, 0.0, NEG_INF).astype(np.float32)
    neg = np.broadcast_to(neg[:, None, :], (GRID_W, 2, GRID_W)).reshape(1, GRID_W * LANES)
    return jnp.asarray(e_mat.reshape(64, GRID_W * LANES), MXU_DTYPE), jnp.asarray(neg)


def _na_rowpairs(rpb):
    p = jnp.pad(rpb, ((0, 0), (0, 0), (0, 1)))
    return jnp.concatenate([p[:, :-1], p[:, 1:]], axis=-1).reshape(NA_HEADS * (2 * NA_KH - 2), 64)


def _na_bias_table(rpb):
    r2 = _na_rowpairs(rpb)
    e_mat, neg = _na_onehot()

    def body(r_ref, e_ref, n_ref, o_ref):
        hi, mid, lo = _split3(r_ref[...])
        e = e_ref[...]
        o_ref[...] = _dot(hi, e) + _dot(mid, e) + _dot(lo, e) + n_ref[...]

    out = pl.pallas_call(
        body,
        name="na_bias_table",
        out_shape=jax.ShapeDtypeStruct((r2.shape[0], GRID_W * LANES), F32),
        compiler_params=pltpu.CompilerParams(vmem_limit_bytes=_vmem(6 * r2.shape[0] * GRID_W * LANES * 4)),
    )(r2, e_mat, neg)
    return out.reshape(NA_HEADS, 2 * NA_KH - 2, GRID_W, LANES)


def _na_bias_grad(dbt):
    e_mat, _ = _na_onehot()
    flat = dbt.reshape(NA_HEADS * (2 * NA_KH - 2), GRID_W * LANES)

    def body(d_ref, e_ref, o_ref):
        hi, mid, lo = _split3(d_ref[...])
        e = e_ref[...]
        o_ref[...] = _dot_nt(hi, e) + _dot_nt(mid, e) + _dot_nt(lo, e)

    g = pl.pallas_call(
        body,
        name="na_bias_grad",
        out_shape=jax.ShapeDtypeStruct((flat.shape[0], 64), F32),
        compiler_params=pltpu.CompilerParams(vmem_limit_bytes=_vmem(6 * flat.shape[0] * GRID_W * LANES * 4)),
    )(flat, e_mat)
    g = g.reshape(NA_HEADS, 2 * NA_KH - 2, 2, 32)[..., :2 * NA_KW - 1]
    first = jnp.pad(g[:, :, 0], ((0, 0), (0, 1), (0, 0)))
    second = jnp.pad(g[:, :, 1], ((0, 0), (1, 0), (0, 0)))
    return first + second


def _all_gather(arrs, *, name):
    na = len(arrs)

    def body(*refs):
        ins, outs = refs[:na], refs[na:2 * na]
        send_sems, recv_sems, local_sems = refs[2 * na:]
        x, y, c = lax.axis_index("x"), lax.axis_index("y"), lax.axis_index("c")
        me, sibling = (x, y, c), (x, y, 1 - c)
        chips = [(1 - x, y), (x, 1 - y), (1 - x, 1 - y)]

        def rows(a, px, py, pc):
            r = ins[a].shape[0]
            return outs[a].at[pl.ds((4 * px + 2 * py + pc) * r, r), :]

        def copy(a, k, block, to, src=None):
            return pltpu.make_async_remote_copy(
                src_ref=rows(a, *block) if src is None else src, dst_ref=rows(a, *block),
                send_sem=send_sems.at[a, k], recv_sem=recv_sems.at[a, k], device_id=to, device_id_type=MESH)

        mine = [pltpu.make_async_copy(ins[a], rows(a, *me), local_sems.at[a]) for a in range(na)]
        for cp in mine:
            cp.start()
        first = []
        for a in range(na):
            first.append(copy(a, 0, me, sibling, src=ins[a]))
            first += [copy(a, 1 + j, me, (*chip, c), src=ins[a]) for j, chip in enumerate(chips)]
        for cp in first:
            cp.start()
        passed = []
        for j, chip in enumerate(chips):
            for a in range(na):
                copy(a, 1 + j, (*chip, c), me).wait_recv()
                cp = copy(a, 4 + j, (*chip, c), sibling)
                cp.start()
                passed.append(cp)
        for a in range(na):
            copy(a, 0, sibling, me).wait_recv()
        for j, chip in enumerate(chips):
            for a in range(na):
                copy(a, 4 + j, (*chip, 1 - c), me).wait_recv()
        for cp in first + passed:
            cp.wait_send()
        for cp in mine:
            cp.wait()

    return pl.pallas_call(
        body,
        name=name,
        in_specs=[ANY] * na,
        out_specs=[ANY] * na,
        out_shape=[jax.ShapeDtypeStruct((N_DEV * a.shape[0], a.shape[1]), a.dtype) for a in arrs],
        scratch_shapes=[pltpu.SemaphoreType.DMA((na, 7)), pltpu.SemaphoreType.DMA((na, 7)), pltpu.SemaphoreType.DMA((na,))],
    )(*arrs)


HBM = pl.BlockSpec(memory_space=pltpu.HBM)
SEM = pl.BlockSpec(memory_space=pltpu.SEMAPHORE)
EFFECT = pltpu.SideEffectType.DATAFLOW_SIDE_EFFECTING


def _peer_of(k):
    x, y, c = lax.axis_index("x"), lax.axis_index("y"), lax.axis_index("c")
    return x ^ ((k >> 2) & 1), y ^ ((k >> 1) & 1), c ^ (k & 1)


def _split_copies(gather, src_ref, land_ref, send_sems, recv_sems):
    x, y, c = lax.axis_index("x"), lax.axis_index("y"), lax.axis_index("c")
    my = 4 * x + 2 * y + c
    r = src_ref.shape[0] if gather else src_ref.shape[0] // N_DEV
    copies = []
    for k in ((1, 2, 4, 6) if gather == "chip" else range(1, N_DEV)):
        px, py, pc = _peer_of(k)
        if gather:
            src, dst = src_ref, land_ref.at[pl.ds(my * r, r), :]
        else:
            src, dst = src_ref.at[pl.ds((4 * px + 2 * py + pc) * r, r), :], land_ref.at[k - 1]
        copies.append(pltpu.make_async_remote_copy(
            src_ref=src, dst_ref=dst, send_sem=send_sems.at[k - 1], recv_sem=recv_sems.at[k - 1],
            device_id=(px, py, pc), device_id_type=MESH))
    return copies


def _split_start(srcs, lands, *, gather, name, after=None):
    na = len(srcs)
    extra = [] if after is None else [after]

    def body(*refs):
        src_refs, land_refs = refs[:na], refs[na:2 * na]
        outs = refs[2 * na + len(extra):]
        for a in range(na):
            for cp in _split_copies(gather, src_refs[a], land_refs[a], outs[4 * a], outs[4 * a + 1]):
                cp.start()
        outs[4 * na][...] = jnp.zeros_like(outs[4 * na])

    out_shape, out_specs, aliases = [], [], {}
    for a in range(na):
        out_shape += [pltpu.SemaphoreType.DMA((N_DEV - 1,)), pltpu.SemaphoreType.DMA((N_DEV - 1,)),
                      pltpu.HBM(srcs[a].shape, srcs[a].dtype), pltpu.HBM(lands[a].shape, lands[a].dtype)]
        out_specs += [SEM, SEM, HBM, HBM]
        aliases[a] = 4 * a + 2
        aliases[na + a] = 4 * a + 3
    out_shape.append(jax.ShapeDtypeStruct((8, LANES), F32))
    out_specs.append(pl.BlockSpec(memory_space=pltpu.VMEM))
    res = pl.pallas_call(
        body,
        name=name,
        out_shape=tuple(out_shape),
        in_specs=[HBM] * (2 * na) + [ANY] * len(extra),
        out_specs=tuple(out_specs),
        input_output_aliases=aliases,
        compiler_params=pltpu.CompilerParams(has_side_effects=EFFECT),
    )(*[pltpu.with_memory_space_constraint(a, pltpu.HBM) for a in list(srcs) + list(lands)], *extra)
    return [tuple(res[4 * a:4 * a + 4]) for a in range(na)], res[4 * na][0, 0]


def _split_wait(handles, after, *, gather, name):
    na = len(handles)

    def body(*refs):
        src_refs, land_refs = refs[:na], refs[na:2 * na]
        sems = refs[2 * na:4 * na]
        for a in range(na):
            for cp in _split_copies(gather, src_refs[a], land_refs[a], sems[2 * a], sems[2 * a + 1]):
                cp.wait_send()
                cp.wait_recv()

    srcs = [h[2] for h in handles]
    lands = [h[3] for h in handles]
    sems = [s for h in handles for s in h[:2]]
    res = pl.pallas_call(
        body,
        name=name,
        out_shape=tuple(pltpu.HBM(a.shape, a.dtype) for a in srcs + lands),
        in_specs=[HBM] * (2 * na) + [SEM] * (2 * na) + [ANY],
        out_specs=tuple([HBM] * (2 * na)),
        input_output_aliases={i: i for i in range(2 * na)},
        compiler_params=pltpu.CompilerParams(has_side_effects=EFFECT),
    )(*srcs, *lands, *sems, after)
    return list(res[:na]), list(res[na:])


def _forward_copies(land_ref, send_sems, recv_sems):
    x, y, c = lax.axis_index("x"), lax.axis_index("y"), lax.axis_index("c")
    r = land_ref.shape[0] // N_DEV
    copies = []
    for j, k in enumerate((2, 4, 6)):
        px, py, pc = _peer_of(k)
        rows = land_ref.at[pl.ds((4 * px + 2 * py + pc) * r, r), :]
        copies.append(pltpu.make_async_remote_copy(
            src_ref=rows, dst_ref=rows, send_sem=send_sems.at[j], recv_sem=recv_sems.at[j],
            device_id=(x, y, 1 - c), device_id_type=MESH))
    return copies


def _forward_start(land, *, name):
    def body(land_ref, send_sems, recv_sems, land_thru, token):
        for cp in _forward_copies(land_ref, send_sems, recv_sems):
            cp.start()
        token[...] = jnp.zeros_like(token)

    res = pl.pallas_call(
        body,
        name=name,
        out_shape=(pltpu.SemaphoreType.DMA((3,)), pltpu.SemaphoreType.DMA((3,)), pltpu.HBM(land.shape, land.dtype),
                   jax.ShapeDtypeStruct((8, LANES), F32)),
        in_specs=[HBM],
        out_specs=(SEM, SEM, HBM, pl.BlockSpec(memory_space=pltpu.VMEM)),
        input_output_aliases={0: 2},
        compiler_params=pltpu.CompilerParams(has_side_effects=EFFECT),
    )(pltpu.with_memory_space_constraint(land, pltpu.HBM))
    return res[:3]


def _forward_wait(handle, *, name):
    send_sems, recv_sems, land = handle

    def body(land_ref, send_ref, recv_ref, land_out):
        for cp in _forward_copies(land_ref, send_ref, recv_ref):
            cp.wait_send()
            cp.wait_recv()

    return pl.pallas_call(
        body,
        name=name,
        out_shape=pltpu.HBM(land.shape, land.dtype),
        in_specs=[HBM, SEM, SEM],
        out_specs=HBM,
        input_output_aliases={0: 0},
        compiler_params=pltpu.CompilerParams(has_side_effects=EFFECT),
    )(land, send_sems, recv_sems)


def _sum8(own, recv, *, name):
    _, r, w = recv.shape
    fits = lambda c: 2 * c * w * (N_DEV * recv.dtype.itemsize + 4) <= 32 * 2**20
    tr = _pick(r, tuple(c for c in (r // 2, r // 4, 256, 128, 64, 32, 16, 8) if c % 16 == 0 and fits(c)))

    def body(own_ref, a_ref, o_ref):
        acc = own_ref[...].astype(F32)
        for i in range(N_DEV - 1):
            acc = acc + a_ref[i].astype(F32)
        o_ref[...] = acc

    return pl.pallas_call(
        body,
        name=name,
        grid=(r // tr,),
        in_specs=[pl.BlockSpec((tr, w), lambda i: (i, 0)), pl.BlockSpec((N_DEV - 1, tr, w), lambda i: (0, i, 0))],
        out_specs=pl.BlockSpec((tr, w), lambda i: (i, 0)),
        out_shape=jax.ShapeDtypeStruct((r, w), F32),
        compiler_params=pltpu.CompilerParams(dimension_semantics=("parallel",), vmem_limit_bytes=_vmem(4 * N_DEV * tr * w * 4)),
    )(own, recv)


def _adamw(w, g, m, v, *, name):
    def fn(rows, _):
        wv, gv, mv, vv = rows
        m1 = ADAM_B1 * mv + (1.0 - ADAM_B1) * gv
        v1 = ADAM_B2 * vv + (1.0 - ADAM_B2) * jnp.square(gv)
        m_hat = m1 / (1.0 - ADAM_B1 ** ADAM_STEP)
        v_hat = v1 / (1.0 - ADAM_B2 ** ADAM_STEP)
        delta = -ADAM_LR * (m_hat / (jnp.sqrt(v_hat) + ADAM_EPS) + ADAM_WD * wv)
        return [delta, m1, v1], []

    c = w.shape[1]
    return _rowmap(fn, [w, g, m, v], [], [(c, F32)] * 3, [], name=name)


_SMALL = ("b_ada", "g_norm1", "g_norm2", "b_gate", "g_qa", "g_ka", "g_qb", "g_kb", "rpb", "loss")


def _pack_small(parts):
    flat = []
    for nme in _SMALL:
        a = parts[nme].reshape(-1).astype(F32)
        flat.append(jnp.pad(a, (0, (-a.shape[0]) % LANES)))
    flat = jnp.concatenate(flat)
    flat = jnp.pad(flat, (0, (-flat.shape[0]) % (LANES * LANES)))
    return flat.reshape(-1, LANES)


def _unpack_small(packed, shapes):
    flat = packed.reshape(-1)
    out, pos = {}, 0
    for nme in _SMALL:
        n = int(np.prod(shapes[nme]))
        out[nme] = flat[pos:pos + n].reshape(shapes[nme])
        pos += n + (-n) % LANES
    return out


_RELAYOUTS = [0]


def _strided_relayout(a, d, to_class):
    t, w = a.shape
    seg = t // d
    c = _pick(seg, (512, 256, 128, 64, 32, 16, 8))
    _RELAYOUTS[0] += 1
    tok = pl.BlockSpec((c * d, LANES), lambda i, j: (i, j))
    cls = pl.BlockSpec((d, c, LANES), lambda i, j: (0, i, j))

    def body(src, dst):
        for r in range(d):
            if to_class:
                dst[r] = src[pl.ds(r, c, stride=d), :]
            else:
                dst[pl.ds(r, c, stride=d), :] = src[r]

    out = pl.pallas_call(
        body,
        name=f"class_relayout_{_RELAYOUTS[0]}",
        grid=(seg // c, w // LANES),
        in_specs=[tok if to_class else cls],
        out_specs=cls if to_class else tok,
        out_shape=jax.ShapeDtypeStruct((d, seg, w) if to_class else (t, w), a.dtype),
        compiler_params=pltpu.CompilerParams(dimension_semantics=("parallel", "parallel"),
                                             vmem_limit_bytes=_vmem(4 * c * d * LANES * 4)),
    )(a if to_class else a.reshape(d, seg, w))
    return out.reshape(t, w)


def _to_class(a, d):
    t, w = a.shape
    if d == 1:
        return a
    if a.dtype == F32:
        return _strided_relayout(a, d, True)
    return a.reshape(t // d, d, w).transpose(1, 0, 2).reshape(t, w)


def _from_class(a, d):
    t, w = a.shape
    if d == 1:
        return a
    if a.dtype == F32:
        return _strided_relayout(a, d, False)
    return a.reshape(d, t // d, w).transpose(1, 0, 2).reshape(t, w)


def kernel(x, c, w_ada, b_ada, g_norm1, g_norm2, w_in, b_gate, g_qa, g_ka, g_qb, g_kb, rpb, w_proj_a, w_proj_b, w_o, w_ffn_in, w_ffn_out, loss_target, m_w_ada, m_b_ada, m_g_norm1, m_g_norm2, m_w_in, m_b_gate, m_g_qa, m_g_ka, m_g_qb, m_g_kb, m_rpb, m_w_proj_a, m_w_proj_b, m_w_o, m_w_ffn_in, m_w_ffn_out, v_w_ada, v_b_ada, v_g_norm1, v_g_norm2, v_w_in, v_b_gate, v_g_qa, v_g_ka, v_g_qb, v_g_kb, v_rpb, v_w_proj_a, v_w_proj_b, v_w_o, v_w_ffn_in, v_w_ffn_out):
    t, d = x.shape[1], x.shape[2]
    d_ff = w_ffn_out.shape[1] * N_DEV
    me = 4 * lax.axis_index("x") + 2 * lax.axis_index("y") + lax.axis_index("c")
    xt, tgt = x.reshape(t, d), loss_target.reshape(t, d)
    ones = _head_ones()

    shards = [s.astype(WIRE_DTYPE) for s in (w_in[0].T, w_ffn_in[0].T, w_proj_a[0].T, w_proj_b[0].T, w_o[0], w_ffn_out[0])]
    lands = [lax.dynamic_update_slice(lax.empty((N_DEV * s.shape[0], s.shape[1]), s.dtype), s, (me * s.shape[0], 0))
             for s in shards]

    c_all = _all_gather([jnp.pad(c, ((0, 7), (0, 0)))], name="gather_c")[0][::8]
    c_all = jnp.pad(c_all, ((0, LANES - N_DEV), (0, 0)))

    def mod_body(c_ref, w_ref, b_ref, o_ref, act_ref):
        act = _silu(c_ref[...])
        act_ref[...] = act
        hi, mid, lo = _split3(act)
        w = w_ref[...]
        whi, wmid, wlo = _split3(w)
        acc = _dot(hi, whi) + (_dot(hi, wmid) + _dot(mid, whi)) + (_dot(hi, wlo) + _dot(mid, wmid) + _dot(lo, whi))
        o_ref[...] = acc + b_ref[...]

    ncol = w_ada.shape[2]
    b_ada_mine = lax.dynamic_slice(b_ada, (0, me * ncol), (1, ncol))
    mod_part, c_act = pl.pallas_call(
        mod_body,
        name="ada_mod",
        out_shape=[jax.ShapeDtypeStruct((LANES, ncol), F32), jax.ShapeDtypeStruct((LANES, d), F32)],
        compiler_params=pltpu.CompilerParams(vmem_limit_bytes=_vmem(6 * d * ncol * 4)),
    )(c_all, w_ada[0], b_ada_mine)
    mod_all = _all_gather([mod_part[:N_DEV]], name="gather_mod")[0].reshape(N_DEV, N_DEV, ncol)
    mod = lax.dynamic_index_in_dim(mod_all, me, axis=1, keepdims=False).reshape(6, d)
    sh1, sc1, gt1, sh2, sc2, gt2 = [mod[i:i + 1] for i in range(6)]

    def norm_fwd(rows, vecs):
        (xv,), (g, sc, sh) = rows, vecs
        return [xv * _rms(xv) * g * (1.0 + sc) + sh], []

    w_in_handle, w_token = _split_start(shards[:1], lands[:1], gather="chip", after=mod, name="gather_w_in_start")
    (h,) = _rowmap(norm_fwd, [xt], [g_norm1 + w_token, sc1, sh1], [(d, MXU_DTYPE)], [], name="norm1")
    n_a, n_b = 3 * WA, 3 * WB
    (w_in_t,) = _split_wait(w_in_handle, h, gather="chip", name="gather_w_in_wait")[1]
    w_in_t = _forward_wait(_forward_start(w_in_t, name="gather_w_in_forward_start"), name="gather_w_in_forward_wait")
    w_handles, w_token = _split_start(shards[1:], lands[1:], gather=True, after=w_in_t, name="gather_weights_start")
    w_in_a, w_in_b, w_in_g = w_in_t[:n_a], w_in_t[n_a:n_a + n_b], w_in_t[n_a + n_b:]

    rot_c, rot_lo, rot_hi = _rot_tables(t)
    tile_g = lambda g, heads: jnp.tile(g, (1, heads))

    def qk_fwd(width, rotate):
        def fn(xv, rows, vecs):
            gq, gk, on = vecs
            qkv = []
            for i, g in enumerate((gq, gk)):
                xi = xv[:, i * width:(i + 1) * width]
                r = lax.rsqrt(_headsum(xi * xi, on) * (1.0 / HEAD_DIM) + EPS)
                yi = xi * r * g
                if rotate:
                    yi = _rot(yi, rows[0], rows[1], rows[2])
                qkv.append(yi)
            qkv.append(xv[:, 2 * width:])
            if not rotate:
                return [xv] + qkv, []
            groups = [jnp.concatenate([a[:, g * WB_OUT:(g + 1) * WB_OUT] for a in qkv], axis=1)
                      for g in range(len(DIL_CONFIGS))]
            return [xv] + groups, []
        return fn

    qkv_a, qa, ka, va = _mm_parts_rows(
        [(h, w_in_a, "nt")], qk_fwd(WA, False), [], [tile_g(g_qa, NA_HEADS) + w_token, tile_g(g_ka, NA_HEADS), ones],
        [(3 * WA, ACT_DTYPE)] + [(WA, MXU_DTYPE)] * 3, [], name="proj_a_qknorm")
    qkv_b, *qkv_groups = _mm_parts_rows(
        [(h, w_in_b, "nt")], qk_fwd(WB, True), [rot_c, rot_lo, rot_hi],
        [tile_g(g_qb, DIL_HEADS), tile_g(g_kb, DIL_HEADS), ones],
        [(3 * WB, ACT_DTYPE)] + [(3 * WB_OUT, MXU_DTYPE)] * len(DIL_CONFIGS), [], name="proj_b_qknorm")

    bias_tab = _na_bias_table(rpb[0])
    o_a, lse_a = _attn_fwd(qa, ka, va, kind="na", bias=bias_tab, name="na_fwd")

    grp = []
    for g, (_, dil) in enumerate(DIL_CONFIGS):
        qkv_c = _to_class(qkv_groups[g], dil)
        og, lg = _attn_fwd((qkv_c, 0), (qkv_c, 1), (qkv_c, 2), kind="dil", seg=t // dil, name=f"dil_fwd{g}")
        grp.append(dict(qkv=qkv_c, o=_from_class(og, dil), lse=_from_class(lg, dil), lse_c=lg, dil=dil))

    def merge_fwd(rows, _):
        o0, o1, o2, l0, l1, l2 = rows
        mx = jnp.maximum(jnp.maximum(l0, l1), l2)
        e0, e1, e2 = jnp.exp(l0 - mx), jnp.exp(l1 - mx), jnp.exp(l2 - mx)
        s = e0 + e1 + e2
        return [(e0 / s) * o0 + (e1 / s) * o1 + (e2 / s) * o2], []

    (o_b,) = _rowmap(merge_fwd, [gr["o"] for gr in grp] + [gr["lse"] for gr in grp], [], [(WB_OUT, F32)], [], name="dil_merge")

    w_pa_t, w_pb_t, w_o_f = _split_wait(w_handles[1:4], o_b, gather=True, name="gather_w_out_wait")[1]
    def gate_fwd(prods, _, vecs):
        gv, pav, pbv = prods
        sg = jax.nn.sigmoid(gv + vecs[0])
        return [gv, pav, pbv, sg[:, :d] * pav + sg[:, d:] * pbv], []

    gates, pa, pb, merged = _mm_parts_rows(
        [(h, w_in_g, "nt"), (o_a, w_pa_t, "nt"), (o_b, w_pb_t, "nt")], gate_fwd, [], [b_gate],
        [(2 * d, ACT_DTYPE), (d, ACT_DTYPE), (d, ACT_DTYPE), (d, MXU_DTYPE)], [], separate=True, name="proj_gates_out_merge")
    def resid_norm(av, rows, vecs):
        (xv,), (gt, g, sc, sh) = rows, vecs
        x1v = xv + gt * av
        return [av, x1v, x1v * _rms(x1v) * g * (1.0 + sc) + sh], []

    att, x1, h2 = _mm_parts_rows([(merged, w_o_f)], resid_norm, [xt], [gt1, g_norm2, sc2, sh2],
                           [(d, F32), (d, F32), (d, MXU_DTYPE)], [], name="proj_o_resid_norm2")

    w_ffn_in_t, w_ffn_out_f = _split_wait([w_handles[0], w_handles[4]], h2, gather=True, name="gather_w_ffn_wait")[1]
    w_ffn_a, w_ffn_up = w_ffn_in_t[:d_ff], w_ffn_in_t[d_ff:]

    def swiglu_fwd(prods, _):
        a, up = prods
        return [a, up, _silu(a) * up]

    ua, uu, f = _mm_ew(h2, [w_ffn_a, w_ffn_up], swiglu_fwd, [], [ACT_DTYPE, ACT_DTYPE, MXU_DTYPE], name="ffn_in_swiglu")

    def loss_fn(yv, rows, vecs):
        (x1v, tv), gt = rows, vecs[0]
        err = x1v + gt * yv - tv
        dout = err * (1.0 / d)
        return [dout, dout * gt], [_colsum(err * err), _colsum(dout * yv)]

    dout, dy2, err2, dgt2 = _mm_parts_rows([(f, w_ffn_out_f)], loss_fn, [x1, tgt], [gt2], [(d, F32), (d, MXU_DTYPE)],
                                           [d, d], name="ffn_out_loss")

    dw_ffn_out = _mm(f, dy2, ta=True, out_dtype=WIRE_DTYPE, name="wgrad_ffn_out")
    def swiglu_bwd(prods, rows):
        (dfv,), (a, up) = prods, rows
        sg = jax.nn.sigmoid(a)
        return [dfv * up * (sg * (1.0 + a * (1.0 - sg))), dfv * (a * sg)]

    da, dup = _mm_ew(dy2, [w_ffn_out_f], swiglu_bwd, [ua, uu], [MXU_DTYPE, MXU_DTYPE], name="dgrad_ffn_out_swiglu_bwd")
    dw_ffn_in_t = _mm(da, h2, ta=True, into=(lax.empty((2 * d_ff, d), WIRE_DTYPE), 0), name="wgrad_ffn_in_a")
    dw_ffn_in_t = _mm(dup, h2, ta=True, into=(dw_ffn_in_t, d_ff), name="wgrad_ffn_in_up")
    land7 = lambda a: lax.empty((N_DEV - 1, a.shape[0] // N_DEV, a.shape[1]), a.dtype)
    own_block = lambda a: lax.dynamic_slice(a, (me * (a.shape[0] // N_DEV), 0), (a.shape[0] // N_DEV, a.shape[1]))
    g_ffn = [dw_ffn_in_t, dw_ffn_out]
    h_ffn, tok_ffn = _split_start(g_ffn, [land7(a) for a in g_ffn], gather=False, name="exchange_ffn_start")
    def norm_bwd(dh, xv, g, sc):
        r = _rms(xv)
        xh = xv * r
        dxh = dh * g * (1.0 + sc)
        dxv = r * (dxh - xh * jnp.mean(dxh * xh, axis=-1, keepdims=True))
        return dxv, [_colsum(dh), _colsum(dh * xh * g), _colsum(dh * xh * (1.0 + sc))]

    def norm2_bwd(dhv, rows, vecs):
        (x1v, dov, av), (g, sc, gt) = rows, vecs
        dxv, sums = norm_bwd(dhv, x1v, g, sc)
        dx1v = dov + dxv
        return [dx1v, dx1v * gt], sums + [_colsum(dx1v * av)]

    dx1, datt, dsh2, dsc2, dg2, dgt1 = _mm_parts_rows(
        [(da, w_ffn_a), (dup, w_ffn_up)], norm2_bwd, [x1, dout, att], [g_norm2 + tok_ffn, sc2, gt1],
        [(d, F32), (d, MXU_DTYPE)], [d] * 4, name="dgrad_ffn_in_norm2_bwd")
    dw_o = _mm(merged, datt, ta=True, out_dtype=WIRE_DTYPE, name="wgrad_o")
    def gate_bwd(dm, rows, vecs):
        gv, pav, pbv = rows
        sg = jax.nn.sigmoid(gv + vecs[0])
        ga, gb = sg[:, :d], sg[:, d:]
        dgp = jnp.concatenate([dm * pav * ga * (1.0 - ga), dm * pbv * gb * (1.0 - gb)], axis=1)
        return [dm * ga, dm * gb, dgp], [_colsum(dgp)]

    dpa, dpb, dgates, db_gate = _mm_parts_rows(
        [(datt, w_o_f.T)], gate_bwd, [gates, pa, pb], [b_gate],
        [(d, MXU_DTYPE), (d, MXU_DTYPE), (2 * d, MXU_DTYPE)], [2 * d], name="dgrad_o_gate_bwd")
    dw_pa_t = _mm(dpa, o_a, ta=True, out_dtype=WIRE_DTYPE, name="wgrad_proj_a")
    dw_pb_t = _mm(dpb, o_b, ta=True, out_dtype=WIRE_DTYPE, name="wgrad_proj_b")
    g_out = [dw_pa_t, dw_pb_t, dw_o]
    h_out, tok_out = _split_start(g_out, [land7(a) for a in g_out], gather=False, name="exchange_out_start")
    def delta_a(doa, rows, vecs):
        return [doa, _headsum(doa * rows[0], vecs[0])], []

    do_a, dterm_a = _mm_parts_rows([(dpa, w_pa_t)], delta_a, [o_a], [ones + tok_out.astype(ones.dtype)],
                                   [(WA, F32), (WA, F32)], [], name="dgrad_proj_a_delta")
    dqa, dka, dva, dbias = _attn_bwd(qa, ka, va, do_a, dterm_a, lse_a, kind="na", bias=bias_tab, name="na_bwd")
    g_rpb = _na_bias_grad(dbias)

    def merge_bwd(dob, rows, vecs):
        o0, o1, o2, l0, l1, l2 = rows
        on = vecs[0]
        mx = jnp.maximum(jnp.maximum(l0, l1), l2)
        e0, e1, e2 = jnp.exp(l0 - mx), jnp.exp(l1 - mx), jnp.exp(l2 - mx)
        s = e0 + e1 + e2
        ws = [e0 / s, e1 / s, e2 / s]
        dws = [_headsum(dob * o, on) for o in (o0, o1, o2)]
        mean = ws[0] * dws[0] + ws[1] * dws[1] + ws[2] * dws[2]
        return [jnp.concatenate([w * dob, w * mean], axis=1) for w in ws], []

    mb = _mm_parts_rows([(dpb, w_pb_t)], merge_bwd, [gr["o"] for gr in grp] + [gr["lse"] for gr in grp], [ones],
                        [(2 * WB_OUT, F32)] * len(grp), [], name="dgrad_proj_b_merge_bwd")
    dqb, dkb, dvb = [], [], []
    for g, gr in enumerate(grp):
        dil, qkv_c = gr["dil"], gr["qkv"]
        dd_c = _to_class(mb[g], dil)
        dq, dk, dv = _attn_bwd((qkv_c, 0), (qkv_c, 1), (qkv_c, 2), (dd_c, 0), (dd_c, 1), (gr["lse_c"], 0),
                               kind="dil", seg=t // dil, name=f"dil_bwd{g}")
        dqb.append(_from_class(dq, dil))
        dkb.append(dk[0] if dil == 1 else _from_class(dk[0][0][0], dil))
        dvb.append(dv[0] if dil == 1 else _from_class(dv[0][0][0], dil))

    def qk_bwd(width, rotate, nparts):
        def fn(rows, vecs):
            gq, gk, on = vecs
            xv = rows[0]
            pos = 1
            if rotate:
                rc, rlo, rhi = rows[1:4]
                pos = 4
            cat = lambda parts: parts[0] if len(parts) == 1 else jnp.concatenate(parts, axis=1)
            ends = np.cumsum((pos,) + nparts)
            dq, dk, dv = [cat(rows[ends[i]:ends[i + 1]]) for i in range(3)]
            outs, sums = [], []
            for i, (dy, g) in enumerate(((dq, gq), (dk, gk))):
                if rotate:
                    dy = _rot(dy, rc, -rlo, -rhi)
                xi = xv[:, i * width:(i + 1) * width]
                r = lax.rsqrt(_headsum(xi * xi, on) * (1.0 / HEAD_DIM) + EPS)
                xh = xi * r
                dxh = dy * g
                outs.append(r * (dxh - xh * (_headsum(dxh * xh, on) * (1.0 / HEAD_DIM))))
                sums.append(_colsum(dy * xh))
            return [jnp.concatenate(outs + [dv], axis=1)], sums
        return fn

    dqkv_a, dg_qa, dg_ka = _rowmap(qk_bwd(WA, False, (1, len(dka), len(dva))), [qkv_a, dqa] + dka + dva,
                                   [tile_g(g_qa, NA_HEADS), tile_g(g_ka, NA_HEADS), ones],
                                   [(3 * WA, MXU_DTYPE)], [WA, WA], name="qknorm_a_bwd")
    dqkv_b, dg_qb, dg_kb = _rowmap(qk_bwd(WB, True, (3, 3, 3)), [qkv_b, rot_c, rot_lo, rot_hi] + dqb + dkb + dvb,
                                   [tile_g(g_qb, DIL_HEADS), tile_g(g_kb, DIL_HEADS), ones],
                                   [(3 * WB, MXU_DTYPE)], [WB, WB], name="qknorm_b_bwd")

    dw_in_t = jnp.concatenate([
        _mm(dqkv_a, h, ta=True, out_dtype=WIRE_DTYPE, name="wgrad_in_a"),
        _mm(dqkv_b, h, ta=True, out_dtype=WIRE_DTYPE, name="wgrad_in_b"),
        _mm(dgates, h, ta=True, out_dtype=WIRE_DTYPE, name="wgrad_in_gates")], axis=0)
    h_in, tok_in = _split_start([dw_in_t], [land7(dw_in_t)], gather=False, name="exchange_in_start")
    def norm1_bwd(dhv, rows, vecs):
        xv, dx1v = rows
        dxv, sums = norm_bwd(dhv, xv, vecs[0], vecs[1])
        return [dx1v + dxv], sums

    grad_x, dsh1, dsc1, dg1 = _mm_parts_rows(
        [(dqkv_a, w_in_a), (dqkv_b, w_in_b), (dgates, w_in_g)], norm1_bwd, [xt, dx1], [g_norm1 + tok_in, sc1],
        [(d, F32)], [d] * 3, name="dgrad_in_norm1_bwd")

    heads_sum = lambda a, heads: a.reshape(heads, HEAD_DIM).sum(axis=0)
    dmod = jnp.concatenate([dsh1, dsc1, dgt1, dsh2, dsc2, dgt2], axis=1)
    local_small = _pack_small(dict(
        b_ada=dmod, g_norm1=dg1, g_norm2=dg2, b_gate=db_gate, g_qa=heads_sum(dg_qa, NA_HEADS),
        g_ka=heads_sum(dg_ka, NA_HEADS), g_qb=heads_sum(dg_qb, DIL_HEADS), g_kb=heads_sum(dg_kb, DIL_HEADS),
        rpb=g_rpb, loss=(0.5 / d) * jnp.sum(err2)))
    srows = local_small.shape[0]
    small_all = _all_gather([local_small], name="gather_small")[0].reshape(N_DEV, srows, LANES)
    small_sum = _sum8(small_all[0], small_all[1:], name="sum_small")
    small_shapes = dict(b_ada=b_ada.shape, g_norm1=g_norm1.shape, g_norm2=g_norm2.shape, b_gate=b_gate.shape,
                        g_qa=g_qa.shape, g_ka=g_ka.shape, g_qb=g_qb.shape, g_kb=g_kb.shape, rpb=rpb.shape, loss=())
    small_w = dict(b_ada=b_ada, g_norm1=g_norm1, g_norm2=g_norm2, b_gate=b_gate, g_qa=g_qa, g_ka=g_ka, g_qb=g_qb,
                   g_kb=g_kb, rpb=rpb, loss=jnp.zeros((), F32))
    small_m = dict(b_ada=m_b_ada, g_norm1=m_g_norm1, g_norm2=m_g_norm2, b_gate=m_b_gate, g_qa=m_g_qa, g_ka=m_g_ka,
                   g_qb=m_g_qb, g_kb=m_g_kb, rpb=m_rpb, loss=jnp.zeros((), F32))
    small_v = dict(b_ada=v_b_ada, g_norm1=v_g_norm1, g_norm2=v_g_norm2, b_gate=v_b_gate, g_qa=v_g_qa, g_ka=v_g_ka,
                   g_qb=v_g_qb, g_kb=v_g_kb, rpb=v_rpb, loss=jnp.zeros((), F32))
    s_delta, s_m, s_v = _adamw(_pack_small(small_w), small_sum, _pack_small(small_m), _pack_small(small_v), name="adamw_small")
    gs = _unpack_small(small_sum, small_shapes)
    ds_, ms_, vs_ = [_unpack_small(a, small_shapes) for a in (s_delta, s_m, s_v)]

    dmod_all = small_all[:, :6 * d // LANES].reshape(N_DEV, 6 * d)
    dmod_mine = jnp.pad(lax.dynamic_slice(dmod_all, (0, me * ncol), (N_DEV, ncol)), ((0, LANES - N_DEV), (0, 0)))

    def wada_body(c_ref, dm_ref, o_ref):
        chi, cmid, clo = _split3(c_ref[...])
        dhi, dmid, dlo = _split3(dm_ref[...])
        o_ref[...] = (_dot_tn(chi, dhi) + (_dot_tn(chi, dmid) + _dot_tn(cmid, dhi))
                      + (_dot_tn(chi, dlo) + _dot_tn(cmid, dmid) + _dot_tn(clo, dhi)))

    g_w_ada = pl.pallas_call(
        wada_body,
        name="wgrad_ada",
        out_shape=jax.ShapeDtypeStruct((d, ncol), F32),
        compiler_params=pltpu.CompilerParams(vmem_limit_bytes=_vmem(4 * d * ncol * 4)),
    )(c_act, dmod_mine)

    sent, recv = _split_wait(h_in + h_ffn + h_out, small_sum, gather=False, name="exchange_wait")
    names = ("w_in", "w_ffn_in", "w_ffn_out", "w_proj_a", "w_proj_b", "w_o")
    transposed = (True, True, False, True, True, False)
    big_g = {}
    for nme, own, r, tr in zip(names, sent, recv, transposed):
        s = _sum8(own_block(own), r, name=f"sum_{nme}")
        big_g[nme] = s.T if tr else s
    big_g["w_ada"] = g_w_ada
    big_w = dict(w_ada=w_ada, w_in=w_in, w_proj_a=w_proj_a, w_proj_b=w_proj_b, w_o=w_o, w_ffn_in=w_ffn_in, w_ffn_out=w_ffn_out)
    big_m = dict(w_ada=m_w_ada, w_in=m_w_in, w_proj_a=m_w_proj_a, w_proj_b=m_w_proj_b, w_o=m_w_o, w_ffn_in=m_w_ffn_in, w_ffn_out=m_w_ffn_out)
    big_v = dict(w_ada=v_w_ada, w_in=v_w_in, w_proj_a=v_w_proj_a, w_proj_b=v_w_proj_b, w_o=v_w_o, w_ffn_in=v_w_ffn_in, w_ffn_out=v_w_ffn_out)
    grads, deltas, new_m, new_v = {}, {}, {}, {}
    for nme in big_w:
        dl, m1, v1 = _adamw(big_w[nme][0], big_g[nme], big_m[nme][0], big_v[nme][0], name=f"adamw_{nme}")
        grads[nme], deltas[nme], new_m[nme], new_v[nme] = big_g[nme][None], dl[None], m1[None], v1[None]
    for nme in _SMALL[:-1]:
        grads[nme], deltas[nme], new_m[nme], new_v[nme] = gs[nme], ds_[nme], ms_[nme], vs_[nme]

    order = ("w_ada", "b_ada", "g_norm1", "g_norm2", "w_in", "b_gate", "g_qa", "g_ka", "g_qb", "g_kb", "rpb",
             "w_proj_a", "w_proj_b", "w_o", "w_ffn_in", "w_ffn_out")
    return (gs["loss"], grad_x[None], *[grads[n] for n in order], *[deltas[n] for n in order],
            *[new_m[n] for n in order], *[new_v[n] for n in order])
```
